```python
import math
import jax, jax.numpy as jnp
from jax import lax
import numpy as np

D_MODEL = 1024
BATCH = 16
SEQ = 2048
DEPTH = 4

GRID_W = 64
CTX_LEN = 256
MIXERS = ("s5", "conv")
N_MIX = len(MIXERS)
S5_GROUP = 16
S5_GROUPS = D_MODEL // S5_GROUP
S5_STATE = 64
N_DIR = 2
CONV_WIDTH = 31
CONV_HALF = CONV_WIDTH // 2
D_FF = 4 * D_MODEL
N_S5_LAYERS = sum(1 for _i in range(DEPTH) if MIXERS[_i % N_MIX] == "s5")
N_CONV_LAYERS = DEPTH - N_S5_LAYERS
DN_ALPHA = (2.0 * DEPTH) ** 0.25
DN_BETA = (8.0 * DEPTH) ** -0.25
LN_EPS = 1e-5
DT_MIN = 1e-3
DT_MAX = 1e-1
POS_TEMP = 10000.0
LAMBDA_RE_MAX = -1e-4

kernel_name = "hybrid_s5_conformer_dit_trunk"


def layer_norm(x, g, b):
    xf = x.astype(jnp.float32)
    mu = jnp.mean(xf, axis=-1, keepdims=True)
    var = jnp.mean(jnp.square(xf - mu), axis=-1, keepdims=True)
    y = (xf - mu) * lax.rsqrt(var + LN_EPS) * g.astype(jnp.float32) + b.astype(jnp.float32)
    return y.astype(x.dtype)


def modulate(x, shift, scale):
    return x * (1 + scale) + shift


def sincos_1d(pos, dim):
    quarter = dim // 2
    omega = POS_TEMP ** (-jnp.arange(quarter, dtype=jnp.float32) / quarter)
    ang = pos[:, None] * omega[None, :]
    return jnp.concatenate([jnp.sin(ang), jnp.cos(ang)], axis=-1)


def grid_pos_embed(rows, dim):
    row_idx = jnp.repeat(jnp.arange(rows), GRID_W).astype(jnp.float32)
    col_idx = jnp.tile(jnp.arange(GRID_W), rows).astype(jnp.float32)
    return jnp.concatenate([sincos_1d(row_idx, dim // 2), sincos_1d(col_idx, dim // 2)], axis=-1)


def s5_discretise(lam_re, lam_im, log_dt, b_re, b_im):
    lr = jnp.minimum(lam_re.astype(jnp.float32), LAMBDA_RE_MAX)
    li = lam_im.astype(jnp.float32)
    dt = jnp.exp(log_dt.astype(jnp.float32))[:, None]
    mag = jnp.exp(lr * dt)
    ab_re = mag * jnp.cos(li * dt)
    ab_im = mag * jnp.sin(li * dt)
    den = lr * lr + li * li
    nr = ab_re - 1.0
    ni = ab_im
    coef_re = (nr * lr + ni * li) / den
    coef_im = (ni * lr - nr * li) / den
    br = b_re.astype(jnp.float32)
    bi = b_im.astype(jnp.float32)
    bb_re = coef_re[..., None] * br - coef_im[..., None] * bi
    bb_im = coef_re[..., None] * bi + coef_im[..., None] * br
    return ab_re, ab_im, bb_re, bb_im


def _scan_op(e1, e2):
    ar1, ai1, br1, bi1 = e1
    ar2, ai2, br2, bi2 = e2
    return (ar2 * ar1 - ai2 * ai1,
            ar2 * ai1 + ai2 * ar1,
            ar2 * br1 - ai2 * bi1 + br2,
            ar2 * bi1 + ai2 * br1 + bi2)


def s5_scan(u, lam_re, lam_im, log_dt, b_re, b_im, h0s):
    length = u.shape[1]
    states = []
    for d in range(N_DIR):
        reverse = d == 1
        ab_re, ab_im, bb_re, bb_im = s5_discretise(lam_re[d], lam_im[d], log_dt[d], b_re[d], b_im[d])
        bu_re = jnp.einsum("blgc,gpc->blgp", u, bb_re)
        bu_im = jnp.einsum("blgc,gpc->blgp", u, bb_im)
        if h0s is not None:
            h0_re, h0_im = h0s[d]
            edge = -1 if reverse else 0
            bu_re = bu_re.at[:, edge].add(ab_re * h0_re - ab_im * h0_im)
            bu_im = bu_im.at[:, edge].add(ab_re * h0_im + ab_im * h0_re)
        a_re = jnp.broadcast_to(ab_re[None, None], (1, length) + ab_re.shape)
        a_im = jnp.broadcast_to(ab_im[None, None], (1, length) + ab_im.shape)
        _, _, h_re, h_im = lax.associative_scan(_scan_op, (a_re, a_im, bu_re, bu_im), reverse=reverse, axis=1)
        states.append((h_re, h_im))
    return states


def s5_readout(u, states, c_re, c_im, d_skip, w_glu, b_glu, out_dtype):
    y = d_skip.astype(jnp.float32).reshape(S5_GROUPS, S5_GROUP) * u
    for d, (h_re, h_im) in enumerate(states):
        y = y + jnp.einsum("blgp,gcp->blgc", h_re, c_re[d].astype(jnp.float32)) \
              - jnp.einsum("blgp,gcp->blgc", h_im, c_im[d].astype(jnp.float32))
    bsz, length = u.shape[0], u.shape[1]
    z = jax.nn.gelu(y.reshape(bsz, length, D_MODEL), approximate=False).astype(out_dtype)
    zz = z @ w_glu + b_glu
    return zz[..., :D_MODEL] * jax.nn.sigmoid(zz[..., D_MODEL:])


def to_groups(h):
    return h.astype(jnp.float32).reshape(h.shape[0], h.shape[1], S5_GROUPS, S5_GROUP)


def conv_module(h, w_pw1, b_pw1, w_dw, b_dw, ln_g, ln_b, w_pw2, b_pw2):
    a = h @ w_pw1 + b_pw1
    a = a[..., :D_MODEL] * jax.nn.sigmoid(a[..., D_MODEL:])
    a = lax.conv_general_dilated(a, w_dw[:, None, :].astype(a.dtype), window_strides=(1,),
                                 padding=[(CONV_HALF, CONV_HALF)],
                                 dimension_numbers=("NWC", "WIO", "NWC"),
                                 feature_group_count=D_MODEL) + b_dw
    a = jax.nn.silu(layer_norm(a, ln_g, ln_b))
    return a @ w_pw2 + b_pw2


def sq_relu_mlp(h, w1, w2):
    return jnp.square(jax.nn.relu(h @ w1)) @ w2


def _fwd_setup_inputs(seed: int = 0) -> dict:
    key = jax.random.key(seed)
    ks = jax.random.split(key, 32)
    f32 = jnp.float32
    D, G, P, CH = D_MODEL, S5_GROUPS, S5_STATE, S5_GROUP
    nrm = lambda k, shape, s: jax.random.normal(k, shape, f32) * s
    x = nrm(ks[0], (BATCH, SEQ, D), 1.0)
    c = nrm(ks[1], (BATCH, D), 1.0)
    ctx = nrm(ks[2], (BATCH, CTX_LEN, D), 1.0)
    c_ctx = nrm(ks[3], (D,), 1.0)
    w_ada = nrm(ks[4], (DEPTH, D, 6 * D), D ** -0.5)
    b_ada = nrm(ks[5], (DEPTH, 6 * D), 0.02)
    ln_gain = 1.0 + nrm(ks[6], (DEPTH, 2, D), 0.02)
    ln_bias = nrm(ks[7], (DEPTH, 2, D), 0.02)
    n_idx = jnp.arange(P, dtype=f32)
    s5_lam_re = -0.5 + nrm(ks[8], (N_S5_LAYERS, N_DIR, G, P), 0.01)
    s5_lam_im = math.pi * n_idx + nrm(ks[9], (N_S5_LAYERS, N_DIR, G, P), 0.01)
    s5_log_dt = jax.random.uniform(ks[10], (N_S5_LAYERS, N_DIR, G), f32, math.log(DT_MIN), math.log(DT_MAX))
    s5_b_re = nrm(ks[11], (N_S5_LAYERS, N_DIR, G, P, CH), (2.0 * CH) ** -0.5)
    s5_b_im = nrm(ks[12], (N_S5_LAYERS, N_DIR, G, P, CH), (2.0 * CH) ** -0.5)
    s5_c_re = nrm(ks[13], (N_S5_LAYERS, N_DIR, G, CH, P), P ** -0.5)
    s5_c_im = nrm(ks[14], (N_S5_LAYERS, N_DIR, G, CH, P), P ** -0.5)
    s5_d = 1.0 + nrm(ks[15], (N_S5_LAYERS, D), 0.1)
    glu_out = nrm(ks[16], (N_S5_LAYERS, D, D), DN_BETA * D ** -0.5)
    glu_gate = nrm(ks[17], (N_S5_LAYERS, D, D), D ** -0.5)
    s5_w_glu = jnp.concatenate([glu_out, glu_gate], axis=-1)
    s5_b_glu = nrm(ks[18], (N_S5_LAYERS, 2 * D), 0.02)
    cv_w_pw1 = nrm(ks[19], (N_CONV_LAYERS, D, 2 * D), D ** -0.5)
    cv_b_pw1 = nrm(ks[20], (N_CONV_LAYERS, 2 * D), 0.02)
    cv_w_dw = nrm(ks[21], (N_CONV_LAYERS, CONV_WIDTH, D), CONV_WIDTH ** -0.5)
    cv_b_dw = nrm(ks[22], (N_CONV_LAYERS, D), 0.02)
    cv_ln_g = 1.0 + nrm(ks[23], (N_CONV_LAYERS, D), 0.02)
    cv_ln_b = nrm(ks[24], (N_CONV_LAYERS, D), 0.02)
    cv_w_pw2 = nrm(ks[25], (N_CONV_LAYERS, D, D), DN_BETA * D ** -0.5)
    cv_b_pw2 = nrm(ks[26], (N_CONV_LAYERS, D), 0.02)
    mlp_w1 = nrm(ks[27], (DEPTH, D, D_FF), D ** -0.5)
    mlp_w2 = nrm(ks[28], (DEPTH, D_FF, D), DN_BETA * D_FF ** -0.5)
    return {"x": x, "c": c, "ctx": ctx, "c_ctx": c_ctx,
            "w_ada": w_ada, "b_ada": b_ada, "ln_gain": ln_gain, "ln_bias": ln_bias,
            "s5_lam_re": s5_lam_re, "s5_lam_im": s5_lam_im, "s5_log_dt": s5_log_dt,
            "s5_b_re": s5_b_re, "s5_b_im": s5_b_im, "s5_c_re": s5_c_re, "s5_c_im": s5_c_im,
            "s5_d": s5_d, "s5_w_glu": s5_w_glu, "s5_b_glu": s5_b_glu,
            "cv_w_pw1": cv_w_pw1, "cv_b_pw1": cv_b_pw1, "cv_w_dw": cv_w_dw, "cv_b_dw": cv_b_dw,
            "cv_ln_g": cv_ln_g, "cv_ln_b": cv_ln_b, "cv_w_pw2": cv_w_pw2, "cv_b_pw2": cv_b_pw2,
            "mlp_w1": mlp_w1, "mlp_w2": mlp_w2}


def _fwd_reference(x, c, ctx, c_ctx, w_ada, b_ada, ln_gain, ln_bias,
              s5_lam_re, s5_lam_im, s5_log_dt, s5_b_re, s5_b_im, s5_c_re, s5_c_im,
              s5_d, s5_w_glu, s5_b_glu,
              cv_w_pw1, cv_b_pw1, cv_w_dw, cv_b_dw, cv_ln_g, cv_ln_b, cv_w_pw2, cv_b_pw2,
              mlp_w1, mlp_w2):
    rows = x.shape[1] // GRID_W
    x = x + grid_pos_embed(rows, D_MODEL).astype(x.dtype)[None]
    cond = jax.nn.silu(c)
    cond_ctx = jax.nn.silu(c_ctx)
    kinds = [MIXERS[i % N_MIX] for i in range(DEPTH)]
    s5_j = 0
    cv_j = 0
    for i in range(DEPTH):
        kind = kinds[i]
        ctx_read_here = kind == "s5"
        ctx_needed_later = any(k == "s5" for k in kinds[i + 1:])
        use_ctx = ctx_read_here or ctx_needed_later
        mod = (cond @ w_ada[i] + b_ada[i])[:, None, :]
        sh1, sc1, g1, sh2, sc2, g2 = jnp.split(mod, 6, axis=-1)
        if use_ctx:
            mod_c = (cond_ctx @ w_ada[i] + b_ada[i])[None, None, :]
            csh1, csc1, cg1, csh2, csc2, cg2 = jnp.split(mod_c, 6, axis=-1)
            hc = modulate(ctx, csh1, csc1)
        h = modulate(x, sh1, sc1)
        if kind == "s5":
            j = s5_j
            s5_j += 1
            uc = to_groups(hc)
            states_c = s5_scan(uc, s5_lam_re[j], s5_lam_im[j], s5_log_dt[j], s5_b_re[j], s5_b_im[j], None)
            h0s = [(states_c[0][0][:, -1], states_c[0][1][:, -1]),
                   (states_c[1][0][:, 0], states_c[1][1][:, 0])]
            u = to_groups(h)
            states = s5_scan(u, s5_lam_re[j], s5_lam_im[j], s5_log_dt[j], s5_b_re[j], s5_b_im[j], h0s)
            mix = s5_readout(u, states, s5_c_re[j], s5_c_im[j], s5_d[j], s5_w_glu[j], s5_b_glu[j], x.dtype)
            if ctx_needed_later:
                mix_c = s5_readout(uc, states_c, s5_c_re[j], s5_c_im[j], s5_d[j], s5_w_glu[j], s5_b_glu[j], ctx.dtype)
        else:
            j = cv_j
            cv_j += 1
            cv_args = (cv_w_pw1[j], cv_b_pw1[j], cv_w_dw[j], cv_b_dw[j], cv_ln_g[j], cv_ln_b[j], cv_w_pw2[j], cv_b_pw2[j])
            mix = conv_module(h, *cv_args)
            if ctx_needed_later:
                mix_c = conv_module(hc, *cv_args)
        x = layer_norm(DN_ALPHA * x + g1 * mix, ln_gain[i, 0], ln_bias[i, 0])
        h = modulate(x, sh2, sc2)
        x = layer_norm(DN_ALPHA * x + g2 * sq_relu_mlp(h, mlp_w1[i], mlp_w2[i]), ln_gain[i, 1], ln_bias[i, 1])
        if ctx_needed_later:
            ctx = layer_norm(DN_ALPHA * ctx + cg1 * mix_c, ln_gain[i, 0], ln_bias[i, 0])
            hc2 = modulate(ctx, csh2, csc2)
            ctx = layer_norm(DN_ALPHA * ctx + cg2 * sq_relu_mlp(hc2, mlp_w1[i], mlp_w2[i]), ln_gain[i, 1], ln_bias[i, 1])
    return x


import jax as _jax
import jax.numpy as _jnp

TWIN_FORMAT = 'train_step'
FWD_PARAMS = ['x', 'c', 'ctx', 'c_ctx', 'w_ada', 'b_ada', 'ln_gain', 'ln_bias', 's5_lam_re', 's5_lam_im', 's5_log_dt', 's5_b_re', 's5_b_im', 's5_c_re', 's5_c_im', 's5_d', 's5_w_glu', 's5_b_glu', 'cv_w_pw1', 'cv_b_pw1', 'cv_w_dw', 'cv_b_dw', 'cv_ln_g', 'cv_ln_b', 'cv_w_pw2', 'cv_b_pw2', 'mlp_w1', 'mlp_w2']
TWIN_WEIGHTS = ['c_ctx', 'w_ada', 'b_ada', 'ln_gain', 'ln_bias', 's5_lam_re', 's5_lam_im', 's5_log_dt', 's5_b_re', 's5_b_im', 's5_c_re', 's5_c_im', 's5_d', 's5_w_glu', 's5_b_glu', 'cv_w_pw1', 'cv_b_pw1', 'cv_w_dw', 'cv_b_dw', 'cv_ln_g', 'cv_ln_b', 'cv_w_pw2', 'cv_b_pw2', 'mlp_w1', 'mlp_w2']
TWIN_DIFF_INPUT = 'x'
TWIN_INPUTS = ['x', 'c', 'ctx', 'c_ctx', 'w_ada', 'b_ada', 'ln_gain', 'ln_bias', 's5_lam_re', 's5_lam_im', 's5_log_dt', 's5_b_re', 's5_b_im', 's5_c_re', 's5_c_im', 's5_d', 's5_w_glu', 's5_b_glu', 'cv_w_pw1', 'cv_b_pw1', 'cv_w_dw', 'cv_b_dw', 'cv_ln_g', 'cv_ln_b', 'cv_w_pw2', 'cv_b_pw2', 'mlp_w1', 'mlp_w2', 'loss_target', 'm_c_ctx', 'm_w_ada', 'm_b_ada', 'm_ln_gain', 'm_ln_bias', 'm_s5_lam_re', 'm_s5_lam_im', 'm_s5_log_dt', 'm_s5_b_re', 'm_s5_b_im', 'm_s5_c_re', 'm_s5_c_im', 'm_s5_d', 'm_s5_w_glu', 'm_s5_b_glu', 'm_cv_w_pw1', 'm_cv_b_pw1', 'm_cv_w_dw', 'm_cv_b_dw', 'm_cv_ln_g', 'm_cv_ln_b', 'm_cv_w_pw2', 'm_cv_b_pw2', 'm_mlp_w1', 'm_mlp_w2', 'v_c_ctx', 'v_w_ada', 'v_b_ada', 'v_ln_gain', 'v_ln_bias', 'v_s5_lam_re', 'v_s5_lam_im', 'v_s5_log_dt', 'v_s5_b_re', 'v_s5_b_im', 'v_s5_c_re', 'v_s5_c_im', 'v_s5_d', 'v_s5_w_glu', 'v_s5_b_glu', 'v_cv_w_pw1', 'v_cv_b_pw1', 'v_cv_w_dw', 'v_cv_b_dw', 'v_cv_ln_g', 'v_cv_ln_b', 'v_cv_w_pw2', 'v_cv_b_pw2', 'v_mlp_w1', 'v_mlp_w2']
TWIN_OUTPUTS = ['loss', 'grad_x', 'grad_c_ctx', 'grad_w_ada', 'grad_b_ada', 'grad_ln_gain', 'grad_ln_bias', 'grad_s5_lam_re', 'grad_s5_lam_im', 'grad_s5_log_dt', 'grad_s5_b_re', 'grad_s5_b_im', 'grad_s5_c_re', 'grad_s5_c_im', 'grad_s5_d', 'grad_s5_w_glu', 'grad_s5_b_glu', 'grad_cv_w_pw1', 'grad_cv_b_pw1', 'grad_cv_w_dw', 'grad_cv_b_dw', 'grad_cv_ln_g', 'grad_cv_ln_b', 'grad_cv_w_pw2', 'grad_cv_b_pw2', 'grad_mlp_w1', 'grad_mlp_w2', 'delta_c_ctx', 'delta_w_ada', 'delta_b_ada', 'delta_ln_gain', 'delta_ln_bias', 'delta_s5_lam_re', 'delta_s5_lam_im', 'delta_s5_log_dt', 'delta_s5_b_re', 'delta_s5_b_im', 'delta_s5_c_re', 'delta_s5_c_im', 'delta_s5_d', 'delta_s5_w_glu', 'delta_s5_b_glu', 'delta_cv_w_pw1', 'delta_cv_b_pw1', 'delta_cv_w_dw', 'delta_cv_b_dw', 'delta_cv_ln_g', 'delta_cv_ln_b', 'delta_cv_w_pw2', 'delta_cv_b_pw2', 'delta_mlp_w1', 'delta_mlp_w2', 'new_m_c_ctx', 'new_m_w_ada', 'new_m_b_ada', 'new_m_ln_gain', 'new_m_ln_bias', 'new_m_s5_lam_re', 'new_m_s5_lam_im', 'new_m_s5_log_dt', 'new_m_s5_b_re', 'new_m_s5_b_im', 'new_m_s5_c_re', 'new_m_s5_c_im', 'new_m_s5_d', 'new_m_s5_w_glu', 'new_m_s5_b_glu', 'new_m_cv_w_pw1', 'new_m_cv_b_pw1', 'new_m_cv_w_dw', 'new_m_cv_b_dw', 'new_m_cv_ln_g', 'new_m_cv_ln_b', 'new_m_cv_w_pw2', 'new_m_cv_b_pw2', 'new_m_mlp_w1', 'new_m_mlp_w2', 'new_v_c_ctx', 'new_v_w_ada', 'new_v_b_ada', 'new_v_ln_gain', 'new_v_ln_bias', 'new_v_s5_lam_re', 'new_v_s5_lam_im', 'new_v_s5_log_dt', 'new_v_s5_b_re', 'new_v_s5_b_im', 'new_v_s5_c_re', 'new_v_s5_c_im', 'new_v_s5_d', 'new_v_s5_w_glu', 'new_v_s5_b_glu', 'new_v_cv_w_pw1', 'new_v_cv_b_pw1', 'new_v_cv_w_dw', 'new_v_cv_b_dw', 'new_v_cv_ln_g', 'new_v_cv_ln_b', 'new_v_cv_w_pw2', 'new_v_cv_b_pw2', 'new_v_mlp_w1', 'new_v_mlp_w2']
TWIN_LEAF_KINDS = {'loss': 'loss', 'grad_x': 'grad_x', 'grad_c_ctx': 'grad_w', 'grad_w_ada': 'grad_w', 'grad_b_ada': 'grad_w', 'grad_ln_gain': 'grad_w', 'grad_ln_bias': 'grad_w', 'grad_s5_lam_re': 'grad_w', 'grad_s5_lam_im': 'grad_w', 'grad_s5_log_dt': 'grad_w', 'grad_s5_b_re': 'grad_w', 'grad_s5_b_im': 'grad_w', 'grad_s5_c_re': 'grad_w', 'grad_s5_c_im': 'grad_w', 'grad_s5_d': 'grad_w', 'grad_s5_w_glu': 'grad_w', 'grad_s5_b_glu': 'grad_w', 'grad_cv_w_pw1': 'grad_w', 'grad_cv_b_pw1': 'grad_w', 'grad_cv_w_dw': 'grad_w', 'grad_cv_b_dw': 'grad_w', 'grad_cv_ln_g': 'grad_w', 'grad_cv_ln_b': 'grad_w', 'grad_cv_w_pw2': 'grad_w', 'grad_cv_b_pw2': 'grad_w', 'grad_mlp_w1': 'grad_w', 'grad_mlp_w2': 'grad_w', 'delta_c_ctx': 'delta_w', 'delta_w_ada': 'delta_w', 'delta_b_ada': 'delta_w', 'delta_ln_gain': 'delta_w', 'delta_ln_bias': 'delta_w', 'delta_s5_lam_re': 'delta_w', 'delta_s5_lam_im': 'delta_w', 'delta_s5_log_dt': 'delta_w', 'delta_s5_b_re': 'delta_w', 'delta_s5_b_im': 'delta_w', 'delta_s5_c_re': 'delta_w', 'delta_s5_c_im': 'delta_w', 'delta_s5_d': 'delta_w', 'delta_s5_w_glu': 'delta_w', 'delta_s5_b_glu': 'delta_w', 'delta_cv_w_pw1': 'delta_w', 'delta_cv_b_pw1': 'delta_w', 'delta_cv_w_dw': 'delta_w', 'delta_cv_b_dw': 'delta_w', 'delta_cv_ln_g': 'delta_w', 'delta_cv_ln_b': 'delta_w', 'delta_cv_w_pw2': 'delta_w', 'delta_cv_b_pw2': 'delta_w', 'delta_mlp_w1': 'delta_w', 'delta_mlp_w2': 'delta_w', 'new_m_c_ctx': 'new_m', 'new_m_w_ada': 'new_m', 'new_m_b_ada': 'new_m', 'new_m_ln_gain': 'new_m', 'new_m_ln_bias': 'new_m', 'new_m_s5_lam_re': 'new_m', 'new_m_s5_lam_im': 'new_m', 'new_m_s5_log_dt': 'new_m', 'new_m_s5_b_re': 'new_m', 'new_m_s5_b_im': 'new_m', 'new_m_s5_c_re': 'new_m', 'new_m_s5_c_im': 'new_m', 'new_m_s5_d': 'new_m', 'new_m_s5_w_glu': 'new_m', 'new_m_s5_b_glu': 'new_m', 'new_m_cv_w_pw1': 'new_m', 'new_m_cv_b_pw1': 'new_m', 'new_m_cv_w_dw': 'new_m', 'new_m_cv_b_dw': 'new_m', 'new_m_cv_ln_g': 'new_m', 'new_m_cv_ln_b': 'new_m', 'new_m_cv_w_pw2': 'new_m', 'new_m_cv_b_pw2': 'new_m', 'new_m_mlp_w1': 'new_m', 'new_m_mlp_w2': 'new_m', 'new_v_c_ctx': 'new_v', 'new_v_w_ada': 'new_v', 'new_v_b_ada': 'new_v', 'new_v_ln_gain': 'new_v', 'new_v_ln_bias': 'new_v', 'new_v_s5_lam_re': 'new_v', 'new_v_s5_lam_im': 'new_v', 'new_v_s5_log_dt': 'new_v', 'new_v_s5_b_re': 'new_v', 'new_v_s5_b_im': 'new_v', 'new_v_s5_c_re': 'new_v', 'new_v_s5_c_im': 'new_v', 'new_v_s5_d': 'new_v', 'new_v_s5_w_glu': 'new_v', 'new_v_s5_b_glu': 'new_v', 'new_v_cv_w_pw1': 'new_v', 'new_v_cv_b_pw1': 'new_v', 'new_v_cv_w_dw': 'new_v', 'new_v_cv_b_dw': 'new_v', 'new_v_cv_ln_g': 'new_v', 'new_v_cv_ln_b': 'new_v', 'new_v_cv_w_pw2': 'new_v', 'new_v_cv_b_pw2': 'new_v', 'new_v_mlp_w1': 'new_v', 'new_v_mlp_w2': 'new_v'}


def _forward(args):
    return _fwd_reference(*[args[k] for k in FWD_PARAMS])


def _output_shape():
    out = _jax.eval_shape(lambda: _forward(_fwd_setup_inputs(0)))
    return out.shape, out.dtype

N_MICROBATCH = 1
ADAM_LR = 0.001
ADAM_B1 = 0.9
ADAM_B2 = 0.999
ADAM_EPS = 1e-08
ADAM_WD = 0.01
ADAM_STEP = 10
PER_EXAMPLE_BATCH_AXIS = {'x': 0, 'c': 0, 'ctx': 0, 'loss_target': 0}
SHARED_INPUTS = []
_WEIGHT_DTYPES = {'c_ctx': _jnp.float32, 'w_ada': _jnp.float32, 'b_ada': _jnp.float32, 'ln_gain': _jnp.float32, 'ln_bias': _jnp.float32, 's5_lam_re': _jnp.float32, 's5_lam_im': _jnp.float32, 's5_log_dt': _jnp.float32, 's5_b_re': _jnp.float32, 's5_b_im': _jnp.float32, 's5_c_re': _jnp.float32, 's5_c_im': _jnp.float32, 's5_d': _jnp.float32, 's5_w_glu': _jnp.float32, 's5_b_glu': _jnp.float32, 'cv_w_pw1': _jnp.float32, 'cv_b_pw1': _jnp.float32, 'cv_w_dw': _jnp.float32, 'cv_b_dw': _jnp.float32, 'cv_ln_g': _jnp.float32, 'cv_ln_b': _jnp.float32, 'cv_w_pw2': _jnp.float32, 'cv_b_pw2': _jnp.float32, 'mlp_w1': _jnp.float32, 'mlp_w2': _jnp.float32}
MOMENT_SCALE = {'c_ctx': 1.533168e-03, 'w_ada': 4.151681e-02, 'b_ada': 7.819650e-02, 'ln_gain': 1.163407e+01, 'ln_bias': 3.948959e+00, 's5_lam_re': 3.894659e-03, 's5_lam_im': 3.413826e-03, 's5_log_dt': 5.468732e-01, 's5_b_re': 2.016794e-03, 's5_b_im': 1.807883e-03, 's5_c_re': 2.890723e-03, 's5_c_im': 3.225174e-03, 's5_d': 2.536626e-02, 's5_w_glu': 3.577865e-02, 's5_b_glu': 4.323187e-02, 'cv_w_pw1': 1.605579e-02, 'cv_b_pw1': 1.622929e-02, 'cv_w_dw': 2.097449e-02, 'cv_b_dw': 3.519468e-02, 'cv_ln_g': 2.835246e-02, 'cv_ln_b': 2.669517e-02, 'cv_w_pw2': 5.204380e-02, 'cv_b_pw2': 1.202537e-01, 'mlp_w1': 3.855953e-02, 'mlp_w2': 1.866360e-01}


def _to_microbatches(a, axis):
    t = _jnp.moveaxis(a, axis, 0)
    t = t.reshape((N_MICROBATCH, t.shape[0] // N_MICROBATCH) + t.shape[1:])
    return _jnp.moveaxis(t, 1, axis + 1)


def setup_inputs(seed: int = 0) -> dict:
    inp = _fwd_setup_inputs(seed)
    key = _jax.random.fold_in(_jax.random.key(seed), 7919)
    shape, _ = _output_shape()
    out = dict(inp)
    out["loss_target"] = _jax.random.normal(_jax.random.fold_in(key, 0), shape, _jnp.float32)
    for i, name in enumerate(TWIN_WEIGHTS):
        w = inp[name].astype(_jnp.float32)
        if MOMENT_SCALE is None:
            s = _jnp.sqrt(_jnp.mean(_jnp.square(w)) + 1e-30)
        else:
            s = MOMENT_SCALE[name]
        km, kv = _jax.random.split(_jax.random.fold_in(key, i + 1))
        out[name] = w
        out["m_" + name] = s * _jax.random.normal(km, w.shape, _jnp.float32)
        out["v_" + name] = (s * s) * _jax.random.uniform(kv, w.shape, _jnp.float32, 0.5, 1.5)
    if N_MICROBATCH > 1:
        for name, axis in PER_EXAMPLE_BATCH_AXIS.items():
            out[name] = _to_microbatches(out[name], axis)
    return {'x': out['x'], 'c': out['c'], 'ctx': out['ctx'], 'c_ctx': out['c_ctx'], 'w_ada': out['w_ada'], 'b_ada': out['b_ada'], 'ln_gain': out['ln_gain'], 'ln_bias': out['ln_bias'], 's5_lam_re': out['s5_lam_re'], 's5_lam_im': out['s5_lam_im'], 's5_log_dt': out['s5_log_dt'], 's5_b_re': out['s5_b_re'], 's5_b_im': out['s5_b_im'], 's5_c_re': out['s5_c_re'], 's5_c_im': out['s5_c_im'], 's5_d': out['s5_d'], 's5_w_glu': out['s5_w_glu'], 's5_b_glu': out['s5_b_glu'], 'cv_w_pw1': out['cv_w_pw1'], 'cv_b_pw1': out['cv_b_pw1'], 'cv_w_dw': out['cv_w_dw'], 'cv_b_dw': out['cv_b_dw'], 'cv_ln_g': out['cv_ln_g'], 'cv_ln_b': out['cv_ln_b'], 'cv_w_pw2': out['cv_w_pw2'], 'cv_b_pw2': out['cv_b_pw2'], 'mlp_w1': out['mlp_w1'], 'mlp_w2': out['mlp_w2'], 'loss_target': out['loss_target'], 'm_c_ctx': out['m_c_ctx'], 'm_w_ada': out['m_w_ada'], 'm_b_ada': out['m_b_ada'], 'm_ln_gain': out['m_ln_gain'], 'm_ln_bias': out['m_ln_bias'], 'm_s5_lam_re': out['m_s5_lam_re'], 'm_s5_lam_im': out['m_s5_lam_im'], 'm_s5_log_dt': out['m_s5_log_dt'], 'm_s5_b_re': out['m_s5_b_re'], 'm_s5_b_im': out['m_s5_b_im'], 'm_s5_c_re': out['m_s5_c_re'], 'm_s5_c_im': out['m_s5_c_im'], 'm_s5_d': out['m_s5_d'], 'm_s5_w_glu': out['m_s5_w_glu'], 'm_s5_b_glu': out['m_s5_b_glu'], 'm_cv_w_pw1': out['m_cv_w_pw1'], 'm_cv_b_pw1': out['m_cv_b_pw1'], 'm_cv_w_dw': out['m_cv_w_dw'], 'm_cv_b_dw': out['m_cv_b_dw'], 'm_cv_ln_g': out['m_cv_ln_g'], 'm_cv_ln_b': out['m_cv_ln_b'], 'm_cv_w_pw2': out['m_cv_w_pw2'], 'm_cv_b_pw2': out['m_cv_b_pw2'], 'm_mlp_w1': out['m_mlp_w1'], 'm_mlp_w2': out['m_mlp_w2'], 'v_c_ctx': out['v_c_ctx'], 'v_w_ada': out['v_w_ada'], 'v_b_ada': out['v_b_ada'], 'v_ln_gain': out['v_ln_gain'], 'v_ln_bias': out['v_ln_bias'], 'v_s5_lam_re': out['v_s5_lam_re'], 'v_s5_lam_im': out['v_s5_lam_im'], 'v_s5_log_dt': out['v_s5_log_dt'], 'v_s5_b_re': out['v_s5_b_re'], 'v_s5_b_im': out['v_s5_b_im'], 'v_s5_c_re': out['v_s5_c_re'], 'v_s5_c_im': out['v_s5_c_im'], 'v_s5_d': out['v_s5_d'], 'v_s5_w_glu': out['v_s5_w_glu'], 'v_s5_b_glu': out['v_s5_b_glu'], 'v_cv_w_pw1': out['v_cv_w_pw1'], 'v_cv_b_pw1': out['v_cv_b_pw1'], 'v_cv_w_dw': out['v_cv_w_dw'], 'v_cv_b_dw': out['v_cv_b_dw'], 'v_cv_ln_g': out['v_cv_ln_g'], 'v_cv_ln_b': out['v_cv_ln_b'], 'v_cv_w_pw2': out['v_cv_w_pw2'], 'v_cv_b_pw2': out['v_cv_b_pw2'], 'v_mlp_w1': out['v_mlp_w1'], 'v_mlp_w2': out['v_mlp_w2']}


def _loss(weights, diff, rest, loss_target):
    with _jax.named_scope("forward"):
        args = {**rest, TWIN_DIFF_INPUT: diff, **{k: w.astype(_WEIGHT_DTYPES[k]) for k, w in weights.items()}}
        y = _forward(args)
    with _jax.named_scope("loss_head"):
        err = _jnp.square(y.astype(_jnp.float32) - loss_target)
        return 0.5 * _jnp.sum(_jnp.mean(err, axis=-1)) if err.ndim else 0.5 * err


def _adamw(w, g, m, v):
    m = ADAM_B1 * m + (1.0 - ADAM_B1) * g
    v = ADAM_B2 * v + (1.0 - ADAM_B2) * _jnp.square(g)
    m_hat = m / (1.0 - ADAM_B1 ** ADAM_STEP)
    v_hat = v / (1.0 - ADAM_B2 ** ADAM_STEP)
    delta = -ADAM_LR * (m_hat / (_jnp.sqrt(v_hat) + ADAM_EPS) + ADAM_WD * w)
    return delta, m, v


def reference(x, c, ctx, c_ctx, w_ada, b_ada, ln_gain, ln_bias, s5_lam_re, s5_lam_im, s5_log_dt, s5_b_re, s5_b_im, s5_c_re, s5_c_im, s5_d, s5_w_glu, s5_b_glu, cv_w_pw1, cv_b_pw1, cv_w_dw, cv_b_dw, cv_ln_g, cv_ln_b, cv_w_pw2, cv_b_pw2, mlp_w1, mlp_w2, loss_target, m_c_ctx, m_w_ada, m_b_ada, m_ln_gain, m_ln_bias, m_s5_lam_re, m_s5_lam_im, m_s5_log_dt, m_s5_b_re, m_s5_b_im, m_s5_c_re, m_s5_c_im, m_s5_d, m_s5_w_glu, m_s5_b_glu, m_cv_w_pw1, m_cv_b_pw1, m_cv_w_dw, m_cv_b_dw, m_cv_ln_g, m_cv_ln_b, m_cv_w_pw2, m_cv_b_pw2, m_mlp_w1, m_mlp_w2, v_c_ctx, v_w_ada, v_b_ada, v_ln_gain, v_ln_bias, v_s5_lam_re, v_s5_lam_im, v_s5_log_dt, v_s5_b_re, v_s5_b_im, v_s5_c_re, v_s5_c_im, v_s5_d, v_s5_w_glu, v_s5_b_glu, v_cv_w_pw1, v_cv_b_pw1, v_cv_w_dw, v_cv_b_dw, v_cv_ln_g, v_cv_ln_b, v_cv_w_pw2, v_cv_b_pw2, v_mlp_w1, v_mlp_w2):
    given = dict(x=x, c=c, ctx=ctx, c_ctx=c_ctx, w_ada=w_ada, b_ada=b_ada, ln_gain=ln_gain, ln_bias=ln_bias, s5_lam_re=s5_lam_re, s5_lam_im=s5_lam_im, s5_log_dt=s5_log_dt, s5_b_re=s5_b_re, s5_b_im=s5_b_im, s5_c_re=s5_c_re, s5_c_im=s5_c_im, s5_d=s5_d, s5_w_glu=s5_w_glu, s5_b_glu=s5_b_glu, cv_w_pw1=cv_w_pw1, cv_b_pw1=cv_b_pw1, cv_w_dw=cv_w_dw, cv_b_dw=cv_b_dw, cv_ln_g=cv_ln_g, cv_ln_b=cv_ln_b, cv_w_pw2=cv_w_pw2, cv_b_pw2=cv_b_pw2, mlp_w1=mlp_w1, mlp_w2=mlp_w2, loss_target=loss_target, m_c_ctx=m_c_ctx, m_w_ada=m_w_ada, m_b_ada=m_b_ada, m_ln_gain=m_ln_gain, m_ln_bias=m_ln_bias, m_s5_lam_re=m_s5_lam_re, m_s5_lam_im=m_s5_lam_im, m_s5_log_dt=m_s5_log_dt, m_s5_b_re=m_s5_b_re, m_s5_b_im=m_s5_b_im, m_s5_c_re=m_s5_c_re, m_s5_c_im=m_s5_c_im, m_s5_d=m_s5_d, m_s5_w_glu=m_s5_w_glu, m_s5_b_glu=m_s5_b_glu, m_cv_w_pw1=m_cv_w_pw1, m_cv_b_pw1=m_cv_b_pw1, m_cv_w_dw=m_cv_w_dw, m_cv_b_dw=m_cv_b_dw, m_cv_ln_g=m_cv_ln_g, m_cv_ln_b=m_cv_ln_b, m_cv_w_pw2=m_cv_w_pw2, m_cv_b_pw2=m_cv_b_pw2, m_mlp_w1=m_mlp_w1, m_mlp_w2=m_mlp_w2, v_c_ctx=v_c_ctx, v_w_ada=v_w_ada, v_b_ada=v_b_ada, v_ln_gain=v_ln_gain, v_ln_bias=v_ln_bias, v_s5_lam_re=v_s5_lam_re, v_s5_lam_im=v_s5_lam_im, v_s5_log_dt=v_s5_log_dt, v_s5_b_re=v_s5_b_re, v_s5_b_im=v_s5_b_im, v_s5_c_re=v_s5_c_re, v_s5_c_im=v_s5_c_im, v_s5_d=v_s5_d, v_s5_w_glu=v_s5_w_glu, v_s5_b_glu=v_s5_b_glu, v_cv_w_pw1=v_cv_w_pw1, v_cv_b_pw1=v_cv_b_pw1, v_cv_w_dw=v_cv_w_dw, v_cv_b_dw=v_cv_b_dw, v_cv_ln_g=v_cv_ln_g, v_cv_ln_b=v_cv_ln_b, v_cv_w_pw2=v_cv_w_pw2, v_cv_b_pw2=v_cv_b_pw2, v_mlp_w1=v_mlp_w1, v_mlp_w2=v_mlp_w2)
    weights = {n: given[n] for n in TWIN_WEIGHTS}
    shared = {n: given[n] for n in SHARED_INPUTS}
    per_example = {n: given[n] for n in ['x', 'c', 'ctx']}
    grad_fn = _jax.value_and_grad(_loss, argnums=(0, 1))

    def one_microbatch(ex, loss_target):
        ex = dict(ex)
        diff = ex.pop(TWIN_DIFF_INPUT)
        return grad_fn(weights, diff, {**shared, **ex}, loss_target)

    if N_MICROBATCH == 1:
        loss, (grad_w, grad_x) = one_microbatch(per_example, given["loss_target"])
    else:
        def body(carry, xs):
            loss_sum, grad_sum = carry
            l_k, (gw_k, gx_k) = one_microbatch(xs[0], xs[1])
            with _jax.named_scope("update"):
                return (loss_sum + l_k, _jax.tree.map(_jnp.add, grad_sum, gw_k)), gx_k

        init = (_jnp.zeros((), _jnp.float32), _jax.tree.map(_jnp.zeros_like, weights))
        (loss, grad_w), grad_x = _jax.lax.scan(body, init, (per_example, given["loss_target"]))
    with _jax.named_scope("update"):
        delta_w, new_m, new_v = {}, {}, {}
        for n in TWIN_WEIGHTS:
            delta_w[n], new_m[n], new_v[n] = _adamw(weights[n], grad_w[n], given["m_" + n], given["v_" + n])
    return (loss, grad_x, *[grad_w[n] for n in TWIN_WEIGHTS], *[delta_w[n] for n in TWIN_WEIGHTS],
            *[new_m[n] for n in TWIN_WEIGHTS], *[new_v[n] for n in TWIN_WEIGHTS])
```

```python
import functools
import math

import jax
import jax.numpy as jnp
from jax import lax
from jax.experimental import pallas as pl
from jax.experimental.pallas import tpu as pltpu

F32 = jnp.float32
BF16 = jnp.bfloat16
N_DEV = 8
LANES = 128
SUBLANES = 8
VMEM_LIMIT = 56 * 1024 * 1024
GRID_W = 64
POS_TEMP = 10000.0
LN_EPS = 1e-5
LAMBDA_RE_MAX = -1e-4
ADAM_LR, ADAM_B1, ADAM_B2, ADAM_EPS, ADAM_WD, ADAM_STEP = 0.001, 0.9, 0.999, 1e-08, 0.01, 10
MESH = pl.DeviceIdType.MESH


def _params(sem):
    return pltpu.CompilerParams(dimension_semantics=sem, vmem_limit_bytes=VMEM_LIMIT)


def _accumulate(ref, val, first):
    @pl.when(first)
    def _():
        ref[...] = val

    @pl.when(jnp.logical_not(first))
    def _():
        ref[...] += val


def _rowwise(name, fn, rows, segs, vecs, row_outs, seg_accs, vec_accs, cfg):
    tr, tpl, nb = cfg["tr"], cfg["tpl"], cfg["nb"]
    n_rows = rows[0].shape[0]
    nt = n_rows // tr
    nr, ns, nv = len(rows), len(segs), len(vecs)
    nro, nsa = len(row_outs), len(seg_accs)

    def seg_of(t):
        return jnp.minimum(t // tpl, nb)

    def body(*refs):
        t = pl.program_id(0)
        ins, outs = refs[:nr + ns + nv], refs[nr + ns + nv:]
        vals = [r[...] for r in ins[:nr]] + [r[0] for r in ins[nr:nr + ns]] + [r[...] for r in ins[nr + ns:]]
        res = fn(*vals)
        for o, v in zip(outs[:nro], res[:nro]):
            o[...] = v.astype(o.dtype)
        first_seg = jnp.logical_or(t == 0, seg_of(t) != seg_of(jnp.maximum(t - 1, 0)))
        for o, v in zip(outs[nro:nro + nsa], res[nro:nro + nsa]):
            _accumulate(o.at[0], v, first_seg)
        for o, v in zip(outs[nro + nsa:], res[nro + nsa:]):
            _accumulate(o, v, t == 0)

    in_specs = ([pl.BlockSpec((tr, a.shape[1]), lambda t: (t, 0)) for a in rows]
                + [pl.BlockSpec((1, 1, a.shape[2]), lambda t: (seg_of(t), 0, 0)) for a in segs]
                + [pl.BlockSpec((1, a.shape[1]), lambda t: (0, 0)) for a in vecs])
    out_specs = ([pl.BlockSpec((tr, c), lambda t: (t, 0)) for c, _ in row_outs]
                 + [pl.BlockSpec((1, 1, c), lambda t: (seg_of(t), 0, 0)) for c in seg_accs]
                 + [pl.BlockSpec((1, c), lambda t: (0, 0)) for c in vec_accs])
    out_shape = ([jax.ShapeDtypeStruct((n_rows, c), dt) for c, dt in row_outs]
                 + [jax.ShapeDtypeStruct((nb + 1, 1, c), F32) for c in seg_accs]
                 + [jax.ShapeDtypeStruct((1, c), F32) for c in vec_accs])
    return pl.pallas_call(body, name=name, grid=(nt,), in_specs=in_specs, out_specs=out_specs,
                          out_shape=out_shape, compiler_params=_params(("arbitrary",)))(*rows, *segs, *vecs)


def _vjp_fn(fn, n_row, cot_groups, want):
    n_cot = sum(cot_groups)

    def bwd(*args):
        primals = [a.astype(F32) for a in args[:n_row] + args[n_row + n_cot:]]
        outs, vjp = jax.vjp(fn, *primals)
        cots, pos = [], n_row
        for n, o in zip(cot_groups, outs):
            cot = jnp.zeros_like(o)
            for part in args[pos:pos + n]:
                cot = cot + part.astype(F32)
            cots.append(cot)
            pos += n
        grads = vjp(tuple(cots))
        return tuple(grads[i] for i in want)
    return bwd


def _ln(r, g, b):
    mu = jnp.mean(r, axis=-1, keepdims=True)
    var = jnp.mean(jnp.square(r - mu), axis=-1, keepdims=True)
    return (r - mu) * lax.rsqrt(var + LN_EPS) * g + b


def _glu(zz, bias):
    d = zz.shape[1] // 2
    return (zz[:, :d] + bias[:, :d]) * jax.nn.sigmoid(zz[:, d:] + bias[:, d:])


def _f_entry(xc, pos, sh, sc):
    x0 = xc + pos
    return x0, x0 * (1 + sc) + sh


def _f_gelu(x, y0, y1, sh, sc, dsk):
    u = x * (1 + sc) + sh
    y = dsk * u + y0 + y1
    return (0.5 * y * (1.0 + lax.erf(y * (2.0 ** -0.5))),)


def _make_sub1_s5(alpha):
    def f(x, zz, g1, sh2, sc2, bglu, gain, bias):
        x1 = _ln(alpha * x + g1 * _glu(zz, bglu), gain, bias)
        return x1, x1 * (1 + sc2) + sh2
    return f


def _make_sub1_cv(alpha):
    def f(x, mm, g1, sh2, sc2, bpw2, gain, bias):
        x1 = _ln(alpha * x + g1 * (mm + bpw2), gain, bias)
        return x1, x1 * (1 + sc2) + sh2
    return f


def _make_sub2(alpha):
    def f(x1, m, g2, shn, scn, gain, bias):
        x2 = _ln(alpha * x1 + g2 * m, gain, bias)
        return x2, x2 * (1 + scn) + shn
    return f


def _f_cvglu(zz, bpw1):
    return (_glu(zz, bpw1),)


def _f_cvln(cv, bdw, lng, lnb):
    return (jax.nn.silu(_ln(cv + bdw, lng, lnb)),)


def _matmul(name, a, b, extras, grid, a_spec, b_spec, extra_specs, o_specs, out_shape, dims, red_axis, epi, sem):
    n_extra = len(extras)
    n_out = len(out_shape)
    acc_shape = o_specs[0].block_shape
    acc_shape = tuple(s for s in acc_shape if s is not None)

    def body(*refs):
        a_ref, b_ref = refs[0], refs[1]
        ex = refs[2:2 + n_extra]
        outs = refs[2 + n_extra:2 + n_extra + n_out]
        prod = lax.dot_general(a_ref[...], b_ref[...], dims, preferred_element_type=F32)

        def finish(acc):
            res = epi(acc, *[e[...] for e in ex]) if epi is not None else (acc,)
            for o, v in zip(outs, res):
                o[...] = v.astype(o.dtype)

        if red_axis is None:
            finish(prod)
        else:
            acc_ref = refs[-1]
            k = pl.program_id(red_axis)
            nk = pl.num_programs(red_axis)

            @pl.when(k == 0)
            def _():
                acc_ref[...] = prod

            @pl.when(k > 0)
            def _():
                acc_ref[...] += prod

            @pl.when(k == nk - 1)
            def _():
                finish(acc_ref[...])

    scratch = [] if red_axis is None else [pltpu.VMEM(acc_shape, F32)]
    res = pl.pallas_call(body, name=name, grid=grid, in_specs=[a_spec, b_spec] + list(extra_specs),
                         out_specs=list(o_specs), out_shape=list(out_shape), scratch_shapes=scratch,
                         compiler_params=_params(sem))(a, b, *extras)
    return res


NN = (((1,), (0,)), ((), ()))
NT = (((1,), (1,)), ((), ()))
TN = (((0,), (0,)), ((), ()))


def _mm_cols(name, a, wb, tm, out_dtypes=(F32,), epi=None):
    m, k = a.shape
    nblk, _, n = wb.shape
    return _matmul(name, a, wb, (), (nblk, m // tm),
                   pl.BlockSpec((tm, k), lambda j, i: (i, 0)), pl.BlockSpec((None, k, n), lambda j, i: (j, 0, 0)), (),
                   [pl.BlockSpec((tm, n), lambda j, i: (i, j)) for _ in out_dtypes],
                   [jax.ShapeDtypeStruct((m, nblk * n), dt) for dt in out_dtypes], NN, None, epi,
                   ("arbitrary", "arbitrary"))


def _mm_rows(name, a, w, tm, tk):
    m, k = a.shape
    n = w.shape[1]
    return _matmul(name, a, w, (), (m // tm, k // tk),
                   pl.BlockSpec((tm, tk), lambda i, kk: (i, kk)), pl.BlockSpec((tk, n), lambda i, kk: (kk, 0)), (),
                   [pl.BlockSpec((tm, n), lambda i, kk: (i, 0))], [jax.ShapeDtypeStruct((m, n), F32)], NN, 1, None,
                   ("arbitrary", "arbitrary"))[0]


def _mm_dgrad_cols(name, dy, wb, tm):
    m = dy.shape[0]
    nblk, k, n = wb.shape
    return _matmul(name, dy, wb, (), (m // tm, nblk),
                   pl.BlockSpec((tm, n), lambda i, j: (i, j)), pl.BlockSpec((None, k, n), lambda i, j: (j, 0, 0)), (),
                   [pl.BlockSpec((tm, k), lambda i, j: (i, 0))], [jax.ShapeDtypeStruct((m, k), F32)], NT, 1, None,
                   ("arbitrary", "arbitrary"))[0]


def _mm_dgrad_rows(name, dy, w, tm, tkw, extras=(), out_dtype=F32, epi=None):
    m, n = dy.shape
    kw = w.shape[0]
    return _matmul(name, dy, w, tuple(extras), (m // tm, kw // tkw),
                   pl.BlockSpec((tm, n), lambda i, j: (i, 0)), pl.BlockSpec((tkw, n), lambda i, j: (j, 0)),
                   [pl.BlockSpec((tm, tkw), lambda i, j: (i, j)) for _ in extras],
                   [pl.BlockSpec((tm, tkw), lambda i, j: (i, j))], [jax.ShapeDtypeStruct((m, kw), out_dtype)], NT, None, epi,
                   ("arbitrary", "arbitrary"))[0]


def _mm_wgrad_cols(name, a, dy, nblk, tm):
    m, k = a.shape
    n = dy.shape[1] // nblk
    return _matmul(name, a, dy, (), (nblk, m // tm),
                   pl.BlockSpec((tm, k), lambda j, i: (i, 0)), pl.BlockSpec((tm, n), lambda j, i: (i, j)), (),
                   [pl.BlockSpec((None, k, n), lambda j, i: (j, 0, 0))], [jax.ShapeDtypeStruct((nblk, k, n), BF16)], TN, 1, None,
                   ("arbitrary", "arbitrary"))[0]


def _mm_wgrad_rows(name, a, dy, tkw, tm):
    m, kw = a.shape
    n = dy.shape[1]
    return _matmul(name, a, dy, (), (kw // tkw, m // tm),
                   pl.BlockSpec((tm, tkw), lambda j, i: (i, j)), pl.BlockSpec((tm, n), lambda j, i: (i, 0)), (),
                   [pl.BlockSpec((tkw, n), lambda j, i: (j, 0))], [jax.ShapeDtypeStruct((kw, n), BF16)], TN, 1, None,
                   ("arbitrary", "arbitrary"))[0]


def _exchange(name, arrays, scatter):
    n = len(arrays)

    def body(*refs):
        ins, outs = refs[:n], refs[n:2 * n]
        send_sems, recv_sems, local_sems = refs[2 * n:]
        x, y, c = lax.axis_index("x"), lax.axis_index("y"), lax.axis_index("c")
        me = 4 * x + 2 * y + c
        copies = []
        for i in range(n):
            own = ins[i].at[me] if scatter else ins[i]
            local = pltpu.make_async_copy(own, outs[i].at[me], local_sems.at[i])
            local.start()
            copies.append(local)
            for k in range(1, N_DEV):
                px = 1 - x if k & 4 else x
                py = 1 - y if k & 2 else y
                pc = 1 - c if k & 1 else c
                src = ins[i].at[4 * px + 2 * py + pc] if scatter else ins[i]
                cp = pltpu.make_async_remote_copy(src_ref=src, dst_ref=outs[i].at[me], send_sem=send_sems.at[i, k - 1],
                                                  recv_sem=recv_sems.at[i, k - 1], device_id=(px, py, pc), device_id_type=MESH)
                cp.start()
                copies.append(cp)
        for cp in copies:
            cp.wait()

    any_spec = pl.BlockSpec(memory_space=pl.ANY)
    out_shape = [jax.ShapeDtypeStruct((N_DEV,) + (a.shape[1:] if scatter else a.shape), a.dtype) for a in arrays]
    return pl.pallas_call(body, name=name, in_specs=[any_spec] * n, out_specs=[any_spec] * n, out_shape=out_shape,
                          scratch_shapes=[pltpu.SemaphoreType.DMA((n, N_DEV - 1)), pltpu.SemaphoreType.DMA((n, N_DEV - 1)),
                                          pltpu.SemaphoreType.DMA((n,))])(*arrays)


def _sum_lead(name, parts, tr):
    npart, r, c = parts.shape

    def body(p_ref, o_ref):
        acc = p_ref[0].astype(F32)
        for p in range(1, npart):
            acc = acc + p_ref[p].astype(F32)
        o_ref[...] = acc

    return pl.pallas_call(body, name=name, grid=(r // tr,), in_specs=[pl.BlockSpec((npart, tr, c), lambda i: (0, i, 0))],
                          out_specs=pl.BlockSpec((tr, c), lambda i: (i, 0)), out_shape=jax.ShapeDtypeStruct((r, c), F32),
                          compiler_params=_params(("arbitrary",)))(parts)


def _adamw(name, parts, w, m, v, tr):
    npart, r, c = parts.shape

    def body(p_ref, w_ref, m_ref, v_ref, g_out, d_out, m_out, v_out):
        g = p_ref[0].astype(F32)
        for p in range(1, npart):
            g = g + p_ref[p].astype(F32)
        m2 = ADAM_B1 * m_ref[...] + (1.0 - ADAM_B1) * g
        v2 = ADAM_B2 * v_ref[...] + (1.0 - ADAM_B2) * jnp.square(g)
        m_hat = m2 / (1.0 - ADAM_B1 ** ADAM_STEP)
        v_hat = v2 / (1.0 - ADAM_B2 ** ADAM_STEP)
        g_out[...] = g
        d_out[...] = -ADAM_LR * (m_hat / (jnp.sqrt(v_hat) + ADAM_EPS) + ADAM_WD * w_ref[...])
        m_out[...] = m2
        v_out[...] = v2

    row = pl.BlockSpec((tr, c), lambda i: (i, 0))
    return pl.pallas_call(body, name=name, grid=(r // tr,),
                          in_specs=[pl.BlockSpec((npart, tr, c), lambda i: (0, i, 0)), row, row, row],
                          out_specs=[row] * 4, out_shape=[jax.ShapeDtypeStruct((r, c), F32)] * 4,
                          compiler_params=_params(("arbitrary",)))(parts, w, m, v)


def _row_tile(r, cap):
    if r <= cap:
        return r
    t = cap - cap % SUBLANES
    while r % t:
        t -= SUBLANES
    return t


def _ada_fwd(cc, w_ada, b_loc):
    nl, d, n = w_ada.shape
    rows = cc.shape[0]

    def body(c_ref, w_ref, b_ref, o_ref):
        cond = jax.nn.silu(c_ref[...]).astype(BF16)
        o_ref[...] = jnp.dot(cond, w_ref[...].astype(BF16), preferred_element_type=F32) + b_ref[...]

    return pl.pallas_call(body, name="ada_fwd", grid=(nl,),
                          in_specs=[pl.BlockSpec((rows, d), lambda i: (0, 0)), pl.BlockSpec((None, d, n), lambda i: (i, 0, 0)),
                                    pl.BlockSpec((None, 1, n), lambda i: (i, 0, 0))],
                          out_specs=pl.BlockSpec((None, rows, n), lambda i: (i, 0, 0)),
                          out_shape=jax.ShapeDtypeStruct((nl, rows, n), F32), compiler_params=_params(("arbitrary",)))(cc, w_ada, b_loc)


def _ada_bwd(cc, w_ada, dmod_rows, dmod_ctx):
    nl, d, n = w_ada.shape
    rows = cc.shape[0]
    ctx_row = rows - SUBLANES

    def body(c_ref, w_ref, dr_ref, dc_ref, gw_ref, tot_ref, dcond_ref):
        i = pl.program_id(0)
        total = dc_ref[0]
        for p in range(1, N_DEV):
            total = total + dc_ref[p]
        tot_ref[...] = total
        row_id = lax.broadcasted_iota(jnp.int32, (rows, n), 0)
        dm = jnp.where(row_id == ctx_row, jnp.broadcast_to(total, (rows, n)), dr_ref[...]).astype(BF16)
        cond = jax.nn.silu(c_ref[...]).astype(BF16)
        gw_ref[...] = lax.dot_general(cond, dm, TN, preferred_element_type=F32)
        part = lax.dot_general(dm, w_ref[...].astype(BF16), NT, preferred_element_type=F32)
        _accumulate(dcond_ref, part, i == 0)

    return pl.pallas_call(body, name="ada_bwd", grid=(nl,),
                          in_specs=[pl.BlockSpec((rows, d), lambda i: (0, 0)), pl.BlockSpec((None, d, n), lambda i: (i, 0, 0)),
                                    pl.BlockSpec((None, rows, n), lambda i: (i, 0, 0)),
                                    pl.BlockSpec((N_DEV, None, 1, n), lambda i: (0, i, 0, 0))],
                          out_specs=[pl.BlockSpec((None, d, n), lambda i: (i, 0, 0)), pl.BlockSpec((None, 1, n), lambda i: (i, 0, 0)),
                                     pl.BlockSpec((rows, d), lambda i: (0, 0))],
                          out_shape=[jax.ShapeDtypeStruct((nl, d, n), F32), jax.ShapeDtypeStruct((nl, 1, n), F32),
                                     jax.ShapeDtypeStruct((rows, d), F32)],
                          compiler_params=_params(("arbitrary",)))(cc, w_ada, dmod_rows, dmod_ctx)


def _cctx_grad(parts, c_ctx):
    def body(p_ref, c_ref, o_ref):
        tot = p_ref[0]
        for p in range(1, N_DEV):
            tot = tot + p_ref[p]
        _, vjp = jax.vjp(jax.nn.silu, c_ref[...])
        o_ref[...] = vjp(tot)[0]

    return pl.pallas_call(body, name="cctx_grad", out_shape=jax.ShapeDtypeStruct(c_ctx.shape, F32))(parts, c_ctx)


def _discretise(lam_re, lam_im, log_dt, b_re, b_im):
    lr = jnp.minimum(lam_re, LAMBDA_RE_MAX)
    li = lam_im
    dt = jnp.exp(log_dt)
    mag = jnp.exp(lr * dt)
    ab_re = mag * jnp.cos(li * dt)
    ab_im = mag * jnp.sin(li * dt)
    den = lr * lr + li * li
    nr = ab_re - 1.0
    ni = ab_im
    coef_re = ((nr * lr + ni * li) / den)[:, None]
    coef_im = ((ni * lr - nr * li) / den)[:, None]
    bb_re = coef_re * b_re - coef_im * b_im
    bb_im = coef_re * b_im + coef_im * b_re
    return ab_re, ab_im, bb_re, bb_im


def _s5_prep(name, lam_re, lam_im, log_dt, b_re, b_im):
    def body(a, b, c, d, e, o1, o2, o3, o4):
        res = _discretise(a[...], b[...], c[...], d[...], e[...])
        for o, v in zip((o1, o2, o3, o4), res):
            o[...] = v

    shp = [jax.ShapeDtypeStruct(lam_re.shape, F32)] * 2 + [jax.ShapeDtypeStruct(b_re.shape, F32)] * 2
    return pl.pallas_call(body, name=name, out_shape=shp)(lam_re, lam_im, log_dt, b_re, b_im)


def _s5_prep_bwd(name, lam_re, lam_im, log_dt, b_re, b_im, cots):
    def body(a, b, c, d, e, c1, c2, c3, c4, o1, o2, o3, o4, o5):
        _, vjp = jax.vjp(_discretise, a[...], b[...], c[...], d[...], e[...])
        grads = vjp((c1[...], c2[...], c3[...], c4[...]))
        for o, v in zip((o1, o2, o3, o4, o5), grads):
            o[...] = v

    shp = [jax.ShapeDtypeStruct(a.shape, F32) for a in (lam_re, lam_im, log_dt, b_re, b_im)]
    return pl.pallas_call(body, name=name, out_shape=shp)(lam_re, lam_im, log_dt, b_re, b_im, *cots)


def _scan_tile(h_ref, t_ref, carry_ref, up, hook=None, hook_init=None):
    sw = h_ref.shape[1] // 2
    ng = h_ref.shape[0] // SUBLANES

    def tab(i):
        return t_ref[SUBLANES * i:SUBLANES * (i + 1), :sw], t_ref[SUBLANES * i:SUBLANES * (i + 1), sw:]

    steps = ((1, tab(0)), (2, tab(1)), (4, tab(2)))
    p_re, p_im = tab(3)
    edge = 0 if up else SUBLANES - 1

    def group(j, state):
        carry, extra = state
        g = ng - 1 - j if up else j
        r0 = pl.multiple_of(g * SUBLANES, SUBLANES)
        xr = h_ref[pl.ds(r0, SUBLANES), :sw]
        xi = h_ref[pl.ds(r0, SUBLANES), sw:]
        for sh, (a_re, a_im) in steps:
            amount = SUBLANES - sh if up else sh
            sr = pltpu.roll(xr, amount, 0)
            si = pltpu.roll(xi, amount, 0)
            xr, xi = xr + a_re * sr - a_im * si, xi + a_re * si + a_im * sr
        cr, ci = carry
        xr, xi = xr + p_re * cr - p_im * ci, xi + p_re * ci + p_im * cr
        h_ref[pl.ds(r0, SUBLANES), :sw] = xr
        h_ref[pl.ds(r0, SUBLANES), sw:] = xi
        if hook is not None:
            extra = hook(r0, xr, xi, cr, ci, extra)
        new_carry = (jnp.broadcast_to(xr[edge:edge + 1], xr.shape), jnp.broadcast_to(xi[edge:edge + 1], xi.shape))
        return new_carry, extra

    carry0 = (carry_ref[:, :sw], carry_ref[:, sw:])
    carry, extra = lax.fori_loop(0, ng, group, (carry0, hook_init))
    carry_ref[:, :sw] = carry[0]
    carry_ref[:, sw:] = carry[1]
    return extra


def _s5_tile_index(cfg, dirn, adjoint):
    tpl, nb = cfg["tpl"], cfg["nb"]

    def idx(b, k):
        if not adjoint:
            lat = b * tpl + (k - 1 if dirn == 0 else tpl - k)
            return jnp.where(k == 0, nb * tpl + b, lat)
        lat = b * tpl + (tpl - 1 - k if dirn == 0 else k)
        return jnp.where(k == tpl, nb * tpl + b, lat)
    return idx


def _s5_fwd(name, u, bmat, cmat, tab, dirn, cfg):
    tr, tpl, nb = cfg["tr"], cfg["tpl"], cfg["nb"]
    n_rows, d = u.shape
    ns, _, sw2 = bmat.shape
    tile = _s5_tile_index(cfg, dirn, False)
    up = dirn == 1

    def body(u_ref, b_ref, c_ref, t_ref, h_ref, y_ref, carry_ref):
        @pl.when(pl.program_id(2) == 0)
        def _():
            carry_ref[...] = jnp.zeros_like(carry_ref)

        h_ref[...] = jnp.dot(u_ref[...], b_ref[...], preferred_element_type=F32)
        _scan_tile(h_ref, t_ref, carry_ref, up)
        y_ref[...] = jnp.dot(h_ref[...].astype(BF16), c_ref[...], preferred_element_type=F32)

    return pl.pallas_call(
        body, name=name, grid=(ns, nb, tpl + 1),
        in_specs=[pl.BlockSpec((tr, LANES), lambda s, b, k: (tile(b, k), s)),
                  pl.BlockSpec((None, LANES, sw2), lambda s, b, k: (s, 0, 0)),
                  pl.BlockSpec((None, sw2, LANES), lambda s, b, k: (s, 0, 0)),
                  pl.BlockSpec((None, 4 * SUBLANES, sw2), lambda s, b, k: (s, 0, 0))],
        out_specs=[pl.BlockSpec((tr, sw2), lambda s, b, k: (tile(b, k), s)),
                   pl.BlockSpec((tr, LANES), lambda s, b, k: (tile(b, k), s))],
        out_shape=[jax.ShapeDtypeStruct((n_rows, ns * sw2), F32), jax.ShapeDtypeStruct((n_rows, d), F32)],
        scratch_shapes=[pltpu.VMEM((SUBLANES, sw2), F32)],
        compiler_params=_params(("arbitrary", "arbitrary", "arbitrary")))(u, bmat, cmat, tab)


def _s5_bwd(name, dy, h, u, cmat_t, bmat_t, tab, dirn, cfg):
    tr, tpl, nb = cfg["tr"], cfg["tpl"], cfg["nb"]
    n_rows, d = u.shape
    ns, _, sw2 = cmat_t.shape
    sw = sw2 // 2
    tile = _s5_tile_index(cfg, dirn, True)
    up = dirn == 0
    edge = SUBLANES - 1 if up else 0
    one = SUBLANES - 1 if up else 1

    def body(dy_ref, h_ref, u_ref, ct_ref, bt_ref, t_ref, du_ref, db_ref, dc_ref, da_ref, lam_ref, carry_ref):
        first = jnp.logical_and(pl.program_id(1) == 0, pl.program_id(2) == 0)

        @pl.when(pl.program_id(2) == 0)
        def _():
            carry_ref[...] = jnp.zeros_like(carry_ref)

        lam_ref[...] = jnp.dot(dy_ref[...], ct_ref[...], preferred_element_type=F32)
        row_id = lax.broadcasted_iota(jnp.int32, (SUBLANES, sw), 0)

        def hook(r0, xr, xi, cr, ci, acc):
            lr = jnp.where(row_id == edge, cr, pltpu.roll(xr, one, 0))
            li = jnp.where(row_id == edge, ci, pltpu.roll(xi, one, 0))
            hr = h_ref[pl.ds(r0, SUBLANES), :sw]
            hi = h_ref[pl.ds(r0, SUBLANES), sw:]
            return acc[0] + lr * hr + li * hi, acc[1] + li * hr - lr * hi

        zero = jnp.zeros((SUBLANES, sw), F32)
        acc = _scan_tile(lam_ref, t_ref, carry_ref, up, hook, (zero, zero))
        lam = lam_ref[...].astype(BF16)
        d_b = lax.dot_general(u_ref[...], lam, TN, preferred_element_type=F32)
        d_c = lax.dot_general(h_ref[...].astype(BF16), dy_ref[...], TN, preferred_element_type=F32)
        du_ref[...] = jnp.dot(lam, bt_ref[...], preferred_element_type=F32)

        @pl.when(first)
        def _():
            db_ref[...] = d_b
            dc_ref[...] = d_c
            da_ref[:, :sw] = acc[0]
            da_ref[:, sw:] = acc[1]

        @pl.when(jnp.logical_not(first))
        def _():
            db_ref[...] += d_b
            dc_ref[...] += d_c
            da_ref[:, :sw] += acc[0]
            da_ref[:, sw:] += acc[1]

    return pl.pallas_call(
        body, name=name, grid=(ns, nb, tpl + 1),
        in_specs=[pl.BlockSpec((tr, LANES), lambda s, b, k: (tile(b, k), s)),
                  pl.BlockSpec((tr, sw2), lambda s, b, k: (tile(b, k), s)),
                  pl.BlockSpec((tr, LANES), lambda s, b, k: (tile(b, k), s)),
                  pl.BlockSpec((None, LANES, sw2), lambda s, b, k: (s, 0, 0)),
                  pl.BlockSpec((None, sw2, LANES), lambda s, b, k: (s, 0, 0)),
                  pl.BlockSpec((None, 4 * SUBLANES, sw2), lambda s, b, k: (s, 0, 0))],
        out_specs=[pl.BlockSpec((tr, LANES), lambda s, b, k: (tile(b, k), s)),
                   pl.BlockSpec((None, LANES, sw2), lambda s, b, k: (s, 0, 0)),
                   pl.BlockSpec((None, sw2, LANES), lambda s, b, k: (s, 0, 0)),
                   pl.BlockSpec((None, SUBLANES, sw2), lambda s, b, k: (s, 0, 0))],
        out_shape=[jax.ShapeDtypeStruct((n_rows, d), F32), jax.ShapeDtypeStruct((ns, LANES, sw2), F32),
                   jax.ShapeDtypeStruct((ns, sw2, LANES), F32), jax.ShapeDtypeStruct((ns, SUBLANES, sw2), F32)],
        scratch_shapes=[pltpu.VMEM((tr, sw2), F32), pltpu.VMEM((SUBLANES, sw2), F32)],
        compiler_params=_params(("arbitrary", "arbitrary", "arbitrary")))(dy, h, u, cmat_t, bmat_t, tab)


def _s5_tables(ab_re, ab_im, up, conj, ns):
    a_re = ab_re.reshape(ns, -1)
    a_im = (-ab_im if conj else ab_im).reshape(ns, -1)
    powers = [(a_re, a_im)]
    for _ in range(SUBLANES - 1):
        q_re, q_im = powers[-1]
        powers.append((q_re * a_re - q_im * a_im, q_re * a_im + q_im * a_re))
    rows = jnp.arange(SUBLANES)
    re_blocks, im_blocks = [], []
    for sh in (1, 2, 4):
        keep = (rows <= SUBLANES - 1 - sh) if up else (rows >= sh)
        q_re, q_im = powers[sh - 1]
        re_blocks.append(jnp.where(keep[None, :, None], q_re[:, None, :], 0.0))
        im_blocks.append(jnp.where(keep[None, :, None], q_im[:, None, :], 0.0))
    dist = range(SUBLANES, 0, -1) if up else range(1, SUBLANES + 1)
    re_blocks.append(jnp.stack([powers[dd - 1][0] for dd in dist], axis=1))
    im_blocks.append(jnp.stack([powers[dd - 1][1] for dd in dist], axis=1))
    return jnp.concatenate([jnp.concatenate(re_blocks, axis=1), jnp.concatenate(im_blocks, axis=1)], axis=2)


def _block_diag(blocks):
    ns, gs, a, b = blocks.shape
    eye = jnp.eye(gs, dtype=blocks.dtype)
    return (blocks[:, :, :, None, :] * eye[None, :, None, :, None]).reshape(ns, gs * a, gs * b)


def _diag_blocks(mat, gs):
    ns, ra, rb = mat.shape
    a, b = ra // gs, rb // gs
    m5 = mat.reshape(ns, gs, a, gs, b)
    eye = jnp.eye(gs, dtype=mat.dtype)
    return jnp.sum(m5 * eye[None, :, None, :, None], axis=3)


def _conv_flags(t, cfg):
    tpl, nb = cfg["tpl"], cfg["nb"]
    latent = t < nb * tpl
    first = jnp.logical_or(jnp.logical_not(latent), t % tpl == 0)
    last = jnp.logical_or(jnp.logical_not(latent), t % tpl == tpl - 1)
    return first, last


def _fill_ext(ext_ref, prev_ref, cur_ref, next_ref, t, cfg, halo):
    first, last = _conv_flags(t, cfg)
    tr = cur_ref.shape[0]
    ext_ref[0:halo, :] = jnp.where(first, 0.0, prev_ref[...])
    ext_ref[halo:halo + tr, :] = cur_ref[...]
    ext_ref[halo + tr:, :] = jnp.where(last, 0.0, next_ref[...])


def _conv_specs(tr, n_rows, halo):
    per = tr // halo
    n_halo = n_rows // halo
    return [pl.BlockSpec((halo, LANES), lambda c, t: (jnp.maximum(t * per - 1, 0), c)),
            pl.BlockSpec((tr, LANES), lambda c, t: (t, c)),
            pl.BlockSpec((halo, LANES), lambda c, t: (jnp.minimum((t + 1) * per, n_halo - 1), c))]


def _dwconv(name, a, w, cfg):
    tr = cfg["tr"]
    n_rows, d = a.shape
    kw = w.shape[0]
    half = kw // 2
    halo = 2 * SUBLANES

    def body(prev_ref, cur_ref, next_ref, w_ref, o_ref, ext_ref):
        _fill_ext(ext_ref, prev_ref, cur_ref, next_ref, pl.program_id(1), cfg, halo)
        acc = jnp.zeros((tr, LANES), F32)
        for k in range(kw):
            acc = acc + ext_ref[pl.ds(halo - half + k, tr), :] * w_ref[k:k + 1, :]
        o_ref[...] = acc

    return pl.pallas_call(body, name=name, grid=(d // LANES, n_rows // tr),
                          in_specs=_conv_specs(tr, n_rows, halo) + [pl.BlockSpec((kw, LANES), lambda c, t: (0, c))],
                          out_specs=pl.BlockSpec((tr, LANES), lambda c, t: (t, c)),
                          out_shape=jax.ShapeDtypeStruct((n_rows, d), F32),
                          scratch_shapes=[pltpu.VMEM((tr + 2 * halo, LANES), F32)],
                          compiler_params=_params(("arbitrary", "arbitrary")))(a, a, a, w)


def _dwconv_wgrad(name, a, dout, kw, cfg):
    tr = cfg["tr"]
    n_rows, d = a.shape
    half = kw // 2
    halo = 2 * SUBLANES

    def body(prev_ref, cur_ref, next_ref, do_ref, o_ref, ext_ref):
        t = pl.program_id(1)
        _fill_ext(ext_ref, prev_ref, cur_ref, next_ref, t, cfg, halo)
        dout_t = do_ref[...]
        rows = [jnp.sum(ext_ref[pl.ds(halo - half + k, tr), :] * dout_t, axis=0, keepdims=True) for k in range(kw)]
        part = jnp.concatenate(rows, axis=0)
        _accumulate(o_ref, part, t == 0)

    return pl.pallas_call(body, name=name, grid=(d // LANES, n_rows // tr),
                          in_specs=_conv_specs(tr, n_rows, halo) + [pl.BlockSpec((tr, LANES), lambda c, t: (t, c))],
                          out_specs=pl.BlockSpec((kw, LANES), lambda c, t: (0, c)),
                          out_shape=jax.ShapeDtypeStruct((kw, d), F32),
                          scratch_shapes=[pltpu.VMEM((tr + 2 * halo, LANES), F32)],
                          compiler_params=_params(("arbitrary", "arbitrary")))(a, a, a, dout)


def _sincos_1d(pos, dim):
    quarter = dim // 2
    omega = POS_TEMP ** (-jnp.arange(quarter, dtype=F32) / quarter)
    ang = pos[:, None] * omega[None, :]
    return jnp.concatenate([jnp.sin(ang), jnp.cos(ang)], axis=-1)


def _grid_pos_embed(rows, dim):
    row_idx = jnp.repeat(jnp.arange(rows), GRID_W).astype(F32)
    col_idx = jnp.tile(jnp.arange(GRID_W), rows).astype(F32)
    return jnp.concatenate([_sincos_1d(row_idx, dim // 2), _sincos_1d(col_idx, dim // 2)], axis=-1)


def _pack(arrs):
    flat = jnp.concatenate([a.reshape(-1).astype(F32) for a in arrs])
    pad = (-flat.shape[0]) % (SUBLANES * LANES)
    return jnp.pad(flat, (0, pad)).reshape(-1, LANES)


def _unpack(buf, shapes):
    flat = buf.reshape(-1)
    out, pos = [], 0
    for shp in shapes:
        n = math.prod(shp)
        out.append(flat[pos:pos + n].reshape(shp))
        pos += n
    return out


def _unpack_gathered(buf, shapes):
    flat = buf.reshape(N_DEV, -1)
    out, pos = [], 0
    for shp in shapes:
        n = math.prod(shp)
        part = flat[:, pos:pos + n].reshape((N_DEV,) + tuple(shp))
        out.append(jnp.moveaxis(part, 0, -2).reshape(tuple(shp[:-1]) + (N_DEV * shp[-1],)))
        pos += n
    return out


WEIGHTS = ("c_ctx", "w_ada", "b_ada", "ln_gain", "ln_bias", "s5_lam_re", "s5_lam_im", "s5_log_dt", "s5_b_re", "s5_b_im",
           "s5_c_re", "s5_c_im", "s5_d", "s5_w_glu", "s5_b_glu", "cv_w_pw1", "cv_b_pw1", "cv_w_dw", "cv_b_dw", "cv_ln_g",
           "cv_ln_b", "cv_w_pw2", "cv_b_pw2", "mlp_w1", "mlp_w2")
SHARDED_SMALL = ("ln_gain", "ln_bias", "cv_b_pw1", "cv_w_dw", "cv_b_dw", "cv_ln_g", "cv_ln_b", "cv_b_pw2")
REPLICATED_SMALL = ("s5_lam_re", "s5_lam_im", "s5_log_dt", "s5_b_re", "s5_b_im", "s5_c_re", "s5_c_im", "s5_d", "s5_b_glu")
BIG = ("mlp_w1", "mlp_w2", "s5_w_glu", "cv_w_pw1", "cv_w_pw2")


def _step(a):
    x, c, ctx = a["x"], a["c"], a["ctx"]
    nb, seq, d = x.shape
    lc = ctx.shape[1]
    nl = a["w_ada"].shape[0]
    tr = lc
    tpl = seq // tr
    cfg = {"tr": tr, "tpl": tpl, "nb": nb}
    n_rows = nb * (seq + lc)
    alpha = (2.0 * nl) ** 0.25
    me = 4 * lax.axis_index("x") + 2 * lax.axis_index("y") + lax.axis_index("c")
    n_grp, n_state = a["s5_lam_re"].shape[2:]
    ch = a["s5_b_re"].shape[-1]
    gs = LANES // ch
    ns = d // LANES
    tm = 2 * tr if n_rows % (2 * tr) == 0 else tr
    f_sub1_s5, f_sub1_cv, f_sub2 = _make_sub1_s5(alpha), _make_sub1_cv(alpha), _make_sub2(alpha)

    big_local = [a[n].astype(BF16).reshape(-1, a[n].shape[-1]) for n in BIG]
    big_all = _exchange("gather_weights", big_local, False)
    w1_all = big_all[0].reshape(N_DEV, nl, d, -1)
    w2_all = big_all[1].reshape(N_DEV, nl, -1, d)
    glu_all = big_all[2].reshape(N_DEV, nl // 2, d, -1)
    pw1_all = big_all[3].reshape(N_DEV, nl // 2, d, -1)
    pw2_all = big_all[4].reshape(N_DEV, nl // 2, -1, d)
    small_all = _exchange("gather_small", [_pack([a[n] for n in SHARDED_SMALL])], False)[0]
    full = dict(zip(SHARDED_SMALL, _unpack_gathered(small_all, [a[n].shape for n in SHARDED_SMALL])))
    c_all = _exchange("gather_c", [c], False)[0].reshape(N_DEV * nb, d)
    cond_rows = N_DEV * nb + SUBLANES
    cc = jnp.concatenate([c_all, a["c_ctx"][None], jnp.zeros((SUBLANES - 1, d), F32)], axis=0)

    n_ada = a["w_ada"].shape[2]
    b_loc = lax.dynamic_slice(a["b_ada"], (0, me * n_ada), (nl, n_ada))[:, None, :]
    mod_cols = _ada_fwd(cc, a["w_ada"], b_loc)
    mod_all = _exchange("gather_mod", [mod_cols.reshape(nl * cond_rows, n_ada)], False)[0].reshape(N_DEV, nl, cond_rows, n_ada)
    mod_mine = jnp.concatenate([lax.dynamic_slice(mod_all, (0, 0, nb * me, 0), (N_DEV, nl, nb, n_ada)),
                                mod_all[:, :, N_DEV * nb:N_DEV * nb + 1]], axis=2)
    mod = jnp.transpose(mod_mine, (1, 2, 0, 3)).reshape(nl, nb + 1, 6, 1, d)

    def seg(i, q):
        return mod[i, :, q]

    zero_seg = jnp.zeros((nb + 1, 1, d), F32)

    def vec(v):
        return v.reshape(1, -1)

    pos = _grid_pos_embed(seq // GRID_W, d)
    xc = jnp.concatenate([x.reshape(nb * seq, d), ctx.reshape(nb * lc, d)], axis=0)
    pos_rows = jnp.concatenate([jnp.tile(pos, (nb, 1)), jnp.zeros((nb * lc, d), F32)], axis=0)
    x_cur, h_cur = _rowwise("entry", _f_entry, [xc, pos_rows], [seg(0, 0), seg(0, 1)], [], [(d, F32), (d, BF16)], [], [], cfg)
    saved = []
    for i in range(nl):
        j = i // 2
        sv = {"x": x_cur, "h": h_cur}
        sh1, sc1, g1, sh2, sc2, g2 = (seg(i, q) for q in range(6))
        gain0, bias0, gain1, bias1 = (vec(full["ln_gain"][i, 0]), vec(full["ln_bias"][i, 0]),
                                      vec(full["ln_gain"][i, 1]), vec(full["ln_bias"][i, 1]))
        if i % 2 == 0:
            lam_re, lam_im = a["s5_lam_re"][j], a["s5_lam_im"][j]
            log_dt = a["s5_log_dt"][j][:, :, None]
            b_re_t = jnp.transpose(a["s5_b_re"][j], (0, 3, 1, 2))
            b_im_t = jnp.transpose(a["s5_b_im"][j], (0, 3, 1, 2))
            sv["prep_in"] = (lam_re, lam_im, log_dt, b_re_t, b_im_t)
            ab_re, ab_im, bb_re, bb_im = _s5_prep(f"s5_prep{i}", *sv["prep_in"])
            sv["ab"] = (ab_re, ab_im)
            ys = []
            for dirn in range(2):
                def blocks(t):
                    return jnp.transpose(t, (1, 0, 2)).reshape(ns, gs, ch, n_state)
                bmat = jnp.concatenate([_block_diag(blocks(bb_re[dirn])), _block_diag(blocks(bb_im[dirn]))], axis=2).astype(BF16)
                c_re_t = jnp.transpose(a["s5_c_re"][j, dirn], (0, 2, 1)).reshape(ns, gs, n_state, ch)
                c_im_t = jnp.transpose(a["s5_c_im"][j, dirn], (0, 2, 1)).reshape(ns, gs, n_state, ch)
                cmat = jnp.concatenate([_block_diag(c_re_t), -_block_diag(c_im_t)], axis=1).astype(BF16)
                tab = _s5_tables(ab_re[dirn], ab_im[dirn], dirn == 1, False, ns)
                h_states, y_dir = _s5_fwd(f"s5_fwd{i}_{dirn}", h_cur, bmat, cmat, tab, dirn, cfg)
                sv[f"mats{dirn}"] = (jnp.transpose(bmat, (0, 2, 1)), jnp.transpose(cmat, (0, 2, 1)))
                sv[f"states{dirn}"] = h_states
                ys.append(y_dir)
            sv["y"] = ys
            dsk = vec(a["s5_d"][j])
            z = _rowwise(f"gelu{i}", _f_gelu, [x_cur, ys[0], ys[1]], [sh1, sc1], [dsk], [(d, BF16)], [], [], cfg)[0]
            glu_w = glu_all[:, j]
            zz = _mm_cols(f"glu{i}", z, glu_w, tm)[0]
            bglu = vec(a["s5_b_glu"][j])
            x1, h2 = _rowwise(f"sub1_{i}", f_sub1_s5, [x_cur, zz], [g1, sh2, sc2], [bglu, gain0, bias0],
                              [(d, F32), (d, BF16)], [], [], cfg)
            sv.update(z=z, zz=zz, w_mix=glu_w)
        else:
            pw1_w = pw1_all[:, j]
            zz = _mm_cols(f"pw1_{i}", h_cur, pw1_w, tm)[0]
            bpw1 = vec(full["cv_b_pw1"][j])
            act = _rowwise(f"cvglu{i}", _f_cvglu, [zz], [], [bpw1], [(d, F32)], [], [], cfg)[0]
            w_dw = full["cv_w_dw"][j]
            cv = _dwconv(f"dwconv{i}", act, w_dw, cfg)
            bdw, lng, lnb = vec(full["cv_b_dw"][j]), vec(full["cv_ln_g"][j]), vec(full["cv_ln_b"][j])
            s_act = _rowwise(f"cvln{i}", _f_cvln, [cv], [], [bdw, lng, lnb], [(d, BF16)], [], [], cfg)[0]
            pw2_w = pw2_all[:, j].reshape(d, d)
            mm = _mm_rows(f"pw2_{i}", s_act, pw2_w, tm, min(d, 512))
            bpw2 = vec(full["cv_b_pw2"][j])
            x1, h2 = _rowwise(f"sub1_{i}", f_sub1_cv, [x_cur, mm], [g1, sh2, sc2], [bpw2, gain0, bias0],
                              [(d, F32), (d, BF16)], [], [], cfg)
            sv.update(zz=zz, act=act, cv=cv, s_act=s_act, mm=mm, w_pw1=pw1_w, w_pw2=pw2_w, w_dw=w_dw)
        w1 = w1_all[:, i]
        w2 = w2_all[:, i].reshape(-1, d)
        p_act, r_act = _mm_cols(f"mlp1_{i}", h2, w1, tm, (BF16, BF16),
                                lambda acc: (jnp.square(jnp.maximum(acc, 0.0)), jnp.maximum(acc, 0.0)))
        m_out = _mm_rows(f"mlp2_{i}", p_act, w2, tm, 512)
        shn, scn = (seg(i + 1, 0), seg(i + 1, 1)) if i + 1 < nl else (zero_seg, zero_seg)
        x2, hn = _rowwise(f"sub2_{i}", f_sub2, [x1, m_out], [g2, shn, scn], [gain1, bias1], [(d, F32), (d, BF16)], [], [], cfg)
        sv.update(x1=x1, h2=h2, p=p_act, r=r_act, m=m_out, w1=w1, w2=w2, shn=shn, scn=scn)
        saved.append(sv)
        x_cur, h_cur = x2, hn

    target = jnp.concatenate([a["loss_target"].reshape(nb * seq, d), jnp.zeros((nb * lc, d), F32)], axis=0)
    mask = jnp.concatenate([jnp.ones((nb, 1, d), F32), jnp.zeros((1, 1, d), F32)], axis=0)

    def f_loss(xf, tgt, msk):
        err = (xf - tgt) * msk
        part = 0.5 * jnp.sum(jnp.square(err), axis=(0, 1), keepdims=True) / d
        return err / d, jnp.broadcast_to(part, (1, LANES))

    dx_final, loss_part = _rowwise("loss", f_loss, [x_cur, target], [mask], [], [(d, F32)], [], [LANES], cfg)
    loss = lax.psum(loss_part[0, 0], ("x", "y", "c"))

    grads = {n: [None] * a[n].shape[0] for n in WEIGHTS if n not in ("c_ctx", "w_ada", "b_ada")}
    dmod = [[None] * 6 for _ in range(nl)]

    def add_mod(i, q, val):
        dmod[i][q] = val if dmod[i][q] is None else dmod[i][q] + val

    dx_parts, dh_parts = [dx_final], []
    for i in reversed(range(nl)):
        j = i // 2
        sv = saved[i]
        sh1, sc1, g1, sh2, sc2, g2 = (seg(i, q) for q in range(6))
        gain0, bias0, gain1, bias1 = (vec(full["ln_gain"][i, 0]), vec(full["ln_bias"][i, 0]),
                                      vec(full["ln_gain"][i, 1]), vec(full["ln_bias"][i, 1]))
        bwd = _vjp_fn(f_sub2, 2, (len(dx_parts), len(dh_parts)), (0, 1, 2, 3, 4, 5, 6))
        dx1, dm, dg2, dshn, dscn, dgain1, dbias1 = _rowwise(
            f"sub2_bwd{i}", bwd, [sv["x1"], sv["m"]] + dx_parts + dh_parts, [g2, sv["shn"], sv["scn"]], [gain1, bias1],
            [(d, F32), (d, BF16)], [d, d, d], [d, d], cfg)
        add_mod(i, 5, dg2)
        if i + 1 < nl:
            add_mod(i + 1, 0, dshn)
            add_mod(i + 1, 1, dscn)
        dff = sv["w2"].shape[0]
        da = _mm_dgrad_rows(f"mlp2_dgrad{i}", dm, sv["w2"], tm, min(dff, 512), [sv["r"]], BF16, lambda acc, r: (acc * 2.0 * r,))
        grads["mlp_w2"][i] = _mm_wgrad_rows(f"mlp2_wgrad{i}", sv["p"], dm, min(dff, 512), tm).reshape(N_DEV, -1, d)
        grads["mlp_w1"][i] = _mm_wgrad_cols(f"mlp1_wgrad{i}", sv["h2"], da, N_DEV, tm)
        dh2 = _mm_dgrad_cols(f"mlp1_dgrad{i}", da, sv["w1"], tm)
        if i % 2 == 0:
            bglu = vec(a["s5_b_glu"][j])
            bwd = _vjp_fn(f_sub1_s5, 2, (1, 1), (0, 1, 2, 3, 4, 5, 6, 7))
            dxa, dzz, dg1, dsh2, dsc2, dbglu, dgain0, dbias0 = _rowwise(
                f"sub1_bwd{i}", bwd, [sv["x"], sv["zz"], dx1, dh2], [g1, sh2, sc2], [bglu, gain0, bias0],
                [(d, F32), (2 * d, BF16)], [d, d, d], [2 * d, d, d], cfg)
            grads["s5_b_glu"][j] = dbglu[0]
            grads["s5_w_glu"][j] = _mm_wgrad_cols(f"glu_wgrad{i}", sv["z"], dzz, N_DEV, tm)
            dz = _mm_dgrad_cols(f"glu_dgrad{i}", dzz, sv["w_mix"], tm)
            dsk = vec(a["s5_d"][j])
            bwd = _vjp_fn(_f_gelu, 3, (1,), (0, 1, 3, 4, 5))
            dxb, dy, dsh1, dsc1, ddsk = _rowwise(f"gelu_bwd{i}", bwd, [sv["x"], sv["y"][0], sv["y"][1], dz], [sh1, sc1], [dsk],
                                                 [(d, F32), (d, BF16)], [d, d], [d], cfg)
            grads["s5_d"][j] = ddsk[0]
            add_mod(i, 0, dsh1)
            add_mod(i, 1, dsc1)
            ab_re, ab_im = sv["ab"]
            dus, d_ab_re, d_ab_im, d_bb_re, d_bb_im, d_c_re, d_c_im = [], [], [], [], [], [], []
            for dirn in range(2):
                bmat_t, cmat_t = sv[f"mats{dirn}"]
                tab = _s5_tables(ab_re[dirn], ab_im[dirn], dirn == 0, True, ns)
                du, d_b, d_c, d_a = _s5_bwd(f"s5_bwd{i}_{dirn}", dy, sv[f"states{dirn}"], sv["h"], cmat_t, bmat_t, tab, dirn, cfg)
                dus.append(du)
                sw = d_a.shape[2] // 2
                d_a = jnp.sum(d_a, axis=1)
                d_ab_re.append(d_a[:, :sw].reshape(n_grp, n_state))
                d_ab_im.append(d_a[:, sw:].reshape(n_grp, n_state))

                def unblock_b(t):
                    return jnp.transpose(_diag_blocks(t, gs).reshape(n_grp, ch, n_state), (1, 0, 2))

                def unblock_c(t):
                    return jnp.transpose(_diag_blocks(t, gs).reshape(n_grp, n_state, ch), (0, 2, 1))
                d_bb_re.append(unblock_b(d_b[:, :, :sw]))
                d_bb_im.append(unblock_b(d_b[:, :, sw:]))
                d_c_re.append(unblock_c(d_c[:, :sw]))
                d_c_im.append(-unblock_c(d_c[:, sw:]))
            g_lre, g_lim, g_ldt, g_bre, g_bim = _s5_prep_bwd(
                f"s5_prep_bwd{i}", *sv["prep_in"], (jnp.stack(d_ab_re), jnp.stack(d_ab_im), jnp.stack(d_bb_re), jnp.stack(d_bb_im)))
            grads["s5_lam_re"][j], grads["s5_lam_im"][j], grads["s5_log_dt"][j] = g_lre, g_lim, g_ldt[:, :, 0]
            grads["s5_b_re"][j] = jnp.transpose(g_bre, (0, 2, 3, 1))
            grads["s5_b_im"][j] = jnp.transpose(g_bim, (0, 2, 3, 1))
            grads["s5_c_re"][j], grads["s5_c_im"][j] = jnp.stack(d_c_re), jnp.stack(d_c_im)
            dx_parts, dh_parts = [dxa, dxb], dus
        else:
            bpw2 = vec(full["cv_b_pw2"][j])
            bwd = _vjp_fn(f_sub1_cv, 2, (1, 1), (0, 1, 2, 3, 4, 5, 6, 7))
            dxa, dmm, dg1, dsh2, dsc2, dbpw2, dgain0, dbias0 = _rowwise(
                f"sub1_bwd{i}", bwd, [sv["x"], sv["mm"], dx1, dh2], [g1, sh2, sc2], [bpw2, gain0, bias0],
                [(d, F32), (d, BF16)], [d, d, d], [d, d, d], cfg)
            grads["cv_b_pw2"][j] = dbpw2[0]
            grads["cv_w_pw2"][j] = _mm_wgrad_rows(f"pw2_wgrad{i}", sv["s_act"], dmm, min(d, 512), tm).reshape(N_DEV, -1, d)
            ds = _mm_dgrad_rows(f"pw2_dgrad{i}", dmm, sv["w_pw2"], tm, min(d, 512))
            bdw, lng, lnb = vec(full["cv_b_dw"][j]), vec(full["cv_ln_g"][j]), vec(full["cv_ln_b"][j])
            bwd = _vjp_fn(_f_cvln, 1, (1,), (0, 1, 2, 3))
            dcv, dbdw, dlng, dlnb = _rowwise(f"cvln_bwd{i}", bwd, [sv["cv"], ds], [], [bdw, lng, lnb], [(d, F32)], [], [d, d, d], cfg)
            grads["cv_b_dw"][j], grads["cv_ln_g"][j], grads["cv_ln_b"][j] = dbdw[0], dlng[0], dlnb[0]
            dact = _dwconv(f"dwconv_bwd{i}", dcv, sv["w_dw"][::-1], cfg)
            grads["cv_w_dw"][j] = _dwconv_wgrad(f"dwconv_wgrad{i}", sv["act"], dcv, sv["w_dw"].shape[0], cfg)
            bpw1 = vec(full["cv_b_pw1"][j])
            bwd = _vjp_fn(_f_cvglu, 1, (1,), (0, 1))
            dzz, dbpw1 = _rowwise(f"cvglu_bwd{i}", bwd, [sv["zz"], dact], [], [bpw1], [(2 * d, BF16)], [], [2 * d], cfg)
            grads["cv_b_pw1"][j] = dbpw1[0]
            grads["cv_w_pw1"][j] = _mm_wgrad_cols(f"pw1_wgrad{i}", sv["h"], dzz, N_DEV, tm)
            dh = _mm_dgrad_cols(f"pw1_dgrad{i}", dzz, sv["w_pw1"], tm)
            dx_parts, dh_parts = [dxa], [dh]
        grads["ln_gain"][i] = jnp.stack([dgain0[0], dgain1[0]])
        grads["ln_bias"][i] = jnp.stack([dbias0[0], dbias1[0]])
        add_mod(i, 2, dg1)
        add_mod(i, 3, dsh2)
        add_mod(i, 4, dsc2)
    bwd = _vjp_fn(_f_entry, 2, (len(dx_parts), len(dh_parts)), (0, 2, 3))
    dxc, dsh1, dsc1 = _rowwise("entry_bwd", bwd, [xc, pos_rows] + dx_parts + dh_parts, [seg(0, 0), seg(0, 1)], [],
                               [(d, F32)], [d, d], [], cfg)
    add_mod(0, 0, dsh1)
    add_mod(0, 1, dsc1)
    grad_x = dxc[:nb * seq].reshape(nb, seq, d)

    dmod_loc = jnp.stack([jnp.concatenate([q[:, 0] for q in dmod[i]], axis=1) for i in range(nl)])
    dmod_all = _exchange("gather_dmod", [dmod_loc.reshape(nl * (nb + 1), 6 * d)], False)[0].reshape(N_DEV, nl, nb + 1, 6 * d)
    mine = lax.dynamic_slice(dmod_all, (0, 0, 0, me * n_ada), (N_DEV, nl, nb + 1, n_ada))
    dmod_rows = jnp.transpose(mine[:, :, :nb], (1, 0, 2, 3)).reshape(nl, N_DEV * nb, n_ada)
    dmod_rows = jnp.concatenate([dmod_rows, jnp.zeros((nl, SUBLANES, n_ada), F32)], axis=1)
    g_w_ada, _, dcond = _ada_bwd(cc, a["w_ada"], dmod_rows, mine[:, :, nb:])
    g_b_ada = _sum_lead("b_ada_sum", jnp.transpose(dmod_all, (0, 2, 1, 3)).reshape(N_DEV * (nb + 1), nl, 6 * d), nl)
    dcond_all = _exchange("gather_dcond", [dcond[N_DEV * nb:N_DEV * nb + 1]], False)[0]
    g_c_ctx = _cctx_grad(dcond_all, a["c_ctx"][None])[0]

    big_grads = [g for n in BIG for g in grads[n]]
    big_recv = _exchange("scatter_grads", big_grads, True)
    small_names = SHARDED_SMALL + REPLICATED_SMALL
    small_full = [jnp.stack(grads[n]) for n in small_names]
    small_recv = _exchange("gather_small_grads", [_pack(small_full)], False)[0]
    small_sum = _sum_lead("small_grad_sum", small_recv, _row_tile(small_recv.shape[1], 512))
    small_g = dict(zip(small_names, _unpack(small_sum, [g.shape for g in small_full])))
    for n in SHARDED_SMALL:
        width = a[n].shape[-1]
        start = (0,) * (small_g[n].ndim - 1) + (me * width,)
        small_g[n] = lax.dynamic_slice(small_g[n], start, a[n].shape)
    small_g["c_ctx"], small_g["b_ada"] = g_c_ctx, g_b_ada

    out = {}

    def update(n, parts):
        shp = a[n].shape
        cols = parts.shape[-1]
        rows = parts.shape[1]
        res = _adamw(f"adamw_{n}", parts, a[n].reshape(rows, cols), a["m_" + n].reshape(rows, cols), a["v_" + n].reshape(rows, cols),
                     _row_tile(rows, max(SUBLANES, 131072 // cols)))
        out[n] = [r.reshape(shp) for r in res]

    pos_big = 0
    for n in BIG:
        count = a[n].shape[0]
        per_layer = big_recv[pos_big:pos_big + count]
        pos_big += count
        parts = jnp.stack(per_layer, axis=1)
        update(n, parts.reshape(N_DEV, -1, parts.shape[-1]))
    update("w_ada", g_w_ada.reshape(1, -1, n_ada))
    small_all_names = ("c_ctx", "b_ada") + small_names
    packed = [_pack([src[n] for n in small_all_names]) for src in
              (small_g, a, {n: a["m_" + n] for n in small_all_names}, {n: a["v_" + n] for n in small_all_names})]
    res = _adamw("adamw_small", packed[0][None], packed[1], packed[2], packed[3], _row_tile(packed[0].shape[0], 512))
    shapes = [a[n].shape for n in small_all_names]
    for n, vals in zip(small_all_names, zip(*[_unpack(r, shapes) for r in res])):
        out[n] = list(vals)
    return (loss, grad_x, *[out[n][0] for n in WEIGHTS], *[out[n][1] for n in WEIGHTS],
            *[out[n][2] for n in WEIGHTS], *[out[n][3] for n in WEIGHTS])


def kernel(x, c, ctx, c_ctx, w_ada, b_ada, ln_gain, ln_bias, s5_lam_re, s5_lam_im, s5_log_dt, s5_b_re, s5_b_im, s5_c_re, s5_c_im, s5_d, s5_w_glu, s5_b_glu, cv_w_pw1, cv_b_pw1, cv_w_dw, cv_b_dw, cv_ln_g, cv_ln_b, cv_w_pw2, cv_b_pw2, mlp_w1, mlp_w2, loss_target, m_c_ctx, m_w_ada, m_b_ada, m_ln_gain, m_ln_bias, m_s5_lam_re, m_s5_lam_im, m_s5_log_dt, m_s5_b_re, m_s5_b_im, m_s5_c_re, m_s5_c_im, m_s5_d, m_s5_w_glu, m_s5_b_glu, m_cv_w_pw1, m_cv_b_pw1, m_cv_w_dw, m_cv_b_dw, m_cv_ln_g, m_cv_ln_b, m_cv_w_pw2, m_cv_b_pw2, m_mlp_w1, m_mlp_w2, v_c_ctx, v_w_ada, v_b_ada, v_ln_gain, v_ln_bias, v_s5_lam_re, v_s5_lam_im, v_s5_log_dt, v_s5_b_re, v_s5_b_im, v_s5_c_re, v_s5_c_im, v_s5_d, v_s5_w_glu, v_s5_b_glu, v_cv_w_pw1, v_cv_b_pw1, v_cv_w_dw, v_cv_b_dw, v_cv_ln_g, v_cv_ln_b, v_cv_w_pw2, v_cv_b_pw2, v_mlp_w1, v_mlp_w2):
    return _step(dict(locals()))
```

```python
import functools
import math

import jax
import jax.numpy as jnp
from jax import lax
from jax.experimental import pallas as pl
from jax.experimental.pallas import tpu as pltpu

F32 = jnp.float32
BF16 = jnp.bfloat16
N_DEV = 8
LANES = 128
SUBLANES = 8
VMEM_LIMIT = 56 * 1024 * 1024
GRID_W = 64
POS_TEMP = 10000.0
LN_EPS = 1e-5
LAMBDA_RE_MAX = -1e-4
ADAM_LR, ADAM_B1, ADAM_B2, ADAM_EPS, ADAM_WD, ADAM_STEP = 0.001, 0.9, 0.999, 1e-08, 0.01, 10
MESH = pl.DeviceIdType.MESH


def _params(sem):
    return pltpu.CompilerParams(dimension_semantics=sem, vmem_limit_bytes=VMEM_LIMIT)


def _accumulate(ref, val, first):
    @pl.when(first)
    def _():
        ref[...] = val

    @pl.when(jnp.logical_not(first))
    def _():
        ref[...] += val


def _rowwise(name, fn, rows, segs, vecs, row_outs, seg_accs, vec_accs, cfg):
    tr, tpl, nb = cfg["tr"], cfg["tpl"], cfg["nb"]
    n_rows = rows[0].shape[0]
    nt = n_rows // tr
    nr, ns, nv = len(rows), len(segs), len(vecs)
    nro, nsa = len(row_outs), len(seg_accs)

    def seg_of(t):
        return jnp.minimum(t // tpl, nb)

    def body(*refs):
        t = pl.program_id(0)
        ins, outs = refs[:nr + ns + nv], refs[nr + ns + nv:]
        vals = [r[...] for r in ins[:nr]] + [r[0] for r in ins[nr:nr + ns]] + [r[...] for r in ins[nr + ns:]]
        res = fn(*vals)
        for o, v in zip(outs[:nro], res[:nro]):
            o[...] = v.astype(o.dtype)
        first_seg = jnp.logical_or(t == 0, seg_of(t) != seg_of(jnp.maximum(t - 1, 0)))
        for o, v in zip(outs[nro:nro + nsa], res[nro:nro + nsa]):
            _accumulate(o.at[0], v, first_seg)
        for o, v in zip(outs[nro + nsa:], res[nro + nsa:]):
            _accumulate(o, v, t == 0)

    in_specs = ([pl.BlockSpec((tr, a.shape[1]), lambda t: (t, 0)) for a in rows]
                + [pl.BlockSpec((1, 1, a.shape[2]), lambda t: (seg_of(t), 0, 0)) for a in segs]
                + [pl.BlockSpec((1, a.shape[1]), lambda t: (0, 0)) for a in vecs])
    out_specs = ([pl.BlockSpec((tr, c), lambda t: (t, 0)) for c, _ in row_outs]
                 + [pl.BlockSpec((1, 1, c), lambda t: (seg_of(t), 0, 0)) for c in seg_accs]
                 + [pl.BlockSpec((1, c), lambda t: (0, 0)) for c in vec_accs])
    out_shape = ([jax.ShapeDtypeStruct((n_rows, c), dt) for c, dt in row_outs]
                 + [jax.ShapeDtypeStruct((nb + 1, 1, c), F32) for c in seg_accs]
                 + [jax.ShapeDtypeStruct((1, c), F32) for c in vec_accs])
    return pl.pallas_call(body, name=name, grid=(nt,), in_specs=in_specs, out_specs=out_specs,
                          out_shape=out_shape, compiler_params=_params(("arbitrary",)))(*rows, *segs, *vecs)


def _vjp_fn(fn, n_row, cot_groups, want):
    n_cot = sum(cot_groups)

    def bwd(*args):
        primals = [a.astype(F32) for a in args[:n_row] + args[n_row + n_cot:]]
        outs, vjp = jax.vjp(fn, *primals)
        cots, pos = [], n_row
        for n, o in zip(cot_groups, outs):
            cot = jnp.zeros_like(o)
            for part in args[pos:pos + n]:
                cot = cot + part.astype(F32)
            cots.append(cot)
            pos += n
        grads = vjp(tuple(cots))
        return tuple(grads[i] for i in want)
    return bwd


def _ln(r, g, b):
    mu = jnp.mean(r, axis=-1, keepdims=True)
    var = jnp.mean(jnp.square(r - mu), axis=-1, keepdims=True)
    return (r - mu) * lax.rsqrt(var + LN_EPS) * g + b


def _glu(zz, bias):
    d = zz.shape[1] // 2
    return (zz[:, :d] + bias[:, :d]) * jax.nn.sigmoid(zz[:, d:] + bias[:, d:])


def _f_entry(xc, pos, sh, sc):
    x0 = xc + pos
    return x0, x0 * (1 + sc) + sh


def _f_gelu(x, y0, y1, sh, sc, dsk):
    u = x * (1 + sc) + sh
    y = dsk * u + y0 + y1
    return (0.5 * y * (1.0 + lax.erf(y * (2.0 ** -0.5))),)


def _make_sub1_s5(alpha):
    def f(x, zz, g1, sh2, sc2, bglu, gain, bias):
        x1 = _ln(alpha * x + g1 * _glu(zz, bglu), gain, bias)
        return x1, x1 * (1 + sc2) + sh2
    return f


def _make_sub1_cv(alpha):
    def f(x, mm, g1, sh2, sc2, bpw2, gain, bias):
        x1 = _ln(alpha * x + g1 * (mm + bpw2), gain, bias)
        return x1, x1 * (1 + sc2) + sh2
    return f


def _make_sub2(alpha):
    def f(x1, m, g2, shn, scn, gain, bias):
        x2 = _ln(alpha * x1 + g2 * m, gain, bias)
        return x2, x2 * (1 + scn) + shn
    return f


def _f_cvglu(zz, bpw1):
    return (_glu(zz, bpw1),)


def _f_cvln(cv, bdw, lng, lnb):
    return (jax.nn.silu(_ln(cv + bdw, lng, lnb)),)


def _matmul(name, a, b, extras, grid, a_spec, b_spec, extra_specs, o_specs, out_shape, dims, red_axis, epi, sem, aliases=None):
    n_extra = len(extras)
    n_out = len(out_shape)
    acc_shape = o_specs[0].block_shape
    acc_shape = tuple(s for s in acc_shape if s is not None)

    def body(*refs):
        a_ref, b_ref = refs[0], refs[1]
        ex = refs[2:2 + n_extra]
        outs = refs[2 + n_extra:2 + n_extra + n_out]
        prod = lax.dot_general(a_ref[...], b_ref[...], dims, preferred_element_type=F32)

        def finish(acc):
            res = epi(acc, *[e[...] for e in (() if aliases else ex)]) if epi is not None else (acc,)
            for o, v in zip(outs, res):
                o[...] = v.astype(o.dtype)

        if red_axis is None:
            finish(prod)
        else:
            acc_ref = refs[-1]
            k = pl.program_id(red_axis)
            nk = pl.num_programs(red_axis)

            @pl.when(k == 0)
            def _():
                acc_ref[...] = prod

            @pl.when(k > 0)
            def _():
                acc_ref[...] += prod

            @pl.when(k == nk - 1)
            def _():
                finish(acc_ref[...])

    scratch = [] if red_axis is None else [pltpu.VMEM(acc_shape, F32)]
    res = pl.pallas_call(body, name=name, grid=grid, in_specs=[a_spec, b_spec] + list(extra_specs),
                         out_specs=list(o_specs), out_shape=list(out_shape), scratch_shapes=scratch,
                         input_output_aliases=aliases or {}, compiler_params=_params(sem))(a, b, *extras)
    return res


NN = (((1,), (0,)), ((), ()))
NT = (((1,), (1,)), ((), ()))
TN = (((0,), (0,)), ((), ()))


def _mm_nn(name, a, w3, layer, tm, tn, out_dtypes=(F32,), epi=None):
    m, k = a.shape
    n = w3.shape[2]
    return _matmul(name, a, w3, (), (n // tn, m // tm),
                   pl.BlockSpec((tm, k), lambda j, i: (i, 0)), pl.BlockSpec((None, k, tn), lambda j, i: (layer, 0, j)), (),
                   [pl.BlockSpec((tm, tn), lambda j, i: (i, j)) for _ in out_dtypes],
                   [jax.ShapeDtypeStruct((m, n), dt) for dt in out_dtypes], NN, None, epi, ("arbitrary", "arbitrary"))


def _mm_nt(name, dy, w3, layer, tm, tkw, extras=(), out_dtype=F32, epi=None):
    m, n = dy.shape
    kw = w3.shape[1]
    return _matmul(name, dy, w3, tuple(extras), (kw // tkw, m // tm),
                   pl.BlockSpec((tm, n), lambda j, i: (i, 0)), pl.BlockSpec((None, tkw, n), lambda j, i: (layer, j, 0)),
                   [pl.BlockSpec((tm, tkw), lambda j, i: (i, j)) for _ in extras],
                   [pl.BlockSpec((tm, tkw), lambda j, i: (i, j))], [jax.ShapeDtypeStruct((m, kw), out_dtype)], NT, None, epi,
                   ("arbitrary", "arbitrary"))[0]


def _mm_wgrad_cols(name, a, dy, buf, layer, tm):
    m, k = a.shape
    nblk, _, _, n = buf.shape
    return _matmul(name, a, dy, (buf,), (nblk, m // tm),
                   pl.BlockSpec((tm, k), lambda j, i: (i, 0)), pl.BlockSpec((tm, n), lambda j, i: (i, j)),
                   [pl.BlockSpec(memory_space=pl.ANY)],
                   [pl.BlockSpec((None, None, k, n), lambda j, i: (j, layer, 0, 0))], [jax.ShapeDtypeStruct(buf.shape, buf.dtype)],
                   TN, 1, None, ("arbitrary", "arbitrary"), {2: 0})[0]


def _mm_wgrad_rows(name, a, dy, buf, layer, tm):
    m = a.shape[0]
    nblk, _, r, n = buf.shape
    return _matmul(name, a, dy, (buf,), (nblk, m // tm),
                   pl.BlockSpec((tm, r), lambda j, i: (i, j)), pl.BlockSpec((tm, n), lambda j, i: (i, 0)),
                   [pl.BlockSpec(memory_space=pl.ANY)],
                   [pl.BlockSpec((None, None, r, n), lambda j, i: (j, layer, 0, 0))], [jax.ShapeDtypeStruct(buf.shape, buf.dtype)],
                   TN, 1, None, ("arbitrary", "arbitrary"), {2: 0})[0]


def _exchange(name, arrays, kinds):
    n = len(arrays)

    def body(*refs):
        ins, outs = refs[:n], refs[n:2 * n]
        send_sems, recv_sems, local_sems = refs[2 * n:]
        x, y, c = lax.axis_index("x"), lax.axis_index("y"), lax.axis_index("c")
        me = 4 * x + 2 * y + c

        def landing(i):
            if kinds[i] in ("slot", "scatter"):
                return outs[i].at[me]
            size = ins[i].shape[kinds[i]]
            mine = pl.ds(pl.multiple_of(me * size, size), size)
            return outs[i].at[:, mine] if kinds[i] == 1 else outs[i].at[:, :, mine]

        copies = []
        for i in range(n):
            scatter = kinds[i] == "scatter"
            local = pltpu.make_async_copy(ins[i].at[me] if scatter else ins[i], landing(i), local_sems.at[i])
            local.start()
            copies.append(local)
            for k in range(1, N_DEV):
                px = 1 - x if k & 4 else x
                py = 1 - y if k & 2 else y
                pc = 1 - c if k & 1 else c
                src = ins[i].at[4 * px + 2 * py + pc] if scatter else ins[i]
                cp = pltpu.make_async_remote_copy(src_ref=src, dst_ref=landing(i), send_sem=send_sems.at[i, k - 1],
                                                  recv_sem=recv_sems.at[i, k - 1], device_id=(px, py, pc), device_id_type=MESH)
                cp.start()
                copies.append(cp)
        for cp in copies:
            cp.wait()

    def result(a, kind):
        if kind == "slot":
            return (N_DEV,) + a.shape
        if kind == "scatter":
            return a.shape
        return a.shape[:kind] + (N_DEV * a.shape[kind],) + a.shape[kind + 1:]

    any_spec = pl.BlockSpec(memory_space=pl.ANY)
    out_shape = [jax.ShapeDtypeStruct(result(a, kind), a.dtype) for a, kind in zip(arrays, kinds)]
    return pl.pallas_call(body, name=name, in_specs=[any_spec] * n, out_specs=[any_spec] * n, out_shape=out_shape,
                          scratch_shapes=[pltpu.SemaphoreType.DMA((n, N_DEV - 1)), pltpu.SemaphoreType.DMA((n, N_DEV - 1)),
                                          pltpu.SemaphoreType.DMA((n,))])(*arrays)


def _sum_lead(name, parts, tr):
    npart, r, c = parts.shape

    def body(p_ref, o_ref):
        acc = p_ref[0].astype(F32)
        for p in range(1, npart):
            acc = acc + p_ref[p].astype(F32)
        o_ref[...] = acc

    return pl.pallas_call(body, name=name, grid=(r // tr,), in_specs=[pl.BlockSpec((npart, tr, c), lambda i: (0, i, 0))],
                          out_specs=pl.BlockSpec((tr, c), lambda i: (i, 0)), out_shape=jax.ShapeDtypeStruct((r, c), F32),
                          compiler_params=_params(("arbitrary",)))(parts)


def _adamw(name, parts, w, m, v, tr):
    npart, r, c = parts.shape

    def body(p_ref, w_ref, m_ref, v_ref, g_out, d_out, m_out, v_out):
        g = p_ref[0].astype(F32)
        for p in range(1, npart):
            g = g + p_ref[p].astype(F32)
        m2 = ADAM_B1 * m_ref[...] + (1.0 - ADAM_B1) * g
        v2 = ADAM_B2 * v_ref[...] + (1.0 - ADAM_B2) * jnp.square(g)
        m_hat = m2 / (1.0 - ADAM_B1 ** ADAM_STEP)
        v_hat = v2 / (1.0 - ADAM_B2 ** ADAM_STEP)
        g_out[...] = g
        d_out[...] = -ADAM_LR * (m_hat / (jnp.sqrt(v_hat) + ADAM_EPS) + ADAM_WD * w_ref[...])
        m_out[...] = m2
        v_out[...] = v2

    row = pl.BlockSpec((tr, c), lambda i: (i, 0))
    return pl.pallas_call(body, name=name, grid=(r // tr,),
                          in_specs=[pl.BlockSpec((npart, tr, c), lambda i: (0, i, 0)), row, row, row],
                          out_specs=[row] * 4, out_shape=[jax.ShapeDtypeStruct((r, c), F32)] * 4,
                          compiler_params=_params(("arbitrary",)))(parts, w, m, v)


def _row_tile(r, cap):
    if r <= cap:
        return r
    t = cap - cap % SUBLANES
    while r % t:
        t -= SUBLANES
    return t


def _ada_fwd(cc, w_ada, b_loc):
    nl, d, n = w_ada.shape
    rows = cc.shape[0]

    def body(c_ref, w_ref, b_ref, o_ref):
        cond = jax.nn.silu(c_ref[...]).astype(BF16)
        o_ref[...] = jnp.dot(cond, w_ref[...].astype(BF16), preferred_element_type=F32) + b_ref[...]

    return pl.pallas_call(body, name="ada_fwd", grid=(nl,),
                          in_specs=[pl.BlockSpec((rows, d), lambda i: (0, 0)), pl.BlockSpec((None, d, n), lambda i: (i, 0, 0)),
                                    pl.BlockSpec((None, 1, n), lambda i: (i, 0, 0))],
                          out_specs=pl.BlockSpec((None, rows, n), lambda i: (i, 0, 0)),
                          out_shape=jax.ShapeDtypeStruct((nl, rows, n), F32), compiler_params=_params(("arbitrary",)))(cc, w_ada, b_loc)


def _ada_bwd(cc, w_ada, dmod_rows, dmod_ctx):
    nl, d, n = w_ada.shape
    rows = cc.shape[0]
    ctx_row = rows - SUBLANES

    def body(c_ref, w_ref, dr_ref, dc_ref, gw_ref, tot_ref, dcond_ref):
        i = pl.program_id(0)
        total = dc_ref[0]
        for p in range(1, N_DEV):
            total = total + dc_ref[p]
        tot_ref[...] = total
        row_id = lax.broadcasted_iota(jnp.int32, (rows, n), 0)
        dm = jnp.where(row_id == ctx_row, jnp.broadcast_to(total, (rows, n)), dr_ref[...]).astype(BF16)
        cond = jax.nn.silu(c_ref[...]).astype(BF16)
        gw_ref[...] = lax.dot_general(cond, dm, TN, preferred_element_type=F32)
        part = lax.dot_general(dm, w_ref[...].astype(BF16), NT, preferred_element_type=F32)
        _accumulate(dcond_ref, part, i == 0)

    return pl.pallas_call(body, name="ada_bwd", grid=(nl,),
                          in_specs=[pl.BlockSpec((rows, d), lambda i: (0, 0)), pl.BlockSpec((None, d, n), lambda i: (i, 0, 0)),
                                    pl.BlockSpec((None, rows, n), lambda i: (i, 0, 0)),
                                    pl.BlockSpec((N_DEV, None, 1, n), lambda i: (0, i, 0, 0))],
                          out_specs=[pl.BlockSpec((None, d, n), lambda i: (i, 0, 0)), pl.BlockSpec((None, 1, n), lambda i: (i, 0, 0)),
                                     pl.BlockSpec((rows, d), lambda i: (0, 0))],
                          out_shape=[jax.ShapeDtypeStruct((nl, d, n), F32), jax.ShapeDtypeStruct((nl, 1, n), F32),
                                     jax.ShapeDtypeStruct((rows, d), F32)],
                          compiler_params=_params(("arbitrary",)))(cc, w_ada, dmod_rows, dmod_ctx)


def _cctx_grad(parts, c_ctx):
    def body(p_ref, c_ref, o_ref):
        tot = p_ref[0]
        for p in range(1, N_DEV):
            tot = tot + p_ref[p]
        _, vjp = jax.vjp(jax.nn.silu, c_ref[...])
        o_ref[...] = vjp(tot)[0]

    return pl.pallas_call(body, name="cctx_grad", out_shape=jax.ShapeDtypeStruct(c_ctx.shape, F32))(parts, c_ctx)


def _discretise(lam_re, lam_im, log_dt, b_re, b_im):
    lr = jnp.minimum(lam_re, LAMBDA_RE_MAX)
    li = lam_im
    dt = jnp.exp(log_dt)
    mag = jnp.exp(lr * dt)
    ab_re = mag * jnp.cos(li * dt)
    ab_im = mag * jnp.sin(li * dt)
    den = lr * lr + li * li
    nr = ab_re - 1.0
    ni = ab_im
    coef_re = ((nr * lr + ni * li) / den)[:, None]
    coef_im = ((ni * lr - nr * li) / den)[:, None]
    bb_re = coef_re * b_re - coef_im * b_im
    bb_im = coef_re * b_im + coef_im * b_re
    return ab_re, ab_im, bb_re, bb_im


def _s5_prep(name, lam_re, lam_im, log_dt, b_re, b_im):
    def body(a, b, c, d, e, o1, o2, o3, o4):
        res = _discretise(a[...], b[...], c[...], d[...], e[...])
        for o, v in zip((o1, o2, o3, o4), res):
            o[...] = v

    shp = [jax.ShapeDtypeStruct(lam_re.shape, F32)] * 2 + [jax.ShapeDtypeStruct(b_re.shape, F32)] * 2
    return pl.pallas_call(body, name=name, out_shape=shp)(lam_re, lam_im, log_dt, b_re, b_im)


def _s5_prep_bwd(name, lam_re, lam_im, log_dt, b_re, b_im, cots):
    def body(a, b, c, d, e, c1, c2, c3, c4, o1, o2, o3, o4, o5):
        _, vjp = jax.vjp(_discretise, a[...], b[...], c[...], d[...], e[...])
        grads = vjp((c1[...], c2[...], c3[...], c4[...]))
        for o, v in zip((o1, o2, o3, o4, o5), grads):
            o[...] = v

    shp = [jax.ShapeDtypeStruct(a.shape, F32) for a in (lam_re, lam_im, log_dt, b_re, b_im)]
    return pl.pallas_call(body, name=name, out_shape=shp)(lam_re, lam_im, log_dt, b_re, b_im, *cots)


def _scan_tile(h_ref, t_ref, carry_ref, up, hook=None, hook_init=None):
    sw = h_ref.shape[1] // 2
    ng = h_ref.shape[0] // SUBLANES

    def tab(i):
        return t_ref[SUBLANES * i:SUBLANES * (i + 1), :sw], t_ref[SUBLANES * i:SUBLANES * (i + 1), sw:]

    steps = ((1, tab(0)), (2, tab(1)), (4, tab(2)))
    p_re, p_im = tab(3)
    edge = 0 if up else SUBLANES - 1

    def group(j, state):
        carry, extra = state
        g = ng - 1 - j if up else j
        r0 = pl.multiple_of(g * SUBLANES, SUBLANES)
        xr = h_ref[pl.ds(r0, SUBLANES), :sw]
        xi = h_ref[pl.ds(r0, SUBLANES), sw:]
        for sh, (a_re, a_im) in steps:
            amount = SUBLANES - sh if up else sh
            sr = pltpu.roll(xr, amount, 0)
            si = pltpu.roll(xi, amount, 0)
            xr, xi = xr + a_re * sr - a_im * si, xi + a_re * si + a_im * sr
        cr, ci = carry
        xr, xi = xr + p_re * cr - p_im * ci, xi + p_re * ci + p_im * cr
        h_ref[pl.ds(r0, SUBLANES), :sw] = xr
        h_ref[pl.ds(r0, SUBLANES), sw:] = xi
        if hook is not None:
            extra = hook(r0, xr, xi, cr, ci, extra)
        new_carry = (jnp.broadcast_to(xr[edge:edge + 1], xr.shape), jnp.broadcast_to(xi[edge:edge + 1], xi.shape))
        return new_carry, extra

    carry0 = (carry_ref[:, :sw], carry_ref[:, sw:])
    carry, extra = lax.fori_loop(0, ng, group, (carry0, hook_init))
    carry_ref[:, :sw] = carry[0]
    carry_ref[:, sw:] = carry[1]
    return extra


def _s5_tile_index(cfg, dirn, adjoint):
    tpl, nb = cfg["tpl"], cfg["nb"]

    def idx(b, k):
        if not adjoint:
            lat = b * tpl + (k - 1 if dirn == 0 else tpl - k)
            return jnp.where(k == 0, nb * tpl + b, lat)
        lat = b * tpl + (tpl - 1 - k if dirn == 0 else k)
        return jnp.where(k == tpl, nb * tpl + b, lat)
    return idx


def _s5_fwd(name, u, bmat, cmat, tab, dirn, cfg):
    tr, tpl, nb = cfg["tr"], cfg["tpl"], cfg["nb"]
    n_rows, d = u.shape
    ns, _, sw2 = bmat.shape
    tile = _s5_tile_index(cfg, dirn, False)
    up = dirn == 1

    def body(u_ref, b_ref, c_ref, t_ref, h_ref, y_ref, carry_ref):
        @pl.when(pl.program_id(2) == 0)
        def _():
            carry_ref[...] = jnp.zeros_like(carry_ref)

        h_ref[...] = jnp.dot(u_ref[...], b_ref[...], preferred_element_type=F32)
        _scan_tile(h_ref, t_ref, carry_ref, up)
        y_ref[...] = jnp.dot(h_ref[...].astype(BF16), c_ref[...], preferred_element_type=F32)

    return pl.pallas_call(
        body, name=name, grid=(ns, nb, tpl + 1),
        in_specs=[pl.BlockSpec((tr, LANES), lambda s, b, k: (tile(b, k), s)),
                  pl.BlockSpec((None, LANES, sw2), lambda s, b, k: (s, 0, 0)),
                  pl.BlockSpec((None, sw2, LANES), lambda s, b, k: (s, 0, 0)),
                  pl.BlockSpec((None, 4 * SUBLANES, sw2), lambda s, b, k: (s, 0, 0))],
        out_specs=[pl.BlockSpec((tr, sw2), lambda s, b, k: (tile(b, k), s)),
                   pl.BlockSpec((tr, LANES), lambda s, b, k: (tile(b, k), s))],
        out_shape=[jax.ShapeDtypeStruct((n_rows, ns * sw2), F32), jax.ShapeDtypeStruct((n_rows, d), F32)],
        scratch_shapes=[pltpu.VMEM((SUBLANES, sw2), F32)],
        compiler_params=_params(("arbitrary", "arbitrary", "arbitrary")))(u, bmat, cmat, tab)


def _s5_bwd(name, dy, h, u, cmat_t, bmat_t, tab, dirn, cfg):
    tr, tpl, nb = cfg["tr"], cfg["tpl"], cfg["nb"]
    n_rows, d = u.shape
    ns, _, sw2 = cmat_t.shape
    sw = sw2 // 2
    tile = _s5_tile_index(cfg, dirn, True)
    up = dirn == 0
    edge = SUBLANES - 1 if up else 0
    one = SUBLANES - 1 if up else 1

    def body(dy_ref, h_ref, u_ref, ct_ref, bt_ref, t_ref, du_ref, db_ref, dc_ref, da_ref, lam_ref, carry_ref):
        first = jnp.logical_and(pl.program_id(1) == 0, pl.program_id(2) == 0)

        @pl.when(pl.program_id(2) == 0)
        def _():
            carry_ref[...] = jnp.zeros_like(carry_ref)

        lam_ref[...] = jnp.dot(dy_ref[...], ct_ref[...], preferred_element_type=F32)
        row_id = lax.broadcasted_iota(jnp.int32, (SUBLANES, sw), 0)

        def hook(r0, xr, xi, cr, ci, acc):
            lr = jnp.where(row_id == edge, cr, pltpu.roll(xr, one, 0))
            li = jnp.where(row_id == edge, ci, pltpu.roll(xi, one, 0))
            hr = h_ref[pl.ds(r0, SUBLANES), :sw]
            hi = h_ref[pl.ds(r0, SUBLANES), sw:]
            return acc[0] + lr * hr + li * hi, acc[1] + li * hr - lr * hi

        zero = jnp.zeros((SUBLANES, sw), F32)
        acc = _scan_tile(lam_ref, t_ref, carry_ref, up, hook, (zero, zero))
        lam = lam_ref[...].astype(BF16)
        d_b = lax.dot_general(u_ref[...], lam, TN, preferred_element_type=F32)
        d_c = lax.dot_general(h_ref[...].astype(BF16), dy_ref[...], TN, preferred_element_type=F32)
        du_ref[...] = jnp.dot(lam, bt_ref[...], preferred_element_type=F32)

        @pl.when(first)
        def _():
            db_ref[...] = d_b
            dc_ref[...] = d_c
            da_ref[:, :sw] = acc[0]
            da_ref[:, sw:] = acc[1]

        @pl.when(jnp.logical_not(first))
        def _():
            db_ref[...] += d_b
            dc_ref[...] += d_c
            da_ref[:, :sw] += acc[0]
            da_ref[:, sw:] += acc[1]

    return pl.pallas_call(
        body, name=name, grid=(ns, nb, tpl + 1),
        in_specs=[pl.BlockSpec((tr, LANES), lambda s, b, k: (tile(b, k), s)),
                  pl.BlockSpec((tr, sw2), lambda s, b, k: (tile(b, k), s)),
                  pl.BlockSpec((tr, LANES), lambda s, b, k: (tile(b, k), s)),
                  pl.BlockSpec((None, LANES, sw2), lambda s, b, k: (s, 0, 0)),
                  pl.BlockSpec((None, sw2, LANES), lambda s, b, k: (s, 0, 0)),
                  pl.BlockSpec((None, 4 * SUBLANES, sw2), lambda s, b, k: (s, 0, 0))],
        out_specs=[pl.BlockSpec((tr, LANES), lambda s, b, k: (tile(b, k), s)),
                   pl.BlockSpec((None, LANES, sw2), lambda s, b, k: (s, 0, 0)),
                   pl.BlockSpec((None, sw2, LANES), lambda s, b, k: (s, 0, 0)),
                   pl.BlockSpec((None, SUBLANES, sw2), lambda s, b, k: (s, 0, 0))],
        out_shape=[jax.ShapeDtypeStruct((n_rows, d), F32), jax.ShapeDtypeStruct((ns, LANES, sw2), F32),
                   jax.ShapeDtypeStruct((ns, sw2, LANES), F32), jax.ShapeDtypeStruct((ns, SUBLANES, sw2), F32)],
        scratch_shapes=[pltpu.VMEM((tr, sw2), F32), pltpu.VMEM((SUBLANES, sw2), F32)],
        compiler_params=_params(("arbitrary", "arbitrary", "arbitrary")))(dy, h, u, cmat_t, bmat_t, tab)


def _s5_tables(ab_re, ab_im, up, conj, ns):
    a_re = ab_re.reshape(ns, -1)
    a_im = (-ab_im if conj else ab_im).reshape(ns, -1)
    powers = [(a_re, a_im)]
    for _ in range(SUBLANES - 1):
        q_re, q_im = powers[-1]
        powers.append((q_re * a_re - q_im * a_im, q_re * a_im + q_im * a_re))
    rows = jnp.arange(SUBLANES)
    re_blocks, im_blocks = [], []
    for sh in (1, 2, 4):
        keep = (rows <= SUBLANES - 1 - sh) if up else (rows >= sh)
        q_re, q_im = powers[sh - 1]
        re_blocks.append(jnp.where(keep[None, :, None], q_re[:, None, :], 0.0))
        im_blocks.append(jnp.where(keep[None, :, None], q_im[:, None, :], 0.0))
    dist = range(SUBLANES, 0, -1) if up else range(1, SUBLANES + 1)
    re_blocks.append(jnp.stack([powers[dd - 1][0] for dd in dist], axis=1))
    im_blocks.append(jnp.stack([powers[dd - 1][1] for dd in dist], axis=1))
    return jnp.concatenate([jnp.concatenate(re_blocks, axis=1), jnp.concatenate(im_blocks, axis=1)], axis=2)


def _block_diag(blocks):
    ns, gs, a, b = blocks.shape
    eye = jnp.eye(gs, dtype=blocks.dtype)
    return (blocks[:, :, :, None, :] * eye[None, :, None, :, None]).reshape(ns, gs * a, gs * b)


def _diag_blocks(mat, gs):
    ns, ra, rb = mat.shape
    a, b = ra // gs, rb // gs
    m5 = mat.reshape(ns, gs, a, gs, b)
    eye = jnp.eye(gs, dtype=mat.dtype)
    return jnp.sum(m5 * eye[None, :, None, :, None], axis=3)


def _conv_flags(t, cfg):
    tpl, nb = cfg["tpl"], cfg["nb"]
    latent = t < nb * tpl
    first = jnp.logical_or(jnp.logical_not(latent), t % tpl == 0)
    last = jnp.logical_or(jnp.logical_not(latent), t % tpl == tpl - 1)
    return first, last


def _fill_ext(ext_ref, prev_ref, cur_ref, next_ref, t, cfg, halo):
    first, last = _conv_flags(t, cfg)
    tr = cur_ref.shape[0]
    ext_ref[0:halo, :] = jnp.where(first, 0.0, prev_ref[...])
    ext_ref[halo:halo + tr, :] = cur_ref[...]
    ext_ref[halo + tr:, :] = jnp.where(last, 0.0, next_ref[...])


def _conv_specs(tr, n_rows, halo):
    per = tr // halo
    n_halo = n_rows // halo
    return [pl.BlockSpec((halo, LANES), lambda c, t: (jnp.maximum(t * per - 1, 0), c)),
            pl.BlockSpec((tr, LANES), lambda c, t: (t, c)),
            pl.BlockSpec((halo, LANES), lambda c, t: (jnp.minimum((t + 1) * per, n_halo - 1), c))]


def _dwconv(name, a, w, cfg):
    tr = cfg["tr"]
    n_rows, d = a.shape
    kw = w.shape[0]
    half = kw // 2
    halo = 2 * SUBLANES

    def body(prev_ref, cur_ref, next_ref, w_ref, o_ref, ext_ref):
        _fill_ext(ext_ref, prev_ref, cur_ref, next_ref, pl.program_id(1), cfg, halo)
        acc = jnp.zeros((tr, LANES), F32)
        for k in range(kw):
            acc = acc + ext_ref[pl.ds(halo - half + k, tr), :] * w_ref[k:k + 1, :]
        o_ref[...] = acc

    return pl.pallas_call(body, name=name, grid=(d // LANES, n_rows // tr),
                          in_specs=_conv_specs(tr, n_rows, halo) + [pl.BlockSpec((kw, LANES), lambda c, t: (0, c))],
                          out_specs=pl.BlockSpec((tr, LANES), lambda c, t: (t, c)),
                          out_shape=jax.ShapeDtypeStruct((n_rows, d), F32),
                          scratch_shapes=[pltpu.VMEM((tr + 2 * halo, LANES), F32)],
                          compiler_params=_params(("arbitrary", "arbitrary")))(a, a, a, w)


def _dwconv_wgrad(name, a, dout, kw, cfg):
    tr = cfg["tr"]
    n_rows, d = a.shape
    half = kw // 2
    halo = 2 * SUBLANES

    def body(prev_ref, cur_ref, next_ref, do_ref, o_ref, ext_ref):
        t = pl.program_id(1)
        _fill_ext(ext_ref, prev_ref, cur_ref, next_ref, t, cfg, halo)
        dout_t = do_ref[...]
        rows = [jnp.sum(ext_ref[pl.ds(halo - half + k, tr), :] * dout_t, axis=0, keepdims=True) for k in range(kw)]
        part = jnp.concatenate(rows, axis=0)
        _accumulate(o_ref, part, t == 0)

    return pl.pallas_call(body, name=name, grid=(d // LANES, n_rows // tr),
                          in_specs=_conv_specs(tr, n_rows, halo) + [pl.BlockSpec((tr, LANES), lambda c, t: (t, c))],
                          out_specs=pl.BlockSpec((kw, LANES), lambda c, t: (0, c)),
                          out_shape=jax.ShapeDtypeStruct((kw, d), F32),
                          scratch_shapes=[pltpu.VMEM((tr + 2 * halo, LANES), F32)],
                          compiler_params=_params(("arbitrary", "arbitrary")))(a, a, a, dout)


def _sincos_1d(pos, dim):
    quarter = dim // 2
    omega = POS_TEMP ** (-jnp.arange(quarter, dtype=F32) / quarter)
    ang = pos[:, None] * omega[None, :]
    return jnp.concatenate([jnp.sin(ang), jnp.cos(ang)], axis=-1)


def _grid_pos_embed(rows, dim):
    row_idx = jnp.repeat(jnp.arange(rows), GRID_W).astype(F32)
    col_idx = jnp.tile(jnp.arange(GRID_W), rows).astype(F32)
    return jnp.concatenate([_sincos_1d(row_idx, dim // 2), _sincos_1d(col_idx, dim // 2)], axis=-1)


def _pack(arrs, row_multiple=SUBLANES):
    flat = jnp.concatenate([a.reshape(-1).astype(F32) for a in arrs])
    pad = (-flat.shape[0]) % (row_multiple * LANES)
    return jnp.pad(flat, (0, pad)).reshape(-1, LANES)


def _unpack(buf, shapes):
    flat = buf.reshape(-1)
    out, pos = [], 0
    for shp in shapes:
        n = math.prod(shp)
        out.append(flat[pos:pos + n].reshape(shp))
        pos += n
    return out


def _unpack_gathered(buf, shapes):
    flat = buf.reshape(N_DEV, -1)
    out, pos = [], 0
    for shp in shapes:
        n = math.prod(shp)
        part = flat[:, pos:pos + n].reshape((N_DEV,) + tuple(shp))
        out.append(jnp.moveaxis(part, 0, -2).reshape(tuple(shp[:-1]) + (N_DEV * shp[-1],)))
        pos += n
    return out


WEIGHTS = ("c_ctx", "w_ada", "b_ada", "ln_gain", "ln_bias", "s5_lam_re", "s5_lam_im", "s5_log_dt", "s5_b_re", "s5_b_im",
           "s5_c_re", "s5_c_im", "s5_d", "s5_w_glu", "s5_b_glu", "cv_w_pw1", "cv_b_pw1", "cv_w_dw", "cv_b_dw", "cv_ln_g",
           "cv_ln_b", "cv_w_pw2", "cv_b_pw2", "mlp_w1", "mlp_w2")
SHARDED_SMALL = ("ln_gain", "ln_bias", "cv_b_pw1", "cv_w_dw", "cv_b_dw", "cv_ln_g", "cv_ln_b", "cv_b_pw2")
REPLICATED_SMALL = ("s5_lam_re", "s5_lam_im", "s5_log_dt", "s5_b_re", "s5_b_im", "s5_c_re", "s5_c_im", "s5_d", "s5_b_glu")
BIG = ("mlp_w1", "mlp_w2", "s5_w_glu", "cv_w_pw1", "cv_w_pw2")
BIG_AXIS = (2, 1, 2, 2, 1)


def _step(a):
    x, c, ctx = a["x"], a["c"], a["ctx"]
    nb, seq, d = x.shape
    lc = ctx.shape[1]
    nl = a["w_ada"].shape[0]
    tr = lc
    tpl = seq // tr
    cfg = {"tr": tr, "tpl": tpl, "nb": nb}
    n_rows = nb * (seq + lc)
    alpha = (2.0 * nl) ** 0.25
    me = 4 * lax.axis_index("x") + 2 * lax.axis_index("y") + lax.axis_index("c")
    n_grp, n_state = a["s5_lam_re"].shape[2:]
    ch = a["s5_b_re"].shape[-1]
    gs = LANES // ch
    ns = d // LANES
    tm = 2 * tr if n_rows % (2 * tr) == 0 else tr
    tm_big = n_rows // 3 if n_rows % (3 * 2 * SUBLANES) == 0 else tm
    f_sub1_s5, f_sub1_cv, f_sub2 = _make_sub1_s5(alpha), _make_sub1_cv(alpha), _make_sub2(alpha)

    w1_all, w2_all, glu_all, pw1_all, pw2_all = _exchange("gather_weights", [a[n].astype(BF16) for n in BIG], BIG_AXIS)
    grad_bufs = {n: lax.empty((N_DEV,) + a[n].shape, BF16) for n in BIG}
    small_all = _exchange("gather_small", [_pack([a[n] for n in SHARDED_SMALL])], ["slot"])[0]
    full = dict(zip(SHARDED_SMALL, _unpack_gathered(small_all, [a[n].shape for n in SHARDED_SMALL])))
    c_all = _exchange("gather_c", [c], ["slot"])[0].reshape(N_DEV * nb, d)
    cond_rows = N_DEV * nb + SUBLANES
    cc = jnp.concatenate([c_all, a["c_ctx"][None], jnp.zeros((SUBLANES - 1, d), F32)], axis=0)

    n_ada = a["w_ada"].shape[2]
    b_loc = lax.dynamic_slice(a["b_ada"], (0, me * n_ada), (nl, n_ada))[:, None, :]
    mod_cols = _ada_fwd(cc, a["w_ada"], b_loc)
    mod_all = _exchange("gather_mod", [mod_cols.reshape(nl * cond_rows, n_ada)], ["slot"])[0].reshape(N_DEV, nl, cond_rows, n_ada)
    mod_mine = jnp.concatenate([lax.dynamic_slice(mod_all, (0, 0, nb * me, 0), (N_DEV, nl, nb, n_ada)),
                                mod_all[:, :, N_DEV * nb:N_DEV * nb + 1]], axis=2)
    mod = jnp.transpose(mod_mine, (1, 2, 0, 3)).reshape(nl, nb + 1, 6, 1, d)

    def seg(i, q):
        return mod[i, :, q]

    zero_seg = jnp.zeros((nb + 1, 1, d), F32)

    def vec(v):
        return v.reshape(1, -1)

    pos = _grid_pos_embed(seq // GRID_W, d)
    xc = jnp.concatenate([x.reshape(nb * seq, d), ctx.reshape(nb * lc, d)], axis=0)
    pos_rows = jnp.concatenate([jnp.tile(pos, (nb, 1)), jnp.zeros((nb * lc, d), F32)], axis=0)
    x_cur, h_cur = _rowwise("entry", _f_entry, [xc, pos_rows], [seg(0, 0), seg(0, 1)], [], [(d, F32), (d, BF16)], [], [], cfg)
    saved = []
    for i in range(nl):
        j = i // 2
        sv = {"x": x_cur, "h": h_cur}
        sh1, sc1, g1, sh2, sc2, g2 = (seg(i, q) for q in range(6))
        gain0, bias0, gain1, bias1 = (vec(full["ln_gain"][i, 0]), vec(full["ln_bias"][i, 0]),
                                      vec(full["ln_gain"][i, 1]), vec(full["ln_bias"][i, 1]))
        if i % 2 == 0:
            lam_re, lam_im = a["s5_lam_re"][j], a["s5_lam_im"][j]
            log_dt = a["s5_log_dt"][j][:, :, None]
            b_re_t = jnp.transpose(a["s5_b_re"][j], (0, 3, 1, 2))
            b_im_t = jnp.transpose(a["s5_b_im"][j], (0, 3, 1, 2))
            sv["prep_in"] = (lam_re, lam_im, log_dt, b_re_t, b_im_t)
            ab_re, ab_im, bb_re, bb_im = _s5_prep(f"s5_prep{i}", *sv["prep_in"])
            sv["ab"] = (ab_re, ab_im)
            ys = []
            for dirn in range(2):
                def blocks(t):
                    return jnp.transpose(t, (1, 0, 2)).reshape(ns, gs, ch, n_state)
                bmat = jnp.concatenate([_block_diag(blocks(bb_re[dirn])), _block_diag(blocks(bb_im[dirn]))], axis=2).astype(BF16)
                c_re_t = jnp.transpose(a["s5_c_re"][j, dirn], (0, 2, 1)).reshape(ns, gs, n_state, ch)
                c_im_t = jnp.transpose(a["s5_c_im"][j, dirn], (0, 2, 1)).reshape(ns, gs, n_state, ch)
                cmat = jnp.concatenate([_block_diag(c_re_t), -_block_diag(c_im_t)], axis=1).astype(BF16)
                tab = _s5_tables(ab_re[dirn], ab_im[dirn], dirn == 1, False, ns)
                h_states, y_dir = _s5_fwd(f"s5_fwd{i}_{dirn}", h_cur, bmat, cmat, tab, dirn, cfg)
                sv[f"mats{dirn}"] = (jnp.transpose(bmat, (0, 2, 1)), jnp.transpose(cmat, (0, 2, 1)))
                sv[f"states{dirn}"] = h_states
                ys.append(y_dir)
            sv["y"] = ys
            dsk = vec(a["s5_d"][j])
            z = _rowwise(f"gelu{i}", _f_gelu, [x_cur, ys[0], ys[1]], [sh1, sc1], [dsk], [(d, BF16)], [], [], cfg)[0]
            zz = _mm_nn(f"glu{i}", z, glu_all, j, tm_big, min(2 * d, 512))[0]
            bglu = vec(a["s5_b_glu"][j])
            x1, h2 = _rowwise(f"sub1_{i}", f_sub1_s5, [x_cur, zz], [g1, sh2, sc2], [bglu, gain0, bias0],
                              [(d, F32), (d, BF16)], [], [], cfg)
            sv.update(z=z, zz=zz)
        else:
            zz = _mm_nn(f"pw1_{i}", h_cur, pw1_all, j, tm_big, min(2 * d, 512))[0]
            bpw1 = vec(full["cv_b_pw1"][j])
            act = _rowwise(f"cvglu{i}", _f_cvglu, [zz], [], [bpw1], [(d, F32)], [], [], cfg)[0]
            w_dw = full["cv_w_dw"][j]
            cv = _dwconv(f"dwconv{i}", act, w_dw, cfg)
            bdw, lng, lnb = vec(full["cv_b_dw"][j]), vec(full["cv_ln_g"][j]), vec(full["cv_ln_b"][j])
            s_act = _rowwise(f"cvln{i}", _f_cvln, [cv], [], [bdw, lng, lnb], [(d, BF16)], [], [], cfg)[0]
            mm = _mm_nn(f"pw2_{i}", s_act, pw2_all, j, tm_big, d)[0]
            bpw2 = vec(full["cv_b_pw2"][j])
            x1, h2 = _rowwise(f"sub1_{i}", f_sub1_cv, [x_cur, mm], [g1, sh2, sc2], [bpw2, gain0, bias0],
                              [(d, F32), (d, BF16)], [], [], cfg)
            sv.update(zz=zz, act=act, cv=cv, s_act=s_act, mm=mm, w_dw=w_dw)
        dff = w1_all.shape[2]
        p_act, r_act = _mm_nn(f"mlp1_{i}", h2, w1_all, i, tm_big, min(dff, 1024), (BF16, BF16),
                              lambda acc: (jnp.square(jnp.maximum(acc, 0.0)), jnp.maximum(acc, 0.0)))
        m_out = _mm_nn(f"mlp2_{i}", p_act, w2_all, i, tm, d)[0]
        shn, scn = (seg(i + 1, 0), seg(i + 1, 1)) if i + 1 < nl else (zero_seg, zero_seg)
        x2, hn = _rowwise(f"sub2_{i}", f_sub2, [x1, m_out], [g2, shn, scn], [gain1, bias1], [(d, F32), (d, BF16)], [], [], cfg)
        sv.update(x1=x1, h2=h2, p=p_act, r=r_act, m=m_out, shn=shn, scn=scn)
        saved.append(sv)
        x_cur, h_cur = x2, hn

    target = jnp.concatenate([a["loss_target"].reshape(nb * seq, d), jnp.zeros((nb * lc, d), F32)], axis=0)
    mask = jnp.concatenate([jnp.ones((nb, 1, d), F32), jnp.zeros((1, 1, d), F32)], axis=0)

    def f_loss(xf, tgt, msk):
        err = (xf - tgt) * msk
        part = 0.5 * jnp.sum(jnp.square(err), axis=(0, 1), keepdims=True) / d
        return err / d, jnp.broadcast_to(part, (1, LANES))

    dx_final, loss_part = _rowwise("loss", f_loss, [x_cur, target], [mask], [], [(d, F32)], [], [LANES], cfg)
    loss = lax.psum(loss_part[0, 0], ("x", "y", "c"))

    grads = {n: [None] * a[n].shape[0] for n in WEIGHTS if n not in ("c_ctx", "w_ada", "b_ada")}
    dmod = [[None] * 6 for _ in range(nl)]

    def add_mod(i, q, val):
        dmod[i][q] = val if dmod[i][q] is None else dmod[i][q] + val

    dx_parts, dh_parts = [dx_final], []
    for i in reversed(range(nl)):
        j = i // 2
        sv = saved[i]
        sh1, sc1, g1, sh2, sc2, g2 = (seg(i, q) for q in range(6))
        gain0, bias0, gain1, bias1 = (vec(full["ln_gain"][i, 0]), vec(full["ln_bias"][i, 0]),
                                      vec(full["ln_gain"][i, 1]), vec(full["ln_bias"][i, 1]))
        bwd = _vjp_fn(f_sub2, 2, (len(dx_parts), len(dh_parts)), (0, 1, 2, 3, 4, 5, 6))
        dx1, dm, dg2, dshn, dscn, dgain1, dbias1 = _rowwise(
            f"sub2_bwd{i}", bwd, [sv["x1"], sv["m"]] + dx_parts + dh_parts, [g2, sv["shn"], sv["scn"]], [gain1, bias1],
            [(d, F32), (d, BF16)], [d, d, d], [d, d], cfg)
        add_mod(i, 5, dg2)
        if i + 1 < nl:
            add_mod(i + 1, 0, dshn)
            add_mod(i + 1, 1, dscn)
        da = _mm_nt(f"mlp2_dgrad{i}", dm, w2_all, i, tm_big, min(dff, 1024), [sv["r"]], BF16, lambda acc, r: (acc * 2.0 * r,))
        grad_bufs["mlp_w2"] = _mm_wgrad_rows(f"mlp2_wgrad{i}", sv["p"], dm, grad_bufs["mlp_w2"], i, tm_big)
        grad_bufs["mlp_w1"] = _mm_wgrad_cols(f"mlp1_wgrad{i}", sv["h2"], da, grad_bufs["mlp_w1"], i, tm_big)
        dh2 = _mm_nt(f"mlp1_dgrad{i}", da, w1_all, i, tm, d)
        if i % 2 == 0:
            bglu = vec(a["s5_b_glu"][j])
            bwd = _vjp_fn(f_sub1_s5, 2, (1, 1), (0, 1, 2, 3, 4, 5, 6, 7))
            dxa, dzz, dg1, dsh2, dsc2, dbglu, dgain0, dbias0 = _rowwise(
                f"sub1_bwd{i}", bwd, [sv["x"], sv["zz"], dx1, dh2], [g1, sh2, sc2], [bglu, gain0, bias0],
                [(d, F32), (2 * d, BF16)], [d, d, d], [2 * d, d, d], cfg)
            grads["s5_b_glu"][j] = dbglu[0]
            grad_bufs["s5_w_glu"] = _mm_wgrad_cols(f"glu_wgrad{i}", sv["z"], dzz, grad_bufs["s5_w_glu"], j, tm_big)
            dz = _mm_nt(f"glu_dgrad{i}", dzz, glu_all, j, tm, d)
            dsk = vec(a["s5_d"][j])
            bwd = _vjp_fn(_f_gelu, 3, (1,), (0, 1, 3, 4, 5))
            dxb, dy, dsh1, dsc1, ddsk = _rowwise(f"gelu_bwd{i}", bwd, [sv["x"], sv["y"][0], sv["y"][1], dz], [sh1, sc1], [dsk],
                                                 [(d, F32), (d, BF16)], [d, d], [d], cfg)
            grads["s5_d"][j] = ddsk[0]
            add_mod(i, 0, dsh1)
            add_mod(i, 1, dsc1)
            ab_re, ab_im = sv["ab"]
            dus, d_ab_re, d_ab_im, d_bb_re, d_bb_im, d_c_re, d_c_im = [], [], [], [], [], [], []
            for dirn in range(2):
                bmat_t, cmat_t = sv[f"mats{dirn}"]
                tab = _s5_tables(ab_re[dirn], ab_im[dirn], dirn == 0, True, ns)
                du, d_b, d_c, d_a = _s5_bwd(f"s5_bwd{i}_{dirn}", dy, sv[f"states{dirn}"], sv["h"], cmat_t, bmat_t, tab, dirn, cfg)
                dus.append(du)
                sw = d_a.shape[2] // 2
                d_a = jnp.sum(d_a, axis=1)
                d_ab_re.append(d_a[:, :sw].reshape(n_grp, n_state))
                d_ab_im.append(d_a[:, sw:].reshape(n_grp, n_state))

                def unblock_b(t):
                    return jnp.transpose(_diag_blocks(t, gs).reshape(n_grp, ch, n_state), (1, 0, 2))

                def unblock_c(t):
                    return jnp.transpose(_diag_blocks(t, gs).reshape(n_grp, n_state, ch), (0, 2, 1))
                d_bb_re.append(unblock_b(d_b[:, :, :sw]))
                d_bb_im.append(unblock_b(d_b[:, :, sw:]))
                d_c_re.append(unblock_c(d_c[:, :sw]))
                d_c_im.append(-unblock_c(d_c[:, sw:]))
            g_lre, g_lim, g_ldt, g_bre, g_bim = _s5_prep_bwd(
                f"s5_prep_bwd{i}", *sv["prep_in"], (jnp.stack(d_ab_re), jnp.stack(d_ab_im), jnp.stack(d_bb_re), jnp.stack(d_bb_im)))
            grads["s5_lam_re"][j], grads["s5_lam_im"][j], grads["s5_log_dt"][j] = g_lre, g_lim, g_ldt[:, :, 0]
            grads["s5_b_re"][j] = jnp.transpose(g_bre, (0, 2, 3, 1))
            grads["s5_b_im"][j] = jnp.transpose(g_bim, (0, 2, 3, 1))
            grads["s5_c_re"][j], grads["s5_c_im"][j] = jnp.stack(d_c_re), jnp.stack(d_c_im)
            dx_parts, dh_parts = [dxa, dxb], dus
        else:
            bpw2 = vec(full["cv_b_pw2"][j])
            bwd = _vjp_fn(f_sub1_cv, 2, (1, 1), (0, 1, 2, 3, 4, 5, 6, 7))
            dxa, dmm, dg1, dsh2, dsc2, dbpw2, dgain0, dbias0 = _rowwise(
                f"sub1_bwd{i}", bwd, [sv["x"], sv["mm"], dx1, dh2], [g1, sh2, sc2], [bpw2, gain0, bias0],
                [(d, F32), (d, BF16)], [d, d, d], [d, d, d], cfg)
            grads["cv_b_pw2"][j] = dbpw2[0]
            grad_bufs["cv_w_pw2"] = _mm_wgrad_rows(f"pw2_wgrad{i}", sv["s_act"], dmm, grad_bufs["cv_w_pw2"], j, tm_big)
            ds = _mm_nt(f"pw2_dgrad{i}", dmm, pw2_all, j, tm_big, d)
            bdw, lng, lnb = vec(full["cv_b_dw"][j]), vec(full["cv_ln_g"][j]), vec(full["cv_ln_b"][j])
            bwd = _vjp_fn(_f_cvln, 1, (1,), (0, 1, 2, 3))
            dcv, dbdw, dlng, dlnb = _rowwise(f"cvln_bwd{i}", bwd, [sv["cv"], ds], [], [bdw, lng, lnb], [(d, F32)], [], [d, d, d], cfg)
            grads["cv_b_dw"][j], grads["cv_ln_g"][j], grads["cv_ln_b"][j] = dbdw[0], dlng[0], dlnb[0]
            dact = _dwconv(f"dwconv_bwd{i}", dcv, sv["w_dw"][::-1], cfg)
            grads["cv_w_dw"][j] = _dwconv_wgrad(f"dwconv_wgrad{i}", sv["act"], dcv, sv["w_dw"].shape[0], cfg)
            bpw1 = vec(full["cv_b_pw1"][j])
            bwd = _vjp_fn(_f_cvglu, 1, (1,), (0, 1))
            dzz, dbpw1 = _rowwise(f"cvglu_bwd{i}", bwd, [sv["zz"], dact], [], [bpw1], [(2 * d, BF16)], [], [2 * d], cfg)
            grads["cv_b_pw1"][j] = dbpw1[0]
            grad_bufs["cv_w_pw1"] = _mm_wgrad_cols(f"pw1_wgrad{i}", sv["h"], dzz, grad_bufs["cv_w_pw1"], j, tm_big)
            dh = _mm_nt(f"pw1_dgrad{i}", dzz, pw1_all, j, tm, d)
            dx_parts, dh_parts = [dxa], [dh]
        grads["ln_gain"][i] = jnp.stack([dgain0[0], dgain1[0]])
        grads["ln_bias"][i] = jnp.stack([dbias0[0], dbias1[0]])
        add_mod(i, 2, dg1)
        add_mod(i, 3, dsh2)
        add_mod(i, 4, dsc2)
    bwd = _vjp_fn(_f_entry, 2, (len(dx_parts), len(dh_parts)), (0, 2, 3))
    dxc, dsh1, dsc1 = _rowwise("entry_bwd", bwd, [xc, pos_rows] + dx_parts + dh_parts, [seg(0, 0), seg(0, 1)], [],
                               [(d, F32)], [d, d], [], cfg)
    add_mod(0, 0, dsh1)
    add_mod(0, 1, dsc1)
    grad_x = dxc[:nb * seq].reshape(nb, seq, d)

    dmod_loc = jnp.stack([jnp.concatenate([q[:, 0] for q in dmod[i]], axis=1) for i in range(nl)])
    dmod_all = _exchange("gather_dmod", [dmod_loc.reshape(nl * (nb + 1), 6 * d)], ["slot"])[0].reshape(N_DEV, nl, nb + 1, 6 * d)
    mine = lax.dynamic_slice(dmod_all, (0, 0, 0, me * n_ada), (N_DEV, nl, nb + 1, n_ada))
    dmod_rows = jnp.transpose(mine[:, :, :nb], (1, 0, 2, 3)).reshape(nl, N_DEV * nb, n_ada)
    dmod_rows = jnp.concatenate([dmod_rows, jnp.zeros((nl, SUBLANES, n_ada), F32)], axis=1)
    g_w_ada, _, dcond = _ada_bwd(cc, a["w_ada"], dmod_rows, mine[:, :, nb:])
    g_b_ada = _sum_lead("b_ada_sum", jnp.transpose(dmod_all, (0, 2, 1, 3)).reshape(N_DEV * (nb + 1), nl, 6 * d), nl)
    dcond_all = _exchange("gather_dcond", [dcond[N_DEV * nb:N_DEV * nb + 1]], ["slot"])[0]
    g_c_ctx = _cctx_grad(dcond_all, a["c_ctx"][None])[0]

    small_names = SHARDED_SMALL + REPLICATED_SMALL
    small_full = [jnp.stack(grads[n]) for n in small_names]
    small_packed = _pack(small_full, N_DEV * SUBLANES)
    received = _exchange("scatter_grads", [grad_bufs[n] for n in BIG] + [small_packed.reshape(N_DEV, -1, LANES)],
                         ["scatter"] * (len(BIG) + 1))
    small_part = _sum_lead("small_grad_sum", received[-1], _row_tile(received[-1].shape[1], 512))
    small_sum = _exchange("gather_small_sum", [small_part], ["slot"])[0]
    small_g = dict(zip(small_names, _unpack(small_sum, [g.shape for g in small_full])))
    for n in SHARDED_SMALL:
        width = a[n].shape[-1]
        start = (0,) * (small_g[n].ndim - 1) + (me * width,)
        small_g[n] = lax.dynamic_slice(small_g[n], start, a[n].shape)
    small_g["c_ctx"], small_g["b_ada"] = g_c_ctx, g_b_ada

    out = {}

    def update(n, parts):
        shp = a[n].shape
        cols = parts.shape[-1]
        rows = parts.shape[1]
        res = _adamw(f"adamw_{n}", parts, a[n].reshape(rows, cols), a["m_" + n].reshape(rows, cols), a["v_" + n].reshape(rows, cols),
                     _row_tile(rows, max(SUBLANES, 131072 // cols)))
        out[n] = [r.reshape(shp) for r in res]

    for n, parts in zip(BIG, received):
        update(n, parts.reshape(N_DEV, -1, parts.shape[-1]))
    update("w_ada", g_w_ada.reshape(1, -1, n_ada))
    small_all_names = ("c_ctx", "b_ada") + small_names
    packed = [_pack([src[n] for n in small_all_names]) for src in
              (small_g, a, {n: a["m_" + n] for n in small_all_names}, {n: a["v_" + n] for n in small_all_names})]
    res = _adamw("adamw_small", packed[0][None], packed[1], packed[2], packed[3], _row_tile(packed[0].shape[0], 512))
    shapes = [a[n].shape for n in small_all_names]
    for n, vals in zip(small_all_names, zip(*[_unpack(r, shapes) for r in res])):
        out[n] = list(vals)
    return (loss, grad_x, *[out[n][0] for n in WEIGHTS], *[out[n][1] for n in WEIGHTS],
            *[out[n][2] for n in WEIGHTS], *[out[n][3] for n in WEIGHTS])


def kernel(x, c, ctx, c_ctx, w_ada, b_ada, ln_gain, ln_bias, s5_lam_re, s5_lam_im, s5_log_dt, s5_b_re, s5_b_im, s5_c_re, s5_c_im, s5_d, s5_w_glu, s5_b_glu, cv_w_pw1, cv_b_pw1, cv_w_dw, cv_b_dw, cv_ln_g, cv_ln_b, cv_w_pw2, cv_b_pw2, mlp_w1, mlp_w2, loss_target, m_c_ctx, m_w_ada, m_b_ada, m_ln_gain, m_ln_bias, m_s5_lam_re, m_s5_lam_im, m_s5_log_dt, m_s5_b_re, m_s5_b_im, m_s5_c_re, m_s5_c_im, m_s5_d, m_s5_w_glu, m_s5_b_glu, m_cv_w_pw1, m_cv_b_pw1, m_cv_w_dw, m_cv_b_dw, m_cv_ln_g, m_cv_ln_b, m_cv_w_pw2, m_cv_b_pw2, m_mlp_w1, m_mlp_w2, v_c_ctx, v_w_ada, v_b_ada, v_ln_gain, v_ln_bias, v_s5_lam_re, v_s5_lam_im, v_s5_log_dt, v_s5_b_re, v_s5_b_im, v_s5_c_re, v_s5_c_im, v_s5_d, v_s5_w_glu, v_s5_b_glu, v_cv_w_pw1, v_cv_b_pw1, v_cv_w_dw, v_cv_b_dw, v_cv_ln_g, v_cv_ln_b, v_cv_w_pw2, v_cv_b_pw2, v_mlp_w1, v_mlp_w2):
    return _step(dict(locals()))
```

```python
import functools
import math

import jax
import jax.numpy as jnp
from jax import lax
from jax.experimental import pallas as pl
from jax.experimental.pallas import tpu as pltpu

F32 = jnp.float32
BF16 = jnp.bfloat16
N_DEV = 8
LANES = 128
SUBLANES = 8
VMEM_LIMIT = 56 * 1024 * 1024
GRID_W = 64
POS_TEMP = 10000.0
LN_EPS = 1e-5
LAMBDA_RE_MAX = -1e-4
ADAM_LR, ADAM_B1, ADAM_B2, ADAM_EPS, ADAM_WD, ADAM_STEP = 0.001, 0.9, 0.999, 1e-08, 0.01, 10
MESH = pl.DeviceIdType.MESH


def _params(sem):
    return pltpu.CompilerParams(dimension_semantics=sem, vmem_limit_bytes=VMEM_LIMIT)


def _accumulate(ref, val, first):
    @pl.when(first)
    def _():
        ref[...] = val

    @pl.when(jnp.logical_not(first))
    def _():
        ref[...] += val


def _rowwise(name, fn, rows, segs, vecs, row_outs, seg_accs, vec_accs, cfg):
    tr, tpl, nb = cfg["tr"], cfg["tpl"], cfg["nb"]
    n_rows = rows[0].shape[0]
    nt = n_rows // tr
    nr, ns, nv = len(rows), len(segs), len(vecs)
    nro, nsa = len(row_outs), len(seg_accs)

    def seg_of(t):
        return jnp.minimum(t // tpl, nb)

    def body(*refs):
        t = pl.program_id(0)
        ins, outs = refs[:nr + ns + nv], refs[nr + ns + nv:]
        vals = [r[...] for r in ins[:nr]] + [r[0] for r in ins[nr:nr + ns]] + [r[...] for r in ins[nr + ns:]]
        res = fn(*vals)
        for o, v in zip(outs[:nro], res[:nro]):
            o[...] = v.astype(o.dtype)
        first_seg = jnp.logical_or(t == 0, seg_of(t) != seg_of(jnp.maximum(t - 1, 0)))
        for o, v in zip(outs[nro:nro + nsa], res[nro:nro + nsa]):
            _accumulate(o.at[0], v, first_seg)
        for o, v in zip(outs[nro + nsa:], res[nro + nsa:]):
            _accumulate(o, v, t == 0)

    in_specs = ([pl.BlockSpec((tr, a.shape[1]), lambda t: (t, 0)) for a in rows]
                + [pl.BlockSpec((1, 1, a.shape[2]), lambda t: (seg_of(t), 0, 0)) for a in segs]
                + [pl.BlockSpec((1, a.shape[1]), lambda t: (0, 0)) for a in vecs])
    out_specs = ([pl.BlockSpec((tr, c), lambda t: (t, 0)) for c, _ in row_outs]
                 + [pl.BlockSpec((1, 1, c), lambda t: (seg_of(t), 0, 0)) for c in seg_accs]
                 + [pl.BlockSpec((1, c), lambda t: (0, 0)) for c in vec_accs])
    out_shape = ([jax.ShapeDtypeStruct((n_rows, c), dt) for c, dt in row_outs]
                 + [jax.ShapeDtypeStruct((nb + 1, 1, c), F32) for c in seg_accs]
                 + [jax.ShapeDtypeStruct((1, c), F32) for c in vec_accs])
    return pl.pallas_call(body, name=name, grid=(nt,), in_specs=in_specs, out_specs=out_specs,
                          out_shape=out_shape, compiler_params=_params(("arbitrary",)))(*rows, *segs, *vecs)


def _vjp_fn(fn, n_row, cot_groups, want):
    n_cot = sum(cot_groups)

    def bwd(*args):
        primals = [a.astype(F32) for a in args[:n_row] + args[n_row + n_cot:]]
        outs, vjp = jax.vjp(fn, *primals)
        cots, pos = [], n_row
        for n, o in zip(cot_groups, outs):
            cot = jnp.zeros_like(o)
            for part in args[pos:pos + n]:
                cot = cot + part.astype(F32)
            cots.append(cot)
            pos += n
        grads = vjp(tuple(cots))
        return tuple(grads[i] for i in want)
    return bwd


def _ln(r, g, b):
    mu = jnp.mean(r, axis=-1, keepdims=True)
    var = jnp.mean(jnp.square(r - mu), axis=-1, keepdims=True)
    return (r - mu) * lax.rsqrt(var + LN_EPS) * g + b


def _glu(zz, bias):
    d = zz.shape[1] // 2
    return (zz[:, :d] + bias[:, :d]) * jax.nn.sigmoid(zz[:, d:] + bias[:, d:])


def _f_entry(xc, pos, sh, sc):
    x0 = xc + pos
    return x0, x0 * (1 + sc) + sh


def _f_gelu(x, y0, y1, sh, sc, dsk):
    u = x * (1 + sc) + sh
    y = dsk * u + y0 + y1
    return (0.5 * y * (1.0 + lax.erf(y * (2.0 ** -0.5))),)


def _make_sub1_s5(alpha):
    def f(x, zz, g1, sh2, sc2, bglu, gain, bias):
        x1 = _ln(alpha * x + g1 * _glu(zz, bglu), gain, bias)
        return x1, x1 * (1 + sc2) + sh2
    return f


def _make_sub1_cv(alpha):
    def f(x, mm, g1, sh2, sc2, bpw2, gain, bias):
        x1 = _ln(alpha * x + g1 * (mm + bpw2), gain, bias)
        return x1, x1 * (1 + sc2) + sh2
    return f


def _make_sub2(alpha):
    def f(x1, m, g2, shn, scn, gain, bias):
        x2 = _ln(alpha * x1 + g2 * m, gain, bias)
        return x2, x2 * (1 + scn) + shn
    return f


def _f_cvglu(zz, bpw1):
    return (_glu(zz, bpw1),)


def _f_cvln(cv, bdw, lng, lnb):
    return (jax.nn.silu(_ln(cv + bdw, lng, lnb)),)


def _matmul(name, a, b, extras, grid, a_spec, b_spec, extra_specs, o_specs, out_shape, dims, red_axis, epi, sem):
    n_extra = len(extras)
    n_out = len(out_shape)
    acc_shape = o_specs[0].block_shape
    acc_shape = tuple(s for s in acc_shape if s is not None)

    def body(*refs):
        a_ref, b_ref = refs[0], refs[1]
        ex = refs[2:2 + n_extra]
        outs = refs[2 + n_extra:2 + n_extra + n_out]
        prod = lax.dot_general(a_ref[...], b_ref[...], dims, preferred_element_type=F32)

        def finish(acc):
            res = epi(acc, *[e[...] for e in ex]) if epi is not None else (acc,)
            for o, v in zip(outs, res):
                o[...] = v.astype(o.dtype)

        if red_axis is None:
            finish(prod)
        else:
            acc_ref = refs[-1]
            k = pl.program_id(red_axis)
            nk = pl.num_programs(red_axis)

            @pl.when(k == 0)
            def _():
                acc_ref[...] = prod

            @pl.when(k > 0)
            def _():
                acc_ref[...] += prod

            @pl.when(k == nk - 1)
            def _():
                finish(acc_ref[...])

    scratch = [] if red_axis is None else [pltpu.VMEM(acc_shape, F32)]
    res = pl.pallas_call(body, name=name, grid=grid, in_specs=[a_spec, b_spec] + list(extra_specs),
                         out_specs=list(o_specs), out_shape=list(out_shape), scratch_shapes=scratch,
                         compiler_params=_params(sem))(a, b, *extras)
    return res


NN = (((1,), (0,)), ((), ()))
NT = (((1,), (1,)), ((), ()))
TN = (((0,), (0,)), ((), ()))


def _mm_nn(name, a, w3, layer, tm, tn, out_dtypes=(F32,), epi=None):
    m, k = a.shape
    n = w3.shape[2]
    return _matmul(name, a, w3, (), (n // tn, m // tm),
                   pl.BlockSpec((tm, k), lambda j, i: (i, 0)), pl.BlockSpec((None, k, tn), lambda j, i: (layer, 0, j)), (),
                   [pl.BlockSpec((tm, tn), lambda j, i: (i, j)) for _ in out_dtypes],
                   [jax.ShapeDtypeStruct((m, n), dt) for dt in out_dtypes], NN, None, epi, ("arbitrary", "arbitrary"))


def _mm_nt(name, dy, w3, layer, tm, tkw, extras=(), out_dtype=F32, epi=None):
    m, n = dy.shape
    kw = w3.shape[1]
    return _matmul(name, dy, w3, tuple(extras), (kw // tkw, m // tm),
                   pl.BlockSpec((tm, n), lambda j, i: (i, 0)), pl.BlockSpec((None, tkw, n), lambda j, i: (layer, j, 0)),
                   [pl.BlockSpec((tm, tkw), lambda j, i: (i, j)) for _ in extras],
                   [pl.BlockSpec((tm, tkw), lambda j, i: (i, j))], [jax.ShapeDtypeStruct((m, kw), out_dtype)], NT, None, epi,
                   ("arbitrary", "arbitrary"))[0]


def _mm_wgrad_cols(name, a, dy, tm):
    m, k = a.shape
    n = dy.shape[1] // N_DEV
    return _matmul(name, a, dy, (), (N_DEV, m // tm),
                   pl.BlockSpec((tm, k), lambda j, i: (i, 0)), pl.BlockSpec((tm, n), lambda j, i: (i, j)), (),
                   [pl.BlockSpec((None, k, n), lambda j, i: (j, 0, 0))], [jax.ShapeDtypeStruct((N_DEV, k, n), BF16)],
                   TN, 1, None, ("arbitrary", "arbitrary"))[0]


def _mm_wgrad_rows(name, a, dy, tm):
    m = a.shape[0]
    r = a.shape[1] // N_DEV
    n = dy.shape[1]
    return _matmul(name, a, dy, (), (N_DEV, m // tm),
                   pl.BlockSpec((tm, r), lambda j, i: (i, j)), pl.BlockSpec((tm, n), lambda j, i: (i, 0)), (),
                   [pl.BlockSpec((None, r, n), lambda j, i: (j, 0, 0))], [jax.ShapeDtypeStruct((N_DEV, r, n), BF16)],
                   TN, 1, None, ("arbitrary", "arbitrary"))[0]


class _Copies:
    def __init__(self, arrays, kinds):
        self.arrays, self.kinds, self.n = list(arrays), list(kinds), len(arrays)
        any_spec = pl.BlockSpec(memory_space=pl.ANY)
        self.in_specs = [any_spec] * self.n
        self.out_specs = [any_spec] * self.n
        self.out_shape = [jax.ShapeDtypeStruct(self._result(a, kind), a.dtype) for a, kind in zip(arrays, kinds)]
        self.scratch = [pltpu.SemaphoreType.DMA((self.n, N_DEV - 1)), pltpu.SemaphoreType.DMA((self.n, N_DEV - 1)),
                        pltpu.SemaphoreType.DMA((self.n,))] if self.n else []

    @staticmethod
    def _result(a, kind):
        if kind == "slot":
            return (N_DEV,) + a.shape
        if kind == "scatter":
            return a.shape
        return a.shape[:kind] + (N_DEV * a.shape[kind],) + a.shape[kind + 1:]

    def descriptors(self, ins, outs, sems):
        send_sems, recv_sems, local_sems = sems
        x, y, c = lax.axis_index("x"), lax.axis_index("y"), lax.axis_index("c")
        me = 4 * x + 2 * y + c

        def landing(i):
            kind = self.kinds[i]
            if kind in ("slot", "scatter"):
                return outs[i].at[me]
            size = ins[i].shape[kind]
            mine = pl.ds(pl.multiple_of(me * size, size), size)
            return outs[i].at[(slice(None),) * kind + (mine,)]

        copies = []
        for i in range(self.n):
            scatter = self.kinds[i] == "scatter"
            copies.append(pltpu.make_async_copy(ins[i].at[me] if scatter else ins[i], landing(i), local_sems.at[i]))
            for k in range(1, N_DEV):
                px = 1 - x if k & 4 else x
                py = 1 - y if k & 2 else y
                pc = 1 - c if k & 1 else c
                src = ins[i].at[4 * px + 2 * py + pc] if scatter else ins[i]
                copies.append(pltpu.make_async_remote_copy(src_ref=src, dst_ref=landing(i), send_sem=send_sems.at[i, k - 1],
                                                           recv_sem=recv_sems.at[i, k - 1], device_id=(px, py, pc),
                                                           device_id_type=MESH))
        return copies


def _exchange(name, arrays, kinds):
    cps = _Copies(arrays, kinds)
    n = cps.n

    def body(*refs):
        copies = cps.descriptors(refs[:n], refs[n:2 * n], refs[2 * n:])
        for cp in copies:
            cp.start()
        for cp in copies:
            cp.wait()

    return pl.pallas_call(body, name=name, in_specs=cps.in_specs, out_specs=cps.out_specs, out_shape=cps.out_shape,
                          scratch_shapes=cps.scratch)(*arrays)


def _carried(cps, n_in, n_out, n_scratch, refs):
    if cps is None:
        return [], refs
    n = cps.n
    ins = refs[n_in:n_in + n]
    outs = refs[n_in + n + n_out:n_in + n + n_out + n]
    sems = refs[n_in + n + n_out + n + n_scratch:]
    own = refs[:n_in] + refs[n_in + n:n_in + n + n_out] + refs[n_in + n + n_out + n:n_in + n + n_out + n + n_scratch]
    return cps.descriptors(ins, outs, sems), own


def _start_all(copies, when):
    @pl.when(when)
    def _():
        for cp in copies:
            cp.start()


def _wait_all(copies, when):
    @pl.when(when)
    def _():
        for cp in copies:
            cp.wait()


def _sum_lead(name, parts, tr):
    npart, r, c = parts.shape

    def body(p_ref, o_ref):
        acc = p_ref[0].astype(F32)
        for p in range(1, npart):
            acc = acc + p_ref[p].astype(F32)
        o_ref[...] = acc

    return pl.pallas_call(body, name=name, grid=(r // tr,), in_specs=[pl.BlockSpec((npart, tr, c), lambda i: (0, i, 0))],
                          out_specs=pl.BlockSpec((tr, c), lambda i: (i, 0)), out_shape=jax.ShapeDtypeStruct((r, c), F32),
                          compiler_params=_params(("arbitrary",)))(parts)


def _adamw_body(npart):
    def body(p_ref, w_ref, m_ref, v_ref, *rest):
        g_out, d_out, m_out, v_out = rest[-4:]
        g = p_ref[0].astype(F32)
        for p in range(1, npart):
            g = g + p_ref[p].astype(F32)
        m2 = ADAM_B1 * m_ref[...] + (1.0 - ADAM_B1) * g
        v2 = ADAM_B2 * v_ref[...] + (1.0 - ADAM_B2) * jnp.square(g)
        m_hat = m2 / (1.0 - ADAM_B1 ** ADAM_STEP)
        v_hat = v2 / (1.0 - ADAM_B2 ** ADAM_STEP)
        g_out[...] = g
        d_out[...] = -ADAM_LR * (m_hat / (jnp.sqrt(v_hat) + ADAM_EPS) + ADAM_WD * w_ref[...])
        m_out[...] = m2
        v_out[...] = v2
    return body


def _adamw(name, parts, w, m, v, tr):
    npart, r, c = parts.shape
    row = pl.BlockSpec((tr, c), lambda i: (i, 0))
    return pl.pallas_call(_adamw_body(npart), name=name, grid=(r // tr,),
                          in_specs=[pl.BlockSpec((npart, tr, c), lambda i: (0, i, 0)), row, row, row],
                          out_specs=[row] * 4, out_shape=[jax.ShapeDtypeStruct((r, c), F32)] * 4,
                          compiler_params=_params(("arbitrary",)))(parts, w, m, v)


def _adamw_layer(name, parts, w3, m3, v3, bufs, layer, tr):
    npart, r, c = parts.shape
    lay = pl.BlockSpec((None, tr, c), lambda i: (layer, i, 0))
    hbm = pl.BlockSpec(memory_space=pl.ANY)
    return pl.pallas_call(_adamw_body(npart), name=name, grid=(r // tr,),
                          in_specs=[pl.BlockSpec((npart, tr, c), lambda i: (0, i, 0)), lay, lay, lay] + [hbm] * 4,
                          out_specs=[lay] * 4, out_shape=[jax.ShapeDtypeStruct(w3.shape, F32)] * 4,
                          input_output_aliases={4: 0, 5: 1, 6: 2, 7: 3},
                          compiler_params=_params(("arbitrary",)))(parts, w3, m3, v3, *bufs)


def _row_tile(r, cap):
    if r <= cap:
        return r
    t = cap - cap % SUBLANES
    while r % t:
        t -= SUBLANES
    return t


def _ada_fwd(cc, w_ada, b_loc):
    nl, d, n = w_ada.shape
    rows = cc.shape[0]

    def body(c_ref, w_ref, b_ref, o_ref):
        cond = jax.nn.silu(c_ref[...]).astype(BF16)
        o_ref[...] = jnp.dot(cond, w_ref[...].astype(BF16), preferred_element_type=F32) + b_ref[...]

    return pl.pallas_call(body, name="ada_fwd", grid=(nl,),
                          in_specs=[pl.BlockSpec((rows, d), lambda i: (0, 0)), pl.BlockSpec((None, d, n), lambda i: (i, 0, 0)),
                                    pl.BlockSpec((None, 1, n), lambda i: (i, 0, 0))],
                          out_specs=pl.BlockSpec((None, rows, n), lambda i: (i, 0, 0)),
                          out_shape=jax.ShapeDtypeStruct((nl, rows, n), F32), compiler_params=_params(("arbitrary",)))(cc, w_ada, b_loc)


def _ada_bwd(cc, w_ada, dmod_rows, dmod_ctx):
    nl, d, n = w_ada.shape
    rows = cc.shape[0]
    ctx_row = rows - SUBLANES

    def body(c_ref, w_ref, dr_ref, dc_ref, gw_ref, tot_ref, dcond_ref):
        i = pl.program_id(0)
        total = dc_ref[0]
        for p in range(1, N_DEV):
            total = total + dc_ref[p]
        tot_ref[...] = total
        row_id = lax.broadcasted_iota(jnp.int32, (rows, n), 0)
        dm = jnp.where(row_id == ctx_row, jnp.broadcast_to(total, (rows, n)), dr_ref[...]).astype(BF16)
        cond = jax.nn.silu(c_ref[...]).astype(BF16)
        gw_ref[...] = lax.dot_general(cond, dm, TN, preferred_element_type=F32)
        part = lax.dot_general(dm, w_ref[...].astype(BF16), NT, preferred_element_type=F32)
        _accumulate(dcond_ref, part, i == 0)

    return pl.pallas_call(body, name="ada_bwd", grid=(nl,),
                          in_specs=[pl.BlockSpec((rows, d), lambda i: (0, 0)), pl.BlockSpec((None, d, n), lambda i: (i, 0, 0)),
                                    pl.BlockSpec((None, rows, n), lambda i: (i, 0, 0)),
                                    pl.BlockSpec((N_DEV, None, 1, n), lambda i: (0, i, 0, 0))],
                          out_specs=[pl.BlockSpec((None, d, n), lambda i: (i, 0, 0)), pl.BlockSpec((None, 1, n), lambda i: (i, 0, 0)),
                                     pl.BlockSpec((rows, d), lambda i: (0, 0))],
                          out_shape=[jax.ShapeDtypeStruct((nl, d, n), F32), jax.ShapeDtypeStruct((nl, 1, n), F32),
                                     jax.ShapeDtypeStruct((rows, d), F32)],
                          compiler_params=_params(("arbitrary",)))(cc, w_ada, dmod_rows, dmod_ctx)


def _cctx_grad(parts, c_ctx):
    def body(p_ref, c_ref, o_ref):
        tot = p_ref[0]
        for p in range(1, N_DEV):
            tot = tot + p_ref[p]
        _, vjp = jax.vjp(jax.nn.silu, c_ref[...])
        o_ref[...] = vjp(tot)[0]

    return pl.pallas_call(body, name="cctx_grad", out_shape=jax.ShapeDtypeStruct(c_ctx.shape, F32))(parts, c_ctx)


def _discretise(lam_re, lam_im, log_dt, b_re, b_im):
    lr = jnp.minimum(lam_re, LAMBDA_RE_MAX)
    li = lam_im
    dt = jnp.exp(log_dt)
    mag = jnp.exp(lr * dt)
    ab_re = mag * jnp.cos(li * dt)
    ab_im = mag * jnp.sin(li * dt)
    den = lr * lr + li * li
    nr = ab_re - 1.0
    ni = ab_im
    coef_re = ((nr * lr + ni * li) / den)[:, None]
    coef_im = ((ni * lr - nr * li) / den)[:, None]
    bb_re = coef_re * b_re - coef_im * b_im
    bb_im = coef_re * b_im + coef_im * b_re
    return ab_re, ab_im, bb_re, bb_im


def _s5_prep(name, lam_re, lam_im, log_dt, b_re, b_im):
    def body(a, b, c, d, e, o1, o2, o3, o4):
        res = _discretise(a[...], b[...], c[...], d[...], e[...])
        for o, v in zip((o1, o2, o3, o4), res):
            o[...] = v

    shp = [jax.ShapeDtypeStruct(lam_re.shape, F32)] * 2 + [jax.ShapeDtypeStruct(b_re.shape, F32)] * 2
    return pl.pallas_call(body, name=name, out_shape=shp)(lam_re, lam_im, log_dt, b_re, b_im)


def _s5_prep_bwd(name, lam_re, lam_im, log_dt, b_re, b_im, cots):
    def body(a, b, c, d, e, c1, c2, c3, c4, o1, o2, o3, o4, o5):
        _, vjp = jax.vjp(_discretise, a[...], b[...], c[...], d[...], e[...])
        grads = vjp((c1[...], c2[...], c3[...], c4[...]))
        for o, v in zip((o1, o2, o3, o4, o5), grads):
            o[...] = v

    shp = [jax.ShapeDtypeStruct(a.shape, F32) for a in (lam_re, lam_im, log_dt, b_re, b_im)]
    return pl.pallas_call(body, name=name, out_shape=shp)(lam_re, lam_im, log_dt, b_re, b_im, *cots)


def _scan_tile(h_ref, t_ref, carry_ref, up, hook=None, hook_init=None):
    sw = h_ref.shape[1] // 2
    ng = h_ref.shape[0] // SUBLANES

    def tab(i):
        return t_ref[SUBLANES * i:SUBLANES * (i + 1), :sw], t_ref[SUBLANES * i:SUBLANES * (i + 1), sw:]

    steps = ((1, tab(0)), (2, tab(1)), (4, tab(2)))
    p_re, p_im = tab(3)
    edge = 0 if up else SUBLANES - 1

    def group(j, state):
        carry, extra = state
        g = ng - 1 - j if up else j
        r0 = pl.multiple_of(g * SUBLANES, SUBLANES)
        xr = h_ref[pl.ds(r0, SUBLANES), :sw]
        xi = h_ref[pl.ds(r0, SUBLANES), sw:]
        for sh, (a_re, a_im) in steps:
            amount = SUBLANES - sh if up else sh
            sr = pltpu.roll(xr, amount, 0)
            si = pltpu.roll(xi, amount, 0)
            xr, xi = xr + a_re * sr - a_im * si, xi + a_re * si + a_im * sr
        cr, ci = carry
        xr, xi = xr + p_re * cr - p_im * ci, xi + p_re * ci + p_im * cr
        h_ref[pl.ds(r0, SUBLANES), :sw] = xr
        h_ref[pl.ds(r0, SUBLANES), sw:] = xi
        if hook is not None:
            extra = hook(r0, xr, xi, cr, ci, extra)
        new_carry = (jnp.broadcast_to(xr[edge:edge + 1], xr.shape), jnp.broadcast_to(xi[edge:edge + 1], xi.shape))
        return new_carry, extra

    carry0 = (carry_ref[:, :sw], carry_ref[:, sw:])
    carry, extra = lax.fori_loop(0, ng, group, (carry0, hook_init))
    carry_ref[:, :sw] = carry[0]
    carry_ref[:, sw:] = carry[1]
    return extra


def _s5_tile_index(cfg, dirn, adjoint):
    tpl, nb = cfg["tpl"], cfg["nb"]

    def idx(b, k):
        if not adjoint:
            lat = b * tpl + (k - 1 if dirn == 0 else tpl - k)
            return jnp.where(k == 0, nb * tpl + b, lat)
        lat = b * tpl + (tpl - 1 - k if dirn == 0 else k)
        return jnp.where(k == tpl, nb * tpl + b, lat)
    return idx


def _grid_ends(grid):
    ids = [pl.program_id(i) for i in range(len(grid))]
    first = functools.reduce(jnp.logical_and, [i == 0 for i in ids])
    last = functools.reduce(jnp.logical_and, [i == n - 1 for i, n in zip(ids, grid)])
    return first, last


def _s5_fwd(name, u, bmat, cmat, tab, dirn, cfg, cps=None):
    tr, tpl, nb = cfg["tr"], cfg["tpl"], cfg["nb"]
    n_rows, d = u.shape
    ns, _, sw2 = bmat.shape
    tile = _s5_tile_index(cfg, dirn, False)
    up = dirn == 1
    grid = (ns, nb, tpl + 1)

    def body(*refs):
        copies, (u_ref, b_ref, c_ref, t_ref, h_ref, y_ref, carry_ref) = _carried(cps, 4, 2, 1, refs)
        first, last = _grid_ends(grid)
        _start_all(copies, first)

        @pl.when(pl.program_id(2) == 0)
        def _():
            carry_ref[...] = jnp.zeros_like(carry_ref)

        h_ref[...] = jnp.dot(u_ref[...], b_ref[...], preferred_element_type=F32)
        _scan_tile(h_ref, t_ref, carry_ref, up)
        y_ref[...] = jnp.dot(h_ref[...].astype(BF16), c_ref[...], preferred_element_type=F32)
        _wait_all(copies, last)

    extra = cps if cps is not None else _Copies([], [])
    return pl.pallas_call(
        body, name=name, grid=grid,
        in_specs=[pl.BlockSpec((tr, LANES), lambda s, b, k: (tile(b, k), s)),
                  pl.BlockSpec((None, LANES, sw2), lambda s, b, k: (s, 0, 0)),
                  pl.BlockSpec((None, sw2, LANES), lambda s, b, k: (s, 0, 0)),
                  pl.BlockSpec((None, 4 * SUBLANES, sw2), lambda s, b, k: (s, 0, 0))] + extra.in_specs,
        out_specs=[pl.BlockSpec((tr, sw2), lambda s, b, k: (tile(b, k), s)),
                   pl.BlockSpec((tr, LANES), lambda s, b, k: (tile(b, k), s))] + extra.out_specs,
        out_shape=[jax.ShapeDtypeStruct((n_rows, ns * sw2), F32), jax.ShapeDtypeStruct((n_rows, d), F32)] + extra.out_shape,
        scratch_shapes=[pltpu.VMEM((SUBLANES, sw2), F32)] + extra.scratch,
        compiler_params=_params(("arbitrary", "arbitrary", "arbitrary")))(u, bmat, cmat, tab, *extra.arrays)


def _s5_bwd(name, dy, h, u, cmat_t, bmat_t, tab, dirn, cfg, cps=None):
    tr, tpl, nb = cfg["tr"], cfg["tpl"], cfg["nb"]
    n_rows, d = u.shape
    ns, _, sw2 = cmat_t.shape
    sw = sw2 // 2
    tile = _s5_tile_index(cfg, dirn, True)
    up = dirn == 0
    edge = SUBLANES - 1 if up else 0
    one = SUBLANES - 1 if up else 1

    grid = (ns, nb, tpl + 1)

    def body(*refs):
        copies, own = _carried(cps, 6, 4, 2, refs)
        dy_ref, h_ref, u_ref, ct_ref, bt_ref, t_ref, du_ref, db_ref, dc_ref, da_ref, lam_ref, carry_ref = own
        grid_first, grid_last = _grid_ends(grid)
        _start_all(copies, grid_first)
        first = jnp.logical_and(pl.program_id(1) == 0, pl.program_id(2) == 0)

        @pl.when(pl.program_id(2) == 0)
        def _():
            carry_ref[...] = jnp.zeros_like(carry_ref)

        lam_ref[...] = jnp.dot(dy_ref[...], ct_ref[...], preferred_element_type=F32)
        row_id = lax.broadcasted_iota(jnp.int32, (SUBLANES, sw), 0)

        def hook(r0, xr, xi, cr, ci, acc):
            lr = jnp.where(row_id == edge, cr, pltpu.roll(xr, one, 0))
            li = jnp.where(row_id == edge, ci, pltpu.roll(xi, one, 0))
            hr = h_ref[pl.ds(r0, SUBLANES), :sw]
            hi = h_ref[pl.ds(r0, SUBLANES), sw:]
            return acc[0] + lr * hr + li * hi, acc[1] + li * hr - lr * hi

        zero = jnp.zeros((SUBLANES, sw), F32)
        acc = _scan_tile(lam_ref, t_ref, carry_ref, up, hook, (zero, zero))
        lam = lam_ref[...].astype(BF16)
        d_b = lax.dot_general(u_ref[...], lam, TN, preferred_element_type=F32)
        d_c = lax.dot_general(h_ref[...].astype(BF16), dy_ref[...], TN, preferred_element_type=F32)
        du_ref[...] = jnp.dot(lam, bt_ref[...], preferred_element_type=F32)

        @pl.when(first)
        def _():
            db_ref[...] = d_b
            dc_ref[...] = d_c
            da_ref[:, :sw] = acc[0]
            da_ref[:, sw:] = acc[1]

        @pl.when(jnp.logical_not(first))
        def _():
            db_ref[...] += d_b
            dc_ref[...] += d_c
            da_ref[:, :sw] += acc[0]
            da_ref[:, sw:] += acc[1]

        _wait_all(copies, grid_last)

    extra = cps if cps is not None else _Copies([], [])
    return pl.pallas_call(
        body, name=name, grid=grid,
        in_specs=[pl.BlockSpec((tr, LANES), lambda s, b, k: (tile(b, k), s)),
                  pl.BlockSpec((tr, sw2), lambda s, b, k: (tile(b, k), s)),
                  pl.BlockSpec((tr, LANES), lambda s, b, k: (tile(b, k), s)),
                  pl.BlockSpec((None, LANES, sw2), lambda s, b, k: (s, 0, 0)),
                  pl.BlockSpec((None, sw2, LANES), lambda s, b, k: (s, 0, 0)),
                  pl.BlockSpec((None, 4 * SUBLANES, sw2), lambda s, b, k: (s, 0, 0))] + extra.in_specs,
        out_specs=[pl.BlockSpec((tr, LANES), lambda s, b, k: (tile(b, k), s)),
                   pl.BlockSpec((None, LANES, sw2), lambda s, b, k: (s, 0, 0)),
                   pl.BlockSpec((None, sw2, LANES), lambda s, b, k: (s, 0, 0)),
                   pl.BlockSpec((None, SUBLANES, sw2), lambda s, b, k: (s, 0, 0))] + extra.out_specs,
        out_shape=[jax.ShapeDtypeStruct((n_rows, d), F32), jax.ShapeDtypeStruct((ns, LANES, sw2), F32),
                   jax.ShapeDtypeStruct((ns, sw2, LANES), F32), jax.ShapeDtypeStruct((ns, SUBLANES, sw2), F32)] + extra.out_shape,
        scratch_shapes=[pltpu.VMEM((tr, sw2), F32), pltpu.VMEM((SUBLANES, sw2), F32)] + extra.scratch,
        compiler_params=_params(("arbitrary", "arbitrary", "arbitrary")))(dy, h, u, cmat_t, bmat_t, tab, *extra.arrays)


def _s5_tables(ab_re, ab_im, up, conj, ns):
    a_re = ab_re.reshape(ns, -1)
    a_im = (-ab_im if conj else ab_im).reshape(ns, -1)
    powers = [(a_re, a_im)]
    for _ in range(SUBLANES - 1):
        q_re, q_im = powers[-1]
        powers.append((q_re * a_re - q_im * a_im, q_re * a_im + q_im * a_re))
    rows = jnp.arange(SUBLANES)
    re_blocks, im_blocks = [], []
    for sh in (1, 2, 4):
        keep = (rows <= SUBLANES - 1 - sh) if up else (rows >= sh)
        q_re, q_im = powers[sh - 1]
        re_blocks.append(jnp.where(keep[None, :, None], q_re[:, None, :], 0.0))
        im_blocks.append(jnp.where(keep[None, :, None], q_im[:, None, :], 0.0))
    dist = range(SUBLANES, 0, -1) if up else range(1, SUBLANES + 1)
    re_blocks.append(jnp.stack([powers[dd - 1][0] for dd in dist], axis=1))
    im_blocks.append(jnp.stack([powers[dd - 1][1] for dd in dist], axis=1))
    return jnp.concatenate([jnp.concatenate(re_blocks, axis=1), jnp.concatenate(im_blocks, axis=1)], axis=2)


def _block_diag(blocks):
    ns, gs, a, b = blocks.shape
    eye = jnp.eye(gs, dtype=blocks.dtype)
    return (blocks[:, :, :, None, :] * eye[None, :, None, :, None]).reshape(ns, gs * a, gs * b)


def _diag_blocks(mat, gs):
    ns, ra, rb = mat.shape
    a, b = ra // gs, rb // gs
    m5 = mat.reshape(ns, gs, a, gs, b)
    eye = jnp.eye(gs, dtype=mat.dtype)
    return jnp.sum(m5 * eye[None, :, None, :, None], axis=3)


def _conv_flags(t, cfg):
    tpl, nb = cfg["tpl"], cfg["nb"]
    latent = t < nb * tpl
    first = jnp.logical_or(jnp.logical_not(latent), t % tpl == 0)
    last = jnp.logical_or(jnp.logical_not(latent), t % tpl == tpl - 1)
    return first, last


def _fill_ext(ext_ref, prev_ref, cur_ref, next_ref, t, cfg, halo):
    first, last = _conv_flags(t, cfg)
    tr = cur_ref.shape[0]
    ext_ref[0:halo, :] = jnp.where(first, 0.0, prev_ref[...])
    ext_ref[halo:halo + tr, :] = cur_ref[...]
    ext_ref[halo + tr:, :] = jnp.where(last, 0.0, next_ref[...])


CONV_LANES = 4 * LANES


def _conv_specs(tr, n_rows, halo, cw):
    per = tr // halo
    n_halo = n_rows // halo
    return [pl.BlockSpec((halo, cw), lambda c, t: (jnp.maximum(t * per - 1, 0), c)),
            pl.BlockSpec((tr, cw), lambda c, t: (t, c)),
            pl.BlockSpec((halo, cw), lambda c, t: (jnp.minimum((t + 1) * per, n_halo - 1), c))]


def _dwconv(name, a, w, cfg):
    tr = cfg["tr"]
    n_rows, d = a.shape
    kw = w.shape[0]
    half = kw // 2
    halo = 2 * SUBLANES
    cw = min(d, CONV_LANES)

    def body(prev_ref, cur_ref, next_ref, w_ref, o_ref, ext_ref):
        _fill_ext(ext_ref, prev_ref, cur_ref, next_ref, pl.program_id(1), cfg, halo)
        for lane in range(0, cw, LANES):
            acc = jnp.zeros((tr, LANES), F32)
            for k in range(kw):
                acc = acc + ext_ref[pl.ds(halo - half + k, tr), lane:lane + LANES] * w_ref[k:k + 1, lane:lane + LANES]
            o_ref[:, lane:lane + LANES] = acc

    return pl.pallas_call(body, name=name, grid=(d // cw, n_rows // tr),
                          in_specs=_conv_specs(tr, n_rows, halo, cw) + [pl.BlockSpec((kw, cw), lambda c, t: (0, c))],
                          out_specs=pl.BlockSpec((tr, cw), lambda c, t: (t, c)),
                          out_shape=jax.ShapeDtypeStruct((n_rows, d), F32),
                          scratch_shapes=[pltpu.VMEM((tr + 2 * halo, cw), F32)],
                          compiler_params=_params(("arbitrary", "arbitrary")))(a, a, a, w)


def _dwconv_wgrad(name, a, dout, kw, cfg):
    tr = cfg["tr"]
    n_rows, d = a.shape
    half = kw // 2
    halo = 2 * SUBLANES
    cw = min(d, CONV_LANES)

    def body(prev_ref, cur_ref, next_ref, do_ref, o_ref, ext_ref):
        t = pl.program_id(1)
        _fill_ext(ext_ref, prev_ref, cur_ref, next_ref, t, cfg, halo)
        for lane in range(0, cw, LANES):
            dout_t = do_ref[:, lane:lane + LANES]
            rows = [jnp.sum(ext_ref[pl.ds(halo - half + k, tr), lane:lane + LANES] * dout_t, axis=0, keepdims=True)
                    for k in range(kw)]
            _accumulate(o_ref.at[:, lane:lane + LANES], jnp.concatenate(rows, axis=0), t == 0)

    return pl.pallas_call(body, name=name, grid=(d // cw, n_rows // tr),
                          in_specs=_conv_specs(tr, n_rows, halo, cw) + [pl.BlockSpec((tr, cw), lambda c, t: (t, c))],
                          out_specs=pl.BlockSpec((kw, cw), lambda c, t: (0, c)),
                          out_shape=jax.ShapeDtypeStruct((kw, d), F32),
                          scratch_shapes=[pltpu.VMEM((tr + 2 * halo, cw), F32)],
                          compiler_params=_params(("arbitrary", "arbitrary")))(a, a, a, dout)


def _sincos_1d(pos, dim):
    quarter = dim // 2
    omega = POS_TEMP ** (-jnp.arange(quarter, dtype=F32) / quarter)
    ang = pos[:, None] * omega[None, :]
    return jnp.concatenate([jnp.sin(ang), jnp.cos(ang)], axis=-1)


def _grid_pos_embed(rows, dim):
    row_idx = jnp.repeat(jnp.arange(rows), GRID_W).astype(F32)
    col_idx = jnp.tile(jnp.arange(GRID_W), rows).astype(F32)
    return jnp.concatenate([_sincos_1d(row_idx, dim // 2), _sincos_1d(col_idx, dim // 2)], axis=-1)


def _pack(arrs, row_multiple=SUBLANES):
    flat = jnp.concatenate([a.reshape(-1).astype(F32) for a in arrs])
    pad = (-flat.shape[0]) % (row_multiple * LANES)
    return jnp.pad(flat, (0, pad)).reshape(-1, LANES)


def _unpack(buf, shapes):
    flat = buf.reshape(-1)
    out, pos = [], 0
    for shp in shapes:
        n = math.prod(shp)
        out.append(flat[pos:pos + n].reshape(shp))
        pos += n
    return out


def _unpack_gathered(buf, shapes):
    flat = buf.reshape(N_DEV, -1)
    out, pos = [], 0
    for shp in shapes:
        n = math.prod(shp)
        part = flat[:, pos:pos + n].reshape((N_DEV,) + tuple(shp))
        out.append(jnp.moveaxis(part, 0, -2).reshape(tuple(shp[:-1]) + (N_DEV * shp[-1],)))
        pos += n
    return out


WEIGHTS = ("c_ctx", "w_ada", "b_ada", "ln_gain", "ln_bias", "s5_lam_re", "s5_lam_im", "s5_log_dt", "s5_b_re", "s5_b_im",
           "s5_c_re", "s5_c_im", "s5_d", "s5_w_glu", "s5_b_glu", "cv_w_pw1", "cv_b_pw1", "cv_w_dw", "cv_b_dw", "cv_ln_g",
           "cv_ln_b", "cv_w_pw2", "cv_b_pw2", "mlp_w1", "mlp_w2")
SHARDED_SMALL = ("ln_gain", "ln_bias", "cv_b_pw1", "cv_w_dw", "cv_b_dw", "cv_ln_g", "cv_ln_b", "cv_b_pw2")
REPLICATED_SMALL = ("s5_lam_re", "s5_lam_im", "s5_log_dt", "s5_b_re", "s5_b_im", "s5_c_re", "s5_c_im", "s5_d", "s5_b_glu")
BIG = ("mlp_w1", "mlp_w2", "s5_w_glu", "cv_w_pw1", "cv_w_pw2")


def _step(a):
    x, c, ctx = a["x"], a["c"], a["ctx"]
    nb, seq, d = x.shape
    lc = ctx.shape[1]
    nl = a["w_ada"].shape[0]
    tr = lc
    tpl = seq // tr
    cfg = {"tr": tr, "tpl": tpl, "nb": nb}
    n_rows = nb * (seq + lc)
    alpha = (2.0 * nl) ** 0.25
    me = 4 * lax.axis_index("x") + 2 * lax.axis_index("y") + lax.axis_index("c")
    n_grp, n_state = a["s5_lam_re"].shape[2:]
    ch = a["s5_b_re"].shape[-1]
    gs = LANES // ch
    ns = d // LANES
    tm = 2 * tr if n_rows % (2 * tr) == 0 else tr
    tm_big = n_rows // 3 if n_rows % (3 * 2 * SUBLANES) == 0 else tm
    f_sub1_s5, f_sub1_cv, f_sub2 = _make_sub1_s5(alpha), _make_sub1_cv(alpha), _make_sub2(alpha)

    def layer_weights(i):
        mixer = [("s5_w_glu", i // 2, 1)] if i % 2 == 0 else [("cv_w_pw1", i // 2, 1), ("cv_w_pw2", i // 2, 0)]
        return mixer + [("mlp_w1", i, 1), ("mlp_w2", i, 0)]

    weights, wgrads, received = {}, {}, {}
    small_all = _exchange("gather_small", [_pack([a[n] for n in SHARDED_SMALL])], ["slot"])[0]
    full = dict(zip(SHARDED_SMALL, _unpack_gathered(small_all, [a[n].shape for n in SHARDED_SMALL])))
    c_all = _exchange("gather_c", [c], ["slot"])[0].reshape(N_DEV * nb, d)
    cond_rows = N_DEV * nb + SUBLANES
    cc = jnp.concatenate([c_all, a["c_ctx"][None], jnp.zeros((SUBLANES - 1, d), F32)], axis=0)

    n_ada = a["w_ada"].shape[2]
    b_loc = lax.dynamic_slice(a["b_ada"], (0, me * n_ada), (nl, n_ada))[:, None, :]
    mod_cols = _ada_fwd(cc, a["w_ada"], b_loc)
    mod_all = _exchange("gather_mod", [mod_cols.reshape(nl * cond_rows, n_ada)], ["slot"])[0].reshape(N_DEV, nl, cond_rows, n_ada)
    mod_mine = jnp.concatenate([lax.dynamic_slice(mod_all, (0, 0, nb * me, 0), (N_DEV, nl, nb, n_ada)),
                                mod_all[:, :, N_DEV * nb:N_DEV * nb + 1]], axis=2)
    mod = jnp.transpose(mod_mine, (1, 2, 0, 3)).reshape(nl, nb + 1, 6, 1, d)

    def seg(i, q):
        return mod[i, :, q]

    zero_seg = jnp.zeros((nb + 1, 1, d), F32)

    def vec(v):
        return v.reshape(1, -1)

    pos = _grid_pos_embed(seq // GRID_W, d)
    xc = jnp.concatenate([x.reshape(nb * seq, d), ctx.reshape(nb * lc, d)], axis=0)
    pos_rows = jnp.concatenate([jnp.tile(pos, (nb, 1)), jnp.zeros((nb * lc, d), F32)], axis=0)
    x_cur, h_cur = _rowwise("entry", _f_entry, [xc, pos_rows], [seg(0, 0), seg(0, 1)], [], [(d, F32), (d, BF16)], [], [], cfg)
    saved = []
    for i in range(nl):
        j = i // 2
        sv = {"x": x_cur, "h": h_cur}
        sh1, sc1, g1, sh2, sc2, g2 = (seg(i, q) for q in range(6))
        gain0, bias0, gain1, bias1 = (vec(full["ln_gain"][i, 0]), vec(full["ln_bias"][i, 0]),
                                      vec(full["ln_gain"][i, 1]), vec(full["ln_bias"][i, 1]))
        if i % 2 == 0:
            lam_re, lam_im = a["s5_lam_re"][j], a["s5_lam_im"][j]
            log_dt = a["s5_log_dt"][j][:, :, None]
            b_re_t = jnp.transpose(a["s5_b_re"][j], (0, 3, 1, 2))
            b_im_t = jnp.transpose(a["s5_b_im"][j], (0, 3, 1, 2))
            sv["prep_in"] = (lam_re, lam_im, log_dt, b_re_t, b_im_t)
            ab_re, ab_im, bb_re, bb_im = _s5_prep(f"s5_prep{i}", *sv["prep_in"])
            sv["ab"] = (ab_re, ab_im)
            ys = []
            for dirn in range(2):
                def blocks(t):
                    return jnp.transpose(t, (1, 0, 2)).reshape(ns, gs, ch, n_state)
                bmat = jnp.concatenate([_block_diag(blocks(bb_re[dirn])), _block_diag(blocks(bb_im[dirn]))], axis=2).astype(BF16)
                c_re_t = jnp.transpose(a["s5_c_re"][j, dirn], (0, 2, 1)).reshape(ns, gs, n_state, ch)
                c_im_t = jnp.transpose(a["s5_c_im"][j, dirn], (0, 2, 1)).reshape(ns, gs, n_state, ch)
                cmat = jnp.concatenate([_block_diag(c_re_t), -_block_diag(c_im_t)], axis=1).astype(BF16)
                tab = _s5_tables(ab_re[dirn], ab_im[dirn], dirn == 1, False, ns)
                group = layer_weights(i + dirn)
                cps = _Copies([a[n][idx].astype(BF16) for n, idx, _ in group], [axis for _, _, axis in group])
                h_states, y_dir, *gathered = _s5_fwd(f"s5_fwd{i}_{dirn}", h_cur, bmat, cmat, tab, dirn, cfg, cps)
                weights.update({(n, idx): w[None] for (n, idx, _), w in zip(group, gathered)})
                sv[f"mats{dirn}"] = (jnp.transpose(bmat, (0, 2, 1)), jnp.transpose(cmat, (0, 2, 1)))
                sv[f"states{dirn}"] = h_states
                ys.append(y_dir)
            sv["y"] = ys
            dsk = vec(a["s5_d"][j])
            z = _rowwise(f"gelu{i}", _f_gelu, [x_cur, ys[0], ys[1]], [sh1, sc1], [dsk], [(d, BF16)], [], [], cfg)[0]
            zz = _mm_nn(f"glu{i}", z, weights["s5_w_glu", j], 0, tm_big, min(2 * d, 512))[0]
            bglu = vec(a["s5_b_glu"][j])
            x1, h2 = _rowwise(f"sub1_{i}", f_sub1_s5, [x_cur, zz], [g1, sh2, sc2], [bglu, gain0, bias0],
                              [(d, F32), (d, BF16)], [], [], cfg)
            sv.update(z=z, zz=zz)
        else:
            zz = _mm_nn(f"pw1_{i}", h_cur, weights["cv_w_pw1", j], 0, tm_big, min(2 * d, 512))[0]
            bpw1 = vec(full["cv_b_pw1"][j])
            act = _rowwise(f"cvglu{i}", _f_cvglu, [zz], [], [bpw1], [(d, F32)], [], [], cfg)[0]
            w_dw = full["cv_w_dw"][j]
            cv = _dwconv(f"dwconv{i}", act, w_dw, cfg)
            bdw, lng, lnb = vec(full["cv_b_dw"][j]), vec(full["cv_ln_g"][j]), vec(full["cv_ln_b"][j])
            s_act = _rowwise(f"cvln{i}", _f_cvln, [cv], [], [bdw, lng, lnb], [(d, BF16)], [], [], cfg)[0]
            mm = _mm_nn(f"pw2_{i}", s_act, weights["cv_w_pw2", j], 0, tm_big, d)[0]
            bpw2 = vec(full["cv_b_pw2"][j])
            x1, h2 = _rowwise(f"sub1_{i}", f_sub1_cv, [x_cur, mm], [g1, sh2, sc2], [bpw2, gain0, bias0],
                              [(d, F32), (d, BF16)], [], [], cfg)
            sv.update(zz=zz, act=act, cv=cv, s_act=s_act, mm=mm, w_dw=w_dw)
        dff = weights["mlp_w1", i].shape[2]
        p_act, r_act = _mm_nn(f"mlp1_{i}", h2, weights["mlp_w1", i], 0, tm_big, min(dff, 1024), (BF16, BF16),
                              lambda acc: (jnp.square(jnp.maximum(acc, 0.0)), jnp.maximum(acc, 0.0)))
        m_out = _mm_nn(f"mlp2_{i}", p_act, weights["mlp_w2", i], 0, tm, d)[0]
        shn, scn = (seg(i + 1, 0), seg(i + 1, 1)) if i + 1 < nl else (zero_seg, zero_seg)
        x2, hn = _rowwise(f"sub2_{i}", f_sub2, [x1, m_out], [g2, shn, scn], [gain1, bias1], [(d, F32), (d, BF16)], [], [], cfg)
        sv.update(x1=x1, h2=h2, p=p_act, r=r_act, m=m_out, shn=shn, scn=scn)
        saved.append(sv)
        x_cur, h_cur = x2, hn

    target = jnp.concatenate([a["loss_target"].reshape(nb * seq, d), jnp.zeros((nb * lc, d), F32)], axis=0)
    mask = jnp.concatenate([jnp.ones((nb, 1, d), F32), jnp.zeros((1, 1, d), F32)], axis=0)

    def f_loss(xf, tgt, msk):
        err = (xf - tgt) * msk
        part = 0.5 * jnp.sum(jnp.square(err), axis=(0, 1), keepdims=True) / d
        return err / d, jnp.broadcast_to(part, (1, LANES))

    dx_final, loss_part = _rowwise("loss", f_loss, [x_cur, target], [mask], [], [(d, F32)], [], [LANES], cfg)
    loss = lax.psum(loss_part[0, 0], ("x", "y", "c"))

    grads = {n: [None] * a[n].shape[0] for n in WEIGHTS if n not in ("c_ctx", "w_ada", "b_ada")}
    dmod = [[None] * 6 for _ in range(nl)]

    def add_mod(i, q, val):
        dmod[i][q] = val if dmod[i][q] is None else dmod[i][q] + val

    dx_parts, dh_parts = [dx_final], []
    for i in reversed(range(nl)):
        j = i // 2
        sv = saved[i]
        sh1, sc1, g1, sh2, sc2, g2 = (seg(i, q) for q in range(6))
        gain0, bias0, gain1, bias1 = (vec(full["ln_gain"][i, 0]), vec(full["ln_bias"][i, 0]),
                                      vec(full["ln_gain"][i, 1]), vec(full["ln_bias"][i, 1]))
        bwd = _vjp_fn(f_sub2, 2, (len(dx_parts), len(dh_parts)), (0, 1, 2, 3, 4, 5, 6))
        dx1, dm, dg2, dshn, dscn, dgain1, dbias1 = _rowwise(
            f"sub2_bwd{i}", bwd, [sv["x1"], sv["m"]] + dx_parts + dh_parts, [g2, sv["shn"], sv["scn"]], [gain1, bias1],
            [(d, F32), (d, BF16)], [d, d, d], [d, d], cfg)
        add_mod(i, 5, dg2)
        if i + 1 < nl:
            add_mod(i + 1, 0, dshn)
            add_mod(i + 1, 1, dscn)
        da = _mm_nt(f"mlp2_dgrad{i}", dm, weights["mlp_w2", i], 0, tm_big, min(dff, 1024), [sv["r"]], BF16,
                    lambda acc, r: (acc * 2.0 * r,))
        wgrads["mlp_w2", i] = _mm_wgrad_rows(f"mlp2_wgrad{i}", sv["p"], dm, tm_big)
        wgrads["mlp_w1", i] = _mm_wgrad_cols(f"mlp1_wgrad{i}", sv["h2"], da, tm_big)
        dh2 = _mm_nt(f"mlp1_dgrad{i}", da, weights["mlp_w1", i], 0, tm, d)
        if i % 2 == 0:
            bglu = vec(a["s5_b_glu"][j])
            bwd = _vjp_fn(f_sub1_s5, 2, (1, 1), (0, 1, 2, 3, 4, 5, 6, 7))
            dxa, dzz, dg1, dsh2, dsc2, dbglu, dgain0, dbias0 = _rowwise(
                f"sub1_bwd{i}", bwd, [sv["x"], sv["zz"], dx1, dh2], [g1, sh2, sc2], [bglu, gain0, bias0],
                [(d, F32), (2 * d, BF16)], [d, d, d], [2 * d, d, d], cfg)
            grads["s5_b_glu"][j] = dbglu[0]
            wgrads["s5_w_glu", j] = _mm_wgrad_cols(f"glu_wgrad{i}", sv["z"], dzz, tm_big)
            dz = _mm_nt(f"glu_dgrad{i}", dzz, weights["s5_w_glu", j], 0, tm, d)
            dsk = vec(a["s5_d"][j])
            bwd = _vjp_fn(_f_gelu, 3, (1,), (0, 1, 3, 4, 5))
            dxb, dy, dsh1, dsc1, ddsk = _rowwise(f"gelu_bwd{i}", bwd, [sv["x"], sv["y"][0], sv["y"][1], dz], [sh1, sc1], [dsk],
                                                 [(d, F32), (d, BF16)], [d, d], [d], cfg)
            grads["s5_d"][j] = ddsk[0]
            add_mod(i, 0, dsh1)
            add_mod(i, 1, dsc1)
            ab_re, ab_im = sv["ab"]
            dus, d_ab_re, d_ab_im, d_bb_re, d_bb_im, d_c_re, d_c_im = [], [], [], [], [], [], []
            for dirn in range(2):
                bmat_t, cmat_t = sv[f"mats{dirn}"]
                tab = _s5_tables(ab_re[dirn], ab_im[dirn], dirn == 0, True, ns)
                group = layer_weights(i + 1 - dirn)
                cps = _Copies([wgrads[n, idx] for n, idx, _ in group], ["scatter"] * len(group))
                du, d_b, d_c, d_a, *parts = _s5_bwd(f"s5_bwd{i}_{dirn}", dy, sv[f"states{dirn}"], sv["h"], cmat_t, bmat_t, tab,
                                                    dirn, cfg, cps)
                received.update({(n, idx): p for (n, idx, _), p in zip(group, parts)})
                dus.append(du)
                sw = d_a.shape[2] // 2
                d_a = jnp.sum(d_a, axis=1)
                d_ab_re.append(d_a[:, :sw].reshape(n_grp, n_state))
                d_ab_im.append(d_a[:, sw:].reshape(n_grp, n_state))

                def unblock_b(t):
                    return jnp.transpose(_diag_blocks(t, gs).reshape(n_grp, ch, n_state), (1, 0, 2))

                def unblock_c(t):
                    return jnp.transpose(_diag_blocks(t, gs).reshape(n_grp, n_state, ch), (0, 2, 1))
                d_bb_re.append(unblock_b(d_b[:, :, :sw]))
                d_bb_im.append(unblock_b(d_b[:, :, sw:]))
                d_c_re.append(unblock_c(d_c[:, :sw]))
                d_c_im.append(-unblock_c(d_c[:, sw:]))
            g_lre, g_lim, g_ldt, g_bre, g_bim = _s5_prep_bwd(
                f"s5_prep_bwd{i}", *sv["prep_in"], (jnp.stack(d_ab_re), jnp.stack(d_ab_im), jnp.stack(d_bb_re), jnp.stack(d_bb_im)))
            grads["s5_lam_re"][j], grads["s5_lam_im"][j], grads["s5_log_dt"][j] = g_lre, g_lim, g_ldt[:, :, 0]
            grads["s5_b_re"][j] = jnp.transpose(g_bre, (0, 2, 3, 1))
            grads["s5_b_im"][j] = jnp.transpose(g_bim, (0, 2, 3, 1))
            grads["s5_c_re"][j], grads["s5_c_im"][j] = jnp.stack(d_c_re), jnp.stack(d_c_im)
            dx_parts, dh_parts = [dxa, dxb], dus
        else:
            bpw2 = vec(full["cv_b_pw2"][j])
            bwd = _vjp_fn(f_sub1_cv, 2, (1, 1), (0, 1, 2, 3, 4, 5, 6, 7))
            dxa, dmm, dg1, dsh2, dsc2, dbpw2, dgain0, dbias0 = _rowwise(
                f"sub1_bwd{i}", bwd, [sv["x"], sv["mm"], dx1, dh2], [g1, sh2, sc2], [bpw2, gain0, bias0],
                [(d, F32), (d, BF16)], [d, d, d], [d, d, d], cfg)
            grads["cv_b_pw2"][j] = dbpw2[0]
            wgrads["cv_w_pw2", j] = _mm_wgrad_rows(f"pw2_wgrad{i}", sv["s_act"], dmm, tm_big)
            ds = _mm_nt(f"pw2_dgrad{i}", dmm, weights["cv_w_pw2", j], 0, tm_big, d)
            bdw, lng, lnb = vec(full["cv_b_dw"][j]), vec(full["cv_ln_g"][j]), vec(full["cv_ln_b"][j])
            bwd = _vjp_fn(_f_cvln, 1, (1,), (0, 1, 2, 3))
            dcv, dbdw, dlng, dlnb = _rowwise(f"cvln_bwd{i}", bwd, [sv["cv"], ds], [], [bdw, lng, lnb], [(d, F32)], [], [d, d, d], cfg)
            grads["cv_b_dw"][j], grads["cv_ln_g"][j], grads["cv_ln_b"][j] = dbdw[0], dlng[0], dlnb[0]
            dact = _dwconv(f"dwconv_bwd{i}", dcv, sv["w_dw"][::-1], cfg)
            grads["cv_w_dw"][j] = _dwconv_wgrad(f"dwconv_wgrad{i}", sv["act"], dcv, sv["w_dw"].shape[0], cfg)
            bpw1 = vec(full["cv_b_pw1"][j])
            bwd = _vjp_fn(_f_cvglu, 1, (1,), (0, 1))
            dzz, dbpw1 = _rowwise(f"cvglu_bwd{i}", bwd, [sv["zz"], dact], [], [bpw1], [(2 * d, BF16)], [], [2 * d], cfg)
            grads["cv_b_pw1"][j] = dbpw1[0]
            wgrads["cv_w_pw1", j] = _mm_wgrad_cols(f"pw1_wgrad{i}", sv["h"], dzz, tm_big)
            dh = _mm_nt(f"pw1_dgrad{i}", dzz, weights["cv_w_pw1", j], 0, tm, d)
            dx_parts, dh_parts = [dxa], [dh]
        grads["ln_gain"][i] = jnp.stack([dgain0[0], dgain1[0]])
        grads["ln_bias"][i] = jnp.stack([dbias0[0], dbias1[0]])
        add_mod(i, 2, dg1)
        add_mod(i, 3, dsh2)
        add_mod(i, 4, dsc2)
    bwd = _vjp_fn(_f_entry, 2, (len(dx_parts), len(dh_parts)), (0, 2, 3))
    dxc, dsh1, dsc1 = _rowwise("entry_bwd", bwd, [xc, pos_rows] + dx_parts + dh_parts, [seg(0, 0), seg(0, 1)], [],
                               [(d, F32)], [d, d], [], cfg)
    add_mod(0, 0, dsh1)
    add_mod(0, 1, dsc1)
    grad_x = dxc[:nb * seq].reshape(nb, seq, d)

    dmod_loc = jnp.stack([jnp.concatenate([q[:, 0] for q in dmod[i]], axis=1) for i in range(nl)])
    dmod_all = _exchange("gather_dmod", [dmod_loc.reshape(nl * (nb + 1), 6 * d)], ["slot"])[0].reshape(N_DEV, nl, nb + 1, 6 * d)
    mine = lax.dynamic_slice(dmod_all, (0, 0, 0, me * n_ada), (N_DEV, nl, nb + 1, n_ada))
    dmod_rows = jnp.transpose(mine[:, :, :nb], (1, 0, 2, 3)).reshape(nl, N_DEV * nb, n_ada)
    dmod_rows = jnp.concatenate([dmod_rows, jnp.zeros((nl, SUBLANES, n_ada), F32)], axis=1)
    g_w_ada, _, dcond = _ada_bwd(cc, a["w_ada"], dmod_rows, mine[:, :, nb:])
    g_b_ada = _sum_lead("b_ada_sum", jnp.transpose(dmod_all, (0, 2, 1, 3)).reshape(N_DEV * (nb + 1), nl, 6 * d), nl)
    dcond_all = _exchange("gather_dcond", [dcond[N_DEV * nb:N_DEV * nb + 1]], ["slot"])[0]
    g_c_ctx = _cctx_grad(dcond_all, a["c_ctx"][None])[0]

    small_names = SHARDED_SMALL + REPLICATED_SMALL
    small_full = [jnp.stack(grads[n]) for n in small_names]
    small_packed = _pack(small_full, N_DEV * SUBLANES)
    small_parts = _exchange("scatter_small_grads", [small_packed.reshape(N_DEV, -1, LANES)], ["scatter"])[0]
    small_part = _sum_lead("small_grad_sum", small_parts, _row_tile(small_parts.shape[1], 512))
    small_sum = _exchange("gather_small_sum", [small_part], ["slot"])[0]
    small_g = dict(zip(small_names, _unpack(small_sum, [g.shape for g in small_full])))
    for n in SHARDED_SMALL:
        width = a[n].shape[-1]
        start = (0,) * (small_g[n].ndim - 1) + (me * width,)
        small_g[n] = lax.dynamic_slice(small_g[n], start, a[n].shape)
    small_g["c_ctx"], small_g["b_ada"] = g_c_ctx, g_b_ada

    out = {}

    def update(n, parts):
        shp = a[n].shape
        cols = parts.shape[-1]
        rows = parts.shape[1]
        res = _adamw(f"adamw_{n}", parts, a[n].reshape(rows, cols), a["m_" + n].reshape(rows, cols), a["v_" + n].reshape(rows, cols),
                     _row_tile(rows, max(SUBLANES, 131072 // cols)))
        out[n] = [r.reshape(shp) for r in res]

    for n in BIG:
        rows, cols = a[n].shape[1:]
        bufs = [lax.empty(a[n].shape, F32) for _ in range(4)]
        for idx in range(a[n].shape[0]):
            bufs = _adamw_layer(f"adamw_{n}{idx}", received[n, idx], a[n], a["m_" + n], a["v_" + n], bufs, idx,
                                _row_tile(rows, max(SUBLANES, 131072 // cols)))
        out[n] = bufs
    update("w_ada", g_w_ada.reshape(1, -1, n_ada))
    small_all_names = ("c_ctx", "b_ada") + small_names
    packed = [_pack([src[n] for n in small_all_names]) for src in
              (small_g, a, {n: a["m_" + n] for n in small_all_names}, {n: a["v_" + n] for n in small_all_names})]
    res = _adamw("adamw_small", packed[0][None], packed[1], packed[2], packed[3], _row_tile(packed[0].shape[0], 512))
    shapes = [a[n].shape for n in small_all_names]
    for n, vals in zip(small_all_names, zip(*[_unpack(r, shapes) for r in res])):
        out[n] = list(vals)
    return (loss, grad_x, *[out[n][0] for n in WEIGHTS], *[out[n][1] for n in WEIGHTS],
            *[out[n][2] for n in WEIGHTS], *[out[n][3] for n in WEIGHTS])


def kernel(x, c, ctx, c_ctx, w_ada, b_ada, ln_gain, ln_bias, s5_lam_re, s5_lam_im, s5_log_dt, s5_b_re, s5_b_im, s5_c_re, s5_c_im, s5_d, s5_w_glu, s5_b_glu, cv_w_pw1, cv_b_pw1, cv_w_dw, cv_b_dw, cv_ln_g, cv_ln_b, cv_w_pw2, cv_b_pw2, mlp_w1, mlp_w2, loss_target, m_c_ctx, m_w_ada, m_b_ada, m_ln_gain, m_ln_bias, m_s5_lam_re, m_s5_lam_im, m_s5_log_dt, m_s5_b_re, m_s5_b_im, m_s5_c_re, m_s5_c_im, m_s5_d, m_s5_w_glu, m_s5_b_glu, m_cv_w_pw1, m_cv_b_pw1, m_cv_w_dw, m_cv_b_dw, m_cv_ln_g, m_cv_ln_b, m_cv_w_pw2, m_cv_b_pw2, m_mlp_w1, m_mlp_w2, v_c_ctx, v_w_ada, v_b_ada, v_ln_gain, v_ln_bias, v_s5_lam_re, v_s5_lam_im, v_s5_log_dt, v_s5_b_re, v_s5_b_im, v_s5_c_re, v_s5_c_im, v_s5_d, v_s5_w_glu, v_s5_b_glu, v_cv_w_pw1, v_cv_b_pw1, v_cv_w_dw, v_cv_b_dw, v_cv_ln_g, v_cv_ln_b, v_cv_w_pw2, v_cv_b_pw2, v_mlp_w1, v_mlp_w2):
    return _step(dict(locals()))
```

```python
import functools
import math

import jax
import jax.numpy as jnp
from jax import lax
from jax.experimental import pallas as pl
from jax.experimental.pallas import tpu as pltpu

F32 = jnp.float32
BF16 = jnp.bfloat16
N_DEV = 8
LANES = 128
SUBLANES = 8
VMEM_LIMIT = 56 * 1024 * 1024
GRID_W = 64
POS_TEMP = 10000.0
LN_EPS = 1e-5
LAMBDA_RE_MAX = -1e-4
ADAM_LR, ADAM_B1, ADAM_B2, ADAM_EPS, ADAM_WD, ADAM_STEP = 0.001, 0.9, 0.999, 1e-08, 0.01, 10
MESH = pl.DeviceIdType.MESH


def _params(sem):
    return pltpu.CompilerParams(dimension_semantics=sem, vmem_limit_bytes=VMEM_LIMIT)


def _accumulate(ref, val, first):
    @pl.when(first)
    def _():
        ref[...] = val

    @pl.when(jnp.logical_not(first))
    def _():
        ref[...] += val


def _rowwise(name, fn, rows, segs, vecs, row_outs, seg_accs, vec_accs, cfg):
    tr, tpl, nb = cfg["tr"], cfg["tpl"], cfg["nb"]
    n_rows = rows[0].shape[0]
    nt = n_rows // tr
    nr, ns, nv = len(rows), len(segs), len(vecs)
    nro, nsa = len(row_outs), len(seg_accs)

    def seg_of(t):
        return jnp.minimum(t // tpl, nb)

    def body(*refs):
        t = pl.program_id(0)
        ins, outs = refs[:nr + ns + nv], refs[nr + ns + nv:]
        vals = [r[...] for r in ins[:nr]] + [r[0] for r in ins[nr:nr + ns]] + [r[...] for r in ins[nr + ns:]]
        res = fn(*vals)
        for o, v in zip(outs[:nro], res[:nro]):
            o[...] = v.astype(o.dtype)
        first_seg = jnp.logical_or(t == 0, seg_of(t) != seg_of(jnp.maximum(t - 1, 0)))
        for o, v in zip(outs[nro:nro + nsa], res[nro:nro + nsa]):
            _accumulate(o.at[0], v, first_seg)
        for o, v in zip(outs[nro + nsa:], res[nro + nsa:]):
            _accumulate(o, v, t == 0)

    in_specs = ([pl.BlockSpec((tr, a.shape[1]), lambda t: (t, 0)) for a in rows]
                + [pl.BlockSpec((1, 1, a.shape[2]), lambda t: (seg_of(t), 0, 0)) for a in segs]
                + [pl.BlockSpec((1, a.shape[1]), lambda t: (0, 0)) for a in vecs])
    out_specs = ([pl.BlockSpec((tr, c), lambda t: (t, 0)) for c, _ in row_outs]
                 + [pl.BlockSpec((1, 1, c), lambda t: (seg_of(t), 0, 0)) for c in seg_accs]
                 + [pl.BlockSpec((1, c), lambda t: (0, 0)) for c in vec_accs])
    out_shape = ([jax.ShapeDtypeStruct((n_rows, c), dt) for c, dt in row_outs]
                 + [jax.ShapeDtypeStruct((nb + 1, 1, c), F32) for c in seg_accs]
                 + [jax.ShapeDtypeStruct((1, c), F32) for c in vec_accs])
    return pl.pallas_call(body, name=name, grid=(nt,), in_specs=in_specs, out_specs=out_specs,
                          out_shape=out_shape, compiler_params=_params(("arbitrary",)))(*rows, *segs, *vecs)


def _vjp_fn(fn, n_row, cot_groups, want):
    n_cot = sum(cot_groups)

    def bwd(*args):
        primals = [a.astype(F32) for a in args[:n_row] + args[n_row + n_cot:]]
        outs, vjp = jax.vjp(fn, *primals)
        cots, pos = [], n_row
        for n, o in zip(cot_groups, outs):
            cot = jnp.zeros_like(o)
            for part in args[pos:pos + n]:
                cot = cot + part.astype(F32)
            cots.append(cot)
            pos += n
        grads = vjp(tuple(cots))
        return tuple(grads[i] for i in want)
    return bwd


def _ln(r, g, b):
    mu = jnp.mean(r, axis=-1, keepdims=True)
    var = jnp.mean(jnp.square(r - mu), axis=-1, keepdims=True)
    return (r - mu) * lax.rsqrt(var + LN_EPS) * g + b


def _glu(zz, bias):
    d = zz.shape[1] // 2
    return (zz[:, :d] + bias[:, :d]) * jax.nn.sigmoid(zz[:, d:] + bias[:, d:])


def _f_entry(xc, pos, sh, sc):
    x0 = xc + pos
    return x0, x0 * (1 + sc) + sh


def _f_gelu(x, y0, y1, sh, sc, dsk):
    u = x * (1 + sc) + sh
    y = dsk * u + y0 + y1
    return (0.5 * y * (1.0 + lax.erf(y * (2.0 ** -0.5))),)


def _make_sub1_s5(alpha):
    def f(x, zz, g1, sh2, sc2, bglu, gain, bias):
        x1 = _ln(alpha * x + g1 * _glu(zz, bglu), gain, bias)
        return x1, x1 * (1 + sc2) + sh2
    return f


def _make_sub1_cv(alpha):
    def f(x, mm, g1, sh2, sc2, bpw2, gain, bias):
        x1 = _ln(alpha * x + g1 * (mm + bpw2), gain, bias)
        return x1, x1 * (1 + sc2) + sh2
    return f


def _make_sub2(alpha):
    def f(x1, m, g2, shn, scn, gain, bias):
        x2 = _ln(alpha * x1 + g2 * m, gain, bias)
        return x2, x2 * (1 + scn) + shn
    return f


def _f_cvglu(zz, bpw1):
    return (_glu(zz, bpw1),)


def _f_cvln(cv, bdw, lng, lnb):
    return (jax.nn.silu(_ln(cv + bdw, lng, lnb)),)


def _matmul(name, a, b, extras, grid, a_spec, b_spec, extra_specs, o_specs, out_shape, dims, red_axis, epi, sem):
    n_extra = len(extras)
    n_out = len(out_shape)
    acc_shape = o_specs[0].block_shape
    acc_shape = tuple(s for s in acc_shape if s is not None)

    def body(*refs):
        a_ref, b_ref = refs[0], refs[1]
        ex = refs[2:2 + n_extra]
        outs = refs[2 + n_extra:2 + n_extra + n_out]
        prod = lax.dot_general(a_ref[...], b_ref[...], dims, preferred_element_type=F32)

        def finish(acc):
            res = epi(acc, *[e[...] for e in ex]) if epi is not None else (acc,)
            for o, v in zip(outs, res):
                o[...] = v.astype(o.dtype)

        if red_axis is None:
            finish(prod)
        else:
            acc_ref = refs[-1]
            k = pl.program_id(red_axis)
            nk = pl.num_programs(red_axis)

            @pl.when(k == 0)
            def _():
                acc_ref[...] = prod

            @pl.when(k > 0)
            def _():
                acc_ref[...] += prod

            @pl.when(k == nk - 1)
            def _():
                finish(acc_ref[...])

    scratch = [] if red_axis is None else [pltpu.VMEM(acc_shape, F32)]
    res = pl.pallas_call(body, name=name, grid=grid, in_specs=[a_spec, b_spec] + list(extra_specs),
                         out_specs=list(o_specs), out_shape=list(out_shape), scratch_shapes=scratch,
                         compiler_params=_params(sem))(a, b, *extras)
    return res


NN = (((1,), (0,)), ((), ()))
NT = (((1,), (1,)), ((), ()))
TN = (((0,), (0,)), ((), ()))


def _mm_nn(name, a, w3, layer, tm, tn, out_dtypes=(F32,), epi=None):
    m, k = a.shape
    n = w3.shape[2]
    return _matmul(name, a, w3, (), (n // tn, m // tm),
                   pl.BlockSpec((tm, k), lambda j, i: (i, 0)), pl.BlockSpec((None, k, tn), lambda j, i: (layer, 0, j)), (),
                   [pl.BlockSpec((tm, tn), lambda j, i: (i, j)) for _ in out_dtypes],
                   [jax.ShapeDtypeStruct((m, n), dt) for dt in out_dtypes], NN, None, epi, ("arbitrary", "arbitrary"))


def _mm_nt(name, dy, w3, layer, tm, tkw, extras=(), out_dtype=F32, epi=None):
    m, n = dy.shape
    kw = w3.shape[1]
    return _matmul(name, dy, w3, tuple(extras), (kw // tkw, m // tm),
                   pl.BlockSpec((tm, n), lambda j, i: (i, 0)), pl.BlockSpec((None, tkw, n), lambda j, i: (layer, j, 0)),
                   [pl.BlockSpec((tm, tkw), lambda j, i: (i, j)) for _ in extras],
                   [pl.BlockSpec((tm, tkw), lambda j, i: (i, j))], [jax.ShapeDtypeStruct((m, kw), out_dtype)], NT, None, epi,
                   ("arbitrary", "arbitrary"))[0]


def _mm_wgrad_cols(name, a, dy, tm):
    m, k = a.shape
    n = dy.shape[1] // N_DEV
    return _matmul(name, a, dy, (), (N_DEV, m // tm),
                   pl.BlockSpec((tm, k), lambda j, i: (i, 0)), pl.BlockSpec((tm, n), lambda j, i: (i, j)), (),
                   [pl.BlockSpec((None, k, n), lambda j, i: (j, 0, 0))], [jax.ShapeDtypeStruct((N_DEV, k, n), BF16)],
                   TN, 1, None, ("arbitrary", "arbitrary"))[0]


def _mm_wgrad_rows(name, a, dy, tm):
    m = a.shape[0]
    r = a.shape[1] // N_DEV
    n = dy.shape[1]
    return _matmul(name, a, dy, (), (N_DEV, m // tm),
                   pl.BlockSpec((tm, r), lambda j, i: (i, j)), pl.BlockSpec((tm, n), lambda j, i: (i, 0)), (),
                   [pl.BlockSpec((None, r, n), lambda j, i: (j, 0, 0))], [jax.ShapeDtypeStruct((N_DEV, r, n), BF16)],
                   TN, 1, None, ("arbitrary", "arbitrary"))[0]


class _Copies:
    def __init__(self, arrays, kinds):
        self.arrays, self.kinds, self.n = list(arrays), list(kinds), len(arrays)
        any_spec = pl.BlockSpec(memory_space=pl.ANY)
        self.in_specs = [any_spec] * self.n
        self.out_specs = [any_spec] * self.n
        self.out_shape = [jax.ShapeDtypeStruct(self._result(a, kind), a.dtype) for a, kind in zip(arrays, kinds)]
        self.scratch = [pltpu.SemaphoreType.DMA((self.n, N_DEV - 1)), pltpu.SemaphoreType.DMA((self.n, N_DEV - 1)),
                        pltpu.SemaphoreType.DMA((self.n,))] if self.n else []

    @staticmethod
    def _result(a, kind):
        if kind == "slot":
            return (N_DEV,) + a.shape
        if kind == "scatter":
            return a.shape
        return a.shape[:kind] + (N_DEV * a.shape[kind],) + a.shape[kind + 1:]

    def descriptors(self, ins, outs, sems):
        send_sems, recv_sems, local_sems = sems
        x, y, c = lax.axis_index("x"), lax.axis_index("y"), lax.axis_index("c")
        me = 4 * x + 2 * y + c

        def landing(i):
            kind = self.kinds[i]
            if kind in ("slot", "scatter"):
                return outs[i].at[me]
            size = ins[i].shape[kind]
            mine = pl.ds(pl.multiple_of(me * size, size), size)
            return outs[i].at[(slice(None),) * kind + (mine,)]

        copies = []
        for i in range(self.n):
            scatter = self.kinds[i] == "scatter"
            copies.append(pltpu.make_async_copy(ins[i].at[me] if scatter else ins[i], landing(i), local_sems.at[i]))
            for k in range(1, N_DEV):
                px = 1 - x if k & 4 else x
                py = 1 - y if k & 2 else y
                pc = 1 - c if k & 1 else c
                src = ins[i].at[4 * px + 2 * py + pc] if scatter else ins[i]
                copies.append(pltpu.make_async_remote_copy(src_ref=src, dst_ref=landing(i), send_sem=send_sems.at[i, k - 1],
                                                           recv_sem=recv_sems.at[i, k - 1], device_id=(px, py, pc),
                                                           device_id_type=MESH))
        return copies


def _exchange(name, arrays, kinds):
    cps = _Copies(arrays, kinds)
    n = cps.n

    def body(*refs):
        copies = cps.descriptors(refs[:n], refs[n:2 * n], refs[2 * n:])
        for cp in copies:
            cp.start()
        for cp in copies:
            cp.wait()

    return pl.pallas_call(body, name=name, in_specs=cps.in_specs, out_specs=cps.out_specs, out_shape=cps.out_shape,
                          scratch_shapes=cps.scratch)(*arrays)


def _carried(cps, n_in, n_out, n_scratch, refs):
    if cps is None:
        return [], refs
    n = cps.n
    ins = refs[n_in:n_in + n]
    outs = refs[n_in + n + n_out:n_in + n + n_out + n]
    sems = refs[n_in + n + n_out + n + n_scratch:]
    own = refs[:n_in] + refs[n_in + n:n_in + n + n_out] + refs[n_in + n + n_out + n:n_in + n + n_out + n + n_scratch]
    return cps.descriptors(ins, outs, sems), own


def _start_all(copies, when):
    @pl.when(when)
    def _():
        for cp in copies:
            cp.start()


def _wait_all(copies, when):
    @pl.when(when)
    def _():
        for cp in copies:
            cp.wait()


def _sum_lead(name, parts, tr):
    npart, r, c = parts.shape

    def body(p_ref, o_ref):
        acc = p_ref[0].astype(F32)
        for p in range(1, npart):
            acc = acc + p_ref[p].astype(F32)
        o_ref[...] = acc

    return pl.pallas_call(body, name=name, grid=(r // tr,), in_specs=[pl.BlockSpec((npart, tr, c), lambda i: (0, i, 0))],
                          out_specs=pl.BlockSpec((tr, c), lambda i: (i, 0)), out_shape=jax.ShapeDtypeStruct((r, c), F32),
                          compiler_params=_params(("arbitrary",)))(parts)


def _adamw_math(g, w, m, v):
    m2 = ADAM_B1 * m + (1.0 - ADAM_B1) * g
    v2 = ADAM_B2 * v + (1.0 - ADAM_B2) * jnp.square(g)
    m_hat = m2 / (1.0 - ADAM_B1 ** ADAM_STEP)
    v_hat = v2 / (1.0 - ADAM_B2 ** ADAM_STEP)
    return -ADAM_LR * (m_hat / (jnp.sqrt(v_hat) + ADAM_EPS) + ADAM_WD * w), m2, v2


def _adamw_body(npart):
    def body(p_ref, w_ref, m_ref, v_ref, *rest):
        g_out, d_out, m_out, v_out = rest[-4:]
        g = p_ref[0].astype(F32)
        for p in range(1, npart):
            g = g + p_ref[p].astype(F32)
        g_out[...] = g
        d_out[...], m_out[...], v_out[...] = _adamw_math(g, w_ref[...], m_ref[...], v_ref[...])
    return body


def _adamw_native(name, g, w, m, v):
    rest = w.shape[2:]
    spec = pl.BlockSpec((None, None) + rest, lambda i, j: (i, j) + (0,) * len(rest))

    def body(g_ref, w_ref, m_ref, v_ref, d_out, m_out, v_out):
        d_out[...], m_out[...], v_out[...] = _adamw_math(g_ref[...], w_ref[...], m_ref[...], v_ref[...])

    return pl.pallas_call(body, name=name, grid=w.shape[:2], in_specs=[spec] * 4, out_specs=[spec] * 3,
                          out_shape=[jax.ShapeDtypeStruct(w.shape, F32)] * 3,
                          compiler_params=_params(("arbitrary", "arbitrary")))(g, w, m, v)


def _adamw(name, parts, w, m, v, tr):
    npart, r, c = parts.shape
    row = pl.BlockSpec((tr, c), lambda i: (i, 0))
    return pl.pallas_call(_adamw_body(npart), name=name, grid=(r // tr,),
                          in_specs=[pl.BlockSpec((npart, tr, c), lambda i: (0, i, 0)), row, row, row],
                          out_specs=[row] * 4, out_shape=[jax.ShapeDtypeStruct((r, c), F32)] * 4,
                          compiler_params=_params(("arbitrary",)))(parts, w, m, v)


def _adamw_layer(name, parts, w3, m3, v3, bufs, layer, tr):
    npart, r, c = parts.shape
    lay = pl.BlockSpec((None, tr, c), lambda i: (layer, i, 0))
    hbm = pl.BlockSpec(memory_space=pl.ANY)
    return pl.pallas_call(_adamw_body(npart), name=name, grid=(r // tr,),
                          in_specs=[pl.BlockSpec((npart, tr, c), lambda i: (0, i, 0)), lay, lay, lay] + [hbm] * 4,
                          out_specs=[lay] * 4, out_shape=[jax.ShapeDtypeStruct(w3.shape, F32)] * 4,
                          input_output_aliases={4: 0, 5: 1, 6: 2, 7: 3},
                          compiler_params=_params(("arbitrary",)))(parts, w3, m3, v3, *bufs)


def _row_tile(r, cap):
    if r <= cap:
        return r
    t = cap - cap % SUBLANES
    while r % t:
        t -= SUBLANES
    return t


def _ada_fwd(cc, w_ada, b_loc):
    nl, d, n = w_ada.shape
    rows = cc.shape[0]

    def body(c_ref, w_ref, b_ref, o_ref):
        cond = jax.nn.silu(c_ref[...]).astype(BF16)
        o_ref[...] = jnp.dot(cond, w_ref[...].astype(BF16), preferred_element_type=F32) + b_ref[...]

    return pl.pallas_call(body, name="ada_fwd", grid=(nl,),
                          in_specs=[pl.BlockSpec((rows, d), lambda i: (0, 0)), pl.BlockSpec((None, d, n), lambda i: (i, 0, 0)),
                                    pl.BlockSpec((None, 1, n), lambda i: (i, 0, 0))],
                          out_specs=pl.BlockSpec((None, rows, n), lambda i: (i, 0, 0)),
                          out_shape=jax.ShapeDtypeStruct((nl, rows, n), F32), compiler_params=_params(("arbitrary",)))(cc, w_ada, b_loc)


def _ada_bwd(cc, w_ada, dmod_rows, dmod_ctx):
    nl, d, n = w_ada.shape
    rows = cc.shape[0]
    ctx_row = rows - SUBLANES

    def body(c_ref, w_ref, dr_ref, dc_ref, gw_ref, tot_ref, dcond_ref):
        i = pl.program_id(0)
        total = dc_ref[0]
        for p in range(1, N_DEV):
            total = total + dc_ref[p]
        tot_ref[...] = total
        row_id = lax.broadcasted_iota(jnp.int32, (rows, n), 0)
        dm = jnp.where(row_id == ctx_row, jnp.broadcast_to(total, (rows, n)), dr_ref[...]).astype(BF16)
        cond = jax.nn.silu(c_ref[...]).astype(BF16)
        gw_ref[...] = lax.dot_general(cond, dm, TN, preferred_element_type=F32)
        part = lax.dot_general(dm, w_ref[...].astype(BF16), NT, preferred_element_type=F32)
        _accumulate(dcond_ref, part, i == 0)

    return pl.pallas_call(body, name="ada_bwd", grid=(nl,),
                          in_specs=[pl.BlockSpec((rows, d), lambda i: (0, 0)), pl.BlockSpec((None, d, n), lambda i: (i, 0, 0)),
                                    pl.BlockSpec((None, rows, n), lambda i: (i, 0, 0)),
                                    pl.BlockSpec((N_DEV, None, 1, n), lambda i: (0, i, 0, 0))],
                          out_specs=[pl.BlockSpec((None, d, n), lambda i: (i, 0, 0)), pl.BlockSpec((None, 1, n), lambda i: (i, 0, 0)),
                                     pl.BlockSpec((rows, d), lambda i: (0, 0))],
                          out_shape=[jax.ShapeDtypeStruct((nl, d, n), F32), jax.ShapeDtypeStruct((nl, 1, n), F32),
                                     jax.ShapeDtypeStruct((rows, d), F32)],
                          compiler_params=_params(("arbitrary",)))(cc, w_ada, dmod_rows, dmod_ctx)


def _cctx_grad(parts, c_ctx):
    def body(p_ref, c_ref, o_ref):
        tot = p_ref[0]
        for p in range(1, N_DEV):
            tot = tot + p_ref[p]
        _, vjp = jax.vjp(jax.nn.silu, c_ref[...])
        o_ref[...] = vjp(tot)[0]

    return pl.pallas_call(body, name="cctx_grad", out_shape=jax.ShapeDtypeStruct(c_ctx.shape, F32))(parts, c_ctx)


def _discretise(lam_re, lam_im, log_dt, b_re, b_im):
    lr = jnp.minimum(lam_re, LAMBDA_RE_MAX)
    li = lam_im
    dt = jnp.exp(log_dt)
    mag = jnp.exp(lr * dt)
    ab_re = mag * jnp.cos(li * dt)
    ab_im = mag * jnp.sin(li * dt)
    den = lr * lr + li * li
    nr = ab_re - 1.0
    ni = ab_im
    coef_re = ((nr * lr + ni * li) / den)[:, None]
    coef_im = ((ni * lr - nr * li) / den)[:, None]
    bb_re = coef_re * b_re - coef_im * b_im
    bb_im = coef_re * b_im + coef_im * b_re
    return ab_re, ab_im, bb_re, bb_im


def _s5_prep(name, lam_re, lam_im, log_dt, b_re, b_im):
    def body(a, b, c, d, e, o1, o2, o3, o4):
        res = _discretise(a[...], b[...], c[...], d[...], e[...])
        for o, v in zip((o1, o2, o3, o4), res):
            o[...] = v

    shp = [jax.ShapeDtypeStruct(lam_re.shape, F32)] * 2 + [jax.ShapeDtypeStruct(b_re.shape, F32)] * 2
    return pl.pallas_call(body, name=name, out_shape=shp)(lam_re, lam_im, log_dt, b_re, b_im)


def _s5_prep_bwd(name, lam_re, lam_im, log_dt, b_re, b_im, cots):
    def body(a, b, c, d, e, c1, c2, c3, c4, o1, o2, o3, o4, o5):
        _, vjp = jax.vjp(_discretise, a[...], b[...], c[...], d[...], e[...])
        grads = vjp((c1[...], c2[...], c3[...], c4[...]))
        for o, v in zip((o1, o2, o3, o4, o5), grads):
            o[...] = v

    shp = [jax.ShapeDtypeStruct(a.shape, F32) for a in (lam_re, lam_im, log_dt, b_re, b_im)]
    return pl.pallas_call(body, name=name, out_shape=shp)(lam_re, lam_im, log_dt, b_re, b_im, *cots)


def _interleave_rows(ref, dtype):
    n_j = ref.shape[0] // SUBLANES
    return jnp.concatenate([ref[pl.ds(j, SUBLANES, stride=n_j), :] for j in range(n_j)], axis=0).astype(dtype)


def _store_tokens(out_ref, ref):
    n_j = ref.shape[0] // SUBLANES
    for s in range(SUBLANES):
        out_ref[s * n_j:(s + 1) * n_j, :] = ref[pl.ds(s, n_j, stride=SUBLANES), :].astype(out_ref.dtype)


def _scan_tile(h_ref, t_ref, carry_ref, up, states_ref=None):
    sw = h_ref.shape[1] // 2
    n_j = h_ref.shape[0] // SUBLANES

    def rows(g):
        if isinstance(g, int):
            return pl.ds(g * SUBLANES, SUBLANES)
        return pl.ds(pl.multiple_of(g * SUBLANES, SUBLANES), SUBLANES)

    def tab(g):
        return t_ref[rows(g), :sw], t_ref[rows(g), sw:]

    def order(i):
        return n_j - 1 - i if up else i

    def cmul_add(xr, xi, ar, ai, yr, yi):
        return xr + ar * yr - ai * yi, xi + ar * yi + ai * yr

    a_re, a_im = tab(0)

    def local_step(i, x):
        j = order(i)
        xr, xi = cmul_add(h_ref[rows(j), :sw], h_ref[rows(j), sw:], a_re, a_im, x[0], x[1])
        h_ref[rows(j), :sw] = xr
        h_ref[rows(j), sw:] = xi
        return xr, xi

    zero = jnp.zeros((SUBLANES, sw), F32)
    dr, di = lax.fori_loop(0, n_j, local_step, (zero, zero))
    for level, sh in enumerate((1, 2, 4)):
        amount = SUBLANES - sh if up else sh
        dr, di = cmul_add(dr, di, *tab(1 + level), pltpu.roll(dr, amount, 0), pltpu.roll(di, amount, 0))
    cr, ci = carry_ref[:, :sw], carry_ref[:, sw:]
    dr, di = cmul_add(dr, di, *tab(4), cr, ci)
    out_row = 0 if up else SUBLANES - 1
    carry_ref[:, :sw] = jnp.broadcast_to(dr[out_row:out_row + 1], dr.shape)
    carry_ref[:, sw:] = jnp.broadcast_to(di[out_row:out_row + 1], di.shape)
    in_row = SUBLANES - 1 if up else 0
    one = SUBLANES - 1 if up else 1
    is_in = lax.broadcasted_iota(jnp.int32, (SUBLANES, sw), 0) == in_row
    enter_r = jnp.where(is_in, cr, pltpu.roll(dr, one, 0))
    enter_i = jnp.where(is_in, ci, pltpu.roll(di, one, 0))

    def fix_step(i, state):
        j = order(i)
        xr, xi = cmul_add(h_ref[rows(j), :sw], h_ref[rows(j), sw:], *tab(5 + j), enter_r, enter_i)
        h_ref[rows(j), :sw] = xr
        h_ref[rows(j), sw:] = xi
        if states_ref is None:
            return state
        (lr, li), (acc_r, acc_i) = state
        hr, hi = states_ref[rows(j), :sw], states_ref[rows(j), sw:]
        return (xr, xi), (acc_r + lr * hr + li * hi, acc_i + li * hr - lr * hi)

    if states_ref is None:
        lax.fori_loop(0, n_j, fix_step, 0)
        return None
    (lr, li), (acc_r, acc_i) = lax.fori_loop(0, n_j, fix_step, ((zero, zero), (zero, zero)))
    lr = jnp.where(is_in, cr, pltpu.roll(lr, one, 0))
    li = jnp.where(is_in, ci, pltpu.roll(li, one, 0))
    j0 = order(0)
    hr, hi = states_ref[rows(j0), :sw], states_ref[rows(j0), sw:]
    return acc_r + lr * hr + li * hi, acc_i + li * hr - lr * hi


def _s5_tile_index(cfg, dirn, adjoint):
    tpl, nb = cfg["tpl"], cfg["nb"]

    def idx(b, k):
        if not adjoint:
            lat = b * tpl + (k - 1 if dirn == 0 else tpl - k)
            return jnp.where(k == 0, nb * tpl + b, lat)
        lat = b * tpl + (tpl - 1 - k if dirn == 0 else k)
        return jnp.where(k == tpl, nb * tpl + b, lat)
    return idx


def _grid_ends(grid):
    ids = [pl.program_id(i) for i in range(len(grid))]
    first = functools.reduce(jnp.logical_and, [i == 0 for i in ids])
    last = functools.reduce(jnp.logical_and, [i == n - 1 for i, n in zip(ids, grid)])
    return first, last


def _s5_fwd(name, u, bmat, cmat, tab, dirn, cfg, cps=None):
    tr, tpl, nb = cfg["tr"], cfg["tpl"], cfg["nb"]
    n_rows, d = u.shape
    ns, _, sw2 = bmat.shape
    tile = _s5_tile_index(cfg, dirn, False)
    up = dirn == 1
    grid = (ns, nb, tpl + 1)

    def body(*refs):
        copies, (u_ref, b_ref, c_ref, t_ref, h_ref, y_ref, carry_ref, mix_ref) = _carried(cps, 4, 2, 2, refs)
        first, last = _grid_ends(grid)
        _start_all(copies, first)

        @pl.when(pl.program_id(2) == 0)
        def _():
            carry_ref[...] = jnp.zeros_like(carry_ref)

        mix_ref[...] = u_ref[...].astype(F32)
        h_ref[...] = jnp.dot(_interleave_rows(mix_ref, BF16), b_ref[...], preferred_element_type=F32)
        _scan_tile(h_ref, t_ref, carry_ref, up)
        mix_ref[...] = jnp.dot(h_ref[...].astype(BF16), c_ref[...], preferred_element_type=F32)
        _store_tokens(y_ref, mix_ref)
        _wait_all(copies, last)

    extra = cps if cps is not None else _Copies([], [])
    return pl.pallas_call(
        body, name=name, grid=grid,
        in_specs=[pl.BlockSpec((tr, LANES), lambda s, b, k: (tile(b, k), s)),
                  pl.BlockSpec((None, LANES, sw2), lambda s, b, k: (s, 0, 0)),
                  pl.BlockSpec((None, sw2, LANES), lambda s, b, k: (s, 0, 0)),
                  pl.BlockSpec((None, tab.shape[1], sw2), lambda s, b, k: (s, 0, 0))] + extra.in_specs,
        out_specs=[pl.BlockSpec((tr, sw2), lambda s, b, k: (tile(b, k), s)),
                   pl.BlockSpec((tr, LANES), lambda s, b, k: (tile(b, k), s))] + extra.out_specs,
        out_shape=[jax.ShapeDtypeStruct((n_rows, ns * sw2), F32), jax.ShapeDtypeStruct((n_rows, d), F32)] + extra.out_shape,
        scratch_shapes=[pltpu.VMEM((SUBLANES, sw2), F32), pltpu.VMEM((tr, LANES), F32)] + extra.scratch,
        compiler_params=_params(("arbitrary", "arbitrary", "arbitrary")))(u, bmat, cmat, tab, *extra.arrays)


def _s5_bwd(name, dy, h, u, cmat_t, bmat_t, tab, dirn, cfg, cps=None):
    tr, tpl, nb = cfg["tr"], cfg["tpl"], cfg["nb"]
    n_rows, d = u.shape
    ns, _, sw2 = cmat_t.shape
    sw = sw2 // 2
    tile = _s5_tile_index(cfg, dirn, True)
    up = dirn == 0
    grid = (ns, nb, tpl + 1)

    def body(*refs):
        copies, own = _carried(cps, 6, 4, 3, refs)
        dy_ref, h_ref, u_ref, ct_ref, bt_ref, t_ref, du_ref, db_ref, dc_ref, da_ref, lam_ref, carry_ref, mix_ref = own
        grid_first, grid_last = _grid_ends(grid)
        _start_all(copies, grid_first)
        first = jnp.logical_and(pl.program_id(1) == 0, pl.program_id(2) == 0)

        @pl.when(pl.program_id(2) == 0)
        def _():
            carry_ref[...] = jnp.zeros_like(carry_ref)

        mix_ref[...] = dy_ref[...].astype(F32)
        dy = _interleave_rows(mix_ref, BF16)
        mix_ref[...] = u_ref[...].astype(F32)
        u_mixed = _interleave_rows(mix_ref, BF16)
        lam_ref[...] = jnp.dot(dy, ct_ref[...], preferred_element_type=F32)
        acc = _scan_tile(lam_ref, t_ref, carry_ref, up, h_ref)
        lam = lam_ref[...].astype(BF16)
        d_b = lax.dot_general(u_mixed, lam, TN, preferred_element_type=F32)
        d_c = lax.dot_general(h_ref[...].astype(BF16), dy, TN, preferred_element_type=F32)
        mix_ref[...] = jnp.dot(lam, bt_ref[...], preferred_element_type=F32)
        _store_tokens(du_ref, mix_ref)

        @pl.when(first)
        def _():
            db_ref[...] = d_b
            dc_ref[...] = d_c
            da_ref[:, :sw] = acc[0]
            da_ref[:, sw:] = acc[1]

        @pl.when(jnp.logical_not(first))
        def _():
            db_ref[...] += d_b
            dc_ref[...] += d_c
            da_ref[:, :sw] += acc[0]
            da_ref[:, sw:] += acc[1]

        _wait_all(copies, grid_last)

    extra = cps if cps is not None else _Copies([], [])
    return pl.pallas_call(
        body, name=name, grid=grid,
        in_specs=[pl.BlockSpec((tr, LANES), lambda s, b, k: (tile(b, k), s)),
                  pl.BlockSpec((tr, sw2), lambda s, b, k: (tile(b, k), s)),
                  pl.BlockSpec((tr, LANES), lambda s, b, k: (tile(b, k), s)),
                  pl.BlockSpec((None, LANES, sw2), lambda s, b, k: (s, 0, 0)),
                  pl.BlockSpec((None, sw2, LANES), lambda s, b, k: (s, 0, 0)),
                  pl.BlockSpec((None, tab.shape[1], sw2), lambda s, b, k: (s, 0, 0))] + extra.in_specs,
        out_specs=[pl.BlockSpec((tr, LANES), lambda s, b, k: (tile(b, k), s)),
                   pl.BlockSpec((None, LANES, sw2), lambda s, b, k: (s, 0, 0)),
                   pl.BlockSpec((None, sw2, LANES), lambda s, b, k: (s, 0, 0)),
                   pl.BlockSpec((None, SUBLANES, sw2), lambda s, b, k: (s, 0, 0))] + extra.out_specs,
        out_shape=[jax.ShapeDtypeStruct((n_rows, d), F32), jax.ShapeDtypeStruct((ns, LANES, sw2), F32),
                   jax.ShapeDtypeStruct((ns, sw2, LANES), F32), jax.ShapeDtypeStruct((ns, SUBLANES, sw2), F32)] + extra.out_shape,
        scratch_shapes=[pltpu.VMEM((tr, sw2), F32), pltpu.VMEM((SUBLANES, sw2), F32), pltpu.VMEM((tr, LANES), F32)] + extra.scratch,
        compiler_params=_params(("arbitrary", "arbitrary", "arbitrary")))(dy, h, u, cmat_t, bmat_t, tab, *extra.arrays)


def _s5_tables(ab_re, ab_im, up, conj, ns, n_j):
    def powers_of(base, count):
        out = [base]
        for _ in range(count - 1):
            q_re, q_im = out[-1]
            out.append((q_re * base[0] - q_im * base[1], q_re * base[1] + q_im * base[0]))
        return out

    def spread(q):
        return jnp.broadcast_to(q[:, None, :], (q.shape[0], SUBLANES, q.shape[1]))

    steps = powers_of((ab_re.reshape(ns, -1), (-ab_im if conj else ab_im).reshape(ns, -1)), n_j)
    jumps = powers_of(steps[-1], SUBLANES)
    rows = jnp.arange(SUBLANES)
    blocks = [tuple(spread(q) for q in steps[0])]
    for sh in (1, 2, 4):
        keep = ((rows <= SUBLANES - 1 - sh) if up else (rows >= sh))[None, :, None]
        blocks.append(tuple(jnp.where(keep, q[:, None, :], 0.0) for q in jumps[sh - 1]))
    dist = range(SUBLANES, 0, -1) if up else range(1, SUBLANES + 1)
    blocks.append(tuple(jnp.stack([jumps[dd - 1][part] for dd in dist], axis=1) for part in (0, 1)))
    for j in range(n_j):
        blocks.append(tuple(spread(q) for q in steps[n_j - j - 1 if up else j]))
    return jnp.concatenate([jnp.concatenate([b[0] for b in blocks], axis=1), jnp.concatenate([b[1] for b in blocks], axis=1)], axis=2)


def _block_diag(blocks):
    ns, gs, a, b = blocks.shape
    eye = jnp.eye(gs, dtype=blocks.dtype)
    return (blocks[:, :, :, None, :] * eye[None, :, None, :, None]).reshape(ns, gs * a, gs * b)


def _diag_blocks(mat, gs):
    ns, ra, rb = mat.shape
    a, b = ra // gs, rb // gs
    m5 = mat.reshape(ns, gs, a, gs, b)
    eye = jnp.eye(gs, dtype=mat.dtype)
    return jnp.sum(m5 * eye[None, :, None, :, None], axis=3)


def _conv_flags(t, cfg):
    tpl, nb = cfg["tpl"], cfg["nb"]
    latent = t < nb * tpl
    first = jnp.logical_or(jnp.logical_not(latent), t % tpl == 0)
    last = jnp.logical_or(jnp.logical_not(latent), t % tpl == tpl - 1)
    return first, last


def _fill_ext(ext_ref, prev_ref, cur_ref, next_ref, t, cfg, halo):
    first, last = _conv_flags(t, cfg)
    tr = cur_ref.shape[0]
    for p in range(ext_ref.shape[0]):
        lanes = slice(p * LANES, (p + 1) * LANES)
        ext_ref[p, 0:halo, :] = jnp.where(first, 0.0, prev_ref[:, lanes])
        ext_ref[p, halo:halo + tr, :] = cur_ref[:, lanes]
        ext_ref[p, halo + tr:, :] = jnp.where(last, 0.0, next_ref[:, lanes])


CONV_LANES = 4 * LANES


def _conv_specs(tr, n_rows, halo, cw):
    per = tr // halo
    n_halo = n_rows // halo
    return [pl.BlockSpec((halo, cw), lambda c, t: (jnp.maximum(t * per - 1, 0), c)),
            pl.BlockSpec((tr, cw), lambda c, t: (t, c)),
            pl.BlockSpec((halo, cw), lambda c, t: (jnp.minimum((t + 1) * per, n_halo - 1), c))]


def _dwconv(name, a, w, cfg):
    tr = cfg["tr"]
    n_rows, d = a.shape
    kw = w.shape[0]
    half = kw // 2
    halo = 2 * SUBLANES
    cw = min(d, CONV_LANES)

    def body(prev_ref, cur_ref, next_ref, w_ref, o_ref, ext_ref):
        _fill_ext(ext_ref, prev_ref, cur_ref, next_ref, pl.program_id(1), cfg, halo)
        for p in range(cw // LANES):
            lanes = slice(p * LANES, (p + 1) * LANES)
            acc = jnp.zeros((tr, LANES), F32)
            for k in range(kw):
                acc = acc + ext_ref[p, pl.ds(halo - half + k, tr), :] * w_ref[k:k + 1, lanes]
            o_ref[:, lanes] = acc

    return pl.pallas_call(body, name=name, grid=(d // cw, n_rows // tr),
                          in_specs=_conv_specs(tr, n_rows, halo, cw) + [pl.BlockSpec((kw, cw), lambda c, t: (0, c))],
                          out_specs=pl.BlockSpec((tr, cw), lambda c, t: (t, c)),
                          out_shape=jax.ShapeDtypeStruct((n_rows, d), F32),
                          scratch_shapes=[pltpu.VMEM((cw // LANES, tr + 2 * halo, LANES), F32)],
                          compiler_params=_params(("arbitrary", "arbitrary")))(a, a, a, w)


def _dwconv_wgrad(name, a, dout, kw, cfg):
    tr = cfg["tr"]
    n_rows, d = a.shape
    half = kw // 2
    halo = 2 * SUBLANES
    cw = min(d, CONV_LANES)

    def body(prev_ref, cur_ref, next_ref, do_ref, o_ref, ext_ref):
        t = pl.program_id(1)
        _fill_ext(ext_ref, prev_ref, cur_ref, next_ref, t, cfg, halo)
        for p in range(cw // LANES):
            lanes = slice(p * LANES, (p + 1) * LANES)
            dout_t = do_ref[:, lanes]
            rows = [jnp.sum(ext_ref[p, pl.ds(halo - half + k, tr), :] * dout_t, axis=0, keepdims=True) for k in range(kw)]
            _accumulate(o_ref.at[:, lanes], jnp.concatenate(rows, axis=0), t == 0)

    return pl.pallas_call(body, name=name, grid=(d // cw, n_rows // tr),
                          in_specs=_conv_specs(tr, n_rows, halo, cw) + [pl.BlockSpec((tr, cw), lambda c, t: (t, c))],
                          out_specs=pl.BlockSpec((kw, cw), lambda c, t: (0, c)),
                          out_shape=jax.ShapeDtypeStruct((kw, d), F32),
                          scratch_shapes=[pltpu.VMEM((cw // LANES, tr + 2 * halo, LANES), F32)],
                          compiler_params=_params(("arbitrary", "arbitrary")))(a, a, a, dout)


def _sincos_1d(pos, dim):
    quarter = dim // 2
    omega = POS_TEMP ** (-jnp.arange(quarter, dtype=F32) / quarter)
    ang = pos[:, None] * omega[None, :]
    return jnp.concatenate([jnp.sin(ang), jnp.cos(ang)], axis=-1)


def _grid_pos_embed(rows, dim):
    row_idx = jnp.repeat(jnp.arange(rows), GRID_W).astype(F32)
    col_idx = jnp.tile(jnp.arange(GRID_W), rows).astype(F32)
    return jnp.concatenate([_sincos_1d(row_idx, dim // 2), _sincos_1d(col_idx, dim // 2)], axis=-1)


def _pack(arrs, row_multiple=SUBLANES):
    flat = jnp.concatenate([a.reshape(-1).astype(F32) for a in arrs])
    pad = (-flat.shape[0]) % (row_multiple * LANES)
    return jnp.pad(flat, (0, pad)).reshape(-1, LANES)


def _unpack(buf, shapes):
    flat = buf.reshape(-1)
    out, pos = [], 0
    for shp in shapes:
        n = math.prod(shp)
        out.append(flat[pos:pos + n].reshape(shp))
        pos += n
    return out


def _unpack_gathered(buf, shapes):
    flat = buf.reshape(N_DEV, -1)
    out, pos = [], 0
    for shp in shapes:
        n = math.prod(shp)
        part = flat[:, pos:pos + n].reshape((N_DEV,) + tuple(shp))
        out.append(jnp.moveaxis(part, 0, -2).reshape(tuple(shp[:-1]) + (N_DEV * shp[-1],)))
        pos += n
    return out


WEIGHTS = ("c_ctx", "w_ada", "b_ada", "ln_gain", "ln_bias", "s5_lam_re", "s5_lam_im", "s5_log_dt", "s5_b_re", "s5_b_im",
           "s5_c_re", "s5_c_im", "s5_d", "s5_w_glu", "s5_b_glu", "cv_w_pw1", "cv_b_pw1", "cv_w_dw", "cv_b_dw", "cv_ln_g",
           "cv_ln_b", "cv_w_pw2", "cv_b_pw2", "mlp_w1", "mlp_w2")
SHARDED_SMALL = ("ln_gain", "ln_bias", "cv_b_pw1", "cv_w_dw", "cv_b_dw", "cv_ln_g", "cv_ln_b", "cv_b_pw2")
REPLICATED_SMALL = ("s5_lam_re", "s5_lam_im", "s5_log_dt", "s5_b_re", "s5_b_im", "s5_c_re", "s5_c_im", "s5_d", "s5_b_glu")
NATIVE_SMALL = ("s5_lam_re", "s5_lam_im", "s5_b_re", "s5_b_im", "s5_c_re", "s5_c_im")
BIG = ("mlp_w1", "mlp_w2", "s5_w_glu", "cv_w_pw1", "cv_w_pw2")


def _step(a):
    x, c, ctx = a["x"], a["c"], a["ctx"]
    nb, seq, d = x.shape
    lc = ctx.shape[1]
    nl = a["w_ada"].shape[0]
    tr = lc
    tpl = seq // tr
    cfg = {"tr": tr, "tpl": tpl, "nb": nb}
    n_rows = nb * (seq + lc)
    alpha = (2.0 * nl) ** 0.25
    me = 4 * lax.axis_index("x") + 2 * lax.axis_index("y") + lax.axis_index("c")
    n_grp, n_state = a["s5_lam_re"].shape[2:]
    ch = a["s5_b_re"].shape[-1]
    gs = LANES // ch
    ns = d // LANES
    tm = 2 * tr if n_rows % (2 * tr) == 0 else tr
    tm_big = n_rows // 3 if n_rows % (3 * 2 * SUBLANES) == 0 else tm
    f_sub1_s5, f_sub1_cv, f_sub2 = _make_sub1_s5(alpha), _make_sub1_cv(alpha), _make_sub2(alpha)

    def layer_weights(i):
        mixer = [("s5_w_glu", i // 2, 1)] if i % 2 == 0 else [("cv_w_pw1", i // 2, 1), ("cv_w_pw2", i // 2, 0)]
        return mixer + [("mlp_w1", i, 1), ("mlp_w2", i, 0)]

    weights, wgrads, received = {}, {}, {}
    small_all = _exchange("gather_small", [_pack([a[n] for n in SHARDED_SMALL])], ["slot"])[0]
    full = dict(zip(SHARDED_SMALL, _unpack_gathered(small_all, [a[n].shape for n in SHARDED_SMALL])))
    c_all = _exchange("gather_c", [c], ["slot"])[0].reshape(N_DEV * nb, d)
    cond_rows = N_DEV * nb + SUBLANES
    cc = jnp.concatenate([c_all, a["c_ctx"][None], jnp.zeros((SUBLANES - 1, d), F32)], axis=0)

    n_ada = a["w_ada"].shape[2]
    b_loc = lax.dynamic_slice(a["b_ada"], (0, me * n_ada), (nl, n_ada))[:, None, :]
    mod_cols = _ada_fwd(cc, a["w_ada"], b_loc)
    mod_all = _exchange("gather_mod", [mod_cols.reshape(nl * cond_rows, n_ada)], ["slot"])[0].reshape(N_DEV, nl, cond_rows, n_ada)
    mod_mine = jnp.concatenate([lax.dynamic_slice(mod_all, (0, 0, nb * me, 0), (N_DEV, nl, nb, n_ada)),
                                mod_all[:, :, N_DEV * nb:N_DEV * nb + 1]], axis=2)
    mod = jnp.transpose(mod_mine, (1, 2, 0, 3)).reshape(nl, nb + 1, 6, 1, d)

    def seg(i, q):
        return mod[i, :, q]

    zero_seg = jnp.zeros((nb + 1, 1, d), F32)

    def vec(v):
        return v.reshape(1, -1)

    pos = _grid_pos_embed(seq // GRID_W, d)
    xc = jnp.concatenate([x.reshape(nb * seq, d), ctx.reshape(nb * lc, d)], axis=0)
    pos_rows = jnp.concatenate([jnp.tile(pos, (nb, 1)), jnp.zeros((nb * lc, d), F32)], axis=0)
    x_cur, h_cur = _rowwise("entry", _f_entry, [xc, pos_rows], [seg(0, 0), seg(0, 1)], [], [(d, F32), (d, BF16)], [], [], cfg)
    saved = []
    for i in range(nl):
        j = i // 2
        sv = {"x": x_cur, "h": h_cur}
        sh1, sc1, g1, sh2, sc2, g2 = (seg(i, q) for q in range(6))
        gain0, bias0, gain1, bias1 = (vec(full["ln_gain"][i, 0]), vec(full["ln_bias"][i, 0]),
                                      vec(full["ln_gain"][i, 1]), vec(full["ln_bias"][i, 1]))
        if i % 2 == 0:
            lam_re, lam_im = a["s5_lam_re"][j], a["s5_lam_im"][j]
            log_dt = a["s5_log_dt"][j][:, :, None]
            b_re_t = jnp.transpose(a["s5_b_re"][j], (0, 3, 1, 2))
            b_im_t = jnp.transpose(a["s5_b_im"][j], (0, 3, 1, 2))
            sv["prep_in"] = (lam_re, lam_im, log_dt, b_re_t, b_im_t)
            ab_re, ab_im, bb_re, bb_im = _s5_prep(f"s5_prep{i}", *sv["prep_in"])
            sv["ab"] = (ab_re, ab_im)
            ys = []
            for dirn in range(2):
                def blocks(t):
                    return jnp.transpose(t, (1, 0, 2)).reshape(ns, gs, ch, n_state)
                bmat = jnp.concatenate([_block_diag(blocks(bb_re[dirn])), _block_diag(blocks(bb_im[dirn]))], axis=2).astype(BF16)
                c_re_t = jnp.transpose(a["s5_c_re"][j, dirn], (0, 2, 1)).reshape(ns, gs, n_state, ch)
                c_im_t = jnp.transpose(a["s5_c_im"][j, dirn], (0, 2, 1)).reshape(ns, gs, n_state, ch)
                cmat = jnp.concatenate([_block_diag(c_re_t), -_block_diag(c_im_t)], axis=1).astype(BF16)
                tab = _s5_tables(ab_re[dirn], ab_im[dirn], dirn == 1, False, ns, tr // SUBLANES)
                group = layer_weights(i + dirn)
                cps = _Copies([a[n][idx].astype(BF16) for n, idx, _ in group], [axis for _, _, axis in group])
                h_states, y_dir, *gathered = _s5_fwd(f"s5_fwd{i}_{dirn}", h_cur, bmat, cmat, tab, dirn, cfg, cps)
                weights.update({(n, idx): w[None] for (n, idx, _), w in zip(group, gathered)})
                sv[f"mats{dirn}"] = (jnp.transpose(bmat, (0, 2, 1)), jnp.transpose(cmat, (0, 2, 1)))
                sv[f"states{dirn}"] = h_states
                ys.append(y_dir)
            sv["y"] = ys
            dsk = vec(a["s5_d"][j])
            z = _rowwise(f"gelu{i}", _f_gelu, [x_cur, ys[0], ys[1]], [sh1, sc1], [dsk], [(d, BF16)], [], [], cfg)[0]
            zz = _mm_nn(f"glu{i}", z, weights["s5_w_glu", j], 0, tm_big, min(2 * d, 512))[0]
            bglu = vec(a["s5_b_glu"][j])
            x1, h2 = _rowwise(f"sub1_{i}", f_sub1_s5, [x_cur, zz], [g1, sh2, sc2], [bglu, gain0, bias0],
                              [(d, F32), (d, BF16)], [], [], cfg)
            sv.update(z=z, zz=zz)
        else:
            zz = _mm_nn(f"pw1_{i}", h_cur, weights["cv_w_pw1", j], 0, tm_big, min(2 * d, 512))[0]
            bpw1 = vec(full["cv_b_pw1"][j])
            act = _rowwise(f"cvglu{i}", _f_cvglu, [zz], [], [bpw1], [(d, F32)], [], [], cfg)[0]
            w_dw = full["cv_w_dw"][j]
            cv = _dwconv(f"dwconv{i}", act, w_dw, cfg)
            bdw, lng, lnb = vec(full["cv_b_dw"][j]), vec(full["cv_ln_g"][j]), vec(full["cv_ln_b"][j])
            s_act = _rowwise(f"cvln{i}", _f_cvln, [cv], [], [bdw, lng, lnb], [(d, BF16)], [], [], cfg)[0]
            mm = _mm_nn(f"pw2_{i}", s_act, weights["cv_w_pw2", j], 0, tm_big, d)[0]
            bpw2 = vec(full["cv_b_pw2"][j])
            x1, h2 = _rowwise(f"sub1_{i}", f_sub1_cv, [x_cur, mm], [g1, sh2, sc2], [bpw2, gain0, bias0],
                              [(d, F32), (d, BF16)], [], [], cfg)
            sv.update(zz=zz, act=act, cv=cv, s_act=s_act, mm=mm, w_dw=w_dw)
        dff = weights["mlp_w1", i].shape[2]
        p_act, r_act = _mm_nn(f"mlp1_{i}", h2, weights["mlp_w1", i], 0, tm_big, min(dff, 1024), (BF16, BF16),
                              lambda acc: (jnp.square(jnp.maximum(acc, 0.0)), jnp.maximum(acc, 0.0)))
        m_out = _mm_nn(f"mlp2_{i}", p_act, weights["mlp_w2", i], 0, tm, d)[0]
        shn, scn = (seg(i + 1, 0), seg(i + 1, 1)) if i + 1 < nl else (zero_seg, zero_seg)
        x2, hn = _rowwise(f"sub2_{i}", f_sub2, [x1, m_out], [g2, shn, scn], [gain1, bias1], [(d, F32), (d, BF16)], [], [], cfg)
        sv.update(x1=x1, h2=h2, p=p_act, r=r_act, m=m_out, shn=shn, scn=scn)
        saved.append(sv)
        x_cur, h_cur = x2, hn

    target = jnp.concatenate([a["loss_target"].reshape(nb * seq, d), jnp.zeros((nb * lc, d), F32)], axis=0)
    mask = jnp.concatenate([jnp.ones((nb, 1, d), F32), jnp.zeros((1, 1, d), F32)], axis=0)

    def f_loss(xf, tgt, msk):
        err = (xf - tgt) * msk
        part = 0.5 * jnp.sum(jnp.square(err), axis=(0, 1), keepdims=True) / d
        return err / d, jnp.broadcast_to(part, (1, LANES))

    dx_final, loss_part = _rowwise("loss", f_loss, [x_cur, target], [mask], [], [(d, F32)], [], [LANES], cfg)
    loss = lax.psum(loss_part[0, 0], ("x", "y", "c"))

    grads = {n: [None] * a[n].shape[0] for n in WEIGHTS if n not in ("c_ctx", "w_ada", "b_ada")}
    dmod = [[None] * 6 for _ in range(nl)]

    def add_mod(i, q, val):
        dmod[i][q] = val if dmod[i][q] is None else dmod[i][q] + val

    dx_parts, dh_parts = [dx_final], []
    for i in reversed(range(nl)):
        j = i // 2
        sv = saved[i]
        sh1, sc1, g1, sh2, sc2, g2 = (seg(i, q) for q in range(6))
        gain0, bias0, gain1, bias1 = (vec(full["ln_gain"][i, 0]), vec(full["ln_bias"][i, 0]),
                                      vec(full["ln_gain"][i, 1]), vec(full["ln_bias"][i, 1]))
        bwd = _vjp_fn(f_sub2, 2, (len(dx_parts), len(dh_parts)), (0, 1, 2, 3, 4, 5, 6))
        dx1, dm, dg2, dshn, dscn, dgain1, dbias1 = _rowwise(
            f"sub2_bwd{i}", bwd, [sv["x1"], sv["m"]] + dx_parts + dh_parts, [g2, sv["shn"], sv["scn"]], [gain1, bias1],
            [(d, F32), (d, BF16)], [d, d, d], [d, d], cfg)
        add_mod(i, 5, dg2)
        if i + 1 < nl:
            add_mod(i + 1, 0, dshn)
            add_mod(i + 1, 1, dscn)
        da = _mm_nt(f"mlp2_dgrad{i}", dm, weights["mlp_w2", i], 0, tm_big, min(dff, 1024), [sv["r"]], BF16,
                    lambda acc, r: (acc * 2.0 * r,))
        wgrads["mlp_w2", i] = _mm_wgrad_rows(f"mlp2_wgrad{i}", sv["p"], dm, tm_big)
        wgrads["mlp_w1", i] = _mm_wgrad_cols(f"mlp1_wgrad{i}", sv["h2"], da, tm_big)
        dh2 = _mm_nt(f"mlp1_dgrad{i}", da, weights["mlp_w1", i], 0, tm, d)
        if i % 2 == 0:
            bglu = vec(a["s5_b_glu"][j])
            bwd = _vjp_fn(f_sub1_s5, 2, (1, 1), (0, 1, 2, 3, 4, 5, 6, 7))
            dxa, dzz, dg1, dsh2, dsc2, dbglu, dgain0, dbias0 = _rowwise(
                f"sub1_bwd{i}", bwd, [sv["x"], sv["zz"], dx1, dh2], [g1, sh2, sc2], [bglu, gain0, bias0],
                [(d, F32), (2 * d, BF16)], [d, d, d], [2 * d, d, d], cfg)
            grads["s5_b_glu"][j] = dbglu[0]
            wgrads["s5_w_glu", j] = _mm_wgrad_cols(f"glu_wgrad{i}", sv["z"], dzz, tm_big)
            dz = _mm_nt(f"glu_dgrad{i}", dzz, weights["s5_w_glu", j], 0, tm, d)
            dsk = vec(a["s5_d"][j])
            bwd = _vjp_fn(_f_gelu, 3, (1,), (0, 1, 3, 4, 5))
            dxb, dy, dsh1, dsc1, ddsk = _rowwise(f"gelu_bwd{i}", bwd, [sv["x"], sv["y"][0], sv["y"][1], dz], [sh1, sc1], [dsk],
                                                 [(d, F32), (d, BF16)], [d, d], [d], cfg)
            grads["s5_d"][j] = ddsk[0]
            add_mod(i, 0, dsh1)
            add_mod(i, 1, dsc1)
            ab_re, ab_im = sv["ab"]
            dus, d_ab_re, d_ab_im, d_bb_re, d_bb_im, d_c_re, d_c_im = [], [], [], [], [], [], []
            for dirn in range(2):
                bmat_t, cmat_t = sv[f"mats{dirn}"]
                tab = _s5_tables(ab_re[dirn], ab_im[dirn], dirn == 0, True, ns, tr // SUBLANES)
                group = layer_weights(i + 1 - dirn)
                cps = _Copies([wgrads[n, idx] for n, idx, _ in group], ["scatter"] * len(group))
                du, d_b, d_c, d_a, *parts = _s5_bwd(f"s5_bwd{i}_{dirn}", dy, sv[f"states{dirn}"], sv["h"], cmat_t, bmat_t, tab,
                                                    dirn, cfg, cps)
                received.update({(n, idx): p for (n, idx, _), p in zip(group, parts)})
                dus.append(du)
                sw = d_a.shape[2] // 2
                d_a = jnp.sum(d_a, axis=1)
                d_ab_re.append(d_a[:, :sw].reshape(n_grp, n_state))
                d_ab_im.append(d_a[:, sw:].reshape(n_grp, n_state))

                def unblock_b(t):
                    return jnp.transpose(_diag_blocks(t, gs).reshape(n_grp, ch, n_state), (1, 0, 2))

                def unblock_c(t):
                    return jnp.transpose(_diag_blocks(t, gs).reshape(n_grp, n_state, ch), (0, 2, 1))
                d_bb_re.append(unblock_b(d_b[:, :, :sw]))
                d_bb_im.append(unblock_b(d_b[:, :, sw:]))
                d_c_re.append(unblock_c(d_c[:, :sw]))
                d_c_im.append(-unblock_c(d_c[:, sw:]))
            g_lre, g_lim, g_ldt, g_bre, g_bim = _s5_prep_bwd(
                f"s5_prep_bwd{i}", *sv["prep_in"], (jnp.stack(d_ab_re), jnp.stack(d_ab_im), jnp.stack(d_bb_re), jnp.stack(d_bb_im)))
            grads["s5_lam_re"][j], grads["s5_lam_im"][j], grads["s5_log_dt"][j] = g_lre, g_lim, g_ldt[:, :, 0]
            grads["s5_b_re"][j] = jnp.transpose(g_bre, (0, 2, 3, 1))
            grads["s5_b_im"][j] = jnp.transpose(g_bim, (0, 2, 3, 1))
            grads["s5_c_re"][j], grads["s5_c_im"][j] = jnp.stack(d_c_re), jnp.stack(d_c_im)
            dx_parts, dh_parts = [dxa, dxb], dus
        else:
            bpw2 = vec(full["cv_b_pw2"][j])
            bwd = _vjp_fn(f_sub1_cv, 2, (1, 1), (0, 1, 2, 3, 4, 5, 6, 7))
            dxa, dmm, dg1, dsh2, dsc2, dbpw2, dgain0, dbias0 = _rowwise(
                f"sub1_bwd{i}", bwd, [sv["x"], sv["mm"], dx1, dh2], [g1, sh2, sc2], [bpw2, gain0, bias0],
                [(d, F32), (d, BF16)], [d, d, d], [d, d, d], cfg)
            grads["cv_b_pw2"][j] = dbpw2[0]
            wgrads["cv_w_pw2", j] = _mm_wgrad_rows(f"pw2_wgrad{i}", sv["s_act"], dmm, tm_big)
            ds = _mm_nt(f"pw2_dgrad{i}", dmm, weights["cv_w_pw2", j], 0, tm_big, d)
            bdw, lng, lnb = vec(full["cv_b_dw"][j]), vec(full["cv_ln_g"][j]), vec(full["cv_ln_b"][j])
            bwd = _vjp_fn(_f_cvln, 1, (1,), (0, 1, 2, 3))
            dcv, dbdw, dlng, dlnb = _rowwise(f"cvln_bwd{i}", bwd, [sv["cv"], ds], [], [bdw, lng, lnb], [(d, F32)], [], [d, d, d], cfg)
            grads["cv_b_dw"][j], grads["cv_ln_g"][j], grads["cv_ln_b"][j] = dbdw[0], dlng[0], dlnb[0]
            dact = _dwconv(f"dwconv_bwd{i}", dcv, sv["w_dw"][::-1], cfg)
            grads["cv_w_dw"][j] = _dwconv_wgrad(f"dwconv_wgrad{i}", sv["act"], dcv, sv["w_dw"].shape[0], cfg)
            bpw1 = vec(full["cv_b_pw1"][j])
            bwd = _vjp_fn(_f_cvglu, 1, (1,), (0, 1))
            dzz, dbpw1 = _rowwise(f"cvglu_bwd{i}", bwd, [sv["zz"], dact], [], [bpw1], [(2 * d, BF16)], [], [2 * d], cfg)
            grads["cv_b_pw1"][j] = dbpw1[0]
            wgrads["cv_w_pw1", j] = _mm_wgrad_cols(f"pw1_wgrad{i}", sv["h"], dzz, tm_big)
            dh = _mm_nt(f"pw1_dgrad{i}", dzz, weights["cv_w_pw1", j], 0, tm, d)
            dx_parts, dh_parts = [dxa], [dh]
        grads["ln_gain"][i] = jnp.stack([dgain0[0], dgain1[0]])
        grads["ln_bias"][i] = jnp.stack([dbias0[0], dbias1[0]])
        add_mod(i, 2, dg1)
        add_mod(i, 3, dsh2)
        add_mod(i, 4, dsc2)
    bwd = _vjp_fn(_f_entry, 2, (len(dx_parts), len(dh_parts)), (0, 2, 3))
    dxc, dsh1, dsc1 = _rowwise("entry_bwd", bwd, [xc, pos_rows] + dx_parts + dh_parts, [seg(0, 0), seg(0, 1)], [],
                               [(d, F32)], [d, d], [], cfg)
    add_mod(0, 0, dsh1)
    add_mod(0, 1, dsc1)
    grad_x = dxc[:nb * seq].reshape(nb, seq, d)

    dmod_loc = jnp.stack([jnp.concatenate([q[:, 0] for q in dmod[i]], axis=1) for i in range(nl)])
    dmod_all = _exchange("gather_dmod", [dmod_loc.reshape(nl * (nb + 1), 6 * d)], ["slot"])[0].reshape(N_DEV, nl, nb + 1, 6 * d)
    mine = lax.dynamic_slice(dmod_all, (0, 0, 0, me * n_ada), (N_DEV, nl, nb + 1, n_ada))
    dmod_rows = jnp.transpose(mine[:, :, :nb], (1, 0, 2, 3)).reshape(nl, N_DEV * nb, n_ada)
    dmod_rows = jnp.concatenate([dmod_rows, jnp.zeros((nl, SUBLANES, n_ada), F32)], axis=1)
    g_w_ada, _, dcond = _ada_bwd(cc, a["w_ada"], dmod_rows, mine[:, :, nb:])
    g_b_ada = _sum_lead("b_ada_sum", jnp.transpose(dmod_all, (0, 2, 1, 3)).reshape(N_DEV * (nb + 1), nl, 6 * d), nl)
    dcond_all = _exchange("gather_dcond", [dcond[N_DEV * nb:N_DEV * nb + 1]], ["slot"])[0]
    g_c_ctx = _cctx_grad(dcond_all, a["c_ctx"][None])[0]

    small_names = SHARDED_SMALL + REPLICATED_SMALL
    small_full = [jnp.stack(grads[n]) for n in small_names]
    small_packed = _pack(small_full, N_DEV * SUBLANES)
    small_parts = _exchange("scatter_small_grads", [small_packed.reshape(N_DEV, -1, LANES)], ["scatter"])[0]
    small_part = _sum_lead("small_grad_sum", small_parts, _row_tile(small_parts.shape[1], 512))
    small_sum = _exchange("gather_small_sum", [small_part], ["slot"])[0]
    small_g = dict(zip(small_names, _unpack(small_sum, [g.shape for g in small_full])))
    for n in SHARDED_SMALL:
        width = a[n].shape[-1]
        start = (0,) * (small_g[n].ndim - 1) + (me * width,)
        small_g[n] = lax.dynamic_slice(small_g[n], start, a[n].shape)
    small_g["c_ctx"], small_g["b_ada"] = g_c_ctx, g_b_ada

    out = {}

    def update(n, parts):
        shp = a[n].shape
        cols = parts.shape[-1]
        rows = parts.shape[1]
        res = _adamw(f"adamw_{n}", parts, a[n].reshape(rows, cols), a["m_" + n].reshape(rows, cols), a["v_" + n].reshape(rows, cols),
                     _row_tile(rows, max(SUBLANES, 131072 // cols)))
        out[n] = [r.reshape(shp) for r in res]

    for n in BIG:
        rows, cols = a[n].shape[1:]
        bufs = [lax.empty(a[n].shape, F32) for _ in range(4)]
        for idx in range(a[n].shape[0]):
            bufs = _adamw_layer(f"adamw_{n}{idx}", received[n, idx], a[n], a["m_" + n], a["v_" + n], bufs, idx,
                                _row_tile(rows, max(SUBLANES, 131072 // cols)))
        out[n] = bufs
    update("w_ada", g_w_ada.reshape(1, -1, n_ada))
    for n in NATIVE_SMALL:
        out[n] = [small_g[n], *_adamw_native(f"adamw_{n}", small_g[n], a[n], a["m_" + n], a["v_" + n])]
    small_all_names = ("c_ctx", "b_ada") + tuple(n for n in small_names if n not in NATIVE_SMALL)
    packed = [_pack([src[n] for n in small_all_names]) for src in
              (small_g, a, {n: a["m_" + n] for n in small_all_names}, {n: a["v_" + n] for n in small_all_names})]
    res = _adamw("adamw_small", packed[0][None], packed[1], packed[2], packed[3], _row_tile(packed[0].shape[0], 512))
    shapes = [a[n].shape for n in small_all_names]
    for n, vals in zip(small_all_names, zip(*[_unpack(r, shapes) for r in res])):
        out[n] = list(vals)
    return (loss, grad_x, *[out[n][0] for n in WEIGHTS], *[out[n][1] for n in WEIGHTS],
            *[out[n][2] for n in WEIGHTS], *[out[n][3] for n in WEIGHTS])


def kernel(x, c, ctx, c_ctx, w_ada, b_ada, ln_gain, ln_bias, s5_lam_re, s5_lam_im, s5_log_dt, s5_b_re, s5_b_im, s5_c_re, s5_c_im, s5_d, s5_w_glu, s5_b_glu, cv_w_pw1, cv_b_pw1, cv_w_dw, cv_b_dw, cv_ln_g, cv_ln_b, cv_w_pw2, cv_b_pw2, mlp_w1, mlp_w2, loss_target, m_c_ctx, m_w_ada, m_b_ada, m_ln_gain, m_ln_bias, m_s5_lam_re, m_s5_lam_im, m_s5_log_dt, m_s5_b_re, m_s5_b_im, m_s5_c_re, m_s5_c_im, m_s5_d, m_s5_w_glu, m_s5_b_glu, m_cv_w_pw1, m_cv_b_pw1, m_cv_w_dw, m_cv_b_dw, m_cv_ln_g, m_cv_ln_b, m_cv_w_pw2, m_cv_b_pw2, m_mlp_w1, m_mlp_w2, v_c_ctx, v_w_ada, v_b_ada, v_ln_gain, v_ln_bias, v_s5_lam_re, v_s5_lam_im, v_s5_log_dt, v_s5_b_re, v_s5_b_im, v_s5_c_re, v_s5_c_im, v_s5_d, v_s5_w_glu, v_s5_b_glu, v_cv_w_pw1, v_cv_b_pw1, v_cv_w_dw, v_cv_b_dw, v_cv_ln_g, v_cv_ln_b, v_cv_w_pw2, v_cv_b_pw2, v_mlp_w1, v_mlp_w2):
    return _step(dict(locals()))
```

```python
import functools
import math

import jax
import jax.numpy as jnp
from jax import lax
from jax.experimental import pallas as pl
from jax.experimental.pallas import tpu as pltpu

F32 = jnp.float32
BF16 = jnp.bfloat16
N_DEV = 8
LANES = 128
SUBLANES = 8
VMEM_LIMIT = 56 * 1024 * 1024
GRID_W = 64
POS_TEMP = 10000.0
LN_EPS = 1e-5
LAMBDA_RE_MAX = -1e-4
ADAM_LR, ADAM_B1, ADAM_B2, ADAM_EPS, ADAM_WD, ADAM_STEP = 0.001, 0.9, 0.999, 1e-08, 0.01, 10
MESH = pl.DeviceIdType.MESH


def _params(sem):
    return pltpu.CompilerParams(dimension_semantics=sem, vmem_limit_bytes=VMEM_LIMIT)


def _accumulate(ref, val, first):
    @pl.when(first)
    def _():
        ref[...] = val

    @pl.when(jnp.logical_not(first))
    def _():
        ref[...] += val


def _rowwise(name, fn, rows, segs, vecs, row_outs, seg_accs, vec_accs, cfg):
    tr, tpl, nb = cfg["tr"], cfg["tpl"], cfg["nb"]
    n_rows = rows[0].shape[0]
    nt = n_rows // tr
    nr, ns, nv = len(rows), len(segs), len(vecs)
    nro, nsa = len(row_outs), len(seg_accs)

    def seg_of(t):
        return jnp.minimum(t // tpl, nb)

    def body(*refs):
        t = pl.program_id(0)
        ins, outs = refs[:nr + ns + nv], refs[nr + ns + nv:]
        vals = [r[...] for r in ins[:nr]] + [r[0] for r in ins[nr:nr + ns]] + [r[...] for r in ins[nr + ns:]]
        res = fn(*vals)
        for o, v in zip(outs[:nro], res[:nro]):
            o[...] = v.astype(o.dtype)
        first_seg = jnp.logical_or(t == 0, seg_of(t) != seg_of(jnp.maximum(t - 1, 0)))
        for o, v in zip(outs[nro:nro + nsa], res[nro:nro + nsa]):
            _accumulate(o.at[0], v, first_seg)
        for o, v in zip(outs[nro + nsa:], res[nro + nsa:]):
            _accumulate(o, v, t == 0)

    in_specs = ([pl.BlockSpec((tr, a.shape[1]), lambda t: (t, 0)) for a in rows]
                + [pl.BlockSpec((1, 1, a.shape[2]), lambda t: (seg_of(t), 0, 0)) for a in segs]
                + [pl.BlockSpec((1, a.shape[1]), lambda t: (0, 0)) for a in vecs])
    out_specs = ([pl.BlockSpec((tr, c), lambda t: (t, 0)) for c, _ in row_outs]
                 + [pl.BlockSpec((1, 1, c), lambda t: (seg_of(t), 0, 0)) for c in seg_accs]
                 + [pl.BlockSpec((1, c), lambda t: (0, 0)) for c in vec_accs])
    out_shape = ([jax.ShapeDtypeStruct((n_rows, c), dt) for c, dt in row_outs]
                 + [jax.ShapeDtypeStruct((nb + 1, 1, c), F32) for c in seg_accs]
                 + [jax.ShapeDtypeStruct((1, c), F32) for c in vec_accs])
    return pl.pallas_call(body, name=name, grid=(nt,), in_specs=in_specs, out_specs=out_specs,
                          out_shape=out_shape, compiler_params=_params(("arbitrary",)))(*rows, *segs, *vecs)


def _vjp_fn(fn, n_row, cot_groups, want):
    n_cot = sum(cot_groups)

    def bwd(*args):
        primals = [a.astype(F32) for a in args[:n_row] + args[n_row + n_cot:]]
        outs, vjp = jax.vjp(fn, *primals)
        cots, pos = [], n_row
        for n, o in zip(cot_groups, outs):
            cot = jnp.zeros_like(o)
            for part in args[pos:pos + n]:
                cot = cot + part.astype(F32)
            cots.append(cot)
            pos += n
        grads = vjp(tuple(cots))
        return tuple(grads[i] for i in want)
    return bwd


def _ln(r, g, b):
    mu = jnp.mean(r, axis=-1, keepdims=True)
    var = jnp.mean(jnp.square(r - mu), axis=-1, keepdims=True)
    return (r - mu) * lax.rsqrt(var + LN_EPS) * g + b


def _glu(zz, bias):
    d = zz.shape[1] // 2
    return (zz[:, :d] + bias[:, :d]) * jax.nn.sigmoid(zz[:, d:] + bias[:, d:])


def _f_entry(xc, pos, sh, sc):
    x0 = xc + pos
    return x0, x0 * (1 + sc) + sh


def _f_gelu(x, y0, y1, sh, sc, dsk):
    u = x * (1 + sc) + sh
    y = dsk * u + y0 + y1
    return (0.5 * y * (1.0 + lax.erf(y * (2.0 ** -0.5))),)


def _make_sub1_s5(alpha):
    def f(x, zz, g1, sh2, sc2, bglu, gain, bias):
        x1 = _ln(alpha * x + g1 * _glu(zz, bglu), gain, bias)
        return x1, x1 * (1 + sc2) + sh2
    return f


def _make_sub1_cv(alpha):
    def f(x, mm, g1, sh2, sc2, bpw2, gain, bias):
        x1 = _ln(alpha * x + g1 * (mm + bpw2), gain, bias)
        return x1, x1 * (1 + sc2) + sh2
    return f


def _make_sub2(alpha):
    def f(x1, m, g2, shn, scn, gain, bias):
        x2 = _ln(alpha * x1 + g2 * m, gain, bias)
        return x2, x2 * (1 + scn) + shn
    return f


def _f_cvglu(zz, bpw1):
    return (_glu(zz, bpw1),)


def _f_cvln(cv, bdw, lng, lnb):
    return (jax.nn.silu(_ln(cv + bdw, lng, lnb)),)


def _matmul(name, a, b, extras, grid, a_spec, b_spec, extra_specs, o_specs, out_shape, dims, red_axis, epi, sem):
    n_extra = len(extras)
    n_out = len(out_shape)
    acc_shape = o_specs[0].block_shape
    acc_shape = tuple(s for s in acc_shape if s is not None)

    def body(*refs):
        a_ref, b_ref = refs[0], refs[1]
        ex = refs[2:2 + n_extra]
        outs = refs[2 + n_extra:2 + n_extra + n_out]
        prod = lax.dot_general(a_ref[...], b_ref[...], dims, preferred_element_type=F32)

        def finish(acc):
            res = epi(acc, *[e[...] for e in ex]) if epi is not None else (acc,)
            for o, v in zip(outs, res):
                o[...] = v.astype(o.dtype)

        if red_axis is None:
            finish(prod)
        else:
            acc_ref = refs[-1]
            k = pl.program_id(red_axis)
            nk = pl.num_programs(red_axis)

            @pl.when(k == 0)
            def _():
                acc_ref[...] = prod

            @pl.when(k > 0)
            def _():
                acc_ref[...] += prod

            @pl.when(k == nk - 1)
            def _():
                finish(acc_ref[...])

    scratch = [] if red_axis is None else [pltpu.VMEM(acc_shape, F32)]
    res = pl.pallas_call(body, name=name, grid=grid, in_specs=[a_spec, b_spec] + list(extra_specs),
                         out_specs=list(o_specs), out_shape=list(out_shape), scratch_shapes=scratch,
                         compiler_params=_params(sem))(a, b, *extras)
    return res


NN = (((1,), (0,)), ((), ()))
NT = (((1,), (1,)), ((), ()))
TN = (((0,), (0,)), ((), ()))


def _mm_nn(name, a, w3, layer, tm, tn, out_dtypes=(F32,), epi=None):
    m, k = a.shape
    n = w3.shape[2]
    return _matmul(name, a, w3, (), (n // tn, m // tm),
                   pl.BlockSpec((tm, k), lambda j, i: (i, 0)), pl.BlockSpec((None, k, tn), lambda j, i: (layer, 0, j)), (),
                   [pl.BlockSpec((tm, tn), lambda j, i: (i, j)) for _ in out_dtypes],
                   [jax.ShapeDtypeStruct((m, n), dt) for dt in out_dtypes], NN, None, epi, ("arbitrary", "arbitrary"))


def _mm_nt(name, dy, w3, layer, tm, tkw, extras=(), out_dtype=F32, epi=None):
    m, n = dy.shape
    kw = w3.shape[1]
    return _matmul(name, dy, w3, tuple(extras), (kw // tkw, m // tm),
                   pl.BlockSpec((tm, n), lambda j, i: (i, 0)), pl.BlockSpec((None, tkw, n), lambda j, i: (layer, j, 0)),
                   [pl.BlockSpec((tm, tkw), lambda j, i: (i, j)) for _ in extras],
                   [pl.BlockSpec((tm, tkw), lambda j, i: (i, j))], [jax.ShapeDtypeStruct((m, kw), out_dtype)], NT, None, epi,
                   ("arbitrary", "arbitrary"))[0]


def _mm_wgrad_cols(name, a, dy, tm):
    m, k = a.shape
    n = dy.shape[1] // N_DEV
    return _matmul(name, a, dy, (), (N_DEV, m // tm),
                   pl.BlockSpec((tm, k), lambda j, i: (i, 0)), pl.BlockSpec((tm, n), lambda j, i: (i, j)), (),
                   [pl.BlockSpec((None, k, n), lambda j, i: (j, 0, 0))], [jax.ShapeDtypeStruct((N_DEV, k, n), BF16)],
                   TN, 1, None, ("arbitrary", "arbitrary"))[0]


def _mm_wgrad_rows(name, a, dy, tm):
    m = a.shape[0]
    r = a.shape[1] // N_DEV
    n = dy.shape[1]
    return _matmul(name, a, dy, (), (N_DEV, m // tm),
                   pl.BlockSpec((tm, r), lambda j, i: (i, j)), pl.BlockSpec((tm, n), lambda j, i: (i, 0)), (),
                   [pl.BlockSpec((None, r, n), lambda j, i: (j, 0, 0))], [jax.ShapeDtypeStruct((N_DEV, r, n), BF16)],
                   TN, 1, None, ("arbitrary", "arbitrary"))[0]


class _Copies:
    def __init__(self, arrays, kinds):
        self.arrays, self.kinds, self.n = list(arrays), list(kinds), len(arrays)
        any_spec = pl.BlockSpec(memory_space=pl.ANY)
        self.in_specs = [any_spec] * self.n
        self.out_specs = [any_spec] * self.n
        self.out_shape = [jax.ShapeDtypeStruct(self._result(a, kind), a.dtype) for a, kind in zip(arrays, kinds)]
        self.scratch = [pltpu.SemaphoreType.DMA((self.n, N_DEV - 1)), pltpu.SemaphoreType.DMA((self.n, N_DEV - 1)),
                        pltpu.SemaphoreType.DMA((self.n,))] if self.n else []

    @staticmethod
    def _result(a, kind):
        if kind == "slot":
            return (N_DEV,) + a.shape
        if kind == "scatter":
            return a.shape
        return a.shape[:kind] + (N_DEV * a.shape[kind],) + a.shape[kind + 1:]

    def descriptors(self, ins, outs, sems):
        send_sems, recv_sems, local_sems = sems
        x, y, c = lax.axis_index("x"), lax.axis_index("y"), lax.axis_index("c")
        me = 4 * x + 2 * y + c

        def landing(i):
            kind = self.kinds[i]
            if kind in ("slot", "scatter"):
                return outs[i].at[me]
            size = ins[i].shape[kind]
            mine = pl.ds(pl.multiple_of(me * size, size), size)
            return outs[i].at[(slice(None),) * kind + (mine,)]

        copies = []
        for i in range(self.n):
            scatter = self.kinds[i] == "scatter"
            copies.append(pltpu.make_async_copy(ins[i].at[me] if scatter else ins[i], landing(i), local_sems.at[i]))
            for k in range(1, N_DEV):
                px = 1 - x if k & 4 else x
                py = 1 - y if k & 2 else y
                pc = 1 - c if k & 1 else c
                src = ins[i].at[4 * px + 2 * py + pc] if scatter else ins[i]
                copies.append(pltpu.make_async_remote_copy(src_ref=src, dst_ref=landing(i), send_sem=send_sems.at[i, k - 1],
                                                           recv_sem=recv_sems.at[i, k - 1], device_id=(px, py, pc),
                                                           device_id_type=MESH))
        return copies


def _exchange(name, arrays, kinds):
    cps = _Copies(arrays, kinds)
    n = cps.n

    def body(*refs):
        copies = cps.descriptors(refs[:n], refs[n:2 * n], refs[2 * n:])
        for cp in copies:
            cp.start()
        for cp in copies:
            cp.wait()

    return pl.pallas_call(body, name=name, in_specs=cps.in_specs, out_specs=cps.out_specs, out_shape=cps.out_shape,
                          scratch_shapes=cps.scratch)(*arrays)


def _carried(cps, n_in, n_out, n_scratch, refs):
    if cps is None:
        return [], refs
    n = cps.n
    ins = refs[n_in:n_in + n]
    outs = refs[n_in + n + n_out:n_in + n + n_out + n]
    sems = refs[n_in + n + n_out + n + n_scratch:]
    own = refs[:n_in] + refs[n_in + n:n_in + n + n_out] + refs[n_in + n + n_out + n:n_in + n + n_out + n + n_scratch]
    return cps.descriptors(ins, outs, sems), own


def _start_all(copies, when):
    @pl.when(when)
    def _():
        for cp in copies:
            cp.start()


def _wait_all(copies, when):
    @pl.when(when)
    def _():
        for cp in copies:
            cp.wait()


def _sum_lead(name, parts, tr):
    npart, r, c = parts.shape

    def body(p_ref, o_ref):
        acc = p_ref[0].astype(F32)
        for p in range(1, npart):
            acc = acc + p_ref[p].astype(F32)
        o_ref[...] = acc

    return pl.pallas_call(body, name=name, grid=(r // tr,), in_specs=[pl.BlockSpec((npart, tr, c), lambda i: (0, i, 0))],
                          out_specs=pl.BlockSpec((tr, c), lambda i: (i, 0)), out_shape=jax.ShapeDtypeStruct((r, c), F32),
                          compiler_params=_params(("arbitrary",)))(parts)


def _adamw_math(g, w, m, v):
    m2 = ADAM_B1 * m + (1.0 - ADAM_B1) * g
    v2 = ADAM_B2 * v + (1.0 - ADAM_B2) * jnp.square(g)
    m_hat = m2 / (1.0 - ADAM_B1 ** ADAM_STEP)
    v_hat = v2 / (1.0 - ADAM_B2 ** ADAM_STEP)
    return -ADAM_LR * (m_hat / (jnp.sqrt(v_hat) + ADAM_EPS) + ADAM_WD * w), m2, v2


def _adamw_body(npart):
    def body(p_ref, w_ref, m_ref, v_ref, *rest):
        g_out, d_out, m_out, v_out = rest[-4:]
        g = p_ref[0].astype(F32)
        for p in range(1, npart):
            g = g + p_ref[p].astype(F32)
        g_out[...] = g
        d_out[...], m_out[...], v_out[...] = _adamw_math(g, w_ref[...], m_ref[...], v_ref[...])
    return body


def _adamw_native(name, g, w, m, v):
    rest = w.shape[2:]
    spec = pl.BlockSpec((None, None) + rest, lambda i, j: (i, j) + (0,) * len(rest))

    def body(g_ref, w_ref, m_ref, v_ref, d_out, m_out, v_out):
        d_out[...], m_out[...], v_out[...] = _adamw_math(g_ref[...], w_ref[...], m_ref[...], v_ref[...])

    return pl.pallas_call(body, name=name, grid=w.shape[:2], in_specs=[spec] * 4, out_specs=[spec] * 3,
                          out_shape=[jax.ShapeDtypeStruct(w.shape, F32)] * 3,
                          compiler_params=_params(("arbitrary", "arbitrary")))(g, w, m, v)


def _adamw(name, parts, w, m, v, tr):
    npart, r, c = parts.shape
    row = pl.BlockSpec((tr, c), lambda i: (i, 0))
    return pl.pallas_call(_adamw_body(npart), name=name, grid=(r // tr,),
                          in_specs=[pl.BlockSpec((npart, tr, c), lambda i: (0, i, 0)), row, row, row],
                          out_specs=[row] * 4, out_shape=[jax.ShapeDtypeStruct((r, c), F32)] * 4,
                          compiler_params=_params(("arbitrary",)))(parts, w, m, v)


def _adamw_layer(name, parts, w3, m3, v3, bufs, layer, tr):
    npart, r, c = parts.shape
    lay = pl.BlockSpec((None, tr, c), lambda i: (layer, i, 0))
    hbm = pl.BlockSpec(memory_space=pl.ANY)
    return pl.pallas_call(_adamw_body(npart), name=name, grid=(r // tr,),
                          in_specs=[pl.BlockSpec((npart, tr, c), lambda i: (0, i, 0)), lay, lay, lay] + [hbm] * 4,
                          out_specs=[lay] * 4, out_shape=[jax.ShapeDtypeStruct(w3.shape, F32)] * 4,
                          input_output_aliases={4: 0, 5: 1, 6: 2, 7: 3},
                          compiler_params=_params(("arbitrary",)))(parts, w3, m3, v3, *bufs)


def _row_tile(r, cap):
    if r <= cap:
        return r
    t = cap - cap % SUBLANES
    while r % t:
        t -= SUBLANES
    return t


def _ada_fwd(cc, w_ada, b_loc):
    nl, d, n = w_ada.shape
    rows = cc.shape[0]

    def body(c_ref, w_ref, b_ref, o_ref):
        cond = jax.nn.silu(c_ref[...]).astype(BF16)
        o_ref[...] = jnp.dot(cond, w_ref[...].astype(BF16), preferred_element_type=F32) + b_ref[...]

    return pl.pallas_call(body, name="ada_fwd", grid=(nl,),
                          in_specs=[pl.BlockSpec((rows, d), lambda i: (0, 0)), pl.BlockSpec((None, d, n), lambda i: (i, 0, 0)),
                                    pl.BlockSpec((None, 1, n), lambda i: (i, 0, 0))],
                          out_specs=pl.BlockSpec((None, rows, n), lambda i: (i, 0, 0)),
                          out_shape=jax.ShapeDtypeStruct((nl, rows, n), F32), compiler_params=_params(("arbitrary",)))(cc, w_ada, b_loc)


def _ada_bwd(cc, w_ada, dmod_rows, dmod_ctx):
    nl, d, n = w_ada.shape
    rows = cc.shape[0]
    ctx_row = rows - SUBLANES

    def body(c_ref, w_ref, dr_ref, dc_ref, gw_ref, tot_ref, dcond_ref):
        i = pl.program_id(0)
        total = dc_ref[0]
        for p in range(1, N_DEV):
            total = total + dc_ref[p]
        tot_ref[...] = total
        row_id = lax.broadcasted_iota(jnp.int32, (rows, n), 0)
        dm = jnp.where(row_id == ctx_row, jnp.broadcast_to(total, (rows, n)), dr_ref[...]).astype(BF16)
        cond = jax.nn.silu(c_ref[...]).astype(BF16)
        gw_ref[...] = lax.dot_general(cond, dm, TN, preferred_element_type=F32)
        part = lax.dot_general(dm, w_ref[...].astype(BF16), NT, preferred_element_type=F32)
        _accumulate(dcond_ref, part, i == 0)

    return pl.pallas_call(body, name="ada_bwd", grid=(nl,),
                          in_specs=[pl.BlockSpec((rows, d), lambda i: (0, 0)), pl.BlockSpec((None, d, n), lambda i: (i, 0, 0)),
                                    pl.BlockSpec((None, rows, n), lambda i: (i, 0, 0)),
                                    pl.BlockSpec((N_DEV, None, 1, n), lambda i: (0, i, 0, 0))],
                          out_specs=[pl.BlockSpec((None, d, n), lambda i: (i, 0, 0)), pl.BlockSpec((None, 1, n), lambda i: (i, 0, 0)),
                                     pl.BlockSpec((rows, d), lambda i: (0, 0))],
                          out_shape=[jax.ShapeDtypeStruct((nl, d, n), F32), jax.ShapeDtypeStruct((nl, 1, n), F32),
                                     jax.ShapeDtypeStruct((rows, d), F32)],
                          compiler_params=_params(("arbitrary",)))(cc, w_ada, dmod_rows, dmod_ctx)


def _cctx_grad(parts, c_ctx):
    def body(p_ref, c_ref, o_ref):
        tot = p_ref[0]
        for p in range(1, N_DEV):
            tot = tot + p_ref[p]
        _, vjp = jax.vjp(jax.nn.silu, c_ref[...])
        o_ref[...] = vjp(tot)[0]

    return pl.pallas_call(body, name="cctx_grad", out_shape=jax.ShapeDtypeStruct(c_ctx.shape, F32))(parts, c_ctx)


def _discretise(lam_re, lam_im, log_dt, b_re, b_im):
    lr = jnp.minimum(lam_re, LAMBDA_RE_MAX)
    li = lam_im
    dt = jnp.exp(log_dt)
    mag = jnp.exp(lr * dt)
    ab_re = mag * jnp.cos(li * dt)
    ab_im = mag * jnp.sin(li * dt)
    den = lr * lr + li * li
    nr = ab_re - 1.0
    ni = ab_im
    coef_re = ((nr * lr + ni * li) / den)[:, None]
    coef_im = ((ni * lr - nr * li) / den)[:, None]
    bb_re = coef_re * b_re - coef_im * b_im
    bb_im = coef_re * b_im + coef_im * b_re
    return ab_re, ab_im, bb_re, bb_im


def _s5_prep(name, lam_re, lam_im, log_dt, b_re, b_im):
    def body(a, b, c, d, e, o1, o2, o3, o4):
        res = _discretise(a[...], b[...], c[...], d[...], e[...])
        for o, v in zip((o1, o2, o3, o4), res):
            o[...] = v

    shp = [jax.ShapeDtypeStruct(lam_re.shape, F32)] * 2 + [jax.ShapeDtypeStruct(b_re.shape, F32)] * 2
    return pl.pallas_call(body, name=name, out_shape=shp)(lam_re, lam_im, log_dt, b_re, b_im)


def _s5_prep_bwd(name, lam_re, lam_im, log_dt, b_re, b_im, cots):
    def body(a, b, c, d, e, c1, c2, c3, c4, o1, o2, o3, o4, o5):
        _, vjp = jax.vjp(_discretise, a[...], b[...], c[...], d[...], e[...])
        grads = vjp((c1[...], c2[...], c3[...], c4[...]))
        for o, v in zip((o1, o2, o3, o4, o5), grads):
            o[...] = v

    shp = [jax.ShapeDtypeStruct(a.shape, F32) for a in (lam_re, lam_im, log_dt, b_re, b_im)]
    return pl.pallas_call(body, name=name, out_shape=shp)(lam_re, lam_im, log_dt, b_re, b_im, *cots)


def _interleave_rows(ref, dtype):
    n_j = ref.shape[0] // SUBLANES
    return jnp.concatenate([ref[pl.ds(j, SUBLANES, stride=n_j), :] for j in range(n_j)], axis=0).astype(dtype)


def _store_tokens(out_ref, ref):
    n_j = ref.shape[0] // SUBLANES
    for s in range(SUBLANES):
        out_ref[s * n_j:(s + 1) * n_j, :] = ref[pl.ds(s, n_j, stride=SUBLANES), :].astype(out_ref.dtype)


def _expand_powers(t_ref, pow_ref):
    for j in range(pow_ref.shape[0] // SUBLANES):
        row = 5 * SUBLANES + j
        pow_ref[j * SUBLANES:(j + 1) * SUBLANES, :] = jnp.broadcast_to(t_ref[row:row + 1, :], (SUBLANES, pow_ref.shape[1]))


def _scan_tile(h_ref, t_ref, pow_ref, carry_ref, up, states_ref=None):
    sw = h_ref.shape[1] // 2
    n_j = h_ref.shape[0] // SUBLANES

    def rows(g):
        if isinstance(g, int):
            return pl.ds(g * SUBLANES, SUBLANES)
        return pl.ds(pl.multiple_of(g * SUBLANES, SUBLANES), SUBLANES)

    def tab(g):
        return t_ref[rows(g), :sw], t_ref[rows(g), sw:]

    def order(i):
        return n_j - 1 - i if up else i

    def cmul_add(xr, xi, ar, ai, yr, yi):
        return xr + ar * yr - ai * yi, xi + ar * yi + ai * yr

    a_re, a_im = tab(0)

    def local_step(i, x):
        j = order(i)
        xr, xi = cmul_add(h_ref[rows(j), :sw], h_ref[rows(j), sw:], a_re, a_im, x[0], x[1])
        h_ref[rows(j), :sw] = xr
        h_ref[rows(j), sw:] = xi
        return xr, xi

    zero = jnp.zeros((SUBLANES, sw), F32)
    dr, di = lax.fori_loop(0, n_j, local_step, (zero, zero))
    for level, sh in enumerate((1, 2, 4)):
        amount = SUBLANES - sh if up else sh
        dr, di = cmul_add(dr, di, *tab(1 + level), pltpu.roll(dr, amount, 0), pltpu.roll(di, amount, 0))
    cr, ci = carry_ref[:, :sw], carry_ref[:, sw:]
    dr, di = cmul_add(dr, di, *tab(4), cr, ci)
    out_row = 0 if up else SUBLANES - 1
    carry_ref[:, :sw] = jnp.broadcast_to(dr[out_row:out_row + 1], dr.shape)
    carry_ref[:, sw:] = jnp.broadcast_to(di[out_row:out_row + 1], di.shape)
    in_row = SUBLANES - 1 if up else 0
    one = SUBLANES - 1 if up else 1
    is_in = lax.broadcasted_iota(jnp.int32, (SUBLANES, sw), 0) == in_row
    enter_r = jnp.where(is_in, cr, pltpu.roll(dr, one, 0))
    enter_i = jnp.where(is_in, ci, pltpu.roll(di, one, 0))

    def fix_step(i, state):
        j = order(i)
        xr, xi = cmul_add(h_ref[rows(j), :sw], h_ref[rows(j), sw:], pow_ref[rows(j), :sw], pow_ref[rows(j), sw:], enter_r, enter_i)
        h_ref[rows(j), :sw] = xr
        h_ref[rows(j), sw:] = xi
        if states_ref is None:
            return state
        (lr, li), (acc_r, acc_i) = state
        hr, hi = states_ref[rows(j), :sw], states_ref[rows(j), sw:]
        return (xr, xi), (acc_r + lr * hr + li * hi, acc_i + li * hr - lr * hi)

    if states_ref is None:
        lax.fori_loop(0, n_j, fix_step, 0)
        return None
    (lr, li), (acc_r, acc_i) = lax.fori_loop(0, n_j, fix_step, ((zero, zero), (zero, zero)))
    lr = jnp.where(is_in, cr, pltpu.roll(lr, one, 0))
    li = jnp.where(is_in, ci, pltpu.roll(li, one, 0))
    j0 = order(0)
    hr, hi = states_ref[rows(j0), :sw], states_ref[rows(j0), sw:]
    return acc_r + lr * hr + li * hi, acc_i + li * hr - lr * hi


def _s5_tile_index(cfg, dirn, adjoint):
    tpl, nb = cfg["tpl"], cfg["nb"]

    def idx(b, k):
        if not adjoint:
            lat = b * tpl + (k - 1 if dirn == 0 else tpl - k)
            return jnp.where(k == 0, nb * tpl + b, lat)
        lat = b * tpl + (tpl - 1 - k if dirn == 0 else k)
        return jnp.where(k == tpl, nb * tpl + b, lat)
    return idx


def _grid_ends(grid):
    ids = [pl.program_id(i) for i in range(len(grid))]
    first = functools.reduce(jnp.logical_and, [i == 0 for i in ids])
    last = functools.reduce(jnp.logical_and, [i == n - 1 for i, n in zip(ids, grid)])
    return first, last


def _s5_fwd(name, u, bmat, cmat, tab, dirn, cfg, cps=None):
    tr, tpl, nb = cfg["tr"], cfg["tpl"], cfg["nb"]
    n_rows, d = u.shape
    ns, _, sw2 = bmat.shape
    tile = _s5_tile_index(cfg, dirn, False)
    up = dirn == 1
    grid = (ns, nb, tpl + 1)

    def body(*refs):
        copies, (u_ref, b_ref, c_ref, t_ref, h_ref, y_ref, carry_ref, mix_ref, pow_ref) = _carried(cps, 4, 2, 3, refs)
        first, last = _grid_ends(grid)
        _start_all(copies, first)

        @pl.when(pl.program_id(2) == 0)
        def _():
            carry_ref[...] = jnp.zeros_like(carry_ref)

        @pl.when(jnp.logical_and(pl.program_id(1) == 0, pl.program_id(2) == 0))
        def _():
            _expand_powers(t_ref, pow_ref)

        mix_ref[...] = u_ref[...].astype(F32)
        h_ref[...] = jnp.dot(_interleave_rows(mix_ref, BF16), b_ref[...], preferred_element_type=F32)
        _scan_tile(h_ref, t_ref, pow_ref, carry_ref, up)
        mix_ref[...] = jnp.dot(h_ref[...].astype(BF16), c_ref[...], preferred_element_type=F32)
        _store_tokens(y_ref, mix_ref)
        _wait_all(copies, last)

    extra = cps if cps is not None else _Copies([], [])
    return pl.pallas_call(
        body, name=name, grid=grid,
        in_specs=[pl.BlockSpec((tr, LANES), lambda s, b, k: (tile(b, k), s)),
                  pl.BlockSpec((None, LANES, sw2), lambda s, b, k: (s, 0, 0)),
                  pl.BlockSpec((None, sw2, LANES), lambda s, b, k: (s, 0, 0)),
                  pl.BlockSpec((None, tab.shape[1], sw2), lambda s, b, k: (s, 0, 0))] + extra.in_specs,
        out_specs=[pl.BlockSpec((tr, sw2), lambda s, b, k: (tile(b, k), s)),
                   pl.BlockSpec((tr, LANES), lambda s, b, k: (tile(b, k), s))] + extra.out_specs,
        out_shape=[jax.ShapeDtypeStruct((n_rows, ns * sw2), F32), jax.ShapeDtypeStruct((n_rows, d), F32)] + extra.out_shape,
        scratch_shapes=[pltpu.VMEM((SUBLANES, sw2), F32), pltpu.VMEM((tr, LANES), F32), pltpu.VMEM((tr, sw2), F32)] + extra.scratch,
        compiler_params=_params(("arbitrary", "arbitrary", "arbitrary")))(u, bmat, cmat, tab, *extra.arrays)


def _s5_bwd(name, dy, h, u, cmat_t, bmat_t, tab, dirn, cfg, cps=None):
    tr, tpl, nb = cfg["tr"], cfg["tpl"], cfg["nb"]
    n_rows, d = u.shape
    ns, _, sw2 = cmat_t.shape
    sw = sw2 // 2
    tile = _s5_tile_index(cfg, dirn, True)
    up = dirn == 0
    grid = (ns, nb, tpl + 1)

    def body(*refs):
        copies, own = _carried(cps, 6, 4, 4, refs)
        dy_ref, h_ref, u_ref, ct_ref, bt_ref, t_ref, du_ref, db_ref, dc_ref, da_ref, lam_ref, carry_ref, mix_ref, pow_ref = own
        grid_first, grid_last = _grid_ends(grid)
        _start_all(copies, grid_first)
        first = jnp.logical_and(pl.program_id(1) == 0, pl.program_id(2) == 0)

        @pl.when(pl.program_id(2) == 0)
        def _():
            carry_ref[...] = jnp.zeros_like(carry_ref)

        @pl.when(first)
        def _():
            _expand_powers(t_ref, pow_ref)

        mix_ref[...] = dy_ref[...].astype(F32)
        dy = _interleave_rows(mix_ref, BF16)
        mix_ref[...] = u_ref[...].astype(F32)
        u_mixed = _interleave_rows(mix_ref, BF16)
        lam_ref[...] = jnp.dot(dy, ct_ref[...], preferred_element_type=F32)
        acc = _scan_tile(lam_ref, t_ref, pow_ref, carry_ref, up, h_ref)
        lam = lam_ref[...].astype(BF16)
        d_b = lax.dot_general(u_mixed, lam, TN, preferred_element_type=F32)
        d_c = lax.dot_general(h_ref[...].astype(BF16), dy, TN, preferred_element_type=F32)
        mix_ref[...] = jnp.dot(lam, bt_ref[...], preferred_element_type=F32)
        _store_tokens(du_ref, mix_ref)

        @pl.when(first)
        def _():
            db_ref[...] = d_b
            dc_ref[...] = d_c
            da_ref[:, :sw] = acc[0]
            da_ref[:, sw:] = acc[1]

        @pl.when(jnp.logical_not(first))
        def _():
            db_ref[...] += d_b
            dc_ref[...] += d_c
            da_ref[:, :sw] += acc[0]
            da_ref[:, sw:] += acc[1]

        _wait_all(copies, grid_last)

    extra = cps if cps is not None else _Copies([], [])
    return pl.pallas_call(
        body, name=name, grid=grid,
        in_specs=[pl.BlockSpec((tr, LANES), lambda s, b, k: (tile(b, k), s)),
                  pl.BlockSpec((tr, sw2), lambda s, b, k: (tile(b, k), s)),
                  pl.BlockSpec((tr, LANES), lambda s, b, k: (tile(b, k), s)),
                  pl.BlockSpec((None, LANES, sw2), lambda s, b, k: (s, 0, 0)),
                  pl.BlockSpec((None, sw2, LANES), lambda s, b, k: (s, 0, 0)),
                  pl.BlockSpec((None, tab.shape[1], sw2), lambda s, b, k: (s, 0, 0))] + extra.in_specs,
        out_specs=[pl.BlockSpec((tr, LANES), lambda s, b, k: (tile(b, k), s)),
                   pl.BlockSpec((None, LANES, sw2), lambda s, b, k: (s, 0, 0)),
                   pl.BlockSpec((None, sw2, LANES), lambda s, b, k: (s, 0, 0)),
                   pl.BlockSpec((None, SUBLANES, sw2), lambda s, b, k: (s, 0, 0))] + extra.out_specs,
        out_shape=[jax.ShapeDtypeStruct((n_rows, d), F32), jax.ShapeDtypeStruct((ns, LANES, sw2), F32),
                   jax.ShapeDtypeStruct((ns, sw2, LANES), F32), jax.ShapeDtypeStruct((ns, SUBLANES, sw2), F32)] + extra.out_shape,
        scratch_shapes=[pltpu.VMEM((tr, sw2), F32), pltpu.VMEM((SUBLANES, sw2), F32), pltpu.VMEM((tr, LANES), F32),
                        pltpu.VMEM((tr, sw2), F32)] + extra.scratch,
        compiler_params=_params(("arbitrary", "arbitrary", "arbitrary")))(dy, h, u, cmat_t, bmat_t, tab, *extra.arrays)


def _s5_tables(ab_re, ab_im, up, conj, ns, n_j):
    def powers_of(base, count):
        out = [base]
        for _ in range(count - 1):
            q_re, q_im = out[-1]
            out.append((q_re * base[0] - q_im * base[1], q_re * base[1] + q_im * base[0]))
        return out

    def spread(q):
        return jnp.broadcast_to(q[:, None, :], (q.shape[0], SUBLANES, q.shape[1]))

    steps = powers_of((ab_re.reshape(ns, -1), (-ab_im if conj else ab_im).reshape(ns, -1)), n_j)
    jumps = powers_of(steps[-1], SUBLANES)
    rows = jnp.arange(SUBLANES)
    blocks = [tuple(spread(q) for q in steps[0])]
    for sh in (1, 2, 4):
        keep = ((rows <= SUBLANES - 1 - sh) if up else (rows >= sh))[None, :, None]
        blocks.append(tuple(jnp.where(keep, q[:, None, :], 0.0) for q in jumps[sh - 1]))
    dist = range(SUBLANES, 0, -1) if up else range(1, SUBLANES + 1)
    blocks.append(tuple(jnp.stack([jumps[dd - 1][part] for dd in dist], axis=1) for part in (0, 1)))
    ordered = steps[::-1] if up else steps
    blocks.append(tuple(jnp.stack([q[part] for q in ordered], axis=1) for part in (0, 1)))
    return jnp.concatenate([jnp.concatenate([b[0] for b in blocks], axis=1), jnp.concatenate([b[1] for b in blocks], axis=1)], axis=2)


def _block_diag(blocks):
    ns, gs, a, b = blocks.shape
    eye = jnp.eye(gs, dtype=blocks.dtype)
    return (blocks[:, :, :, None, :] * eye[None, :, None, :, None]).reshape(ns, gs * a, gs * b)


def _diag_blocks(mat, gs):
    ns, ra, rb = mat.shape
    a, b = ra // gs, rb // gs
    m5 = mat.reshape(ns, gs, a, gs, b)
    eye = jnp.eye(gs, dtype=mat.dtype)
    return jnp.sum(m5 * eye[None, :, None, :, None], axis=3)


def _conv_flags(t, cfg):
    tpl, nb = cfg["tpl"], cfg["nb"]
    latent = t < nb * tpl
    first = jnp.logical_or(jnp.logical_not(latent), t % tpl == 0)
    last = jnp.logical_or(jnp.logical_not(latent), t % tpl == tpl - 1)
    return first, last


def _fill_ext(ext_ref, prev_ref, cur_ref, next_ref, t, cfg, halo):
    first, last = _conv_flags(t, cfg)
    tr = cur_ref.shape[0]
    for p in range(ext_ref.shape[0]):
        lanes = slice(p * LANES, (p + 1) * LANES)
        ext_ref[p, 0:halo, :] = jnp.where(first, 0.0, prev_ref[:, lanes])
        ext_ref[p, halo:halo + tr, :] = cur_ref[:, lanes]
        ext_ref[p, halo + tr:, :] = jnp.where(last, 0.0, next_ref[:, lanes])


CONV_LANES = 4 * LANES


def _conv_specs(tr, n_rows, halo, cw):
    per = tr // halo
    n_halo = n_rows // halo
    return [pl.BlockSpec((halo, cw), lambda c, t: (jnp.maximum(t * per - 1, 0), c)),
            pl.BlockSpec((tr, cw), lambda c, t: (t, c)),
            pl.BlockSpec((halo, cw), lambda c, t: (jnp.minimum((t + 1) * per, n_halo - 1), c))]


def _dwconv(name, a, w, cfg):
    tr = cfg["tr"]
    n_rows, d = a.shape
    kw = w.shape[0]
    half = kw // 2
    halo = 2 * SUBLANES
    cw = min(d, CONV_LANES)

    def body(prev_ref, cur_ref, next_ref, w_ref, o_ref, ext_ref):
        _fill_ext(ext_ref, prev_ref, cur_ref, next_ref, pl.program_id(1), cfg, halo)
        for p in range(cw // LANES):
            lanes = slice(p * LANES, (p + 1) * LANES)
            acc = jnp.zeros((tr, LANES), F32)
            for k in range(kw):
                acc = acc + ext_ref[p, pl.ds(halo - half + k, tr), :] * w_ref[k:k + 1, lanes]
            o_ref[:, lanes] = acc

    return pl.pallas_call(body, name=name, grid=(d // cw, n_rows // tr),
                          in_specs=_conv_specs(tr, n_rows, halo, cw) + [pl.BlockSpec((kw, cw), lambda c, t: (0, c))],
                          out_specs=pl.BlockSpec((tr, cw), lambda c, t: (t, c)),
                          out_shape=jax.ShapeDtypeStruct((n_rows, d), F32),
                          scratch_shapes=[pltpu.VMEM((cw // LANES, tr + 2 * halo, LANES), F32)],
                          compiler_params=_params(("arbitrary", "arbitrary")))(a, a, a, w)


def _dwconv_wgrad(name, a, dout, kw, cfg):
    tr = cfg["tr"]
    n_rows, d = a.shape
    half = kw // 2
    halo = 2 * SUBLANES
    cw = min(d, CONV_LANES)

    def body(prev_ref, cur_ref, next_ref, do_ref, o_ref, ext_ref):
        t = pl.program_id(1)
        _fill_ext(ext_ref, prev_ref, cur_ref, next_ref, t, cfg, halo)
        for p in range(cw // LANES):
            lanes = slice(p * LANES, (p + 1) * LANES)
            dout_t = do_ref[:, lanes]
            rows = [jnp.sum(ext_ref[p, pl.ds(halo - half + k, tr), :] * dout_t, axis=0, keepdims=True) for k in range(kw)]
            _accumulate(o_ref.at[:, lanes], jnp.concatenate(rows, axis=0), t == 0)

    return pl.pallas_call(body, name=name, grid=(d // cw, n_rows // tr),
                          in_specs=_conv_specs(tr, n_rows, halo, cw) + [pl.BlockSpec((tr, cw), lambda c, t: (t, c))],
                          out_specs=pl.BlockSpec((kw, cw), lambda c, t: (0, c)),
                          out_shape=jax.ShapeDtypeStruct((kw, d), F32),
                          scratch_shapes=[pltpu.VMEM((cw // LANES, tr + 2 * halo, LANES), F32)],
                          compiler_params=_params(("arbitrary", "arbitrary")))(a, a, a, dout)


def _sincos_1d(pos, dim):
    quarter = dim // 2
    omega = POS_TEMP ** (-jnp.arange(quarter, dtype=F32) / quarter)
    ang = pos[:, None] * omega[None, :]
    return jnp.concatenate([jnp.sin(ang), jnp.cos(ang)], axis=-1)


def _grid_pos_embed(rows, dim):
    row_idx = jnp.repeat(jnp.arange(rows), GRID_W).astype(F32)
    col_idx = jnp.tile(jnp.arange(GRID_W), rows).astype(F32)
    return jnp.concatenate([_sincos_1d(row_idx, dim // 2), _sincos_1d(col_idx, dim // 2)], axis=-1)


def _pack(arrs, row_multiple=SUBLANES):
    flat = jnp.concatenate([a.reshape(-1).astype(F32) for a in arrs])
    pad = (-flat.shape[0]) % (row_multiple * LANES)
    return jnp.pad(flat, (0, pad)).reshape(-1, LANES)


def _unpack(buf, shapes):
    flat = buf.reshape(-1)
    out, pos = [], 0
    for shp in shapes:
        n = math.prod(shp)
        out.append(flat[pos:pos + n].reshape(shp))
        pos += n
    return out


def _unpack_gathered(buf, shapes):
    flat = buf.reshape(N_DEV, -1)
    out, pos = [], 0
    for shp in shapes:
        n = math.prod(shp)
        part = flat[:, pos:pos + n].reshape((N_DEV,) + tuple(shp))
        out.append(jnp.moveaxis(part, 0, -2).reshape(tuple(shp[:-1]) + (N_DEV * shp[-1],)))
        pos += n
    return out


WEIGHTS = ("c_ctx", "w_ada", "b_ada", "ln_gain", "ln_bias", "s5_lam_re", "s5_lam_im", "s5_log_dt", "s5_b_re", "s5_b_im",
           "s5_c_re", "s5_c_im", "s5_d", "s5_w_glu", "s5_b_glu", "cv_w_pw1", "cv_b_pw1", "cv_w_dw", "cv_b_dw", "cv_ln_g",
           "cv_ln_b", "cv_w_pw2", "cv_b_pw2", "mlp_w1", "mlp_w2")
SHARDED_SMALL = ("ln_gain", "ln_bias", "cv_b_pw1", "cv_w_dw", "cv_b_dw", "cv_ln_g", "cv_ln_b", "cv_b_pw2")
REPLICATED_SMALL = ("s5_lam_re", "s5_lam_im", "s5_log_dt", "s5_b_re", "s5_b_im", "s5_c_re", "s5_c_im", "s5_d", "s5_b_glu")
NATIVE_SMALL = ("s5_lam_re", "s5_lam_im", "s5_b_re", "s5_b_im", "s5_c_re", "s5_c_im")
BIG = ("mlp_w1", "mlp_w2", "s5_w_glu", "cv_w_pw1", "cv_w_pw2")


def _step(a):
    x, c, ctx = a["x"], a["c"], a["ctx"]
    nb, seq, d = x.shape
    lc = ctx.shape[1]
    nl = a["w_ada"].shape[0]
    tr = lc
    tpl = seq // tr
    cfg = {"tr": tr, "tpl": tpl, "nb": nb}
    n_rows = nb * (seq + lc)
    alpha = (2.0 * nl) ** 0.25
    me = 4 * lax.axis_index("x") + 2 * lax.axis_index("y") + lax.axis_index("c")
    n_grp, n_state = a["s5_lam_re"].shape[2:]
    ch = a["s5_b_re"].shape[-1]
    gs = LANES // ch
    ns = d // LANES
    tm = 2 * tr if n_rows % (2 * tr) == 0 else tr
    tm_big = n_rows // 3 if n_rows % (3 * 2 * SUBLANES) == 0 else tm
    f_sub1_s5, f_sub1_cv, f_sub2 = _make_sub1_s5(alpha), _make_sub1_cv(alpha), _make_sub2(alpha)

    def layer_weights(i):
        mixer = [("s5_w_glu", i // 2, 1)] if i % 2 == 0 else [("cv_w_pw1", i // 2, 1), ("cv_w_pw2", i // 2, 0)]
        return mixer + [("mlp_w1", i, 1), ("mlp_w2", i, 0)]

    weights, wgrads, received = {}, {}, {}
    small_all = _exchange("gather_small", [_pack([a[n] for n in SHARDED_SMALL])], ["slot"])[0]
    full = dict(zip(SHARDED_SMALL, _unpack_gathered(small_all, [a[n].shape for n in SHARDED_SMALL])))
    c_all = _exchange("gather_c", [c], ["slot"])[0].reshape(N_DEV * nb, d)
    cond_rows = N_DEV * nb + SUBLANES
    cc = jnp.concatenate([c_all, a["c_ctx"][None], jnp.zeros((SUBLANES - 1, d), F32)], axis=0)

    n_ada = a["w_ada"].shape[2]
    b_loc = lax.dynamic_slice(a["b_ada"], (0, me * n_ada), (nl, n_ada))[:, None, :]
    mod_cols = _ada_fwd(cc, a["w_ada"], b_loc)
    mod_all = _exchange("gather_mod", [mod_cols.reshape(nl * cond_rows, n_ada)], ["slot"])[0].reshape(N_DEV, nl, cond_rows, n_ada)
    mod_mine = jnp.concatenate([lax.dynamic_slice(mod_all, (0, 0, nb * me, 0), (N_DEV, nl, nb, n_ada)),
                                mod_all[:, :, N_DEV * nb:N_DEV * nb + 1]], axis=2)
    mod = jnp.transpose(mod_mine, (1, 2, 0, 3)).reshape(nl, nb + 1, 6, 1, d)

    def seg(i, q):
        return mod[i, :, q]

    zero_seg = jnp.zeros((nb + 1, 1, d), F32)

    def vec(v):
        return v.reshape(1, -1)

    pos = _grid_pos_embed(seq // GRID_W, d)
    xc = jnp.concatenate([x.reshape(nb * seq, d), ctx.reshape(nb * lc, d)], axis=0)
    pos_rows = jnp.concatenate([jnp.tile(pos, (nb, 1)), jnp.zeros((nb * lc, d), F32)], axis=0)
    x_cur, h_cur = _rowwise("entry", _f_entry, [xc, pos_rows], [seg(0, 0), seg(0, 1)], [], [(d, F32), (d, BF16)], [], [], cfg)
    saved = []
    for i in range(nl):
        j = i // 2
        sv = {"x": x_cur, "h": h_cur}
        sh1, sc1, g1, sh2, sc2, g2 = (seg(i, q) for q in range(6))
        gain0, bias0, gain1, bias1 = (vec(full["ln_gain"][i, 0]), vec(full["ln_bias"][i, 0]),
                                      vec(full["ln_gain"][i, 1]), vec(full["ln_bias"][i, 1]))
        if i % 2 == 0:
            lam_re, lam_im = a["s5_lam_re"][j], a["s5_lam_im"][j]
            log_dt = a["s5_log_dt"][j][:, :, None]
            b_re_t = jnp.transpose(a["s5_b_re"][j], (0, 3, 1, 2))
            b_im_t = jnp.transpose(a["s5_b_im"][j], (0, 3, 1, 2))
            sv["prep_in"] = (lam_re, lam_im, log_dt, b_re_t, b_im_t)
            ab_re, ab_im, bb_re, bb_im = _s5_prep(f"s5_prep{i}", *sv["prep_in"])
            sv["ab"] = (ab_re, ab_im)
            ys = []
            for dirn in range(2):
                def blocks(t):
                    return jnp.transpose(t, (1, 0, 2)).reshape(ns, gs, ch, n_state)
                bmat = jnp.concatenate([_block_diag(blocks(bb_re[dirn])), _block_diag(blocks(bb_im[dirn]))], axis=2).astype(BF16)
                c_re_t = jnp.transpose(a["s5_c_re"][j, dirn], (0, 2, 1)).reshape(ns, gs, n_state, ch)
                c_im_t = jnp.transpose(a["s5_c_im"][j, dirn], (0, 2, 1)).reshape(ns, gs, n_state, ch)
                cmat = jnp.concatenate([_block_diag(c_re_t), -_block_diag(c_im_t)], axis=1).astype(BF16)
                tab = _s5_tables(ab_re[dirn], ab_im[dirn], dirn == 1, False, ns, tr // SUBLANES)
                group = layer_weights(i + dirn)
                cps = _Copies([a[n][idx].astype(BF16) for n, idx, _ in group], [axis for _, _, axis in group])
                h_states, y_dir, *gathered = _s5_fwd(f"s5_fwd{i}_{dirn}", h_cur, bmat, cmat, tab, dirn, cfg, cps)
                weights.update({(n, idx): w[None] for (n, idx, _), w in zip(group, gathered)})
                sv[f"mats{dirn}"] = (jnp.transpose(bmat, (0, 2, 1)), jnp.transpose(cmat, (0, 2, 1)))
                sv[f"states{dirn}"] = h_states
                ys.append(y_dir)
            sv["y"] = ys
            dsk = vec(a["s5_d"][j])
            z = _rowwise(f"gelu{i}", _f_gelu, [x_cur, ys[0], ys[1]], [sh1, sc1], [dsk], [(d, BF16)], [], [], cfg)[0]
            zz = _mm_nn(f"glu{i}", z, weights["s5_w_glu", j], 0, tm_big, min(2 * d, 512))[0]
            bglu = vec(a["s5_b_glu"][j])
            x1, h2 = _rowwise(f"sub1_{i}", f_sub1_s5, [x_cur, zz], [g1, sh2, sc2], [bglu, gain0, bias0],
                              [(d, F32), (d, BF16)], [], [], cfg)
            sv.update(z=z, zz=zz)
        else:
            zz = _mm_nn(f"pw1_{i}", h_cur, weights["cv_w_pw1", j], 0, tm_big, min(2 * d, 512))[0]
            bpw1 = vec(full["cv_b_pw1"][j])
            act = _rowwise(f"cvglu{i}", _f_cvglu, [zz], [], [bpw1], [(d, F32)], [], [], cfg)[0]
            w_dw = full["cv_w_dw"][j]
            cv = _dwconv(f"dwconv{i}", act, w_dw, cfg)
            bdw, lng, lnb = vec(full["cv_b_dw"][j]), vec(full["cv_ln_g"][j]), vec(full["cv_ln_b"][j])
            s_act = _rowwise(f"cvln{i}", _f_cvln, [cv], [], [bdw, lng, lnb], [(d, BF16)], [], [], cfg)[0]
            mm = _mm_nn(f"pw2_{i}", s_act, weights["cv_w_pw2", j], 0, tm_big, d)[0]
            bpw2 = vec(full["cv_b_pw2"][j])
            x1, h2 = _rowwise(f"sub1_{i}", f_sub1_cv, [x_cur, mm], [g1, sh2, sc2], [bpw2, gain0, bias0],
                              [(d, F32), (d, BF16)], [], [], cfg)
            sv.update(zz=zz, act=act, cv=cv, s_act=s_act, mm=mm, w_dw=w_dw)
        dff = weights["mlp_w1", i].shape[2]
        p_act, r_act = _mm_nn(f"mlp1_{i}", h2, weights["mlp_w1", i], 0, tm_big, min(dff, 1024), (BF16, BF16),
                              lambda acc: (jnp.square(jnp.maximum(acc, 0.0)), jnp.maximum(acc, 0.0)))
        m_out = _mm_nn(f"mlp2_{i}", p_act, weights["mlp_w2", i], 0, tm, d)[0]
        shn, scn = (seg(i + 1, 0), seg(i + 1, 1)) if i + 1 < nl else (zero_seg, zero_seg)
        x2, hn = _rowwise(f"sub2_{i}", f_sub2, [x1, m_out], [g2, shn, scn], [gain1, bias1], [(d, F32), (d, BF16)], [], [], cfg)
        sv.update(x1=x1, h2=h2, p=p_act, r=r_act, m=m_out, shn=shn, scn=scn)
        saved.append(sv)
        x_cur, h_cur = x2, hn

    target = jnp.concatenate([a["loss_target"].reshape(nb * seq, d), jnp.zeros((nb * lc, d), F32)], axis=0)
    mask = jnp.concatenate([jnp.ones((nb, 1, d), F32), jnp.zeros((1, 1, d), F32)], axis=0)

    def f_loss(xf, tgt, msk):
        err = (xf - tgt) * msk
        part = 0.5 * jnp.sum(jnp.square(err), axis=(0, 1), keepdims=True) / d
        return err / d, jnp.broadcast_to(part, (1, LANES))

    dx_final, loss_part = _rowwise("loss", f_loss, [x_cur, target], [mask], [], [(d, F32)], [], [LANES], cfg)
    loss = lax.psum(loss_part[0, 0], ("x", "y", "c"))

    grads = {n: [None] * a[n].shape[0] for n in WEIGHTS if n not in ("c_ctx", "w_ada", "b_ada")}
    dmod = [[None] * 6 for _ in range(nl)]

    def add_mod(i, q, val):
        dmod[i][q] = val if dmod[i][q] is None else dmod[i][q] + val

    dx_parts, dh_parts = [dx_final], []
    for i in reversed(range(nl)):
        j = i // 2
        sv = saved[i]
        sh1, sc1, g1, sh2, sc2, g2 = (seg(i, q) for q in range(6))
        gain0, bias0, gain1, bias1 = (vec(full["ln_gain"][i, 0]), vec(full["ln_bias"][i, 0]),
                                      vec(full["ln_gain"][i, 1]), vec(full["ln_bias"][i, 1]))
        bwd = _vjp_fn(f_sub2, 2, (len(dx_parts), len(dh_parts)), (0, 1, 2, 3, 4, 5, 6))
        dx1, dm, dg2, dshn, dscn, dgain1, dbias1 = _rowwise(
            f"sub2_bwd{i}", bwd, [sv["x1"], sv["m"]] + dx_parts + dh_parts, [g2, sv["shn"], sv["scn"]], [gain1, bias1],
            [(d, F32), (d, BF16)], [d, d, d], [d, d], cfg)
        add_mod(i, 5, dg2)
        if i + 1 < nl:
            add_mod(i + 1, 0, dshn)
            add_mod(i + 1, 1, dscn)
        da = _mm_nt(f"mlp2_dgrad{i}", dm, weights["mlp_w2", i], 0, tm_big, min(dff, 1024), [sv["r"]], BF16,
                    lambda acc, r: (acc * 2.0 * r,))
        wgrads["mlp_w2", i] = _mm_wgrad_rows(f"mlp2_wgrad{i}", sv["p"], dm, tm_big)
        wgrads["mlp_w1", i] = _mm_wgrad_cols(f"mlp1_wgrad{i}", sv["h2"], da, tm_big)
        dh2 = _mm_nt(f"mlp1_dgrad{i}", da, weights["mlp_w1", i], 0, tm, d)
        if i % 2 == 0:
            bglu = vec(a["s5_b_glu"][j])
            bwd = _vjp_fn(f_sub1_s5, 2, (1, 1), (0, 1, 2, 3, 4, 5, 6, 7))
            dxa, dzz, dg1, dsh2, dsc2, dbglu, dgain0, dbias0 = _rowwise(
                f"sub1_bwd{i}", bwd, [sv["x"], sv["zz"], dx1, dh2], [g1, sh2, sc2], [bglu, gain0, bias0],
                [(d, F32), (2 * d, BF16)], [d, d, d], [2 * d, d, d], cfg)
            grads["s5_b_glu"][j] = dbglu[0]
            wgrads["s5_w_glu", j] = _mm_wgrad_cols(f"glu_wgrad{i}", sv["z"], dzz, tm_big)
            dz = _mm_nt(f"glu_dgrad{i}", dzz, weights["s5_w_glu", j], 0, tm, d)
            dsk = vec(a["s5_d"][j])
            bwd = _vjp_fn(_f_gelu, 3, (1,), (0, 1, 3, 4, 5))
            dxb, dy, dsh1, dsc1, ddsk = _rowwise(f"gelu_bwd{i}", bwd, [sv["x"], sv["y"][0], sv["y"][1], dz], [sh1, sc1], [dsk],
                                                 [(d, F32), (d, BF16)], [d, d], [d], cfg)
            grads["s5_d"][j] = ddsk[0]
            add_mod(i, 0, dsh1)
            add_mod(i, 1, dsc1)
            ab_re, ab_im = sv["ab"]
            dus, d_ab_re, d_ab_im, d_bb_re, d_bb_im, d_c_re, d_c_im = [], [], [], [], [], [], []
            for dirn in range(2):
                bmat_t, cmat_t = sv[f"mats{dirn}"]
                tab = _s5_tables(ab_re[dirn], ab_im[dirn], dirn == 0, True, ns, tr // SUBLANES)
                group = layer_weights(i + 1 - dirn)
                cps = _Copies([wgrads[n, idx] for n, idx, _ in group], ["scatter"] * len(group))
                du, d_b, d_c, d_a, *parts = _s5_bwd(f"s5_bwd{i}_{dirn}", dy, sv[f"states{dirn}"], sv["h"], cmat_t, bmat_t, tab,
                                                    dirn, cfg, cps)
                received.update({(n, idx): p for (n, idx, _), p in zip(group, parts)})
                dus.append(du)
                sw = d_a.shape[2] // 2
                d_a = jnp.sum(d_a, axis=1)
                d_ab_re.append(d_a[:, :sw].reshape(n_grp, n_state))
                d_ab_im.append(d_a[:, sw:].reshape(n_grp, n_state))

                def unblock_b(t):
                    return jnp.transpose(_diag_blocks(t, gs).reshape(n_grp, ch, n_state), (1, 0, 2))

                def unblock_c(t):
                    return jnp.transpose(_diag_blocks(t, gs).reshape(n_grp, n_state, ch), (0, 2, 1))
                d_bb_re.append(unblock_b(d_b[:, :, :sw]))
                d_bb_im.append(unblock_b(d_b[:, :, sw:]))
                d_c_re.append(unblock_c(d_c[:, :sw]))
                d_c_im.append(-unblock_c(d_c[:, sw:]))
            g_lre, g_lim, g_ldt, g_bre, g_bim = _s5_prep_bwd(
                f"s5_prep_bwd{i}", *sv["prep_in"], (jnp.stack(d_ab_re), jnp.stack(d_ab_im), jnp.stack(d_bb_re), jnp.stack(d_bb_im)))
            grads["s5_lam_re"][j], grads["s5_lam_im"][j], grads["s5_log_dt"][j] = g_lre, g_lim, g_ldt[:, :, 0]
            grads["s5_b_re"][j] = jnp.transpose(g_bre, (0, 2, 3, 1))
            grads["s5_b_im"][j] = jnp.transpose(g_bim, (0, 2, 3, 1))
            grads["s5_c_re"][j], grads["s5_c_im"][j] = jnp.stack(d_c_re), jnp.stack(d_c_im)
            dx_parts, dh_parts = [dxa, dxb], dus
        else:
            bpw2 = vec(full["cv_b_pw2"][j])
            bwd = _vjp_fn(f_sub1_cv, 2, (1, 1), (0, 1, 2, 3, 4, 5, 6, 7))
            dxa, dmm, dg1, dsh2, dsc2, dbpw2, dgain0, dbias0 = _rowwise(
                f"sub1_bwd{i}", bwd, [sv["x"], sv["mm"], dx1, dh2], [g1, sh2, sc2], [bpw2, gain0, bias0],
                [(d, F32), (d, BF16)], [d, d, d], [d, d, d], cfg)
            grads["cv_b_pw2"][j] = dbpw2[0]
            wgrads["cv_w_pw2", j] = _mm_wgrad_rows(f"pw2_wgrad{i}", sv["s_act"], dmm, tm_big)
            ds = _mm_nt(f"pw2_dgrad{i}", dmm, weights["cv_w_pw2", j], 0, tm_big, d)
            bdw, lng, lnb = vec(full["cv_b_dw"][j]), vec(full["cv_ln_g"][j]), vec(full["cv_ln_b"][j])
            bwd = _vjp_fn(_f_cvln, 1, (1,), (0, 1, 2, 3))
            dcv, dbdw, dlng, dlnb = _rowwise(f"cvln_bwd{i}", bwd, [sv["cv"], ds], [], [bdw, lng, lnb], [(d, F32)], [], [d, d, d], cfg)
            grads["cv_b_dw"][j], grads["cv_ln_g"][j], grads["cv_ln_b"][j] = dbdw[0], dlng[0], dlnb[0]
            dact = _dwconv(f"dwconv_bwd{i}", dcv, sv["w_dw"][::-1], cfg)
            grads["cv_w_dw"][j] = _dwconv_wgrad(f"dwconv_wgrad{i}", sv["act"], dcv, sv["w_dw"].shape[0], cfg)
            bpw1 = vec(full["cv_b_pw1"][j])
            bwd = _vjp_fn(_f_cvglu, 1, (1,), (0, 1))
            dzz, dbpw1 = _rowwise(f"cvglu_bwd{i}", bwd, [sv["zz"], dact], [], [bpw1], [(2 * d, BF16)], [], [2 * d], cfg)
            grads["cv_b_pw1"][j] = dbpw1[0]
            wgrads["cv_w_pw1", j] = _mm_wgrad_cols(f"pw1_wgrad{i}", sv["h"], dzz, tm_big)
            dh = _mm_nt(f"pw1_dgrad{i}", dzz, weights["cv_w_pw1", j], 0, tm, d)
            dx_parts, dh_parts = [dxa], [dh]
        grads["ln_gain"][i] = jnp.stack([dgain0[0], dgain1[0]])
        grads["ln_bias"][i] = jnp.stack([dbias0[0], dbias1[0]])
        add_mod(i, 2, dg1)
        add_mod(i, 3, dsh2)
        add_mod(i, 4, dsc2)
    bwd = _vjp_fn(_f_entry, 2, (len(dx_parts), len(dh_parts)), (0, 2, 3))
    dxc, dsh1, dsc1 = _rowwise("entry_bwd", bwd, [xc, pos_rows] + dx_parts + dh_parts, [seg(0, 0), seg(0, 1)], [],
                               [(d, F32)], [d, d], [], cfg)
    add_mod(0, 0, dsh1)
    add_mod(0, 1, dsc1)
    grad_x = dxc[:nb * seq].reshape(nb, seq, d)

    dmod_loc = jnp.stack([jnp.concatenate([q[:, 0] for q in dmod[i]], axis=1) for i in range(nl)])
    dmod_all = _exchange("gather_dmod", [dmod_loc.reshape(nl * (nb + 1), 6 * d)], ["slot"])[0].reshape(N_DEV, nl, nb + 1, 6 * d)
    mine = lax.dynamic_slice(dmod_all, (0, 0, 0, me * n_ada), (N_DEV, nl, nb + 1, n_ada))
    dmod_rows = jnp.transpose(mine[:, :, :nb], (1, 0, 2, 3)).reshape(nl, N_DEV * nb, n_ada)
    dmod_rows = jnp.concatenate([dmod_rows, jnp.zeros((nl, SUBLANES, n_ada), F32)], axis=1)
    g_w_ada, _, dcond = _ada_bwd(cc, a["w_ada"], dmod_rows, mine[:, :, nb:])
    g_b_ada = _sum_lead("b_ada_sum", jnp.transpose(dmod_all, (0, 2, 1, 3)).reshape(N_DEV * (nb + 1), nl, 6 * d), nl)
    dcond_all = _exchange("gather_dcond", [dcond[N_DEV * nb:N_DEV * nb + 1]], ["slot"])[0]
    g_c_ctx = _cctx_grad(dcond_all, a["c_ctx"][None])[0]

    small_names = SHARDED_SMALL + REPLICATED_SMALL
    small_full = [jnp.stack(grads[n]) for n in small_names]
    small_packed = _pack(small_full, N_DEV * SUBLANES)
    small_parts = _exchange("scatter_small_grads", [small_packed.reshape(N_DEV, -1, LANES)], ["scatter"])[0]
    small_part = _sum_lead("small_grad_sum", small_parts, _row_tile(small_parts.shape[1], 512))
    small_sum = _exchange("gather_small_sum", [small_part], ["slot"])[0]
    small_g = dict(zip(small_names, _unpack(small_sum, [g.shape for g in small_full])))
    for n in SHARDED_SMALL:
        width = a[n].shape[-1]
        start = (0,) * (small_g[n].ndim - 1) + (me * width,)
        small_g[n] = lax.dynamic_slice(small_g[n], start, a[n].shape)
    small_g["c_ctx"], small_g["b_ada"] = g_c_ctx, g_b_ada

    out = {}

    def update(n, parts):
        shp = a[n].shape
        cols = parts.shape[-1]
        rows = parts.shape[1]
        res = _adamw(f"adamw_{n}", parts, a[n].reshape(rows, cols), a["m_" + n].reshape(rows, cols), a["v_" + n].reshape(rows, cols),
                     _row_tile(rows, max(SUBLANES, 131072 // cols)))
        out[n] = [r.reshape(shp) for r in res]

    for n in BIG:
        rows, cols = a[n].shape[1:]
        bufs = [lax.empty(a[n].shape, F32) for _ in range(4)]
        for idx in range(a[n].shape[0]):
            bufs = _adamw_layer(f"adamw_{n}{idx}", received[n, idx], a[n], a["m_" + n], a["v_" + n], bufs, idx,
                                _row_tile(rows, max(SUBLANES, 131072 // cols)))
        out[n] = bufs
    update("w_ada", g_w_ada.reshape(1, -1, n_ada))
    for n in NATIVE_SMALL:
        out[n] = [small_g[n], *_adamw_native(f"adamw_{n}", small_g[n], a[n], a["m_" + n], a["v_" + n])]
    small_all_names = ("c_ctx", "b_ada") + tuple(n for n in small_names if n not in NATIVE_SMALL)
    packed = [_pack([src[n] for n in small_all_names]) for src in
              (small_g, a, {n: a["m_" + n] for n in small_all_names}, {n: a["v_" + n] for n in small_all_names})]
    res = _adamw("adamw_small", packed[0][None], packed[1], packed[2], packed[3], _row_tile(packed[0].shape[0], 512))
    shapes = [a[n].shape for n in small_all_names]
    for n, vals in zip(small_all_names, zip(*[_unpack(r, shapes) for r in res])):
        out[n] = list(vals)
    return (loss, grad_x, *[out[n][0] for n in WEIGHTS], *[out[n][1] for n in WEIGHTS],
            *[out[n][2] for n in WEIGHTS], *[out[n][3] for n in WEIGHTS])


def kernel(x, c, ctx, c_ctx, w_ada, b_ada, ln_gain, ln_bias, s5_lam_re, s5_lam_im, s5_log_dt, s5_b_re, s5_b_im, s5_c_re, s5_c_im, s5_d, s5_w_glu, s5_b_glu, cv_w_pw1, cv_b_pw1, cv_w_dw, cv_b_dw, cv_ln_g, cv_ln_b, cv_w_pw2, cv_b_pw2, mlp_w1, mlp_w2, loss_target, m_c_ctx, m_w_ada, m_b_ada, m_ln_gain, m_ln_bias, m_s5_lam_re, m_s5_lam_im, m_s5_log_dt, m_s5_b_re, m_s5_b_im, m_s5_c_re, m_s5_c_im, m_s5_d, m_s5_w_glu, m_s5_b_glu, m_cv_w_pw1, m_cv_b_pw1, m_cv_w_dw, m_cv_b_dw, m_cv_ln_g, m_cv_ln_b, m_cv_w_pw2, m_cv_b_pw2, m_mlp_w1, m_mlp_w2, v_c_ctx, v_w_ada, v_b_ada, v_ln_gain, v_ln_bias, v_s5_lam_re, v_s5_lam_im, v_s5_log_dt, v_s5_b_re, v_s5_b_im, v_s5_c_re, v_s5_c_im, v_s5_d, v_s5_w_glu, v_s5_b_glu, v_cv_w_pw1, v_cv_b_pw1, v_cv_w_dw, v_cv_b_dw, v_cv_ln_g, v_cv_ln_b, v_cv_w_pw2, v_cv_b_pw2, v_mlp_w1, v_mlp_w2):
    return _step(dict(locals()))
```

```python
import functools
import math

import jax
import jax.numpy as jnp
from jax import lax
from jax.experimental import pallas as pl
from jax.experimental.pallas import tpu as pltpu

F32 = jnp.float32
BF16 = jnp.bfloat16
N_DEV = 8
LANES = 128
SUBLANES = 8
VMEM_LIMIT = 56 * 1024 * 1024
GRID_W = 64
POS_TEMP = 10000.0
LN_EPS = 1e-5
LAMBDA_RE_MAX = -1e-4
ADAM_LR, ADAM_B1, ADAM_B2, ADAM_EPS, ADAM_WD, ADAM_STEP = 0.001, 0.9, 0.999, 1e-08, 0.01, 10
MESH = pl.DeviceIdType.MESH


def _params(sem):
    return pltpu.CompilerParams(dimension_semantics=sem, vmem_limit_bytes=VMEM_LIMIT)


def _accumulate(ref, val, first):
    @pl.when(first)
    def _():
        ref[...] = val

    @pl.when(jnp.logical_not(first))
    def _():
        ref[...] += val


def _rowwise(name, fn, rows, segs, vecs, row_outs, seg_accs, vec_accs, cfg):
    tr, tpl, nb = cfg["tr"], cfg["tpl"], cfg["nb"]
    n_rows = rows[0].shape[0]
    nt = n_rows // tr
    nr, ns, nv = len(rows), len(segs), len(vecs)
    nro, nsa = len(row_outs), len(seg_accs)

    def seg_of(t):
        return jnp.minimum(t // tpl, nb)

    def body(*refs):
        t = pl.program_id(0)
        ins, outs = refs[:nr + ns + nv], refs[nr + ns + nv:]
        vals = [r[...] for r in ins[:nr]] + [r[0] for r in ins[nr:nr + ns]] + [r[...] for r in ins[nr + ns:]]
        res = fn(*vals)
        for o, v in zip(outs[:nro], res[:nro]):
            o[...] = v.astype(o.dtype)
        first_seg = jnp.logical_or(t == 0, seg_of(t) != seg_of(jnp.maximum(t - 1, 0)))
        for o, v in zip(outs[nro:nro + nsa], res[nro:nro + nsa]):
            _accumulate(o.at[0], v, first_seg)
        for o, v in zip(outs[nro + nsa:], res[nro + nsa:]):
            _accumulate(o, v, t == 0)

    in_specs = ([pl.BlockSpec((tr, a.shape[1]), lambda t: (_phys_tile(t, cfg), 0)) for a in rows]
                + [pl.BlockSpec((1, 1, a.shape[2]), lambda t: (seg_of(t), 0, 0)) for a in segs]
                + [pl.BlockSpec((1, a.shape[1]), lambda t: (0, 0)) for a in vecs])
    out_specs = ([pl.BlockSpec((tr, c), lambda t: (_phys_tile(t, cfg), 0)) for c, _ in row_outs]
                 + [pl.BlockSpec((1, 1, c), lambda t: (seg_of(t), 0, 0)) for c in seg_accs]
                 + [pl.BlockSpec((1, c), lambda t: (0, 0)) for c in vec_accs])
    out_shape = ([jax.ShapeDtypeStruct((n_rows, c), dt) for c, dt in row_outs]
                 + [jax.ShapeDtypeStruct((nb + 1, 1, c), F32) for c in seg_accs]
                 + [jax.ShapeDtypeStruct((1, c), F32) for c in vec_accs])
    return pl.pallas_call(body, name=name, grid=(nt,), in_specs=in_specs, out_specs=out_specs,
                          out_shape=out_shape, compiler_params=_params(("arbitrary",)))(*rows, *segs, *vecs)


def _vjp_fn(fn, n_row, cot_groups, want):
    n_cot = sum(cot_groups)

    def bwd(*args):
        primals = [a.astype(F32) for a in args[:n_row] + args[n_row + n_cot:]]
        outs, vjp = jax.vjp(fn, *primals)
        cots, pos = [], n_row
        for n, o in zip(cot_groups, outs):
            cot = jnp.zeros_like(o)
            for part in args[pos:pos + n]:
                cot = cot + part.astype(F32)
            cots.append(cot)
            pos += n
        grads = vjp(tuple(cots))
        return tuple(grads[i] for i in want)
    return bwd


def _ln(r, g, b):
    mu = jnp.mean(r, axis=-1, keepdims=True)
    var = jnp.mean(jnp.square(r - mu), axis=-1, keepdims=True)
    return (r - mu) * lax.rsqrt(var + LN_EPS) * g + b


def _glu(zz, bias):
    d = zz.shape[1] // 2
    return (zz[:, :d] + bias[:, :d]) * jax.nn.sigmoid(zz[:, d:] + bias[:, d:])


def _f_entry(xc, pos, sh, sc):
    x0 = xc + pos
    return x0, x0 * (1 + sc) + sh


def _f_gelu(x, y0, y1, sh, sc, dsk):
    u = x * (1 + sc) + sh
    y = dsk * u + y0 + y1
    return (0.5 * y * (1.0 + lax.erf(y * (2.0 ** -0.5))),)


def _make_sub1_s5(alpha):
    def f(x, zz, g1, sh2, sc2, bglu, gain, bias):
        x1 = _ln(alpha * x + g1 * _glu(zz, bglu), gain, bias)
        return x1, x1 * (1 + sc2) + sh2
    return f


def _make_sub1_cv(alpha):
    def f(x, mm, g1, sh2, sc2, bpw2, gain, bias):
        x1 = _ln(alpha * x + g1 * (mm + bpw2), gain, bias)
        return x1, x1 * (1 + sc2) + sh2
    return f


def _make_sub2(alpha):
    def f(x1, m, g2, shn, scn, gain, bias):
        x2 = _ln(alpha * x1 + g2 * m, gain, bias)
        return x2, x2 * (1 + scn) + shn
    return f


def _f_cvglu(zz, bpw1):
    return (_glu(zz, bpw1),)


def _f_cvln(cv, bdw, lng, lnb):
    return (jax.nn.silu(_ln(cv + bdw, lng, lnb)),)


def _matmul(name, a, b, extras, grid, a_spec, b_spec, extra_specs, o_specs, out_shape, dims, red_axis, epi, sem):
    n_extra = len(extras)
    n_out = len(out_shape)
    acc_shape = o_specs[0].block_shape
    acc_shape = tuple(s for s in acc_shape if s is not None)

    def body(*refs):
        a_ref, b_ref = refs[0], refs[1]
        ex = refs[2:2 + n_extra]
        outs = refs[2 + n_extra:2 + n_extra + n_out]
        prod = lax.dot_general(a_ref[...], b_ref[...], dims, preferred_element_type=F32)

        def finish(acc):
            res = epi(acc, *[e[...] for e in ex]) if epi is not None else (acc,)
            for o, v in zip(outs, res):
                o[...] = v.astype(o.dtype)

        if red_axis is None:
            finish(prod)
        else:
            acc_ref = refs[-1]
            k = pl.program_id(red_axis)
            nk = pl.num_programs(red_axis)

            @pl.when(k == 0)
            def _():
                acc_ref[...] = prod

            @pl.when(k > 0)
            def _():
                acc_ref[...] += prod

            @pl.when(k == nk - 1)
            def _():
                finish(acc_ref[...])

    scratch = [] if red_axis is None else [pltpu.VMEM(acc_shape, F32)]
    res = pl.pallas_call(body, name=name, grid=grid, in_specs=[a_spec, b_spec] + list(extra_specs),
                         out_specs=list(o_specs), out_shape=list(out_shape), scratch_shapes=scratch,
                         compiler_params=_params(sem))(a, b, *extras)
    return res


NN = (((1,), (0,)), ((), ()))
NT = (((1,), (1,)), ((), ()))
TN = (((0,), (0,)), ((), ()))


def _mm_nn(name, a, w3, layer, tm, tn, out_dtypes=(F32,), epi=None):
    m, k = a.shape
    n = w3.shape[2]
    return _matmul(name, a, w3, (), (n // tn, m // tm),
                   pl.BlockSpec((tm, k), lambda j, i: (i, 0)), pl.BlockSpec((None, k, tn), lambda j, i: (layer, 0, j)), (),
                   [pl.BlockSpec((tm, tn), lambda j, i: (i, j)) for _ in out_dtypes],
                   [jax.ShapeDtypeStruct((m, n), dt) for dt in out_dtypes], NN, None, epi, ("arbitrary", "arbitrary"))


def _mm_nt(name, dy, w3, layer, tm, tkw, extras=(), out_dtype=F32, epi=None):
    m, n = dy.shape
    kw = w3.shape[1]
    return _matmul(name, dy, w3, tuple(extras), (kw // tkw, m // tm),
                   pl.BlockSpec((tm, n), lambda j, i: (i, 0)), pl.BlockSpec((None, tkw, n), lambda j, i: (layer, j, 0)),
                   [pl.BlockSpec((tm, tkw), lambda j, i: (i, j)) for _ in extras],
                   [pl.BlockSpec((tm, tkw), lambda j, i: (i, j))], [jax.ShapeDtypeStruct((m, kw), out_dtype)], NT, None, epi,
                   ("arbitrary", "arbitrary"))[0]


def _mm_wgrad_cols(name, a, dy, tm):
    m, k = a.shape
    n = dy.shape[1] // N_DEV
    return _matmul(name, a, dy, (), (N_DEV, m // tm),
                   pl.BlockSpec((tm, k), lambda j, i: (i, 0)), pl.BlockSpec((tm, n), lambda j, i: (i, j)), (),
                   [pl.BlockSpec((None, k, n), lambda j, i: (j, 0, 0))], [jax.ShapeDtypeStruct((N_DEV, k, n), BF16)],
                   TN, 1, None, ("arbitrary", "arbitrary"))[0]


def _mm_wgrad_rows(name, a, dy, tm):
    m = a.shape[0]
    r = a.shape[1] // N_DEV
    n = dy.shape[1]
    return _matmul(name, a, dy, (), (N_DEV, m // tm),
                   pl.BlockSpec((tm, r), lambda j, i: (i, j)), pl.BlockSpec((tm, n), lambda j, i: (i, 0)), (),
                   [pl.BlockSpec((None, r, n), lambda j, i: (j, 0, 0))], [jax.ShapeDtypeStruct((N_DEV, r, n), BF16)],
                   TN, 1, None, ("arbitrary", "arbitrary"))[0]


class _Copies:
    def __init__(self, arrays, kinds):
        self.arrays, self.kinds, self.n = list(arrays), list(kinds), len(arrays)
        any_spec = pl.BlockSpec(memory_space=pl.ANY)
        self.in_specs = [any_spec] * self.n
        self.out_specs = [any_spec] * self.n
        self.out_shape = [jax.ShapeDtypeStruct(self._result(a, kind), a.dtype) for a, kind in zip(arrays, kinds)]
        self.scratch = [pltpu.SemaphoreType.DMA((self.n, N_DEV - 1)), pltpu.SemaphoreType.DMA((self.n, N_DEV - 1)),
                        pltpu.SemaphoreType.DMA((self.n,))] if self.n else []

    @staticmethod
    def _result(a, kind):
        if kind == "slot":
            return (N_DEV,) + a.shape
        if kind == "scatter":
            return a.shape
        return a.shape[:kind] + (N_DEV * a.shape[kind],) + a.shape[kind + 1:]

    def descriptors(self, ins, outs, sems):
        send_sems, recv_sems, local_sems = sems
        x, y, c = lax.axis_index("x"), lax.axis_index("y"), lax.axis_index("c")
        me = 4 * x + 2 * y + c

        def landing(i):
            kind = self.kinds[i]
            if kind in ("slot", "scatter"):
                return outs[i].at[me]
            size = ins[i].shape[kind]
            mine = pl.ds(pl.multiple_of(me * size, size), size)
            return outs[i].at[(slice(None),) * kind + (mine,)]

        copies = []
        for i in range(self.n):
            scatter = self.kinds[i] == "scatter"
            copies.append(pltpu.make_async_copy(ins[i].at[me] if scatter else ins[i], landing(i), local_sems.at[i]))
            for k in range(1, N_DEV):
                px = 1 - x if k & 4 else x
                py = 1 - y if k & 2 else y
                pc = 1 - c if k & 1 else c
                src = ins[i].at[4 * px + 2 * py + pc] if scatter else ins[i]
                copies.append(pltpu.make_async_remote_copy(src_ref=src, dst_ref=landing(i), send_sem=send_sems.at[i, k - 1],
                                                           recv_sem=recv_sems.at[i, k - 1], device_id=(px, py, pc),
                                                           device_id_type=MESH))
        return copies


def _exchange(name, arrays, kinds):
    cps = _Copies(arrays, kinds)
    n = cps.n

    def body(*refs):
        copies = cps.descriptors(refs[:n], refs[n:2 * n], refs[2 * n:])
        for cp in copies:
            cp.start()
        for cp in copies:
            cp.wait()

    return pl.pallas_call(body, name=name, in_specs=cps.in_specs, out_specs=cps.out_specs, out_shape=cps.out_shape,
                          scratch_shapes=cps.scratch)(*arrays)


def _carried(cps, n_in, n_out, n_scratch, refs):
    if cps is None:
        return [], refs
    n = cps.n
    ins = refs[n_in:n_in + n]
    outs = refs[n_in + n + n_out:n_in + n + n_out + n]
    sems = refs[n_in + n + n_out + n + n_scratch:]
    own = refs[:n_in] + refs[n_in + n:n_in + n + n_out] + refs[n_in + n + n_out + n:n_in + n + n_out + n + n_scratch]
    return cps.descriptors(ins, outs, sems), own


def _start_all(copies, when):
    @pl.when(when)
    def _():
        for cp in copies:
            cp.start()


def _wait_all(copies, when):
    @pl.when(when)
    def _():
        for cp in copies:
            cp.wait()


def _sum_lead(name, parts, tr):
    npart, r, c = parts.shape

    def body(p_ref, o_ref):
        acc = p_ref[0].astype(F32)
        for p in range(1, npart):
            acc = acc + p_ref[p].astype(F32)
        o_ref[...] = acc

    return pl.pallas_call(body, name=name, grid=(r // tr,), in_specs=[pl.BlockSpec((npart, tr, c), lambda i: (0, i, 0))],
                          out_specs=pl.BlockSpec((tr, c), lambda i: (i, 0)), out_shape=jax.ShapeDtypeStruct((r, c), F32),
                          compiler_params=_params(("arbitrary",)))(parts)


def _adamw_math(g, w, m, v):
    m2 = ADAM_B1 * m + (1.0 - ADAM_B1) * g
    v2 = ADAM_B2 * v + (1.0 - ADAM_B2) * jnp.square(g)
    m_hat = m2 / (1.0 - ADAM_B1 ** ADAM_STEP)
    v_hat = v2 / (1.0 - ADAM_B2 ** ADAM_STEP)
    return -ADAM_LR * (m_hat / (jnp.sqrt(v_hat) + ADAM_EPS) + ADAM_WD * w), m2, v2


def _adamw_body(npart):
    def body(p_ref, w_ref, m_ref, v_ref, *rest):
        g_out, d_out, m_out, v_out = rest[-4:]
        g = p_ref[0].astype(F32)
        for p in range(1, npart):
            g = g + p_ref[p].astype(F32)
        g_out[...] = g
        d_out[...], m_out[...], v_out[...] = _adamw_math(g, w_ref[...], m_ref[...], v_ref[...])
    return body


def _adamw_native(name, g, w, m, v):
    rest = w.shape[2:]
    spec = pl.BlockSpec((None, None) + rest, lambda i, j: (i, j) + (0,) * len(rest))

    def body(g_ref, w_ref, m_ref, v_ref, d_out, m_out, v_out):
        d_out[...], m_out[...], v_out[...] = _adamw_math(g_ref[...], w_ref[...], m_ref[...], v_ref[...])

    return pl.pallas_call(body, name=name, grid=w.shape[:2], in_specs=[spec] * 4, out_specs=[spec] * 3,
                          out_shape=[jax.ShapeDtypeStruct(w.shape, F32)] * 3,
                          compiler_params=_params(("arbitrary", "arbitrary")))(g, w, m, v)


def _adamw(name, parts, w, m, v, tr):
    npart, r, c = parts.shape
    row = pl.BlockSpec((tr, c), lambda i: (i, 0))
    return pl.pallas_call(_adamw_body(npart), name=name, grid=(r // tr,),
                          in_specs=[pl.BlockSpec((npart, tr, c), lambda i: (0, i, 0)), row, row, row],
                          out_specs=[row] * 4, out_shape=[jax.ShapeDtypeStruct((r, c), F32)] * 4,
                          compiler_params=_params(("arbitrary",)))(parts, w, m, v)


def _adamw_layer(name, parts, w3, m3, v3, bufs, layer, tr):
    npart, r, c = parts.shape
    lay = pl.BlockSpec((None, tr, c), lambda i: (layer, i, 0))
    hbm = pl.BlockSpec(memory_space=pl.ANY)
    return pl.pallas_call(_adamw_body(npart), name=name, grid=(r // tr,),
                          in_specs=[pl.BlockSpec((npart, tr, c), lambda i: (0, i, 0)), lay, lay, lay] + [hbm] * 4,
                          out_specs=[lay] * 4, out_shape=[jax.ShapeDtypeStruct(w3.shape, F32)] * 4,
                          input_output_aliases={4: 0, 5: 1, 6: 2, 7: 3},
                          compiler_params=_params(("arbitrary",)))(parts, w3, m3, v3, *bufs)


def _row_tile(r, cap):
    if r <= cap:
        return r
    t = cap - cap % SUBLANES
    while r % t:
        t -= SUBLANES
    return t


def _ada_fwd(cc, w_ada, b_loc):
    nl, d, n = w_ada.shape
    rows = cc.shape[0]

    def body(c_ref, w_ref, b_ref, o_ref):
        cond = jax.nn.silu(c_ref[...]).astype(BF16)
        o_ref[...] = jnp.dot(cond, w_ref[...].astype(BF16), preferred_element_type=F32) + b_ref[...]

    return pl.pallas_call(body, name="ada_fwd", grid=(nl,),
                          in_specs=[pl.BlockSpec((rows, d), lambda i: (0, 0)), pl.BlockSpec((None, d, n), lambda i: (i, 0, 0)),
                                    pl.BlockSpec((None, 1, n), lambda i: (i, 0, 0))],
                          out_specs=pl.BlockSpec((None, rows, n), lambda i: (i, 0, 0)),
                          out_shape=jax.ShapeDtypeStruct((nl, rows, n), F32), compiler_params=_params(("arbitrary",)))(cc, w_ada, b_loc)


def _ada_bwd(cc, w_ada, dmod_rows, dmod_ctx):
    nl, d, n = w_ada.shape
    rows = cc.shape[0]
    ctx_row = rows - SUBLANES

    def body(c_ref, w_ref, dr_ref, dc_ref, gw_ref, tot_ref, dcond_ref):
        i = pl.program_id(0)
        total = dc_ref[0]
        for p in range(1, N_DEV):
            total = total + dc_ref[p]
        tot_ref[...] = total
        row_id = lax.broadcasted_iota(jnp.int32, (rows, n), 0)
        dm = jnp.where(row_id == ctx_row, jnp.broadcast_to(total, (rows, n)), dr_ref[...]).astype(BF16)
        cond = jax.nn.silu(c_ref[...]).astype(BF16)
        gw_ref[...] = lax.dot_general(cond, dm, TN, preferred_element_type=F32)
        part = lax.dot_general(dm, w_ref[...].astype(BF16), NT, preferred_element_type=F32)
        _accumulate(dcond_ref, part, i == 0)

    return pl.pallas_call(body, name="ada_bwd", grid=(nl,),
                          in_specs=[pl.BlockSpec((rows, d), lambda i: (0, 0)), pl.BlockSpec((None, d, n), lambda i: (i, 0, 0)),
                                    pl.BlockSpec((None, rows, n), lambda i: (i, 0, 0)),
                                    pl.BlockSpec((N_DEV, None, 1, n), lambda i: (0, i, 0, 0))],
                          out_specs=[pl.BlockSpec((None, d, n), lambda i: (i, 0, 0)), pl.BlockSpec((None, 1, n), lambda i: (i, 0, 0)),
                                     pl.BlockSpec((rows, d), lambda i: (0, 0))],
                          out_shape=[jax.ShapeDtypeStruct((nl, d, n), F32), jax.ShapeDtypeStruct((nl, 1, n), F32),
                                     jax.ShapeDtypeStruct((rows, d), F32)],
                          compiler_params=_params(("arbitrary",)))(cc, w_ada, dmod_rows, dmod_ctx)


def _cctx_grad(parts, c_ctx):
    def body(p_ref, c_ref, o_ref):
        tot = p_ref[0]
        for p in range(1, N_DEV):
            tot = tot + p_ref[p]
        _, vjp = jax.vjp(jax.nn.silu, c_ref[...])
        o_ref[...] = vjp(tot)[0]

    return pl.pallas_call(body, name="cctx_grad", out_shape=jax.ShapeDtypeStruct(c_ctx.shape, F32))(parts, c_ctx)


def _discretise(lam_re, lam_im, log_dt, b_re, b_im):
    lr = jnp.minimum(lam_re, LAMBDA_RE_MAX)
    li = lam_im
    dt = jnp.exp(log_dt)
    mag = jnp.exp(lr * dt)
    ab_re = mag * jnp.cos(li * dt)
    ab_im = mag * jnp.sin(li * dt)
    den = lr * lr + li * li
    nr = ab_re - 1.0
    ni = ab_im
    coef_re = ((nr * lr + ni * li) / den)[:, None]
    coef_im = ((ni * lr - nr * li) / den)[:, None]
    bb_re = coef_re * b_re - coef_im * b_im
    bb_im = coef_re * b_im + coef_im * b_re
    return ab_re, ab_im, bb_re, bb_im


def _s5_prep(name, lam_re, lam_im, log_dt, b_re, b_im):
    def body(a, b, c, d, e, o1, o2, o3, o4):
        res = _discretise(a[...], b[...], c[...], d[...], e[...])
        for o, v in zip((o1, o2, o3, o4), res):
            o[...] = v

    shp = [jax.ShapeDtypeStruct(lam_re.shape, F32)] * 2 + [jax.ShapeDtypeStruct(b_re.shape, F32)] * 2
    return pl.pallas_call(body, name=name, out_shape=shp)(lam_re, lam_im, log_dt, b_re, b_im)


def _s5_prep_bwd(name, lam_re, lam_im, log_dt, b_re, b_im, cots):
    def body(a, b, c, d, e, c1, c2, c3, c4, o1, o2, o3, o4, o5):
        _, vjp = jax.vjp(_discretise, a[...], b[...], c[...], d[...], e[...])
        grads = vjp((c1[...], c2[...], c3[...], c4[...]))
        for o, v in zip((o1, o2, o3, o4, o5), grads):
            o[...] = v

    shp = [jax.ShapeDtypeStruct(a.shape, F32) for a in (lam_re, lam_im, log_dt, b_re, b_im)]
    return pl.pallas_call(body, name=name, out_shape=shp)(lam_re, lam_im, log_dt, b_re, b_im, *cots)


def _interleave_rows(ref, n_seq, dtype):
    n_j = ref.shape[0] // (SUBLANES * n_seq)
    return jnp.concatenate([ref[pl.ds(q * SUBLANES * n_j + j, SUBLANES, stride=n_j), :] for q in range(n_seq) for j in range(n_j)],
                           axis=0).astype(dtype)


def _store_tokens(out_ref, ref, n_seq):
    n_j = ref.shape[0] // (SUBLANES * n_seq)
    for q in range(n_seq):
        for s in range(SUBLANES):
            start = (q * SUBLANES + s) * n_j
            out_ref[start:start + n_j, :] = ref[pl.ds(q * SUBLANES * n_j + s, n_j, stride=SUBLANES), :].astype(out_ref.dtype)


def _expand_powers(t_ref, pow_ref):
    for j in range(pow_ref.shape[0] // SUBLANES):
        row = 5 * SUBLANES + j
        pow_ref[j * SUBLANES:(j + 1) * SUBLANES, :] = jnp.broadcast_to(t_ref[row:row + 1, :], (SUBLANES, pow_ref.shape[1]))


def _scan_tile(h_ref, t_ref, pow_ref, carry_ref, up, n_seq, states_ref=None):
    sw = h_ref.shape[1] // 2
    n_j = h_ref.shape[0] // (SUBLANES * n_seq)
    seqs = range(n_seq)

    def rows(g):
        if isinstance(g, int):
            return pl.ds(g * SUBLANES, SUBLANES)
        return pl.ds(pl.multiple_of(g * SUBLANES, SUBLANES), SUBLANES)

    def at(q, j):
        return rows(q * n_j + j)

    def tab(g):
        return t_ref[rows(g), :sw], t_ref[rows(g), sw:]

    def order(i):
        return n_j - 1 - i if up else i

    def cmul_add(xr, xi, ar, ai, yr, yi):
        return xr + ar * yr - ai * yi, xi + ar * yi + ai * yr

    a_re, a_im = tab(0)

    def local_step(i, xs):
        j = order(i)
        out = []
        for q in seqs:
            xr, xi = cmul_add(h_ref[at(q, j), :sw], h_ref[at(q, j), sw:], a_re, a_im, *xs[q])
            h_ref[at(q, j), :sw] = xr
            h_ref[at(q, j), sw:] = xi
            out.append((xr, xi))
        return tuple(out)

    zero = jnp.zeros((SUBLANES, sw), F32)
    ends = lax.fori_loop(0, n_j, local_step, tuple((zero, zero) for _ in seqs))
    out_row = 0 if up else SUBLANES - 1
    in_row = SUBLANES - 1 if up else 0
    one = SUBLANES - 1 if up else 1
    is_in = lax.broadcasted_iota(jnp.int32, (SUBLANES, sw), 0) == in_row
    carried, enters = [], []
    for q in seqs:
        dr, di = ends[q]
        for level, sh in enumerate((1, 2, 4)):
            amount = SUBLANES - sh if up else sh
            dr, di = cmul_add(dr, di, *tab(1 + level), pltpu.roll(dr, amount, 0), pltpu.roll(di, amount, 0))
        cr, ci = carry_ref[rows(q), :sw], carry_ref[rows(q), sw:]
        dr, di = cmul_add(dr, di, *tab(4), cr, ci)
        carry_ref[rows(q), :sw] = jnp.broadcast_to(dr[out_row:out_row + 1], dr.shape)
        carry_ref[rows(q), sw:] = jnp.broadcast_to(di[out_row:out_row + 1], di.shape)
        carried.append((cr, ci))
        enters.append((jnp.where(is_in, cr, pltpu.roll(dr, one, 0)), jnp.where(is_in, ci, pltpu.roll(di, one, 0))))

    def fix_step(i, state):
        j = order(i)
        nows = []
        for q in seqs:
            xr, xi = cmul_add(h_ref[at(q, j), :sw], h_ref[at(q, j), sw:], pow_ref[rows(j), :sw], pow_ref[rows(j), sw:], *enters[q])
            h_ref[at(q, j), :sw] = xr
            h_ref[at(q, j), sw:] = xi
            nows.append((xr, xi))
        if states_ref is None:
            return state
        befores, (acc_r, acc_i) = state
        for q in seqs:
            lr, li = befores[q]
            hr, hi = states_ref[at(q, j), :sw], states_ref[at(q, j), sw:]
            acc_r, acc_i = acc_r + lr * hr + li * hi, acc_i + li * hr - lr * hi
        return tuple(nows), (acc_r, acc_i)

    if states_ref is None:
        lax.fori_loop(0, n_j, fix_step, 0)
        return None
    lasts, (acc_r, acc_i) = lax.fori_loop(0, n_j, fix_step, (tuple((zero, zero) for _ in seqs), (zero, zero)))
    for q in seqs:
        lr = jnp.where(is_in, carried[q][0], pltpu.roll(lasts[q][0], one, 0))
        li = jnp.where(is_in, carried[q][1], pltpu.roll(lasts[q][1], one, 0))
        hr, hi = states_ref[at(q, order(0)), :sw], states_ref[at(q, order(0)), sw:]
        acc_r, acc_i = acc_r + lr * hr + li * hi, acc_i + li * hr - lr * hi
    return acc_r, acc_i


def _phys_tile(t, cfg):
    tpl, nb = cfg["tpl"], cfg["nb"]
    return jnp.where(t < nb * tpl, (t % tpl) * nb + t // tpl, t)


def _s5_block_index(cfg, dirn, adjoint):
    tpl = cfg["tpl"]

    def idx(k):
        if not adjoint:
            return jnp.where(k == 0, tpl, k - 1 if dirn == 0 else tpl - k)
        return jnp.where(k == tpl, tpl, tpl - 1 - k if dirn == 0 else k)
    return idx


def _grid_ends(grid):
    ids = [pl.program_id(i) for i in range(len(grid))]
    first = functools.reduce(jnp.logical_and, [i == 0 for i in ids])
    last = functools.reduce(jnp.logical_and, [i == n - 1 for i, n in zip(ids, grid)])
    return first, last


def _s5_fwd(name, u, bmat, cmat, tab, dirn, cfg, cps=None):
    tr, tpl, nb = cfg["tr"], cfg["tpl"], cfg["nb"]
    n_rows, d = u.shape
    ns, _, sw2 = bmat.shape
    block = _s5_block_index(cfg, dirn, False)
    up = dirn == 1
    grid = (ns, tpl + 1)
    br = nb * tr

    def body(*refs):
        copies, (u_ref, b_ref, c_ref, t_ref, h_ref, y_ref, carry_ref, mix_ref, pow_ref) = _carried(cps, 4, 2, 3, refs)
        first, last = _grid_ends(grid)
        _start_all(copies, first)

        @pl.when(pl.program_id(1) == 0)
        def _():
            carry_ref[...] = jnp.zeros_like(carry_ref)
            _expand_powers(t_ref, pow_ref)

        mix_ref[...] = u_ref[...].astype(F32)
        h_ref[...] = jnp.dot(_interleave_rows(mix_ref, nb, BF16), b_ref[...], preferred_element_type=F32)
        _scan_tile(h_ref, t_ref, pow_ref, carry_ref, up, nb)
        mix_ref[...] = jnp.dot(h_ref[...].astype(BF16), c_ref[...], preferred_element_type=F32)
        _store_tokens(y_ref, mix_ref, nb)
        _wait_all(copies, last)

    extra = cps if cps is not None else _Copies([], [])
    return pl.pallas_call(
        body, name=name, grid=grid,
        in_specs=[pl.BlockSpec((br, LANES), lambda s, k: (block(k), s)),
                  pl.BlockSpec((None, LANES, sw2), lambda s, k: (s, 0, 0)),
                  pl.BlockSpec((None, sw2, LANES), lambda s, k: (s, 0, 0)),
                  pl.BlockSpec((None, tab.shape[1], sw2), lambda s, k: (s, 0, 0))] + extra.in_specs,
        out_specs=[pl.BlockSpec((br, sw2), lambda s, k: (block(k), s)),
                   pl.BlockSpec((br, LANES), lambda s, k: (block(k), s))] + extra.out_specs,
        out_shape=[jax.ShapeDtypeStruct((n_rows, ns * sw2), F32), jax.ShapeDtypeStruct((n_rows, d), F32)] + extra.out_shape,
        scratch_shapes=[pltpu.VMEM((nb * SUBLANES, sw2), F32), pltpu.VMEM((br, LANES), F32), pltpu.VMEM((tr, sw2), F32)] + extra.scratch,
        compiler_params=_params(("arbitrary", "arbitrary")))(u, bmat, cmat, tab, *extra.arrays)


def _s5_bwd(name, dy, h, u, cmat_t, bmat_t, tab, dirn, cfg, cps=None):
    tr, tpl, nb = cfg["tr"], cfg["tpl"], cfg["nb"]
    n_rows, d = u.shape
    ns, _, sw2 = cmat_t.shape
    sw = sw2 // 2
    block = _s5_block_index(cfg, dirn, True)
    up = dirn == 0
    grid = (ns, tpl + 1)
    br = nb * tr

    def body(*refs):
        copies, own = _carried(cps, 6, 4, 4, refs)
        dy_ref, h_ref, u_ref, ct_ref, bt_ref, t_ref, du_ref, db_ref, dc_ref, da_ref, lam_ref, carry_ref, mix_ref, pow_ref = own
        grid_first, grid_last = _grid_ends(grid)
        _start_all(copies, grid_first)
        first = pl.program_id(1) == 0

        @pl.when(first)
        def _():
            carry_ref[...] = jnp.zeros_like(carry_ref)
            _expand_powers(t_ref, pow_ref)

        mix_ref[...] = dy_ref[...].astype(F32)
        dy = _interleave_rows(mix_ref, nb, BF16)
        mix_ref[...] = u_ref[...].astype(F32)
        u_mixed = _interleave_rows(mix_ref, nb, BF16)
        lam_ref[...] = jnp.dot(dy, ct_ref[...], preferred_element_type=F32)
        acc = _scan_tile(lam_ref, t_ref, pow_ref, carry_ref, up, nb, h_ref)
        lam = lam_ref[...].astype(BF16)
        d_b = lax.dot_general(u_mixed, lam, TN, preferred_element_type=F32)
        d_c = lax.dot_general(h_ref[...].astype(BF16), dy, TN, preferred_element_type=F32)
        mix_ref[...] = jnp.dot(lam, bt_ref[...], preferred_element_type=F32)
        _store_tokens(du_ref, mix_ref, nb)

        @pl.when(first)
        def _():
            db_ref[...] = d_b
            dc_ref[...] = d_c
            da_ref[:, :sw] = acc[0]
            da_ref[:, sw:] = acc[1]

        @pl.when(jnp.logical_not(first))
        def _():
            db_ref[...] += d_b
            dc_ref[...] += d_c
            da_ref[:, :sw] += acc[0]
            da_ref[:, sw:] += acc[1]

        _wait_all(copies, grid_last)

    extra = cps if cps is not None else _Copies([], [])
    return pl.pallas_call(
        body, name=name, grid=grid,
        in_specs=[pl.BlockSpec((br, LANES), lambda s, k: (block(k), s)),
                  pl.BlockSpec((br, sw2), lambda s, k: (block(k), s)),
                  pl.BlockSpec((br, LANES), lambda s, k: (block(k), s)),
                  pl.BlockSpec((None, LANES, sw2), lambda s, k: (s, 0, 0)),
                  pl.BlockSpec((None, sw2, LANES), lambda s, k: (s, 0, 0)),
                  pl.BlockSpec((None, tab.shape[1], sw2), lambda s, k: (s, 0, 0))] + extra.in_specs,
        out_specs=[pl.BlockSpec((br, LANES), lambda s, k: (block(k), s)),
                   pl.BlockSpec((None, LANES, sw2), lambda s, k: (s, 0, 0)),
                   pl.BlockSpec((None, sw2, LANES), lambda s, k: (s, 0, 0)),
                   pl.BlockSpec((None, SUBLANES, sw2), lambda s, k: (s, 0, 0))] + extra.out_specs,
        out_shape=[jax.ShapeDtypeStruct((n_rows, d), F32), jax.ShapeDtypeStruct((ns, LANES, sw2), F32),
                   jax.ShapeDtypeStruct((ns, sw2, LANES), F32), jax.ShapeDtypeStruct((ns, SUBLANES, sw2), F32)] + extra.out_shape,
        scratch_shapes=[pltpu.VMEM((br, sw2), F32), pltpu.VMEM((nb * SUBLANES, sw2), F32), pltpu.VMEM((br, LANES), F32),
                        pltpu.VMEM((tr, sw2), F32)] + extra.scratch,
        compiler_params=_params(("arbitrary", "arbitrary")))(dy, h, u, cmat_t, bmat_t, tab, *extra.arrays)


def _s5_tables(ab_re, ab_im, up, conj, ns, n_j):
    def powers_of(base, count):
        out = [base]
        for _ in range(count - 1):
            q_re, q_im = out[-1]
            out.append((q_re * base[0] - q_im * base[1], q_re * base[1] + q_im * base[0]))
        return out

    def spread(q):
        return jnp.broadcast_to(q[:, None, :], (q.shape[0], SUBLANES, q.shape[1]))

    steps = powers_of((ab_re.reshape(ns, -1), (-ab_im if conj else ab_im).reshape(ns, -1)), n_j)
    jumps = powers_of(steps[-1], SUBLANES)
    rows = jnp.arange(SUBLANES)
    blocks = [tuple(spread(q) for q in steps[0])]
    for sh in (1, 2, 4):
        keep = ((rows <= SUBLANES - 1 - sh) if up else (rows >= sh))[None, :, None]
        blocks.append(tuple(jnp.where(keep, q[:, None, :], 0.0) for q in jumps[sh - 1]))
    dist = range(SUBLANES, 0, -1) if up else range(1, SUBLANES + 1)
    blocks.append(tuple(jnp.stack([jumps[dd - 1][part] for dd in dist], axis=1) for part in (0, 1)))
    ordered = steps[::-1] if up else steps
    blocks.append(tuple(jnp.stack([q[part] for q in ordered], axis=1) for part in (0, 1)))
    return jnp.concatenate([jnp.concatenate([b[0] for b in blocks], axis=1), jnp.concatenate([b[1] for b in blocks], axis=1)], axis=2)


def _block_diag(blocks):
    ns, gs, a, b = blocks.shape
    eye = jnp.eye(gs, dtype=blocks.dtype)
    return (blocks[:, :, :, None, :] * eye[None, :, None, :, None]).reshape(ns, gs * a, gs * b)


def _diag_blocks(mat, gs):
    ns, ra, rb = mat.shape
    a, b = ra // gs, rb // gs
    m5 = mat.reshape(ns, gs, a, gs, b)
    eye = jnp.eye(gs, dtype=mat.dtype)
    return jnp.sum(m5 * eye[None, :, None, :, None], axis=3)


def _conv_flags(t, cfg):
    tpl, nb = cfg["tpl"], cfg["nb"]
    latent = t < nb * tpl
    first = jnp.logical_or(jnp.logical_not(latent), t % tpl == 0)
    last = jnp.logical_or(jnp.logical_not(latent), t % tpl == tpl - 1)
    return first, last


def _fill_ext(ext_ref, prev_ref, cur_ref, next_ref, t, cfg, halo):
    first, last = _conv_flags(t, cfg)
    tr = cur_ref.shape[0]
    for p in range(ext_ref.shape[0]):
        lanes = slice(p * LANES, (p + 1) * LANES)
        ext_ref[p, 0:halo, :] = jnp.where(first, 0.0, prev_ref[:, lanes])
        ext_ref[p, halo:halo + tr, :] = cur_ref[:, lanes]
        ext_ref[p, halo + tr:, :] = jnp.where(last, 0.0, next_ref[:, lanes])


CONV_LANES = 4 * LANES


def _conv_specs(tr, n_rows, halo, cw, cfg):
    per = tr // halo
    n_halo = n_rows // halo
    nb = cfg["nb"]
    return [pl.BlockSpec((halo, cw), lambda c, t: (jnp.maximum((_phys_tile(t, cfg) - nb + 1) * per - 1, 0), c)),
            pl.BlockSpec((tr, cw), lambda c, t: (_phys_tile(t, cfg), c)),
            pl.BlockSpec((halo, cw), lambda c, t: (jnp.minimum((_phys_tile(t, cfg) + nb) * per, n_halo - 1), c))]


def _dwconv(name, a, w, cfg):
    tr = cfg["tr"]
    n_rows, d = a.shape
    kw = w.shape[0]
    half = kw // 2
    halo = 2 * SUBLANES
    cw = min(d, CONV_LANES)

    def body(prev_ref, cur_ref, next_ref, w_ref, o_ref, ext_ref):
        _fill_ext(ext_ref, prev_ref, cur_ref, next_ref, pl.program_id(1), cfg, halo)
        for p in range(cw // LANES):
            lanes = slice(p * LANES, (p + 1) * LANES)
            acc = jnp.zeros((tr, LANES), F32)
            for k in range(kw):
                acc = acc + ext_ref[p, pl.ds(halo - half + k, tr), :] * w_ref[k:k + 1, lanes]
            o_ref[:, lanes] = acc

    return pl.pallas_call(body, name=name, grid=(d // cw, n_rows // tr),
                          in_specs=_conv_specs(tr, n_rows, halo, cw, cfg) + [pl.BlockSpec((kw, cw), lambda c, t: (0, c))],
                          out_specs=pl.BlockSpec((tr, cw), lambda c, t: (_phys_tile(t, cfg), c)),
                          out_shape=jax.ShapeDtypeStruct((n_rows, d), F32),
                          scratch_shapes=[pltpu.VMEM((cw // LANES, tr + 2 * halo, LANES), F32)],
                          compiler_params=_params(("arbitrary", "arbitrary")))(a, a, a, w)


def _dwconv_wgrad(name, a, dout, kw, cfg):
    tr = cfg["tr"]
    n_rows, d = a.shape
    half = kw // 2
    halo = 2 * SUBLANES
    cw = min(d, CONV_LANES)

    def body(prev_ref, cur_ref, next_ref, do_ref, o_ref, ext_ref):
        t = pl.program_id(1)
        _fill_ext(ext_ref, prev_ref, cur_ref, next_ref, t, cfg, halo)
        for p in range(cw // LANES):
            lanes = slice(p * LANES, (p + 1) * LANES)
            dout_t = do_ref[:, lanes]
            rows = [jnp.sum(ext_ref[p, pl.ds(halo - half + k, tr), :] * dout_t, axis=0, keepdims=True) for k in range(kw)]
            _accumulate(o_ref.at[:, lanes], jnp.concatenate(rows, axis=0), t == 0)

    return pl.pallas_call(body, name=name, grid=(d // cw, n_rows // tr),
                          in_specs=_conv_specs(tr, n_rows, halo, cw, cfg) + [pl.BlockSpec((tr, cw), lambda c, t: (_phys_tile(t, cfg), c))],
                          out_specs=pl.BlockSpec((kw, cw), lambda c, t: (0, c)),
                          out_shape=jax.ShapeDtypeStruct((kw, d), F32),
                          scratch_shapes=[pltpu.VMEM((cw // LANES, tr + 2 * halo, LANES), F32)],
                          compiler_params=_params(("arbitrary", "arbitrary")))(a, a, a, dout)


def _sincos_1d(pos, dim):
    quarter = dim // 2
    omega = POS_TEMP ** (-jnp.arange(quarter, dtype=F32) / quarter)
    ang = pos[:, None] * omega[None, :]
    return jnp.concatenate([jnp.sin(ang), jnp.cos(ang)], axis=-1)


def _grid_pos_embed(rows, dim):
    row_idx = jnp.repeat(jnp.arange(rows), GRID_W).astype(F32)
    col_idx = jnp.tile(jnp.arange(GRID_W), rows).astype(F32)
    return jnp.concatenate([_sincos_1d(row_idx, dim // 2), _sincos_1d(col_idx, dim // 2)], axis=-1)


def _pack(arrs, row_multiple=SUBLANES):
    flat = jnp.concatenate([a.reshape(-1).astype(F32) for a in arrs])
    pad = (-flat.shape[0]) % (row_multiple * LANES)
    return jnp.pad(flat, (0, pad)).reshape(-1, LANES)


def _unpack(buf, shapes):
    flat = buf.reshape(-1)
    out, pos = [], 0
    for shp in shapes:
        n = math.prod(shp)
        out.append(flat[pos:pos + n].reshape(shp))
        pos += n
    return out


def _unpack_gathered(buf, shapes):
    flat = buf.reshape(N_DEV, -1)
    out, pos = [], 0
    for shp in shapes:
        n = math.prod(shp)
        part = flat[:, pos:pos + n].reshape((N_DEV,) + tuple(shp))
        out.append(jnp.moveaxis(part, 0, -2).reshape(tuple(shp[:-1]) + (N_DEV * shp[-1],)))
        pos += n
    return out


WEIGHTS = ("c_ctx", "w_ada", "b_ada", "ln_gain", "ln_bias", "s5_lam_re", "s5_lam_im", "s5_log_dt", "s5_b_re", "s5_b_im",
           "s5_c_re", "s5_c_im", "s5_d", "s5_w_glu", "s5_b_glu", "cv_w_pw1", "cv_b_pw1", "cv_w_dw", "cv_b_dw", "cv_ln_g",
           "cv_ln_b", "cv_w_pw2", "cv_b_pw2", "mlp_w1", "mlp_w2")
SHARDED_SMALL = ("ln_gain", "ln_bias", "cv_b_pw1", "cv_w_dw", "cv_b_dw", "cv_ln_g", "cv_ln_b", "cv_b_pw2")
REPLICATED_SMALL = ("s5_lam_re", "s5_lam_im", "s5_log_dt", "s5_b_re", "s5_b_im", "s5_c_re", "s5_c_im", "s5_d", "s5_b_glu")
NATIVE_SMALL = ("s5_lam_re", "s5_lam_im", "s5_b_re", "s5_b_im", "s5_c_re", "s5_c_im")
BIG = ("mlp_w1", "mlp_w2", "s5_w_glu", "cv_w_pw1", "cv_w_pw2")


def _step(a):
    x, c, ctx = a["x"], a["c"], a["ctx"]
    nb, seq, d = x.shape
    lc = ctx.shape[1]
    nl = a["w_ada"].shape[0]
    tr = lc
    tpl = seq // tr
    cfg = {"tr": tr, "tpl": tpl, "nb": nb}
    n_rows = nb * (seq + lc)
    alpha = (2.0 * nl) ** 0.25
    me = 4 * lax.axis_index("x") + 2 * lax.axis_index("y") + lax.axis_index("c")
    n_grp, n_state = a["s5_lam_re"].shape[2:]
    ch = a["s5_b_re"].shape[-1]
    gs = LANES // ch
    ns = d // LANES
    tm = 2 * tr if n_rows % (2 * tr) == 0 else tr
    tm_big = n_rows // 3 if n_rows % (3 * 2 * SUBLANES) == 0 else tm
    f_sub1_s5, f_sub1_cv, f_sub2 = _make_sub1_s5(alpha), _make_sub1_cv(alpha), _make_sub2(alpha)

    def layer_weights(i):
        mixer = [("s5_w_glu", i // 2, 1)] if i % 2 == 0 else [("cv_w_pw1", i // 2, 1), ("cv_w_pw2", i // 2, 0)]
        return mixer + [("mlp_w1", i, 1), ("mlp_w2", i, 0)]

    weights, wgrads, received = {}, {}, {}
    small_all = _exchange("gather_small", [_pack([a[n] for n in SHARDED_SMALL])], ["slot"])[0]
    full = dict(zip(SHARDED_SMALL, _unpack_gathered(small_all, [a[n].shape for n in SHARDED_SMALL])))
    c_all = _exchange("gather_c", [c], ["slot"])[0].reshape(N_DEV * nb, d)
    cond_rows = N_DEV * nb + SUBLANES
    cc = jnp.concatenate([c_all, a["c_ctx"][None], jnp.zeros((SUBLANES - 1, d), F32)], axis=0)

    n_ada = a["w_ada"].shape[2]
    b_loc = lax.dynamic_slice(a["b_ada"], (0, me * n_ada), (nl, n_ada))[:, None, :]
    mod_cols = _ada_fwd(cc, a["w_ada"], b_loc)
    mod_all = _exchange("gather_mod", [mod_cols.reshape(nl * cond_rows, n_ada)], ["slot"])[0].reshape(N_DEV, nl, cond_rows, n_ada)
    mod_mine = jnp.concatenate([lax.dynamic_slice(mod_all, (0, 0, nb * me, 0), (N_DEV, nl, nb, n_ada)),
                                mod_all[:, :, N_DEV * nb:N_DEV * nb + 1]], axis=2)
    mod = jnp.transpose(mod_mine, (1, 2, 0, 3)).reshape(nl, nb + 1, 6, 1, d)

    def seg(i, q):
        return mod[i, :, q]

    zero_seg = jnp.zeros((nb + 1, 1, d), F32)

    def vec(v):
        return v.reshape(1, -1)

    pos = _grid_pos_embed(seq // GRID_W, d)
    def latent_rows(v):
        return jnp.transpose(v.reshape(nb, tpl, tr, d), (1, 0, 2, 3)).reshape(nb * seq, d)

    xc = jnp.concatenate([latent_rows(x), ctx.reshape(nb * lc, d)], axis=0)
    pos_rows = jnp.concatenate([latent_rows(jnp.broadcast_to(pos[None], (nb, seq, d))), jnp.zeros((nb * lc, d), F32)], axis=0)
    x_cur, h_cur = _rowwise("entry", _f_entry, [xc, pos_rows], [seg(0, 0), seg(0, 1)], [], [(d, F32), (d, BF16)], [], [], cfg)
    saved = []
    for i in range(nl):
        j = i // 2
        sv = {"x": x_cur, "h": h_cur}
        sh1, sc1, g1, sh2, sc2, g2 = (seg(i, q) for q in range(6))
        gain0, bias0, gain1, bias1 = (vec(full["ln_gain"][i, 0]), vec(full["ln_bias"][i, 0]),
                                      vec(full["ln_gain"][i, 1]), vec(full["ln_bias"][i, 1]))
        if i % 2 == 0:
            lam_re, lam_im = a["s5_lam_re"][j], a["s5_lam_im"][j]
            log_dt = a["s5_log_dt"][j][:, :, None]
            b_re_t = jnp.transpose(a["s5_b_re"][j], (0, 3, 1, 2))
            b_im_t = jnp.transpose(a["s5_b_im"][j], (0, 3, 1, 2))
            sv["prep_in"] = (lam_re, lam_im, log_dt, b_re_t, b_im_t)
            ab_re, ab_im, bb_re, bb_im = _s5_prep(f"s5_prep{i}", *sv["prep_in"])
            sv["ab"] = (ab_re, ab_im)
            ys = []
            for dirn in range(2):
                def blocks(t):
                    return jnp.transpose(t, (1, 0, 2)).reshape(ns, gs, ch, n_state)
                bmat = jnp.concatenate([_block_diag(blocks(bb_re[dirn])), _block_diag(blocks(bb_im[dirn]))], axis=2).astype(BF16)
                c_re_t = jnp.transpose(a["s5_c_re"][j, dirn], (0, 2, 1)).reshape(ns, gs, n_state, ch)
                c_im_t = jnp.transpose(a["s5_c_im"][j, dirn], (0, 2, 1)).reshape(ns, gs, n_state, ch)
                cmat = jnp.concatenate([_block_diag(c_re_t), -_block_diag(c_im_t)], axis=1).astype(BF16)
                tab = _s5_tables(ab_re[dirn], ab_im[dirn], dirn == 1, False, ns, tr // SUBLANES)
                group = layer_weights(i + dirn)
                cps = _Copies([a[n][idx].astype(BF16) for n, idx, _ in group], [axis for _, _, axis in group])
                h_states, y_dir, *gathered = _s5_fwd(f"s5_fwd{i}_{dirn}", h_cur, bmat, cmat, tab, dirn, cfg, cps)
                weights.update({(n, idx): w[None] for (n, idx, _), w in zip(group, gathered)})
                sv[f"mats{dirn}"] = (jnp.transpose(bmat, (0, 2, 1)), jnp.transpose(cmat, (0, 2, 1)))
                sv[f"states{dirn}"] = h_states
                ys.append(y_dir)
            sv["y"] = ys
            dsk = vec(a["s5_d"][j])
            z = _rowwise(f"gelu{i}", _f_gelu, [x_cur, ys[0], ys[1]], [sh1, sc1], [dsk], [(d, BF16)], [], [], cfg)[0]
            zz = _mm_nn(f"glu{i}", z, weights["s5_w_glu", j], 0, tm_big, min(2 * d, 512))[0]
            bglu = vec(a["s5_b_glu"][j])
            x1, h2 = _rowwise(f"sub1_{i}", f_sub1_s5, [x_cur, zz], [g1, sh2, sc2], [bglu, gain0, bias0],
                              [(d, F32), (d, BF16)], [], [], cfg)
            sv.update(z=z, zz=zz)
        else:
            zz = _mm_nn(f"pw1_{i}", h_cur, weights["cv_w_pw1", j], 0, tm_big, min(2 * d, 512))[0]
            bpw1 = vec(full["cv_b_pw1"][j])
            act = _rowwise(f"cvglu{i}", _f_cvglu, [zz], [], [bpw1], [(d, F32)], [], [], cfg)[0]
            w_dw = full["cv_w_dw"][j]
            cv = _dwconv(f"dwconv{i}", act, w_dw, cfg)
            bdw, lng, lnb = vec(full["cv_b_dw"][j]), vec(full["cv_ln_g"][j]), vec(full["cv_ln_b"][j])
            s_act = _rowwise(f"cvln{i}", _f_cvln, [cv], [], [bdw, lng, lnb], [(d, BF16)], [], [], cfg)[0]
            mm = _mm_nn(f"pw2_{i}", s_act, weights["cv_w_pw2", j], 0, tm_big, d)[0]
            bpw2 = vec(full["cv_b_pw2"][j])
            x1, h2 = _rowwise(f"sub1_{i}", f_sub1_cv, [x_cur, mm], [g1, sh2, sc2], [bpw2, gain0, bias0],
                              [(d, F32), (d, BF16)], [], [], cfg)
            sv.update(zz=zz, act=act, cv=cv, s_act=s_act, mm=mm, w_dw=w_dw)
        dff = weights["mlp_w1", i].shape[2]
        p_act, r_act = _mm_nn(f"mlp1_{i}", h2, weights["mlp_w1", i], 0, tm_big, min(dff, 1024), (BF16, BF16),
                              lambda acc: (jnp.square(jnp.maximum(acc, 0.0)), jnp.maximum(acc, 0.0)))
        m_out = _mm_nn(f"mlp2_{i}", p_act, weights["mlp_w2", i], 0, tm, d)[0]
        shn, scn = (seg(i + 1, 0), seg(i + 1, 1)) if i + 1 < nl else (zero_seg, zero_seg)
        x2, hn = _rowwise(f"sub2_{i}", f_sub2, [x1, m_out], [g2, shn, scn], [gain1, bias1], [(d, F32), (d, BF16)], [], [], cfg)
        sv.update(x1=x1, h2=h2, p=p_act, r=r_act, m=m_out, shn=shn, scn=scn)
        saved.append(sv)
        x_cur, h_cur = x2, hn

    target = jnp.concatenate([latent_rows(a["loss_target"]), jnp.zeros((nb * lc, d), F32)], axis=0)
    mask = jnp.concatenate([jnp.ones((nb, 1, d), F32), jnp.zeros((1, 1, d), F32)], axis=0)

    def f_loss(xf, tgt, msk):
        err = (xf - tgt) * msk
        part = 0.5 * jnp.sum(jnp.square(err), axis=(0, 1), keepdims=True) / d
        return err / d, jnp.broadcast_to(part, (1, LANES))

    dx_final, loss_part = _rowwise("loss", f_loss, [x_cur, target], [mask], [], [(d, F32)], [], [LANES], cfg)
    loss = lax.psum(loss_part[0, 0], ("x", "y", "c"))

    grads = {n: [None] * a[n].shape[0] for n in WEIGHTS if n not in ("c_ctx", "w_ada", "b_ada")}
    dmod = [[None] * 6 for _ in range(nl)]

    def add_mod(i, q, val):
        dmod[i][q] = val if dmod[i][q] is None else dmod[i][q] + val

    dx_parts, dh_parts = [dx_final], []
    for i in reversed(range(nl)):
        j = i // 2
        sv = saved[i]
        sh1, sc1, g1, sh2, sc2, g2 = (seg(i, q) for q in range(6))
        gain0, bias0, gain1, bias1 = (vec(full["ln_gain"][i, 0]), vec(full["ln_bias"][i, 0]),
                                      vec(full["ln_gain"][i, 1]), vec(full["ln_bias"][i, 1]))
        bwd = _vjp_fn(f_sub2, 2, (len(dx_parts), len(dh_parts)), (0, 1, 2, 3, 4, 5, 6))
        dx1, dm, dg2, dshn, dscn, dgain1, dbias1 = _rowwise(
            f"sub2_bwd{i}", bwd, [sv["x1"], sv["m"]] + dx_parts + dh_parts, [g2, sv["shn"], sv["scn"]], [gain1, bias1],
            [(d, F32), (d, BF16)], [d, d, d], [d, d], cfg)
        add_mod(i, 5, dg2)
        if i + 1 < nl:
            add_mod(i + 1, 0, dshn)
            add_mod(i + 1, 1, dscn)
        da = _mm_nt(f"mlp2_dgrad{i}", dm, weights["mlp_w2", i], 0, tm_big, min(dff, 1024), [sv["r"]], BF16,
                    lambda acc, r: (acc * 2.0 * r,))
        wgrads["mlp_w2", i] = _mm_wgrad_rows(f"mlp2_wgrad{i}", sv["p"], dm, tm_big)
        wgrads["mlp_w1", i] = _mm_wgrad_cols(f"mlp1_wgrad{i}", sv["h2"], da, tm_big)
        dh2 = _mm_nt(f"mlp1_dgrad{i}", da, weights["mlp_w1", i], 0, tm, d)
        if i % 2 == 0:
            bglu = vec(a["s5_b_glu"][j])
            bwd = _vjp_fn(f_sub1_s5, 2, (1, 1), (0, 1, 2, 3, 4, 5, 6, 7))
            dxa, dzz, dg1, dsh2, dsc2, dbglu, dgain0, dbias0 = _rowwise(
                f"sub1_bwd{i}", bwd, [sv["x"], sv["zz"], dx1, dh2], [g1, sh2, sc2], [bglu, gain0, bias0],
                [(d, F32), (2 * d, BF16)], [d, d, d], [2 * d, d, d], cfg)
            grads["s5_b_glu"][j] = dbglu[0]
            wgrads["s5_w_glu", j] = _mm_wgrad_cols(f"glu_wgrad{i}", sv["z"], dzz, tm_big)
            dz = _mm_nt(f"glu_dgrad{i}", dzz, weights["s5_w_glu", j], 0, tm, d)
            dsk = vec(a["s5_d"][j])
            bwd = _vjp_fn(_f_gelu, 3, (1,), (0, 1, 3, 4, 5))
            dxb, dy, dsh1, dsc1, ddsk = _rowwise(f"gelu_bwd{i}", bwd, [sv["x"], sv["y"][0], sv["y"][1], dz], [sh1, sc1], [dsk],
                                                 [(d, F32), (d, BF16)], [d, d], [d], cfg)
            grads["s5_d"][j] = ddsk[0]
            add_mod(i, 0, dsh1)
            add_mod(i, 1, dsc1)
            ab_re, ab_im = sv["ab"]
            dus, d_ab_re, d_ab_im, d_bb_re, d_bb_im, d_c_re, d_c_im = [], [], [], [], [], [], []
            for dirn in range(2):
                bmat_t, cmat_t = sv[f"mats{dirn}"]
                tab = _s5_tables(ab_re[dirn], ab_im[dirn], dirn == 0, True, ns, tr // SUBLANES)
                group = layer_weights(i + 1 - dirn)
                cps = _Copies([wgrads[n, idx] for n, idx, _ in group], ["scatter"] * len(group))
                du, d_b, d_c, d_a, *parts = _s5_bwd(f"s5_bwd{i}_{dirn}", dy, sv[f"states{dirn}"], sv["h"], cmat_t, bmat_t, tab,
                                                    dirn, cfg, cps)
                received.update({(n, idx): p for (n, idx, _), p in zip(group, parts)})
                dus.append(du)
                sw = d_a.shape[2] // 2
                d_a = jnp.sum(d_a, axis=1)
                d_ab_re.append(d_a[:, :sw].reshape(n_grp, n_state))
                d_ab_im.append(d_a[:, sw:].reshape(n_grp, n_state))

                def unblock_b(t):
                    return jnp.transpose(_diag_blocks(t, gs).reshape(n_grp, ch, n_state), (1, 0, 2))

                def unblock_c(t):
                    return jnp.transpose(_diag_blocks(t, gs).reshape(n_grp, n_state, ch), (0, 2, 1))
                d_bb_re.append(unblock_b(d_b[:, :, :sw]))
                d_bb_im.append(unblock_b(d_b[:, :, sw:]))
                d_c_re.append(unblock_c(d_c[:, :sw]))
                d_c_im.append(-unblock_c(d_c[:, sw:]))
            g_lre, g_lim, g_ldt, g_bre, g_bim = _s5_prep_bwd(
                f"s5_prep_bwd{i}", *sv["prep_in"], (jnp.stack(d_ab_re), jnp.stack(d_ab_im), jnp.stack(d_bb_re), jnp.stack(d_bb_im)))
            grads["s5_lam_re"][j], grads["s5_lam_im"][j], grads["s5_log_dt"][j] = g_lre, g_lim, g_ldt[:, :, 0]
            grads["s5_b_re"][j] = jnp.transpose(g_bre, (0, 2, 3, 1))
            grads["s5_b_im"][j] = jnp.transpose(g_bim, (0, 2, 3, 1))
            grads["s5_c_re"][j], grads["s5_c_im"][j] = jnp.stack(d_c_re), jnp.stack(d_c_im)
            dx_parts, dh_parts = [dxa, dxb], dus
        else:
            bpw2 = vec(full["cv_b_pw2"][j])
            bwd = _vjp_fn(f_sub1_cv, 2, (1, 1), (0, 1, 2, 3, 4, 5, 6, 7))
            dxa, dmm, dg1, dsh2, dsc2, dbpw2, dgain0, dbias0 = _rowwise(
                f"sub1_bwd{i}", bwd, [sv["x"], sv["mm"], dx1, dh2], [g1, sh2, sc2], [bpw2, gain0, bias0],
                [(d, F32), (d, BF16)], [d, d, d], [d, d, d], cfg)
            grads["cv_b_pw2"][j] = dbpw2[0]
            wgrads["cv_w_pw2", j] = _mm_wgrad_rows(f"pw2_wgrad{i}", sv["s_act"], dmm, tm_big)
            ds = _mm_nt(f"pw2_dgrad{i}", dmm, weights["cv_w_pw2", j], 0, tm_big, d)
            bdw, lng, lnb = vec(full["cv_b_dw"][j]), vec(full["cv_ln_g"][j]), vec(full["cv_ln_b"][j])
            bwd = _vjp_fn(_f_cvln, 1, (1,), (0, 1, 2, 3))
            dcv, dbdw, dlng, dlnb = _rowwise(f"cvln_bwd{i}", bwd, [sv["cv"], ds], [], [bdw, lng, lnb], [(d, F32)], [], [d, d, d], cfg)
            grads["cv_b_dw"][j], grads["cv_ln_g"][j], grads["cv_ln_b"][j] = dbdw[0], dlng[0], dlnb[0]
            dact = _dwconv(f"dwconv_bwd{i}", dcv, sv["w_dw"][::-1], cfg)
            grads["cv_w_dw"][j] = _dwconv_wgrad(f"dwconv_wgrad{i}", sv["act"], dcv, sv["w_dw"].shape[0], cfg)
            bpw1 = vec(full["cv_b_pw1"][j])
            bwd = _vjp_fn(_f_cvglu, 1, (1,), (0, 1))
            dzz, dbpw1 = _rowwise(f"cvglu_bwd{i}", bwd, [sv["zz"], dact], [], [bpw1], [(2 * d, BF16)], [], [2 * d], cfg)
            grads["cv_b_pw1"][j] = dbpw1[0]
            wgrads["cv_w_pw1", j] = _mm_wgrad_cols(f"pw1_wgrad{i}", sv["h"], dzz, tm_big)
            dh = _mm_nt(f"pw1_dgrad{i}", dzz, weights["cv_w_pw1", j], 0, tm, d)
            dx_parts, dh_parts = [dxa], [dh]
        grads["ln_gain"][i] = jnp.stack([dgain0[0], dgain1[0]])
        grads["ln_bias"][i] = jnp.stack([dbias0[0], dbias1[0]])
        add_mod(i, 2, dg1)
        add_mod(i, 3, dsh2)
        add_mod(i, 4, dsc2)
    bwd = _vjp_fn(_f_entry, 2, (len(dx_parts), len(dh_parts)), (0, 2, 3))
    dxc, dsh1, dsc1 = _rowwise("entry_bwd", bwd, [xc, pos_rows] + dx_parts + dh_parts, [seg(0, 0), seg(0, 1)], [],
                               [(d, F32)], [d, d], [], cfg)
    add_mod(0, 0, dsh1)
    add_mod(0, 1, dsc1)
    grad_x = jnp.transpose(dxc[:nb * seq].reshape(tpl, nb, tr, d), (1, 0, 2, 3)).reshape(nb, seq, d)

    dmod_loc = jnp.stack([jnp.concatenate([q[:, 0] for q in dmod[i]], axis=1) for i in range(nl)])
    dmod_all = _exchange("gather_dmod", [dmod_loc.reshape(nl * (nb + 1), 6 * d)], ["slot"])[0].reshape(N_DEV, nl, nb + 1, 6 * d)
    mine = lax.dynamic_slice(dmod_all, (0, 0, 0, me * n_ada), (N_DEV, nl, nb + 1, n_ada))
    dmod_rows = jnp.transpose(mine[:, :, :nb], (1, 0, 2, 3)).reshape(nl, N_DEV * nb, n_ada)
    dmod_rows = jnp.concatenate([dmod_rows, jnp.zeros((nl, SUBLANES, n_ada), F32)], axis=1)
    g_w_ada, _, dcond = _ada_bwd(cc, a["w_ada"], dmod_rows, mine[:, :, nb:])
    g_b_ada = _sum_lead("b_ada_sum", jnp.transpose(dmod_all, (0, 2, 1, 3)).reshape(N_DEV * (nb + 1), nl, 6 * d), nl)
    dcond_all = _exchange("gather_dcond", [dcond[N_DEV * nb:N_DEV * nb + 1]], ["slot"])[0]
    g_c_ctx = _cctx_grad(dcond_all, a["c_ctx"][None])[0]

    small_names = SHARDED_SMALL + REPLICATED_SMALL
    small_full = [jnp.stack(grads[n]) for n in small_names]
    small_packed = _pack(small_full, N_DEV * SUBLANES)
    small_parts = _exchange("scatter_small_grads", [small_packed.reshape(N_DEV, -1, LANES)], ["scatter"])[0]
    small_part = _sum_lead("small_grad_sum", small_parts, _row_tile(small_parts.shape[1], 512))
    small_sum = _exchange("gather_small_sum", [small_part], ["slot"])[0]
    small_g = dict(zip(small_names, _unpack(small_sum, [g.shape for g in small_full])))
    for n in SHARDED_SMALL:
        width = a[n].shape[-1]
        start = (0,) * (small_g[n].ndim - 1) + (me * width,)
        small_g[n] = lax.dynamic_slice(small_g[n], start, a[n].shape)
    small_g["c_ctx"], small_g["b_ada"] = g_c_ctx, g_b_ada

    out = {}

    def update(n, parts):
        shp = a[n].shape
        cols = parts.shape[-1]
        rows = parts.shape[1]
        res = _adamw(f"adamw_{n}", parts, a[n].reshape(rows, cols), a["m_" + n].reshape(rows, cols), a["v_" + n].reshape(rows, cols),
                     _row_tile(rows, max(SUBLANES, 131072 // cols)))
        out[n] = [r.reshape(shp) for r in res]

    for n in BIG:
        rows, cols = a[n].shape[1:]
        bufs = [lax.empty(a[n].shape, F32) for _ in range(4)]
        for idx in range(a[n].shape[0]):
            bufs = _adamw_layer(f"adamw_{n}{idx}", received[n, idx], a[n], a["m_" + n], a["v_" + n], bufs, idx,
                                _row_tile(rows, max(SUBLANES, 131072 // cols)))
        out[n] = bufs
    update("w_ada", g_w_ada.reshape(1, -1, n_ada))
    for n in NATIVE_SMALL:
        out[n] = [small_g[n], *_adamw_native(f"adamw_{n}", small_g[n], a[n], a["m_" + n], a["v_" + n])]
    small_all_names = ("c_ctx", "b_ada") + tuple(n for n in small_names if n not in NATIVE_SMALL)
    packed = [_pack([src[n] for n in small_all_names]) for src in
              (small_g, a, {n: a["m_" + n] for n in small_all_names}, {n: a["v_" + n] for n in small_all_names})]
    res = _adamw("adamw_small", packed[0][None], packed[1], packed[2], packed[3], _row_tile(packed[0].shape[0], 512))
    shapes = [a[n].shape for n in small_all_names]
    for n, vals in zip(small_all_names, zip(*[_unpack(r, shapes) for r in res])):
        out[n] = list(vals)
    return (loss, grad_x, *[out[n][0] for n in WEIGHTS], *[out[n][1] for n in WEIGHTS],
            *[out[n][2] for n in WEIGHTS], *[out[n][3] for n in WEIGHTS])


def kernel(x, c, ctx, c_ctx, w_ada, b_ada, ln_gain, ln_bias, s5_lam_re, s5_lam_im, s5_log_dt, s5_b_re, s5_b_im, s5_c_re, s5_c_im, s5_d, s5_w_glu, s5_b_glu, cv_w_pw1, cv_b_pw1, cv_w_dw, cv_b_dw, cv_ln_g, cv_ln_b, cv_w_pw2, cv_b_pw2, mlp_w1, mlp_w2, loss_target, m_c_ctx, m_w_ada, m_b_ada, m_ln_gain, m_ln_bias, m_s5_lam_re, m_s5_lam_im, m_s5_log_dt, m_s5_b_re, m_s5_b_im, m_s5_c_re, m_s5_c_im, m_s5_d, m_s5_w_glu, m_s5_b_glu, m_cv_w_pw1, m_cv_b_pw1, m_cv_w_dw, m_cv_b_dw, m_cv_ln_g, m_cv_ln_b, m_cv_w_pw2, m_cv_b_pw2, m_mlp_w1, m_mlp_w2, v_c_ctx, v_w_ada, v_b_ada, v_ln_gain, v_ln_bias, v_s5_lam_re, v_s5_lam_im, v_s5_log_dt, v_s5_b_re, v_s5_b_im, v_s5_c_re, v_s5_c_im, v_s5_d, v_s5_w_glu, v_s5_b_glu, v_cv_w_pw1, v_cv_b_pw1, v_cv_w_dw, v_cv_b_dw, v_cv_ln_g, v_cv_ln_b, v_cv_w_pw2, v_cv_b_pw2, v_mlp_w1, v_mlp_w2):
    return _step(dict(locals()))
```

```python
import functools
import math

import jax
import jax.numpy as jnp
from jax import lax
from jax.experimental import pallas as pl
from jax.experimental.pallas import tpu as pltpu

F32 = jnp.float32
BF16 = jnp.bfloat16
N_DEV = 8
LANES = 128
SUBLANES = 8
VMEM_LIMIT = 56 * 1024 * 1024
GRID_W = 64
POS_TEMP = 10000.0
LN_EPS = 1e-5
LAMBDA_RE_MAX = -1e-4
ADAM_LR, ADAM_B1, ADAM_B2, ADAM_EPS, ADAM_WD, ADAM_STEP = 0.001, 0.9, 0.999, 1e-08, 0.01, 10
MESH = pl.DeviceIdType.MESH


def _params(sem):
    return pltpu.CompilerParams(dimension_semantics=sem, vmem_limit_bytes=VMEM_LIMIT)


def _accumulate(ref, val, first):
    @pl.when(first)
    def _():
        ref[...] = val

    @pl.when(jnp.logical_not(first))
    def _():
        ref[...] += val


def _rowwise(name, fn, rows, segs, vecs, row_outs, seg_accs, vec_accs, cfg):
    tr, tpl, nb = cfg["tr"], cfg["tpl"], cfg["nb"]
    n_rows = rows[0].shape[0]
    nt = n_rows // tr
    nr, ns, nv = len(rows), len(segs), len(vecs)
    nro, nsa = len(row_outs), len(seg_accs)

    def seg_of(t):
        return jnp.minimum(t // tpl, nb)

    def body(*refs):
        t = pl.program_id(0)
        ins, outs = refs[:nr + ns + nv], refs[nr + ns + nv:]
        vals = [r[...] for r in ins[:nr]] + [r[0] for r in ins[nr:nr + ns]] + [r[...] for r in ins[nr + ns:]]
        res = fn(*vals)
        for o, v in zip(outs[:nro], res[:nro]):
            o[...] = v.astype(o.dtype)
        first_seg = jnp.logical_or(t == 0, seg_of(t) != seg_of(jnp.maximum(t - 1, 0)))
        for o, v in zip(outs[nro:nro + nsa], res[nro:nro + nsa]):
            _accumulate(o.at[0], v, first_seg)
        for o, v in zip(outs[nro + nsa:], res[nro + nsa:]):
            _accumulate(o, v, t == 0)

    in_specs = ([pl.BlockSpec((tr, a.shape[1]), lambda t: (_phys_tile(t, cfg), 0)) for a in rows]
                + [pl.BlockSpec((1, 1, a.shape[2]), lambda t: (seg_of(t), 0, 0)) for a in segs]
                + [pl.BlockSpec((1, a.shape[1]), lambda t: (0, 0)) for a in vecs])
    out_specs = ([pl.BlockSpec((tr, c), lambda t: (_phys_tile(t, cfg), 0)) for c, _ in row_outs]
                 + [pl.BlockSpec((1, 1, c), lambda t: (seg_of(t), 0, 0)) for c in seg_accs]
                 + [pl.BlockSpec((1, c), lambda t: (0, 0)) for c in vec_accs])
    out_shape = ([jax.ShapeDtypeStruct((n_rows, c), dt) for c, dt in row_outs]
                 + [jax.ShapeDtypeStruct((nb + 1, 1, c), F32) for c in seg_accs]
                 + [jax.ShapeDtypeStruct((1, c), F32) for c in vec_accs])
    return pl.pallas_call(body, name=name, grid=(nt,), in_specs=in_specs, out_specs=out_specs,
                          out_shape=out_shape, compiler_params=_params(("arbitrary",)))(*rows, *segs, *vecs)


def _vjp_fn(fn, n_row, cot_groups, want):
    n_cot = sum(cot_groups)

    def bwd(*args):
        primals = [a.astype(F32) for a in args[:n_row] + args[n_row + n_cot:]]
        outs, vjp = jax.vjp(fn, *primals)
        cots, pos = [], n_row
        for n, o in zip(cot_groups, outs):
            cot = jnp.zeros_like(o)
            for part in args[pos:pos + n]:
                cot = cot + part.astype(F32)
            cots.append(cot)
            pos += n
        grads = vjp(tuple(cots))
        return tuple(grads[i] for i in want)
    return bwd


def _ln(r, g, b):
    mu = jnp.mean(r, axis=-1, keepdims=True)
    var = jnp.mean(jnp.square(r - mu), axis=-1, keepdims=True)
    return (r - mu) * lax.rsqrt(var + LN_EPS) * g + b


def _glu(zz, bias):
    d = zz.shape[1] // 2
    return (zz[:, :d] + bias[:, :d]) * jax.nn.sigmoid(zz[:, d:] + bias[:, d:])


def _f_entry(xc, pos, sh, sc):
    x0 = xc + pos
    return x0, x0 * (1 + sc) + sh


def _f_gelu(x, y0, y1, sh, sc, dsk):
    u = x * (1 + sc) + sh
    y = dsk * u + y0 + y1
    return (0.5 * y * (1.0 + lax.erf(y * (2.0 ** -0.5))),)


def _make_sub1_s5(alpha):
    def f(x, zz, g1, sh2, sc2, bglu, gain, bias):
        x1 = _ln(alpha * x + g1 * _glu(zz, bglu), gain, bias)
        return x1, x1 * (1 + sc2) + sh2
    return f


def _make_sub1_cv(alpha):
    def f(x, mm, g1, sh2, sc2, bpw2, gain, bias):
        x1 = _ln(alpha * x + g1 * (mm + bpw2), gain, bias)
        return x1, x1 * (1 + sc2) + sh2
    return f


def _make_sub2(alpha):
    def f(x1, m, g2, shn, scn, gain, bias):
        x2 = _ln(alpha * x1 + g2 * m, gain, bias)
        return x2, x2 * (1 + scn) + shn
    return f


def _f_cvglu(zz, bpw1):
    return (_glu(zz, bpw1),)


def _f_cvln(cv, bdw, lng, lnb):
    return (jax.nn.silu(_ln(cv + bdw, lng, lnb)),)


def _matmul(name, a, b, extras, grid, a_spec, b_spec, extra_specs, o_specs, out_shape, dims, red_axis, epi, sem):
    n_extra = len(extras)
    n_out = len(out_shape)
    acc_shape = o_specs[0].block_shape
    acc_shape = tuple(s for s in acc_shape if s is not None)

    def body(*refs):
        a_ref, b_ref = refs[0], refs[1]
        ex = refs[2:2 + n_extra]
        outs = refs[2 + n_extra:2 + n_extra + n_out]
        prod = lax.dot_general(a_ref[...], b_ref[...], dims, preferred_element_type=F32)

        def finish(acc):
            res = epi(acc, *[e[...] for e in ex]) if epi is not None else (acc,)
            for o, v in zip(outs, res):
                o[...] = v.astype(o.dtype)

        if red_axis is None:
            finish(prod)
        else:
            acc_ref = refs[-1]
            k = pl.program_id(red_axis)
            nk = pl.num_programs(red_axis)

            @pl.when(k == 0)
            def _():
                acc_ref[...] = prod

            @pl.when(k > 0)
            def _():
                acc_ref[...] += prod

            @pl.when(k == nk - 1)
            def _():
                finish(acc_ref[...])

    scratch = [] if red_axis is None else [pltpu.VMEM(acc_shape, F32)]
    res = pl.pallas_call(body, name=name, grid=grid, in_specs=[a_spec, b_spec] + list(extra_specs),
                         out_specs=list(o_specs), out_shape=list(out_shape), scratch_shapes=scratch,
                         compiler_params=_params(sem))(a, b, *extras)
    return res


NN = (((1,), (0,)), ((), ()))
NT = (((1,), (1,)), ((), ()))
TN = (((0,), (0,)), ((), ()))


def _mm_nn(name, a, w3, layer, tm, tn, out_dtypes=(F32,), epi=None):
    m, k = a.shape
    n = w3.shape[2]
    return _matmul(name, a, w3, (), (n // tn, m // tm),
                   pl.BlockSpec((tm, k), lambda j, i: (i, 0)), pl.BlockSpec((None, k, tn), lambda j, i: (layer, 0, j)), (),
                   [pl.BlockSpec((tm, tn), lambda j, i: (i, j)) for _ in out_dtypes],
                   [jax.ShapeDtypeStruct((m, n), dt) for dt in out_dtypes], NN, None, epi, ("arbitrary", "arbitrary"))


def _mm_nt(name, dy, w3, layer, tm, tkw, extras=(), out_dtype=F32, epi=None):
    m, n = dy.shape
    kw = w3.shape[1]
    return _matmul(name, dy, w3, tuple(extras), (kw // tkw, m // tm),
                   pl.BlockSpec((tm, n), lambda j, i: (i, 0)), pl.BlockSpec((None, tkw, n), lambda j, i: (layer, j, 0)),
                   [pl.BlockSpec((tm, tkw), lambda j, i: (i, j)) for _ in extras],
                   [pl.BlockSpec((tm, tkw), lambda j, i: (i, j))], [jax.ShapeDtypeStruct((m, kw), out_dtype)], NT, None, epi,
                   ("arbitrary", "arbitrary"))[0]


def _mm_wgrad_cols(name, a, dy, tm):
    m, k = a.shape
    n = dy.shape[1] // N_DEV
    return _matmul(name, a, dy, (), (N_DEV, m // tm),
                   pl.BlockSpec((tm, k), lambda j, i: (i, 0)), pl.BlockSpec((tm, n), lambda j, i: (i, j)), (),
                   [pl.BlockSpec((None, k, n), lambda j, i: (j, 0, 0))], [jax.ShapeDtypeStruct((N_DEV, k, n), BF16)],
                   TN, 1, None, ("arbitrary", "arbitrary"))[0]


def _mm_wgrad_rows(name, a, dy, tm):
    m = a.shape[0]
    r = a.shape[1] // N_DEV
    n = dy.shape[1]
    return _matmul(name, a, dy, (), (N_DEV, m // tm),
                   pl.BlockSpec((tm, r), lambda j, i: (i, j)), pl.BlockSpec((tm, n), lambda j, i: (i, 0)), (),
                   [pl.BlockSpec((None, r, n), lambda j, i: (j, 0, 0))], [jax.ShapeDtypeStruct((N_DEV, r, n), BF16)],
                   TN, 1, None, ("arbitrary", "arbitrary"))[0]


class _Copies:
    def __init__(self, arrays, kinds):
        self.arrays, self.kinds, self.n = list(arrays), list(kinds), len(arrays)
        any_spec = pl.BlockSpec(memory_space=pl.ANY)
        self.in_specs = [any_spec] * self.n
        self.out_specs = [any_spec] * self.n
        self.out_shape = [jax.ShapeDtypeStruct(self._result(a, kind), a.dtype) for a, kind in zip(arrays, kinds)]
        self.scratch = [pltpu.SemaphoreType.DMA((self.n, N_DEV - 1)), pltpu.SemaphoreType.DMA((self.n, N_DEV - 1)),
                        pltpu.SemaphoreType.DMA((self.n,))] if self.n else []

    @staticmethod
    def _result(a, kind):
        if kind == "slot":
            return (N_DEV,) + a.shape
        if kind == "scatter":
            return a.shape
        return a.shape[:kind] + (N_DEV * a.shape[kind],) + a.shape[kind + 1:]

    def descriptors(self, ins, outs, sems):
        send_sems, recv_sems, local_sems = sems
        x, y, c = lax.axis_index("x"), lax.axis_index("y"), lax.axis_index("c")
        me = 4 * x + 2 * y + c
        first, relay, finish = [], [], []

        def remote(i, k, src, dst, to):
            return pltpu.make_async_remote_copy(src_ref=src, dst_ref=dst, send_sem=send_sems.at[i, k], recv_sem=recv_sems.at[i, k],
                                                device_id=to, device_id_type=MESH)

        for i, kind in enumerate(self.kinds):
            if kind in ("slot", "scatter"):
                scatter = kind == "scatter"
                local = pltpu.make_async_copy(ins[i].at[me] if scatter else ins[i], outs[i].at[me], local_sems.at[i])
                first.append(local)
                finish.append((local, "all"))
                for k in range(1, N_DEV):
                    px = 1 - x if k & 4 else x
                    py = 1 - y if k & 2 else y
                    pc = 1 - c if k & 1 else c
                    cp = remote(i, k - 1, ins[i].at[4 * px + 2 * py + pc] if scatter else ins[i], outs[i].at[me], (px, py, pc))
                    first.append(cp)
                    finish.append((cp, "all"))
                continue
            size = ins[i].shape[kind]

            def block(px, py, pc):
                return outs[i].at[(slice(None),) * kind + (pl.ds(pl.multiple_of((4 * px + 2 * py + pc) * size, size), size),)]

            local = pltpu.make_async_copy(ins[i], block(x, y, c), local_sems.at[i])
            sibling = remote(i, 0, ins[i], block(x, y, c), (x, y, 1 - c))
            first += [local, sibling]
            finish += [(local, "all"), (sibling, "all")]
            for j, (qx, qy) in enumerate([(1 - x, y), (x, 1 - y), (1 - x, 1 - y)]):
                out = remote(i, 1 + j, ins[i], block(x, y, c), (qx, qy, c))
                onward = remote(i, 4 + j, block(qx, qy, c), block(qx, qy, c), (x, y, 1 - c))
                first.append(out)
                relay.append((out, onward))
                finish += [(out, "send"), (onward, "all")]
        return first, relay, finish


def _guarded(when, fn):
    if when is None:
        fn()
    else:
        pl.when(when)(fn)


def _start_all(plan, when=None):
    def run():
        for cp in plan[0]:
            cp.start()
    _guarded(when, run)


def _relay_all(plan, when=None):
    def run():
        for arrived, onward in plan[1]:
            arrived.wait_recv()
            onward.start()
    if plan[1]:
        _guarded(when, run)


def _wait_all(plan, when=None):
    def run():
        for cp, left in plan[2]:
            if left == "send":
                cp.wait_send()
            else:
                cp.wait()
    _guarded(when, run)


def _exchange(name, arrays, kinds):
    cps = _Copies(arrays, kinds)
    n = cps.n

    def body(*refs):
        plan = cps.descriptors(refs[:n], refs[n:2 * n], refs[2 * n:])
        _start_all(plan)
        _relay_all(plan)
        _wait_all(plan)

    return pl.pallas_call(body, name=name, in_specs=cps.in_specs, out_specs=cps.out_specs, out_shape=cps.out_shape,
                          scratch_shapes=cps.scratch)(*arrays)


def _carried(cps, n_in, n_out, n_scratch, refs):
    if cps is None:
        return ([], [], []), refs
    n = cps.n
    ins = refs[n_in:n_in + n]
    outs = refs[n_in + n + n_out:n_in + n + n_out + n]
    sems = refs[n_in + n + n_out + n + n_scratch:]
    own = refs[:n_in] + refs[n_in + n:n_in + n + n_out] + refs[n_in + n + n_out + n:n_in + n + n_out + n + n_scratch]
    return cps.descriptors(ins, outs, sems), own


def _sum_lead(name, parts, tr):
    npart, r, c = parts.shape

    def body(p_ref, o_ref):
        acc = p_ref[0].astype(F32)
        for p in range(1, npart):
            acc = acc + p_ref[p].astype(F32)
        o_ref[...] = acc

    return pl.pallas_call(body, name=name, grid=(r // tr,), in_specs=[pl.BlockSpec((npart, tr, c), lambda i: (0, i, 0))],
                          out_specs=pl.BlockSpec((tr, c), lambda i: (i, 0)), out_shape=jax.ShapeDtypeStruct((r, c), F32),
                          compiler_params=_params(("arbitrary",)))(parts)


def _adamw_math(g, w, m, v):
    m2 = ADAM_B1 * m + (1.0 - ADAM_B1) * g
    v2 = ADAM_B2 * v + (1.0 - ADAM_B2) * jnp.square(g)
    m_hat = m2 / (1.0 - ADAM_B1 ** ADAM_STEP)
    v_hat = v2 / (1.0 - ADAM_B2 ** ADAM_STEP)
    return -ADAM_LR * (m_hat / (jnp.sqrt(v_hat) + ADAM_EPS) + ADAM_WD * w), m2, v2


def _adamw_body(npart):
    def body(p_ref, w_ref, m_ref, v_ref, *rest):
        g_out, d_out, m_out, v_out = rest[-4:]
        g = p_ref[0].astype(F32)
        for p in range(1, npart):
            g = g + p_ref[p].astype(F32)
        g_out[...] = g
        d_out[...], m_out[...], v_out[...] = _adamw_math(g, w_ref[...], m_ref[...], v_ref[...])
    return body


def _adamw_native(name, g, w, m, v):
    rest = w.shape[2:]
    spec = pl.BlockSpec((None, None) + rest, lambda i, j: (i, j) + (0,) * len(rest))

    def body(g_ref, w_ref, m_ref, v_ref, d_out, m_out, v_out):
        d_out[...], m_out[...], v_out[...] = _adamw_math(g_ref[...], w_ref[...], m_ref[...], v_ref[...])

    return pl.pallas_call(body, name=name, grid=w.shape[:2], in_specs=[spec] * 4, out_specs=[spec] * 3,
                          out_shape=[jax.ShapeDtypeStruct(w.shape, F32)] * 3,
                          compiler_params=_params(("arbitrary", "arbitrary")))(g, w, m, v)


def _adamw(name, parts, w, m, v, tr):
    npart, r, c = parts.shape
    row = pl.BlockSpec((tr, c), lambda i: (i, 0))
    return pl.pallas_call(_adamw_body(npart), name=name, grid=(r // tr,),
                          in_specs=[pl.BlockSpec((npart, tr, c), lambda i: (0, i, 0)), row, row, row],
                          out_specs=[row] * 4, out_shape=[jax.ShapeDtypeStruct((r, c), F32)] * 4,
                          compiler_params=_params(("arbitrary",)))(parts, w, m, v)


def _adamw_layer(name, parts, w3, m3, v3, bufs, layer, tr):
    npart, r, c = parts.shape
    lay = pl.BlockSpec((None, tr, c), lambda i: (layer, i, 0))
    hbm = pl.BlockSpec(memory_space=pl.ANY)
    return pl.pallas_call(_adamw_body(npart), name=name, grid=(r // tr,),
                          in_specs=[pl.BlockSpec((npart, tr, c), lambda i: (0, i, 0)), lay, lay, lay] + [hbm] * 4,
                          out_specs=[lay] * 4, out_shape=[jax.ShapeDtypeStruct(w3.shape, F32)] * 4,
                          input_output_aliases={4: 0, 5: 1, 6: 2, 7: 3},
                          compiler_params=_params(("arbitrary",)))(parts, w3, m3, v3, *bufs)


def _row_tile(r, cap):
    if r <= cap:
        return r
    t = cap - cap % SUBLANES
    while r % t:
        t -= SUBLANES
    return t


def _ada_fwd(cc, w_ada, b_loc):
    nl, d, n = w_ada.shape
    rows = cc.shape[0]

    def body(c_ref, w_ref, b_ref, o_ref):
        cond = jax.nn.silu(c_ref[...]).astype(BF16)
        o_ref[...] = jnp.dot(cond, w_ref[...].astype(BF16), preferred_element_type=F32) + b_ref[...]

    return pl.pallas_call(body, name="ada_fwd", grid=(nl,),
                          in_specs=[pl.BlockSpec((rows, d), lambda i: (0, 0)), pl.BlockSpec((None, d, n), lambda i: (i, 0, 0)),
                                    pl.BlockSpec((None, 1, n), lambda i: (i, 0, 0))],
                          out_specs=pl.BlockSpec((None, rows, n), lambda i: (i, 0, 0)),
                          out_shape=jax.ShapeDtypeStruct((nl, rows, n), F32), compiler_params=_params(("arbitrary",)))(cc, w_ada, b_loc)


def _ada_bwd(cc, w_ada, dmod_rows, dmod_ctx):
    nl, d, n = w_ada.shape
    rows = cc.shape[0]
    ctx_row = rows - SUBLANES

    def body(c_ref, w_ref, dr_ref, dc_ref, gw_ref, tot_ref, dcond_ref):
        i = pl.program_id(0)
        total = dc_ref[0]
        for p in range(1, N_DEV):
            total = total + dc_ref[p]
        tot_ref[...] = total
        row_id = lax.broadcasted_iota(jnp.int32, (rows, n), 0)
        dm = jnp.where(row_id == ctx_row, jnp.broadcast_to(total, (rows, n)), dr_ref[...]).astype(BF16)
        cond = jax.nn.silu(c_ref[...]).astype(BF16)
        gw_ref[...] = lax.dot_general(cond, dm, TN, preferred_element_type=F32)
        part = lax.dot_general(dm, w_ref[...].astype(BF16), NT, preferred_element_type=F32)
        _accumulate(dcond_ref, part, i == 0)

    return pl.pallas_call(body, name="ada_bwd", grid=(nl,),
                          in_specs=[pl.BlockSpec((rows, d), lambda i: (0, 0)), pl.BlockSpec((None, d, n), lambda i: (i, 0, 0)),
                                    pl.BlockSpec((None, rows, n), lambda i: (i, 0, 0)),
                                    pl.BlockSpec((N_DEV, None, 1, n), lambda i: (0, i, 0, 0))],
                          out_specs=[pl.BlockSpec((None, d, n), lambda i: (i, 0, 0)), pl.BlockSpec((None, 1, n), lambda i: (i, 0, 0)),
                                     pl.BlockSpec((rows, d), lambda i: (0, 0))],
                          out_shape=[jax.ShapeDtypeStruct((nl, d, n), F32), jax.ShapeDtypeStruct((nl, 1, n), F32),
                                     jax.ShapeDtypeStruct((rows, d), F32)],
                          compiler_params=_params(("arbitrary",)))(cc, w_ada, dmod_rows, dmod_ctx)


def _cctx_grad(parts, c_ctx):
    def body(p_ref, c_ref, o_ref):
        tot = p_ref[0]
        for p in range(1, N_DEV):
            tot = tot + p_ref[p]
        _, vjp = jax.vjp(jax.nn.silu, c_ref[...])
        o_ref[...] = vjp(tot)[0]

    return pl.pallas_call(body, name="cctx_grad", out_shape=jax.ShapeDtypeStruct(c_ctx.shape, F32))(parts, c_ctx)


def _discretise(lam_re, lam_im, log_dt, b_re, b_im):
    lr = jnp.minimum(lam_re, LAMBDA_RE_MAX)
    li = lam_im
    dt = jnp.exp(log_dt)
    mag = jnp.exp(lr * dt)
    ab_re = mag * jnp.cos(li * dt)
    ab_im = mag * jnp.sin(li * dt)
    den = lr * lr + li * li
    nr = ab_re - 1.0
    ni = ab_im
    coef_re = ((nr * lr + ni * li) / den)[:, None]
    coef_im = ((ni * lr - nr * li) / den)[:, None]
    bb_re = coef_re * b_re - coef_im * b_im
    bb_im = coef_re * b_im + coef_im * b_re
    return ab_re, ab_im, bb_re, bb_im


def _s5_prep(name, lam_re, lam_im, log_dt, b_re, b_im):
    def body(a, b, c, d, e, o1, o2, o3, o4):
        res = _discretise(a[...], b[...], c[...], d[...], e[...])
        for o, v in zip((o1, o2, o3, o4), res):
            o[...] = v

    shp = [jax.ShapeDtypeStruct(lam_re.shape, F32)] * 2 + [jax.ShapeDtypeStruct(b_re.shape, F32)] * 2
    return pl.pallas_call(body, name=name, out_shape=shp)(lam_re, lam_im, log_dt, b_re, b_im)


def _s5_prep_bwd(name, lam_re, lam_im, log_dt, b_re, b_im, cots):
    def body(a, b, c, d, e, c1, c2, c3, c4, o1, o2, o3, o4, o5):
        _, vjp = jax.vjp(_discretise, a[...], b[...], c[...], d[...], e[...])
        grads = vjp((c1[...], c2[...], c3[...], c4[...]))
        for o, v in zip((o1, o2, o3, o4, o5), grads):
            o[...] = v

    shp = [jax.ShapeDtypeStruct(a.shape, F32) for a in (lam_re, lam_im, log_dt, b_re, b_im)]
    return pl.pallas_call(body, name=name, out_shape=shp)(lam_re, lam_im, log_dt, b_re, b_im, *cots)


def _interleave_rows(ref, n_seq, dtype):
    n_j = ref.shape[0] // (SUBLANES * n_seq)
    return jnp.concatenate([ref[pl.ds(q * SUBLANES * n_j + j, SUBLANES, stride=n_j), :] for q in range(n_seq) for j in range(n_j)],
                           axis=0).astype(dtype)


def _store_tokens(out_ref, ref, n_seq):
    n_j = ref.shape[0] // (SUBLANES * n_seq)
    for q in range(n_seq):
        for s in range(SUBLANES):
            start = (q * SUBLANES + s) * n_j
            out_ref[start:start + n_j, :] = ref[pl.ds(q * SUBLANES * n_j + s, n_j, stride=SUBLANES), :].astype(out_ref.dtype)


def _expand_powers(t_ref, pow_ref):
    for j in range(pow_ref.shape[0] // SUBLANES):
        row = 5 * SUBLANES + j
        pow_ref[j * SUBLANES:(j + 1) * SUBLANES, :] = jnp.broadcast_to(t_ref[row:row + 1, :], (SUBLANES, pow_ref.shape[1]))


def _scan_tile(h_ref, t_ref, pow_ref, carry_ref, up, n_seq, states_ref=None):
    sw = h_ref.shape[1] // 2
    n_j = h_ref.shape[0] // (SUBLANES * n_seq)
    seqs = range(n_seq)

    def rows(g):
        if isinstance(g, int):
            return pl.ds(g * SUBLANES, SUBLANES)
        return pl.ds(pl.multiple_of(g * SUBLANES, SUBLANES), SUBLANES)

    def at(q, j):
        return rows(q * n_j + j)

    def tab(g):
        return t_ref[rows(g), :sw], t_ref[rows(g), sw:]

    def order(i):
        return n_j - 1 - i if up else i

    def cmul_add(xr, xi, ar, ai, yr, yi):
        return xr + ar * yr - ai * yi, xi + ar * yi + ai * yr

    a_re, a_im = tab(0)

    def local_step(i, xs):
        j = order(i)
        out = []
        for q in seqs:
            xr, xi = cmul_add(h_ref[at(q, j), :sw], h_ref[at(q, j), sw:], a_re, a_im, *xs[q])
            h_ref[at(q, j), :sw] = xr
            h_ref[at(q, j), sw:] = xi
            out.append((xr, xi))
        return tuple(out)

    zero = jnp.zeros((SUBLANES, sw), F32)
    ends = lax.fori_loop(0, n_j, local_step, tuple((zero, zero) for _ in seqs))
    out_row = 0 if up else SUBLANES - 1
    in_row = SUBLANES - 1 if up else 0
    one = SUBLANES - 1 if up else 1
    is_in = lax.broadcasted_iota(jnp.int32, (SUBLANES, sw), 0) == in_row
    carried, enters = [], []
    for q in seqs:
        dr, di = ends[q]
        for level, sh in enumerate((1, 2, 4)):
            amount = SUBLANES - sh if up else sh
            dr, di = cmul_add(dr, di, *tab(1 + level), pltpu.roll(dr, amount, 0), pltpu.roll(di, amount, 0))
        cr, ci = carry_ref[rows(q), :sw], carry_ref[rows(q), sw:]
        dr, di = cmul_add(dr, di, *tab(4), cr, ci)
        carry_ref[rows(q), :sw] = jnp.broadcast_to(dr[out_row:out_row + 1], dr.shape)
        carry_ref[rows(q), sw:] = jnp.broadcast_to(di[out_row:out_row + 1], di.shape)
        carried.append((cr, ci))
        enters.append((jnp.where(is_in, cr, pltpu.roll(dr, one, 0)), jnp.where(is_in, ci, pltpu.roll(di, one, 0))))

    def fix_step(i, state):
        j = order(i)
        nows = []
        for q in seqs:
            xr, xi = cmul_add(h_ref[at(q, j), :sw], h_ref[at(q, j), sw:], pow_ref[rows(j), :sw], pow_ref[rows(j), sw:], *enters[q])
            h_ref[at(q, j), :sw] = xr
            h_ref[at(q, j), sw:] = xi
            nows.append((xr, xi))
        if states_ref is None:
            return state
        befores, (acc_r, acc_i) = state
        for q in seqs:
            lr, li = befores[q]
            hr, hi = states_ref[at(q, j), :sw], states_ref[at(q, j), sw:]
            acc_r, acc_i = acc_r + lr * hr + li * hi, acc_i + li * hr - lr * hi
        return tuple(nows), (acc_r, acc_i)

    if states_ref is None:
        lax.fori_loop(0, n_j, fix_step, 0)
        return None
    lasts, (acc_r, acc_i) = lax.fori_loop(0, n_j, fix_step, (tuple((zero, zero) for _ in seqs), (zero, zero)))
    for q in seqs:
        lr = jnp.where(is_in, carried[q][0], pltpu.roll(lasts[q][0], one, 0))
        li = jnp.where(is_in, carried[q][1], pltpu.roll(lasts[q][1], one, 0))
        hr, hi = states_ref[at(q, order(0)), :sw], states_ref[at(q, order(0)), sw:]
        acc_r, acc_i = acc_r + lr * hr + li * hi, acc_i + li * hr - lr * hi
    return acc_r, acc_i


def _phys_tile(t, cfg):
    tpl, nb = cfg["tpl"], cfg["nb"]
    return jnp.where(t < nb * tpl, (t % tpl) * nb + t // tpl, t)


def _s5_block_index(cfg, dirn, adjoint):
    tpl = cfg["tpl"]

    def idx(k):
        if not adjoint:
            return jnp.where(k == 0, tpl, k - 1 if dirn == 0 else tpl - k)
        return jnp.where(k == tpl, tpl, tpl - 1 - k if dirn == 0 else k)
    return idx


RELAY_AT = 0.65


def _grid_ends(grid):
    step = 0
    for i, n in enumerate(grid):
        step = step * n + pl.program_id(i)
    total = math.prod(grid)
    return step == 0, step == int(RELAY_AT * total), step == total - 1


def _s5_fwd(name, u, bmat, cmat, tab, dirn, cfg, cps=None):
    tr, tpl, nb = cfg["tr"], cfg["tpl"], cfg["nb"]
    n_rows, d = u.shape
    ns, _, sw2 = bmat.shape
    block = _s5_block_index(cfg, dirn, False)
    up = dirn == 1
    grid = (ns, tpl + 1)
    br = nb * tr

    def body(*refs):
        copies, (u_ref, b_ref, c_ref, t_ref, h_ref, y_ref, carry_ref, mix_ref, pow_ref) = _carried(cps, 4, 2, 3, refs)
        first, middle, last = _grid_ends(grid)
        _start_all(copies, first)
        _relay_all(copies, middle)

        @pl.when(pl.program_id(1) == 0)
        def _():
            carry_ref[...] = jnp.zeros_like(carry_ref)
            _expand_powers(t_ref, pow_ref)

        mix_ref[...] = u_ref[...].astype(F32)
        h_ref[...] = jnp.dot(_interleave_rows(mix_ref, nb, BF16), b_ref[...], preferred_element_type=F32)
        _scan_tile(h_ref, t_ref, pow_ref, carry_ref, up, nb)
        mix_ref[...] = jnp.dot(h_ref[...].astype(BF16), c_ref[...], preferred_element_type=F32)
        _store_tokens(y_ref, mix_ref, nb)
        _wait_all(copies, last)

    extra = cps if cps is not None else _Copies([], [])
    return pl.pallas_call(
        body, name=name, grid=grid,
        in_specs=[pl.BlockSpec((br, LANES), lambda s, k: (block(k), s)),
                  pl.BlockSpec((None, LANES, sw2), lambda s, k: (s, 0, 0)),
                  pl.BlockSpec((None, sw2, LANES), lambda s, k: (s, 0, 0)),
                  pl.BlockSpec((None, tab.shape[1], sw2), lambda s, k: (s, 0, 0))] + extra.in_specs,
        out_specs=[pl.BlockSpec((br, sw2), lambda s, k: (block(k), s)),
                   pl.BlockSpec((br, LANES), lambda s, k: (block(k), s))] + extra.out_specs,
        out_shape=[jax.ShapeDtypeStruct((n_rows, ns * sw2), F32), jax.ShapeDtypeStruct((n_rows, d), F32)] + extra.out_shape,
        scratch_shapes=[pltpu.VMEM((nb * SUBLANES, sw2), F32), pltpu.VMEM((br, LANES), F32), pltpu.VMEM((tr, sw2), F32)] + extra.scratch,
        compiler_params=_params(("arbitrary", "arbitrary")))(u, bmat, cmat, tab, *extra.arrays)


def _s5_bwd(name, dy, h, u, cmat_t, bmat_t, tab, dirn, cfg, cps=None):
    tr, tpl, nb = cfg["tr"], cfg["tpl"], cfg["nb"]
    n_rows, d = u.shape
    ns, _, sw2 = cmat_t.shape
    sw = sw2 // 2
    block = _s5_block_index(cfg, dirn, True)
    up = dirn == 0
    grid = (ns, tpl + 1)
    br = nb * tr

    def body(*refs):
        copies, own = _carried(cps, 6, 4, 4, refs)
        dy_ref, h_ref, u_ref, ct_ref, bt_ref, t_ref, du_ref, db_ref, dc_ref, da_ref, lam_ref, carry_ref, mix_ref, pow_ref = own
        grid_first, grid_middle, grid_last = _grid_ends(grid)
        _start_all(copies, grid_first)
        _relay_all(copies, grid_middle)
        first = pl.program_id(1) == 0

        @pl.when(first)
        def _():
            carry_ref[...] = jnp.zeros_like(carry_ref)
            _expand_powers(t_ref, pow_ref)

        mix_ref[...] = dy_ref[...].astype(F32)
        dy = _interleave_rows(mix_ref, nb, BF16)
        mix_ref[...] = u_ref[...].astype(F32)
        u_mixed = _interleave_rows(mix_ref, nb, BF16)
        lam_ref[...] = jnp.dot(dy, ct_ref[...], preferred_element_type=F32)
        acc = _scan_tile(lam_ref, t_ref, pow_ref, carry_ref, up, nb, h_ref)
        lam = lam_ref[...].astype(BF16)
        d_b = lax.dot_general(u_mixed, lam, TN, preferred_element_type=F32)
        d_c = lax.dot_general(h_ref[...].astype(BF16), dy, TN, preferred_element_type=F32)
        mix_ref[...] = jnp.dot(lam, bt_ref[...], preferred_element_type=F32)
        _store_tokens(du_ref, mix_ref, nb)

        @pl.when(first)
        def _():
            db_ref[...] = d_b
            dc_ref[...] = d_c
            da_ref[:, :sw] = acc[0]
            da_ref[:, sw:] = acc[1]

        @pl.when(jnp.logical_not(first))
        def _():
            db_ref[...] += d_b
            dc_ref[...] += d_c
            da_ref[:, :sw] += acc[0]
            da_ref[:, sw:] += acc[1]

        _wait_all(copies, grid_last)

    extra = cps if cps is not None else _Copies([], [])
    return pl.pallas_call(
        body, name=name, grid=grid,
        in_specs=[pl.BlockSpec((br, LANES), lambda s, k: (block(k), s)),
                  pl.BlockSpec((br, sw2), lambda s, k: (block(k), s)),
                  pl.BlockSpec((br, LANES), lambda s, k: (block(k), s)),
                  pl.BlockSpec((None, LANES, sw2), lambda s, k: (s, 0, 0)),
                  pl.BlockSpec((None, sw2, LANES), lambda s, k: (s, 0, 0)),
                  pl.BlockSpec((None, tab.shape[1], sw2), lambda s, k: (s, 0, 0))] + extra.in_specs,
        out_specs=[pl.BlockSpec((br, LANES), lambda s, k: (block(k), s)),
                   pl.BlockSpec((None, LANES, sw2), lambda s, k: (s, 0, 0)),
                   pl.BlockSpec((None, sw2, LANES), lambda s, k: (s, 0, 0)),
                   pl.BlockSpec((None, SUBLANES, sw2), lambda s, k: (s, 0, 0))] + extra.out_specs,
        out_shape=[jax.ShapeDtypeStruct((n_rows, d), F32), jax.ShapeDtypeStruct((ns, LANES, sw2), F32),
                   jax.ShapeDtypeStruct((ns, sw2, LANES), F32), jax.ShapeDtypeStruct((ns, SUBLANES, sw2), F32)] + extra.out_shape,
        scratch_shapes=[pltpu.VMEM((br, sw2), F32), pltpu.VMEM((nb * SUBLANES, sw2), F32), pltpu.VMEM((br, LANES), F32),
                        pltpu.VMEM((tr, sw2), F32)] + extra.scratch,
        compiler_params=_params(("arbitrary", "arbitrary")))(dy, h, u, cmat_t, bmat_t, tab, *extra.arrays)


def _s5_tables(ab_re, ab_im, up, conj, ns, n_j):
    def powers_of(base, count):
        out = [base]
        for _ in range(count - 1):
            q_re, q_im = out[-1]
            out.append((q_re * base[0] - q_im * base[1], q_re * base[1] + q_im * base[0]))
        return out

    def spread(q):
        return jnp.broadcast_to(q[:, None, :], (q.shape[0], SUBLANES, q.shape[1]))

    steps = powers_of((ab_re.reshape(ns, -1), (-ab_im if conj else ab_im).reshape(ns, -1)), n_j)
    jumps = powers_of(steps[-1], SUBLANES)
    rows = jnp.arange(SUBLANES)
    blocks = [tuple(spread(q) for q in steps[0])]
    for sh in (1, 2, 4):
        keep = ((rows <= SUBLANES - 1 - sh) if up else (rows >= sh))[None, :, None]
        blocks.append(tuple(jnp.where(keep, q[:, None, :], 0.0) for q in jumps[sh - 1]))
    dist = range(SUBLANES, 0, -1) if up else range(1, SUBLANES + 1)
    blocks.append(tuple(jnp.stack([jumps[dd - 1][part] for dd in dist], axis=1) for part in (0, 1)))
    ordered = steps[::-1] if up else steps
    blocks.append(tuple(jnp.stack([q[part] for q in ordered], axis=1) for part in (0, 1)))
    return jnp.concatenate([jnp.concatenate([b[0] for b in blocks], axis=1), jnp.concatenate([b[1] for b in blocks], axis=1)], axis=2)


def _block_diag(blocks):
    ns, gs, a, b = blocks.shape
    eye = jnp.eye(gs, dtype=blocks.dtype)
    return (blocks[:, :, :, None, :] * eye[None, :, None, :, None]).reshape(ns, gs * a, gs * b)


def _diag_blocks(mat, gs):
    ns, ra, rb = mat.shape
    a, b = ra // gs, rb // gs
    m5 = mat.reshape(ns, gs, a, gs, b)
    eye = jnp.eye(gs, dtype=mat.dtype)
    return jnp.sum(m5 * eye[None, :, None, :, None], axis=3)


def _conv_flags(t, cfg):
    tpl, nb = cfg["tpl"], cfg["nb"]
    latent = t < nb * tpl
    first = jnp.logical_or(jnp.logical_not(latent), t % tpl == 0)
    last = jnp.logical_or(jnp.logical_not(latent), t % tpl == tpl - 1)
    return first, last


def _fill_ext(ext_ref, prev_ref, cur_ref, next_ref, t, cfg, halo):
    first, last = _conv_flags(t, cfg)
    tr = cur_ref.shape[0]
    for p in range(ext_ref.shape[0]):
        lanes = slice(p * LANES, (p + 1) * LANES)
        ext_ref[p, 0:halo, :] = jnp.where(first, 0.0, prev_ref[:, lanes])
        ext_ref[p, halo:halo + tr, :] = cur_ref[:, lanes]
        ext_ref[p, halo + tr:, :] = jnp.where(last, 0.0, next_ref[:, lanes])


CONV_LANES = 4 * LANES


def _conv_specs(tr, n_rows, halo, cw, cfg):
    per = tr // halo
    n_halo = n_rows // halo
    nb = cfg["nb"]
    return [pl.BlockSpec((halo, cw), lambda c, t: (jnp.maximum((_phys_tile(t, cfg) - nb + 1) * per - 1, 0), c)),
            pl.BlockSpec((tr, cw), lambda c, t: (_phys_tile(t, cfg), c)),
            pl.BlockSpec((halo, cw), lambda c, t: (jnp.minimum((_phys_tile(t, cfg) + nb) * per, n_halo - 1), c))]


def _dwconv(name, a, w, cfg):
    tr = cfg["tr"]
    n_rows, d = a.shape
    kw = w.shape[0]
    half = kw // 2
    halo = 2 * SUBLANES
    cw = min(d, CONV_LANES)

    def body(prev_ref, cur_ref, next_ref, w_ref, o_ref, ext_ref):
        _fill_ext(ext_ref, prev_ref, cur_ref, next_ref, pl.program_id(1), cfg, halo)
        for p in range(cw // LANES):
            lanes = slice(p * LANES, (p + 1) * LANES)
            acc = jnp.zeros((tr, LANES), F32)
            for k in range(kw):
                acc = acc + ext_ref[p, pl.ds(halo - half + k, tr), :] * w_ref[k:k + 1, lanes]
            o_ref[:, lanes] = acc

    return pl.pallas_call(body, name=name, grid=(d // cw, n_rows // tr),
                          in_specs=_conv_specs(tr, n_rows, halo, cw, cfg) + [pl.BlockSpec((kw, cw), lambda c, t: (0, c))],
                          out_specs=pl.BlockSpec((tr, cw), lambda c, t: (_phys_tile(t, cfg), c)),
                          out_shape=jax.ShapeDtypeStruct((n_rows, d), F32),
                          scratch_shapes=[pltpu.VMEM((cw // LANES, tr + 2 * halo, LANES), F32)],
                          compiler_params=_params(("arbitrary", "arbitrary")))(a, a, a, w)


def _dwconv_wgrad(name, a, dout, kw, cfg):
    tr = cfg["tr"]
    n_rows, d = a.shape
    half = kw // 2
    halo = 2 * SUBLANES
    cw = min(d, CONV_LANES)

    def body(prev_ref, cur_ref, next_ref, do_ref, o_ref, ext_ref):
        t = pl.program_id(1)
        _fill_ext(ext_ref, prev_ref, cur_ref, next_ref, t, cfg, halo)
        for p in range(cw // LANES):
            lanes = slice(p * LANES, (p + 1) * LANES)
            dout_t = do_ref[:, lanes]
            rows = [jnp.sum(ext_ref[p, pl.ds(halo - half + k, tr), :] * dout_t, axis=0, keepdims=True) for k in range(kw)]
            _accumulate(o_ref.at[:, lanes], jnp.concatenate(rows, axis=0), t == 0)

    return pl.pallas_call(body, name=name, grid=(d // cw, n_rows // tr),
                          in_specs=_conv_specs(tr, n_rows, halo, cw, cfg) + [pl.BlockSpec((tr, cw), lambda c, t: (_phys_tile(t, cfg), c))],
                          out_specs=pl.BlockSpec((kw, cw), lambda c, t: (0, c)),
                          out_shape=jax.ShapeDtypeStruct((kw, d), F32),
                          scratch_shapes=[pltpu.VMEM((cw // LANES, tr + 2 * halo, LANES), F32)],
                          compiler_params=_params(("arbitrary", "arbitrary")))(a, a, a, dout)


def _sincos_1d(pos, dim):
    quarter = dim // 2
    omega = POS_TEMP ** (-jnp.arange(quarter, dtype=F32) / quarter)
    ang = pos[:, None] * omega[None, :]
    return jnp.concatenate([jnp.sin(ang), jnp.cos(ang)], axis=-1)


def _grid_pos_embed(rows, dim):
    row_idx = jnp.repeat(jnp.arange(rows), GRID_W).astype(F32)
    col_idx = jnp.tile(jnp.arange(GRID_W), rows).astype(F32)
    return jnp.concatenate([_sincos_1d(row_idx, dim // 2), _sincos_1d(col_idx, dim // 2)], axis=-1)


def _pack(arrs, row_multiple=SUBLANES):
    flat = jnp.concatenate([a.reshape(-1).astype(F32) for a in arrs])
    pad = (-flat.shape[0]) % (row_multiple * LANES)
    return jnp.pad(flat, (0, pad)).reshape(-1, LANES)


def _unpack(buf, shapes):
    flat = buf.reshape(-1)
    out, pos = [], 0
    for shp in shapes:
        n = math.prod(shp)
        out.append(flat[pos:pos + n].reshape(shp))
        pos += n
    return out


def _unpack_gathered(buf, shapes):
    flat = buf.reshape(N_DEV, -1)
    out, pos = [], 0
    for shp in shapes:
        n = math.prod(shp)
        part = flat[:, pos:pos + n].reshape((N_DEV,) + tuple(shp))
        out.append(jnp.moveaxis(part, 0, -2).reshape(tuple(shp[:-1]) + (N_DEV * shp[-1],)))
        pos += n
    return out


WEIGHTS = ("c_ctx", "w_ada", "b_ada", "ln_gain", "ln_bias", "s5_lam_re", "s5_lam_im", "s5_log_dt", "s5_b_re", "s5_b_im",
           "s5_c_re", "s5_c_im", "s5_d", "s5_w_glu", "s5_b_glu", "cv_w_pw1", "cv_b_pw1", "cv_w_dw", "cv_b_dw", "cv_ln_g",
           "cv_ln_b", "cv_w_pw2", "cv_b_pw2", "mlp_w1", "mlp_w2")
SHARDED_SMALL = ("ln_gain", "ln_bias", "cv_b_pw1", "cv_w_dw", "cv_b_dw", "cv_ln_g", "cv_ln_b", "cv_b_pw2")
REPLICATED_SMALL = ("s5_lam_re", "s5_lam_im", "s5_log_dt", "s5_b_re", "s5_b_im", "s5_c_re", "s5_c_im", "s5_d", "s5_b_glu")
NATIVE_SMALL = ("s5_lam_re", "s5_lam_im", "s5_b_re", "s5_b_im", "s5_c_re", "s5_c_im")
BIG = ("mlp_w1", "mlp_w2", "s5_w_glu", "cv_w_pw1", "cv_w_pw2")


def _step(a):
    x, c, ctx = a["x"], a["c"], a["ctx"]
    nb, seq, d = x.shape
    lc = ctx.shape[1]
    nl = a["w_ada"].shape[0]
    tr = lc
    tpl = seq // tr
    cfg = {"tr": tr, "tpl": tpl, "nb": nb}
    n_rows = nb * (seq + lc)
    alpha = (2.0 * nl) ** 0.25
    me = 4 * lax.axis_index("x") + 2 * lax.axis_index("y") + lax.axis_index("c")
    n_grp, n_state = a["s5_lam_re"].shape[2:]
    ch = a["s5_b_re"].shape[-1]
    gs = LANES // ch
    ns = d // LANES
    tm = 2 * tr if n_rows % (2 * tr) == 0 else tr
    tm_big = n_rows // 3 if n_rows % (3 * 2 * SUBLANES) == 0 else tm
    f_sub1_s5, f_sub1_cv, f_sub2 = _make_sub1_s5(alpha), _make_sub1_cv(alpha), _make_sub2(alpha)

    def layer_weights(i):
        mixer = [("s5_w_glu", i // 2, 1)] if i % 2 == 0 else [("cv_w_pw1", i // 2, 1), ("cv_w_pw2", i // 2, 0)]
        return mixer + [("mlp_w1", i, 1), ("mlp_w2", i, 0)]

    weights, wgrads, received = {}, {}, {}
    small_all = _exchange("gather_small", [_pack([a[n] for n in SHARDED_SMALL])], ["slot"])[0]
    full = dict(zip(SHARDED_SMALL, _unpack_gathered(small_all, [a[n].shape for n in SHARDED_SMALL])))
    c_all = _exchange("gather_c", [c], ["slot"])[0].reshape(N_DEV * nb, d)
    cond_rows = N_DEV * nb + SUBLANES
    cc = jnp.concatenate([c_all, a["c_ctx"][None], jnp.zeros((SUBLANES - 1, d), F32)], axis=0)

    n_ada = a["w_ada"].shape[2]
    b_loc = lax.dynamic_slice(a["b_ada"], (0, me * n_ada), (nl, n_ada))[:, None, :]
    mod_cols = _ada_fwd(cc, a["w_ada"], b_loc)
    mod_all = _exchange("gather_mod", [mod_cols.reshape(nl * cond_rows, n_ada)], ["slot"])[0].reshape(N_DEV, nl, cond_rows, n_ada)
    mod_mine = jnp.concatenate([lax.dynamic_slice(mod_all, (0, 0, nb * me, 0), (N_DEV, nl, nb, n_ada)),
                                mod_all[:, :, N_DEV * nb:N_DEV * nb + 1]], axis=2)
    mod = jnp.transpose(mod_mine, (1, 2, 0, 3)).reshape(nl, nb + 1, 6, 1, d)

    def seg(i, q):
        return mod[i, :, q]

    zero_seg = jnp.zeros((nb + 1, 1, d), F32)

    def vec(v):
        return v.reshape(1, -1)

    pos = _grid_pos_embed(seq // GRID_W, d)
    def latent_rows(v):
        return jnp.transpose(v.reshape(nb, tpl, tr, d), (1, 0, 2, 3)).reshape(nb * seq, d)

    xc = jnp.concatenate([latent_rows(x), ctx.reshape(nb * lc, d)], axis=0)
    pos_rows = jnp.concatenate([latent_rows(jnp.broadcast_to(pos[None], (nb, seq, d))), jnp.zeros((nb * lc, d), F32)], axis=0)
    x_cur, h_cur = _rowwise("entry", _f_entry, [xc, pos_rows], [seg(0, 0), seg(0, 1)], [], [(d, F32), (d, BF16)], [], [], cfg)
    saved = []
    for i in range(nl):
        j = i // 2
        sv = {"x": x_cur, "h": h_cur}
        sh1, sc1, g1, sh2, sc2, g2 = (seg(i, q) for q in range(6))
        gain0, bias0, gain1, bias1 = (vec(full["ln_gain"][i, 0]), vec(full["ln_bias"][i, 0]),
                                      vec(full["ln_gain"][i, 1]), vec(full["ln_bias"][i, 1]))
        if i % 2 == 0:
            lam_re, lam_im = a["s5_lam_re"][j], a["s5_lam_im"][j]
            log_dt = a["s5_log_dt"][j][:, :, None]
            b_re_t = jnp.transpose(a["s5_b_re"][j], (0, 3, 1, 2))
            b_im_t = jnp.transpose(a["s5_b_im"][j], (0, 3, 1, 2))
            sv["prep_in"] = (lam_re, lam_im, log_dt, b_re_t, b_im_t)
            ab_re, ab_im, bb_re, bb_im = _s5_prep(f"s5_prep{i}", *sv["prep_in"])
            sv["ab"] = (ab_re, ab_im)
            ys = []
            for dirn in range(2):
                def blocks(t):
                    return jnp.transpose(t, (1, 0, 2)).reshape(ns, gs, ch, n_state)
                bmat = jnp.concatenate([_block_diag(blocks(bb_re[dirn])), _block_diag(blocks(bb_im[dirn]))], axis=2).astype(BF16)
                c_re_t = jnp.transpose(a["s5_c_re"][j, dirn], (0, 2, 1)).reshape(ns, gs, n_state, ch)
                c_im_t = jnp.transpose(a["s5_c_im"][j, dirn], (0, 2, 1)).reshape(ns, gs, n_state, ch)
                cmat = jnp.concatenate([_block_diag(c_re_t), -_block_diag(c_im_t)], axis=1).astype(BF16)
                tab = _s5_tables(ab_re[dirn], ab_im[dirn], dirn == 1, False, ns, tr // SUBLANES)
                group = layer_weights(i + dirn)
                cps = _Copies([a[n][idx].astype(BF16) for n, idx, _ in group], [axis for _, _, axis in group])
                h_states, y_dir, *gathered = _s5_fwd(f"s5_fwd{i}_{dirn}", h_cur, bmat, cmat, tab, dirn, cfg, cps)
                weights.update({(n, idx): w[None] for (n, idx, _), w in zip(group, gathered)})
                sv[f"mats{dirn}"] = (jnp.transpose(bmat, (0, 2, 1)), jnp.transpose(cmat, (0, 2, 1)))
                sv[f"states{dirn}"] = h_states
                ys.append(y_dir)
            sv["y"] = ys
            dsk = vec(a["s5_d"][j])
            z = _rowwise(f"gelu{i}", _f_gelu, [x_cur, ys[0], ys[1]], [sh1, sc1], [dsk], [(d, BF16)], [], [], cfg)[0]
            zz = _mm_nn(f"glu{i}", z, weights["s5_w_glu", j], 0, tm_big, min(2 * d, 512))[0]
            bglu = vec(a["s5_b_glu"][j])
            x1, h2 = _rowwise(f"sub1_{i}", f_sub1_s5, [x_cur, zz], [g1, sh2, sc2], [bglu, gain0, bias0],
                              [(d, F32), (d, BF16)], [], [], cfg)
            sv.update(z=z, zz=zz)
        else:
            zz = _mm_nn(f"pw1_{i}", h_cur, weights["cv_w_pw1", j], 0, tm_big, min(2 * d, 512))[0]
            bpw1 = vec(full["cv_b_pw1"][j])
            act = _rowwise(f"cvglu{i}", _f_cvglu, [zz], [], [bpw1], [(d, F32)], [], [], cfg)[0]
            w_dw = full["cv_w_dw"][j]
            cv = _dwconv(f"dwconv{i}", act, w_dw, cfg)
            bdw, lng, lnb = vec(full["cv_b_dw"][j]), vec(full["cv_ln_g"][j]), vec(full["cv_ln_b"][j])
            s_act = _rowwise(f"cvln{i}", _f_cvln, [cv], [], [bdw, lng, lnb], [(d, BF16)], [], [], cfg)[0]
            mm = _mm_nn(f"pw2_{i}", s_act, weights["cv_w_pw2", j], 0, tm_big, d)[0]
            bpw2 = vec(full["cv_b_pw2"][j])
            x1, h2 = _rowwise(f"sub1_{i}", f_sub1_cv, [x_cur, mm], [g1, sh2, sc2], [bpw2, gain0, bias0],
                              [(d, F32), (d, BF16)], [], [], cfg)
            sv.update(zz=zz, act=act, cv=cv, s_act=s_act, mm=mm, w_dw=w_dw)
        dff = weights["mlp_w1", i].shape[2]
        p_act, r_act = _mm_nn(f"mlp1_{i}", h2, weights["mlp_w1", i], 0, tm_big, min(dff, 1024), (BF16, BF16),
                              lambda acc: (jnp.square(jnp.maximum(acc, 0.0)), jnp.maximum(acc, 0.0)))
        m_out = _mm_nn(f"mlp2_{i}", p_act, weights["mlp_w2", i], 0, tm, d)[0]
        shn, scn = (seg(i + 1, 0), seg(i + 1, 1)) if i + 1 < nl else (zero_seg, zero_seg)
        x2, hn = _rowwise(f"sub2_{i}", f_sub2, [x1, m_out], [g2, shn, scn], [gain1, bias1], [(d, F32), (d, BF16)], [], [], cfg)
        sv.update(x1=x1, h2=h2, p=p_act, r=r_act, m=m_out, shn=shn, scn=scn)
        saved.append(sv)
        x_cur, h_cur = x2, hn

    target = jnp.concatenate([latent_rows(a["loss_target"]), jnp.zeros((nb * lc, d), F32)], axis=0)
    mask = jnp.concatenate([jnp.ones((nb, 1, d), F32), jnp.zeros((1, 1, d), F32)], axis=0)

    def f_loss(xf, tgt, msk):
        err = (xf - tgt) * msk
        part = 0.5 * jnp.sum(jnp.square(err), axis=(0, 1), keepdims=True) / d
        return err / d, jnp.broadcast_to(part, (1, LANES))

    dx_final, loss_part = _rowwise("loss", f_loss, [x_cur, target], [mask], [], [(d, F32)], [], [LANES], cfg)
    loss = lax.psum(loss_part[0, 0], ("x", "y", "c"))

    grads = {n: [None] * a[n].shape[0] for n in WEIGHTS if n not in ("c_ctx", "w_ada", "b_ada")}
    dmod = [[None] * 6 for _ in range(nl)]

    def add_mod(i, q, val):
        dmod[i][q] = val if dmod[i][q] is None else dmod[i][q] + val

    dx_parts, dh_parts = [dx_final], []
    for i in reversed(range(nl)):
        j = i // 2
        sv = saved[i]
        sh1, sc1, g1, sh2, sc2, g2 = (seg(i, q) for q in range(6))
        gain0, bias0, gain1, bias1 = (vec(full["ln_gain"][i, 0]), vec(full["ln_bias"][i, 0]),
                                      vec(full["ln_gain"][i, 1]), vec(full["ln_bias"][i, 1]))
        bwd = _vjp_fn(f_sub2, 2, (len(dx_parts), len(dh_parts)), (0, 1, 2, 3, 4, 5, 6))
        dx1, dm, dg2, dshn, dscn, dgain1, dbias1 = _rowwise(
            f"sub2_bwd{i}", bwd, [sv["x1"], sv["m"]] + dx_parts + dh_parts, [g2, sv["shn"], sv["scn"]], [gain1, bias1],
            [(d, F32), (d, BF16)], [d, d, d], [d, d], cfg)
        add_mod(i, 5, dg2)
        if i + 1 < nl:
            add_mod(i + 1, 0, dshn)
            add_mod(i + 1, 1, dscn)
        da = _mm_nt(f"mlp2_dgrad{i}", dm, weights["mlp_w2", i], 0, tm_big, min(dff, 1024), [sv["r"]], BF16,
                    lambda acc, r: (acc * 2.0 * r,))
        wgrads["mlp_w2", i] = _mm_wgrad_rows(f"mlp2_wgrad{i}", sv["p"], dm, tm_big)
        wgrads["mlp_w1", i] = _mm_wgrad_cols(f"mlp1_wgrad{i}", sv["h2"], da, tm_big)
        dh2 = _mm_nt(f"mlp1_dgrad{i}", da, weights["mlp_w1", i], 0, tm, d)
        if i % 2 == 0:
            bglu = vec(a["s5_b_glu"][j])
            bwd = _vjp_fn(f_sub1_s5, 2, (1, 1), (0, 1, 2, 3, 4, 5, 6, 7))
            dxa, dzz, dg1, dsh2, dsc2, dbglu, dgain0, dbias0 = _rowwise(
                f"sub1_bwd{i}", bwd, [sv["x"], sv["zz"], dx1, dh2], [g1, sh2, sc2], [bglu, gain0, bias0],
                [(d, F32), (2 * d, BF16)], [d, d, d], [2 * d, d, d], cfg)
            grads["s5_b_glu"][j] = dbglu[0]
            wgrads["s5_w_glu", j] = _mm_wgrad_cols(f"glu_wgrad{i}", sv["z"], dzz, tm_big)
            dz = _mm_nt(f"glu_dgrad{i}", dzz, weights["s5_w_glu", j], 0, tm, d)
            dsk = vec(a["s5_d"][j])
            bwd = _vjp_fn(_f_gelu, 3, (1,), (0, 1, 3, 4, 5))
            dxb, dy, dsh1, dsc1, ddsk = _rowwise(f"gelu_bwd{i}", bwd, [sv["x"], sv["y"][0], sv["y"][1], dz], [sh1, sc1], [dsk],
                                                 [(d, F32), (d, BF16)], [d, d], [d], cfg)
            grads["s5_d"][j] = ddsk[0]
            add_mod(i, 0, dsh1)
            add_mod(i, 1, dsc1)
            ab_re, ab_im = sv["ab"]
            dus, d_ab_re, d_ab_im, d_bb_re, d_bb_im, d_c_re, d_c_im = [], [], [], [], [], [], []
            for dirn in range(2):
                bmat_t, cmat_t = sv[f"mats{dirn}"]
                tab = _s5_tables(ab_re[dirn], ab_im[dirn], dirn == 0, True, ns, tr // SUBLANES)
                group = layer_weights(i + 1 - dirn)
                cps = _Copies([wgrads[n, idx] for n, idx, _ in group], ["scatter"] * len(group))
                du, d_b, d_c, d_a, *parts = _s5_bwd(f"s5_bwd{i}_{dirn}", dy, sv[f"states{dirn}"], sv["h"], cmat_t, bmat_t, tab,
                                                    dirn, cfg, cps)
                received.update({(n, idx): p for (n, idx, _), p in zip(group, parts)})
                dus.append(du)
                sw = d_a.shape[2] // 2
                d_a = jnp.sum(d_a, axis=1)
                d_ab_re.append(d_a[:, :sw].reshape(n_grp, n_state))
                d_ab_im.append(d_a[:, sw:].reshape(n_grp, n_state))

                def unblock_b(t):
                    return jnp.transpose(_diag_blocks(t, gs).reshape(n_grp, ch, n_state), (1, 0, 2))

                def unblock_c(t):
                    return jnp.transpose(_diag_blocks(t, gs).reshape(n_grp, n_state, ch), (0, 2, 1))
                d_bb_re.append(unblock_b(d_b[:, :, :sw]))
                d_bb_im.append(unblock_b(d_b[:, :, sw:]))
                d_c_re.append(unblock_c(d_c[:, :sw]))
                d_c_im.append(-unblock_c(d_c[:, sw:]))
            g_lre, g_lim, g_ldt, g_bre, g_bim = _s5_prep_bwd(
                f"s5_prep_bwd{i}", *sv["prep_in"], (jnp.stack(d_ab_re), jnp.stack(d_ab_im), jnp.stack(d_bb_re), jnp.stack(d_bb_im)))
            grads["s5_lam_re"][j], grads["s5_lam_im"][j], grads["s5_log_dt"][j] = g_lre, g_lim, g_ldt[:, :, 0]
            grads["s5_b_re"][j] = jnp.transpose(g_bre, (0, 2, 3, 1))
            grads["s5_b_im"][j] = jnp.transpose(g_bim, (0, 2, 3, 1))
            grads["s5_c_re"][j], grads["s5_c_im"][j] = jnp.stack(d_c_re), jnp.stack(d_c_im)
            dx_parts, dh_parts = [dxa, dxb], dus
        else:
            bpw2 = vec(full["cv_b_pw2"][j])
            bwd = _vjp_fn(f_sub1_cv, 2, (1, 1), (0, 1, 2, 3, 4, 5, 6, 7))
            dxa, dmm, dg1, dsh2, dsc2, dbpw2, dgain0, dbias0 = _rowwise(
                f"sub1_bwd{i}", bwd, [sv["x"], sv["mm"], dx1, dh2], [g1, sh2, sc2], [bpw2, gain0, bias0],
                [(d, F32), (d, BF16)], [d, d, d], [d, d, d], cfg)
            grads["cv_b_pw2"][j] = dbpw2[0]
            wgrads["cv_w_pw2", j] = _mm_wgrad_rows(f"pw2_wgrad{i}", sv["s_act"], dmm, tm_big)
            ds = _mm_nt(f"pw2_dgrad{i}", dmm, weights["cv_w_pw2", j], 0, tm_big, d)
            bdw, lng, lnb = vec(full["cv_b_dw"][j]), vec(full["cv_ln_g"][j]), vec(full["cv_ln_b"][j])
            bwd = _vjp_fn(_f_cvln, 1, (1,), (0, 1, 2, 3))
            dcv, dbdw, dlng, dlnb = _rowwise(f"cvln_bwd{i}", bwd, [sv["cv"], ds], [], [bdw, lng, lnb], [(d, F32)], [], [d, d, d], cfg)
            grads["cv_b_dw"][j], grads["cv_ln_g"][j], grads["cv_ln_b"][j] = dbdw[0], dlng[0], dlnb[0]
            dact = _dwconv(f"dwconv_bwd{i}", dcv, sv["w_dw"][::-1], cfg)
            grads["cv_w_dw"][j] = _dwconv_wgrad(f"dwconv_wgrad{i}", sv["act"], dcv, sv["w_dw"].shape[0], cfg)
            bpw1 = vec(full["cv_b_pw1"][j])
            bwd = _vjp_fn(_f_cvglu, 1, (1,), (0, 1))
            dzz, dbpw1 = _rowwise(f"cvglu_bwd{i}", bwd, [sv["zz"], dact], [], [bpw1], [(2 * d, BF16)], [], [2 * d], cfg)
            grads["cv_b_pw1"][j] = dbpw1[0]
            wgrads["cv_w_pw1", j] = _mm_wgrad_cols(f"pw1_wgrad{i}", sv["h"], dzz, tm_big)
            dh = _mm_nt(f"pw1_dgrad{i}", dzz, weights["cv_w_pw1", j], 0, tm, d)
            dx_parts, dh_parts = [dxa], [dh]
        grads["ln_gain"][i] = jnp.stack([dgain0[0], dgain1[0]])
        grads["ln_bias"][i] = jnp.stack([dbias0[0], dbias1[0]])
        add_mod(i, 2, dg1)
        add_mod(i, 3, dsh2)
        add_mod(i, 4, dsc2)
    bwd = _vjp_fn(_f_entry, 2, (len(dx_parts), len(dh_parts)), (0, 2, 3))
    dxc, dsh1, dsc1 = _rowwise("entry_bwd", bwd, [xc, pos_rows] + dx_parts + dh_parts, [seg(0, 0), seg(0, 1)], [],
                               [(d, F32)], [d, d], [], cfg)
    add_mod(0, 0, dsh1)
    add_mod(0, 1, dsc1)
    grad_x = jnp.transpose(dxc[:nb * seq].reshape(tpl, nb, tr, d), (1, 0, 2, 3)).reshape(nb, seq, d)

    dmod_loc = jnp.stack([jnp.concatenate([q[:, 0] for q in dmod[i]], axis=1) for i in range(nl)])
    dmod_all = _exchange("gather_dmod", [dmod_loc.reshape(nl * (nb + 1), 6 * d)], ["slot"])[0].reshape(N_DEV, nl, nb + 1, 6 * d)
    mine = lax.dynamic_slice(dmod_all, (0, 0, 0, me * n_ada), (N_DEV, nl, nb + 1, n_ada))
    dmod_rows = jnp.transpose(mine[:, :, :nb], (1, 0, 2, 3)).reshape(nl, N_DEV * nb, n_ada)
    dmod_rows = jnp.concatenate([dmod_rows, jnp.zeros((nl, SUBLANES, n_ada), F32)], axis=1)
    g_w_ada, _, dcond = _ada_bwd(cc, a["w_ada"], dmod_rows, mine[:, :, nb:])
    g_b_ada = _sum_lead("b_ada_sum", jnp.transpose(dmod_all, (0, 2, 1, 3)).reshape(N_DEV * (nb + 1), nl, 6 * d), nl)
    dcond_all = _exchange("gather_dcond", [dcond[N_DEV * nb:N_DEV * nb + 1]], ["slot"])[0]
    g_c_ctx = _cctx_grad(dcond_all, a["c_ctx"][None])[0]

    small_names = SHARDED_SMALL + REPLICATED_SMALL
    small_full = [jnp.stack(grads[n]) for n in small_names]
    small_packed = _pack(small_full, N_DEV * SUBLANES)
    small_parts = _exchange("scatter_small_grads", [small_packed.reshape(N_DEV, -1, LANES)], ["scatter"])[0]
    small_part = _sum_lead("small_grad_sum", small_parts, _row_tile(small_parts.shape[1], 512))
    small_sum = _exchange("gather_small_sum", [small_part], ["slot"])[0]
    small_g = dict(zip(small_names, _unpack(small_sum, [g.shape for g in small_full])))
    for n in SHARDED_SMALL:
        width = a[n].shape[-1]
        start = (0,) * (small_g[n].ndim - 1) + (me * width,)
        small_g[n] = lax.dynamic_slice(small_g[n], start, a[n].shape)
    small_g["c_ctx"], small_g["b_ada"] = g_c_ctx, g_b_ada

    out = {}

    def update(n, parts):
        shp = a[n].shape
        cols = parts.shape[-1]
        rows = parts.shape[1]
        res = _adamw(f"adamw_{n}", parts, a[n].reshape(rows, cols), a["m_" + n].reshape(rows, cols), a["v_" + n].reshape(rows, cols),
                     _row_tile(rows, max(SUBLANES, 131072 // cols)))
        out[n] = [r.reshape(shp) for r in res]

    for n in BIG:
        rows, cols = a[n].shape[1:]
        bufs = [lax.empty(a[n].shape, F32) for _ in range(4)]
        for idx in range(a[n].shape[0]):
            bufs = _adamw_layer(f"adamw_{n}{idx}", received[n, idx], a[n], a["m_" + n], a["v_" + n], bufs, idx,
                                _row_tile(rows, max(SUBLANES, 131072 // cols)))
        out[n] = bufs
    update("w_ada", g_w_ada.reshape(1, -1, n_ada))
    for n in NATIVE_SMALL:
        out[n] = [small_g[n], *_adamw_native(f"adamw_{n}", small_g[n], a[n], a["m_" + n], a["v_" + n])]
    small_all_names = ("c_ctx", "b_ada") + tuple(n for n in small_names if n not in NATIVE_SMALL)
    packed = [_pack([src[n] for n in small_all_names]) for src in
              (small_g, a, {n: a["m_" + n] for n in small_all_names}, {n: a["v_" + n] for n in small_all_names})]
    res = _adamw("adamw_small", packed[0][None], packed[1], packed[2], packed[3], _row_tile(packed[0].shape[0], 512))
    shapes = [a[n].shape for n in small_all_names]
    for n, vals in zip(small_all_names, zip(*[_unpack(r, shapes) for r in res])):
        out[n] = list(vals)
    return (loss, grad_x, *[out[n][0] for n in WEIGHTS], *[out[n][1] for n in WEIGHTS],
            *[out[n][2] for n in WEIGHTS], *[out[n][3] for n in WEIGHTS])


def kernel(x, c, ctx, c_ctx, w_ada, b_ada, ln_gain, ln_bias, s5_lam_re, s5_lam_im, s5_log_dt, s5_b_re, s5_b_im, s5_c_re, s5_c_im, s5_d, s5_w_glu, s5_b_glu, cv_w_pw1, cv_b_pw1, cv_w_dw, cv_b_dw, cv_ln_g, cv_ln_b, cv_w_pw2, cv_b_pw2, mlp_w1, mlp_w2, loss_target, m_c_ctx, m_w_ada, m_b_ada, m_ln_gain, m_ln_bias, m_s5_lam_re, m_s5_lam_im, m_s5_log_dt, m_s5_b_re, m_s5_b_im, m_s5_c_re, m_s5_c_im, m_s5_d, m_s5_w_glu, m_s5_b_glu, m_cv_w_pw1, m_cv_b_pw1, m_cv_w_dw, m_cv_b_dw, m_cv_ln_g, m_cv_ln_b, m_cv_w_pw2, m_cv_b_pw2, m_mlp_w1, m_mlp_w2, v_c_ctx, v_w_ada, v_b_ada, v_ln_gain, v_ln_bias, v_s5_lam_re, v_s5_lam_im, v_s5_log_dt, v_s5_b_re, v_s5_b_im, v_s5_c_re, v_s5_c_im, v_s5_d, v_s5_w_glu, v_s5_b_glu, v_cv_w_pw1, v_cv_b_pw1, v_cv_w_dw, v_cv_b_dw, v_cv_ln_g, v_cv_ln_b, v_cv_w_pw2, v_cv_b_pw2, v_mlp_w1, v_mlp_w2):
    return _step(dict(locals()))
```

```python
import functools
import math

import jax
import jax.numpy as jnp
from jax import lax
from jax.experimental import pallas as pl
from jax.experimental.pallas import tpu as pltpu

F32 = jnp.float32
BF16 = jnp.bfloat16
N_DEV = 8
LANES = 128
SUBLANES = 8
VMEM_LIMIT = 56 * 1024 * 1024
GRID_W = 64
POS_TEMP = 10000.0
LN_EPS = 1e-5
LAMBDA_RE_MAX = -1e-4
ADAM_LR, ADAM_B1, ADAM_B2, ADAM_EPS, ADAM_WD, ADAM_STEP = 0.001, 0.9, 0.999, 1e-08, 0.01, 10
MESH = pl.DeviceIdType.MESH


def _params(sem):
    return pltpu.CompilerParams(dimension_semantics=sem, vmem_limit_bytes=VMEM_LIMIT)


def _accumulate(ref, val, first):
    @pl.when(first)
    def _():
        ref[...] = val

    @pl.when(jnp.logical_not(first))
    def _():
        ref[...] += val


def _rowwise(name, fn, rows, segs, vecs, row_outs, seg_accs, vec_accs, cfg):
    tr, tpl, nb = cfg["tr"], cfg["tpl"], cfg["nb"]
    n_rows = rows[0].shape[0]
    nt = n_rows // tr
    nr, ns, nv = len(rows), len(segs), len(vecs)
    nro, nsa = len(row_outs), len(seg_accs)

    def seg_of(t):
        return jnp.minimum(t // tpl, nb)

    def body(*refs):
        t = pl.program_id(0)
        ins, outs = refs[:nr + ns + nv], refs[nr + ns + nv:]
        vals = [r[...] for r in ins[:nr]] + [r[0] for r in ins[nr:nr + ns]] + [r[...] for r in ins[nr + ns:]]
        res = fn(*vals)
        for o, v in zip(outs[:nro], res[:nro]):
            o[...] = v.astype(o.dtype)
        first_seg = jnp.logical_or(t == 0, seg_of(t) != seg_of(jnp.maximum(t - 1, 0)))
        for o, v in zip(outs[nro:nro + nsa], res[nro:nro + nsa]):
            _accumulate(o.at[0], v, first_seg)
        for o, v in zip(outs[nro + nsa:], res[nro + nsa:]):
            _accumulate(o, v, t == 0)

    in_specs = ([pl.BlockSpec((tr, a.shape[1]), lambda t: (_phys_tile(t, cfg), 0)) for a in rows]
                + [pl.BlockSpec((1, 1, a.shape[2]), lambda t: (seg_of(t), 0, 0)) for a in segs]
                + [pl.BlockSpec((1, a.shape[1]), lambda t: (0, 0)) for a in vecs])
    out_specs = ([pl.BlockSpec((tr, c), lambda t: (_phys_tile(t, cfg), 0)) for c, _ in row_outs]
                 + [pl.BlockSpec((1, 1, c), lambda t: (seg_of(t), 0, 0)) for c in seg_accs]
                 + [pl.BlockSpec((1, c), lambda t: (0, 0)) for c in vec_accs])
    out_shape = ([jax.ShapeDtypeStruct((n_rows, c), dt) for c, dt in row_outs]
                 + [jax.ShapeDtypeStruct((nb + 1, 1, c), F32) for c in seg_accs]
                 + [jax.ShapeDtypeStruct((1, c), F32) for c in vec_accs])
    return pl.pallas_call(body, name=name, grid=(nt,), in_specs=in_specs, out_specs=out_specs,
                          out_shape=out_shape, compiler_params=_params(("arbitrary",)))(*rows, *segs, *vecs)


def _vjp_fn(fn, n_row, cot_groups, want):
    n_cot = sum(cot_groups)

    def bwd(*args):
        primals = [a.astype(F32) for a in args[:n_row] + args[n_row + n_cot:]]
        outs, vjp = jax.vjp(fn, *primals)
        cots, pos = [], n_row
        for n, o in zip(cot_groups, outs):
            cot = jnp.zeros_like(o)
            for part in args[pos:pos + n]:
                cot = cot + part.astype(F32)
            cots.append(cot)
            pos += n
        grads = vjp(tuple(cots))
        return tuple(grads[i] for i in want)
    return bwd


def _ln(r, g, b):
    mu = jnp.mean(r, axis=-1, keepdims=True)
    var = jnp.mean(jnp.square(r - mu), axis=-1, keepdims=True)
    return (r - mu) * lax.rsqrt(var + LN_EPS) * g + b


def _glu(zz, bias):
    d = zz.shape[1] // 2
    return (zz[:, :d] + bias[:, :d]) * jax.nn.sigmoid(zz[:, d:] + bias[:, d:])


def _f_entry(xc, pos, sh, sc):
    x0 = xc + pos
    return x0, x0 * (1 + sc) + sh


def _f_gelu(x, y0, y1, sh, sc, dsk):
    u = x * (1 + sc) + sh
    y = dsk * u + y0 + y1
    return (0.5 * y * (1.0 + lax.erf(y * (2.0 ** -0.5))),)


def _make_sub1_s5(alpha):
    def f(x, zz, g1, sh2, sc2, bglu, gain, bias):
        x1 = _ln(alpha * x + g1 * _glu(zz, bglu), gain, bias)
        return x1, x1 * (1 + sc2) + sh2
    return f


def _make_sub1_cv(alpha):
    def f(x, mm, g1, sh2, sc2, bpw2, gain, bias):
        x1 = _ln(alpha * x + g1 * (mm + bpw2), gain, bias)
        return x1, x1 * (1 + sc2) + sh2
    return f


def _make_sub2(alpha):
    def f(x1, m, g2, shn, scn, gain, bias):
        x2 = _ln(alpha * x1 + g2 * m, gain, bias)
        return x2, x2 * (1 + scn) + shn
    return f


def _f_cvglu(zz, bpw1):
    return (_glu(zz, bpw1),)


def _f_cvln(cv, bdw, lng, lnb):
    return (jax.nn.silu(_ln(cv + bdw, lng, lnb)),)


def _matmul(name, a, b, extras, grid, a_spec, b_spec, extra_specs, o_specs, out_shape, dims, red_axis, epi, sem):
    n_extra = len(extras)
    n_out = len(out_shape)
    acc_shape = o_specs[0].block_shape
    acc_shape = tuple(s for s in acc_shape if s is not None)

    def body(*refs):
        a_ref, b_ref = refs[0], refs[1]
        ex = refs[2:2 + n_extra]
        outs = refs[2 + n_extra:2 + n_extra + n_out]
        prod = lax.dot_general(a_ref[...], b_ref[...], dims, preferred_element_type=F32)

        def finish(acc):
            res = epi(acc, *[e[...] for e in ex]) if epi is not None else (acc,)
            for o, v in zip(outs, res):
                o[...] = v.astype(o.dtype)

        if red_axis is None:
            finish(prod)
        else:
            acc_ref = refs[-1]
            k = pl.program_id(red_axis)
            nk = pl.num_programs(red_axis)

            @pl.when(k == 0)
            def _():
                acc_ref[...] = prod

            @pl.when(k > 0)
            def _():
                acc_ref[...] += prod

            @pl.when(k == nk - 1)
            def _():
                finish(acc_ref[...])

    scratch = [] if red_axis is None else [pltpu.VMEM(acc_shape, F32)]
    res = pl.pallas_call(body, name=name, grid=grid, in_specs=[a_spec, b_spec] + list(extra_specs),
                         out_specs=list(o_specs), out_shape=list(out_shape), scratch_shapes=scratch,
                         compiler_params=_params(sem))(a, b, *extras)
    return res


NN = (((1,), (0,)), ((), ()))
NT = (((1,), (1,)), ((), ()))
TN = (((0,), (0,)), ((), ()))


def _mm_nn(name, a, w3, layer, tm, tn, out_dtypes=(F32,), epi=None):
    m, k = a.shape
    n = w3.shape[2]
    return _matmul(name, a, w3, (), (n // tn, m // tm),
                   pl.BlockSpec((tm, k), lambda j, i: (i, 0)), pl.BlockSpec((None, k, tn), lambda j, i: (layer, 0, j)), (),
                   [pl.BlockSpec((tm, tn), lambda j, i: (i, j)) for _ in out_dtypes],
                   [jax.ShapeDtypeStruct((m, n), dt) for dt in out_dtypes], NN, None, epi, ("arbitrary", "arbitrary"))


def _mm_nt(name, dy, w3, layer, tm, tkw, extras=(), out_dtype=F32, epi=None):
    m, n = dy.shape
    kw = w3.shape[1]
    return _matmul(name, dy, w3, tuple(extras), (kw // tkw, m // tm),
                   pl.BlockSpec((tm, n), lambda j, i: (i, 0)), pl.BlockSpec((None, tkw, n), lambda j, i: (layer, j, 0)),
                   [pl.BlockSpec((tm, tkw), lambda j, i: (i, j)) for _ in extras],
                   [pl.BlockSpec((tm, tkw), lambda j, i: (i, j))], [jax.ShapeDtypeStruct((m, kw), out_dtype)], NT, None, epi,
                   ("arbitrary", "arbitrary"))[0]


def _mm_wgrad_cols(name, a, dy, tm):
    m, k = a.shape
    n = dy.shape[1] // N_DEV
    return _matmul(name, a, dy, (), (N_DEV, m // tm),
                   pl.BlockSpec((tm, k), lambda j, i: (i, 0)), pl.BlockSpec((tm, n), lambda j, i: (i, j)), (),
                   [pl.BlockSpec((None, k, n), lambda j, i: (j, 0, 0))], [jax.ShapeDtypeStruct((N_DEV, k, n), BF16)],
                   TN, 1, None, ("arbitrary", "arbitrary"))[0]


def _mm_wgrad_rows(name, a, dy, tm):
    m = a.shape[0]
    r = a.shape[1] // N_DEV
    n = dy.shape[1]
    return _matmul(name, a, dy, (), (N_DEV, m // tm),
                   pl.BlockSpec((tm, r), lambda j, i: (i, j)), pl.BlockSpec((tm, n), lambda j, i: (i, 0)), (),
                   [pl.BlockSpec((None, r, n), lambda j, i: (j, 0, 0))], [jax.ShapeDtypeStruct((N_DEV, r, n), BF16)],
                   TN, 1, None, ("arbitrary", "arbitrary"))[0]


class _Copies:
    def __init__(self, arrays, kinds):
        self.arrays, self.kinds, self.n = list(arrays), list(kinds), len(arrays)
        any_spec = pl.BlockSpec(memory_space=pl.ANY)
        self.in_specs = [any_spec] * self.n
        self.out_specs = [any_spec] * self.n
        self.out_shape = [jax.ShapeDtypeStruct(self._result(a, kind), a.dtype) for a, kind in zip(arrays, kinds)]
        self.scratch = [pltpu.SemaphoreType.DMA((self.n, N_DEV - 1)), pltpu.SemaphoreType.DMA((self.n, N_DEV - 1)),
                        pltpu.SemaphoreType.DMA((self.n,))] if self.n else []

    @staticmethod
    def _result(a, kind):
        if kind == "slot":
            return (N_DEV,) + a.shape
        if kind == "scatter":
            return a.shape
        return a.shape[:kind] + (N_DEV * a.shape[kind],) + a.shape[kind + 1:]

    def descriptors(self, ins, outs, sems):
        send_sems, recv_sems, local_sems = sems
        x, y, c = lax.axis_index("x"), lax.axis_index("y"), lax.axis_index("c")
        me = 4 * x + 2 * y + c
        first, relay, finish = [], [], []

        def remote(i, k, src, dst, to):
            return pltpu.make_async_remote_copy(src_ref=src, dst_ref=dst, send_sem=send_sems.at[i, k], recv_sem=recv_sems.at[i, k],
                                                device_id=to, device_id_type=MESH)

        for i, kind in enumerate(self.kinds):
            if kind in ("slot", "scatter"):
                scatter = kind == "scatter"
                local = pltpu.make_async_copy(ins[i].at[me] if scatter else ins[i], outs[i].at[me], local_sems.at[i])
                first.append(local)
                finish.append((local, "all"))
                for k in range(1, N_DEV):
                    px = 1 - x if k & 4 else x
                    py = 1 - y if k & 2 else y
                    pc = 1 - c if k & 1 else c
                    cp = remote(i, k - 1, ins[i].at[4 * px + 2 * py + pc] if scatter else ins[i], outs[i].at[me], (px, py, pc))
                    first.append(cp)
                    finish.append((cp, "all"))
                continue
            size = ins[i].shape[kind]

            def block(px, py, pc):
                return outs[i].at[(slice(None),) * kind + (pl.ds(pl.multiple_of((4 * px + 2 * py + pc) * size, size), size),)]

            local = pltpu.make_async_copy(ins[i], block(x, y, c), local_sems.at[i])
            sibling = remote(i, 0, ins[i], block(x, y, c), (x, y, 1 - c))
            first += [local, sibling]
            finish += [(local, "all"), (sibling, "all")]
            for j, (qx, qy) in enumerate([(1 - x, y), (x, 1 - y), (1 - x, 1 - y)]):
                out = remote(i, 1 + j, ins[i], block(x, y, c), (qx, qy, c))
                onward = remote(i, 4 + j, block(qx, qy, c), block(qx, qy, c), (x, y, 1 - c))
                first.append(out)
                relay.append((out, onward))
                finish += [(out, "send"), (onward, "all")]
        return first, relay, finish


def _guarded(when, fn):
    if when is None:
        fn()
    else:
        pl.when(when)(fn)


def _start_all(plan, when=None):
    def run():
        for cp in plan[0]:
            cp.start()
    _guarded(when, run)


def _relay_all(plan, when=None):
    def run():
        for arrived, onward in plan[1]:
            arrived.wait_recv()
            onward.start()
    if plan[1]:
        _guarded(when, run)


def _wait_all(plan, when=None):
    def run():
        for cp, left in plan[2]:
            if left == "send":
                cp.wait_send()
            else:
                cp.wait()
    _guarded(when, run)


def _exchange(name, arrays, kinds):
    cps = _Copies(arrays, kinds)
    n = cps.n

    def body(*refs):
        plan = cps.descriptors(refs[:n], refs[n:2 * n], refs[2 * n:])
        _start_all(plan)
        _relay_all(plan)
        _wait_all(plan)

    return pl.pallas_call(body, name=name, in_specs=cps.in_specs, out_specs=cps.out_specs, out_shape=cps.out_shape,
                          scratch_shapes=cps.scratch)(*arrays)


def _carried(cps, n_in, n_out, n_scratch, refs):
    if cps is None:
        return ([], [], []), refs
    n = cps.n
    ins = refs[n_in:n_in + n]
    outs = refs[n_in + n + n_out:n_in + n + n_out + n]
    sems = refs[n_in + n + n_out + n + n_scratch:]
    own = refs[:n_in] + refs[n_in + n:n_in + n + n_out] + refs[n_in + n + n_out + n:n_in + n + n_out + n + n_scratch]
    return cps.descriptors(ins, outs, sems), own


def _sum_lead(name, parts, tr):
    npart, r, c = parts.shape

    def body(p_ref, o_ref):
        acc = p_ref[0].astype(F32)
        for p in range(1, npart):
            acc = acc + p_ref[p].astype(F32)
        o_ref[...] = acc

    return pl.pallas_call(body, name=name, grid=(r // tr,), in_specs=[pl.BlockSpec((npart, tr, c), lambda i: (0, i, 0))],
                          out_specs=pl.BlockSpec((tr, c), lambda i: (i, 0)), out_shape=jax.ShapeDtypeStruct((r, c), F32),
                          compiler_params=_params(("arbitrary",)))(parts)


def _adamw_math(g, w, m, v):
    m2 = ADAM_B1 * m + (1.0 - ADAM_B1) * g
    v2 = ADAM_B2 * v + (1.0 - ADAM_B2) * jnp.square(g)
    m_hat = m2 / (1.0 - ADAM_B1 ** ADAM_STEP)
    v_hat = v2 / (1.0 - ADAM_B2 ** ADAM_STEP)
    return -ADAM_LR * (m_hat / (jnp.sqrt(v_hat) + ADAM_EPS) + ADAM_WD * w), m2, v2


def _adamw_body(npart):
    def body(p_ref, w_ref, m_ref, v_ref, *rest):
        g_out, d_out, m_out, v_out = rest[-4:]
        g = p_ref[0].astype(F32)
        for p in range(1, npart):
            g = g + p_ref[p].astype(F32)
        g_out[...] = g
        d_out[...], m_out[...], v_out[...] = _adamw_math(g, w_ref[...], m_ref[...], v_ref[...])
    return body


def _adamw_native(name, g, w, m, v):
    rest = w.shape[2:]
    spec = pl.BlockSpec((None, None) + rest, lambda i, j: (i, j) + (0,) * len(rest))

    def body(g_ref, w_ref, m_ref, v_ref, d_out, m_out, v_out):
        d_out[...], m_out[...], v_out[...] = _adamw_math(g_ref[...], w_ref[...], m_ref[...], v_ref[...])

    return pl.pallas_call(body, name=name, grid=w.shape[:2], in_specs=[spec] * 4, out_specs=[spec] * 3,
                          out_shape=[jax.ShapeDtypeStruct(w.shape, F32)] * 3,
                          compiler_params=_params(("arbitrary", "arbitrary")))(g, w, m, v)


def _adamw(name, parts, w, m, v, tr):
    npart, r, c = parts.shape
    row = pl.BlockSpec((tr, c), lambda i: (i, 0))
    return pl.pallas_call(_adamw_body(npart), name=name, grid=(r // tr,),
                          in_specs=[pl.BlockSpec((npart, tr, c), lambda i: (0, i, 0)), row, row, row],
                          out_specs=[row] * 4, out_shape=[jax.ShapeDtypeStruct((r, c), F32)] * 4,
                          compiler_params=_params(("arbitrary",)))(parts, w, m, v)


def _adamw_layer(name, parts, w3, m3, v3, bufs, layer, tr):
    npart, r, c = parts.shape
    lay = pl.BlockSpec((None, tr, c), lambda i: (layer, i, 0))
    hbm = pl.BlockSpec(memory_space=pl.ANY)
    return pl.pallas_call(_adamw_body(npart), name=name, grid=(r // tr,),
                          in_specs=[pl.BlockSpec((npart, tr, c), lambda i: (0, i, 0)), lay, lay, lay] + [hbm] * 4,
                          out_specs=[lay] * 4, out_shape=[jax.ShapeDtypeStruct(w3.shape, F32)] * 4,
                          input_output_aliases={4: 0, 5: 1, 6: 2, 7: 3},
                          compiler_params=_params(("arbitrary",)))(parts, w3, m3, v3, *bufs)


def _row_tile(r, cap):
    if r <= cap:
        return r
    t = cap - cap % SUBLANES
    while r % t:
        t -= SUBLANES
    return t


def _ada_fwd(cc, w_ada, b_loc):
    nl, d, n = w_ada.shape
    rows = cc.shape[0]

    def body(c_ref, w_ref, b_ref, o_ref):
        cond = jax.nn.silu(c_ref[...]).astype(BF16)
        o_ref[...] = jnp.dot(cond, w_ref[...].astype(BF16), preferred_element_type=F32) + b_ref[...]

    return pl.pallas_call(body, name="ada_fwd", grid=(nl,),
                          in_specs=[pl.BlockSpec((rows, d), lambda i: (0, 0)), pl.BlockSpec((None, d, n), lambda i: (i, 0, 0)),
                                    pl.BlockSpec((None, 1, n), lambda i: (i, 0, 0))],
                          out_specs=pl.BlockSpec((None, rows, n), lambda i: (i, 0, 0)),
                          out_shape=jax.ShapeDtypeStruct((nl, rows, n), F32), compiler_params=_params(("arbitrary",)))(cc, w_ada, b_loc)


def _ada_bwd(cc, w_ada, dmod_rows, dmod_ctx):
    nl, d, n = w_ada.shape
    rows = cc.shape[0]
    ctx_row = rows - SUBLANES

    def body(c_ref, w_ref, dr_ref, dc_ref, gw_ref, tot_ref, dcond_ref):
        i = pl.program_id(0)
        total = dc_ref[0]
        for p in range(1, N_DEV):
            total = total + dc_ref[p]
        tot_ref[...] = total
        row_id = lax.broadcasted_iota(jnp.int32, (rows, n), 0)
        dm = jnp.where(row_id == ctx_row, jnp.broadcast_to(total, (rows, n)), dr_ref[...]).astype(BF16)
        cond = jax.nn.silu(c_ref[...]).astype(BF16)
        gw_ref[...] = lax.dot_general(cond, dm, TN, preferred_element_type=F32)
        part = lax.dot_general(dm, w_ref[...].astype(BF16), NT, preferred_element_type=F32)
        _accumulate(dcond_ref, part, i == 0)

    return pl.pallas_call(body, name="ada_bwd", grid=(nl,),
                          in_specs=[pl.BlockSpec((rows, d), lambda i: (0, 0)), pl.BlockSpec((None, d, n), lambda i: (i, 0, 0)),
                                    pl.BlockSpec((None, rows, n), lambda i: (i, 0, 0)),
                                    pl.BlockSpec((N_DEV, None, 1, n), lambda i: (0, i, 0, 0))],
                          out_specs=[pl.BlockSpec((None, d, n), lambda i: (i, 0, 0)), pl.BlockSpec((None, 1, n), lambda i: (i, 0, 0)),
                                     pl.BlockSpec((rows, d), lambda i: (0, 0))],
                          out_shape=[jax.ShapeDtypeStruct((nl, d, n), F32), jax.ShapeDtypeStruct((nl, 1, n), F32),
                                     jax.ShapeDtypeStruct((rows, d), F32)],
                          compiler_params=_params(("arbitrary",)))(cc, w_ada, dmod_rows, dmod_ctx)


def _cctx_grad(parts, c_ctx):
    def body(p_ref, c_ref, o_ref):
        tot = p_ref[0]
        for p in range(1, N_DEV):
            tot = tot + p_ref[p]
        _, vjp = jax.vjp(jax.nn.silu, c_ref[...])
        o_ref[...] = vjp(tot)[0]

    return pl.pallas_call(body, name="cctx_grad", out_shape=jax.ShapeDtypeStruct(c_ctx.shape, F32))(parts, c_ctx)


def _discretise(lam_re, lam_im, log_dt, b_re, b_im):
    lr = jnp.minimum(lam_re, LAMBDA_RE_MAX)
    li = lam_im
    dt = jnp.exp(log_dt)
    mag = jnp.exp(lr * dt)
    ab_re = mag * jnp.cos(li * dt)
    ab_im = mag * jnp.sin(li * dt)
    den = lr * lr + li * li
    nr = ab_re - 1.0
    ni = ab_im
    coef_re = ((nr * lr + ni * li) / den)[:, None]
    coef_im = ((ni * lr - nr * li) / den)[:, None]
    bb_re = coef_re * b_re - coef_im * b_im
    bb_im = coef_re * b_im + coef_im * b_re
    return ab_re, ab_im, bb_re, bb_im


def _s5_prep(name, lam_re, lam_im, log_dt, b_re, b_im):
    def body(a, b, c, d, e, o1, o2, o3, o4):
        res = _discretise(a[...], b[...], c[...], d[...], e[...])
        for o, v in zip((o1, o2, o3, o4), res):
            o[...] = v

    shp = [jax.ShapeDtypeStruct(lam_re.shape, F32)] * 2 + [jax.ShapeDtypeStruct(b_re.shape, F32)] * 2
    return pl.pallas_call(body, name=name, out_shape=shp)(lam_re, lam_im, log_dt, b_re, b_im)


def _s5_prep_bwd(name, lam_re, lam_im, log_dt, b_re, b_im, cots):
    def body(a, b, c, d, e, c1, c2, c3, c4, o1, o2, o3, o4, o5):
        _, vjp = jax.vjp(_discretise, a[...], b[...], c[...], d[...], e[...])
        grads = vjp((c1[...], c2[...], c3[...], c4[...]))
        for o, v in zip((o1, o2, o3, o4, o5), grads):
            o[...] = v

    shp = [jax.ShapeDtypeStruct(a.shape, F32) for a in (lam_re, lam_im, log_dt, b_re, b_im)]
    return pl.pallas_call(body, name=name, out_shape=shp)(lam_re, lam_im, log_dt, b_re, b_im, *cots)


def _interleave_rows(ref, n_seq, dtype):
    n_j = ref.shape[0] // (SUBLANES * n_seq)
    return jnp.concatenate([ref[pl.ds(q * SUBLANES * n_j + j, SUBLANES, stride=n_j), :] for q in range(n_seq) for j in range(n_j)],
                           axis=0).astype(dtype)


def _store_tokens(out_ref, ref, n_seq):
    n_j = ref.shape[0] // (SUBLANES * n_seq)
    for q in range(n_seq):
        for s in range(SUBLANES):
            start = (q * SUBLANES + s) * n_j
            out_ref[start:start + n_j, :] = ref[pl.ds(q * SUBLANES * n_j + s, n_j, stride=SUBLANES), :].astype(out_ref.dtype)


def _expand_powers(t_ref, pow_ref):
    for j in range(pow_ref.shape[0] // SUBLANES):
        row = 5 * SUBLANES + j
        pow_ref[j * SUBLANES:(j + 1) * SUBLANES, :] = jnp.broadcast_to(t_ref[row:row + 1, :], (SUBLANES, pow_ref.shape[1]))


def _scan_tile(h_ref, t_ref, pow_ref, carry_ref, up, n_seq, states_ref=None):
    sw = h_ref.shape[1] // 2
    n_j = h_ref.shape[0] // (SUBLANES * n_seq)
    seqs = range(n_seq)

    def rows(g):
        if isinstance(g, int):
            return pl.ds(g * SUBLANES, SUBLANES)
        return pl.ds(pl.multiple_of(g * SUBLANES, SUBLANES), SUBLANES)

    def at(q, j):
        return rows(q * n_j + j)

    def tab(g):
        return t_ref[rows(g), :sw], t_ref[rows(g), sw:]

    def order(i):
        return n_j - 1 - i if up else i

    def cmul_add(xr, xi, ar, ai, yr, yi):
        return xr + ar * yr - ai * yi, xi + ar * yi + ai * yr

    a_re, a_im = tab(0)

    def local_step(i, xs):
        j = order(i)
        out = []
        for q in seqs:
            xr, xi = cmul_add(h_ref[at(q, j), :sw], h_ref[at(q, j), sw:], a_re, a_im, *xs[q])
            h_ref[at(q, j), :sw] = xr
            h_ref[at(q, j), sw:] = xi
            out.append((xr, xi))
        return tuple(out)

    zero = jnp.zeros((SUBLANES, sw), F32)
    ends = lax.fori_loop(0, n_j, local_step, tuple((zero, zero) for _ in seqs))
    out_row = 0 if up else SUBLANES - 1
    in_row = SUBLANES - 1 if up else 0
    one = SUBLANES - 1 if up else 1
    is_in = lax.broadcasted_iota(jnp.int32, (SUBLANES, sw), 0) == in_row
    carried, enters = [], []
    for q in seqs:
        dr, di = ends[q]
        for level, sh in enumerate((1, 2, 4)):
            amount = SUBLANES - sh if up else sh
            dr, di = cmul_add(dr, di, *tab(1 + level), pltpu.roll(dr, amount, 0), pltpu.roll(di, amount, 0))
        cr, ci = carry_ref[rows(q), :sw], carry_ref[rows(q), sw:]
        dr, di = cmul_add(dr, di, *tab(4), cr, ci)
        carry_ref[rows(q), :sw] = jnp.broadcast_to(dr[out_row:out_row + 1], dr.shape)
        carry_ref[rows(q), sw:] = jnp.broadcast_to(di[out_row:out_row + 1], di.shape)
        carried.append((cr, ci))
        enters.append((jnp.where(is_in, cr, pltpu.roll(dr, one, 0)), jnp.where(is_in, ci, pltpu.roll(di, one, 0))))

    acc_r = acc_i = zero
    for q in seqs:
        def fix_step(i, state, q=q):
            j = order(i)
            xr, xi = cmul_add(h_ref[at(q, j), :sw], h_ref[at(q, j), sw:], pow_ref[rows(j), :sw], pow_ref[rows(j), sw:], *enters[q])
            h_ref[at(q, j), :sw] = xr
            h_ref[at(q, j), sw:] = xi
            if states_ref is None:
                return state
            (lr, li), (sum_r, sum_i) = state
            hr, hi = states_ref[at(q, j), :sw], states_ref[at(q, j), sw:]
            return (xr, xi), (sum_r + lr * hr + li * hi, sum_i + li * hr - lr * hi)

        if states_ref is None:
            lax.fori_loop(0, n_j, fix_step, 0)
            continue
        (lr, li), (acc_r, acc_i) = lax.fori_loop(0, n_j, fix_step, ((zero, zero), (acc_r, acc_i)))
        lr = jnp.where(is_in, carried[q][0], pltpu.roll(lr, one, 0))
        li = jnp.where(is_in, carried[q][1], pltpu.roll(li, one, 0))
        hr, hi = states_ref[at(q, order(0)), :sw], states_ref[at(q, order(0)), sw:]
        acc_r, acc_i = acc_r + lr * hr + li * hi, acc_i + li * hr - lr * hi
    return None if states_ref is None else (acc_r, acc_i)


def _phys_tile(t, cfg):
    tpl, nb = cfg["tpl"], cfg["nb"]
    return jnp.where(t < nb * tpl, (t % tpl) * nb + t // tpl, t)


def _s5_block_index(cfg, dirn, adjoint):
    tpl = cfg["tpl"]

    def idx(k):
        if not adjoint:
            return jnp.where(k == 0, tpl, k - 1 if dirn == 0 else tpl - k)
        return jnp.where(k == tpl, tpl, tpl - 1 - k if dirn == 0 else k)
    return idx


RELAY_AT = 0.65


def _grid_ends(grid):
    step = 0
    for i, n in enumerate(grid):
        step = step * n + pl.program_id(i)
    total = math.prod(grid)
    return step == 0, step == int(RELAY_AT * total), step == total - 1


def _s5_fwd(name, u, bmat, cmat, tab, dirn, cfg, cps=None):
    tr, tpl, nb = cfg["tr"], cfg["tpl"], cfg["nb"]
    n_rows, d = u.shape
    ns, _, sw2 = bmat.shape
    block = _s5_block_index(cfg, dirn, False)
    up = dirn == 1
    grid = (ns, tpl + 1)
    br = nb * tr

    def body(*refs):
        copies, (u_ref, b_ref, c_ref, t_ref, h_ref, y_ref, carry_ref, mix_ref, pow_ref) = _carried(cps, 4, 2, 3, refs)
        first, middle, last = _grid_ends(grid)
        _start_all(copies, first)
        _relay_all(copies, middle)

        @pl.when(pl.program_id(1) == 0)
        def _():
            carry_ref[...] = jnp.zeros_like(carry_ref)
            _expand_powers(t_ref, pow_ref)

        mix_ref[...] = u_ref[...].astype(F32)
        h_ref[...] = jnp.dot(_interleave_rows(mix_ref, nb, BF16), b_ref[...], preferred_element_type=F32)
        _scan_tile(h_ref, t_ref, pow_ref, carry_ref, up, nb)
        mix_ref[...] = jnp.dot(h_ref[...].astype(BF16), c_ref[...], preferred_element_type=F32)
        _store_tokens(y_ref, mix_ref, nb)
        _wait_all(copies, last)

    extra = cps if cps is not None else _Copies([], [])
    return pl.pallas_call(
        body, name=name, grid=grid,
        in_specs=[pl.BlockSpec((br, LANES), lambda s, k: (block(k), s)),
                  pl.BlockSpec((None, LANES, sw2), lambda s, k: (s, 0, 0)),
                  pl.BlockSpec((None, sw2, LANES), lambda s, k: (s, 0, 0)),
                  pl.BlockSpec((None, tab.shape[1], sw2), lambda s, k: (s, 0, 0))] + extra.in_specs,
        out_specs=[pl.BlockSpec((br, sw2), lambda s, k: (block(k), s)),
                   pl.BlockSpec((br, LANES), lambda s, k: (block(k), s))] + extra.out_specs,
        out_shape=[jax.ShapeDtypeStruct((n_rows, ns * sw2), F32), jax.ShapeDtypeStruct((n_rows, d), F32)] + extra.out_shape,
        scratch_shapes=[pltpu.VMEM((nb * SUBLANES, sw2), F32), pltpu.VMEM((br, LANES), F32), pltpu.VMEM((tr, sw2), F32)] + extra.scratch,
        compiler_params=_params(("arbitrary", "arbitrary")))(u, bmat, cmat, tab, *extra.arrays)


def _s5_bwd(name, dy, h, u, cmat_t, bmat_t, tab, dirn, cfg, cps=None):
    tr, tpl, nb = cfg["tr"], cfg["tpl"], cfg["nb"]
    n_rows, d = u.shape
    ns, _, sw2 = cmat_t.shape
    sw = sw2 // 2
    block = _s5_block_index(cfg, dirn, True)
    up = dirn == 0
    grid = (ns, tpl + 1)
    br = nb * tr

    def body(*refs):
        copies, own = _carried(cps, 6, 4, 4, refs)
        dy_ref, h_ref, u_ref, ct_ref, bt_ref, t_ref, du_ref, db_ref, dc_ref, da_ref, lam_ref, carry_ref, mix_ref, pow_ref = own
        grid_first, grid_middle, grid_last = _grid_ends(grid)
        _start_all(copies, grid_first)
        _relay_all(copies, grid_middle)
        first = pl.program_id(1) == 0

        @pl.when(first)
        def _():
            carry_ref[...] = jnp.zeros_like(carry_ref)
            _expand_powers(t_ref, pow_ref)

        mix_ref[...] = dy_ref[...].astype(F32)
        dy = _interleave_rows(mix_ref, nb, BF16)
        mix_ref[...] = u_ref[...].astype(F32)
        u_mixed = _interleave_rows(mix_ref, nb, BF16)
        lam_ref[...] = jnp.dot(dy, ct_ref[...], preferred_element_type=F32)
        acc = _scan_tile(lam_ref, t_ref, pow_ref, carry_ref, up, nb, h_ref)
        lam = lam_ref[...].astype(BF16)
        d_b = lax.dot_general(u_mixed, lam, TN, preferred_element_type=F32)
        d_c = lax.dot_general(dy, h_ref[...].astype(BF16), TN, preferred_element_type=F32)
        mix_ref[...] = jnp.dot(lam, bt_ref[...], preferred_element_type=F32)
        _store_tokens(du_ref, mix_ref, nb)

        @pl.when(first)
        def _():
            db_ref[...] = d_b
            dc_ref[...] = d_c
            da_ref[:, :sw] = acc[0]
            da_ref[:, sw:] = acc[1]

        @pl.when(jnp.logical_not(first))
        def _():
            db_ref[...] += d_b
            dc_ref[...] += d_c
            da_ref[:, :sw] += acc[0]
            da_ref[:, sw:] += acc[1]

        _wait_all(copies, grid_last)

    extra = cps if cps is not None else _Copies([], [])
    return pl.pallas_call(
        body, name=name, grid=grid,
        in_specs=[pl.BlockSpec((br, LANES), lambda s, k: (block(k), s)),
                  pl.BlockSpec((br, sw2), lambda s, k: (block(k), s)),
                  pl.BlockSpec((br, LANES), lambda s, k: (block(k), s)),
                  pl.BlockSpec((None, LANES, sw2), lambda s, k: (s, 0, 0)),
                  pl.BlockSpec((None, sw2, LANES), lambda s, k: (s, 0, 0)),
                  pl.BlockSpec((None, tab.shape[1], sw2), lambda s, k: (s, 0, 0))] + extra.in_specs,
        out_specs=[pl.BlockSpec((br, LANES), lambda s, k: (block(k), s)),
                   pl.BlockSpec((None, LANES, sw2), lambda s, k: (s, 0, 0)),
                   pl.BlockSpec((None, LANES, sw2), lambda s, k: (s, 0, 0)),
                   pl.BlockSpec((None, SUBLANES, sw2), lambda s, k: (s, 0, 0))] + extra.out_specs,
        out_shape=[jax.ShapeDtypeStruct((n_rows, d), F32), jax.ShapeDtypeStruct((ns, LANES, sw2), F32),
                   jax.ShapeDtypeStruct((ns, LANES, sw2), F32), jax.ShapeDtypeStruct((ns, SUBLANES, sw2), F32)] + extra.out_shape,
        scratch_shapes=[pltpu.VMEM((br, sw2), F32), pltpu.VMEM((nb * SUBLANES, sw2), F32), pltpu.VMEM((br, LANES), F32),
                        pltpu.VMEM((tr, sw2), F32)] + extra.scratch,
        compiler_params=_params(("arbitrary", "arbitrary")))(dy, h, u, cmat_t, bmat_t, tab, *extra.arrays)


def _s5_tables(ab_re, ab_im, up, conj, ns, n_j):
    def powers_of(base, count):
        out = [base]
        for _ in range(count - 1):
            q_re, q_im = out[-1]
            out.append((q_re * base[0] - q_im * base[1], q_re * base[1] + q_im * base[0]))
        return out

    def spread(q):
        return jnp.broadcast_to(q[:, None, :], (q.shape[0], SUBLANES, q.shape[1]))

    steps = powers_of((ab_re.reshape(ns, -1), (-ab_im if conj else ab_im).reshape(ns, -1)), n_j)
    jumps = powers_of(steps[-1], SUBLANES)
    rows = jnp.arange(SUBLANES)
    blocks = [tuple(spread(q) for q in steps[0])]
    for sh in (1, 2, 4):
        keep = ((rows <= SUBLANES - 1 - sh) if up else (rows >= sh))[None, :, None]
        blocks.append(tuple(jnp.where(keep, q[:, None, :], 0.0) for q in jumps[sh - 1]))
    dist = range(SUBLANES, 0, -1) if up else range(1, SUBLANES + 1)
    blocks.append(tuple(jnp.stack([jumps[dd - 1][part] for dd in dist], axis=1) for part in (0, 1)))
    ordered = steps[::-1] if up else steps
    blocks.append(tuple(jnp.stack([q[part] for q in ordered], axis=1) for part in (0, 1)))
    return jnp.concatenate([jnp.concatenate([b[0] for b in blocks], axis=1), jnp.concatenate([b[1] for b in blocks], axis=1)], axis=2)


def _block_diag(blocks):
    ns, gs, a, b = blocks.shape
    eye = jnp.eye(gs, dtype=blocks.dtype)
    return (blocks[:, :, :, None, :] * eye[None, :, None, :, None]).reshape(ns, gs * a, gs * b)


def _diag_blocks(mat, gs):
    ns, ra, rb = mat.shape
    a, b = ra // gs, rb // gs
    m5 = mat.reshape(ns, gs, a, gs, b)
    eye = jnp.eye(gs, dtype=mat.dtype)
    return jnp.sum(m5 * eye[None, :, None, :, None], axis=3)


def _conv_flags(t, cfg):
    tpl, nb = cfg["tpl"], cfg["nb"]
    latent = t < nb * tpl
    first = jnp.logical_or(jnp.logical_not(latent), t % tpl == 0)
    last = jnp.logical_or(jnp.logical_not(latent), t % tpl == tpl - 1)
    return first, last


def _fill_ext(ext_ref, prev_ref, cur_ref, next_ref, t, cfg, halo):
    first, last = _conv_flags(t, cfg)
    tr = cur_ref.shape[0]
    for p in range(ext_ref.shape[0]):
        lanes = slice(p * LANES, (p + 1) * LANES)
        ext_ref[p, 0:halo, :] = jnp.where(first, 0.0, prev_ref[:, lanes])
        ext_ref[p, halo:halo + tr, :] = cur_ref[:, lanes]
        ext_ref[p, halo + tr:, :] = jnp.where(last, 0.0, next_ref[:, lanes])


CONV_LANES = 4 * LANES


def _conv_specs(tr, n_rows, halo, cw, cfg):
    per = tr // halo
    n_halo = n_rows // halo
    nb = cfg["nb"]
    return [pl.BlockSpec((halo, cw), lambda c, t: (jnp.maximum((_phys_tile(t, cfg) - nb + 1) * per - 1, 0), c)),
            pl.BlockSpec((tr, cw), lambda c, t: (_phys_tile(t, cfg), c)),
            pl.BlockSpec((halo, cw), lambda c, t: (jnp.minimum((_phys_tile(t, cfg) + nb) * per, n_halo - 1), c))]


def _dwconv(name, a, w, cfg):
    tr = cfg["tr"]
    n_rows, d = a.shape
    kw = w.shape[0]
    half = kw // 2
    halo = 2 * SUBLANES
    cw = min(d, CONV_LANES)

    def body(prev_ref, cur_ref, next_ref, w_ref, o_ref, ext_ref):
        _fill_ext(ext_ref, prev_ref, cur_ref, next_ref, pl.program_id(1), cfg, halo)
        for p in range(cw // LANES):
            lanes = slice(p * LANES, (p + 1) * LANES)
            acc = jnp.zeros((tr, LANES), F32)
            for k in range(kw):
                acc = acc + ext_ref[p, pl.ds(halo - half + k, tr), :] * w_ref[k:k + 1, lanes]
            o_ref[:, lanes] = acc

    return pl.pallas_call(body, name=name, grid=(d // cw, n_rows // tr),
                          in_specs=_conv_specs(tr, n_rows, halo, cw, cfg) + [pl.BlockSpec((kw, cw), lambda c, t: (0, c))],
                          out_specs=pl.BlockSpec((tr, cw), lambda c, t: (_phys_tile(t, cfg), c)),
                          out_shape=jax.ShapeDtypeStruct((n_rows, d), F32),
                          scratch_shapes=[pltpu.VMEM((cw // LANES, tr + 2 * halo, LANES), F32)],
                          compiler_params=_params(("arbitrary", "arbitrary")))(a, a, a, w)


def _dwconv_wgrad(name, a, dout, kw, cfg):
    tr = cfg["tr"]
    n_rows, d = a.shape
    half = kw // 2
    halo = 2 * SUBLANES
    cw = min(d, CONV_LANES)

    def body(prev_ref, cur_ref, next_ref, do_ref, o_ref, ext_ref):
        t = pl.program_id(1)
        _fill_ext(ext_ref, prev_ref, cur_ref, next_ref, t, cfg, halo)
        for p in range(cw // LANES):
            lanes = slice(p * LANES, (p + 1) * LANES)
            dout_t = do_ref[:, lanes]
            rows = [jnp.sum(ext_ref[p, pl.ds(halo - half + k, tr), :] * dout_t, axis=0, keepdims=True) for k in range(kw)]
            _accumulate(o_ref.at[:, lanes], jnp.concatenate(rows, axis=0), t == 0)

    return pl.pallas_call(body, name=name, grid=(d // cw, n_rows // tr),
                          in_specs=_conv_specs(tr, n_rows, halo, cw, cfg) + [pl.BlockSpec((tr, cw), lambda c, t: (_phys_tile(t, cfg), c))],
                          out_specs=pl.BlockSpec((kw, cw), lambda c, t: (0, c)),
                          out_shape=jax.ShapeDtypeStruct((kw, d), F32),
                          scratch_shapes=[pltpu.VMEM((cw // LANES, tr + 2 * halo, LANES), F32)],
                          compiler_params=_params(("arbitrary", "arbitrary")))(a, a, a, dout)


def _sincos_1d(pos, dim):
    quarter = dim // 2
    omega = POS_TEMP ** (-jnp.arange(quarter, dtype=F32) / quarter)
    ang = pos[:, None] * omega[None, :]
    return jnp.concatenate([jnp.sin(ang), jnp.cos(ang)], axis=-1)


def _grid_pos_embed(rows, dim):
    row_idx = jnp.repeat(jnp.arange(rows), GRID_W).astype(F32)
    col_idx = jnp.tile(jnp.arange(GRID_W), rows).astype(F32)
    return jnp.concatenate([_sincos_1d(row_idx, dim // 2), _sincos_1d(col_idx, dim // 2)], axis=-1)


def _pack(arrs, row_multiple=SUBLANES):
    flat = jnp.concatenate([a.reshape(-1).astype(F32) for a in arrs])
    pad = (-flat.shape[0]) % (row_multiple * LANES)
    return jnp.pad(flat, (0, pad)).reshape(-1, LANES)


def _unpack(buf, shapes):
    flat = buf.reshape(-1)
    out, pos = [], 0
    for shp in shapes:
        n = math.prod(shp)
        out.append(flat[pos:pos + n].reshape(shp))
        pos += n
    return out


def _unpack_gathered(buf, shapes):
    flat = buf.reshape(N_DEV, -1)
    out, pos = [], 0
    for shp in shapes:
        n = math.prod(shp)
        part = flat[:, pos:pos + n].reshape((N_DEV,) + tuple(shp))
        out.append(jnp.moveaxis(part, 0, -2).reshape(tuple(shp[:-1]) + (N_DEV * shp[-1],)))
        pos += n
    return out


WEIGHTS = ("c_ctx", "w_ada", "b_ada", "ln_gain", "ln_bias", "s5_lam_re", "s5_lam_im", "s5_log_dt", "s5_b_re", "s5_b_im",
           "s5_c_re", "s5_c_im", "s5_d", "s5_w_glu", "s5_b_glu", "cv_w_pw1", "cv_b_pw1", "cv_w_dw", "cv_b_dw", "cv_ln_g",
           "cv_ln_b", "cv_w_pw2", "cv_b_pw2", "mlp_w1", "mlp_w2")
SHARDED_SMALL = ("ln_gain", "ln_bias", "cv_b_pw1", "cv_w_dw", "cv_b_dw", "cv_ln_g", "cv_ln_b", "cv_b_pw2")
REPLICATED_SMALL = ("s5_lam_re", "s5_lam_im", "s5_log_dt", "s5_b_re", "s5_b_im", "s5_c_re", "s5_c_im", "s5_d", "s5_b_glu")
NATIVE_SMALL = ("s5_lam_re", "s5_lam_im", "s5_b_re", "s5_b_im", "s5_c_re", "s5_c_im")
BIG = ("mlp_w1", "mlp_w2", "s5_w_glu", "cv_w_pw1", "cv_w_pw2")


def _step(a):
    x, c, ctx = a["x"], a["c"], a["ctx"]
    nb, seq, d = x.shape
    lc = ctx.shape[1]
    nl = a["w_ada"].shape[0]
    tr = lc
    tpl = seq // tr
    cfg = {"tr": tr, "tpl": tpl, "nb": nb}
    n_rows = nb * (seq + lc)
    alpha = (2.0 * nl) ** 0.25
    me = 4 * lax.axis_index("x") + 2 * lax.axis_index("y") + lax.axis_index("c")
    n_grp, n_state = a["s5_lam_re"].shape[2:]
    ch = a["s5_b_re"].shape[-1]
    gs = LANES // ch
    ns = d // LANES
    tm = 2 * tr if n_rows % (2 * tr) == 0 else tr
    tm_big = n_rows // 3 if n_rows % (3 * 2 * SUBLANES) == 0 else tm
    f_sub1_s5, f_sub1_cv, f_sub2 = _make_sub1_s5(alpha), _make_sub1_cv(alpha), _make_sub2(alpha)

    def layer_weights(i):
        mixer = [("s5_w_glu", i // 2, 1)] if i % 2 == 0 else [("cv_w_pw1", i // 2, 1), ("cv_w_pw2", i // 2, 0)]
        return mixer + [("mlp_w1", i, 1), ("mlp_w2", i, 0)]

    weights, wgrads, received = {}, {}, {}
    small_all, c_all = _exchange("gather_small", [_pack([a[n] for n in SHARDED_SMALL]), c], ["slot", "slot"])
    full = dict(zip(SHARDED_SMALL, _unpack_gathered(small_all, [a[n].shape for n in SHARDED_SMALL])))
    c_all = c_all.reshape(N_DEV * nb, d)
    cond_rows = N_DEV * nb + SUBLANES
    cc = jnp.concatenate([c_all, a["c_ctx"][None], jnp.zeros((SUBLANES - 1, d), F32)], axis=0)

    n_ada = a["w_ada"].shape[2]
    b_loc = lax.dynamic_slice(a["b_ada"], (0, me * n_ada), (nl, n_ada))[:, None, :]
    mod_cols = _ada_fwd(cc, a["w_ada"], b_loc)
    mod_all = _exchange("gather_mod", [mod_cols.reshape(nl * cond_rows, n_ada)], ["slot"])[0].reshape(N_DEV, nl, cond_rows, n_ada)
    mod_mine = jnp.concatenate([lax.dynamic_slice(mod_all, (0, 0, nb * me, 0), (N_DEV, nl, nb, n_ada)),
                                mod_all[:, :, N_DEV * nb:N_DEV * nb + 1]], axis=2)
    mod = jnp.transpose(mod_mine, (1, 2, 0, 3)).reshape(nl, nb + 1, 6, 1, d)

    def seg(i, q):
        return mod[i, :, q]

    zero_seg = jnp.zeros((nb + 1, 1, d), F32)

    def vec(v):
        return v.reshape(1, -1)

    pos = _grid_pos_embed(seq // GRID_W, d)
    def latent_rows(v):
        return jnp.transpose(v.reshape(nb, tpl, tr, d), (1, 0, 2, 3)).reshape(nb * seq, d)

    xc = jnp.concatenate([latent_rows(x), ctx.reshape(nb * lc, d)], axis=0)
    pos_rows = jnp.concatenate([latent_rows(jnp.broadcast_to(pos[None], (nb, seq, d))), jnp.zeros((nb * lc, d), F32)], axis=0)
    x_cur, h_cur = _rowwise("entry", _f_entry, [xc, pos_rows], [seg(0, 0), seg(0, 1)], [], [(d, F32), (d, BF16)], [], [], cfg)
    saved = []
    for i in range(nl):
        j = i // 2
        sv = {"x": x_cur, "h": h_cur}
        sh1, sc1, g1, sh2, sc2, g2 = (seg(i, q) for q in range(6))
        gain0, bias0, gain1, bias1 = (vec(full["ln_gain"][i, 0]), vec(full["ln_bias"][i, 0]),
                                      vec(full["ln_gain"][i, 1]), vec(full["ln_bias"][i, 1]))
        if i % 2 == 0:
            lam_re, lam_im = a["s5_lam_re"][j], a["s5_lam_im"][j]
            log_dt = a["s5_log_dt"][j][:, :, None]
            b_re_t = jnp.transpose(a["s5_b_re"][j], (0, 3, 1, 2))
            b_im_t = jnp.transpose(a["s5_b_im"][j], (0, 3, 1, 2))
            sv["prep_in"] = (lam_re, lam_im, log_dt, b_re_t, b_im_t)
            ab_re, ab_im, bb_re, bb_im = _s5_prep(f"s5_prep{i}", *sv["prep_in"])
            sv["ab"] = (ab_re, ab_im)
            ys = []
            for dirn in range(2):
                def blocks(t):
                    return jnp.transpose(t, (1, 0, 2)).reshape(ns, gs, ch, n_state)
                bmat = jnp.concatenate([_block_diag(blocks(bb_re[dirn])), _block_diag(blocks(bb_im[dirn]))], axis=2).astype(BF16)
                c_re_t = jnp.transpose(a["s5_c_re"][j, dirn], (0, 2, 1)).reshape(ns, gs, n_state, ch)
                c_im_t = jnp.transpose(a["s5_c_im"][j, dirn], (0, 2, 1)).reshape(ns, gs, n_state, ch)
                cmat = jnp.concatenate([_block_diag(c_re_t), -_block_diag(c_im_t)], axis=1).astype(BF16)
                tab = _s5_tables(ab_re[dirn], ab_im[dirn], dirn == 1, False, ns, tr // SUBLANES)
                group = layer_weights(i + dirn)
                cps = _Copies([a[n][idx].astype(BF16) for n, idx, _ in group], [axis for _, _, axis in group])
                h_states, y_dir, *gathered = _s5_fwd(f"s5_fwd{i}_{dirn}", h_cur, bmat, cmat, tab, dirn, cfg, cps)
                weights.update({(n, idx): w[None] for (n, idx, _), w in zip(group, gathered)})
                sv[f"mats{dirn}"] = (jnp.transpose(bmat, (0, 2, 1)), jnp.transpose(cmat, (0, 2, 1)))
                sv[f"states{dirn}"] = h_states
                ys.append(y_dir)
            sv["y"] = ys
            dsk = vec(a["s5_d"][j])
            z = _rowwise(f"gelu{i}", _f_gelu, [x_cur, ys[0], ys[1]], [sh1, sc1], [dsk], [(d, BF16)], [], [], cfg)[0]
            zz = _mm_nn(f"glu{i}", z, weights["s5_w_glu", j], 0, tm_big, min(2 * d, 512))[0]
            bglu = vec(a["s5_b_glu"][j])
            x1, h2 = _rowwise(f"sub1_{i}", f_sub1_s5, [x_cur, zz], [g1, sh2, sc2], [bglu, gain0, bias0],
                              [(d, F32), (d, BF16)], [], [], cfg)
            sv.update(z=z, zz=zz)
        else:
            zz = _mm_nn(f"pw1_{i}", h_cur, weights["cv_w_pw1", j], 0, tm_big, min(2 * d, 512))[0]
            bpw1 = vec(full["cv_b_pw1"][j])
            act = _rowwise(f"cvglu{i}", _f_cvglu, [zz], [], [bpw1], [(d, F32)], [], [], cfg)[0]
            w_dw = full["cv_w_dw"][j]
            cv = _dwconv(f"dwconv{i}", act, w_dw, cfg)
            bdw, lng, lnb = vec(full["cv_b_dw"][j]), vec(full["cv_ln_g"][j]), vec(full["cv_ln_b"][j])
            s_act = _rowwise(f"cvln{i}", _f_cvln, [cv], [], [bdw, lng, lnb], [(d, BF16)], [], [], cfg)[0]
            mm = _mm_nn(f"pw2_{i}", s_act, weights["cv_w_pw2", j], 0, tm_big, d)[0]
            bpw2 = vec(full["cv_b_pw2"][j])
            x1, h2 = _rowwise(f"sub1_{i}", f_sub1_cv, [x_cur, mm], [g1, sh2, sc2], [bpw2, gain0, bias0],
                              [(d, F32), (d, BF16)], [], [], cfg)
            sv.update(zz=zz, act=act, cv=cv, s_act=s_act, mm=mm, w_dw=w_dw)
        dff = weights["mlp_w1", i].shape[2]
        p_act, r_act = _mm_nn(f"mlp1_{i}", h2, weights["mlp_w1", i], 0, tm_big, min(dff, 1024), (BF16, BF16),
                              lambda acc: (jnp.square(jnp.maximum(acc, 0.0)), jnp.maximum(acc, 0.0)))
        m_out = _mm_nn(f"mlp2_{i}", p_act, weights["mlp_w2", i], 0, tm, d)[0]
        shn, scn = (seg(i + 1, 0), seg(i + 1, 1)) if i + 1 < nl else (zero_seg, zero_seg)
        x2, hn = _rowwise(f"sub2_{i}", f_sub2, [x1, m_out], [g2, shn, scn], [gain1, bias1], [(d, F32), (d, BF16)], [], [], cfg)
        sv.update(x1=x1, h2=h2, p=p_act, r=r_act, m=m_out, shn=shn, scn=scn)
        saved.append(sv)
        x_cur, h_cur = x2, hn

    target = jnp.concatenate([latent_rows(a["loss_target"]), jnp.zeros((nb * lc, d), F32)], axis=0)
    mask = jnp.concatenate([jnp.ones((nb, 1, d), F32), jnp.zeros((1, 1, d), F32)], axis=0)

    def f_loss(xf, tgt, msk):
        err = (xf - tgt) * msk
        part = 0.5 * jnp.sum(jnp.square(err), axis=(0, 1), keepdims=True) / d
        return err / d, jnp.broadcast_to(part, (1, LANES))

    dx_final, loss_part = _rowwise("loss", f_loss, [x_cur, target], [mask], [], [(d, F32)], [], [LANES], cfg)
    loss = lax.psum(loss_part[0, 0], ("x", "y", "c"))

    grads = {n: [None] * a[n].shape[0] for n in WEIGHTS if n not in ("c_ctx", "w_ada", "b_ada")}
    dmod = [[None] * 6 for _ in range(nl)]

    def add_mod(i, q, val):
        dmod[i][q] = val if dmod[i][q] is None else dmod[i][q] + val

    dx_parts, dh_parts = [dx_final], []
    for i in reversed(range(nl)):
        j = i // 2
        sv = saved[i]
        sh1, sc1, g1, sh2, sc2, g2 = (seg(i, q) for q in range(6))
        gain0, bias0, gain1, bias1 = (vec(full["ln_gain"][i, 0]), vec(full["ln_bias"][i, 0]),
                                      vec(full["ln_gain"][i, 1]), vec(full["ln_bias"][i, 1]))
        bwd = _vjp_fn(f_sub2, 2, (len(dx_parts), len(dh_parts)), (0, 1, 2, 3, 4, 5, 6))
        dx1, dm, dg2, dshn, dscn, dgain1, dbias1 = _rowwise(
            f"sub2_bwd{i}", bwd, [sv["x1"], sv["m"]] + dx_parts + dh_parts, [g2, sv["shn"], sv["scn"]], [gain1, bias1],
            [(d, F32), (d, BF16)], [d, d, d], [d, d], cfg)
        add_mod(i, 5, dg2)
        if i + 1 < nl:
            add_mod(i + 1, 0, dshn)
            add_mod(i + 1, 1, dscn)
        da = _mm_nt(f"mlp2_dgrad{i}", dm, weights["mlp_w2", i], 0, tm_big, min(dff, 1024), [sv["r"]], BF16,
                    lambda acc, r: (acc * 2.0 * r,))
        wgrads["mlp_w2", i] = _mm_wgrad_rows(f"mlp2_wgrad{i}", sv["p"], dm, tm_big)
        wgrads["mlp_w1", i] = _mm_wgrad_cols(f"mlp1_wgrad{i}", sv["h2"], da, tm_big)
        dh2 = _mm_nt(f"mlp1_dgrad{i}", da, weights["mlp_w1", i], 0, tm, d)
        if i % 2 == 0:
            bglu = vec(a["s5_b_glu"][j])
            bwd = _vjp_fn(f_sub1_s5, 2, (1, 1), (0, 1, 2, 3, 4, 5, 6, 7))
            dxa, dzz, dg1, dsh2, dsc2, dbglu, dgain0, dbias0 = _rowwise(
                f"sub1_bwd{i}", bwd, [sv["x"], sv["zz"], dx1, dh2], [g1, sh2, sc2], [bglu, gain0, bias0],
                [(d, F32), (2 * d, BF16)], [d, d, d], [2 * d, d, d], cfg)
            grads["s5_b_glu"][j] = dbglu[0]
            wgrads["s5_w_glu", j] = _mm_wgrad_cols(f"glu_wgrad{i}", sv["z"], dzz, tm_big)
            dz = _mm_nt(f"glu_dgrad{i}", dzz, weights["s5_w_glu", j], 0, tm, d)
            dsk = vec(a["s5_d"][j])
            bwd = _vjp_fn(_f_gelu, 3, (1,), (0, 1, 3, 4, 5))
            dxb, dy, dsh1, dsc1, ddsk = _rowwise(f"gelu_bwd{i}", bwd, [sv["x"], sv["y"][0], sv["y"][1], dz], [sh1, sc1], [dsk],
                                                 [(d, F32), (d, BF16)], [d, d], [d], cfg)
            grads["s5_d"][j] = ddsk[0]
            add_mod(i, 0, dsh1)
            add_mod(i, 1, dsc1)
            ab_re, ab_im = sv["ab"]
            dus, d_ab_re, d_ab_im, d_bb_re, d_bb_im, d_c_re, d_c_im = [], [], [], [], [], [], []
            for dirn in range(2):
                bmat_t, cmat_t = sv[f"mats{dirn}"]
                tab = _s5_tables(ab_re[dirn], ab_im[dirn], dirn == 0, True, ns, tr // SUBLANES)
                group = layer_weights(i + 1 - dirn)
                cps = _Copies([wgrads[n, idx] for n, idx, _ in group], ["scatter"] * len(group))
                du, d_b, d_c, d_a, *parts = _s5_bwd(f"s5_bwd{i}_{dirn}", dy, sv[f"states{dirn}"], sv["h"], cmat_t, bmat_t, tab,
                                                    dirn, cfg, cps)
                received.update({(n, idx): p for (n, idx, _), p in zip(group, parts)})
                dus.append(du)
                sw = d_a.shape[2] // 2
                d_a = jnp.sum(d_a, axis=1)
                d_ab_re.append(d_a[:, :sw].reshape(n_grp, n_state))
                d_ab_im.append(d_a[:, sw:].reshape(n_grp, n_state))

                def unblock_b(t):
                    return jnp.transpose(_diag_blocks(t, gs).reshape(n_grp, ch, n_state), (1, 0, 2))

                def unblock_c(t):
                    return jnp.transpose(_diag_blocks(t, gs).reshape(n_grp, n_state, ch), (0, 2, 1))
                d_bb_re.append(unblock_b(d_b[:, :, :sw]))
                d_bb_im.append(unblock_b(d_b[:, :, sw:]))
                d_c = jnp.transpose(d_c, (0, 2, 1))
                d_c_re.append(unblock_c(d_c[:, :sw]))
                d_c_im.append(-unblock_c(d_c[:, sw:]))
            g_lre, g_lim, g_ldt, g_bre, g_bim = _s5_prep_bwd(
                f"s5_prep_bwd{i}", *sv["prep_in"], (jnp.stack(d_ab_re), jnp.stack(d_ab_im), jnp.stack(d_bb_re), jnp.stack(d_bb_im)))
            grads["s5_lam_re"][j], grads["s5_lam_im"][j], grads["s5_log_dt"][j] = g_lre, g_lim, g_ldt[:, :, 0]
            grads["s5_b_re"][j] = jnp.transpose(g_bre, (0, 2, 3, 1))
            grads["s5_b_im"][j] = jnp.transpose(g_bim, (0, 2, 3, 1))
            grads["s5_c_re"][j], grads["s5_c_im"][j] = jnp.stack(d_c_re), jnp.stack(d_c_im)
            dx_parts, dh_parts = [dxa, dxb], dus
        else:
            bpw2 = vec(full["cv_b_pw2"][j])
            bwd = _vjp_fn(f_sub1_cv, 2, (1, 1), (0, 1, 2, 3, 4, 5, 6, 7))
            dxa, dmm, dg1, dsh2, dsc2, dbpw2, dgain0, dbias0 = _rowwise(
                f"sub1_bwd{i}", bwd, [sv["x"], sv["mm"], dx1, dh2], [g1, sh2, sc2], [bpw2, gain0, bias0],
                [(d, F32), (d, BF16)], [d, d, d], [d, d, d], cfg)
            grads["cv_b_pw2"][j] = dbpw2[0]
            wgrads["cv_w_pw2", j] = _mm_wgrad_rows(f"pw2_wgrad{i}", sv["s_act"], dmm, tm_big)
            ds = _mm_nt(f"pw2_dgrad{i}", dmm, weights["cv_w_pw2", j], 0, tm_big, d)
            bdw, lng, lnb = vec(full["cv_b_dw"][j]), vec(full["cv_ln_g"][j]), vec(full["cv_ln_b"][j])
            bwd = _vjp_fn(_f_cvln, 1, (1,), (0, 1, 2, 3))
            dcv, dbdw, dlng, dlnb = _rowwise(f"cvln_bwd{i}", bwd, [sv["cv"], ds], [], [bdw, lng, lnb], [(d, F32)], [], [d, d, d], cfg)
            grads["cv_b_dw"][j], grads["cv_ln_g"][j], grads["cv_ln_b"][j] = dbdw[0], dlng[0], dlnb[0]
            dact = _dwconv(f"dwconv_bwd{i}", dcv, sv["w_dw"][::-1], cfg)
            grads["cv_w_dw"][j] = _dwconv_wgrad(f"dwconv_wgrad{i}", sv["act"], dcv, sv["w_dw"].shape[0], cfg)
            bpw1 = vec(full["cv_b_pw1"][j])
            bwd = _vjp_fn(_f_cvglu, 1, (1,), (0, 1))
            dzz, dbpw1 = _rowwise(f"cvglu_bwd{i}", bwd, [sv["zz"], dact], [], [bpw1], [(2 * d, BF16)], [], [2 * d], cfg)
            grads["cv_b_pw1"][j] = dbpw1[0]
            wgrads["cv_w_pw1", j] = _mm_wgrad_cols(f"pw1_wgrad{i}", sv["h"], dzz, tm_big)
            dh = _mm_nt(f"pw1_dgrad{i}", dzz, weights["cv_w_pw1", j], 0, tm, d)
            dx_parts, dh_parts = [dxa], [dh]
        grads["ln_gain"][i] = jnp.stack([dgain0[0], dgain1[0]])
        grads["ln_bias"][i] = jnp.stack([dbias0[0], dbias1[0]])
        add_mod(i, 2, dg1)
        add_mod(i, 3, dsh2)
        add_mod(i, 4, dsc2)
    bwd = _vjp_fn(_f_entry, 2, (len(dx_parts), len(dh_parts)), (0, 2, 3))
    dxc, dsh1, dsc1 = _rowwise("entry_bwd", bwd, [xc, pos_rows] + dx_parts + dh_parts, [seg(0, 0), seg(0, 1)], [],
                               [(d, F32)], [d, d], [], cfg)
    add_mod(0, 0, dsh1)
    add_mod(0, 1, dsc1)
    grad_x = jnp.transpose(dxc[:nb * seq].reshape(tpl, nb, tr, d), (1, 0, 2, 3)).reshape(nb, seq, d)

    dmod_loc = jnp.stack([jnp.concatenate([q[:, 0] for q in dmod[i]], axis=1) for i in range(nl)])
    dmod_all = _exchange("gather_dmod", [dmod_loc.reshape(nl * (nb + 1), 6 * d)], ["slot"])[0].reshape(N_DEV, nl, nb + 1, 6 * d)
    mine = lax.dynamic_slice(dmod_all, (0, 0, 0, me * n_ada), (N_DEV, nl, nb + 1, n_ada))
    dmod_rows = jnp.transpose(mine[:, :, :nb], (1, 0, 2, 3)).reshape(nl, N_DEV * nb, n_ada)
    dmod_rows = jnp.concatenate([dmod_rows, jnp.zeros((nl, SUBLANES, n_ada), F32)], axis=1)
    g_w_ada, _, dcond = _ada_bwd(cc, a["w_ada"], dmod_rows, mine[:, :, nb:])
    g_b_ada = _sum_lead("b_ada_sum", jnp.transpose(dmod_all, (0, 2, 1, 3)).reshape(N_DEV * (nb + 1), nl, 6 * d), nl)

    small_names = SHARDED_SMALL + REPLICATED_SMALL
    small_full = [jnp.stack(grads[n]) for n in small_names]
    small_packed = _pack(small_full, N_DEV * SUBLANES)
    small_parts, dcond_all = _exchange("scatter_small_grads", [small_packed.reshape(N_DEV, -1, LANES), dcond[N_DEV * nb:N_DEV * nb + 1]],
                                       ["scatter", "slot"])
    g_c_ctx = _cctx_grad(dcond_all, a["c_ctx"][None])[0]
    small_part = _sum_lead("small_grad_sum", small_parts, _row_tile(small_parts.shape[1], 512))
    small_sum = _exchange("gather_small_sum", [small_part], ["slot"])[0]
    small_g = dict(zip(small_names, _unpack(small_sum, [g.shape for g in small_full])))
    for n in SHARDED_SMALL:
        width = a[n].shape[-1]
        start = (0,) * (small_g[n].ndim - 1) + (me * width,)
        small_g[n] = lax.dynamic_slice(small_g[n], start, a[n].shape)
    small_g["c_ctx"], small_g["b_ada"] = g_c_ctx, g_b_ada

    out = {}

    def update(n, parts):
        shp = a[n].shape
        cols = parts.shape[-1]
        rows = parts.shape[1]
        res = _adamw(f"adamw_{n}", parts, a[n].reshape(rows, cols), a["m_" + n].reshape(rows, cols), a["v_" + n].reshape(rows, cols),
                     _row_tile(rows, max(SUBLANES, 131072 // cols)))
        out[n] = [r.reshape(shp) for r in res]

    for n in BIG:
        rows, cols = a[n].shape[1:]
        bufs = [lax.empty(a[n].shape, F32) for _ in range(4)]
        for idx in range(a[n].shape[0]):
            bufs = _adamw_layer(f"adamw_{n}{idx}", received[n, idx], a[n], a["m_" + n], a["v_" + n], bufs, idx,
                                _row_tile(rows, max(SUBLANES, 131072 // cols)))
        out[n] = bufs
    update("w_ada", g_w_ada.reshape(1, -1, n_ada))
    for n in NATIVE_SMALL:
        out[n] = [small_g[n], *_adamw_native(f"adamw_{n}", small_g[n], a[n], a["m_" + n], a["v_" + n])]
    small_all_names = ("c_ctx", "b_ada") + tuple(n for n in small_names if n not in NATIVE_SMALL)
    packed = [_pack([src[n] for n in small_all_names]) for src in
              (small_g, a, {n: a["m_" + n] for n in small_all_names}, {n: a["v_" + n] for n in small_all_names})]
    res = _adamw("adamw_small", packed[0][None], packed[1], packed[2], packed[3], _row_tile(packed[0].shape[0], 512))
    shapes = [a[n].shape for n in small_all_names]
    for n, vals in zip(small_all_names, zip(*[_unpack(r, shapes) for r in res])):
        out[n] = list(vals)
    return (loss, grad_x, *[out[n][0] for n in WEIGHTS], *[out[n][1] for n in WEIGHTS],
            *[out[n][2] for n in WEIGHTS], *[out[n][3] for n in WEIGHTS])


def kernel(x, c, ctx, c_ctx, w_ada, b_ada, ln_gain, ln_bias, s5_lam_re, s5_lam_im, s5_log_dt, s5_b_re, s5_b_im, s5_c_re, s5_c_im, s5_d, s5_w_glu, s5_b_glu, cv_w_pw1, cv_b_pw1, cv_w_dw, cv_b_dw, cv_ln_g, cv_ln_b, cv_w_pw2, cv_b_pw2, mlp_w1, mlp_w2, loss_target, m_c_ctx, m_w_ada, m_b_ada, m_ln_gain, m_ln_bias, m_s5_lam_re, m_s5_lam_im, m_s5_log_dt, m_s5_b_re, m_s5_b_im, m_s5_c_re, m_s5_c_im, m_s5_d, m_s5_w_glu, m_s5_b_glu, m_cv_w_pw1, m_cv_b_pw1, m_cv_w_dw, m_cv_b_dw, m_cv_ln_g, m_cv_ln_b, m_cv_w_pw2, m_cv_b_pw2, m_mlp_w1, m_mlp_w2, v_c_ctx, v_w_ada, v_b_ada, v_ln_gain, v_ln_bias, v_s5_lam_re, v_s5_lam_im, v_s5_log_dt, v_s5_b_re, v_s5_b_im, v_s5_c_re, v_s5_c_im, v_s5_d, v_s5_w_glu, v_s5_b_glu, v_cv_w_pw1, v_cv_b_pw1, v_cv_w_dw, v_cv_b_dw, v_cv_ln_g, v_cv_ln_b, v_cv_w_pw2, v_cv_b_pw2, v_mlp_w1, v_mlp_w2):
    return _step(dict(locals()))
```

```python
import functools
import math

import jax
import jax.numpy as jnp
from jax import lax
from jax.experimental import pallas as pl
from jax.experimental.pallas import tpu as pltpu

F32 = jnp.float32
BF16 = jnp.bfloat16
N_DEV = 8
LANES = 128
SUBLANES = 8
VMEM_LIMIT = 56 * 1024 * 1024
GRID_W = 64
POS_TEMP = 10000.0
LN_EPS = 1e-5
LAMBDA_RE_MAX = -1e-4
ADAM_LR, ADAM_B1, ADAM_B2, ADAM_EPS, ADAM_WD, ADAM_STEP = 0.001, 0.9, 0.999, 1e-08, 0.01, 10
MESH = pl.DeviceIdType.MESH


def _params(sem):
    return pltpu.CompilerParams(dimension_semantics=sem, vmem_limit_bytes=VMEM_LIMIT)


def _accumulate(ref, val, first):
    @pl.when(first)
    def _():
        ref[...] = val

    @pl.when(jnp.logical_not(first))
    def _():
        ref[...] += val


def _rowwise(name, fn, rows, segs, vecs, row_outs, seg_accs, vec_accs, cfg, interleaved=()):
    tr, tpl, nb = cfg["tr"], cfg["tpl"], cfg["nb"]
    n_rows = rows[0].shape[0]
    nt = n_rows // tr
    nr, ns, nv = len(rows), len(segs), len(vecs)
    nro, nsa, nva = len(row_outs), len(seg_accs), len(vec_accs)

    def seg_of(t):
        return jnp.minimum(t // tpl, nb)

    def body(*refs):
        t = pl.program_id(0)
        ins, outs, mix_refs = refs[:nr + ns + nv], refs[nr + ns + nv:nr + ns + nv + nro + nsa + nva], refs[nr + ns + nv + nro + nsa + nva:]
        vals = [r[...] for r in ins[:nr]] + [r[0] for r in ins[nr:nr + ns]] + [r[...] for r in ins[nr + ns:]]
        res = fn(*vals)
        for idx, (o, v) in enumerate(zip(outs[:nro], res[:nro])):
            if idx in interleaved:
                mix_ref = mix_refs[interleaved.index(idx)]
                for p in range(mix_ref.shape[0]):
                    mix_ref[p] = v[:, p * LANES:(p + 1) * LANES].astype(F32)
                o[...] = jnp.concatenate([_interleave_rows(mix_ref.at[p], 1, o.dtype) for p in range(mix_ref.shape[0])], axis=1)
            else:
                o[...] = v.astype(o.dtype)
        first_seg = jnp.logical_or(t == 0, seg_of(t) != seg_of(jnp.maximum(t - 1, 0)))
        for o, v in zip(outs[nro:nro + nsa], res[nro:nro + nsa]):
            _accumulate(o.at[0], v, first_seg)
        for o, v in zip(outs[nro + nsa:], res[nro + nsa:]):
            _accumulate(o, v, t == 0)

    in_specs = ([pl.BlockSpec((tr, a.shape[1]), lambda t: (_phys_tile(t, cfg), 0)) for a in rows]
                + [pl.BlockSpec((1, 1, a.shape[2]), lambda t: (seg_of(t), 0, 0)) for a in segs]
                + [pl.BlockSpec((1, a.shape[1]), lambda t: (0, 0)) for a in vecs])
    out_specs = ([pl.BlockSpec((tr, c), lambda t: (_phys_tile(t, cfg), 0)) for c, _ in row_outs]
                 + [pl.BlockSpec((1, 1, c), lambda t: (seg_of(t), 0, 0)) for c in seg_accs]
                 + [pl.BlockSpec((1, c), lambda t: (0, 0)) for c in vec_accs])
    out_shape = ([jax.ShapeDtypeStruct((n_rows, c), dt) for c, dt in row_outs]
                 + [jax.ShapeDtypeStruct((nb + 1, 1, c), F32) for c in seg_accs]
                 + [jax.ShapeDtypeStruct((1, c), F32) for c in vec_accs])
    return pl.pallas_call(body, name=name, grid=(nt,), in_specs=in_specs, out_specs=out_specs, out_shape=out_shape,
                          scratch_shapes=[pltpu.VMEM((row_outs[idx][0] // LANES, tr, LANES), F32) for idx in interleaved],
                          compiler_params=_params(("arbitrary",)))(*rows, *segs, *vecs)


def _vjp_fn(fn, n_row, cot_groups, want):
    n_cot = sum(cot_groups)

    def bwd(*args):
        primals = [a.astype(F32) for a in args[:n_row] + args[n_row + n_cot:]]
        outs, vjp = jax.vjp(fn, *primals)
        cots, pos = [], n_row
        for n, o in zip(cot_groups, outs):
            cot = jnp.zeros_like(o)
            for part in args[pos:pos + n]:
                cot = cot + part.astype(F32)
            cots.append(cot)
            pos += n
        grads = vjp(tuple(cots))
        return tuple(grads[i] for i in want)
    return bwd


def _ln(r, g, b):
    mu = jnp.mean(r, axis=-1, keepdims=True)
    var = jnp.mean(jnp.square(r - mu), axis=-1, keepdims=True)
    return (r - mu) * lax.rsqrt(var + LN_EPS) * g + b


def _glu(zz, bias):
    d = zz.shape[1] // 2
    return (zz[:, :d] + bias[:, :d]) * jax.nn.sigmoid(zz[:, d:] + bias[:, d:])


def _f_entry(xc, pos, sh, sc):
    x0 = xc + pos
    return x0, x0 * (1 + sc) + sh


def _f_gelu(x, y0, y1, sh, sc, dsk):
    u = x * (1 + sc) + sh
    y = dsk * u + y0 + y1
    return (0.5 * y * (1.0 + lax.erf(y * (2.0 ** -0.5))),)


def _make_sub1_s5(alpha):
    def f(x, zz, g1, sh2, sc2, bglu, gain, bias):
        x1 = _ln(alpha * x + g1 * _glu(zz, bglu), gain, bias)
        return x1, x1 * (1 + sc2) + sh2
    return f


def _make_sub1_cv(alpha):
    def f(x, mm, g1, sh2, sc2, bpw2, gain, bias):
        x1 = _ln(alpha * x + g1 * (mm + bpw2), gain, bias)
        return x1, x1 * (1 + sc2) + sh2
    return f


def _make_sub2(alpha):
    def f(x1, m, g2, shn, scn, gain, bias):
        x2 = _ln(alpha * x1 + g2 * m, gain, bias)
        return x2, x2 * (1 + scn) + shn
    return f


def _f_cvglu(zz, bpw1):
    return (_glu(zz, bpw1),)


def _f_cvln(cv, bdw, lng, lnb):
    return (jax.nn.silu(_ln(cv + bdw, lng, lnb)),)


def _matmul(name, a, b, extras, grid, a_spec, b_spec, extra_specs, o_specs, out_shape, dims, red_axis, epi, sem):
    n_extra = len(extras)
    n_out = len(out_shape)
    acc_shape = o_specs[0].block_shape
    acc_shape = tuple(s for s in acc_shape if s is not None)

    def body(*refs):
        a_ref, b_ref = refs[0], refs[1]
        ex = refs[2:2 + n_extra]
        outs = refs[2 + n_extra:2 + n_extra + n_out]
        prod = lax.dot_general(a_ref[...], b_ref[...], dims, preferred_element_type=F32)

        def finish(acc):
            res = epi(acc, *[e[...] for e in ex]) if epi is not None else (acc,)
            for o, v in zip(outs, res):
                o[...] = v.astype(o.dtype)

        if red_axis is None:
            finish(prod)
        else:
            acc_ref = refs[-1]
            k = pl.program_id(red_axis)
            nk = pl.num_programs(red_axis)

            @pl.when(k == 0)
            def _():
                acc_ref[...] = prod

            @pl.when(k > 0)
            def _():
                acc_ref[...] += prod

            @pl.when(k == nk - 1)
            def _():
                finish(acc_ref[...])

    scratch = [] if red_axis is None else [pltpu.VMEM(acc_shape, F32)]
    res = pl.pallas_call(body, name=name, grid=grid, in_specs=[a_spec, b_spec] + list(extra_specs),
                         out_specs=list(o_specs), out_shape=list(out_shape), scratch_shapes=scratch,
                         compiler_params=_params(sem))(a, b, *extras)
    return res


NN = (((1,), (0,)), ((), ()))
NT = (((1,), (1,)), ((), ()))
TN = (((0,), (0,)), ((), ()))


def _mm_nn(name, a, w3, layer, tm, tn, out_dtypes=(F32,), epi=None):
    m, k = a.shape
    n = w3.shape[2]
    return _matmul(name, a, w3, (), (n // tn, m // tm),
                   pl.BlockSpec((tm, k), lambda j, i: (i, 0)), pl.BlockSpec((None, k, tn), lambda j, i: (layer, 0, j)), (),
                   [pl.BlockSpec((tm, tn), lambda j, i: (i, j)) for _ in out_dtypes],
                   [jax.ShapeDtypeStruct((m, n), dt) for dt in out_dtypes], NN, None, epi, ("arbitrary", "arbitrary"))


def _mm_nt(name, dy, w3, layer, tm, tkw, extras=(), out_dtype=F32, epi=None):
    m, n = dy.shape
    kw = w3.shape[1]
    return _matmul(name, dy, w3, tuple(extras), (kw // tkw, m // tm),
                   pl.BlockSpec((tm, n), lambda j, i: (i, 0)), pl.BlockSpec((None, tkw, n), lambda j, i: (layer, j, 0)),
                   [pl.BlockSpec((tm, tkw), lambda j, i: (i, j)) for _ in extras],
                   [pl.BlockSpec((tm, tkw), lambda j, i: (i, j))], [jax.ShapeDtypeStruct((m, kw), out_dtype)], NT, None, epi,
                   ("arbitrary", "arbitrary"))[0]


def _mm_wgrad_cols(name, a, dy, tm):
    m, k = a.shape
    n = dy.shape[1] // N_DEV
    return _matmul(name, a, dy, (), (N_DEV, m // tm),
                   pl.BlockSpec((tm, k), lambda j, i: (i, 0)), pl.BlockSpec((tm, n), lambda j, i: (i, j)), (),
                   [pl.BlockSpec((None, k, n), lambda j, i: (j, 0, 0))], [jax.ShapeDtypeStruct((N_DEV, k, n), BF16)],
                   TN, 1, None, ("arbitrary", "arbitrary"))[0]


def _mm_wgrad_rows(name, a, dy, tm):
    m = a.shape[0]
    r = a.shape[1] // N_DEV
    n = dy.shape[1]
    return _matmul(name, a, dy, (), (N_DEV, m // tm),
                   pl.BlockSpec((tm, r), lambda j, i: (i, j)), pl.BlockSpec((tm, n), lambda j, i: (i, 0)), (),
                   [pl.BlockSpec((None, r, n), lambda j, i: (j, 0, 0))], [jax.ShapeDtypeStruct((N_DEV, r, n), BF16)],
                   TN, 1, None, ("arbitrary", "arbitrary"))[0]


class _Copies:
    def __init__(self, arrays, kinds):
        self.arrays, self.kinds, self.n = list(arrays), list(kinds), len(arrays)
        any_spec = pl.BlockSpec(memory_space=pl.ANY)
        self.in_specs = [any_spec] * self.n
        self.out_specs = [any_spec] * self.n
        self.out_shape = [jax.ShapeDtypeStruct(self._result(a, kind), a.dtype) for a, kind in zip(arrays, kinds)]
        self.scratch = [pltpu.SemaphoreType.DMA((self.n, N_DEV - 1)), pltpu.SemaphoreType.DMA((self.n, N_DEV - 1)),
                        pltpu.SemaphoreType.DMA((self.n,))] if self.n else []

    @staticmethod
    def _result(a, kind):
        if kind == "slot":
            return (N_DEV,) + a.shape
        if kind == "scatter":
            return a.shape
        return a.shape[:kind] + (N_DEV * a.shape[kind],) + a.shape[kind + 1:]

    def descriptors(self, ins, outs, sems):
        send_sems, recv_sems, local_sems = sems
        x, y, c = lax.axis_index("x"), lax.axis_index("y"), lax.axis_index("c")
        me = 4 * x + 2 * y + c
        first, relay, finish = [], [], []

        def remote(i, k, src, dst, to):
            return pltpu.make_async_remote_copy(src_ref=src, dst_ref=dst, send_sem=send_sems.at[i, k], recv_sem=recv_sems.at[i, k],
                                                device_id=to, device_id_type=MESH)

        for i, kind in enumerate(self.kinds):
            if kind in ("slot", "scatter"):
                scatter = kind == "scatter"
                local = pltpu.make_async_copy(ins[i].at[me] if scatter else ins[i], outs[i].at[me], local_sems.at[i])
                first.append(local)
                finish.append((local, "all"))
                for k in range(1, N_DEV):
                    px = 1 - x if k & 4 else x
                    py = 1 - y if k & 2 else y
                    pc = 1 - c if k & 1 else c
                    cp = remote(i, k - 1, ins[i].at[4 * px + 2 * py + pc] if scatter else ins[i], outs[i].at[me], (px, py, pc))
                    first.append(cp)
                    finish.append((cp, "all"))
                continue
            size = ins[i].shape[kind]

            def block(px, py, pc):
                return outs[i].at[(slice(None),) * kind + (pl.ds(pl.multiple_of((4 * px + 2 * py + pc) * size, size), size),)]

            local = pltpu.make_async_copy(ins[i], block(x, y, c), local_sems.at[i])
            sibling = remote(i, 0, ins[i], block(x, y, c), (x, y, 1 - c))
            first += [local, sibling]
            finish += [(local, "all"), (sibling, "all")]
            for j, (qx, qy) in enumerate([(1 - x, y), (x, 1 - y), (1 - x, 1 - y)]):
                out = remote(i, 1 + j, ins[i], block(x, y, c), (qx, qy, c))
                onward = remote(i, 4 + j, block(qx, qy, c), block(qx, qy, c), (x, y, 1 - c))
                first.append(out)
                relay.append((out, onward))
                finish += [(out, "send"), (onward, "all")]
        return first, relay, finish


def _guarded(when, fn):
    if when is None:
        fn()
    else:
        pl.when(when)(fn)


def _start_all(plan, when=None):
    def run():
        for cp in plan[0]:
            cp.start()
    _guarded(when, run)


def _relay_all(plan, when=None):
    def run():
        for arrived, onward in plan[1]:
            arrived.wait_recv()
            onward.start()
    if plan[1]:
        _guarded(when, run)


def _wait_all(plan, when=None):
    def run():
        for cp, left in plan[2]:
            if left == "send":
                cp.wait_send()
            else:
                cp.wait()
    _guarded(when, run)


def _exchange(name, arrays, kinds):
    cps = _Copies(arrays, kinds)
    n = cps.n

    def body(*refs):
        plan = cps.descriptors(refs[:n], refs[n:2 * n], refs[2 * n:])
        _start_all(plan)
        _relay_all(plan)
        _wait_all(plan)

    return pl.pallas_call(body, name=name, in_specs=cps.in_specs, out_specs=cps.out_specs, out_shape=cps.out_shape,
                          scratch_shapes=cps.scratch)(*arrays)


def _carried(cps, n_in, n_out, n_scratch, refs):
    if cps is None:
        return ([], [], []), refs
    n = cps.n
    ins = refs[n_in:n_in + n]
    outs = refs[n_in + n + n_out:n_in + n + n_out + n]
    sems = refs[n_in + n + n_out + n + n_scratch:]
    own = refs[:n_in] + refs[n_in + n:n_in + n + n_out] + refs[n_in + n + n_out + n:n_in + n + n_out + n + n_scratch]
    return cps.descriptors(ins, outs, sems), own


def _sum_lead(name, parts, tr):
    npart, r, c = parts.shape

    def body(p_ref, o_ref):
        acc = p_ref[0].astype(F32)
        for p in range(1, npart):
            acc = acc + p_ref[p].astype(F32)
        o_ref[...] = acc

    return pl.pallas_call(body, name=name, grid=(r // tr,), in_specs=[pl.BlockSpec((npart, tr, c), lambda i: (0, i, 0))],
                          out_specs=pl.BlockSpec((tr, c), lambda i: (i, 0)), out_shape=jax.ShapeDtypeStruct((r, c), F32),
                          compiler_params=_params(("arbitrary",)))(parts)


def _adamw_math(g, w, m, v):
    m2 = ADAM_B1 * m + (1.0 - ADAM_B1) * g
    v2 = ADAM_B2 * v + (1.0 - ADAM_B2) * jnp.square(g)
    m_hat = m2 / (1.0 - ADAM_B1 ** ADAM_STEP)
    v_hat = v2 / (1.0 - ADAM_B2 ** ADAM_STEP)
    return -ADAM_LR * (m_hat / (jnp.sqrt(v_hat) + ADAM_EPS) + ADAM_WD * w), m2, v2


def _adamw_body(npart):
    def body(p_ref, w_ref, m_ref, v_ref, *rest):
        g_out, d_out, m_out, v_out = rest[-4:]
        g = p_ref[0].astype(F32)
        for p in range(1, npart):
            g = g + p_ref[p].astype(F32)
        g_out[...] = g
        d_out[...], m_out[...], v_out[...] = _adamw_math(g, w_ref[...], m_ref[...], v_ref[...])
    return body


def _adamw_native(name, g, w, m, v):
    rest = w.shape[2:]
    spec = pl.BlockSpec((None, None) + rest, lambda i, j: (i, j) + (0,) * len(rest))

    def body(g_ref, w_ref, m_ref, v_ref, d_out, m_out, v_out):
        d_out[...], m_out[...], v_out[...] = _adamw_math(g_ref[...], w_ref[...], m_ref[...], v_ref[...])

    return pl.pallas_call(body, name=name, grid=w.shape[:2], in_specs=[spec] * 4, out_specs=[spec] * 3,
                          out_shape=[jax.ShapeDtypeStruct(w.shape, F32)] * 3,
                          compiler_params=_params(("arbitrary", "arbitrary")))(g, w, m, v)


def _adamw(name, parts, w, m, v, tr):
    npart, r, c = parts.shape
    row = pl.BlockSpec((tr, c), lambda i: (i, 0))
    return pl.pallas_call(_adamw_body(npart), name=name, grid=(r // tr,),
                          in_specs=[pl.BlockSpec((npart, tr, c), lambda i: (0, i, 0)), row, row, row],
                          out_specs=[row] * 4, out_shape=[jax.ShapeDtypeStruct((r, c), F32)] * 4,
                          compiler_params=_params(("arbitrary",)))(parts, w, m, v)


def _adamw_layer(name, parts, w3, m3, v3, bufs, layer, tr):
    npart, r, c = parts.shape
    lay = pl.BlockSpec((None, tr, c), lambda i: (layer, i, 0))
    hbm = pl.BlockSpec(memory_space=pl.ANY)
    return pl.pallas_call(_adamw_body(npart), name=name, grid=(r // tr,),
                          in_specs=[pl.BlockSpec((npart, tr, c), lambda i: (0, i, 0)), lay, lay, lay] + [hbm] * 4,
                          out_specs=[lay] * 4, out_shape=[jax.ShapeDtypeStruct(w3.shape, F32)] * 4,
                          input_output_aliases={4: 0, 5: 1, 6: 2, 7: 3},
                          compiler_params=_params(("arbitrary",)))(parts, w3, m3, v3, *bufs)


def _row_tile(r, cap):
    if r <= cap:
        return r
    t = cap - cap % SUBLANES
    while r % t:
        t -= SUBLANES
    return t


def _ada_fwd(cc, w_ada, b_loc):
    nl, d, n = w_ada.shape
    rows = cc.shape[0]

    def body(c_ref, w_ref, b_ref, o_ref):
        cond = jax.nn.silu(c_ref[...]).astype(BF16)
        o_ref[...] = jnp.dot(cond, w_ref[...].astype(BF16), preferred_element_type=F32) + b_ref[...]

    return pl.pallas_call(body, name="ada_fwd", grid=(nl,),
                          in_specs=[pl.BlockSpec((rows, d), lambda i: (0, 0)), pl.BlockSpec((None, d, n), lambda i: (i, 0, 0)),
                                    pl.BlockSpec((None, 1, n), lambda i: (i, 0, 0))],
                          out_specs=pl.BlockSpec((None, rows, n), lambda i: (i, 0, 0)),
                          out_shape=jax.ShapeDtypeStruct((nl, rows, n), F32), compiler_params=_params(("arbitrary",)))(cc, w_ada, b_loc)


def _ada_bwd(cc, w_ada, dmod_rows, dmod_ctx):
    nl, d, n = w_ada.shape
    rows = cc.shape[0]
    ctx_row = rows - SUBLANES

    def body(c_ref, w_ref, dr_ref, dc_ref, gw_ref, tot_ref, dcond_ref):
        i = pl.program_id(0)
        total = dc_ref[0]
        for p in range(1, N_DEV):
            total = total + dc_ref[p]
        tot_ref[...] = total
        row_id = lax.broadcasted_iota(jnp.int32, (rows, n), 0)
        dm = jnp.where(row_id == ctx_row, jnp.broadcast_to(total, (rows, n)), dr_ref[...]).astype(BF16)
        cond = jax.nn.silu(c_ref[...]).astype(BF16)
        gw_ref[...] = lax.dot_general(cond, dm, TN, preferred_element_type=F32)
        part = lax.dot_general(dm, w_ref[...].astype(BF16), NT, preferred_element_type=F32)
        _accumulate(dcond_ref, part, i == 0)

    return pl.pallas_call(body, name="ada_bwd", grid=(nl,),
                          in_specs=[pl.BlockSpec((rows, d), lambda i: (0, 0)), pl.BlockSpec((None, d, n), lambda i: (i, 0, 0)),
                                    pl.BlockSpec((None, rows, n), lambda i: (i, 0, 0)),
                                    pl.BlockSpec((N_DEV, None, 1, n), lambda i: (0, i, 0, 0))],
                          out_specs=[pl.BlockSpec((None, d, n), lambda i: (i, 0, 0)), pl.BlockSpec((None, 1, n), lambda i: (i, 0, 0)),
                                     pl.BlockSpec((rows, d), lambda i: (0, 0))],
                          out_shape=[jax.ShapeDtypeStruct((nl, d, n), F32), jax.ShapeDtypeStruct((nl, 1, n), F32),
                                     jax.ShapeDtypeStruct((rows, d), F32)],
                          compiler_params=_params(("arbitrary",)))(cc, w_ada, dmod_rows, dmod_ctx)


def _cctx_grad(parts, c_ctx):
    def body(p_ref, c_ref, o_ref):
        tot = p_ref[0]
        for p in range(1, N_DEV):
            tot = tot + p_ref[p]
        _, vjp = jax.vjp(jax.nn.silu, c_ref[...])
        o_ref[...] = vjp(tot)[0]

    return pl.pallas_call(body, name="cctx_grad", out_shape=jax.ShapeDtypeStruct(c_ctx.shape, F32))(parts, c_ctx)


def _discretise(lam_re, lam_im, log_dt, b_re, b_im):
    lr = jnp.minimum(lam_re, LAMBDA_RE_MAX)
    li = lam_im
    dt = jnp.exp(log_dt)
    mag = jnp.exp(lr * dt)
    ab_re = mag * jnp.cos(li * dt)
    ab_im = mag * jnp.sin(li * dt)
    den = lr * lr + li * li
    nr = ab_re - 1.0
    ni = ab_im
    coef_re = ((nr * lr + ni * li) / den)[:, None]
    coef_im = ((ni * lr - nr * li) / den)[:, None]
    bb_re = coef_re * b_re - coef_im * b_im
    bb_im = coef_re * b_im + coef_im * b_re
    return ab_re, ab_im, bb_re, bb_im


def _s5_prep(name, lam_re, lam_im, log_dt, b_re, b_im):
    def body(a, b, c, d, e, o1, o2, o3, o4):
        res = _discretise(a[...], b[...], c[...], d[...], e[...])
        for o, v in zip((o1, o2, o3, o4), res):
            o[...] = v

    shp = [jax.ShapeDtypeStruct(lam_re.shape, F32)] * 2 + [jax.ShapeDtypeStruct(b_re.shape, F32)] * 2
    return pl.pallas_call(body, name=name, out_shape=shp)(lam_re, lam_im, log_dt, b_re, b_im)


def _s5_prep_bwd(name, lam_re, lam_im, log_dt, b_re, b_im, cots):
    def body(a, b, c, d, e, c1, c2, c3, c4, o1, o2, o3, o4, o5):
        _, vjp = jax.vjp(_discretise, a[...], b[...], c[...], d[...], e[...])
        grads = vjp((c1[...], c2[...], c3[...], c4[...]))
        for o, v in zip((o1, o2, o3, o4, o5), grads):
            o[...] = v

    shp = [jax.ShapeDtypeStruct(a.shape, F32) for a in (lam_re, lam_im, log_dt, b_re, b_im)]
    return pl.pallas_call(body, name=name, out_shape=shp)(lam_re, lam_im, log_dt, b_re, b_im, *cots)


def _interleave_rows(ref, n_seq, dtype):
    n_j = ref.shape[0] // (SUBLANES * n_seq)
    return jnp.concatenate([ref[pl.ds(q * SUBLANES * n_j + j, SUBLANES, stride=n_j), :] for q in range(n_seq) for j in range(n_j)],
                           axis=0).astype(dtype)


def _store_tokens(out_ref, ref, n_seq):
    n_j = ref.shape[0] // (SUBLANES * n_seq)
    for q in range(n_seq):
        for s in range(SUBLANES):
            start = (q * SUBLANES + s) * n_j
            out_ref[start:start + n_j, :] = ref[pl.ds(q * SUBLANES * n_j + s, n_j, stride=SUBLANES), :].astype(out_ref.dtype)


def _expand_powers(t_ref, pow_ref):
    for j in range(pow_ref.shape[0] // SUBLANES):
        row = 5 * SUBLANES + j
        pow_ref[j * SUBLANES:(j + 1) * SUBLANES, :] = jnp.broadcast_to(t_ref[row:row + 1, :], (SUBLANES, pow_ref.shape[1]))


def _scan_tile(h_ref, t_ref, pow_ref, carry_ref, up, n_seq, states_ref=None):
    sw = h_ref.shape[1] // 2
    n_j = h_ref.shape[0] // (SUBLANES * n_seq)
    seqs = range(n_seq)

    def rows(g):
        if isinstance(g, int):
            return pl.ds(g * SUBLANES, SUBLANES)
        return pl.ds(pl.multiple_of(g * SUBLANES, SUBLANES), SUBLANES)

    def at(q, j):
        return rows(q * n_j + j)

    def tab(g):
        return t_ref[rows(g), :sw], t_ref[rows(g), sw:]

    def order(i):
        return n_j - 1 - i if up else i

    def cmul_add(xr, xi, ar, ai, yr, yi):
        return xr + ar * yr - ai * yi, xi + ar * yi + ai * yr

    a_re, a_im = tab(0)

    def local_step(i, xs):
        j = order(i)
        out = []
        for q in seqs:
            xr, xi = cmul_add(h_ref[at(q, j), :sw], h_ref[at(q, j), sw:], a_re, a_im, *xs[q])
            h_ref[at(q, j), :sw] = xr
            h_ref[at(q, j), sw:] = xi
            out.append((xr, xi))
        return tuple(out)

    zero = jnp.zeros((SUBLANES, sw), F32)
    ends = lax.fori_loop(0, n_j, local_step, tuple((zero, zero) for _ in seqs))
    out_row = 0 if up else SUBLANES - 1
    in_row = SUBLANES - 1 if up else 0
    one = SUBLANES - 1 if up else 1
    is_in = lax.broadcasted_iota(jnp.int32, (SUBLANES, sw), 0) == in_row
    carried, enters = [], []
    for q in seqs:
        dr, di = ends[q]
        for level, sh in enumerate((1, 2, 4)):
            amount = SUBLANES - sh if up else sh
            dr, di = cmul_add(dr, di, *tab(1 + level), pltpu.roll(dr, amount, 0), pltpu.roll(di, amount, 0))
        cr, ci = carry_ref[rows(q), :sw], carry_ref[rows(q), sw:]
        dr, di = cmul_add(dr, di, *tab(4), cr, ci)
        carry_ref[rows(q), :sw] = jnp.broadcast_to(dr[out_row:out_row + 1], dr.shape)
        carry_ref[rows(q), sw:] = jnp.broadcast_to(di[out_row:out_row + 1], di.shape)
        carried.append((cr, ci))
        enters.append((jnp.where(is_in, cr, pltpu.roll(dr, one, 0)), jnp.where(is_in, ci, pltpu.roll(di, one, 0))))

    def fix_step(i, state):
        j = order(i)
        nows = []
        for q in seqs:
            xr, xi = cmul_add(h_ref[at(q, j), :sw], h_ref[at(q, j), sw:], pow_ref[rows(j), :sw], pow_ref[rows(j), sw:], *enters[q])
            h_ref[at(q, j), :sw] = xr
            h_ref[at(q, j), sw:] = xi
            nows.append((xr, xi))
        if states_ref is None:
            return state
        befores, (acc_r, acc_i) = state
        for q in seqs:
            lr, li = befores[q]
            hr, hi = states_ref[at(q, j), :sw], states_ref[at(q, j), sw:]
            acc_r, acc_i = acc_r + lr * hr + li * hi, acc_i + li * hr - lr * hi
        return tuple(nows), (acc_r, acc_i)

    if states_ref is None:
        lax.fori_loop(0, n_j, fix_step, 0)
        return None
    lasts, (acc_r, acc_i) = lax.fori_loop(0, n_j, fix_step, (tuple((zero, zero) for _ in seqs), (zero, zero)))
    for q in seqs:
        lr = jnp.where(is_in, carried[q][0], pltpu.roll(lasts[q][0], one, 0))
        li = jnp.where(is_in, carried[q][1], pltpu.roll(lasts[q][1], one, 0))
        hr, hi = states_ref[at(q, order(0)), :sw], states_ref[at(q, order(0)), sw:]
        acc_r, acc_i = acc_r + lr * hr + li * hi, acc_i + li * hr - lr * hi
    return acc_r, acc_i


def _phys_tile(t, cfg):
    tpl, nb = cfg["tpl"], cfg["nb"]
    return jnp.where(t < nb * tpl, (t % tpl) * nb + t // tpl, t)


def _s5_block_index(cfg, dirn, adjoint):
    tpl = cfg["tpl"]

    def idx(k):
        if not adjoint:
            return jnp.where(k == 0, tpl, k - 1 if dirn == 0 else tpl - k)
        return jnp.where(k == tpl, tpl, tpl - 1 - k if dirn == 0 else k)
    return idx


RELAY_AT = 0.65


def _grid_ends(grid):
    step = 0
    for i, n in enumerate(grid):
        step = step * n + pl.program_id(i)
    total = math.prod(grid)
    return step == 0, step == int(RELAY_AT * total), step == total - 1


def _s5_fwd(name, u, bmat, cmat, tab, dirn, cfg, cps=None):
    tr, tpl, nb = cfg["tr"], cfg["tpl"], cfg["nb"]
    n_rows, d = u.shape
    ns, _, sw2 = bmat.shape
    block = _s5_block_index(cfg, dirn, False)
    up = dirn == 1
    grid = (ns, tpl + 1)
    br = nb * tr

    def body(*refs):
        copies, (u_ref, b_ref, c_ref, t_ref, h_ref, y_ref, carry_ref, mix_ref, pow_ref) = _carried(cps, 4, 2, 3, refs)
        first, middle, last = _grid_ends(grid)
        _start_all(copies, first)
        _relay_all(copies, middle)

        @pl.when(pl.program_id(1) == 0)
        def _():
            carry_ref[...] = jnp.zeros_like(carry_ref)
            _expand_powers(t_ref, pow_ref)

        h_ref[...] = jnp.dot(u_ref[...], b_ref[...], preferred_element_type=F32)
        _scan_tile(h_ref, t_ref, pow_ref, carry_ref, up, nb)
        mix_ref[...] = jnp.dot(h_ref[...].astype(BF16), c_ref[...], preferred_element_type=F32)
        _store_tokens(y_ref, mix_ref, nb)
        _wait_all(copies, last)

    extra = cps if cps is not None else _Copies([], [])
    return pl.pallas_call(
        body, name=name, grid=grid,
        in_specs=[pl.BlockSpec((br, LANES), lambda s, k: (block(k), s)),
                  pl.BlockSpec((None, LANES, sw2), lambda s, k: (s, 0, 0)),
                  pl.BlockSpec((None, sw2, LANES), lambda s, k: (s, 0, 0)),
                  pl.BlockSpec((None, tab.shape[1], sw2), lambda s, k: (s, 0, 0))] + extra.in_specs,
        out_specs=[pl.BlockSpec((br, sw2), lambda s, k: (block(k), s)),
                   pl.BlockSpec((br, LANES), lambda s, k: (block(k), s))] + extra.out_specs,
        out_shape=[jax.ShapeDtypeStruct((n_rows, ns * sw2), F32), jax.ShapeDtypeStruct((n_rows, d), F32)] + extra.out_shape,
        scratch_shapes=[pltpu.VMEM((nb * SUBLANES, sw2), F32), pltpu.VMEM((br, LANES), F32), pltpu.VMEM((tr, sw2), F32)] + extra.scratch,
        compiler_params=_params(("arbitrary", "arbitrary")))(u, bmat, cmat, tab, *extra.arrays)


def _s5_bwd(name, dy, h, u, cmat_t, bmat_t, tab, dirn, cfg, cps=None):
    tr, tpl, nb = cfg["tr"], cfg["tpl"], cfg["nb"]
    n_rows, d = u.shape
    ns, _, sw2 = cmat_t.shape
    sw = sw2 // 2
    block = _s5_block_index(cfg, dirn, True)
    up = dirn == 0
    grid = (ns, tpl + 1)
    br = nb * tr

    def body(*refs):
        copies, own = _carried(cps, 6, 4, 4, refs)
        dy_ref, h_ref, u_ref, ct_ref, bt_ref, t_ref, du_ref, db_ref, dc_ref, da_ref, lam_ref, carry_ref, mix_ref, pow_ref = own
        grid_first, grid_middle, grid_last = _grid_ends(grid)
        _start_all(copies, grid_first)
        _relay_all(copies, grid_middle)
        first = pl.program_id(1) == 0

        @pl.when(first)
        def _():
            carry_ref[...] = jnp.zeros_like(carry_ref)
            _expand_powers(t_ref, pow_ref)

        dy = dy_ref[...]
        u_mixed = u_ref[...]
        lam_ref[...] = jnp.dot(dy, ct_ref[...], preferred_element_type=F32)
        acc = _scan_tile(lam_ref, t_ref, pow_ref, carry_ref, up, nb, h_ref)
        lam = lam_ref[...].astype(BF16)
        d_b = lax.dot_general(u_mixed, lam, TN, preferred_element_type=F32)
        d_c = lax.dot_general(h_ref[...].astype(BF16), dy, TN, preferred_element_type=F32)
        mix_ref[...] = jnp.dot(lam, bt_ref[...], preferred_element_type=F32)
        _store_tokens(du_ref, mix_ref, nb)

        @pl.when(first)
        def _():
            db_ref[...] = d_b
            dc_ref[...] = d_c
            da_ref[:, :sw] = acc[0]
            da_ref[:, sw:] = acc[1]

        @pl.when(jnp.logical_not(first))
        def _():
            db_ref[...] += d_b
            dc_ref[...] += d_c
            da_ref[:, :sw] += acc[0]
            da_ref[:, sw:] += acc[1]

        _wait_all(copies, grid_last)

    extra = cps if cps is not None else _Copies([], [])
    return pl.pallas_call(
        body, name=name, grid=grid,
        in_specs=[pl.BlockSpec((br, LANES), lambda s, k: (block(k), s)),
                  pl.BlockSpec((br, sw2), lambda s, k: (block(k), s)),
                  pl.BlockSpec((br, LANES), lambda s, k: (block(k), s)),
                  pl.BlockSpec((None, LANES, sw2), lambda s, k: (s, 0, 0)),
                  pl.BlockSpec((None, sw2, LANES), lambda s, k: (s, 0, 0)),
                  pl.BlockSpec((None, tab.shape[1], sw2), lambda s, k: (s, 0, 0))] + extra.in_specs,
        out_specs=[pl.BlockSpec((br, LANES), lambda s, k: (block(k), s)),
                   pl.BlockSpec((None, LANES, sw2), lambda s, k: (s, 0, 0)),
                   pl.BlockSpec((None, sw2, LANES), lambda s, k: (s, 0, 0)),
                   pl.BlockSpec((None, SUBLANES, sw2), lambda s, k: (s, 0, 0))] + extra.out_specs,
        out_shape=[jax.ShapeDtypeStruct((n_rows, d), F32), jax.ShapeDtypeStruct((ns, LANES, sw2), F32),
                   jax.ShapeDtypeStruct((ns, sw2, LANES), F32), jax.ShapeDtypeStruct((ns, SUBLANES, sw2), F32)] + extra.out_shape,
        scratch_shapes=[pltpu.VMEM((br, sw2), F32), pltpu.VMEM((nb * SUBLANES, sw2), F32), pltpu.VMEM((br, LANES), F32),
                        pltpu.VMEM((tr, sw2), F32)] + extra.scratch,
        compiler_params=_params(("arbitrary", "arbitrary")))(dy, h, u, cmat_t, bmat_t, tab, *extra.arrays)


def _s5_tables(ab_re, ab_im, up, conj, ns, n_j):
    def powers_of(base, count):
        out = [base]
        for _ in range(count - 1):
            q_re, q_im = out[-1]
            out.append((q_re * base[0] - q_im * base[1], q_re * base[1] + q_im * base[0]))
        return out

    def spread(q):
        return jnp.broadcast_to(q[:, None, :], (q.shape[0], SUBLANES, q.shape[1]))

    steps = powers_of((ab_re.reshape(ns, -1), (-ab_im if conj else ab_im).reshape(ns, -1)), n_j)
    jumps = powers_of(steps[-1], SUBLANES)
    rows = jnp.arange(SUBLANES)
    blocks = [tuple(spread(q) for q in steps[0])]
    for sh in (1, 2, 4):
        keep = ((rows <= SUBLANES - 1 - sh) if up else (rows >= sh))[None, :, None]
        blocks.append(tuple(jnp.where(keep, q[:, None, :], 0.0) for q in jumps[sh - 1]))
    dist = range(SUBLANES, 0, -1) if up else range(1, SUBLANES + 1)
    blocks.append(tuple(jnp.stack([jumps[dd - 1][part] for dd in dist], axis=1) for part in (0, 1)))
    ordered = steps[::-1] if up else steps
    blocks.append(tuple(jnp.stack([q[part] for q in ordered], axis=1) for part in (0, 1)))
    return jnp.concatenate([jnp.concatenate([b[0] for b in blocks], axis=1), jnp.concatenate([b[1] for b in blocks], axis=1)], axis=2)


def _block_diag(blocks):
    ns, gs, a, b = blocks.shape
    eye = jnp.eye(gs, dtype=blocks.dtype)
    return (blocks[:, :, :, None, :] * eye[None, :, None, :, None]).reshape(ns, gs * a, gs * b)


def _diag_blocks(mat, gs):
    ns, ra, rb = mat.shape
    a, b = ra // gs, rb // gs
    m5 = mat.reshape(ns, gs, a, gs, b)
    eye = jnp.eye(gs, dtype=mat.dtype)
    return jnp.sum(m5 * eye[None, :, None, :, None], axis=3)


def _conv_flags(t, cfg):
    tpl, nb = cfg["tpl"], cfg["nb"]
    latent = t < nb * tpl
    first = jnp.logical_or(jnp.logical_not(latent), t % tpl == 0)
    last = jnp.logical_or(jnp.logical_not(latent), t % tpl == tpl - 1)
    return first, last


def _fill_ext(ext_ref, prev_ref, cur_ref, next_ref, t, cfg, halo):
    first, last = _conv_flags(t, cfg)
    tr = cur_ref.shape[0]
    for p in range(ext_ref.shape[0]):
        lanes = slice(p * LANES, (p + 1) * LANES)
        ext_ref[p, 0:halo, :] = jnp.where(first, 0.0, prev_ref[:, lanes])
        ext_ref[p, halo:halo + tr, :] = cur_ref[:, lanes]
        ext_ref[p, halo + tr:, :] = jnp.where(last, 0.0, next_ref[:, lanes])


CONV_LANES = 4 * LANES


def _conv_specs(tr, n_rows, halo, cw, cfg):
    per = tr // halo
    n_halo = n_rows // halo
    nb = cfg["nb"]
    return [pl.BlockSpec((halo, cw), lambda c, t: (jnp.maximum((_phys_tile(t, cfg) - nb + 1) * per - 1, 0), c)),
            pl.BlockSpec((tr, cw), lambda c, t: (_phys_tile(t, cfg), c)),
            pl.BlockSpec((halo, cw), lambda c, t: (jnp.minimum((_phys_tile(t, cfg) + nb) * per, n_halo - 1), c))]


def _dwconv(name, a, w, cfg):
    tr = cfg["tr"]
    n_rows, d = a.shape
    kw = w.shape[0]
    half = kw // 2
    halo = 2 * SUBLANES
    cw = min(d, CONV_LANES)

    def body(prev_ref, cur_ref, next_ref, w_ref, o_ref, ext_ref):
        _fill_ext(ext_ref, prev_ref, cur_ref, next_ref, pl.program_id(1), cfg, halo)
        for p in range(cw // LANES):
            lanes = slice(p * LANES, (p + 1) * LANES)
            acc = jnp.zeros((tr, LANES), F32)
            for k in range(kw):
                acc = acc + ext_ref[p, pl.ds(halo - half + k, tr), :] * w_ref[k:k + 1, lanes]
            o_ref[:, lanes] = acc

    return pl.pallas_call(body, name=name, grid=(d // cw, n_rows // tr),
                          in_specs=_conv_specs(tr, n_rows, halo, cw, cfg) + [pl.BlockSpec((kw, cw), lambda c, t: (0, c))],
                          out_specs=pl.BlockSpec((tr, cw), lambda c, t: (_phys_tile(t, cfg), c)),
                          out_shape=jax.ShapeDtypeStruct((n_rows, d), F32),
                          scratch_shapes=[pltpu.VMEM((cw // LANES, tr + 2 * halo, LANES), F32)],
                          compiler_params=_params(("arbitrary", "arbitrary")))(a, a, a, w)


def _dwconv_wgrad(name, a, dout, kw, cfg):
    tr = cfg["tr"]
    n_rows, d = a.shape
    half = kw // 2
    halo = 2 * SUBLANES
    cw = min(d, CONV_LANES)

    def body(prev_ref, cur_ref, next_ref, do_ref, o_ref, ext_ref):
        t = pl.program_id(1)
        _fill_ext(ext_ref, prev_ref, cur_ref, next_ref, t, cfg, halo)
        for p in range(cw // LANES):
            lanes = slice(p * LANES, (p + 1) * LANES)
            dout_t = do_ref[:, lanes]
            rows = [jnp.sum(ext_ref[p, pl.ds(halo - half + k, tr), :] * dout_t, axis=0, keepdims=True) for k in range(kw)]
            _accumulate(o_ref.at[:, lanes], jnp.concatenate(rows, axis=0), t == 0)

    return pl.pallas_call(body, name=name, grid=(d // cw, n_rows // tr),
                          in_specs=_conv_specs(tr, n_rows, halo, cw, cfg) + [pl.BlockSpec((tr, cw), lambda c, t: (_phys_tile(t, cfg), c))],
                          out_specs=pl.BlockSpec((kw, cw), lambda c, t: (0, c)),
                          out_shape=jax.ShapeDtypeStruct((kw, d), F32),
                          scratch_shapes=[pltpu.VMEM((cw // LANES, tr + 2 * halo, LANES), F32)],
                          compiler_params=_params(("arbitrary", "arbitrary")))(a, a, a, dout)


def _sincos_1d(pos, dim):
    quarter = dim // 2
    omega = POS_TEMP ** (-jnp.arange(quarter, dtype=F32) / quarter)
    ang = pos[:, None] * omega[None, :]
    return jnp.concatenate([jnp.sin(ang), jnp.cos(ang)], axis=-1)


def _grid_pos_embed(rows, dim):
    row_idx = jnp.repeat(jnp.arange(rows), GRID_W).astype(F32)
    col_idx = jnp.tile(jnp.arange(GRID_W), rows).astype(F32)
    return jnp.concatenate([_sincos_1d(row_idx, dim // 2), _sincos_1d(col_idx, dim // 2)], axis=-1)


def _pack(arrs, row_multiple=SUBLANES):
    flat = jnp.concatenate([a.reshape(-1).astype(F32) for a in arrs])
    pad = (-flat.shape[0]) % (row_multiple * LANES)
    return jnp.pad(flat, (0, pad)).reshape(-1, LANES)


def _unpack(buf, shapes):
    flat = buf.reshape(-1)
    out, pos = [], 0
    for shp in shapes:
        n = math.prod(shp)
        out.append(flat[pos:pos + n].reshape(shp))
        pos += n
    return out


def _unpack_gathered(buf, shapes):
    flat = buf.reshape(N_DEV, -1)
    out, pos = [], 0
    for shp in shapes:
        n = math.prod(shp)
        part = flat[:, pos:pos + n].reshape((N_DEV,) + tuple(shp))
        out.append(jnp.moveaxis(part, 0, -2).reshape(tuple(shp[:-1]) + (N_DEV * shp[-1],)))
        pos += n
    return out


WEIGHTS = ("c_ctx", "w_ada", "b_ada", "ln_gain", "ln_bias", "s5_lam_re", "s5_lam_im", "s5_log_dt", "s5_b_re", "s5_b_im",
           "s5_c_re", "s5_c_im", "s5_d", "s5_w_glu", "s5_b_glu", "cv_w_pw1", "cv_b_pw1", "cv_w_dw", "cv_b_dw", "cv_ln_g",
           "cv_ln_b", "cv_w_pw2", "cv_b_pw2", "mlp_w1", "mlp_w2")
SHARDED_SMALL = ("ln_gain", "ln_bias", "cv_b_pw1", "cv_w_dw", "cv_b_dw", "cv_ln_g", "cv_ln_b", "cv_b_pw2")
REPLICATED_SMALL = ("s5_lam_re", "s5_lam_im", "s5_log_dt", "s5_b_re", "s5_b_im", "s5_c_re", "s5_c_im", "s5_d", "s5_b_glu")
NATIVE_SMALL = ("s5_lam_re", "s5_lam_im", "s5_b_re", "s5_b_im", "s5_c_re", "s5_c_im")
BIG = ("mlp_w1", "mlp_w2", "s5_w_glu", "cv_w_pw1", "cv_w_pw2")


def _step(a):
    x, c, ctx = a["x"], a["c"], a["ctx"]
    nb, seq, d = x.shape
    lc = ctx.shape[1]
    nl = a["w_ada"].shape[0]
    tr = lc
    tpl = seq // tr
    cfg = {"tr": tr, "tpl": tpl, "nb": nb}
    n_rows = nb * (seq + lc)
    alpha = (2.0 * nl) ** 0.25
    me = 4 * lax.axis_index("x") + 2 * lax.axis_index("y") + lax.axis_index("c")
    n_grp, n_state = a["s5_lam_re"].shape[2:]
    ch = a["s5_b_re"].shape[-1]
    gs = LANES // ch
    ns = d // LANES
    tm = 2 * tr if n_rows % (2 * tr) == 0 else tr
    tm_big = n_rows // 3 if n_rows % (3 * 2 * SUBLANES) == 0 else tm
    f_sub1_s5, f_sub1_cv, f_sub2 = _make_sub1_s5(alpha), _make_sub1_cv(alpha), _make_sub2(alpha)

    def layer_weights(i):
        mixer = [("s5_w_glu", i // 2, 1)] if i % 2 == 0 else [("cv_w_pw1", i // 2, 1), ("cv_w_pw2", i // 2, 0)]
        return mixer + [("mlp_w1", i, 1), ("mlp_w2", i, 0)]

    weights, wgrads, received = {}, {}, {}
    small_all, c_all = _exchange("gather_small", [_pack([a[n] for n in SHARDED_SMALL]), c], ["slot", "slot"])
    full = dict(zip(SHARDED_SMALL, _unpack_gathered(small_all, [a[n].shape for n in SHARDED_SMALL])))
    c_all = c_all.reshape(N_DEV * nb, d)
    cond_rows = N_DEV * nb + SUBLANES
    cc = jnp.concatenate([c_all, a["c_ctx"][None], jnp.zeros((SUBLANES - 1, d), F32)], axis=0)

    n_ada = a["w_ada"].shape[2]
    b_loc = lax.dynamic_slice(a["b_ada"], (0, me * n_ada), (nl, n_ada))[:, None, :]
    mod_cols = _ada_fwd(cc, a["w_ada"], b_loc)
    mod_all = _exchange("gather_mod", [mod_cols.reshape(nl * cond_rows, n_ada)], ["slot"])[0].reshape(N_DEV, nl, cond_rows, n_ada)
    mod_mine = jnp.concatenate([lax.dynamic_slice(mod_all, (0, 0, nb * me, 0), (N_DEV, nl, nb, n_ada)),
                                mod_all[:, :, N_DEV * nb:N_DEV * nb + 1]], axis=2)
    mod = jnp.transpose(mod_mine, (1, 2, 0, 3)).reshape(nl, nb + 1, 6, 1, d)

    def seg(i, q):
        return mod[i, :, q]

    zero_seg = jnp.zeros((nb + 1, 1, d), F32)

    def vec(v):
        return v.reshape(1, -1)

    pos = _grid_pos_embed(seq // GRID_W, d)
    def latent_rows(v):
        return jnp.transpose(v.reshape(nb, tpl, tr, d), (1, 0, 2, 3)).reshape(nb * seq, d)

    xc = jnp.concatenate([latent_rows(x), ctx.reshape(nb * lc, d)], axis=0)
    pos_rows = jnp.concatenate([latent_rows(jnp.broadcast_to(pos[None], (nb, seq, d))), jnp.zeros((nb * lc, d), F32)], axis=0)
    x_cur, h_cur = _rowwise("entry", _f_entry, [xc, pos_rows], [seg(0, 0), seg(0, 1)], [], [(d, F32), (d, BF16)], [], [], cfg, (1,))
    saved = []
    for i in range(nl):
        j = i // 2
        sv = {"x": x_cur, "h": h_cur}
        sh1, sc1, g1, sh2, sc2, g2 = (seg(i, q) for q in range(6))
        gain0, bias0, gain1, bias1 = (vec(full["ln_gain"][i, 0]), vec(full["ln_bias"][i, 0]),
                                      vec(full["ln_gain"][i, 1]), vec(full["ln_bias"][i, 1]))
        if i % 2 == 0:
            lam_re, lam_im = a["s5_lam_re"][j], a["s5_lam_im"][j]
            log_dt = a["s5_log_dt"][j][:, :, None]
            b_re_t = jnp.transpose(a["s5_b_re"][j], (0, 3, 1, 2))
            b_im_t = jnp.transpose(a["s5_b_im"][j], (0, 3, 1, 2))
            sv["prep_in"] = (lam_re, lam_im, log_dt, b_re_t, b_im_t)
            ab_re, ab_im, bb_re, bb_im = _s5_prep(f"s5_prep{i}", *sv["prep_in"])
            sv["ab"] = (ab_re, ab_im)
            ys = []
            for dirn in range(2):
                def blocks(t):
                    return jnp.transpose(t, (1, 0, 2)).reshape(ns, gs, ch, n_state)
                bmat = jnp.concatenate([_block_diag(blocks(bb_re[dirn])), _block_diag(blocks(bb_im[dirn]))], axis=2).astype(BF16)
                c_re_t = jnp.transpose(a["s5_c_re"][j, dirn], (0, 2, 1)).reshape(ns, gs, n_state, ch)
                c_im_t = jnp.transpose(a["s5_c_im"][j, dirn], (0, 2, 1)).reshape(ns, gs, n_state, ch)
                cmat = jnp.concatenate([_block_diag(c_re_t), -_block_diag(c_im_t)], axis=1).astype(BF16)
                tab = _s5_tables(ab_re[dirn], ab_im[dirn], dirn == 1, False, ns, tr // SUBLANES)
                group = layer_weights(i + dirn)
                cps = _Copies([a[n][idx].astype(BF16) for n, idx, _ in group], [axis for _, _, axis in group])
                h_states, y_dir, *gathered = _s5_fwd(f"s5_fwd{i}_{dirn}", h_cur, bmat, cmat, tab, dirn, cfg, cps)
                weights.update({(n, idx): w[None] for (n, idx, _), w in zip(group, gathered)})
                sv[f"mats{dirn}"] = (jnp.transpose(bmat, (0, 2, 1)), jnp.transpose(cmat, (0, 2, 1)))
                sv[f"states{dirn}"] = h_states
                ys.append(y_dir)
            sv["y"] = ys
            dsk = vec(a["s5_d"][j])
            z = _rowwise(f"gelu{i}", _f_gelu, [x_cur, ys[0], ys[1]], [sh1, sc1], [dsk], [(d, BF16)], [], [], cfg)[0]
            zz = _mm_nn(f"glu{i}", z, weights["s5_w_glu", j], 0, tm_big, min(2 * d, 512))[0]
            bglu = vec(a["s5_b_glu"][j])
            x1, h2 = _rowwise(f"sub1_{i}", f_sub1_s5, [x_cur, zz], [g1, sh2, sc2], [bglu, gain0, bias0],
                              [(d, F32), (d, BF16)], [], [], cfg)
            sv.update(z=z, zz=zz)
        else:
            zz = _mm_nn(f"pw1_{i}", h_cur, weights["cv_w_pw1", j], 0, tm_big, min(2 * d, 512))[0]
            bpw1 = vec(full["cv_b_pw1"][j])
            act = _rowwise(f"cvglu{i}", _f_cvglu, [zz], [], [bpw1], [(d, F32)], [], [], cfg)[0]
            w_dw = full["cv_w_dw"][j]
            cv = _dwconv(f"dwconv{i}", act, w_dw, cfg)
            bdw, lng, lnb = vec(full["cv_b_dw"][j]), vec(full["cv_ln_g"][j]), vec(full["cv_ln_b"][j])
            s_act = _rowwise(f"cvln{i}", _f_cvln, [cv], [], [bdw, lng, lnb], [(d, BF16)], [], [], cfg)[0]
            mm = _mm_nn(f"pw2_{i}", s_act, weights["cv_w_pw2", j], 0, tm_big, d)[0]
            bpw2 = vec(full["cv_b_pw2"][j])
            x1, h2 = _rowwise(f"sub1_{i}", f_sub1_cv, [x_cur, mm], [g1, sh2, sc2], [bpw2, gain0, bias0],
                              [(d, F32), (d, BF16)], [], [], cfg)
            sv.update(zz=zz, act=act, cv=cv, s_act=s_act, mm=mm, w_dw=w_dw)
        dff = weights["mlp_w1", i].shape[2]
        p_act, r_act = _mm_nn(f"mlp1_{i}", h2, weights["mlp_w1", i], 0, tm_big, min(dff, 1024), (BF16, BF16),
                              lambda acc: (jnp.square(jnp.maximum(acc, 0.0)), jnp.maximum(acc, 0.0)))
        m_out = _mm_nn(f"mlp2_{i}", p_act, weights["mlp_w2", i], 0, tm, d)[0]
        shn, scn = (seg(i + 1, 0), seg(i + 1, 1)) if i + 1 < nl else (zero_seg, zero_seg)
        x2, hn = _rowwise(f"sub2_{i}", f_sub2, [x1, m_out], [g2, shn, scn], [gain1, bias1], [(d, F32), (d, BF16)], [], [], cfg,
                          (1,) if (i + 1) % 2 == 0 and i + 1 < nl else ())
        sv.update(x1=x1, h2=h2, p=p_act, r=r_act, m=m_out, shn=shn, scn=scn)
        saved.append(sv)
        x_cur, h_cur = x2, hn

    target = jnp.concatenate([latent_rows(a["loss_target"]), jnp.zeros((nb * lc, d), F32)], axis=0)
    mask = jnp.concatenate([jnp.ones((nb, 1, d), F32), jnp.zeros((1, 1, d), F32)], axis=0)

    def f_loss(xf, tgt, msk):
        err = (xf - tgt) * msk
        part = 0.5 * jnp.sum(jnp.square(err), axis=(0, 1), keepdims=True) / d
        return err / d, jnp.broadcast_to(part, (1, LANES))

    dx_final, loss_part = _rowwise("loss", f_loss, [x_cur, target], [mask], [], [(d, F32)], [], [LANES], cfg)
    loss = lax.psum(loss_part[0, 0], ("x", "y", "c"))

    grads = {n: [None] * a[n].shape[0] for n in WEIGHTS if n not in ("c_ctx", "w_ada", "b_ada")}
    dmod = [[None] * 6 for _ in range(nl)]

    def add_mod(i, q, val):
        dmod[i][q] = val if dmod[i][q] is None else dmod[i][q] + val

    dx_parts, dh_parts = [dx_final], []
    for i in reversed(range(nl)):
        j = i // 2
        sv = saved[i]
        sh1, sc1, g1, sh2, sc2, g2 = (seg(i, q) for q in range(6))
        gain0, bias0, gain1, bias1 = (vec(full["ln_gain"][i, 0]), vec(full["ln_bias"][i, 0]),
                                      vec(full["ln_gain"][i, 1]), vec(full["ln_bias"][i, 1]))
        bwd = _vjp_fn(f_sub2, 2, (len(dx_parts), len(dh_parts)), (0, 1, 2, 3, 4, 5, 6))
        dx1, dm, dg2, dshn, dscn, dgain1, dbias1 = _rowwise(
            f"sub2_bwd{i}", bwd, [sv["x1"], sv["m"]] + dx_parts + dh_parts, [g2, sv["shn"], sv["scn"]], [gain1, bias1],
            [(d, F32), (d, BF16)], [d, d, d], [d, d], cfg)
        add_mod(i, 5, dg2)
        if i + 1 < nl:
            add_mod(i + 1, 0, dshn)
            add_mod(i + 1, 1, dscn)
        da = _mm_nt(f"mlp2_dgrad{i}", dm, weights["mlp_w2", i], 0, tm_big, min(dff, 1024), [sv["r"]], BF16,
                    lambda acc, r: (acc * 2.0 * r,))
        wgrads["mlp_w2", i] = _mm_wgrad_rows(f"mlp2_wgrad{i}", sv["p"], dm, tm_big)
        wgrads["mlp_w1", i] = _mm_wgrad_cols(f"mlp1_wgrad{i}", sv["h2"], da, tm_big)
        dh2 = _mm_nt(f"mlp1_dgrad{i}", da, weights["mlp_w1", i], 0, tm, d)
        if i % 2 == 0:
            bglu = vec(a["s5_b_glu"][j])
            bwd = _vjp_fn(f_sub1_s5, 2, (1, 1), (0, 1, 2, 3, 4, 5, 6, 7))
            dxa, dzz, dg1, dsh2, dsc2, dbglu, dgain0, dbias0 = _rowwise(
                f"sub1_bwd{i}", bwd, [sv["x"], sv["zz"], dx1, dh2], [g1, sh2, sc2], [bglu, gain0, bias0],
                [(d, F32), (2 * d, BF16)], [d, d, d], [2 * d, d, d], cfg)
            grads["s5_b_glu"][j] = dbglu[0]
            wgrads["s5_w_glu", j] = _mm_wgrad_cols(f"glu_wgrad{i}", sv["z"], dzz, tm_big)
            dz = _mm_nt(f"glu_dgrad{i}", dzz, weights["s5_w_glu", j], 0, tm, d)
            dsk = vec(a["s5_d"][j])
            bwd = _vjp_fn(_f_gelu, 3, (1,), (0, 1, 3, 4, 5))
            dxb, dy, dsh1, dsc1, ddsk = _rowwise(f"gelu_bwd{i}", bwd, [sv["x"], sv["y"][0], sv["y"][1], dz], [sh1, sc1], [dsk],
                                                 [(d, F32), (d, BF16)], [d, d], [d], cfg, (1,))
            grads["s5_d"][j] = ddsk[0]
            add_mod(i, 0, dsh1)
            add_mod(i, 1, dsc1)
            ab_re, ab_im = sv["ab"]
            dus, d_ab_re, d_ab_im, d_bb_re, d_bb_im, d_c_re, d_c_im = [], [], [], [], [], [], []
            for dirn in range(2):
                bmat_t, cmat_t = sv[f"mats{dirn}"]
                tab = _s5_tables(ab_re[dirn], ab_im[dirn], dirn == 0, True, ns, tr // SUBLANES)
                group = layer_weights(i + 1 - dirn)
                cps = _Copies([wgrads[n, idx] for n, idx, _ in group], ["scatter"] * len(group))
                du, d_b, d_c, d_a, *parts = _s5_bwd(f"s5_bwd{i}_{dirn}", dy, sv[f"states{dirn}"], sv["h"], cmat_t, bmat_t, tab,
                                                    dirn, cfg, cps)
                received.update({(n, idx): p for (n, idx, _), p in zip(group, parts)})
                dus.append(du)
                sw = d_a.shape[2] // 2
                d_a = jnp.sum(d_a, axis=1)
                d_ab_re.append(d_a[:, :sw].reshape(n_grp, n_state))
                d_ab_im.append(d_a[:, sw:].reshape(n_grp, n_state))

                def unblock_b(t):
                    return jnp.transpose(_diag_blocks(t, gs).reshape(n_grp, ch, n_state), (1, 0, 2))

                def unblock_c(t):
                    return jnp.transpose(_diag_blocks(t, gs).reshape(n_grp, n_state, ch), (0, 2, 1))
                d_bb_re.append(unblock_b(d_b[:, :, :sw]))
                d_bb_im.append(unblock_b(d_b[:, :, sw:]))
                d_c_re.append(unblock_c(d_c[:, :sw]))
                d_c_im.append(-unblock_c(d_c[:, sw:]))
            g_lre, g_lim, g_ldt, g_bre, g_bim = _s5_prep_bwd(
                f"s5_prep_bwd{i}", *sv["prep_in"], (jnp.stack(d_ab_re), jnp.stack(d_ab_im), jnp.stack(d_bb_re), jnp.stack(d_bb_im)))
            grads["s5_lam_re"][j], grads["s5_lam_im"][j], grads["s5_log_dt"][j] = g_lre, g_lim, g_ldt[:, :, 0]
            grads["s5_b_re"][j] = jnp.transpose(g_bre, (0, 2, 3, 1))
            grads["s5_b_im"][j] = jnp.transpose(g_bim, (0, 2, 3, 1))
            grads["s5_c_re"][j], grads["s5_c_im"][j] = jnp.stack(d_c_re), jnp.stack(d_c_im)
            dx_parts, dh_parts = [dxa, dxb], dus
        else:
            bpw2 = vec(full["cv_b_pw2"][j])
            bwd = _vjp_fn(f_sub1_cv, 2, (1, 1), (0, 1, 2, 3, 4, 5, 6, 7))
            dxa, dmm, dg1, dsh2, dsc2, dbpw2, dgain0, dbias0 = _rowwise(
                f"sub1_bwd{i}", bwd, [sv["x"], sv["mm"], dx1, dh2], [g1, sh2, sc2], [bpw2, gain0, bias0],
                [(d, F32), (d, BF16)], [d, d, d], [d, d, d], cfg)
            grads["cv_b_pw2"][j] = dbpw2[0]
            wgrads["cv_w_pw2", j] = _mm_wgrad_rows(f"pw2_wgrad{i}", sv["s_act"], dmm, tm_big)
            ds = _mm_nt(f"pw2_dgrad{i}", dmm, weights["cv_w_pw2", j], 0, tm_big, d)
            bdw, lng, lnb = vec(full["cv_b_dw"][j]), vec(full["cv_ln_g"][j]), vec(full["cv_ln_b"][j])
            bwd = _vjp_fn(_f_cvln, 1, (1,), (0, 1, 2, 3))
            dcv, dbdw, dlng, dlnb = _rowwise(f"cvln_bwd{i}", bwd, [sv["cv"], ds], [], [bdw, lng, lnb], [(d, F32)], [], [d, d, d], cfg)
            grads["cv_b_dw"][j], grads["cv_ln_g"][j], grads["cv_ln_b"][j] = dbdw[0], dlng[0], dlnb[0]
            dact = _dwconv(f"dwconv_bwd{i}", dcv, sv["w_dw"][::-1], cfg)
            grads["cv_w_dw"][j] = _dwconv_wgrad(f"dwconv_wgrad{i}", sv["act"], dcv, sv["w_dw"].shape[0], cfg)
            bpw1 = vec(full["cv_b_pw1"][j])
            bwd = _vjp_fn(_f_cvglu, 1, (1,), (0, 1))
            dzz, dbpw1 = _rowwise(f"cvglu_bwd{i}", bwd, [sv["zz"], dact], [], [bpw1], [(2 * d, BF16)], [], [2 * d], cfg)
            grads["cv_b_pw1"][j] = dbpw1[0]
            wgrads["cv_w_pw1", j] = _mm_wgrad_cols(f"pw1_wgrad{i}", sv["h"], dzz, tm_big)
            dh = _mm_nt(f"pw1_dgrad{i}", dzz, weights["cv_w_pw1", j], 0, tm, d)
            dx_parts, dh_parts = [dxa], [dh]
        grads["ln_gain"][i] = jnp.stack([dgain0[0], dgain1[0]])
        grads["ln_bias"][i] = jnp.stack([dbias0[0], dbias1[0]])
        add_mod(i, 2, dg1)
        add_mod(i, 3, dsh2)
        add_mod(i, 4, dsc2)
    bwd = _vjp_fn(_f_entry, 2, (len(dx_parts), len(dh_parts)), (0, 2, 3))
    dxc, dsh1, dsc1 = _rowwise("entry_bwd", bwd, [xc, pos_rows] + dx_parts + dh_parts, [seg(0, 0), seg(0, 1)], [],
                               [(d, F32)], [d, d], [], cfg)
    add_mod(0, 0, dsh1)
    add_mod(0, 1, dsc1)
    grad_x = jnp.transpose(dxc[:nb * seq].reshape(tpl, nb, tr, d), (1, 0, 2, 3)).reshape(nb, seq, d)

    dmod_loc = jnp.stack([jnp.concatenate([q[:, 0] for q in dmod[i]], axis=1) for i in range(nl)])
    dmod_all = _exchange("gather_dmod", [dmod_loc.reshape(nl * (nb + 1), 6 * d)], ["slot"])[0].reshape(N_DEV, nl, nb + 1, 6 * d)
    mine = lax.dynamic_slice(dmod_all, (0, 0, 0, me * n_ada), (N_DEV, nl, nb + 1, n_ada))
    dmod_rows = jnp.transpose(mine[:, :, :nb], (1, 0, 2, 3)).reshape(nl, N_DEV * nb, n_ada)
    dmod_rows = jnp.concatenate([dmod_rows, jnp.zeros((nl, SUBLANES, n_ada), F32)], axis=1)
    g_w_ada, _, dcond = _ada_bwd(cc, a["w_ada"], dmod_rows, mine[:, :, nb:])
    g_b_ada = _sum_lead("b_ada_sum", jnp.transpose(dmod_all, (0, 2, 1, 3)).reshape(N_DEV * (nb + 1), nl, 6 * d), nl)

    small_names = SHARDED_SMALL + REPLICATED_SMALL
    small_full = [jnp.stack(grads[n]) for n in small_names]
    small_packed = _pack(small_full, N_DEV * SUBLANES)
    small_parts, dcond_all = _exchange("scatter_small_grads", [small_packed.reshape(N_DEV, -1, LANES), dcond[N_DEV * nb:N_DEV * nb + 1]],
                                       ["scatter", "slot"])
    g_c_ctx = _cctx_grad(dcond_all, a["c_ctx"][None])[0]
    small_part = _sum_lead("small_grad_sum", small_parts, _row_tile(small_parts.shape[1], 512))
    small_sum = _exchange("gather_small_sum", [small_part], ["slot"])[0]
    small_g = dict(zip(small_names, _unpack(small_sum, [g.shape for g in small_full])))
    for n in SHARDED_SMALL:
        width = a[n].shape[-1]
        start = (0,) * (small_g[n].ndim - 1) + (me * width,)
        small_g[n] = lax.dynamic_slice(small_g[n], start, a[n].shape)
    small_g["c_ctx"], small_g["b_ada"] = g_c_ctx, g_b_ada

    out = {}

    def update(n, parts):
        shp = a[n].shape
        cols = parts.shape[-1]
        rows = parts.shape[1]
        res = _adamw(f"adamw_{n}", parts, a[n].reshape(rows, cols), a["m_" + n].reshape(rows, cols), a["v_" + n].reshape(rows, cols),
                     _row_tile(rows, max(SUBLANES, 131072 // cols)))
        out[n] = [r.reshape(shp) for r in res]

    for n in BIG:
        rows, cols = a[n].shape[1:]
        bufs = [lax.empty(a[n].shape, F32) for _ in range(4)]
        for idx in range(a[n].shape[0]):
            bufs = _adamw_layer(f"adamw_{n}{idx}", received[n, idx], a[n], a["m_" + n], a["v_" + n], bufs, idx,
                                _row_tile(rows, max(SUBLANES, 131072 // cols)))
        out[n] = bufs
    update("w_ada", g_w_ada.reshape(1, -1, n_ada))
    for n in NATIVE_SMALL:
        out[n] = [small_g[n], *_adamw_native(f"adamw_{n}", small_g[n], a[n], a["m_" + n], a["v_" + n])]
    small_all_names = ("c_ctx", "b_ada") + tuple(n for n in small_names if n not in NATIVE_SMALL)
    packed = [_pack([src[n] for n in small_all_names]) for src in
              (small_g, a, {n: a["m_" + n] for n in small_all_names}, {n: a["v_" + n] for n in small_all_names})]
    res = _adamw("adamw_small", packed[0][None], packed[1], packed[2], packed[3], _row_tile(packed[0].shape[0], 512))
    shapes = [a[n].shape for n in small_all_names]
    for n, vals in zip(small_all_names, zip(*[_unpack(r, shapes) for r in res])):
        out[n] = list(vals)
    return (loss, grad_x, *[out[n][0] for n in WEIGHTS], *[out[n][1] for n in WEIGHTS],
            *[out[n][2] for n in WEIGHTS], *[out[n][3] for n in WEIGHTS])


def kernel(x, c, ctx, c_ctx, w_ada, b_ada, ln_gain, ln_bias, s5_lam_re, s5_lam_im, s5_log_dt, s5_b_re, s5_b_im, s5_c_re, s5_c_im, s5_d, s5_w_glu, s5_b_glu, cv_w_pw1, cv_b_pw1, cv_w_dw, cv_b_dw, cv_ln_g, cv_ln_b, cv_w_pw2, cv_b_pw2, mlp_w1, mlp_w2, loss_target, m_c_ctx, m_w_ada, m_b_ada, m_ln_gain, m_ln_bias, m_s5_lam_re, m_s5_lam_im, m_s5_log_dt, m_s5_b_re, m_s5_b_im, m_s5_c_re, m_s5_c_im, m_s5_d, m_s5_w_glu, m_s5_b_glu, m_cv_w_pw1, m_cv_b_pw1, m_cv_w_dw, m_cv_b_dw, m_cv_ln_g, m_cv_ln_b, m_cv_w_pw2, m_cv_b_pw2, m_mlp_w1, m_mlp_w2, v_c_ctx, v_w_ada, v_b_ada, v_ln_gain, v_ln_bias, v_s5_lam_re, v_s5_lam_im, v_s5_log_dt, v_s5_b_re, v_s5_b_im, v_s5_c_re, v_s5_c_im, v_s5_d, v_s5_w_glu, v_s5_b_glu, v_cv_w_pw1, v_cv_b_pw1, v_cv_w_dw, v_cv_b_dw, v_cv_ln_g, v_cv_ln_b, v_cv_w_pw2, v_cv_b_pw2, v_mlp_w1, v_mlp_w2):
    return _step(dict(locals()))
```

```python
import functools
import math

import jax
import jax.numpy as jnp
from jax import lax
from jax.experimental import pallas as pl
from jax.experimental.pallas import tpu as pltpu

F32 = jnp.float32
BF16 = jnp.bfloat16
N_DEV = 8
LANES = 128
SUBLANES = 8
VMEM_LIMIT = 56 * 1024 * 1024
GRID_W = 64
POS_TEMP = 10000.0
LN_EPS = 1e-5
LAMBDA_RE_MAX = -1e-4
ADAM_LR, ADAM_B1, ADAM_B2, ADAM_EPS, ADAM_WD, ADAM_STEP = 0.001, 0.9, 0.999, 1e-08, 0.01, 10
MESH = pl.DeviceIdType.MESH


def _params(sem):
    return pltpu.CompilerParams(dimension_semantics=sem, vmem_limit_bytes=VMEM_LIMIT)


def _accumulate(ref, val, first):
    @pl.when(first)
    def _():
        ref[...] = val

    @pl.when(jnp.logical_not(first))
    def _():
        ref[...] += val


def _rowwise(name, fn, rows, segs, vecs, row_outs, seg_accs, vec_accs, cfg, interleaved=()):
    tr, tpl, nb = cfg["tr"], cfg["tpl"], cfg["nb"]
    n_rows = rows[0].shape[0]
    nt = n_rows // tr
    nr, ns, nv = len(rows), len(segs), len(vecs)
    nro, nsa, nva = len(row_outs), len(seg_accs), len(vec_accs)

    def seg_of(t):
        return jnp.minimum(t // tpl, nb)

    def body(*refs):
        t = pl.program_id(0)
        ins, outs, mix_refs = refs[:nr + ns + nv], refs[nr + ns + nv:nr + ns + nv + nro + nsa + nva], refs[nr + ns + nv + nro + nsa + nva:]
        vals = [r[...] for r in ins[:nr]] + [r[0] for r in ins[nr:nr + ns]] + [r[...] for r in ins[nr + ns:]]
        res = fn(*vals)
        for idx, (o, v) in enumerate(zip(outs[:nro], res[:nro])):
            if idx in interleaved:
                mix_ref = mix_refs[interleaved.index(idx)]
                for p in range(mix_ref.shape[0]):
                    mix_ref[p] = v[:, p * LANES:(p + 1) * LANES].astype(F32)
                o[...] = jnp.concatenate([_interleave_rows(mix_ref.at[p], 1, o.dtype) for p in range(mix_ref.shape[0])], axis=1)
            else:
                o[...] = v.astype(o.dtype)
        first_seg = jnp.logical_or(t == 0, seg_of(t) != seg_of(jnp.maximum(t - 1, 0)))
        for o, v in zip(outs[nro:nro + nsa], res[nro:nro + nsa]):
            _accumulate(o.at[0], v, first_seg)
        for o, v in zip(outs[nro + nsa:], res[nro + nsa:]):
            _accumulate(o, v, t == 0)

    in_specs = ([pl.BlockSpec((tr, a.shape[1]), lambda t: (_phys_tile(t, cfg), 0)) for a in rows]
                + [pl.BlockSpec((1, 1, a.shape[2]), lambda t: (seg_of(t), 0, 0)) for a in segs]
                + [pl.BlockSpec((1, a.shape[1]), lambda t: (0, 0)) for a in vecs])
    out_specs = ([pl.BlockSpec((tr, c), lambda t: (_phys_tile(t, cfg), 0)) for c, _ in row_outs]
                 + [pl.BlockSpec((1, 1, c), lambda t: (seg_of(t), 0, 0)) for c in seg_accs]
                 + [pl.BlockSpec((1, c), lambda t: (0, 0)) for c in vec_accs])
    out_shape = ([jax.ShapeDtypeStruct((n_rows, c), dt) for c, dt in row_outs]
                 + [jax.ShapeDtypeStruct((nb + 1, 1, c), F32) for c in seg_accs]
                 + [jax.ShapeDtypeStruct((1, c), F32) for c in vec_accs])
    return pl.pallas_call(body, name=name, grid=(nt,), in_specs=in_specs, out_specs=out_specs, out_shape=out_shape,
                          scratch_shapes=[pltpu.VMEM((row_outs[idx][0] // LANES, tr, LANES), F32) for idx in interleaved],
                          compiler_params=_params(("arbitrary",)))(*rows, *segs, *vecs)


def _vjp_fn(fn, n_row, cot_groups, want):
    n_cot = sum(cot_groups)

    def bwd(*args):
        primals = [a.astype(F32) for a in args[:n_row] + args[n_row + n_cot:]]
        outs, vjp = jax.vjp(fn, *primals)
        cots, pos = [], n_row
        for n, o in zip(cot_groups, outs):
            cot = jnp.zeros_like(o)
            for part in args[pos:pos + n]:
                cot = cot + part.astype(F32)
            cots.append(cot)
            pos += n
        grads = vjp(tuple(cots))
        return tuple(grads[i] for i in want)
    return bwd


def _ln(r, g, b):
    mu = jnp.mean(r, axis=-1, keepdims=True)
    var = jnp.mean(jnp.square(r - mu), axis=-1, keepdims=True)
    return (r - mu) * lax.rsqrt(var + LN_EPS) * g + b


def _glu(zz, bias):
    d = zz.shape[1] // 2
    return (zz[:, :d] + bias[:, :d]) * jax.nn.sigmoid(zz[:, d:] + bias[:, d:])


def _f_entry(xc, pos, sh, sc):
    x0 = xc + pos
    return x0, x0 * (1 + sc) + sh


def _f_gelu(x, y0, y1, sh, sc, dsk):
    u = x * (1 + sc) + sh
    y = dsk * u + y0 + y1
    return (0.5 * y * (1.0 + lax.erf(y * (2.0 ** -0.5))),)


def _make_sub1_s5(alpha):
    def f(x, zz, g1, sh2, sc2, bglu, gain, bias):
        x1 = _ln(alpha * x + g1 * _glu(zz, bglu), gain, bias)
        return x1, x1 * (1 + sc2) + sh2
    return f


def _make_sub1_cv(alpha):
    def f(x, mm, g1, sh2, sc2, bpw2, gain, bias):
        x1 = _ln(alpha * x + g1 * (mm + bpw2), gain, bias)
        return x1, x1 * (1 + sc2) + sh2
    return f


def _make_sub2(alpha):
    def f(x1, m, g2, shn, scn, gain, bias):
        x2 = _ln(alpha * x1 + g2 * m, gain, bias)
        return x2, x2 * (1 + scn) + shn
    return f


def _f_cvglu(zz, bpw1):
    return (_glu(zz, bpw1),)


def _f_cvln(cv, bdw, lng, lnb):
    return (jax.nn.silu(_ln(cv + bdw, lng, lnb)),)


def _matmul(name, a, b, extras, grid, a_spec, b_spec, extra_specs, o_specs, out_shape, dims, red_axis, epi, sem):
    n_extra = len(extras)
    n_out = len(out_shape)
    acc_shape = o_specs[0].block_shape
    acc_shape = tuple(s for s in acc_shape if s is not None)

    def body(*refs):
        a_ref, b_ref = refs[0], refs[1]
        ex = refs[2:2 + n_extra]
        outs = refs[2 + n_extra:2 + n_extra + n_out]
        prod = lax.dot_general(a_ref[...], b_ref[...], dims, preferred_element_type=F32)

        def finish(acc):
            res = epi(acc, *[e[...] for e in ex]) if epi is not None else (acc,)
            for o, v in zip(outs, res):
                o[...] = v.astype(o.dtype)

        if red_axis is None:
            finish(prod)
        else:
            acc_ref = refs[-1]
            k = pl.program_id(red_axis)
            nk = pl.num_programs(red_axis)

            @pl.when(k == 0)
            def _():
                acc_ref[...] = prod

            @pl.when(k > 0)
            def _():
                acc_ref[...] += prod

            @pl.when(k == nk - 1)
            def _():
                finish(acc_ref[...])

    scratch = [] if red_axis is None else [pltpu.VMEM(acc_shape, F32)]
    res = pl.pallas_call(body, name=name, grid=grid, in_specs=[a_spec, b_spec] + list(extra_specs),
                         out_specs=list(o_specs), out_shape=list(out_shape), scratch_shapes=scratch,
                         compiler_params=_params(sem))(a, b, *extras)
    return res


NN = (((1,), (0,)), ((), ()))
NT = (((1,), (1,)), ((), ()))
TN = (((0,), (0,)), ((), ()))


def _mm_nn(name, a, w3, layer, tm, tn, out_dtypes=(F32,), epi=None):
    m, k = a.shape
    n = w3.shape[2]
    return _matmul(name, a, w3, (), (n // tn, m // tm),
                   pl.BlockSpec((tm, k), lambda j, i: (i, 0)), pl.BlockSpec((None, k, tn), lambda j, i: (layer, 0, j)), (),
                   [pl.BlockSpec((tm, tn), lambda j, i: (i, j)) for _ in out_dtypes],
                   [jax.ShapeDtypeStruct((m, n), dt) for dt in out_dtypes], NN, None, epi, ("arbitrary", "arbitrary"))


def _mm_nt(name, dy, w3, layer, tm, tkw, extras=(), out_dtype=F32, epi=None):
    m, n = dy.shape
    kw = w3.shape[1]
    return _matmul(name, dy, w3, tuple(extras), (kw // tkw, m // tm),
                   pl.BlockSpec((tm, n), lambda j, i: (i, 0)), pl.BlockSpec((None, tkw, n), lambda j, i: (layer, j, 0)),
                   [pl.BlockSpec((tm, tkw), lambda j, i: (i, j)) for _ in extras],
                   [pl.BlockSpec((tm, tkw), lambda j, i: (i, j))], [jax.ShapeDtypeStruct((m, kw), out_dtype)], NT, None, epi,
                   ("arbitrary", "arbitrary"))[0]


def _mm_wgrad_cols(name, a, dy, tm):
    m, k = a.shape
    n = dy.shape[1] // N_DEV
    return _matmul(name, a, dy, (), (N_DEV, m // tm),
                   pl.BlockSpec((tm, k), lambda j, i: (i, 0)), pl.BlockSpec((tm, n), lambda j, i: (i, j)), (),
                   [pl.BlockSpec((None, k, n), lambda j, i: (j, 0, 0))], [jax.ShapeDtypeStruct((N_DEV, k, n), BF16)],
                   TN, 1, None, ("arbitrary", "arbitrary"))[0]


def _mm_wgrad_rows(name, a, dy, tm):
    m, k = a.shape
    r = k // N_DEV
    n = dy.shape[1]
    rows = r * max(1, min(N_DEV, 1024 // r))
    out = _matmul(name, a, dy, (), (k // rows, m // tm),
                  pl.BlockSpec((tm, rows), lambda j, i: (i, j)), pl.BlockSpec((tm, n), lambda j, i: (i, 0)), (),
                  [pl.BlockSpec((rows, n), lambda j, i: (j, 0))], [jax.ShapeDtypeStruct((k, n), BF16)],
                  TN, 1, None, ("arbitrary", "arbitrary"))[0]
    return out.reshape(N_DEV, r, n)


class _Copies:
    def __init__(self, arrays, kinds):
        self.arrays, self.kinds, self.n = list(arrays), list(kinds), len(arrays)
        any_spec = pl.BlockSpec(memory_space=pl.ANY)
        self.in_specs = [any_spec] * self.n
        self.out_specs = [any_spec] * self.n
        self.out_shape = [jax.ShapeDtypeStruct(self._result(a, kind), a.dtype) for a, kind in zip(arrays, kinds)]
        self.scratch = [pltpu.SemaphoreType.DMA((self.n, N_DEV - 1)), pltpu.SemaphoreType.DMA((self.n, N_DEV - 1)),
                        pltpu.SemaphoreType.DMA((self.n,))] if self.n else []

    @staticmethod
    def _result(a, kind):
        if kind == "slot":
            return (N_DEV,) + a.shape
        if kind == "scatter":
            return a.shape
        return a.shape[:kind] + (N_DEV * a.shape[kind],) + a.shape[kind + 1:]

    def descriptors(self, ins, outs, sems):
        send_sems, recv_sems, local_sems = sems
        x, y, c = lax.axis_index("x"), lax.axis_index("y"), lax.axis_index("c")
        me = 4 * x + 2 * y + c
        first, relay, finish = [], [], []

        def remote(i, k, src, dst, to):
            return pltpu.make_async_remote_copy(src_ref=src, dst_ref=dst, send_sem=send_sems.at[i, k], recv_sem=recv_sems.at[i, k],
                                                device_id=to, device_id_type=MESH)

        for i, kind in enumerate(self.kinds):
            if kind in ("slot", "scatter"):
                scatter = kind == "scatter"
                local = pltpu.make_async_copy(ins[i].at[me] if scatter else ins[i], outs[i].at[me], local_sems.at[i])
                first.append(local)
                finish.append((local, "all"))
                for k in range(1, N_DEV):
                    px = 1 - x if k & 4 else x
                    py = 1 - y if k & 2 else y
                    pc = 1 - c if k & 1 else c
                    cp = remote(i, k - 1, ins[i].at[4 * px + 2 * py + pc] if scatter else ins[i], outs[i].at[me], (px, py, pc))
                    first.append(cp)
                    finish.append((cp, "all"))
                continue
            size = ins[i].shape[kind]

            def block(px, py, pc):
                return outs[i].at[(slice(None),) * kind + (pl.ds(pl.multiple_of((4 * px + 2 * py + pc) * size, size), size),)]

            local = pltpu.make_async_copy(ins[i], block(x, y, c), local_sems.at[i])
            sibling = remote(i, 0, ins[i], block(x, y, c), (x, y, 1 - c))
            first += [local, sibling]
            finish += [(local, "all"), (sibling, "all")]
            for j, (qx, qy) in enumerate([(1 - x, y), (x, 1 - y), (1 - x, 1 - y)]):
                out = remote(i, 1 + j, ins[i], block(x, y, c), (qx, qy, c))
                onward = remote(i, 4 + j, block(qx, qy, c), block(qx, qy, c), (x, y, 1 - c))
                first.append(out)
                relay.append((out, onward))
                finish += [(out, "send"), (onward, "all")]
        return first, relay, finish


def _guarded(when, fn):
    if when is None:
        fn()
    else:
        pl.when(when)(fn)


def _start_all(plan, when=None):
    def run():
        for cp in plan[0]:
            cp.start()
    _guarded(when, run)


def _relay_all(plan, when=None):
    def run():
        for arrived, onward in plan[1]:
            arrived.wait_recv()
            onward.start()
    if plan[1]:
        _guarded(when, run)


def _wait_all(plan, when=None):
    def run():
        for cp, left in plan[2]:
            if left == "send":
                cp.wait_send()
            else:
                cp.wait()
    _guarded(when, run)


def _exchange(name, arrays, kinds):
    cps = _Copies(arrays, kinds)
    n = cps.n

    def body(*refs):
        plan = cps.descriptors(refs[:n], refs[n:2 * n], refs[2 * n:])
        _start_all(plan)
        _relay_all(plan)
        _wait_all(plan)

    return pl.pallas_call(body, name=name, in_specs=cps.in_specs, out_specs=cps.out_specs, out_shape=cps.out_shape,
                          scratch_shapes=cps.scratch)(*arrays)


def _carried(cps, n_in, n_out, n_scratch, refs):
    if cps is None:
        return ([], [], []), refs
    n = cps.n
    ins = refs[n_in:n_in + n]
    outs = refs[n_in + n + n_out:n_in + n + n_out + n]
    sems = refs[n_in + n + n_out + n + n_scratch:]
    own = refs[:n_in] + refs[n_in + n:n_in + n + n_out] + refs[n_in + n + n_out + n:n_in + n + n_out + n + n_scratch]
    return cps.descriptors(ins, outs, sems), own


def _sum_lead(name, parts, tr):
    npart, r, c = parts.shape

    def body(p_ref, o_ref):
        acc = p_ref[0].astype(F32)
        for p in range(1, npart):
            acc = acc + p_ref[p].astype(F32)
        o_ref[...] = acc

    return pl.pallas_call(body, name=name, grid=(r // tr,), in_specs=[pl.BlockSpec((npart, tr, c), lambda i: (0, i, 0))],
                          out_specs=pl.BlockSpec((tr, c), lambda i: (i, 0)), out_shape=jax.ShapeDtypeStruct((r, c), F32),
                          compiler_params=_params(("arbitrary",)))(parts)


def _adamw_math(g, w, m, v):
    m2 = ADAM_B1 * m + (1.0 - ADAM_B1) * g
    v2 = ADAM_B2 * v + (1.0 - ADAM_B2) * jnp.square(g)
    m_hat = m2 / (1.0 - ADAM_B1 ** ADAM_STEP)
    v_hat = v2 / (1.0 - ADAM_B2 ** ADAM_STEP)
    return -ADAM_LR * (m_hat / (jnp.sqrt(v_hat) + ADAM_EPS) + ADAM_WD * w), m2, v2


def _adamw_body(npart):
    def body(p_ref, w_ref, m_ref, v_ref, *rest):
        g_out, d_out, m_out, v_out = rest[-4:]
        g = p_ref[0].astype(F32)
        for p in range(1, npart):
            g = g + p_ref[p].astype(F32)
        g_out[...] = g
        d_out[...], m_out[...], v_out[...] = _adamw_math(g, w_ref[...], m_ref[...], v_ref[...])
    return body


def _adamw_native(name, g, w, m, v):
    rest = w.shape[2:]
    spec = pl.BlockSpec((None, None) + rest, lambda i, j: (i, j) + (0,) * len(rest))

    def body(g_ref, w_ref, m_ref, v_ref, d_out, m_out, v_out):
        d_out[...], m_out[...], v_out[...] = _adamw_math(g_ref[...], w_ref[...], m_ref[...], v_ref[...])

    return pl.pallas_call(body, name=name, grid=w.shape[:2], in_specs=[spec] * 4, out_specs=[spec] * 3,
                          out_shape=[jax.ShapeDtypeStruct(w.shape, F32)] * 3,
                          compiler_params=_params(("arbitrary", "arbitrary")))(g, w, m, v)


def _adamw(name, parts, w, m, v, tr):
    npart, r, c = parts.shape
    row = pl.BlockSpec((tr, c), lambda i: (i, 0))
    return pl.pallas_call(_adamw_body(npart), name=name, grid=(r // tr,),
                          in_specs=[pl.BlockSpec((npart, tr, c), lambda i: (0, i, 0)), row, row, row],
                          out_specs=[row] * 4, out_shape=[jax.ShapeDtypeStruct((r, c), F32)] * 4,
                          compiler_params=_params(("arbitrary",)))(parts, w, m, v)


def _adamw_layer(name, parts, w3, m3, v3, bufs, layer, tr):
    npart, r, c = parts.shape
    lay = pl.BlockSpec((None, tr, c), lambda i: (layer, i, 0))
    hbm = pl.BlockSpec(memory_space=pl.ANY)
    return pl.pallas_call(_adamw_body(npart), name=name, grid=(r // tr,),
                          in_specs=[pl.BlockSpec((npart, tr, c), lambda i: (0, i, 0)), lay, lay, lay] + [hbm] * 4,
                          out_specs=[lay] * 4, out_shape=[jax.ShapeDtypeStruct(w3.shape, F32)] * 4,
                          input_output_aliases={4: 0, 5: 1, 6: 2, 7: 3},
                          compiler_params=_params(("arbitrary",)))(parts, w3, m3, v3, *bufs)


def _row_tile(r, cap):
    if r <= cap:
        return r
    t = cap - cap % SUBLANES
    while r % t:
        t -= SUBLANES
    return t


def _ada_fwd(cc, w_ada, b_loc):
    nl, d, n = w_ada.shape
    rows = cc.shape[0]

    def body(c_ref, w_ref, b_ref, o_ref):
        cond = jax.nn.silu(c_ref[...]).astype(BF16)
        o_ref[...] = jnp.dot(cond, w_ref[...].astype(BF16), preferred_element_type=F32) + b_ref[...]

    return pl.pallas_call(body, name="ada_fwd", grid=(nl,),
                          in_specs=[pl.BlockSpec((rows, d), lambda i: (0, 0)), pl.BlockSpec((None, d, n), lambda i: (i, 0, 0)),
                                    pl.BlockSpec((None, 1, n), lambda i: (i, 0, 0))],
                          out_specs=pl.BlockSpec((None, rows, n), lambda i: (i, 0, 0)),
                          out_shape=jax.ShapeDtypeStruct((nl, rows, n), F32), compiler_params=_params(("arbitrary",)))(cc, w_ada, b_loc)


def _ada_bwd(cc, w_ada, dmod_rows, dmod_ctx):
    nl, d, n = w_ada.shape
    rows = cc.shape[0]
    ctx_row = rows - SUBLANES

    def body(c_ref, w_ref, dr_ref, dc_ref, gw_ref, tot_ref, dcond_ref):
        i = pl.program_id(0)
        total = dc_ref[0]
        for p in range(1, N_DEV):
            total = total + dc_ref[p]
        tot_ref[...] = total
        row_id = lax.broadcasted_iota(jnp.int32, (rows, n), 0)
        dm = jnp.where(row_id == ctx_row, jnp.broadcast_to(total, (rows, n)), dr_ref[...]).astype(BF16)
        cond = jax.nn.silu(c_ref[...]).astype(BF16)
        gw_ref[...] = lax.dot_general(cond, dm, TN, preferred_element_type=F32)
        part = lax.dot_general(dm, w_ref[...].astype(BF16), NT, preferred_element_type=F32)
        _accumulate(dcond_ref, part, i == 0)

    return pl.pallas_call(body, name="ada_bwd", grid=(nl,),
                          in_specs=[pl.BlockSpec((rows, d), lambda i: (0, 0)), pl.BlockSpec((None, d, n), lambda i: (i, 0, 0)),
                                    pl.BlockSpec((None, rows, n), lambda i: (i, 0, 0)),
                                    pl.BlockSpec((N_DEV, None, 1, n), lambda i: (0, i, 0, 0))],
                          out_specs=[pl.BlockSpec((None, d, n), lambda i: (i, 0, 0)), pl.BlockSpec((None, 1, n), lambda i: (i, 0, 0)),
                                     pl.BlockSpec((rows, d), lambda i: (0, 0))],
                          out_shape=[jax.ShapeDtypeStruct((nl, d, n), F32), jax.ShapeDtypeStruct((nl, 1, n), F32),
                                     jax.ShapeDtypeStruct((rows, d), F32)],
                          compiler_params=_params(("arbitrary",)))(cc, w_ada, dmod_rows, dmod_ctx)


def _cctx_grad(parts, c_ctx):
    def body(p_ref, c_ref, o_ref):
        tot = p_ref[0]
        for p in range(1, N_DEV):
            tot = tot + p_ref[p]
        _, vjp = jax.vjp(jax.nn.silu, c_ref[...])
        o_ref[...] = vjp(tot)[0]

    return pl.pallas_call(body, name="cctx_grad", out_shape=jax.ShapeDtypeStruct(c_ctx.shape, F32))(parts, c_ctx)


def _discretise(lam_re, lam_im, log_dt, b_re, b_im):
    lr = jnp.minimum(lam_re, LAMBDA_RE_MAX)
    li = lam_im
    dt = jnp.exp(log_dt)
    mag = jnp.exp(lr * dt)
    ab_re = mag * jnp.cos(li * dt)
    ab_im = mag * jnp.sin(li * dt)
    den = lr * lr + li * li
    nr = ab_re - 1.0
    ni = ab_im
    coef_re = ((nr * lr + ni * li) / den)[:, None]
    coef_im = ((ni * lr - nr * li) / den)[:, None]
    bb_re = coef_re * b_re - coef_im * b_im
    bb_im = coef_re * b_im + coef_im * b_re
    return ab_re, ab_im, bb_re, bb_im


def _s5_prep(name, lam_re, lam_im, log_dt, b_re, b_im):
    def body(a, b, c, d, e, o1, o2, o3, o4):
        res = _discretise(a[...], b[...], c[...], d[...], e[...])
        for o, v in zip((o1, o2, o3, o4), res):
            o[...] = v

    shp = [jax.ShapeDtypeStruct(lam_re.shape, F32)] * 2 + [jax.ShapeDtypeStruct(b_re.shape, F32)] * 2
    return pl.pallas_call(body, name=name, out_shape=shp)(lam_re, lam_im, log_dt, b_re, b_im)


def _s5_prep_bwd(name, lam_re, lam_im, log_dt, b_re, b_im, cots):
    def body(a, b, c, d, e, c1, c2, c3, c4, o1, o2, o3, o4, o5):
        _, vjp = jax.vjp(_discretise, a[...], b[...], c[...], d[...], e[...])
        grads = vjp((c1[...], c2[...], c3[...], c4[...]))
        for o, v in zip((o1, o2, o3, o4, o5), grads):
            o[...] = v

    shp = [jax.ShapeDtypeStruct(a.shape, F32) for a in (lam_re, lam_im, log_dt, b_re, b_im)]
    return pl.pallas_call(body, name=name, out_shape=shp)(lam_re, lam_im, log_dt, b_re, b_im, *cots)


def _interleave_rows(ref, n_seq, dtype):
    n_j = ref.shape[0] // (SUBLANES * n_seq)
    return jnp.concatenate([ref[pl.ds(q * SUBLANES * n_j + j, SUBLANES, stride=n_j), :] for q in range(n_seq) for j in range(n_j)],
                           axis=0).astype(dtype)


def _store_tokens(out_ref, ref, n_seq):
    n_j = ref.shape[0] // (SUBLANES * n_seq)
    for q in range(n_seq):
        for s in range(SUBLANES):
            start = (q * SUBLANES + s) * n_j
            out_ref[start:start + n_j, :] = ref[pl.ds(q * SUBLANES * n_j + s, n_j, stride=SUBLANES), :].astype(out_ref.dtype)


def _expand_powers(t_ref, pow_ref):
    for j in range(pow_ref.shape[0] // SUBLANES):
        row = 5 * SUBLANES + j
        pow_ref[j * SUBLANES:(j + 1) * SUBLANES, :] = jnp.broadcast_to(t_ref[row:row + 1, :], (SUBLANES, pow_ref.shape[1]))


def _scan_tile(h_ref, t_ref, pow_ref, carry_ref, up, n_seq, states_ref=None):
    sw = h_ref.shape[1] // 2
    n_j = h_ref.shape[0] // (SUBLANES * n_seq)
    seqs = range(n_seq)

    def rows(g):
        if isinstance(g, int):
            return pl.ds(g * SUBLANES, SUBLANES)
        return pl.ds(pl.multiple_of(g * SUBLANES, SUBLANES), SUBLANES)

    def at(q, j):
        return rows(q * n_j + j)

    def tab(g):
        return t_ref[rows(g), :sw], t_ref[rows(g), sw:]

    def order(i):
        return n_j - 1 - i if up else i

    def cmul_add(xr, xi, ar, ai, yr, yi):
        return xr + ar * yr - ai * yi, xi + ar * yi + ai * yr

    a_re, a_im = tab(0)

    def local_step(i, xs):
        j = order(i)
        out = []
        for q in seqs:
            xr, xi = cmul_add(h_ref[at(q, j), :sw], h_ref[at(q, j), sw:], a_re, a_im, *xs[q])
            h_ref[at(q, j), :sw] = xr
            h_ref[at(q, j), sw:] = xi
            out.append((xr, xi))
        return tuple(out)

    zero = jnp.zeros((SUBLANES, sw), F32)
    ends = lax.fori_loop(0, n_j, local_step, tuple((zero, zero) for _ in seqs))
    out_row = 0 if up else SUBLANES - 1
    in_row = SUBLANES - 1 if up else 0
    one = SUBLANES - 1 if up else 1
    is_in = lax.broadcasted_iota(jnp.int32, (SUBLANES, sw), 0) == in_row
    carried, enters = [], []
    for q in seqs:
        dr, di = ends[q]
        for level, sh in enumerate((1, 2, 4)):
            amount = SUBLANES - sh if up else sh
            dr, di = cmul_add(dr, di, *tab(1 + level), pltpu.roll(dr, amount, 0), pltpu.roll(di, amount, 0))
        cr, ci = carry_ref[rows(q), :sw], carry_ref[rows(q), sw:]
        dr, di = cmul_add(dr, di, *tab(4), cr, ci)
        carry_ref[rows(q), :sw] = jnp.broadcast_to(dr[out_row:out_row + 1], dr.shape)
        carry_ref[rows(q), sw:] = jnp.broadcast_to(di[out_row:out_row + 1], di.shape)
        carried.append((cr, ci))
        enters.append((jnp.where(is_in, cr, pltpu.roll(dr, one, 0)), jnp.where(is_in, ci, pltpu.roll(di, one, 0))))

    def fix_step(i, state):
        j = order(i)
        nows = []
        for q in seqs:
            xr, xi = cmul_add(h_ref[at(q, j), :sw], h_ref[at(q, j), sw:], pow_ref[rows(j), :sw], pow_ref[rows(j), sw:], *enters[q])
            h_ref[at(q, j), :sw] = xr
            h_ref[at(q, j), sw:] = xi
            nows.append((xr, xi))
        if states_ref is None:
            return state
        befores, (acc_r, acc_i) = state
        for q in seqs:
            lr, li = befores[q]
            hr, hi = states_ref[at(q, j), :sw], states_ref[at(q, j), sw:]
            acc_r, acc_i = acc_r + lr * hr + li * hi, acc_i + li * hr - lr * hi
        return tuple(nows), (acc_r, acc_i)

    if states_ref is None:
        lax.fori_loop(0, n_j, fix_step, 0)
        return None
    lasts, (acc_r, acc_i) = lax.fori_loop(0, n_j, fix_step, (tuple((zero, zero) for _ in seqs), (zero, zero)))
    for q in seqs:
        lr = jnp.where(is_in, carried[q][0], pltpu.roll(lasts[q][0], one, 0))
        li = jnp.where(is_in, carried[q][1], pltpu.roll(lasts[q][1], one, 0))
        hr, hi = states_ref[at(q, order(0)), :sw], states_ref[at(q, order(0)), sw:]
        acc_r, acc_i = acc_r + lr * hr + li * hi, acc_i + li * hr - lr * hi
    return acc_r, acc_i


def _phys_tile(t, cfg):
    tpl, nb = cfg["tpl"], cfg["nb"]
    return jnp.where(t < nb * tpl, (t % tpl) * nb + t // tpl, t)


def _s5_block_index(cfg, dirn, adjoint):
    tpl = cfg["tpl"]

    def idx(k):
        if not adjoint:
            return jnp.where(k == 0, tpl, k - 1 if dirn == 0 else tpl - k)
        return jnp.where(k == tpl, tpl, tpl - 1 - k if dirn == 0 else k)
    return idx


RELAY_AT = 0.75


def _grid_ends(grid):
    step = 0
    for i, n in enumerate(grid):
        step = step * n + pl.program_id(i)
    total = math.prod(grid)
    return step == 0, step == int(RELAY_AT * total), step == total - 1


def _s5_fwd(name, u, bmat, cmat, tab, dirn, cfg, cps=None):
    tr, tpl, nb = cfg["tr"], cfg["tpl"], cfg["nb"]
    n_rows, d = u.shape
    ns, _, sw2 = bmat.shape
    block = _s5_block_index(cfg, dirn, False)
    up = dirn == 1
    grid = (ns, tpl + 1)
    br = nb * tr

    def body(*refs):
        copies, (u_ref, b_ref, c_ref, t_ref, h_ref, y_ref, carry_ref, mix_ref, pow_ref) = _carried(cps, 4, 2, 3, refs)
        first, middle, last = _grid_ends(grid)
        _start_all(copies, first)
        _relay_all(copies, middle)

        @pl.when(pl.program_id(1) == 0)
        def _():
            carry_ref[...] = jnp.zeros_like(carry_ref)
            _expand_powers(t_ref, pow_ref)

        h_ref[...] = jnp.dot(u_ref[...], b_ref[...], preferred_element_type=F32)
        _scan_tile(h_ref, t_ref, pow_ref, carry_ref, up, nb)
        mix_ref[...] = jnp.dot(h_ref[...].astype(BF16), c_ref[...], preferred_element_type=F32)
        _store_tokens(y_ref, mix_ref, nb)
        _wait_all(copies, last)

    extra = cps if cps is not None else _Copies([], [])
    return pl.pallas_call(
        body, name=name, grid=grid,
        in_specs=[pl.BlockSpec((br, LANES), lambda s, k: (block(k), s)),
                  pl.BlockSpec((None, LANES, sw2), lambda s, k: (s, 0, 0)),
                  pl.BlockSpec((None, sw2, LANES), lambda s, k: (s, 0, 0)),
                  pl.BlockSpec((None, tab.shape[1], sw2), lambda s, k: (s, 0, 0))] + extra.in_specs,
        out_specs=[pl.BlockSpec((br, sw2), lambda s, k: (block(k), s)),
                   pl.BlockSpec((br, LANES), lambda s, k: (block(k), s))] + extra.out_specs,
        out_shape=[jax.ShapeDtypeStruct((n_rows, ns * sw2), F32), jax.ShapeDtypeStruct((n_rows, d), F32)] + extra.out_shape,
        scratch_shapes=[pltpu.VMEM((nb * SUBLANES, sw2), F32), pltpu.VMEM((br, LANES), F32), pltpu.VMEM((tr, sw2), F32)] + extra.scratch,
        compiler_params=_params(("arbitrary", "arbitrary")))(u, bmat, cmat, tab, *extra.arrays)


def _s5_bwd(name, dy, h, u, cmat_t, bmat_t, tab, dirn, cfg, cps=None):
    tr, tpl, nb = cfg["tr"], cfg["tpl"], cfg["nb"]
    n_rows, d = u.shape
    ns, _, sw2 = cmat_t.shape
    sw = sw2 // 2
    block = _s5_block_index(cfg, dirn, True)
    up = dirn == 0
    grid = (ns, tpl + 1)
    br = nb * tr

    def body(*refs):
        copies, own = _carried(cps, 6, 4, 4, refs)
        dy_ref, h_ref, u_ref, ct_ref, bt_ref, t_ref, du_ref, db_ref, dc_ref, da_ref, lam_ref, carry_ref, mix_ref, pow_ref = own
        grid_first, grid_middle, grid_last = _grid_ends(grid)
        _start_all(copies, grid_first)
        _relay_all(copies, grid_middle)
        first = pl.program_id(1) == 0

        @pl.when(first)
        def _():
            carry_ref[...] = jnp.zeros_like(carry_ref)
            _expand_powers(t_ref, pow_ref)

        dy = dy_ref[...]
        u_mixed = u_ref[...]
        lam_ref[...] = jnp.dot(dy, ct_ref[...], preferred_element_type=F32)
        acc = _scan_tile(lam_ref, t_ref, pow_ref, carry_ref, up, nb, h_ref)
        lam = lam_ref[...].astype(BF16)
        d_b = lax.dot_general(u_mixed, lam, TN, preferred_element_type=F32)
        d_c = lax.dot_general(h_ref[...].astype(BF16), dy, TN, preferred_element_type=F32)
        mix_ref[...] = jnp.dot(lam, bt_ref[...], preferred_element_type=F32)
        _store_tokens(du_ref, mix_ref, nb)

        @pl.when(first)
        def _():
            db_ref[...] = d_b
            dc_ref[...] = d_c
            da_ref[:, :sw] = acc[0]
            da_ref[:, sw:] = acc[1]

        @pl.when(jnp.logical_not(first))
        def _():
            db_ref[...] += d_b
            dc_ref[...] += d_c
            da_ref[:, :sw] += acc[0]
            da_ref[:, sw:] += acc[1]

        _wait_all(copies, grid_last)

    extra = cps if cps is not None else _Copies([], [])
    return pl.pallas_call(
        body, name=name, grid=grid,
        in_specs=[pl.BlockSpec((br, LANES), lambda s, k: (block(k), s)),
                  pl.BlockSpec((br, sw2), lambda s, k: (block(k), s)),
                  pl.BlockSpec((br, LANES), lambda s, k: (block(k), s)),
                  pl.BlockSpec((None, LANES, sw2), lambda s, k: (s, 0, 0)),
                  pl.BlockSpec((None, sw2, LANES), lambda s, k: (s, 0, 0)),
                  pl.BlockSpec((None, tab.shape[1], sw2), lambda s, k: (s, 0, 0))] + extra.in_specs,
        out_specs=[pl.BlockSpec((br, LANES), lambda s, k: (block(k), s)),
                   pl.BlockSpec((None, LANES, sw2), lambda s, k: (s, 0, 0)),
                   pl.BlockSpec((None, sw2, LANES), lambda s, k: (s, 0, 0)),
                   pl.BlockSpec((None, SUBLANES, sw2), lambda s, k: (s, 0, 0))] + extra.out_specs,
        out_shape=[jax.ShapeDtypeStruct((n_rows, d), F32), jax.ShapeDtypeStruct((ns, LANES, sw2), F32),
                   jax.ShapeDtypeStruct((ns, sw2, LANES), F32), jax.ShapeDtypeStruct((ns, SUBLANES, sw2), F32)] + extra.out_shape,
        scratch_shapes=[pltpu.VMEM((br, sw2), F32), pltpu.VMEM((nb * SUBLANES, sw2), F32), pltpu.VMEM((br, LANES), F32),
                        pltpu.VMEM((tr, sw2), F32)] + extra.scratch,
        compiler_params=_params(("arbitrary", "arbitrary")))(dy, h, u, cmat_t, bmat_t, tab, *extra.arrays)


def _s5_tables(ab_re, ab_im, up, conj, ns, n_j):
    def powers_of(base, count):
        out = [base]
        for _ in range(count - 1):
            q_re, q_im = out[-1]
            out.append((q_re * base[0] - q_im * base[1], q_re * base[1] + q_im * base[0]))
        return out

    def spread(q):
        return jnp.broadcast_to(q[:, None, :], (q.shape[0], SUBLANES, q.shape[1]))

    steps = powers_of((ab_re.reshape(ns, -1), (-ab_im if conj else ab_im).reshape(ns, -1)), n_j)
    jumps = powers_of(steps[-1], SUBLANES)
    rows = jnp.arange(SUBLANES)
    blocks = [tuple(spread(q) for q in steps[0])]
    for sh in (1, 2, 4):
        keep = ((rows <= SUBLANES - 1 - sh) if up else (rows >= sh))[None, :, None]
        blocks.append(tuple(jnp.where(keep, q[:, None, :], 0.0) for q in jumps[sh - 1]))
    dist = range(SUBLANES, 0, -1) if up else range(1, SUBLANES + 1)
    blocks.append(tuple(jnp.stack([jumps[dd - 1][part] for dd in dist], axis=1) for part in (0, 1)))
    ordered = steps[::-1] if up else steps
    blocks.append(tuple(jnp.stack([q[part] for q in ordered], axis=1) for part in (0, 1)))
    return jnp.concatenate([jnp.concatenate([b[0] for b in blocks], axis=1), jnp.concatenate([b[1] for b in blocks], axis=1)], axis=2)


def _block_diag(blocks):
    ns, gs, a, b = blocks.shape
    eye = jnp.eye(gs, dtype=blocks.dtype)
    return (blocks[:, :, :, None, :] * eye[None, :, None, :, None]).reshape(ns, gs * a, gs * b)


def _diag_blocks(mat, gs):
    ns, ra, rb = mat.shape
    a, b = ra // gs, rb // gs
    m5 = mat.reshape(ns, gs, a, gs, b)
    eye = jnp.eye(gs, dtype=mat.dtype)
    return jnp.sum(m5 * eye[None, :, None, :, None], axis=3)


def _conv_flags(t, cfg):
    tpl, nb = cfg["tpl"], cfg["nb"]
    latent = t < nb * tpl
    first = jnp.logical_or(jnp.logical_not(latent), t % tpl == 0)
    last = jnp.logical_or(jnp.logical_not(latent), t % tpl == tpl - 1)
    return first, last


def _fill_ext(ext_ref, prev_ref, cur_ref, next_ref, t, cfg, halo):
    first, last = _conv_flags(t, cfg)
    tr = cur_ref.shape[0]
    for p in range(ext_ref.shape[0]):
        lanes = slice(p * LANES, (p + 1) * LANES)
        ext_ref[p, 0:halo, :] = jnp.where(first, 0.0, prev_ref[:, lanes])
        ext_ref[p, halo:halo + tr, :] = cur_ref[:, lanes]
        ext_ref[p, halo + tr:, :] = jnp.where(last, 0.0, next_ref[:, lanes])


CONV_LANES = 4 * LANES


def _conv_specs(tr, n_rows, halo, cw, cfg):
    per = tr // halo
    n_halo = n_rows // halo
    nb = cfg["nb"]
    return [pl.BlockSpec((halo, cw), lambda c, t: (jnp.maximum((_phys_tile(t, cfg) - nb + 1) * per - 1, 0), c)),
            pl.BlockSpec((tr, cw), lambda c, t: (_phys_tile(t, cfg), c)),
            pl.BlockSpec((halo, cw), lambda c, t: (jnp.minimum((_phys_tile(t, cfg) + nb) * per, n_halo - 1), c))]


def _dwconv(name, a, w, cfg):
    tr = cfg["tr"]
    n_rows, d = a.shape
    kw = w.shape[0]
    half = kw // 2
    halo = 2 * SUBLANES
    cw = min(d, CONV_LANES)

    def body(prev_ref, cur_ref, next_ref, w_ref, o_ref, ext_ref):
        _fill_ext(ext_ref, prev_ref, cur_ref, next_ref, pl.program_id(1), cfg, halo)
        for p in range(cw // LANES):
            lanes = slice(p * LANES, (p + 1) * LANES)
            acc = jnp.zeros((tr, LANES), F32)
            for k in range(kw):
                acc = acc + ext_ref[p, pl.ds(halo - half + k, tr), :] * w_ref[k:k + 1, lanes]
            o_ref[:, lanes] = acc

    return pl.pallas_call(body, name=name, grid=(d // cw, n_rows // tr),
                          in_specs=_conv_specs(tr, n_rows, halo, cw, cfg) + [pl.BlockSpec((kw, cw), lambda c, t: (0, c))],
                          out_specs=pl.BlockSpec((tr, cw), lambda c, t: (_phys_tile(t, cfg), c)),
                          out_shape=jax.ShapeDtypeStruct((n_rows, d), F32),
                          scratch_shapes=[pltpu.VMEM((cw // LANES, tr + 2 * halo, LANES), F32)],
                          compiler_params=_params(("arbitrary", "arbitrary")))(a, a, a, w)


def _dwconv_wgrad(name, a, dout, kw, cfg):
    tr = cfg["tr"]
    n_rows, d = a.shape
    half = kw // 2
    halo = 2 * SUBLANES
    cw = min(d, CONV_LANES)

    def body(prev_ref, cur_ref, next_ref, do_ref, o_ref, ext_ref):
        t = pl.program_id(1)
        _fill_ext(ext_ref, prev_ref, cur_ref, next_ref, t, cfg, halo)
        for p in range(cw // LANES):
            lanes = slice(p * LANES, (p + 1) * LANES)
            dout_t = do_ref[:, lanes]
            rows = [jnp.sum(ext_ref[p, pl.ds(halo - half + k, tr), :] * dout_t, axis=0, keepdims=True) for k in range(kw)]
            _accumulate(o_ref.at[:, lanes], jnp.concatenate(rows, axis=0), t == 0)

    return pl.pallas_call(body, name=name, grid=(d // cw, n_rows // tr),
                          in_specs=_conv_specs(tr, n_rows, halo, cw, cfg) + [pl.BlockSpec((tr, cw), lambda c, t: (_phys_tile(t, cfg), c))],
                          out_specs=pl.BlockSpec((kw, cw), lambda c, t: (0, c)),
                          out_shape=jax.ShapeDtypeStruct((kw, d), F32),
                          scratch_shapes=[pltpu.VMEM((cw // LANES, tr + 2 * halo, LANES), F32)],
                          compiler_params=_params(("arbitrary", "arbitrary")))(a, a, a, dout)


def _sincos_1d(pos, dim):
    quarter = dim // 2
    omega = POS_TEMP ** (-jnp.arange(quarter, dtype=F32) / quarter)
    ang = pos[:, None] * omega[None, :]
    return jnp.concatenate([jnp.sin(ang), jnp.cos(ang)], axis=-1)


def _grid_pos_embed(rows, dim):
    row_idx = jnp.repeat(jnp.arange(rows), GRID_W).astype(F32)
    col_idx = jnp.tile(jnp.arange(GRID_W), rows).astype(F32)
    return jnp.concatenate([_sincos_1d(row_idx, dim // 2), _sincos_1d(col_idx, dim // 2)], axis=-1)


def _pack(arrs, row_multiple=SUBLANES):
    flat = jnp.concatenate([a.reshape(-1).astype(F32) for a in arrs])
    pad = (-flat.shape[0]) % (row_multiple * LANES)
    return jnp.pad(flat, (0, pad)).reshape(-1, LANES)


def _unpack(buf, shapes):
    flat = buf.reshape(-1)
    out, pos = [], 0
    for shp in shapes:
        n = math.prod(shp)
        out.append(flat[pos:pos + n].reshape(shp))
        pos += n
    return out


def _unpack_gathered(buf, shapes):
    flat = buf.reshape(N_DEV, -1)
    out, pos = [], 0
    for shp in shapes:
        n = math.prod(shp)
        part = flat[:, pos:pos + n].reshape((N_DEV,) + tuple(shp))
        out.append(jnp.moveaxis(part, 0, -2).reshape(tuple(shp[:-1]) + (N_DEV * shp[-1],)))
        pos += n
    return out


WEIGHTS = ("c_ctx", "w_ada", "b_ada", "ln_gain", "ln_bias", "s5_lam_re", "s5_lam_im", "s5_log_dt", "s5_b_re", "s5_b_im",
           "s5_c_re", "s5_c_im", "s5_d", "s5_w_glu", "s5_b_glu", "cv_w_pw1", "cv_b_pw1", "cv_w_dw", "cv_b_dw", "cv_ln_g",
           "cv_ln_b", "cv_w_pw2", "cv_b_pw2", "mlp_w1", "mlp_w2")
SHARDED_SMALL = ("ln_gain", "ln_bias", "cv_b_pw1", "cv_w_dw", "cv_b_dw", "cv_ln_g", "cv_ln_b", "cv_b_pw2")
REPLICATED_SMALL = ("s5_lam_re", "s5_lam_im", "s5_log_dt", "s5_b_re", "s5_b_im", "s5_c_re", "s5_c_im", "s5_d", "s5_b_glu")
NATIVE_SMALL = ("s5_lam_re", "s5_lam_im", "s5_b_re", "s5_b_im", "s5_c_re", "s5_c_im")
BIG = ("mlp_w1", "mlp_w2", "s5_w_glu", "cv_w_pw1", "cv_w_pw2")


def _step(a):
    x, c, ctx = a["x"], a["c"], a["ctx"]
    nb, seq, d = x.shape
    lc = ctx.shape[1]
    nl = a["w_ada"].shape[0]
    tr = lc
    tpl = seq // tr
    cfg = {"tr": tr, "tpl": tpl, "nb": nb}
    n_rows = nb * (seq + lc)
    alpha = (2.0 * nl) ** 0.25
    me = 4 * lax.axis_index("x") + 2 * lax.axis_index("y") + lax.axis_index("c")
    n_grp, n_state = a["s5_lam_re"].shape[2:]
    ch = a["s5_b_re"].shape[-1]
    gs = LANES // ch
    ns = d // LANES
    tm = 2 * tr if n_rows % (2 * tr) == 0 else tr
    tm_big = n_rows // 3 if n_rows % (3 * 2 * SUBLANES) == 0 else tm
    f_sub1_s5, f_sub1_cv, f_sub2 = _make_sub1_s5(alpha), _make_sub1_cv(alpha), _make_sub2(alpha)

    def layer_weights(i):
        mixer = [("s5_w_glu", i // 2, 1)] if i % 2 == 0 else [("cv_w_pw1", i // 2, 1), ("cv_w_pw2", i // 2, 0)]
        return mixer + [("mlp_w1", i, 1), ("mlp_w2", i, 0)]

    weights, wgrads, received = {}, {}, {}
    small_all, c_all = _exchange("gather_small", [_pack([a[n] for n in SHARDED_SMALL]), c], ["slot", "slot"])
    full = dict(zip(SHARDED_SMALL, _unpack_gathered(small_all, [a[n].shape for n in SHARDED_SMALL])))
    c_all = c_all.reshape(N_DEV * nb, d)
    cond_rows = N_DEV * nb + SUBLANES
    cc = jnp.concatenate([c_all, a["c_ctx"][None], jnp.zeros((SUBLANES - 1, d), F32)], axis=0)

    n_ada = a["w_ada"].shape[2]
    b_loc = lax.dynamic_slice(a["b_ada"], (0, me * n_ada), (nl, n_ada))[:, None, :]
    mod_cols = _ada_fwd(cc, a["w_ada"], b_loc)
    mod_all = _exchange("gather_mod", [mod_cols.reshape(nl * cond_rows, n_ada)], ["slot"])[0].reshape(N_DEV, nl, cond_rows, n_ada)
    mod_mine = jnp.concatenate([lax.dynamic_slice(mod_all, (0, 0, nb * me, 0), (N_DEV, nl, nb, n_ada)),
                                mod_all[:, :, N_DEV * nb:N_DEV * nb + 1]], axis=2)
    mod = jnp.transpose(mod_mine, (1, 2, 0, 3)).reshape(nl, nb + 1, 6, 1, d)

    def seg(i, q):
        return mod[i, :, q]

    zero_seg = jnp.zeros((nb + 1, 1, d), F32)

    def vec(v):
        return v.reshape(1, -1)

    pos = _grid_pos_embed(seq // GRID_W, d)
    def latent_rows(v):
        return jnp.transpose(v.reshape(nb, tpl, tr, d), (1, 0, 2, 3)).reshape(nb * seq, d)

    xc = jnp.concatenate([latent_rows(x), ctx.reshape(nb * lc, d)], axis=0)
    pos_rows = jnp.concatenate([latent_rows(jnp.broadcast_to(pos[None], (nb, seq, d))), jnp.zeros((nb * lc, d), F32)], axis=0)
    x_cur, h_cur = _rowwise("entry", _f_entry, [xc, pos_rows], [seg(0, 0), seg(0, 1)], [], [(d, F32), (d, BF16)], [], [], cfg, (1,))
    saved = []
    for i in range(nl):
        j = i // 2
        sv = {"x": x_cur, "h": h_cur}
        sh1, sc1, g1, sh2, sc2, g2 = (seg(i, q) for q in range(6))
        gain0, bias0, gain1, bias1 = (vec(full["ln_gain"][i, 0]), vec(full["ln_bias"][i, 0]),
                                      vec(full["ln_gain"][i, 1]), vec(full["ln_bias"][i, 1]))
        if i % 2 == 0:
            lam_re, lam_im = a["s5_lam_re"][j], a["s5_lam_im"][j]
            log_dt = a["s5_log_dt"][j][:, :, None]
            b_re_t = jnp.transpose(a["s5_b_re"][j], (0, 3, 1, 2))
            b_im_t = jnp.transpose(a["s5_b_im"][j], (0, 3, 1, 2))
            sv["prep_in"] = (lam_re, lam_im, log_dt, b_re_t, b_im_t)
            ab_re, ab_im, bb_re, bb_im = _s5_prep(f"s5_prep{i}", *sv["prep_in"])
            sv["ab"] = (ab_re, ab_im)
            ys = []
            for dirn in range(2):
                def blocks(t):
                    return jnp.transpose(t, (1, 0, 2)).reshape(ns, gs, ch, n_state)
                bmat = jnp.concatenate([_block_diag(blocks(bb_re[dirn])), _block_diag(blocks(bb_im[dirn]))], axis=2).astype(BF16)
                c_re_t = jnp.transpose(a["s5_c_re"][j, dirn], (0, 2, 1)).reshape(ns, gs, n_state, ch)
                c_im_t = jnp.transpose(a["s5_c_im"][j, dirn], (0, 2, 1)).reshape(ns, gs, n_state, ch)
                cmat = jnp.concatenate([_block_diag(c_re_t), -_block_diag(c_im_t)], axis=1).astype(BF16)
                tab = _s5_tables(ab_re[dirn], ab_im[dirn], dirn == 1, False, ns, tr // SUBLANES)
                group = layer_weights(i + dirn)
                cps = _Copies([a[n][idx].astype(BF16) for n, idx, _ in group], [axis for _, _, axis in group])
                h_states, y_dir, *gathered = _s5_fwd(f"s5_fwd{i}_{dirn}", h_cur, bmat, cmat, tab, dirn, cfg, cps)
                weights.update({(n, idx): w[None] for (n, idx, _), w in zip(group, gathered)})
                sv[f"mats{dirn}"] = (jnp.transpose(bmat, (0, 2, 1)), jnp.transpose(cmat, (0, 2, 1)))
                sv[f"states{dirn}"] = h_states
                ys.append(y_dir)
            sv["y"] = ys
            dsk = vec(a["s5_d"][j])
            z = _rowwise(f"gelu{i}", _f_gelu, [x_cur, ys[0], ys[1]], [sh1, sc1], [dsk], [(d, BF16)], [], [], cfg)[0]
            zz = _mm_nn(f"glu{i}", z, weights["s5_w_glu", j], 0, tm_big, min(2 * d, 512))[0]
            bglu = vec(a["s5_b_glu"][j])
            x1, h2 = _rowwise(f"sub1_{i}", f_sub1_s5, [x_cur, zz], [g1, sh2, sc2], [bglu, gain0, bias0],
                              [(d, F32), (d, BF16)], [], [], cfg)
            sv.update(z=z, zz=zz)
        else:
            zz = _mm_nn(f"pw1_{i}", h_cur, weights["cv_w_pw1", j], 0, tm_big, min(2 * d, 512))[0]
            bpw1 = vec(full["cv_b_pw1"][j])
            act = _rowwise(f"cvglu{i}", _f_cvglu, [zz], [], [bpw1], [(d, F32)], [], [], cfg)[0]
            w_dw = full["cv_w_dw"][j]
            cv = _dwconv(f"dwconv{i}", act, w_dw, cfg)
            bdw, lng, lnb = vec(full["cv_b_dw"][j]), vec(full["cv_ln_g"][j]), vec(full["cv_ln_b"][j])
            s_act = _rowwise(f"cvln{i}", _f_cvln, [cv], [], [bdw, lng, lnb], [(d, BF16)], [], [], cfg)[0]
            mm = _mm_nn(f"pw2_{i}", s_act, weights["cv_w_pw2", j], 0, tm_big, d)[0]
            bpw2 = vec(full["cv_b_pw2"][j])
            x1, h2 = _rowwise(f"sub1_{i}", f_sub1_cv, [x_cur, mm], [g1, sh2, sc2], [bpw2, gain0, bias0],
                              [(d, F32), (d, BF16)], [], [], cfg)
            sv.update(zz=zz, act=act, cv=cv, s_act=s_act, mm=mm, w_dw=w_dw)
        dff = weights["mlp_w1", i].shape[2]
        p_act, r_act = _mm_nn(f"mlp1_{i}", h2, weights["mlp_w1", i], 0, tm_big, min(dff, 1024), (BF16, BF16),
                              lambda acc: (jnp.square(jnp.maximum(acc, 0.0)), jnp.maximum(acc, 0.0)))
        m_out = _mm_nn(f"mlp2_{i}", p_act, weights["mlp_w2", i], 0, tm, d)[0]
        shn, scn = (seg(i + 1, 0), seg(i + 1, 1)) if i + 1 < nl else (zero_seg, zero_seg)
        x2, hn = _rowwise(f"sub2_{i}", f_sub2, [x1, m_out], [g2, shn, scn], [gain1, bias1], [(d, F32), (d, BF16)], [], [], cfg,
                          (1,) if (i + 1) % 2 == 0 and i + 1 < nl else ())
        sv.update(x1=x1, h2=h2, p=p_act, r=r_act, m=m_out, shn=shn, scn=scn)
        saved.append(sv)
        x_cur, h_cur = x2, hn

    target = jnp.concatenate([latent_rows(a["loss_target"]), jnp.zeros((nb * lc, d), F32)], axis=0)
    mask = jnp.concatenate([jnp.ones((nb, 1, d), F32), jnp.zeros((1, 1, d), F32)], axis=0)

    def f_loss(xf, tgt, msk):
        err = (xf - tgt) * msk
        part = 0.5 * jnp.sum(jnp.square(err), axis=(0, 1), keepdims=True) / d
        return err / d, jnp.broadcast_to(part, (1, LANES))

    dx_final, loss_part = _rowwise("loss", f_loss, [x_cur, target], [mask], [], [(d, F32)], [], [LANES], cfg)
    loss = lax.psum(loss_part[0, 0], ("x", "y", "c"))

    grads = {n: [None] * a[n].shape[0] for n in WEIGHTS if n not in ("c_ctx", "w_ada", "b_ada")}
    dmod = [[None] * 6 for _ in range(nl)]

    def add_mod(i, q, val):
        dmod[i][q] = val if dmod[i][q] is None else dmod[i][q] + val

    dx_parts, dh_parts = [dx_final], []
    for i in reversed(range(nl)):
        j = i // 2
        sv = saved[i]
        sh1, sc1, g1, sh2, sc2, g2 = (seg(i, q) for q in range(6))
        gain0, bias0, gain1, bias1 = (vec(full["ln_gain"][i, 0]), vec(full["ln_bias"][i, 0]),
                                      vec(full["ln_gain"][i, 1]), vec(full["ln_bias"][i, 1]))
        bwd = _vjp_fn(f_sub2, 2, (len(dx_parts), len(dh_parts)), (0, 1, 2, 3, 4, 5, 6))
        dx1, dm, dg2, dshn, dscn, dgain1, dbias1 = _rowwise(
            f"sub2_bwd{i}", bwd, [sv["x1"], sv["m"]] + dx_parts + dh_parts, [g2, sv["shn"], sv["scn"]], [gain1, bias1],
            [(d, F32), (d, BF16)], [d, d, d], [d, d], cfg)
        add_mod(i, 5, dg2)
        if i + 1 < nl:
            add_mod(i + 1, 0, dshn)
            add_mod(i + 1, 1, dscn)
        da = _mm_nt(f"mlp2_dgrad{i}", dm, weights["mlp_w2", i], 0, tm_big, min(dff, 1024), [sv["r"]], BF16,
                    lambda acc, r: (acc * 2.0 * r,))
        wgrads["mlp_w2", i] = _mm_wgrad_rows(f"mlp2_wgrad{i}", sv["p"], dm, tm_big)
        wgrads["mlp_w1", i] = _mm_wgrad_cols(f"mlp1_wgrad{i}", sv["h2"], da, tm_big)
        dh2 = _mm_nt(f"mlp1_dgrad{i}", da, weights["mlp_w1", i], 0, tm, d)
        if i % 2 == 0:
            bglu = vec(a["s5_b_glu"][j])
            bwd = _vjp_fn(f_sub1_s5, 2, (1, 1), (0, 1, 2, 3, 4, 5, 6, 7))
            dxa, dzz, dg1, dsh2, dsc2, dbglu, dgain0, dbias0 = _rowwise(
                f"sub1_bwd{i}", bwd, [sv["x"], sv["zz"], dx1, dh2], [g1, sh2, sc2], [bglu, gain0, bias0],
                [(d, F32), (2 * d, BF16)], [d, d, d], [2 * d, d, d], cfg)
            grads["s5_b_glu"][j] = dbglu[0]
            wgrads["s5_w_glu", j] = _mm_wgrad_cols(f"glu_wgrad{i}", sv["z"], dzz, tm_big)
            dz = _mm_nt(f"glu_dgrad{i}", dzz, weights["s5_w_glu", j], 0, tm, d)
            dsk = vec(a["s5_d"][j])
            bwd = _vjp_fn(_f_gelu, 3, (1,), (0, 1, 3, 4, 5))
            dxb, dy, dsh1, dsc1, ddsk = _rowwise(f"gelu_bwd{i}", bwd, [sv["x"], sv["y"][0], sv["y"][1], dz], [sh1, sc1], [dsk],
                                                 [(d, F32), (d, BF16)], [d, d], [d], cfg, (1,))
            grads["s5_d"][j] = ddsk[0]
            add_mod(i, 0, dsh1)
            add_mod(i, 1, dsc1)
            ab_re, ab_im = sv["ab"]
            dus, d_ab_re, d_ab_im, d_bb_re, d_bb_im, d_c_re, d_c_im = [], [], [], [], [], [], []
            for dirn in range(2):
                bmat_t, cmat_t = sv[f"mats{dirn}"]
                tab = _s5_tables(ab_re[dirn], ab_im[dirn], dirn == 0, True, ns, tr // SUBLANES)
                group = layer_weights(i + 1 - dirn)
                cps = _Copies([wgrads[n, idx] for n, idx, _ in group], ["scatter"] * len(group))
                du, d_b, d_c, d_a, *parts = _s5_bwd(f"s5_bwd{i}_{dirn}", dy, sv[f"states{dirn}"], sv["h"], cmat_t, bmat_t, tab,
                                                    dirn, cfg, cps)
                received.update({(n, idx): p for (n, idx, _), p in zip(group, parts)})
                dus.append(du)
                sw = d_a.shape[2] // 2
                d_a = jnp.sum(d_a, axis=1)
                d_ab_re.append(d_a[:, :sw].reshape(n_grp, n_state))
                d_ab_im.append(d_a[:, sw:].reshape(n_grp, n_state))

                def unblock_b(t):
                    return jnp.transpose(_diag_blocks(t, gs).reshape(n_grp, ch, n_state), (1, 0, 2))

                def unblock_c(t):
                    return jnp.transpose(_diag_blocks(t, gs).reshape(n_grp, n_state, ch), (0, 2, 1))
                d_bb_re.append(unblock_b(d_b[:, :, :sw]))
                d_bb_im.append(unblock_b(d_b[:, :, sw:]))
                d_c_re.append(unblock_c(d_c[:, :sw]))
                d_c_im.append(-unblock_c(d_c[:, sw:]))
            g_lre, g_lim, g_ldt, g_bre, g_bim = _s5_prep_bwd(
                f"s5_prep_bwd{i}", *sv["prep_in"], (jnp.stack(d_ab_re), jnp.stack(d_ab_im), jnp.stack(d_bb_re), jnp.stack(d_bb_im)))
            grads["s5_lam_re"][j], grads["s5_lam_im"][j], grads["s5_log_dt"][j] = g_lre, g_lim, g_ldt[:, :, 0]
            grads["s5_b_re"][j] = jnp.transpose(g_bre, (0, 2, 3, 1))
            grads["s5_b_im"][j] = jnp.transpose(g_bim, (0, 2, 3, 1))
            grads["s5_c_re"][j], grads["s5_c_im"][j] = jnp.stack(d_c_re), jnp.stack(d_c_im)
            dx_parts, dh_parts = [dxa, dxb], dus
        else:
            bpw2 = vec(full["cv_b_pw2"][j])
            bwd = _vjp_fn(f_sub1_cv, 2, (1, 1), (0, 1, 2, 3, 4, 5, 6, 7))
            dxa, dmm, dg1, dsh2, dsc2, dbpw2, dgain0, dbias0 = _rowwise(
                f"sub1_bwd{i}", bwd, [sv["x"], sv["mm"], dx1, dh2], [g1, sh2, sc2], [bpw2, gain0, bias0],
                [(d, F32), (d, BF16)], [d, d, d], [d, d, d], cfg)
            grads["cv_b_pw2"][j] = dbpw2[0]
            wgrads["cv_w_pw2", j] = _mm_wgrad_rows(f"pw2_wgrad{i}", sv["s_act"], dmm, tm_big)
            ds = _mm_nt(f"pw2_dgrad{i}", dmm, weights["cv_w_pw2", j], 0, tm_big, d)
            bdw, lng, lnb = vec(full["cv_b_dw"][j]), vec(full["cv_ln_g"][j]), vec(full["cv_ln_b"][j])
            bwd = _vjp_fn(_f_cvln, 1, (1,), (0, 1, 2, 3))
            dcv, dbdw, dlng, dlnb = _rowwise(f"cvln_bwd{i}", bwd, [sv["cv"], ds], [], [bdw, lng, lnb], [(d, F32)], [], [d, d, d], cfg)
            grads["cv_b_dw"][j], grads["cv_ln_g"][j], grads["cv_ln_b"][j] = dbdw[0], dlng[0], dlnb[0]
            dact = _dwconv(f"dwconv_bwd{i}", dcv, sv["w_dw"][::-1], cfg)
            grads["cv_w_dw"][j] = _dwconv_wgrad(f"dwconv_wgrad{i}", sv["act"], dcv, sv["w_dw"].shape[0], cfg)
            bpw1 = vec(full["cv_b_pw1"][j])
            bwd = _vjp_fn(_f_cvglu, 1, (1,), (0, 1))
            dzz, dbpw1 = _rowwise(f"cvglu_bwd{i}", bwd, [sv["zz"], dact], [], [bpw1], [(2 * d, BF16)], [], [2 * d], cfg)
            grads["cv_b_pw1"][j] = dbpw1[0]
            wgrads["cv_w_pw1", j] = _mm_wgrad_cols(f"pw1_wgrad{i}", sv["h"], dzz, tm_big)
            dh = _mm_nt(f"pw1_dgrad{i}", dzz, weights["cv_w_pw1", j], 0, tm, d)
            dx_parts, dh_parts = [dxa], [dh]
        grads["ln_gain"][i] = jnp.stack([dgain0[0], dgain1[0]])
        grads["ln_bias"][i] = jnp.stack([dbias0[0], dbias1[0]])
        add_mod(i, 2, dg1)
        add_mod(i, 3, dsh2)
        add_mod(i, 4, dsc2)
    bwd = _vjp_fn(_f_entry, 2, (len(dx_parts), len(dh_parts)), (0, 2, 3))
    dxc, dsh1, dsc1 = _rowwise("entry_bwd", bwd, [xc, pos_rows] + dx_parts + dh_parts, [seg(0, 0), seg(0, 1)], [],
                               [(d, F32)], [d, d], [], cfg)
    add_mod(0, 0, dsh1)
    add_mod(0, 1, dsc1)
    grad_x = jnp.transpose(dxc[:nb * seq].reshape(tpl, nb, tr, d), (1, 0, 2, 3)).reshape(nb, seq, d)

    dmod_loc = jnp.stack([jnp.concatenate([q[:, 0] for q in dmod[i]], axis=1) for i in range(nl)])
    dmod_all = _exchange("gather_dmod", [dmod_loc.reshape(nl * (nb + 1), 6 * d)], ["slot"])[0].reshape(N_DEV, nl, nb + 1, 6 * d)
    mine = lax.dynamic_slice(dmod_all, (0, 0, 0, me * n_ada), (N_DEV, nl, nb + 1, n_ada))
    dmod_rows = jnp.transpose(mine[:, :, :nb], (1, 0, 2, 3)).reshape(nl, N_DEV * nb, n_ada)
    dmod_rows = jnp.concatenate([dmod_rows, jnp.zeros((nl, SUBLANES, n_ada), F32)], axis=1)
    g_w_ada, _, dcond = _ada_bwd(cc, a["w_ada"], dmod_rows, mine[:, :, nb:])
    g_b_ada = _sum_lead("b_ada_sum", jnp.transpose(dmod_all, (0, 2, 1, 3)).reshape(N_DEV * (nb + 1), nl, 6 * d), nl)

    small_names = SHARDED_SMALL + REPLICATED_SMALL
    small_full = [jnp.stack(grads[n]) for n in small_names]
    small_packed = _pack(small_full, N_DEV * SUBLANES)
    small_parts, dcond_all = _exchange("scatter_small_grads", [small_packed.reshape(N_DEV, -1, LANES), dcond[N_DEV * nb:N_DEV * nb + 1]],
                                       ["scatter", "slot"])
    g_c_ctx = _cctx_grad(dcond_all, a["c_ctx"][None])[0]
    small_part = _sum_lead("small_grad_sum", small_parts, _row_tile(small_parts.shape[1], 512))
    small_sum = _exchange("gather_small_sum", [small_part], ["slot"])[0]
    small_g = dict(zip(small_names, _unpack(small_sum, [g.shape for g in small_full])))
    for n in SHARDED_SMALL:
        width = a[n].shape[-1]
        start = (0,) * (small_g[n].ndim - 1) + (me * width,)
        small_g[n] = lax.dynamic_slice(small_g[n], start, a[n].shape)
    small_g["c_ctx"], small_g["b_ada"] = g_c_ctx, g_b_ada

    out = {}

    def update(n, parts):
        shp = a[n].shape
        cols = parts.shape[-1]
        rows = parts.shape[1]
        res = _adamw(f"adamw_{n}", parts, a[n].reshape(rows, cols), a["m_" + n].reshape(rows, cols), a["v_" + n].reshape(rows, cols),
                     _row_tile(rows, max(SUBLANES, 131072 // cols)))
        out[n] = [r.reshape(shp) for r in res]

    for n in BIG:
        rows, cols = a[n].shape[1:]
        bufs = [lax.empty(a[n].shape, F32) for _ in range(4)]
        for idx in range(a[n].shape[0]):
            bufs = _adamw_layer(f"adamw_{n}{idx}", received[n, idx], a[n], a["m_" + n], a["v_" + n], bufs, idx,
                                _row_tile(rows, max(SUBLANES, 131072 // cols)))
        out[n] = bufs
    update("w_ada", g_w_ada.reshape(1, -1, n_ada))
    for n in NATIVE_SMALL:
        out[n] = [small_g[n], *_adamw_native(f"adamw_{n}", small_g[n], a[n], a["m_" + n], a["v_" + n])]
    small_all_names = ("c_ctx", "b_ada") + tuple(n for n in small_names if n not in NATIVE_SMALL)
    packed = [_pack([src[n] for n in small_all_names]) for src in
              (small_g, a, {n: a["m_" + n] for n in small_all_names}, {n: a["v_" + n] for n in small_all_names})]
    res = _adamw("adamw_small", packed[0][None], packed[1], packed[2], packed[3], _row_tile(packed[0].shape[0], 512))
    shapes = [a[n].shape for n in small_all_names]
    for n, vals in zip(small_all_names, zip(*[_unpack(r, shapes) for r in res])):
        out[n] = list(vals)
    return (loss, grad_x, *[out[n][0] for n in WEIGHTS], *[out[n][1] for n in WEIGHTS],
            *[out[n][2] for n in WEIGHTS], *[out[n][3] for n in WEIGHTS])


def kernel(x, c, ctx, c_ctx, w_ada, b_ada, ln_gain, ln_bias, s5_lam_re, s5_lam_im, s5_log_dt, s5_b_re, s5_b_im, s5_c_re, s5_c_im, s5_d, s5_w_glu, s5_b_glu, cv_w_pw1, cv_b_pw1, cv_w_dw, cv_b_dw, cv_ln_g, cv_ln_b, cv_w_pw2, cv_b_pw2, mlp_w1, mlp_w2, loss_target, m_c_ctx, m_w_ada, m_b_ada, m_ln_gain, m_ln_bias, m_s5_lam_re, m_s5_lam_im, m_s5_log_dt, m_s5_b_re, m_s5_b_im, m_s5_c_re, m_s5_c_im, m_s5_d, m_s5_w_glu, m_s5_b_glu, m_cv_w_pw1, m_cv_b_pw1, m_cv_w_dw, m_cv_b_dw, m_cv_ln_g, m_cv_ln_b, m_cv_w_pw2, m_cv_b_pw2, m_mlp_w1, m_mlp_w2, v_c_ctx, v_w_ada, v_b_ada, v_ln_gain, v_ln_bias, v_s5_lam_re, v_s5_lam_im, v_s5_log_dt, v_s5_b_re, v_s5_b_im, v_s5_c_re, v_s5_c_im, v_s5_d, v_s5_w_glu, v_s5_b_glu, v_cv_w_pw1, v_cv_b_pw1, v_cv_w_dw, v_cv_b_dw, v_cv_ln_g, v_cv_ln_b, v_cv_w_pw2, v_cv_b_pw2, v_mlp_w1, v_mlp_w2):
    return _step(dict(locals()))
```

```python
import functools
import math

import jax
import jax.numpy as jnp
from jax import lax
from jax.experimental import pallas as pl
from jax.experimental.pallas import tpu as pltpu

F32 = jnp.float32
BF16 = jnp.bfloat16
N_DEV = 8
LANES = 128
SUBLANES = 8
VMEM_LIMIT = 56 * 1024 * 1024
GRID_W = 64
POS_TEMP = 10000.0
LN_EPS = 1e-5
LAMBDA_RE_MAX = -1e-4
ADAM_LR, ADAM_B1, ADAM_B2, ADAM_EPS, ADAM_WD, ADAM_STEP = 0.001, 0.9, 0.999, 1e-08, 0.01, 10
MESH = pl.DeviceIdType.MESH


def _params(sem):
    return pltpu.CompilerParams(dimension_semantics=sem, vmem_limit_bytes=VMEM_LIMIT)


def _accumulate(ref, val, first):
    @pl.when(first)
    def _():
        ref[...] = val

    @pl.when(jnp.logical_not(first))
    def _():
        ref[...] += val


def _rowwise(name, fn, rows, segs, vecs, row_outs, seg_accs, vec_accs, cfg, interleaved=()):
    tr, tpl, nb = cfg["tr"], cfg["tpl"], cfg["nb"]
    n_rows = rows[0].shape[0]
    nt = n_rows // tr
    nr, ns, nv = len(rows), len(segs), len(vecs)
    nro, nsa, nva = len(row_outs), len(seg_accs), len(vec_accs)

    def seg_of(t):
        return jnp.minimum(t // tpl, nb)

    def body(*refs):
        t = pl.program_id(0)
        ins, outs, mix_refs = refs[:nr + ns + nv], refs[nr + ns + nv:nr + ns + nv + nro + nsa + nva], refs[nr + ns + nv + nro + nsa + nva:]
        vals = [r[...] for r in ins[:nr]] + [r[0] for r in ins[nr:nr + ns]] + [r[...] for r in ins[nr + ns:]]
        res = fn(*vals)
        for idx, (o, v) in enumerate(zip(outs[:nro], res[:nro])):
            if idx in interleaved:
                mix_ref = mix_refs[interleaved.index(idx)]
                for p in range(mix_ref.shape[0]):
                    mix_ref[p] = v[:, p * LANES:(p + 1) * LANES].astype(F32)
                o[...] = jnp.concatenate([_interleave_rows(mix_ref.at[p], 1, o.dtype) for p in range(mix_ref.shape[0])], axis=1)
            else:
                o[...] = v.astype(o.dtype)
        first_seg = jnp.logical_or(t == 0, seg_of(t) != seg_of(jnp.maximum(t - 1, 0)))
        for o, v in zip(outs[nro:nro + nsa], res[nro:nro + nsa]):
            _accumulate(o.at[0], v, first_seg)
        for o, v in zip(outs[nro + nsa:], res[nro + nsa:]):
            _accumulate(o, v, t == 0)

    in_specs = ([pl.BlockSpec((tr, a.shape[1]), lambda t: (_phys_tile(t, cfg), 0)) for a in rows]
                + [pl.BlockSpec((1, 1, a.shape[2]), lambda t: (seg_of(t), 0, 0)) for a in segs]
                + [pl.BlockSpec((1, a.shape[1]), lambda t: (0, 0)) for a in vecs])
    out_specs = ([pl.BlockSpec((tr, c), lambda t: (_phys_tile(t, cfg), 0)) for c, _ in row_outs]
                 + [pl.BlockSpec((1, 1, c), lambda t: (seg_of(t), 0, 0)) for c in seg_accs]
                 + [pl.BlockSpec((1, c), lambda t: (0, 0)) for c in vec_accs])
    out_shape = ([jax.ShapeDtypeStruct((n_rows, c), dt) for c, dt in row_outs]
                 + [jax.ShapeDtypeStruct((nb + 1, 1, c), F32) for c in seg_accs]
                 + [jax.ShapeDtypeStruct((1, c), F32) for c in vec_accs])
    return pl.pallas_call(body, name=name, grid=(nt,), in_specs=in_specs, out_specs=out_specs, out_shape=out_shape,
                          scratch_shapes=[pltpu.VMEM((row_outs[idx][0] // LANES, tr, LANES), F32) for idx in interleaved],
                          compiler_params=_params(("arbitrary",)))(*rows, *segs, *vecs)


def _vjp_fn(fn, n_row, cot_groups, want):
    n_cot = sum(cot_groups)

    def bwd(*args):
        primals = [a.astype(F32) for a in args[:n_row] + args[n_row + n_cot:]]
        outs, vjp = jax.vjp(fn, *primals)
        cots, pos = [], n_row
        for n, o in zip(cot_groups, outs):
            cot = jnp.zeros_like(o)
            for part in args[pos:pos + n]:
                cot = cot + part.astype(F32)
            cots.append(cot)
            pos += n
        grads = vjp(tuple(cots))
        return tuple(grads[i] for i in want)
    return bwd


def _ln(r, g, b):
    mu = jnp.mean(r, axis=-1, keepdims=True)
    var = jnp.mean(jnp.square(r - mu), axis=-1, keepdims=True)
    return (r - mu) * lax.rsqrt(var + LN_EPS) * g + b


def _glu(zz, bias):
    d = zz.shape[1] // 2
    return (zz[:, :d] + bias[:, :d]) * jax.nn.sigmoid(zz[:, d:] + bias[:, d:])


def _f_entry(xc, pos, sh, sc):
    x0 = xc + pos
    return x0, x0 * (1 + sc) + sh


def _f_gelu(x, y0, y1, sh, sc, dsk):
    u = x * (1 + sc) + sh
    y = dsk * u + y0 + y1
    return (0.5 * y * (1.0 + lax.erf(y * (2.0 ** -0.5))),)


def _make_sub1_s5(alpha):
    def f(x, zz, g1, sh2, sc2, bglu, gain, bias):
        x1 = _ln(alpha * x + g1 * _glu(zz, bglu), gain, bias)
        return x1, x1 * (1 + sc2) + sh2
    return f


def _make_sub1_cv(alpha):
    def f(x, mm, g1, sh2, sc2, bpw2, gain, bias):
        x1 = _ln(alpha * x + g1 * (mm + bpw2), gain, bias)
        return x1, x1 * (1 + sc2) + sh2
    return f


def _make_sub2(alpha):
    def f(x1, m, g2, shn, scn, gain, bias):
        x2 = _ln(alpha * x1 + g2 * m, gain, bias)
        return x2, x2 * (1 + scn) + shn
    return f


def _f_cvglu(zz, bpw1):
    return (_glu(zz, bpw1),)


def _f_cvln(cv, bdw, lng, lnb):
    return (jax.nn.silu(_ln(cv + bdw, lng, lnb)),)


def _matmul(name, a, b, extras, grid, a_spec, b_spec, extra_specs, o_specs, out_shape, dims, red_axis, epi, sem, acc_shape=None):
    n_extra = len(extras)
    n_out = len(out_shape)
    if acc_shape is None:
        acc_shape = tuple(s for s in o_specs[0].block_shape if s is not None)

    def body(*refs):
        a_ref, b_ref = refs[0], refs[1]
        ex = refs[2:2 + n_extra]
        outs = refs[2 + n_extra:2 + n_extra + n_out]
        prod = lax.dot_general(a_ref[...], b_ref[...], dims, preferred_element_type=F32)

        def finish(acc):
            res = epi(acc, *[e[...] for e in ex]) if epi is not None else (acc,)
            for o, v in zip(outs, res):
                o[...] = v.astype(o.dtype)

        if red_axis is None:
            finish(prod)
        else:
            acc_ref = refs[-1]
            k = pl.program_id(red_axis)
            nk = pl.num_programs(red_axis)

            @pl.when(k == 0)
            def _():
                acc_ref[...] = prod

            @pl.when(k > 0)
            def _():
                acc_ref[...] += prod

            @pl.when(k == nk - 1)
            def _():
                finish(acc_ref[...])

    scratch = [] if red_axis is None else [pltpu.VMEM(acc_shape, F32)]
    res = pl.pallas_call(body, name=name, grid=grid, in_specs=[a_spec, b_spec] + list(extra_specs),
                         out_specs=list(o_specs), out_shape=list(out_shape), scratch_shapes=scratch,
                         compiler_params=_params(sem))(a, b, *extras)
    return res


NN = (((1,), (0,)), ((), ()))
NT = (((1,), (1,)), ((), ()))
TN = (((0,), (0,)), ((), ()))


def _mm_nn(name, a, w3, layer, tm, tn, out_dtypes=(F32,), epi=None):
    m, k = a.shape
    n = w3.shape[2]
    return _matmul(name, a, w3, (), (n // tn, m // tm),
                   pl.BlockSpec((tm, k), lambda j, i: (i, 0)), pl.BlockSpec((None, k, tn), lambda j, i: (layer, 0, j)), (),
                   [pl.BlockSpec((tm, tn), lambda j, i: (i, j)) for _ in out_dtypes],
                   [jax.ShapeDtypeStruct((m, n), dt) for dt in out_dtypes], NN, None, epi, ("arbitrary", "arbitrary"))


def _mm_nt(name, dy, w3, layer, tm, tkw, extras=(), out_dtype=F32, epi=None):
    m, n = dy.shape
    kw = w3.shape[1]
    return _matmul(name, dy, w3, tuple(extras), (kw // tkw, m // tm),
                   pl.BlockSpec((tm, n), lambda j, i: (i, 0)), pl.BlockSpec((None, tkw, n), lambda j, i: (layer, j, 0)),
                   [pl.BlockSpec((tm, tkw), lambda j, i: (i, j)) for _ in extras],
                   [pl.BlockSpec((tm, tkw), lambda j, i: (i, j))], [jax.ShapeDtypeStruct((m, kw), out_dtype)], NT, None, epi,
                   ("arbitrary", "arbitrary"))[0]


def _mm_wgrad_cols(name, a, dy, tm):
    m, k = a.shape
    n = dy.shape[1] // N_DEV
    per = max(1, min(N_DEV, 1024 // n))
    return _matmul(name, a, dy, (), (N_DEV // per, m // tm),
                   pl.BlockSpec((tm, k), lambda j, i: (i, 0)), pl.BlockSpec((tm, per * n), lambda j, i: (i, j)), (),
                   [pl.BlockSpec((per, k, n), lambda j, i: (j, 0, 0))], [jax.ShapeDtypeStruct((N_DEV, k, n), BF16)],
                   TN, 1, lambda acc: (jnp.stack([acc[:, q * n:(q + 1) * n] for q in range(per)]),), ("arbitrary", "arbitrary"),
                   (k, per * n))[0]


def _mm_wgrad_rows(name, a, dy, tm):
    m, k = a.shape
    r = k // N_DEV
    n = dy.shape[1]
    rows = r * max(1, min(N_DEV, 1024 // r))
    out = _matmul(name, a, dy, (), (k // rows, m // tm),
                  pl.BlockSpec((tm, rows), lambda j, i: (i, j)), pl.BlockSpec((tm, n), lambda j, i: (i, 0)), (),
                  [pl.BlockSpec((rows, n), lambda j, i: (j, 0))], [jax.ShapeDtypeStruct((k, n), BF16)],
                  TN, 1, None, ("arbitrary", "arbitrary"))[0]
    return out.reshape(N_DEV, r, n)


class _Copies:
    def __init__(self, arrays, kinds):
        self.arrays, self.kinds, self.n = list(arrays), list(kinds), len(arrays)
        any_spec = pl.BlockSpec(memory_space=pl.ANY)
        self.in_specs = [any_spec] * self.n
        self.out_specs = [any_spec] * self.n
        self.out_shape = [jax.ShapeDtypeStruct(self._result(a, kind), a.dtype) for a, kind in zip(arrays, kinds)]
        self.scratch = [pltpu.SemaphoreType.DMA((self.n, N_DEV - 1)), pltpu.SemaphoreType.DMA((self.n, N_DEV - 1)),
                        pltpu.SemaphoreType.DMA((self.n,))] if self.n else []

    @staticmethod
    def _result(a, kind):
        if kind == "slot":
            return (N_DEV,) + a.shape
        if kind == "scatter":
            return a.shape
        return a.shape[:kind] + (N_DEV * a.shape[kind],) + a.shape[kind + 1:]

    def descriptors(self, ins, outs, sems):
        send_sems, recv_sems, local_sems = sems
        x, y, c = lax.axis_index("x"), lax.axis_index("y"), lax.axis_index("c")
        me = 4 * x + 2 * y + c
        first, relay, finish = [], [], []

        def remote(i, k, src, dst, to):
            return pltpu.make_async_remote_copy(src_ref=src, dst_ref=dst, send_sem=send_sems.at[i, k], recv_sem=recv_sems.at[i, k],
                                                device_id=to, device_id_type=MESH)

        for i, kind in enumerate(self.kinds):
            if kind in ("slot", "scatter"):
                scatter = kind == "scatter"
                local = pltpu.make_async_copy(ins[i].at[me] if scatter else ins[i], outs[i].at[me], local_sems.at[i])
                first.append(local)
                finish.append((local, "all"))
                for k in range(1, N_DEV):
                    px = 1 - x if k & 4 else x
                    py = 1 - y if k & 2 else y
                    pc = 1 - c if k & 1 else c
                    cp = remote(i, k - 1, ins[i].at[4 * px + 2 * py + pc] if scatter else ins[i], outs[i].at[me], (px, py, pc))
                    first.append(cp)
                    finish.append((cp, "all"))
                continue
            size = ins[i].shape[kind]

            def block(px, py, pc):
                return outs[i].at[(slice(None),) * kind + (pl.ds(pl.multiple_of((4 * px + 2 * py + pc) * size, size), size),)]

            local = pltpu.make_async_copy(ins[i], block(x, y, c), local_sems.at[i])
            sibling = remote(i, 0, ins[i], block(x, y, c), (x, y, 1 - c))
            first += [local, sibling]
            finish += [(local, "all"), (sibling, "all")]
            for j, (qx, qy) in enumerate([(1 - x, y), (x, 1 - y), (1 - x, 1 - y)]):
                out = remote(i, 1 + j, ins[i], block(x, y, c), (qx, qy, c))
                onward = remote(i, 4 + j, block(qx, qy, c), block(qx, qy, c), (x, y, 1 - c))
                first.append(out)
                relay.append((out, onward))
                finish += [(out, "send"), (onward, "all")]
        return first, relay, finish


def _guarded(when, fn):
    if when is None:
        fn()
    else:
        pl.when(when)(fn)


def _start_all(plan, when=None):
    def run():
        for cp in plan[0]:
            cp.start()
    _guarded(when, run)


def _relay_all(plan, when=None):
    def run():
        for arrived, onward in plan[1]:
            arrived.wait_recv()
            onward.start()
    if plan[1]:
        _guarded(when, run)


def _wait_all(plan, when=None):
    def run():
        for cp, left in plan[2]:
            if left == "send":
                cp.wait_send()
            else:
                cp.wait()
    _guarded(when, run)


def _exchange(name, arrays, kinds):
    cps = _Copies(arrays, kinds)
    n = cps.n

    def body(*refs):
        plan = cps.descriptors(refs[:n], refs[n:2 * n], refs[2 * n:])
        _start_all(plan)
        _relay_all(plan)
        _wait_all(plan)

    return pl.pallas_call(body, name=name, in_specs=cps.in_specs, out_specs=cps.out_specs, out_shape=cps.out_shape,
                          scratch_shapes=cps.scratch)(*arrays)


def _carried(cps, n_in, n_out, n_scratch, refs):
    if cps is None:
        return ([], [], []), refs
    n = cps.n
    ins = refs[n_in:n_in + n]
    outs = refs[n_in + n + n_out:n_in + n + n_out + n]
    sems = refs[n_in + n + n_out + n + n_scratch:]
    own = refs[:n_in] + refs[n_in + n:n_in + n + n_out] + refs[n_in + n + n_out + n:n_in + n + n_out + n + n_scratch]
    return cps.descriptors(ins, outs, sems), own


def _sum_lead(name, parts, tr):
    npart, r, c = parts.shape

    def body(p_ref, o_ref):
        acc = p_ref[0].astype(F32)
        for p in range(1, npart):
            acc = acc + p_ref[p].astype(F32)
        o_ref[...] = acc

    return pl.pallas_call(body, name=name, grid=(r // tr,), in_specs=[pl.BlockSpec((npart, tr, c), lambda i: (0, i, 0))],
                          out_specs=pl.BlockSpec((tr, c), lambda i: (i, 0)), out_shape=jax.ShapeDtypeStruct((r, c), F32),
                          compiler_params=_params(("arbitrary",)))(parts)


def _adamw_math(g, w, m, v):
    m2 = ADAM_B1 * m + (1.0 - ADAM_B1) * g
    v2 = ADAM_B2 * v + (1.0 - ADAM_B2) * jnp.square(g)
    m_hat = m2 / (1.0 - ADAM_B1 ** ADAM_STEP)
    v_hat = v2 / (1.0 - ADAM_B2 ** ADAM_STEP)
    return -ADAM_LR * (m_hat / (jnp.sqrt(v_hat) + ADAM_EPS) + ADAM_WD * w), m2, v2


def _adamw_body(npart):
    def body(p_ref, w_ref, m_ref, v_ref, *rest):
        g_out, d_out, m_out, v_out = rest[-4:]
        g = p_ref[0].astype(F32)
        for p in range(1, npart):
            g = g + p_ref[p].astype(F32)
        g_out[...] = g
        d_out[...], m_out[...], v_out[...] = _adamw_math(g, w_ref[...], m_ref[...], v_ref[...])
    return body


def _adamw_native(name, g, w, m, v):
    rest = w.shape[2:]
    spec = pl.BlockSpec((None, None) + rest, lambda i, j: (i, j) + (0,) * len(rest))

    def body(g_ref, w_ref, m_ref, v_ref, d_out, m_out, v_out):
        d_out[...], m_out[...], v_out[...] = _adamw_math(g_ref[...], w_ref[...], m_ref[...], v_ref[...])

    return pl.pallas_call(body, name=name, grid=w.shape[:2], in_specs=[spec] * 4, out_specs=[spec] * 3,
                          out_shape=[jax.ShapeDtypeStruct(w.shape, F32)] * 3,
                          compiler_params=_params(("arbitrary", "arbitrary")))(g, w, m, v)


def _adamw(name, parts, w, m, v, tr):
    npart, r, c = parts.shape
    row = pl.BlockSpec((tr, c), lambda i: (i, 0))
    return pl.pallas_call(_adamw_body(npart), name=name, grid=(r // tr,),
                          in_specs=[pl.BlockSpec((npart, tr, c), lambda i: (0, i, 0)), row, row, row],
                          out_specs=[row] * 4, out_shape=[jax.ShapeDtypeStruct((r, c), F32)] * 4,
                          compiler_params=_params(("arbitrary",)))(parts, w, m, v)


def _adamw_layer(name, parts, w3, m3, v3, bufs, layer, tr):
    npart, r, c = parts.shape
    lay = pl.BlockSpec((None, tr, c), lambda i: (layer, i, 0))
    hbm = pl.BlockSpec(memory_space=pl.ANY)
    return pl.pallas_call(_adamw_body(npart), name=name, grid=(r // tr,),
                          in_specs=[pl.BlockSpec((npart, tr, c), lambda i: (0, i, 0)), lay, lay, lay] + [hbm] * 4,
                          out_specs=[lay] * 4, out_shape=[jax.ShapeDtypeStruct(w3.shape, F32)] * 4,
                          input_output_aliases={4: 0, 5: 1, 6: 2, 7: 3},
                          compiler_params=_params(("arbitrary",)))(parts, w3, m3, v3, *bufs)


def _row_tile(r, cap):
    if r <= cap:
        return r
    t = cap - cap % SUBLANES
    while r % t:
        t -= SUBLANES
    return t


def _ada_fwd(cc, w_ada, b_loc):
    nl, d, n = w_ada.shape
    rows = cc.shape[0]

    def body(c_ref, w_ref, b_ref, o_ref):
        cond = jax.nn.silu(c_ref[...]).astype(BF16)
        o_ref[...] = jnp.dot(cond, w_ref[...].astype(BF16), preferred_element_type=F32) + b_ref[...]

    return pl.pallas_call(body, name="ada_fwd", grid=(nl,),
                          in_specs=[pl.BlockSpec((rows, d), lambda i: (0, 0)), pl.BlockSpec((None, d, n), lambda i: (i, 0, 0)),
                                    pl.BlockSpec((None, 1, n), lambda i: (i, 0, 0))],
                          out_specs=pl.BlockSpec((None, rows, n), lambda i: (i, 0, 0)),
                          out_shape=jax.ShapeDtypeStruct((nl, rows, n), F32), compiler_params=_params(("arbitrary",)))(cc, w_ada, b_loc)


def _ada_bwd(cc, w_ada, dmod_rows, dmod_ctx):
    nl, d, n = w_ada.shape
    rows = cc.shape[0]
    ctx_row = rows - SUBLANES

    def body(c_ref, w_ref, dr_ref, dc_ref, gw_ref, tot_ref, dcond_ref):
        i = pl.program_id(0)
        total = dc_ref[0]
        for p in range(1, N_DEV):
            total = total + dc_ref[p]
        tot_ref[...] = total
        row_id = lax.broadcasted_iota(jnp.int32, (rows, n), 0)
        dm = jnp.where(row_id == ctx_row, jnp.broadcast_to(total, (rows, n)), dr_ref[...]).astype(BF16)
        cond = jax.nn.silu(c_ref[...]).astype(BF16)
        gw_ref[...] = lax.dot_general(cond, dm, TN, preferred_element_type=F32)
        part = lax.dot_general(dm, w_ref[...].astype(BF16), NT, preferred_element_type=F32)
        _accumulate(dcond_ref, part, i == 0)

    return pl.pallas_call(body, name="ada_bwd", grid=(nl,),
                          in_specs=[pl.BlockSpec((rows, d), lambda i: (0, 0)), pl.BlockSpec((None, d, n), lambda i: (i, 0, 0)),
                                    pl.BlockSpec((None, rows, n), lambda i: (i, 0, 0)),
                                    pl.BlockSpec((N_DEV, None, 1, n), lambda i: (0, i, 0, 0))],
                          out_specs=[pl.BlockSpec((None, d, n), lambda i: (i, 0, 0)), pl.BlockSpec((None, 1, n), lambda i: (i, 0, 0)),
                                     pl.BlockSpec((rows, d), lambda i: (0, 0))],
                          out_shape=[jax.ShapeDtypeStruct((nl, d, n), F32), jax.ShapeDtypeStruct((nl, 1, n), F32),
                                     jax.ShapeDtypeStruct((rows, d), F32)],
                          compiler_params=_params(("arbitrary",)))(cc, w_ada, dmod_rows, dmod_ctx)


def _cctx_grad(parts, c_ctx):
    def body(p_ref, c_ref, o_ref):
        tot = p_ref[0]
        for p in range(1, N_DEV):
            tot = tot + p_ref[p]
        _, vjp = jax.vjp(jax.nn.silu, c_ref[...])
        o_ref[...] = vjp(tot)[0]

    return pl.pallas_call(body, name="cctx_grad", out_shape=jax.ShapeDtypeStruct(c_ctx.shape, F32))(parts, c_ctx)


def _discretise(lam_re, lam_im, log_dt, b_re, b_im):
    lr = jnp.minimum(lam_re, LAMBDA_RE_MAX)
    li = lam_im
    dt = jnp.exp(log_dt)
    mag = jnp.exp(lr * dt)
    ab_re = mag * jnp.cos(li * dt)
    ab_im = mag * jnp.sin(li * dt)
    den = lr * lr + li * li
    nr = ab_re - 1.0
    ni = ab_im
    coef_re = ((nr * lr + ni * li) / den)[:, None]
    coef_im = ((ni * lr - nr * li) / den)[:, None]
    bb_re = coef_re * b_re - coef_im * b_im
    bb_im = coef_re * b_im + coef_im * b_re
    return ab_re, ab_im, bb_re, bb_im


def _s5_prep(name, lam_re, lam_im, log_dt, b_re, b_im):
    def body(a, b, c, d, e, o1, o2, o3, o4):
        res = _discretise(a[...], b[...], c[...], d[...], e[...])
        for o, v in zip((o1, o2, o3, o4), res):
            o[...] = v

    shp = [jax.ShapeDtypeStruct(lam_re.shape, F32)] * 2 + [jax.ShapeDtypeStruct(b_re.shape, F32)] * 2
    return pl.pallas_call(body, name=name, out_shape=shp)(lam_re, lam_im, log_dt, b_re, b_im)


def _s5_prep_bwd(name, lam_re, lam_im, log_dt, b_re, b_im, cots):
    def body(a, b, c, d, e, c1, c2, c3, c4, o1, o2, o3, o4, o5):
        _, vjp = jax.vjp(_discretise, a[...], b[...], c[...], d[...], e[...])
        grads = vjp((c1[...], c2[...], c3[...], c4[...]))
        for o, v in zip((o1, o2, o3, o4, o5), grads):
            o[...] = v

    shp = [jax.ShapeDtypeStruct(a.shape, F32) for a in (lam_re, lam_im, log_dt, b_re, b_im)]
    return pl.pallas_call(body, name=name, out_shape=shp)(lam_re, lam_im, log_dt, b_re, b_im, *cots)


def _interleave_rows(ref, n_seq, dtype):
    n_j = ref.shape[0] // (SUBLANES * n_seq)
    return jnp.concatenate([ref[pl.ds(q * SUBLANES * n_j + j, SUBLANES, stride=n_j), :] for q in range(n_seq) for j in range(n_j)],
                           axis=0).astype(dtype)


def _store_tokens(out_ref, ref, n_seq):
    n_j = ref.shape[0] // (SUBLANES * n_seq)
    for q in range(n_seq):
        for s in range(SUBLANES):
            start = (q * SUBLANES + s) * n_j
            out_ref[start:start + n_j, :] = ref[pl.ds(q * SUBLANES * n_j + s, n_j, stride=SUBLANES), :].astype(out_ref.dtype)


def _expand_powers(t_ref, pow_ref):
    for j in range(pow_ref.shape[0] // SUBLANES):
        row = 5 * SUBLANES + j
        pow_ref[j * SUBLANES:(j + 1) * SUBLANES, :] = jnp.broadcast_to(t_ref[row:row + 1, :], (SUBLANES, pow_ref.shape[1]))


def _scan_tile(h_ref, t_ref, pow_ref, carry_ref, up, n_seq, states_ref=None):
    sw = h_ref.shape[1] // 2
    n_j = h_ref.shape[0] // (SUBLANES * n_seq)
    seqs = range(n_seq)

    def rows(g):
        if isinstance(g, int):
            return pl.ds(g * SUBLANES, SUBLANES)
        return pl.ds(pl.multiple_of(g * SUBLANES, SUBLANES), SUBLANES)

    def at(q, j):
        return rows(q * n_j + j)

    def tab(g):
        return t_ref[rows(g), :sw], t_ref[rows(g), sw:]

    def order(i):
        return n_j - 1 - i if up else i

    def cmul_add(xr, xi, ar, ai, yr, yi):
        return xr + ar * yr - ai * yi, xi + ar * yi + ai * yr

    a_re, a_im = tab(0)

    def local_step(i, xs):
        j = order(i)
        out = []
        for q in seqs:
            xr, xi = cmul_add(h_ref[at(q, j), :sw], h_ref[at(q, j), sw:], a_re, a_im, *xs[q])
            h_ref[at(q, j), :sw] = xr
            h_ref[at(q, j), sw:] = xi
            out.append((xr, xi))
        return tuple(out)

    zero = jnp.zeros((SUBLANES, sw), F32)
    ends = lax.fori_loop(0, n_j, local_step, tuple((zero, zero) for _ in seqs))
    out_row = 0 if up else SUBLANES - 1
    in_row = SUBLANES - 1 if up else 0
    one = SUBLANES - 1 if up else 1
    is_in = lax.broadcasted_iota(jnp.int32, (SUBLANES, sw), 0) == in_row
    carried, enters = [], []
    for q in seqs:
        dr, di = ends[q]
        for level, sh in enumerate((1, 2, 4)):
            amount = SUBLANES - sh if up else sh
            dr, di = cmul_add(dr, di, *tab(1 + level), pltpu.roll(dr, amount, 0), pltpu.roll(di, amount, 0))
        cr, ci = carry_ref[rows(q), :sw], carry_ref[rows(q), sw:]
        dr, di = cmul_add(dr, di, *tab(4), cr, ci)
        carry_ref[rows(q), :sw] = jnp.broadcast_to(dr[out_row:out_row + 1], dr.shape)
        carry_ref[rows(q), sw:] = jnp.broadcast_to(di[out_row:out_row + 1], di.shape)
        carried.append((cr, ci))
        enters.append((jnp.where(is_in, cr, pltpu.roll(dr, one, 0)), jnp.where(is_in, ci, pltpu.roll(di, one, 0))))

    def fix_step(i, state):
        j = order(i)
        nows = []
        for q in seqs:
            xr, xi = cmul_add(h_ref[at(q, j), :sw], h_ref[at(q, j), sw:], pow_ref[rows(j), :sw], pow_ref[rows(j), sw:], *enters[q])
            h_ref[at(q, j), :sw] = xr
            h_ref[at(q, j), sw:] = xi
            nows.append((xr, xi))
        if states_ref is None:
            return state
        befores, (acc_r, acc_i) = state
        for q in seqs:
            lr, li = befores[q]
            hr, hi = states_ref[at(q, j), :sw], states_ref[at(q, j), sw:]
            acc_r, acc_i = acc_r + lr * hr + li * hi, acc_i + li * hr - lr * hi
        return tuple(nows), (acc_r, acc_i)

    if states_ref is None:
        lax.fori_loop(0, n_j, fix_step, 0)
        return None
    lasts, (acc_r, acc_i) = lax.fori_loop(0, n_j, fix_step, (tuple((zero, zero) for _ in seqs), (zero, zero)))
    for q in seqs:
        lr = jnp.where(is_in, carried[q][0], pltpu.roll(lasts[q][0], one, 0))
        li = jnp.where(is_in, carried[q][1], pltpu.roll(lasts[q][1], one, 0))
        hr, hi = states_ref[at(q, order(0)), :sw], states_ref[at(q, order(0)), sw:]
        acc_r, acc_i = acc_r + lr * hr + li * hi, acc_i + li * hr - lr * hi
    return acc_r, acc_i


def _phys_tile(t, cfg):
    tpl, nb = cfg["tpl"], cfg["nb"]
    return jnp.where(t < nb * tpl, (t % tpl) * nb + t // tpl, t)


def _s5_block_index(cfg, dirn, adjoint):
    tpl = cfg["tpl"]

    def idx(k):
        if not adjoint:
            return jnp.where(k == 0, tpl, k - 1 if dirn == 0 else tpl - k)
        return jnp.where(k == tpl, tpl, tpl - 1 - k if dirn == 0 else k)
    return idx


RELAY_AT = 0.75


def _grid_ends(grid):
    step = 0
    for i, n in enumerate(grid):
        step = step * n + pl.program_id(i)
    total = math.prod(grid)
    return step == 0, step == int(RELAY_AT * total), step == total - 1


def _s5_fwd(name, u, bmat, cmat, tab, dirn, cfg, cps=None):
    tr, tpl, nb = cfg["tr"], cfg["tpl"], cfg["nb"]
    n_rows, d = u.shape
    ns, _, sw2 = bmat.shape
    block = _s5_block_index(cfg, dirn, False)
    up = dirn == 1
    grid = (ns, tpl + 1)
    br = nb * tr

    def body(*refs):
        copies, (u_ref, b_ref, c_ref, t_ref, h_ref, y_ref, carry_ref, mix_ref, pow_ref) = _carried(cps, 4, 2, 3, refs)
        first, middle, last = _grid_ends(grid)
        _start_all(copies, first)
        _relay_all(copies, middle)

        @pl.when(pl.program_id(1) == 0)
        def _():
            carry_ref[...] = jnp.zeros_like(carry_ref)
            _expand_powers(t_ref, pow_ref)

        h_ref[...] = jnp.dot(u_ref[...], b_ref[...], preferred_element_type=F32)
        _scan_tile(h_ref, t_ref, pow_ref, carry_ref, up, nb)
        mix_ref[...] = jnp.dot(h_ref[...].astype(BF16), c_ref[...], preferred_element_type=F32)
        _store_tokens(y_ref, mix_ref, nb)
        _wait_all(copies, last)

    extra = cps if cps is not None else _Copies([], [])
    return pl.pallas_call(
        body, name=name, grid=grid,
        in_specs=[pl.BlockSpec((br, LANES), lambda s, k: (block(k), s)),
                  pl.BlockSpec((None, LANES, sw2), lambda s, k: (s, 0, 0)),
                  pl.BlockSpec((None, sw2, LANES), lambda s, k: (s, 0, 0)),
                  pl.BlockSpec((None, tab.shape[1], sw2), lambda s, k: (s, 0, 0))] + extra.in_specs,
        out_specs=[pl.BlockSpec((br, sw2), lambda s, k: (block(k), s)),
                   pl.BlockSpec((br, LANES), lambda s, k: (block(k), s))] + extra.out_specs,
        out_shape=[jax.ShapeDtypeStruct((n_rows, ns * sw2), F32), jax.ShapeDtypeStruct((n_rows, d), F32)] + extra.out_shape,
        scratch_shapes=[pltpu.VMEM((nb * SUBLANES, sw2), F32), pltpu.VMEM((br, LANES), F32), pltpu.VMEM((tr, sw2), F32)] + extra.scratch,
        compiler_params=_params(("arbitrary", "arbitrary")))(u, bmat, cmat, tab, *extra.arrays)


def _s5_bwd(name, dy, h, u, cmat_t, bmat_t, tab, dirn, cfg, cps=None):
    tr, tpl, nb = cfg["tr"], cfg["tpl"], cfg["nb"]
    n_rows, d = u.shape
    ns, _, sw2 = cmat_t.shape
    sw = sw2 // 2
    block = _s5_block_index(cfg, dirn, True)
    up = dirn == 0
    grid = (ns, tpl + 1)
    br = nb * tr

    def body(*refs):
        copies, own = _carried(cps, 6, 4, 4, refs)
        dy_ref, h_ref, u_ref, ct_ref, bt_ref, t_ref, du_ref, db_ref, dc_ref, da_ref, lam_ref, carry_ref, mix_ref, pow_ref = own
        grid_first, grid_middle, grid_last = _grid_ends(grid)
        _start_all(copies, grid_first)
        _relay_all(copies, grid_middle)
        first = pl.program_id(1) == 0

        @pl.when(first)
        def _():
            carry_ref[...] = jnp.zeros_like(carry_ref)
            _expand_powers(t_ref, pow_ref)

        dy = dy_ref[...]
        u_mixed = u_ref[...]
        lam_ref[...] = jnp.dot(dy, ct_ref[...], preferred_element_type=F32)
        acc = _scan_tile(lam_ref, t_ref, pow_ref, carry_ref, up, nb, h_ref)
        lam = lam_ref[...].astype(BF16)
        d_b = lax.dot_general(u_mixed, lam, TN, preferred_element_type=F32)
        d_c = lax.dot_general(h_ref[...].astype(BF16), dy, TN, preferred_element_type=F32)
        mix_ref[...] = jnp.dot(lam, bt_ref[...], preferred_element_type=F32)
        _store_tokens(du_ref, mix_ref, nb)

        @pl.when(first)
        def _():
            db_ref[...] = d_b
            dc_ref[...] = d_c
            da_ref[:, :sw] = acc[0]
            da_ref[:, sw:] = acc[1]

        @pl.when(jnp.logical_not(first))
        def _():
            db_ref[...] += d_b
            dc_ref[...] += d_c
            da_ref[:, :sw] += acc[0]
            da_ref[:, sw:] += acc[1]

        _wait_all(copies, grid_last)

    extra = cps if cps is not None else _Copies([], [])
    return pl.pallas_call(
        body, name=name, grid=grid,
        in_specs=[pl.BlockSpec((br, LANES), lambda s, k: (block(k), s)),
                  pl.BlockSpec((br, sw2), lambda s, k: (block(k), s)),
                  pl.BlockSpec((br, LANES), lambda s, k: (block(k), s)),
                  pl.BlockSpec((None, LANES, sw2), lambda s, k: (s, 0, 0)),
                  pl.BlockSpec((None, sw2, LANES), lambda s, k: (s, 0, 0)),
                  pl.BlockSpec((None, tab.shape[1], sw2), lambda s, k: (s, 0, 0))] + extra.in_specs,
        out_specs=[pl.BlockSpec((br, LANES), lambda s, k: (block(k), s)),
                   pl.BlockSpec((None, LANES, sw2), lambda s, k: (s, 0, 0)),
                   pl.BlockSpec((None, sw2, LANES), lambda s, k: (s, 0, 0)),
                   pl.BlockSpec((None, SUBLANES, sw2), lambda s, k: (s, 0, 0))] + extra.out_specs,
        out_shape=[jax.ShapeDtypeStruct((n_rows, d), F32), jax.ShapeDtypeStruct((ns, LANES, sw2), F32),
                   jax.ShapeDtypeStruct((ns, sw2, LANES), F32), jax.ShapeDtypeStruct((ns, SUBLANES, sw2), F32)] + extra.out_shape,
        scratch_shapes=[pltpu.VMEM((br, sw2), F32), pltpu.VMEM((nb * SUBLANES, sw2), F32), pltpu.VMEM((br, LANES), F32),
                        pltpu.VMEM((tr, sw2), F32)] + extra.scratch,
        compiler_params=_params(("arbitrary", "arbitrary")))(dy, h, u, cmat_t, bmat_t, tab, *extra.arrays)


def _s5_tables(ab_re, ab_im, up, conj, ns, n_j):
    def powers_of(base, count):
        out = [base]
        for _ in range(count - 1):
            q_re, q_im = out[-1]
            out.append((q_re * base[0] - q_im * base[1], q_re * base[1] + q_im * base[0]))
        return out

    def spread(q):
        return jnp.broadcast_to(q[:, None, :], (q.shape[0], SUBLANES, q.shape[1]))

    steps = powers_of((ab_re.reshape(ns, -1), (-ab_im if conj else ab_im).reshape(ns, -1)), n_j)
    jumps = powers_of(steps[-1], SUBLANES)
    rows = jnp.arange(SUBLANES)
    blocks = [tuple(spread(q) for q in steps[0])]
    for sh in (1, 2, 4):
        keep = ((rows <= SUBLANES - 1 - sh) if up else (rows >= sh))[None, :, None]
        blocks.append(tuple(jnp.where(keep, q[:, None, :], 0.0) for q in jumps[sh - 1]))
    dist = range(SUBLANES, 0, -1) if up else range(1, SUBLANES + 1)
    blocks.append(tuple(jnp.stack([jumps[dd - 1][part] for dd in dist], axis=1) for part in (0, 1)))
    ordered = steps[::-1] if up else steps
    blocks.append(tuple(jnp.stack([q[part] for q in ordered], axis=1) for part in (0, 1)))
    return jnp.concatenate([jnp.concatenate([b[0] for b in blocks], axis=1), jnp.concatenate([b[1] for b in blocks], axis=1)], axis=2)


def _block_diag(blocks):
    ns, gs, a, b = blocks.shape
    eye = jnp.eye(gs, dtype=blocks.dtype)
    return (blocks[:, :, :, None, :] * eye[None, :, None, :, None]).reshape(ns, gs * a, gs * b)


def _diag_blocks(mat, gs):
    ns, ra, rb = mat.shape
    a, b = ra // gs, rb // gs
    m5 = mat.reshape(ns, gs, a, gs, b)
    eye = jnp.eye(gs, dtype=mat.dtype)
    return jnp.sum(m5 * eye[None, :, None, :, None], axis=3)


def _conv_flags(t, cfg):
    tpl, nb = cfg["tpl"], cfg["nb"]
    latent = t < nb * tpl
    first = jnp.logical_or(jnp.logical_not(latent), t % tpl == 0)
    last = jnp.logical_or(jnp.logical_not(latent), t % tpl == tpl - 1)
    return first, last


def _fill_ext(ext_ref, prev_ref, cur_ref, next_ref, t, cfg, halo):
    first, last = _conv_flags(t, cfg)
    tr = cur_ref.shape[0]
    for p in range(ext_ref.shape[0]):
        lanes = slice(p * LANES, (p + 1) * LANES)
        ext_ref[p, 0:halo, :] = jnp.where(first, 0.0, prev_ref[:, lanes])
        ext_ref[p, halo:halo + tr, :] = cur_ref[:, lanes]
        ext_ref[p, halo + tr:, :] = jnp.where(last, 0.0, next_ref[:, lanes])


CONV_LANES = 4 * LANES


def _conv_specs(tr, n_rows, halo, cw, cfg):
    per = tr // halo
    n_halo = n_rows // halo
    nb = cfg["nb"]
    return [pl.BlockSpec((halo, cw), lambda c, t: (jnp.maximum((_phys_tile(t, cfg) - nb + 1) * per - 1, 0), c)),
            pl.BlockSpec((tr, cw), lambda c, t: (_phys_tile(t, cfg), c)),
            pl.BlockSpec((halo, cw), lambda c, t: (jnp.minimum((_phys_tile(t, cfg) + nb) * per, n_halo - 1), c))]


def _dwconv(name, a, w, cfg):
    tr = cfg["tr"]
    n_rows, d = a.shape
    kw = w.shape[0]
    half = kw // 2
    halo = 2 * SUBLANES
    cw = min(d, CONV_LANES)

    def body(prev_ref, cur_ref, next_ref, w_ref, o_ref, ext_ref):
        _fill_ext(ext_ref, prev_ref, cur_ref, next_ref, pl.program_id(1), cfg, halo)
        for p in range(cw // LANES):
            lanes = slice(p * LANES, (p + 1) * LANES)
            acc = jnp.zeros((tr, LANES), F32)
            for k in range(kw):
                acc = acc + ext_ref[p, pl.ds(halo - half + k, tr), :] * w_ref[k:k + 1, lanes]
            o_ref[:, lanes] = acc

    return pl.pallas_call(body, name=name, grid=(d // cw, n_rows // tr),
                          in_specs=_conv_specs(tr, n_rows, halo, cw, cfg) + [pl.BlockSpec((kw, cw), lambda c, t: (0, c))],
                          out_specs=pl.BlockSpec((tr, cw), lambda c, t: (_phys_tile(t, cfg), c)),
                          out_shape=jax.ShapeDtypeStruct((n_rows, d), F32),
                          scratch_shapes=[pltpu.VMEM((cw // LANES, tr + 2 * halo, LANES), F32)],
                          compiler_params=_params(("arbitrary", "arbitrary")))(a, a, a, w)


def _dwconv_wgrad(name, a, dout, kw, cfg):
    tr = cfg["tr"]
    n_rows, d = a.shape
    half = kw // 2
    halo = 2 * SUBLANES
    cw = min(d, CONV_LANES)

    def body(prev_ref, cur_ref, next_ref, do_ref, o_ref, ext_ref):
        t = pl.program_id(1)
        _fill_ext(ext_ref, prev_ref, cur_ref, next_ref, t, cfg, halo)
        for p in range(cw // LANES):
            lanes = slice(p * LANES, (p + 1) * LANES)
            dout_t = do_ref[:, lanes]
            rows = [jnp.sum(ext_ref[p, pl.ds(halo - half + k, tr), :] * dout_t, axis=0, keepdims=True) for k in range(kw)]
            _accumulate(o_ref.at[:, lanes], jnp.concatenate(rows, axis=0), t == 0)

    return pl.pallas_call(body, name=name, grid=(d // cw, n_rows // tr),
                          in_specs=_conv_specs(tr, n_rows, halo, cw, cfg) + [pl.BlockSpec((tr, cw), lambda c, t: (_phys_tile(t, cfg), c))],
                          out_specs=pl.BlockSpec((kw, cw), lambda c, t: (0, c)),
                          out_shape=jax.ShapeDtypeStruct((kw, d), F32),
                          scratch_shapes=[pltpu.VMEM((cw // LANES, tr + 2 * halo, LANES), F32)],
                          compiler_params=_params(("arbitrary", "arbitrary")))(a, a, a, dout)


def _sincos_1d(pos, dim):
    quarter = dim // 2
    omega = POS_TEMP ** (-jnp.arange(quarter, dtype=F32) / quarter)
    ang = pos[:, None] * omega[None, :]
    return jnp.concatenate([jnp.sin(ang), jnp.cos(ang)], axis=-1)


def _grid_pos_embed(rows, dim):
    row_idx = jnp.repeat(jnp.arange(rows), GRID_W).astype(F32)
    col_idx = jnp.tile(jnp.arange(GRID_W), rows).astype(F32)
    return jnp.concatenate([_sincos_1d(row_idx, dim // 2), _sincos_1d(col_idx, dim // 2)], axis=-1)


def _pack(arrs, row_multiple=SUBLANES):
    flat = jnp.concatenate([a.reshape(-1).astype(F32) for a in arrs])
    pad = (-flat.shape[0]) % (row_multiple * LANES)
    return jnp.pad(flat, (0, pad)).reshape(-1, LANES)


def _unpack(buf, shapes):
    flat = buf.reshape(-1)
    out, pos = [], 0
    for shp in shapes:
        n = math.prod(shp)
        out.append(flat[pos:pos + n].reshape(shp))
        pos += n
    return out


def _unpack_gathered(buf, shapes):
    flat = buf.reshape(N_DEV, -1)
    out, pos = [], 0
    for shp in shapes:
        n = math.prod(shp)
        part = flat[:, pos:pos + n].reshape((N_DEV,) + tuple(shp))
        out.append(jnp.moveaxis(part, 0, -2).reshape(tuple(shp[:-1]) + (N_DEV * shp[-1],)))
        pos += n
    return out


WEIGHTS = ("c_ctx", "w_ada", "b_ada", "ln_gain", "ln_bias", "s5_lam_re", "s5_lam_im", "s5_log_dt", "s5_b_re", "s5_b_im",
           "s5_c_re", "s5_c_im", "s5_d", "s5_w_glu", "s5_b_glu", "cv_w_pw1", "cv_b_pw1", "cv_w_dw", "cv_b_dw", "cv_ln_g",
           "cv_ln_b", "cv_w_pw2", "cv_b_pw2", "mlp_w1", "mlp_w2")
SHARDED_SMALL = ("ln_gain", "ln_bias", "cv_b_pw1", "cv_w_dw", "cv_b_dw", "cv_ln_g", "cv_ln_b", "cv_b_pw2")
REPLICATED_SMALL = ("s5_lam_re", "s5_lam_im", "s5_log_dt", "s5_b_re", "s5_b_im", "s5_c_re", "s5_c_im", "s5_d", "s5_b_glu")
NATIVE_SMALL = ("s5_lam_re", "s5_lam_im", "s5_b_re", "s5_b_im", "s5_c_re", "s5_c_im")
BIG = ("mlp_w1", "mlp_w2", "s5_w_glu", "cv_w_pw1", "cv_w_pw2")


def _step(a):
    x, c, ctx = a["x"], a["c"], a["ctx"]
    nb, seq, d = x.shape
    lc = ctx.shape[1]
    nl = a["w_ada"].shape[0]
    tr = lc
    tpl = seq // tr
    cfg = {"tr": tr, "tpl": tpl, "nb": nb}
    n_rows = nb * (seq + lc)
    alpha = (2.0 * nl) ** 0.25
    me = 4 * lax.axis_index("x") + 2 * lax.axis_index("y") + lax.axis_index("c")
    n_grp, n_state = a["s5_lam_re"].shape[2:]
    ch = a["s5_b_re"].shape[-1]
    gs = LANES // ch
    ns = d // LANES
    tm = 2 * tr if n_rows % (2 * tr) == 0 else tr
    tm_big = n_rows // 3 if n_rows % (3 * 2 * SUBLANES) == 0 else tm
    f_sub1_s5, f_sub1_cv, f_sub2 = _make_sub1_s5(alpha), _make_sub1_cv(alpha), _make_sub2(alpha)

    def layer_weights(i):
        mixer = [("s5_w_glu", i // 2, 1)] if i % 2 == 0 else [("cv_w_pw1", i // 2, 1), ("cv_w_pw2", i // 2, 0)]
        return mixer + [("mlp_w1", i, 1), ("mlp_w2", i, 0)]

    weights, wgrads, received = {}, {}, {}
    small_all, c_all = _exchange("gather_small", [_pack([a[n] for n in SHARDED_SMALL]), c], ["slot", "slot"])
    full = dict(zip(SHARDED_SMALL, _unpack_gathered(small_all, [a[n].shape for n in SHARDED_SMALL])))
    c_all = c_all.reshape(N_DEV * nb, d)
    cond_rows = N_DEV * nb + SUBLANES
    cc = jnp.concatenate([c_all, a["c_ctx"][None], jnp.zeros((SUBLANES - 1, d), F32)], axis=0)

    n_ada = a["w_ada"].shape[2]
    b_loc = lax.dynamic_slice(a["b_ada"], (0, me * n_ada), (nl, n_ada))[:, None, :]
    mod_cols = _ada_fwd(cc, a["w_ada"], b_loc)
    mod_all = _exchange("gather_mod", [mod_cols.reshape(nl * cond_rows, n_ada)], ["slot"])[0].reshape(N_DEV, nl, cond_rows, n_ada)
    mod_mine = jnp.concatenate([lax.dynamic_slice(mod_all, (0, 0, nb * me, 0), (N_DEV, nl, nb, n_ada)),
                                mod_all[:, :, N_DEV * nb:N_DEV * nb + 1]], axis=2)
    mod = jnp.transpose(mod_mine, (1, 2, 0, 3)).reshape(nl, nb + 1, 6, 1, d)

    def seg(i, q):
        return mod[i, :, q]

    zero_seg = jnp.zeros((nb + 1, 1, d), F32)

    def vec(v):
        return v.reshape(1, -1)

    pos = _grid_pos_embed(seq // GRID_W, d)
    def latent_rows(v):
        return jnp.transpose(v.reshape(nb, tpl, tr, d), (1, 0, 2, 3)).reshape(nb * seq, d)

    xc = jnp.concatenate([latent_rows(x), ctx.reshape(nb * lc, d)], axis=0)
    pos_rows = jnp.concatenate([latent_rows(jnp.broadcast_to(pos[None], (nb, seq, d))), jnp.zeros((nb * lc, d), F32)], axis=0)
    x_cur, h_cur = _rowwise("entry", _f_entry, [xc, pos_rows], [seg(0, 0), seg(0, 1)], [], [(d, F32), (d, BF16)], [], [], cfg, (1,))
    saved = []
    for i in range(nl):
        j = i // 2
        sv = {"x": x_cur, "h": h_cur}
        sh1, sc1, g1, sh2, sc2, g2 = (seg(i, q) for q in range(6))
        gain0, bias0, gain1, bias1 = (vec(full["ln_gain"][i, 0]), vec(full["ln_bias"][i, 0]),
                                      vec(full["ln_gain"][i, 1]), vec(full["ln_bias"][i, 1]))
        if i % 2 == 0:
            lam_re, lam_im = a["s5_lam_re"][j], a["s5_lam_im"][j]
            log_dt = a["s5_log_dt"][j][:, :, None]
            b_re_t = jnp.transpose(a["s5_b_re"][j], (0, 3, 1, 2))
            b_im_t = jnp.transpose(a["s5_b_im"][j], (0, 3, 1, 2))
            sv["prep_in"] = (lam_re, lam_im, log_dt, b_re_t, b_im_t)
            ab_re, ab_im, bb_re, bb_im = _s5_prep(f"s5_prep{i}", *sv["prep_in"])
            sv["ab"] = (ab_re, ab_im)
            ys = []
            for dirn in range(2):
                def blocks(t):
                    return jnp.transpose(t, (1, 0, 2)).reshape(ns, gs, ch, n_state)
                bmat = jnp.concatenate([_block_diag(blocks(bb_re[dirn])), _block_diag(blocks(bb_im[dirn]))], axis=2).astype(BF16)
                c_re_t = jnp.transpose(a["s5_c_re"][j, dirn], (0, 2, 1)).reshape(ns, gs, n_state, ch)
                c_im_t = jnp.transpose(a["s5_c_im"][j, dirn], (0, 2, 1)).reshape(ns, gs, n_state, ch)
                cmat = jnp.concatenate([_block_diag(c_re_t), -_block_diag(c_im_t)], axis=1).astype(BF16)
                tab = _s5_tables(ab_re[dirn], ab_im[dirn], dirn == 1, False, ns, tr // SUBLANES)
                group = layer_weights(i + dirn)
                cps = _Copies([a[n][idx].astype(BF16) for n, idx, _ in group], [axis for _, _, axis in group])
                h_states, y_dir, *gathered = _s5_fwd(f"s5_fwd{i}_{dirn}", h_cur, bmat, cmat, tab, dirn, cfg, cps)
                weights.update({(n, idx): w[None] for (n, idx, _), w in zip(group, gathered)})
                sv[f"mats{dirn}"] = (jnp.transpose(bmat, (0, 2, 1)), jnp.transpose(cmat, (0, 2, 1)))
                sv[f"states{dirn}"] = h_states
                ys.append(y_dir)
            sv["y"] = ys
            dsk = vec(a["s5_d"][j])
            z = _rowwise(f"gelu{i}", _f_gelu, [x_cur, ys[0], ys[1]], [sh1, sc1], [dsk], [(d, BF16)], [], [], cfg)[0]
            zz = _mm_nn(f"glu{i}", z, weights["s5_w_glu", j], 0, tm_big, min(2 * d, 512))[0]
            bglu = vec(a["s5_b_glu"][j])
            x1, h2 = _rowwise(f"sub1_{i}", f_sub1_s5, [x_cur, zz], [g1, sh2, sc2], [bglu, gain0, bias0],
                              [(d, F32), (d, BF16)], [], [], cfg)
            sv.update(z=z, zz=zz)
        else:
            zz = _mm_nn(f"pw1_{i}", h_cur, weights["cv_w_pw1", j], 0, tm_big, min(2 * d, 512))[0]
            bpw1 = vec(full["cv_b_pw1"][j])
            act = _rowwise(f"cvglu{i}", _f_cvglu, [zz], [], [bpw1], [(d, F32)], [], [], cfg)[0]
            w_dw = full["cv_w_dw"][j]
            cv = _dwconv(f"dwconv{i}", act, w_dw, cfg)
            bdw, lng, lnb = vec(full["cv_b_dw"][j]), vec(full["cv_ln_g"][j]), vec(full["cv_ln_b"][j])
            s_act = _rowwise(f"cvln{i}", _f_cvln, [cv], [], [bdw, lng, lnb], [(d, BF16)], [], [], cfg)[0]
            mm = _mm_nn(f"pw2_{i}", s_act, weights["cv_w_pw2", j], 0, tm_big, d)[0]
            bpw2 = vec(full["cv_b_pw2"][j])
            x1, h2 = _rowwise(f"sub1_{i}", f_sub1_cv, [x_cur, mm], [g1, sh2, sc2], [bpw2, gain0, bias0],
                              [(d, F32), (d, BF16)], [], [], cfg)
            sv.update(zz=zz, act=act, cv=cv, s_act=s_act, mm=mm, w_dw=w_dw)
        dff = weights["mlp_w1", i].shape[2]
        p_act, r_act = _mm_nn(f"mlp1_{i}", h2, weights["mlp_w1", i], 0, tm_big, min(dff, 1024), (BF16, BF16),
                              lambda acc: (jnp.square(jnp.maximum(acc, 0.0)), jnp.maximum(acc, 0.0)))
        m_out = _mm_nn(f"mlp2_{i}", p_act, weights["mlp_w2", i], 0, tm, d)[0]
        shn, scn = (seg(i + 1, 0), seg(i + 1, 1)) if i + 1 < nl else (zero_seg, zero_seg)
        x2, hn = _rowwise(f"sub2_{i}", f_sub2, [x1, m_out], [g2, shn, scn], [gain1, bias1], [(d, F32), (d, BF16)], [], [], cfg,
                          (1,) if (i + 1) % 2 == 0 and i + 1 < nl else ())
        sv.update(x1=x1, h2=h2, p=p_act, r=r_act, m=m_out, shn=shn, scn=scn)
        saved.append(sv)
        x_cur, h_cur = x2, hn

    target = jnp.concatenate([latent_rows(a["loss_target"]), jnp.zeros((nb * lc, d), F32)], axis=0)
    mask = jnp.concatenate([jnp.ones((nb, 1, d), F32), jnp.zeros((1, 1, d), F32)], axis=0)

    def f_loss(xf, tgt, msk):
        err = (xf - tgt) * msk
        part = 0.5 * jnp.sum(jnp.square(err), axis=(0, 1), keepdims=True) / d
        return err / d, jnp.broadcast_to(part, (1, LANES))

    dx_final, loss_part = _rowwise("loss", f_loss, [x_cur, target], [mask], [], [(d, F32)], [], [LANES], cfg)
    loss = lax.psum(loss_part[0, 0], ("x", "y", "c"))

    grads = {n: [None] * a[n].shape[0] for n in WEIGHTS if n not in ("c_ctx", "w_ada", "b_ada")}
    dmod = [[None] * 6 for _ in range(nl)]

    def add_mod(i, q, val):
        dmod[i][q] = val if dmod[i][q] is None else dmod[i][q] + val

    dx_parts, dh_parts = [dx_final], []
    for i in reversed(range(nl)):
        j = i // 2
        sv = saved[i]
        sh1, sc1, g1, sh2, sc2, g2 = (seg(i, q) for q in range(6))
        gain0, bias0, gain1, bias1 = (vec(full["ln_gain"][i, 0]), vec(full["ln_bias"][i, 0]),
                                      vec(full["ln_gain"][i, 1]), vec(full["ln_bias"][i, 1]))
        bwd = _vjp_fn(f_sub2, 2, (len(dx_parts), len(dh_parts)), (0, 1, 2, 3, 4, 5, 6))
        dx1, dm, dg2, dshn, dscn, dgain1, dbias1 = _rowwise(
            f"sub2_bwd{i}", bwd, [sv["x1"], sv["m"]] + dx_parts + dh_parts, [g2, sv["shn"], sv["scn"]], [gain1, bias1],
            [(d, F32), (d, BF16)], [d, d, d], [d, d], cfg)
        add_mod(i, 5, dg2)
        if i + 1 < nl:
            add_mod(i + 1, 0, dshn)
            add_mod(i + 1, 1, dscn)
        da = _mm_nt(f"mlp2_dgrad{i}", dm, weights["mlp_w2", i], 0, tm_big, min(dff, 1024), [sv["r"]], BF16,
                    lambda acc, r: (acc * 2.0 * r,))
        wgrads["mlp_w2", i] = _mm_wgrad_rows(f"mlp2_wgrad{i}", sv["p"], dm, tm_big)
        wgrads["mlp_w1", i] = _mm_wgrad_cols(f"mlp1_wgrad{i}", sv["h2"], da, tm_big)
        dh2 = _mm_nt(f"mlp1_dgrad{i}", da, weights["mlp_w1", i], 0, tm, d)
        if i % 2 == 0:
            bglu = vec(a["s5_b_glu"][j])
            bwd = _vjp_fn(f_sub1_s5, 2, (1, 1), (0, 1, 2, 3, 4, 5, 6, 7))
            dxa, dzz, dg1, dsh2, dsc2, dbglu, dgain0, dbias0 = _rowwise(
                f"sub1_bwd{i}", bwd, [sv["x"], sv["zz"], dx1, dh2], [g1, sh2, sc2], [bglu, gain0, bias0],
                [(d, F32), (2 * d, BF16)], [d, d, d], [2 * d, d, d], cfg)
            grads["s5_b_glu"][j] = dbglu[0]
            wgrads["s5_w_glu", j] = _mm_wgrad_cols(f"glu_wgrad{i}", sv["z"], dzz, tm_big)
            dz = _mm_nt(f"glu_dgrad{i}", dzz, weights["s5_w_glu", j], 0, tm, d)
            dsk = vec(a["s5_d"][j])
            bwd = _vjp_fn(_f_gelu, 3, (1,), (0, 1, 3, 4, 5))
            dxb, dy, dsh1, dsc1, ddsk = _rowwise(f"gelu_bwd{i}", bwd, [sv["x"], sv["y"][0], sv["y"][1], dz], [sh1, sc1], [dsk],
                                                 [(d, F32), (d, BF16)], [d, d], [d], cfg, (1,))
            grads["s5_d"][j] = ddsk[0]
            add_mod(i, 0, dsh1)
            add_mod(i, 1, dsc1)
            ab_re, ab_im = sv["ab"]
            dus, d_ab_re, d_ab_im, d_bb_re, d_bb_im, d_c_re, d_c_im = [], [], [], [], [], [], []
            for dirn in range(2):
                bmat_t, cmat_t = sv[f"mats{dirn}"]
                tab = _s5_tables(ab_re[dirn], ab_im[dirn], dirn == 0, True, ns, tr // SUBLANES)
                group = layer_weights(i + 1 - dirn)
                cps = _Copies([wgrads[n, idx] for n, idx, _ in group], ["scatter"] * len(group))
                du, d_b, d_c, d_a, *parts = _s5_bwd(f"s5_bwd{i}_{dirn}", dy, sv[f"states{dirn}"], sv["h"], cmat_t, bmat_t, tab,
                                                    dirn, cfg, cps)
                received.update({(n, idx): p for (n, idx, _), p in zip(group, parts)})
                dus.append(du)
                sw = d_a.shape[2] // 2
                d_a = jnp.sum(d_a, axis=1)
                d_ab_re.append(d_a[:, :sw].reshape(n_grp, n_state))
                d_ab_im.append(d_a[:, sw:].reshape(n_grp, n_state))

                def unblock_b(t):
                    return jnp.transpose(_diag_blocks(t, gs).reshape(n_grp, ch, n_state), (1, 0, 2))

                def unblock_c(t):
                    return jnp.transpose(_diag_blocks(t, gs).reshape(n_grp, n_state, ch), (0, 2, 1))
                d_bb_re.append(unblock_b(d_b[:, :, :sw]))
                d_bb_im.append(unblock_b(d_b[:, :, sw:]))
                d_c_re.append(unblock_c(d_c[:, :sw]))
                d_c_im.append(-unblock_c(d_c[:, sw:]))
            g_lre, g_lim, g_ldt, g_bre, g_bim = _s5_prep_bwd(
                f"s5_prep_bwd{i}", *sv["prep_in"], (jnp.stack(d_ab_re), jnp.stack(d_ab_im), jnp.stack(d_bb_re), jnp.stack(d_bb_im)))
            grads["s5_lam_re"][j], grads["s5_lam_im"][j], grads["s5_log_dt"][j] = g_lre, g_lim, g_ldt[:, :, 0]
            grads["s5_b_re"][j] = jnp.transpose(g_bre, (0, 2, 3, 1))
            grads["s5_b_im"][j] = jnp.transpose(g_bim, (0, 2, 3, 1))
            grads["s5_c_re"][j], grads["s5_c_im"][j] = jnp.stack(d_c_re), jnp.stack(d_c_im)
            dx_parts, dh_parts = [dxa, dxb], dus
        else:
            bpw2 = vec(full["cv_b_pw2"][j])
            bwd = _vjp_fn(f_sub1_cv, 2, (1, 1), (0, 1, 2, 3, 4, 5, 6, 7))
            dxa, dmm, dg1, dsh2, dsc2, dbpw2, dgain0, dbias0 = _rowwise(
                f"sub1_bwd{i}", bwd, [sv["x"], sv["mm"], dx1, dh2], [g1, sh2, sc2], [bpw2, gain0, bias0],
                [(d, F32), (d, BF16)], [d, d, d], [d, d, d], cfg)
            grads["cv_b_pw2"][j] = dbpw2[0]
            wgrads["cv_w_pw2", j] = _mm_wgrad_rows(f"pw2_wgrad{i}", sv["s_act"], dmm, tm_big)
            ds = _mm_nt(f"pw2_dgrad{i}", dmm, weights["cv_w_pw2", j], 0, tm_big, d)
            bdw, lng, lnb = vec(full["cv_b_dw"][j]), vec(full["cv_ln_g"][j]), vec(full["cv_ln_b"][j])
            bwd = _vjp_fn(_f_cvln, 1, (1,), (0, 1, 2, 3))
            dcv, dbdw, dlng, dlnb = _rowwise(f"cvln_bwd{i}", bwd, [sv["cv"], ds], [], [bdw, lng, lnb], [(d, F32)], [], [d, d, d], cfg)
            grads["cv_b_dw"][j], grads["cv_ln_g"][j], grads["cv_ln_b"][j] = dbdw[0], dlng[0], dlnb[0]
            dact = _dwconv(f"dwconv_bwd{i}", dcv, sv["w_dw"][::-1], cfg)
            grads["cv_w_dw"][j] = _dwconv_wgrad(f"dwconv_wgrad{i}", sv["act"], dcv, sv["w_dw"].shape[0], cfg)
            bpw1 = vec(full["cv_b_pw1"][j])
            bwd = _vjp_fn(_f_cvglu, 1, (1,), (0, 1))
            dzz, dbpw1 = _rowwise(f"cvglu_bwd{i}", bwd, [sv["zz"], dact], [], [bpw1], [(2 * d, BF16)], [], [2 * d], cfg)
            grads["cv_b_pw1"][j] = dbpw1[0]
            wgrads["cv_w_pw1", j] = _mm_wgrad_cols(f"pw1_wgrad{i}", sv["h"], dzz, tm_big)
            dh = _mm_nt(f"pw1_dgrad{i}", dzz, weights["cv_w_pw1", j], 0, tm, d)
            dx_parts, dh_parts = [dxa], [dh]
        grads["ln_gain"][i] = jnp.stack([dgain0[0], dgain1[0]])
        grads["ln_bias"][i] = jnp.stack([dbias0[0], dbias1[0]])
        add_mod(i, 2, dg1)
        add_mod(i, 3, dsh2)
        add_mod(i, 4, dsc2)
    bwd = _vjp_fn(_f_entry, 2, (len(dx_parts), len(dh_parts)), (0, 2, 3))
    dxc, dsh1, dsc1 = _rowwise("entry_bwd", bwd, [xc, pos_rows] + dx_parts + dh_parts, [seg(0, 0), seg(0, 1)], [],
                               [(d, F32)], [d, d], [], cfg)
    add_mod(0, 0, dsh1)
    add_mod(0, 1, dsc1)
    grad_x = jnp.transpose(dxc[:nb * seq].reshape(tpl, nb, tr, d), (1, 0, 2, 3)).reshape(nb, seq, d)

    dmod_loc = jnp.stack([jnp.concatenate([q[:, 0] for q in dmod[i]], axis=1) for i in range(nl)])
    dmod_all = _exchange("gather_dmod", [dmod_loc.reshape(nl * (nb + 1), 6 * d)], ["slot"])[0].reshape(N_DEV, nl, nb + 1, 6 * d)
    mine = lax.dynamic_slice(dmod_all, (0, 0, 0, me * n_ada), (N_DEV, nl, nb + 1, n_ada))
    dmod_rows = jnp.transpose(mine[:, :, :nb], (1, 0, 2, 3)).reshape(nl, N_DEV * nb, n_ada)
    dmod_rows = jnp.concatenate([dmod_rows, jnp.zeros((nl, SUBLANES, n_ada), F32)], axis=1)
    g_w_ada, _, dcond = _ada_bwd(cc, a["w_ada"], dmod_rows, mine[:, :, nb:])
    g_b_ada = _sum_lead("b_ada_sum", jnp.transpose(dmod_all, (0, 2, 1, 3)).reshape(N_DEV * (nb + 1), nl, 6 * d), nl)

    small_names = SHARDED_SMALL + REPLICATED_SMALL
    small_full = [jnp.stack(grads[n]) for n in small_names]
    small_packed = _pack(small_full, N_DEV * SUBLANES)
    small_parts, dcond_all = _exchange("scatter_small_grads", [small_packed.reshape(N_DEV, -1, LANES), dcond[N_DEV * nb:N_DEV * nb + 1]],
                                       ["scatter", "slot"])
    g_c_ctx = _cctx_grad(dcond_all, a["c_ctx"][None])[0]
    small_part = _sum_lead("small_grad_sum", small_parts, _row_tile(small_parts.shape[1], 512))
    small_sum = _exchange("gather_small_sum", [small_part], ["slot"])[0]
    small_g = dict(zip(small_names, _unpack(small_sum, [g.shape for g in small_full])))
    for n in SHARDED_SMALL:
        width = a[n].shape[-1]
        start = (0,) * (small_g[n].ndim - 1) + (me * width,)
        small_g[n] = lax.dynamic_slice(small_g[n], start, a[n].shape)
    small_g["c_ctx"], small_g["b_ada"] = g_c_ctx, g_b_ada

    out = {}

    def update(n, parts):
        shp = a[n].shape
        cols = parts.shape[-1]
        rows = parts.shape[1]
        res = _adamw(f"adamw_{n}", parts, a[n].reshape(rows, cols), a["m_" + n].reshape(rows, cols), a["v_" + n].reshape(rows, cols),
                     _row_tile(rows, max(SUBLANES, 131072 // cols)))
        out[n] = [r.reshape(shp) for r in res]

    for n in BIG:
        rows, cols = a[n].shape[1:]
        bufs = [lax.empty(a[n].shape, F32) for _ in range(4)]
        for idx in range(a[n].shape[0]):
            bufs = _adamw_layer(f"adamw_{n}{idx}", received[n, idx], a[n], a["m_" + n], a["v_" + n], bufs, idx,
                                _row_tile(rows, max(SUBLANES, 131072 // cols)))
        out[n] = bufs
    update("w_ada", g_w_ada.reshape(1, -1, n_ada))
    for n in NATIVE_SMALL:
        out[n] = [small_g[n], *_adamw_native(f"adamw_{n}", small_g[n], a[n], a["m_" + n], a["v_" + n])]
    small_all_names = ("c_ctx", "b_ada") + tuple(n for n in small_names if n not in NATIVE_SMALL)
    packed = [_pack([src[n] for n in small_all_names]) for src in
              (small_g, a, {n: a["m_" + n] for n in small_all_names}, {n: a["v_" + n] for n in small_all_names})]
    res = _adamw("adamw_small", packed[0][None], packed[1], packed[2], packed[3], _row_tile(packed[0].shape[0], 512))
    shapes = [a[n].shape for n in small_all_names]
    for n, vals in zip(small_all_names, zip(*[_unpack(r, shapes) for r in res])):
        out[n] = list(vals)
    return (loss, grad_x, *[out[n][0] for n in WEIGHTS], *[out[n][1] for n in WEIGHTS],
            *[out[n][2] for n in WEIGHTS], *[out[n][3] for n in WEIGHTS])


def kernel(x, c, ctx, c_ctx, w_ada, b_ada, ln_gain, ln_bias, s5_lam_re, s5_lam_im, s5_log_dt, s5_b_re, s5_b_im, s5_c_re, s5_c_im, s5_d, s5_w_glu, s5_b_glu, cv_w_pw1, cv_b_pw1, cv_w_dw, cv_b_dw, cv_ln_g, cv_ln_b, cv_w_pw2, cv_b_pw2, mlp_w1, mlp_w2, loss_target, m_c_ctx, m_w_ada, m_b_ada, m_ln_gain, m_ln_bias, m_s5_lam_re, m_s5_lam_im, m_s5_log_dt, m_s5_b_re, m_s5_b_im, m_s5_c_re, m_s5_c_im, m_s5_d, m_s5_w_glu, m_s5_b_glu, m_cv_w_pw1, m_cv_b_pw1, m_cv_w_dw, m_cv_b_dw, m_cv_ln_g, m_cv_ln_b, m_cv_w_pw2, m_cv_b_pw2, m_mlp_w1, m_mlp_w2, v_c_ctx, v_w_ada, v_b_ada, v_ln_gain, v_ln_bias, v_s5_lam_re, v_s5_lam_im, v_s5_log_dt, v_s5_b_re, v_s5_b_im, v_s5_c_re, v_s5_c_im, v_s5_d, v_s5_w_glu, v_s5_b_glu, v_cv_w_pw1, v_cv_b_pw1, v_cv_w_dw, v_cv_b_dw, v_cv_ln_g, v_cv_ln_b, v_cv_w_pw2, v_cv_b_pw2, v_mlp_w1, v_mlp_w2):
    return _step(dict(locals()))
```

```python
import functools
import math

import jax
import jax.numpy as jnp
from jax import lax
from jax.experimental import pallas as pl
from jax.experimental.pallas import tpu as pltpu

F32 = jnp.float32
BF16 = jnp.bfloat16
N_DEV = 8
LANES = 128
SUBLANES = 8
VMEM_LIMIT = 56 * 1024 * 1024
GRID_W = 64
POS_TEMP = 10000.0
LN_EPS = 1e-5
LAMBDA_RE_MAX = -1e-4
ADAM_LR, ADAM_B1, ADAM_B2, ADAM_EPS, ADAM_WD, ADAM_STEP = 0.001, 0.9, 0.999, 1e-08, 0.01, 10
MESH = pl.DeviceIdType.MESH


def _params(sem):
    return pltpu.CompilerParams(dimension_semantics=sem, vmem_limit_bytes=VMEM_LIMIT)


def _accumulate(ref, val, first):
    @pl.when(first)
    def _():
        ref[...] = val

    @pl.when(jnp.logical_not(first))
    def _():
        ref[...] += val


def _rowwise(name, fn, rows, segs, vecs, row_outs, seg_accs, vec_accs, cfg, interleaved=()):
    tr, tpl, nb = cfg["tr"], cfg["tpl"], cfg["nb"]
    n_rows = rows[0].shape[0]
    nt = n_rows // tr
    nr, ns, nv = len(rows), len(segs), len(vecs)
    nro, nsa, nva = len(row_outs), len(seg_accs), len(vec_accs)

    def seg_of(t):
        return jnp.minimum(t // tpl, nb)

    def body(*refs):
        t = pl.program_id(0)
        ins, outs, mix_refs = refs[:nr + ns + nv], refs[nr + ns + nv:nr + ns + nv + nro + nsa + nva], refs[nr + ns + nv + nro + nsa + nva:]
        vals = [r[...] for r in ins[:nr]] + [r[0] for r in ins[nr:nr + ns]] + [r[...] for r in ins[nr + ns:]]
        res = fn(*vals)
        for idx, (o, v) in enumerate(zip(outs[:nro], res[:nro])):
            if idx in interleaved:
                mix_ref = mix_refs[interleaved.index(idx)]
                for p in range(mix_ref.shape[0]):
                    mix_ref[p] = v[:, p * LANES:(p + 1) * LANES].astype(F32)
                o[...] = jnp.concatenate([_interleave_rows(mix_ref.at[p], 1, o.dtype) for p in range(mix_ref.shape[0])], axis=1)
            else:
                o[...] = v.astype(o.dtype)
        first_seg = jnp.logical_or(t == 0, seg_of(t) != seg_of(jnp.maximum(t - 1, 0)))
        for o, v in zip(outs[nro:nro + nsa], res[nro:nro + nsa]):
            _accumulate(o.at[0], v, first_seg)
        for o, v in zip(outs[nro + nsa:], res[nro + nsa:]):
            _accumulate(o, v, t == 0)

    in_specs = ([pl.BlockSpec((tr, a.shape[1]), lambda t: (_phys_tile(t, cfg), 0)) for a in rows]
                + [pl.BlockSpec((1, 1, a.shape[2]), lambda t: (seg_of(t), 0, 0)) for a in segs]
                + [pl.BlockSpec((1, a.shape[1]), lambda t: (0, 0)) for a in vecs])
    out_specs = ([pl.BlockSpec((tr, c), lambda t: (_phys_tile(t, cfg), 0)) for c, _ in row_outs]
                 + [pl.BlockSpec((1, 1, c), lambda t: (seg_of(t), 0, 0)) for c in seg_accs]
                 + [pl.BlockSpec((1, c), lambda t: (0, 0)) for c in vec_accs])
    out_shape = ([jax.ShapeDtypeStruct((n_rows, c), dt) for c, dt in row_outs]
                 + [jax.ShapeDtypeStruct((nb + 1, 1, c), F32) for c in seg_accs]
                 + [jax.ShapeDtypeStruct((1, c), F32) for c in vec_accs])
    return pl.pallas_call(body, name=name, grid=(nt,), in_specs=in_specs, out_specs=out_specs, out_shape=out_shape,
                          scratch_shapes=[pltpu.VMEM((row_outs[idx][0] // LANES, tr, LANES), F32) for idx in interleaved],
                          compiler_params=_params(("arbitrary",)))(*rows, *segs, *vecs)


def _vjp_fn(fn, n_row, cot_groups, want):
    n_cot = sum(cot_groups)

    def bwd(*args):
        primals = [a.astype(F32) for a in args[:n_row] + args[n_row + n_cot:]]
        outs, vjp = jax.vjp(fn, *primals)
        cots, pos = [], n_row
        for n, o in zip(cot_groups, outs):
            cot = jnp.zeros_like(o)
            for part in args[pos:pos + n]:
                cot = cot + part.astype(F32)
            cots.append(cot)
            pos += n
        grads = vjp(tuple(cots))
        return tuple(grads[i] for i in want)
    return bwd


def _ln(r, g, b):
    mu = jnp.mean(r, axis=-1, keepdims=True)
    var = jnp.mean(jnp.square(r - mu), axis=-1, keepdims=True)
    return (r - mu) * lax.rsqrt(var + LN_EPS) * g + b


def _glu(zz, bias):
    d = zz.shape[1] // 2
    return (zz[:, :d] + bias[:, :d]) * jax.nn.sigmoid(zz[:, d:] + bias[:, d:])


def _f_entry(xc, pos, sh, sc):
    x0 = xc + pos
    return x0, x0 * (1 + sc) + sh


def _f_gelu(x, y0, y1, sh, sc, dsk):
    u = x * (1 + sc) + sh
    y = dsk * u + y0 + y1
    return (0.5 * y * (1.0 + lax.erf(y * (2.0 ** -0.5))),)


def _make_sub1_s5(alpha):
    def f(x, zz, g1, sh2, sc2, bglu, gain, bias):
        x1 = _ln(alpha * x + g1 * _glu(zz, bglu), gain, bias)
        return x1, x1 * (1 + sc2) + sh2
    return f


def _make_sub1_cv(alpha):
    def f(x, mm, g1, sh2, sc2, bpw2, gain, bias):
        x1 = _ln(alpha * x + g1 * (mm + bpw2), gain, bias)
        return x1, x1 * (1 + sc2) + sh2
    return f


def _make_sub2(alpha):
    def f(x1, m, g2, shn, scn, gain, bias):
        x2 = _ln(alpha * x1 + g2 * m, gain, bias)
        return x2, x2 * (1 + scn) + shn
    return f


def _f_cvglu(zz, bpw1):
    return (_glu(zz, bpw1),)


def _f_cvln(cv, bdw, lng, lnb):
    return (jax.nn.silu(_ln(cv + bdw, lng, lnb)),)


def _matmul(name, a, b, extras, grid, a_spec, b_spec, extra_specs, o_specs, out_shape, dims, red_axis, epi, sem, acc_shape=None):
    n_extra = len(extras)
    n_out = len(out_shape)
    if acc_shape is None:
        acc_shape = tuple(s for s in o_specs[0].block_shape if s is not None)

    def body(*refs):
        a_ref, b_ref = refs[0], refs[1]
        ex = refs[2:2 + n_extra]
        outs = refs[2 + n_extra:2 + n_extra + n_out]
        prod = lax.dot_general(a_ref[...], b_ref[...], dims, preferred_element_type=F32)

        def finish(acc):
            res = epi(acc, *[e[...] for e in ex]) if epi is not None else (acc,)
            for o, v in zip(outs, res):
                o[...] = v.astype(o.dtype)

        if red_axis is None:
            finish(prod)
        else:
            acc_ref = refs[-1]
            k = pl.program_id(red_axis)
            nk = pl.num_programs(red_axis)

            @pl.when(k == 0)
            def _():
                acc_ref[...] = prod

            @pl.when(k > 0)
            def _():
                acc_ref[...] += prod

            @pl.when(k == nk - 1)
            def _():
                finish(acc_ref[...])

    scratch = [] if red_axis is None else [pltpu.VMEM(acc_shape, F32)]
    res = pl.pallas_call(body, name=name, grid=grid, in_specs=[a_spec, b_spec] + list(extra_specs),
                         out_specs=list(o_specs), out_shape=list(out_shape), scratch_shapes=scratch,
                         compiler_params=_params(sem))(a, b, *extras)
    return res


NN = (((1,), (0,)), ((), ()))
NT = (((1,), (1,)), ((), ()))
TN = (((0,), (0,)), ((), ()))


def _mm_nn(name, a, w3, layer, tm, tn, out_dtypes=(F32,), epi=None):
    m, k = a.shape
    n = w3.shape[2]
    return _matmul(name, a, w3, (), (n // tn, m // tm),
                   pl.BlockSpec((tm, k), lambda j, i: (i, 0)), pl.BlockSpec((None, k, tn), lambda j, i: (layer, 0, j)), (),
                   [pl.BlockSpec((tm, tn), lambda j, i: (i, j)) for _ in out_dtypes],
                   [jax.ShapeDtypeStruct((m, n), dt) for dt in out_dtypes], NN, None, epi, ("arbitrary", "arbitrary"))


def _mm_nt(name, dy, w3, layer, tm, tkw, extras=(), out_dtype=F32, epi=None):
    m, n = dy.shape
    kw = w3.shape[1]
    return _matmul(name, dy, w3, tuple(extras), (kw // tkw, m // tm),
                   pl.BlockSpec((tm, n), lambda j, i: (i, 0)), pl.BlockSpec((None, tkw, n), lambda j, i: (layer, j, 0)),
                   [pl.BlockSpec((tm, tkw), lambda j, i: (i, j)) for _ in extras],
                   [pl.BlockSpec((tm, tkw), lambda j, i: (i, j))], [jax.ShapeDtypeStruct((m, kw), out_dtype)], NT, None, epi,
                   ("arbitrary", "arbitrary"))[0]


def _mm_wgrad_cols(name, a, dy, tm):
    m, k = a.shape
    n = dy.shape[1] // N_DEV
    per = max(1, min(N_DEV, 1024 // n))
    return _matmul(name, a, dy, (), (N_DEV // per, m // tm),
                   pl.BlockSpec((tm, k), lambda j, i: (i, 0)), pl.BlockSpec((tm, per * n), lambda j, i: (i, j)), (),
                   [pl.BlockSpec((per, k, n), lambda j, i: (j, 0, 0))], [jax.ShapeDtypeStruct((N_DEV, k, n), BF16)],
                   TN, 1, lambda acc: (jnp.stack([acc[:, q * n:(q + 1) * n] for q in range(per)]),), ("arbitrary", "arbitrary"),
                   (k, per * n))[0]


def _mm_wgrad_rows(name, a, dy, tm):
    m, k = a.shape
    r = k // N_DEV
    n = dy.shape[1]
    rows = r * max(1, min(N_DEV, 1024 // r))
    out = _matmul(name, a, dy, (), (k // rows, m // tm),
                  pl.BlockSpec((tm, rows), lambda j, i: (i, j)), pl.BlockSpec((tm, n), lambda j, i: (i, 0)), (),
                  [pl.BlockSpec((rows, n), lambda j, i: (j, 0))], [jax.ShapeDtypeStruct((k, n), BF16)],
                  TN, 1, None, ("arbitrary", "arbitrary"))[0]
    return out.reshape(N_DEV, r, n)


class _Copies:
    def __init__(self, arrays, kinds):
        self.arrays, self.kinds, self.n = list(arrays), list(kinds), len(arrays)
        any_spec = pl.BlockSpec(memory_space=pl.ANY)
        self.in_specs = [any_spec] * self.n
        self.out_specs = [any_spec] * self.n
        self.out_shape = [jax.ShapeDtypeStruct(self._result(a, kind), a.dtype) for a, kind in zip(arrays, kinds)]
        self.scratch = [pltpu.SemaphoreType.DMA((self.n, N_DEV - 1)), pltpu.SemaphoreType.DMA((self.n, N_DEV - 1)),
                        pltpu.SemaphoreType.DMA((self.n,))] if self.n else []

    @staticmethod
    def _result(a, kind):
        if kind == "slot":
            return (N_DEV,) + a.shape
        if kind == "scatter":
            return a.shape
        return a.shape[:kind] + (N_DEV * a.shape[kind],) + a.shape[kind + 1:]

    def descriptors(self, ins, outs, sems):
        send_sems, recv_sems, local_sems = sems
        x, y, c = lax.axis_index("x"), lax.axis_index("y"), lax.axis_index("c")
        me = 4 * x + 2 * y + c
        first, relay, finish = [], [], []

        def remote(i, k, src, dst, to):
            return pltpu.make_async_remote_copy(src_ref=src, dst_ref=dst, send_sem=send_sems.at[i, k], recv_sem=recv_sems.at[i, k],
                                                device_id=to, device_id_type=MESH)

        for i, kind in enumerate(self.kinds):
            if kind in ("slot", "scatter"):
                scatter = kind == "scatter"
                local = pltpu.make_async_copy(ins[i].at[me] if scatter else ins[i], outs[i].at[me], local_sems.at[i])
                first.append(local)
                finish.append((local, "all"))
                for k in range(1, N_DEV):
                    px = 1 - x if k & 4 else x
                    py = 1 - y if k & 2 else y
                    pc = 1 - c if k & 1 else c
                    cp = remote(i, k - 1, ins[i].at[4 * px + 2 * py + pc] if scatter else ins[i], outs[i].at[me], (px, py, pc))
                    first.append(cp)
                    finish.append((cp, "all"))
                continue
            size = ins[i].shape[kind]

            def block(px, py, pc):
                return outs[i].at[(slice(None),) * kind + (pl.ds(pl.multiple_of((4 * px + 2 * py + pc) * size, size), size),)]

            local = pltpu.make_async_copy(ins[i], block(x, y, c), local_sems.at[i])
            sibling = remote(i, 0, ins[i], block(x, y, c), (x, y, 1 - c))
            first += [local, sibling]
            finish += [(local, "all"), (sibling, "all")]
            for j, (qx, qy) in enumerate([(1 - x, y), (x, 1 - y), (1 - x, 1 - y)]):
                out = remote(i, 1 + j, ins[i], block(x, y, c), (qx, qy, c))
                onward = remote(i, 4 + j, block(qx, qy, c), block(qx, qy, c), (x, y, 1 - c))
                first.append(out)
                relay.append((out, onward))
                finish += [(out, "send"), (onward, "all")]
        return first, relay, finish


def _guarded(when, fn):
    if when is None:
        fn()
    else:
        pl.when(when)(fn)


def _start_all(plan, when=None):
    def run():
        for cp in plan[0]:
            cp.start()
    _guarded(when, run)


def _relay_all(plan, when=None):
    def run():
        for arrived, onward in plan[1]:
            arrived.wait_recv()
            onward.start()
    if plan[1]:
        _guarded(when, run)


def _wait_all(plan, when=None):
    def run():
        for cp, left in plan[2]:
            if left == "send":
                cp.wait_send()
            else:
                cp.wait()
    _guarded(when, run)


def _exchange(name, arrays, kinds):
    cps = _Copies(arrays, kinds)
    n = cps.n

    def body(*refs):
        plan = cps.descriptors(refs[:n], refs[n:2 * n], refs[2 * n:])
        _start_all(plan)
        _relay_all(plan)
        _wait_all(plan)

    return pl.pallas_call(body, name=name, in_specs=cps.in_specs, out_specs=cps.out_specs, out_shape=cps.out_shape,
                          scratch_shapes=cps.scratch)(*arrays)


def _carried(cps, n_in, n_out, n_scratch, refs):
    if cps is None:
        return ([], [], []), refs
    n = cps.n
    ins = refs[n_in:n_in + n]
    outs = refs[n_in + n + n_out:n_in + n + n_out + n]
    sems = refs[n_in + n + n_out + n + n_scratch:]
    own = refs[:n_in] + refs[n_in + n:n_in + n + n_out] + refs[n_in + n + n_out + n:n_in + n + n_out + n + n_scratch]
    return cps.descriptors(ins, outs, sems), own


def _sum_lead(name, parts, tr):
    npart, r, c = parts.shape

    def body(p_ref, o_ref):
        acc = p_ref[0].astype(F32)
        for p in range(1, npart):
            acc = acc + p_ref[p].astype(F32)
        o_ref[...] = acc

    return pl.pallas_call(body, name=name, grid=(r // tr,), in_specs=[pl.BlockSpec((npart, tr, c), lambda i: (0, i, 0))],
                          out_specs=pl.BlockSpec((tr, c), lambda i: (i, 0)), out_shape=jax.ShapeDtypeStruct((r, c), F32),
                          compiler_params=_params(("arbitrary",)))(parts)


def _adamw_math(g, w, m, v):
    m2 = ADAM_B1 * m + (1.0 - ADAM_B1) * g
    v2 = ADAM_B2 * v + (1.0 - ADAM_B2) * jnp.square(g)
    m_hat = m2 / (1.0 - ADAM_B1 ** ADAM_STEP)
    v_hat = v2 / (1.0 - ADAM_B2 ** ADAM_STEP)
    return -ADAM_LR * (m_hat / (jnp.sqrt(v_hat) + ADAM_EPS) + ADAM_WD * w), m2, v2


def _adamw_body(npart):
    def body(p_ref, w_ref, m_ref, v_ref, *rest):
        g_out, d_out, m_out, v_out = rest[-4:]
        g = p_ref[0].astype(F32)
        for p in range(1, npart):
            g = g + p_ref[p].astype(F32)
        g_out[...] = g
        d_out[...], m_out[...], v_out[...] = _adamw_math(g, w_ref[...], m_ref[...], v_ref[...])
    return body


def _adamw_native(name, g, w, m, v):
    rest = w.shape[2:]
    spec = pl.BlockSpec((None, None) + rest, lambda i, j: (i, j) + (0,) * len(rest))

    def body(g_ref, w_ref, m_ref, v_ref, d_out, m_out, v_out):
        d_out[...], m_out[...], v_out[...] = _adamw_math(g_ref[...], w_ref[...], m_ref[...], v_ref[...])

    return pl.pallas_call(body, name=name, grid=w.shape[:2], in_specs=[spec] * 4, out_specs=[spec] * 3,
                          out_shape=[jax.ShapeDtypeStruct(w.shape, F32)] * 3,
                          compiler_params=_params(("arbitrary", "arbitrary")))(g, w, m, v)


def _adamw(name, parts, w, m, v, tr):
    npart, r, c = parts.shape
    row = pl.BlockSpec((tr, c), lambda i: (i, 0))
    return pl.pallas_call(_adamw_body(npart), name=name, grid=(r // tr,),
                          in_specs=[pl.BlockSpec((npart, tr, c), lambda i: (0, i, 0)), row, row, row],
                          out_specs=[row] * 4, out_shape=[jax.ShapeDtypeStruct((r, c), F32)] * 4,
                          compiler_params=_params(("arbitrary",)))(parts, w, m, v)


def _adamw_layer(name, parts, w3, m3, v3, bufs, layer, tr):
    npart, r, c = parts.shape
    lay = pl.BlockSpec((None, tr, c), lambda i: (layer, i, 0))
    hbm = pl.BlockSpec(memory_space=pl.ANY)
    return pl.pallas_call(_adamw_body(npart), name=name, grid=(r // tr,),
                          in_specs=[pl.BlockSpec((npart, tr, c), lambda i: (0, i, 0)), lay, lay, lay] + [hbm] * 4,
                          out_specs=[lay] * 4, out_shape=[jax.ShapeDtypeStruct(w3.shape, F32)] * 4,
                          input_output_aliases={4: 0, 5: 1, 6: 2, 7: 3},
                          compiler_params=_params(("arbitrary",)))(parts, w3, m3, v3, *bufs)


def _row_tile(r, cap):
    if r <= cap:
        return r
    t = cap - cap % SUBLANES
    while r % t:
        t -= SUBLANES
    return t


def _ada_fwd(cc, w_ada, b_loc):
    nl, d, n = w_ada.shape
    rows = cc.shape[0]

    def body(c_ref, w_ref, b_ref, o_ref):
        cond = jax.nn.silu(c_ref[...]).astype(BF16)
        o_ref[...] = jnp.dot(cond, w_ref[...].astype(BF16), preferred_element_type=F32) + b_ref[...]

    return pl.pallas_call(body, name="ada_fwd", grid=(nl,),
                          in_specs=[pl.BlockSpec((rows, d), lambda i: (0, 0)), pl.BlockSpec((None, d, n), lambda i: (i, 0, 0)),
                                    pl.BlockSpec((None, 1, n), lambda i: (i, 0, 0))],
                          out_specs=pl.BlockSpec((None, rows, n), lambda i: (i, 0, 0)),
                          out_shape=jax.ShapeDtypeStruct((nl, rows, n), F32), compiler_params=_params(("arbitrary",)))(cc, w_ada, b_loc)


def _ada_bwd(cc, w_ada, dmod_rows, dmod_ctx):
    nl, d, n = w_ada.shape
    rows = cc.shape[0]
    ctx_row = rows - SUBLANES

    def body(c_ref, w_ref, dr_ref, dc_ref, gw_ref, tot_ref, dcond_ref):
        i = pl.program_id(0)
        total = dc_ref[0]
        for p in range(1, N_DEV):
            total = total + dc_ref[p]
        tot_ref[...] = total
        row_id = lax.broadcasted_iota(jnp.int32, (rows, n), 0)
        dm = jnp.where(row_id == ctx_row, jnp.broadcast_to(total, (rows, n)), dr_ref[...]).astype(BF16)
        cond = jax.nn.silu(c_ref[...]).astype(BF16)
        gw_ref[...] = lax.dot_general(cond, dm, TN, preferred_element_type=F32)
        part = lax.dot_general(dm, w_ref[...].astype(BF16), NT, preferred_element_type=F32)
        _accumulate(dcond_ref, part, i == 0)

    return pl.pallas_call(body, name="ada_bwd", grid=(nl,),
                          in_specs=[pl.BlockSpec((rows, d), lambda i: (0, 0)), pl.BlockSpec((None, d, n), lambda i: (i, 0, 0)),
                                    pl.BlockSpec((None, rows, n), lambda i: (i, 0, 0)),
                                    pl.BlockSpec((N_DEV, None, 1, n), lambda i: (0, i, 0, 0))],
                          out_specs=[pl.BlockSpec((None, d, n), lambda i: (i, 0, 0)), pl.BlockSpec((None, 1, n), lambda i: (i, 0, 0)),
                                     pl.BlockSpec((rows, d), lambda i: (0, 0))],
                          out_shape=[jax.ShapeDtypeStruct((nl, d, n), F32), jax.ShapeDtypeStruct((nl, 1, n), F32),
                                     jax.ShapeDtypeStruct((rows, d), F32)],
                          compiler_params=_params(("arbitrary",)))(cc, w_ada, dmod_rows, dmod_ctx)


def _cctx_grad(parts, c_ctx):
    def body(p_ref, c_ref, o_ref):
        tot = p_ref[0]
        for p in range(1, N_DEV):
            tot = tot + p_ref[p]
        _, vjp = jax.vjp(jax.nn.silu, c_ref[...])
        o_ref[...] = vjp(tot)[0]

    return pl.pallas_call(body, name="cctx_grad", out_shape=jax.ShapeDtypeStruct(c_ctx.shape, F32))(parts, c_ctx)


def _discretise(lam_re, lam_im, log_dt, b_re, b_im):
    lr = jnp.minimum(lam_re, LAMBDA_RE_MAX)
    li = lam_im
    dt = jnp.exp(log_dt)
    mag = jnp.exp(lr * dt)
    ab_re = mag * jnp.cos(li * dt)
    ab_im = mag * jnp.sin(li * dt)
    den = lr * lr + li * li
    nr = ab_re - 1.0
    ni = ab_im
    coef_re = ((nr * lr + ni * li) / den)[:, None]
    coef_im = ((ni * lr - nr * li) / den)[:, None]
    bb_re = coef_re * b_re - coef_im * b_im
    bb_im = coef_re * b_im + coef_im * b_re
    return ab_re, ab_im, bb_re, bb_im


def _s5_prep(name, lam_re, lam_im, log_dt, b_re, b_im):
    def body(a, b, c, d, e, o1, o2, o3, o4):
        res = _discretise(a[...], b[...], c[...], d[...], e[...])
        for o, v in zip((o1, o2, o3, o4), res):
            o[...] = v

    shp = [jax.ShapeDtypeStruct(lam_re.shape, F32)] * 2 + [jax.ShapeDtypeStruct(b_re.shape, F32)] * 2
    return pl.pallas_call(body, name=name, out_shape=shp)(lam_re, lam_im, log_dt, b_re, b_im)


def _s5_prep_bwd(name, lam_re, lam_im, log_dt, b_re, b_im, cots):
    def body(a, b, c, d, e, c1, c2, c3, c4, o1, o2, o3, o4, o5):
        _, vjp = jax.vjp(_discretise, a[...], b[...], c[...], d[...], e[...])
        grads = vjp((c1[...], c2[...], c3[...], c4[...]))
        for o, v in zip((o1, o2, o3, o4, o5), grads):
            o[...] = v

    shp = [jax.ShapeDtypeStruct(a.shape, F32) for a in (lam_re, lam_im, log_dt, b_re, b_im)]
    return pl.pallas_call(body, name=name, out_shape=shp)(lam_re, lam_im, log_dt, b_re, b_im, *cots)


def _interleave_rows(ref, n_seq, dtype):
    n_j = ref.shape[0] // (SUBLANES * n_seq)
    return jnp.concatenate([ref[pl.ds(q * SUBLANES * n_j + j, SUBLANES, stride=n_j), :] for q in range(n_seq) for j in range(n_j)],
                           axis=0).astype(dtype)


def _store_tokens(out_ref, ref, n_seq):
    n_j = ref.shape[0] // (SUBLANES * n_seq)
    for q in range(n_seq):
        for s in range(SUBLANES):
            start = (q * SUBLANES + s) * n_j
            out_ref[start:start + n_j, :] = ref[pl.ds(q * SUBLANES * n_j + s, n_j, stride=SUBLANES), :].astype(out_ref.dtype)


def _expand_powers(t_ref, pow_ref):
    for j in range(pow_ref.shape[0] // SUBLANES):
        row = 5 * SUBLANES + j
        pow_ref[j * SUBLANES:(j + 1) * SUBLANES, :] = jnp.broadcast_to(t_ref[row:row + 1, :], (SUBLANES, pow_ref.shape[1]))


def _scan_tile(h_ref, t_ref, pow_ref, carry_ref, up, n_seq, states_ref=None):
    sw = h_ref.shape[1] // 2
    n_j = h_ref.shape[0] // (SUBLANES * n_seq)
    seqs = range(n_seq)

    def rows(g):
        if isinstance(g, int):
            return pl.ds(g * SUBLANES, SUBLANES)
        return pl.ds(pl.multiple_of(g * SUBLANES, SUBLANES), SUBLANES)

    def at(q, j):
        return rows(q * n_j + j)

    def tab(g):
        return t_ref[rows(g), :sw], t_ref[rows(g), sw:]

    def order(i):
        return n_j - 1 - i if up else i

    def cmul_add(xr, xi, ar, ai, yr, yi):
        return xr + ar * yr - ai * yi, xi + ar * yi + ai * yr

    a_re, a_im = tab(0)

    def local_step(i, xs):
        j = order(i)
        out = []
        for q in seqs:
            xr, xi = cmul_add(h_ref[at(q, j), :sw], h_ref[at(q, j), sw:], a_re, a_im, *xs[q])
            h_ref[at(q, j), :sw] = xr
            h_ref[at(q, j), sw:] = xi
            out.append((xr, xi))
        return tuple(out)

    zero = jnp.zeros((SUBLANES, sw), F32)
    ends = lax.fori_loop(0, n_j, local_step, tuple((zero, zero) for _ in seqs))
    out_row = 0 if up else SUBLANES - 1
    in_row = SUBLANES - 1 if up else 0
    one = SUBLANES - 1 if up else 1
    is_in = lax.broadcasted_iota(jnp.int32, (SUBLANES, sw), 0) == in_row
    carried, enters = [], []
    for q in seqs:
        dr, di = ends[q]
        for level, sh in enumerate((1, 2, 4)):
            amount = SUBLANES - sh if up else sh
            dr, di = cmul_add(dr, di, *tab(1 + level), pltpu.roll(dr, amount, 0), pltpu.roll(di, amount, 0))
        cr, ci = carry_ref[rows(q), :sw], carry_ref[rows(q), sw:]
        dr, di = cmul_add(dr, di, *tab(4), cr, ci)
        carry_ref[rows(q), :sw] = jnp.broadcast_to(dr[out_row:out_row + 1], dr.shape)
        carry_ref[rows(q), sw:] = jnp.broadcast_to(di[out_row:out_row + 1], di.shape)
        carried.append((cr, ci))
        enters.append((jnp.where(is_in, cr, pltpu.roll(dr, one, 0)), jnp.where(is_in, ci, pltpu.roll(di, one, 0))))

    def fix_step(i, state):
        j = order(i)
        nows = []
        for q in seqs:
            xr, xi = cmul_add(h_ref[at(q, j), :sw], h_ref[at(q, j), sw:], pow_ref[rows(j), :sw], pow_ref[rows(j), sw:], *enters[q])
            h_ref[at(q, j), :sw] = xr
            h_ref[at(q, j), sw:] = xi
            nows.append((xr, xi))
        if states_ref is None:
            return state
        befores, (acc_r, acc_i) = state
        for q in seqs:
            lr, li = befores[q]
            hr, hi = states_ref[at(q, j), :sw], states_ref[at(q, j), sw:]
            acc_r, acc_i = acc_r + lr * hr + li * hi, acc_i + li * hr - lr * hi
        return tuple(nows), (acc_r, acc_i)

    if states_ref is None:
        lax.fori_loop(0, n_j, fix_step, 0)
        return None
    lasts, (acc_r, acc_i) = lax.fori_loop(0, n_j, fix_step, (tuple((zero, zero) for _ in seqs), (zero, zero)))
    for q in seqs:
        lr = jnp.where(is_in, carried[q][0], pltpu.roll(lasts[q][0], one, 0))
        li = jnp.where(is_in, carried[q][1], pltpu.roll(lasts[q][1], one, 0))
        hr, hi = states_ref[at(q, order(0)), :sw], states_ref[at(q, order(0)), sw:]
        acc_r, acc_i = acc_r + lr * hr + li * hi, acc_i + li * hr - lr * hi
    return acc_r, acc_i


def _phys_tile(t, cfg):
    tpl, nb = cfg["tpl"], cfg["nb"]
    return jnp.where(t < nb * tpl, (t % tpl) * nb + t // tpl, t)


def _s5_block_index(cfg, dirn, adjoint):
    tpl = cfg["tpl"]

    def idx(k):
        if not adjoint:
            return jnp.where(k == 0, tpl, k - 1 if dirn == 0 else tpl - k)
        return jnp.where(k == tpl, tpl, tpl - 1 - k if dirn == 0 else k)
    return idx


RELAY_AT = 0.75


def _grid_ends(grid):
    step = 0
    for i, n in enumerate(grid):
        step = step * n + pl.program_id(i)
    total = math.prod(grid)
    return step == 0, step == int(RELAY_AT * total), step == total - 1


def _s5_fwd(name, u, bmat, cmat, tab, dirn, cfg, cps=None):
    tr, tpl, nb = cfg["tr"], cfg["tpl"], cfg["nb"]
    n_rows, d = u.shape
    ns, _, sw2 = bmat.shape
    block = _s5_block_index(cfg, dirn, False)
    up = dirn == 1
    grid = (ns, tpl + 1)
    br = nb * tr

    def body(*refs):
        copies, (u_ref, b_ref, c_ref, t_ref, h_ref, y_ref, carry_ref, mix_ref, pow_ref) = _carried(cps, 4, 2, 3, refs)
        first, middle, last = _grid_ends(grid)
        _start_all(copies, first)
        _relay_all(copies, middle)

        @pl.when(pl.program_id(1) == 0)
        def _():
            carry_ref[...] = jnp.zeros_like(carry_ref)
            _expand_powers(t_ref, pow_ref)

        h_ref[...] = jnp.dot(u_ref[...], b_ref[...], preferred_element_type=F32)
        _scan_tile(h_ref, t_ref, pow_ref, carry_ref, up, nb)
        mix_ref[...] = jnp.dot(h_ref[...].astype(BF16), c_ref[...], preferred_element_type=F32)
        _store_tokens(y_ref, mix_ref, nb)
        _wait_all(copies, last)

    extra = cps if cps is not None else _Copies([], [])
    return pl.pallas_call(
        body, name=name, grid=grid,
        in_specs=[pl.BlockSpec((br, LANES), lambda s, k: (block(k), s)),
                  pl.BlockSpec((None, LANES, sw2), lambda s, k: (s, 0, 0)),
                  pl.BlockSpec((None, sw2, LANES), lambda s, k: (s, 0, 0)),
                  pl.BlockSpec((None, tab.shape[1], sw2), lambda s, k: (s, 0, 0))] + extra.in_specs,
        out_specs=[pl.BlockSpec((br, sw2), lambda s, k: (block(k), s)),
                   pl.BlockSpec((br, LANES), lambda s, k: (block(k), s))] + extra.out_specs,
        out_shape=[jax.ShapeDtypeStruct((n_rows, ns * sw2), F32), jax.ShapeDtypeStruct((n_rows, d), F32)] + extra.out_shape,
        scratch_shapes=[pltpu.VMEM((nb * SUBLANES, sw2), F32), pltpu.VMEM((br, LANES), F32), pltpu.VMEM((tr, sw2), F32)] + extra.scratch,
        compiler_params=_params(("arbitrary", "arbitrary")))(u, bmat, cmat, tab, *extra.arrays)


def _s5_bwd(name, dy, h, u, cmat_t, bmat_t, tab, dirn, cfg, cps=None):
    tr, tpl, nb = cfg["tr"], cfg["tpl"], cfg["nb"]
    n_rows, d = u.shape
    ns, _, sw2 = cmat_t.shape
    sw = sw2 // 2
    block = _s5_block_index(cfg, dirn, True)
    up = dirn == 0
    grid = (ns, tpl + 1)
    br = nb * tr

    def body(*refs):
        copies, own = _carried(cps, 6, 4, 4, refs)
        dy_ref, h_ref, u_ref, ct_ref, bt_ref, t_ref, du_ref, db_ref, dc_ref, da_ref, lam_ref, carry_ref, mix_ref, pow_ref = own
        grid_first, grid_middle, grid_last = _grid_ends(grid)
        _start_all(copies, grid_first)
        _relay_all(copies, grid_middle)
        first = pl.program_id(1) == 0

        @pl.when(first)
        def _():
            carry_ref[...] = jnp.zeros_like(carry_ref)
            _expand_powers(t_ref, pow_ref)

        dy = dy_ref[...]
        u_mixed = u_ref[...]
        lam_ref[...] = jnp.dot(dy, ct_ref[...], preferred_element_type=F32)
        acc = _scan_tile(lam_ref, t_ref, pow_ref, carry_ref, up, nb, h_ref)
        lam = lam_ref[...].astype(BF16)
        d_b = lax.dot_general(u_mixed, lam, TN, preferred_element_type=F32)
        d_c = lax.dot_general(h_ref[...].astype(BF16), dy, TN, preferred_element_type=F32)
        mix_ref[...] = jnp.dot(lam, bt_ref[...], preferred_element_type=F32)
        _store_tokens(du_ref, mix_ref, nb)

        @pl.when(first)
        def _():
            db_ref[...] = d_b
            dc_ref[...] = d_c
            da_ref[:, :sw] = acc[0]
            da_ref[:, sw:] = acc[1]

        @pl.when(jnp.logical_not(first))
        def _():
            db_ref[...] += d_b
            dc_ref[...] += d_c
            da_ref[:, :sw] += acc[0]
            da_ref[:, sw:] += acc[1]

        _wait_all(copies, grid_last)

    extra = cps if cps is not None else _Copies([], [])
    return pl.pallas_call(
        body, name=name, grid=grid,
        in_specs=[pl.BlockSpec((br, LANES), lambda s, k: (block(k), s)),
                  pl.BlockSpec((br, sw2), lambda s, k: (block(k), s)),
                  pl.BlockSpec((br, LANES), lambda s, k: (block(k), s)),
                  pl.BlockSpec((None, LANES, sw2), lambda s, k: (s, 0, 0)),
                  pl.BlockSpec((None, sw2, LANES), lambda s, k: (s, 0, 0)),
                  pl.BlockSpec((None, tab.shape[1], sw2), lambda s, k: (s, 0, 0))] + extra.in_specs,
        out_specs=[pl.BlockSpec((br, LANES), lambda s, k: (block(k), s)),
                   pl.BlockSpec((None, LANES, sw2), lambda s, k: (s, 0, 0)),
                   pl.BlockSpec((None, sw2, LANES), lambda s, k: (s, 0, 0)),
                   pl.BlockSpec((None, SUBLANES, sw2), lambda s, k: (s, 0, 0))] + extra.out_specs,
        out_shape=[jax.ShapeDtypeStruct((n_rows, d), F32), jax.ShapeDtypeStruct((ns, LANES, sw2), F32),
                   jax.ShapeDtypeStruct((ns, sw2, LANES), F32), jax.ShapeDtypeStruct((ns, SUBLANES, sw2), F32)] + extra.out_shape,
        scratch_shapes=[pltpu.VMEM((br, sw2), F32), pltpu.VMEM((nb * SUBLANES, sw2), F32), pltpu.VMEM((br, LANES), F32),
                        pltpu.VMEM((tr, sw2), F32)] + extra.scratch,
        compiler_params=_params(("arbitrary", "arbitrary")))(dy, h, u, cmat_t, bmat_t, tab, *extra.arrays)


def _s5_tables(ab_re, ab_im, up, conj, ns, n_j):
    def powers_of(base, count):
        out = [base]
        for _ in range(count - 1):
            q_re, q_im = out[-1]
            out.append((q_re * base[0] - q_im * base[1], q_re * base[1] + q_im * base[0]))
        return out

    def spread(q):
        return jnp.broadcast_to(q[:, None, :], (q.shape[0], SUBLANES, q.shape[1]))

    steps = powers_of((ab_re.reshape(ns, -1), (-ab_im if conj else ab_im).reshape(ns, -1)), n_j)
    jumps = powers_of(steps[-1], SUBLANES)
    rows = jnp.arange(SUBLANES)
    blocks = [tuple(spread(q) for q in steps[0])]
    for sh in (1, 2, 4):
        keep = ((rows <= SUBLANES - 1 - sh) if up else (rows >= sh))[None, :, None]
        blocks.append(tuple(jnp.where(keep, q[:, None, :], 0.0) for q in jumps[sh - 1]))
    dist = range(SUBLANES, 0, -1) if up else range(1, SUBLANES + 1)
    blocks.append(tuple(jnp.stack([jumps[dd - 1][part] for dd in dist], axis=1) for part in (0, 1)))
    ordered = steps[::-1] if up else steps
    blocks.append(tuple(jnp.stack([q[part] for q in ordered], axis=1) for part in (0, 1)))
    return jnp.concatenate([jnp.concatenate([b[0] for b in blocks], axis=1), jnp.concatenate([b[1] for b in blocks], axis=1)], axis=2)


def _block_diag(blocks):
    ns, gs, a, b = blocks.shape
    eye = jnp.eye(gs, dtype=blocks.dtype)
    return (blocks[:, :, :, None, :] * eye[None, :, None, :, None]).reshape(ns, gs * a, gs * b)


def _diag_blocks(mat, gs):
    ns, ra, rb = mat.shape
    a, b = ra // gs, rb // gs
    m5 = mat.reshape(ns, gs, a, gs, b)
    eye = jnp.eye(gs, dtype=mat.dtype)
    return jnp.sum(m5 * eye[None, :, None, :, None], axis=3)


def _conv_flags(t, cfg):
    tpl, nb = cfg["tpl"], cfg["nb"]
    latent = t < nb * tpl
    first = jnp.logical_or(jnp.logical_not(latent), t % tpl == 0)
    last = jnp.logical_or(jnp.logical_not(latent), t % tpl == tpl - 1)
    return first, last


def _fill_ext(ext_ref, prev_ref, cur_ref, next_ref, t, cfg, halo):
    first, last = _conv_flags(t, cfg)
    tr = cur_ref.shape[0]
    for p in range(ext_ref.shape[0]):
        lanes = slice(p * LANES, (p + 1) * LANES)
        ext_ref[p, 0:halo, :] = jnp.where(first, 0.0, prev_ref[:, lanes])
        ext_ref[p, halo:halo + tr, :] = cur_ref[:, lanes]
        ext_ref[p, halo + tr:, :] = jnp.where(last, 0.0, next_ref[:, lanes])


CONV_LANES = 4 * LANES


def _conv_specs(tr, n_rows, halo, cw, cfg):
    per = tr // halo
    n_halo = n_rows // halo
    nb = cfg["nb"]
    return [pl.BlockSpec((halo, cw), lambda c, t: (jnp.maximum((_phys_tile(t, cfg) - nb + 1) * per - 1, 0), c)),
            pl.BlockSpec((tr, cw), lambda c, t: (_phys_tile(t, cfg), c)),
            pl.BlockSpec((halo, cw), lambda c, t: (jnp.minimum((_phys_tile(t, cfg) + nb) * per, n_halo - 1), c))]


def _dwconv(name, a, w, cfg):
    tr = cfg["tr"]
    n_rows, d = a.shape
    kw = w.shape[0]
    half = kw // 2
    halo = 2 * SUBLANES
    cw = min(d, CONV_LANES)

    def body(prev_ref, cur_ref, next_ref, w_ref, o_ref, ext_ref):
        _fill_ext(ext_ref, prev_ref, cur_ref, next_ref, pl.program_id(1), cfg, halo)
        for p in range(cw // LANES):
            lanes = slice(p * LANES, (p + 1) * LANES)
            acc = jnp.zeros((tr, LANES), F32)
            for k in range(kw):
                acc = acc + ext_ref[p, pl.ds(halo - half + k, tr), :] * w_ref[k:k + 1, lanes]
            o_ref[:, lanes] = acc

    return pl.pallas_call(body, name=name, grid=(d // cw, n_rows // tr),
                          in_specs=_conv_specs(tr, n_rows, halo, cw, cfg) + [pl.BlockSpec((kw, cw), lambda c, t: (0, c))],
                          out_specs=pl.BlockSpec((tr, cw), lambda c, t: (_phys_tile(t, cfg), c)),
                          out_shape=jax.ShapeDtypeStruct((n_rows, d), F32),
                          scratch_shapes=[pltpu.VMEM((cw // LANES, tr + 2 * halo, LANES), F32)],
                          compiler_params=_params(("arbitrary", "arbitrary")))(a, a, a, w)


def _dwconv_wgrad(name, a, dout, kw, cfg):
    tr = cfg["tr"]
    n_rows, d = a.shape
    half = kw // 2
    halo = 2 * SUBLANES
    cw = min(d, CONV_LANES)

    def body(prev_ref, cur_ref, next_ref, do_ref, o_ref, ext_ref):
        t = pl.program_id(1)
        _fill_ext(ext_ref, prev_ref, cur_ref, next_ref, t, cfg, halo)
        for p in range(cw // LANES):
            lanes = slice(p * LANES, (p + 1) * LANES)
            dout_t = do_ref[:, lanes]
            rows = [jnp.sum(ext_ref[p, pl.ds(halo - half + k, tr), :] * dout_t, axis=0, keepdims=True) for k in range(kw)]
            _accumulate(o_ref.at[:, lanes], jnp.concatenate(rows, axis=0), t == 0)

    return pl.pallas_call(body, name=name, grid=(d // cw, n_rows // tr),
                          in_specs=_conv_specs(tr, n_rows, halo, cw, cfg) + [pl.BlockSpec((tr, cw), lambda c, t: (_phys_tile(t, cfg), c))],
                          out_specs=pl.BlockSpec((kw, cw), lambda c, t: (0, c)),
                          out_shape=jax.ShapeDtypeStruct((kw, d), F32),
                          scratch_shapes=[pltpu.VMEM((cw // LANES, tr + 2 * halo, LANES), F32)],
                          compiler_params=_params(("arbitrary", "arbitrary")))(a, a, a, dout)


def _sincos_1d(pos, dim):
    quarter = dim // 2
    omega = POS_TEMP ** (-jnp.arange(quarter, dtype=F32) / quarter)
    ang = pos[:, None] * omega[None, :]
    return jnp.concatenate([jnp.sin(ang), jnp.cos(ang)], axis=-1)


def _grid_pos_embed(rows, dim):
    row_idx = jnp.repeat(jnp.arange(rows), GRID_W).astype(F32)
    col_idx = jnp.tile(jnp.arange(GRID_W), rows).astype(F32)
    return jnp.concatenate([_sincos_1d(row_idx, dim // 2), _sincos_1d(col_idx, dim // 2)], axis=-1)


def _pack(arrs, row_multiple=SUBLANES):
    flat = jnp.concatenate([a.reshape(-1).astype(F32) for a in arrs])
    pad = (-flat.shape[0]) % (row_multiple * LANES)
    return jnp.pad(flat, (0, pad)).reshape(-1, LANES)


def _unpack(buf, shapes):
    flat = buf.reshape(-1)
    out, pos = [], 0
    for shp in shapes:
        n = math.prod(shp)
        out.append(flat[pos:pos + n].reshape(shp))
        pos += n
    return out


def _unpack_gathered(buf, shapes):
    flat = buf.reshape(N_DEV, -1)
    out, pos = [], 0
    for shp in shapes:
        n = math.prod(shp)
        part = flat[:, pos:pos + n].reshape((N_DEV,) + tuple(shp))
        out.append(jnp.moveaxis(part, 0, -2).reshape(tuple(shp[:-1]) + (N_DEV * shp[-1],)))
        pos += n
    return out


WEIGHTS = ("c_ctx", "w_ada", "b_ada", "ln_gain", "ln_bias", "s5_lam_re", "s5_lam_im", "s5_log_dt", "s5_b_re", "s5_b_im",
           "s5_c_re", "s5_c_im", "s5_d", "s5_w_glu", "s5_b_glu", "cv_w_pw1", "cv_b_pw1", "cv_w_dw", "cv_b_dw", "cv_ln_g",
           "cv_ln_b", "cv_w_pw2", "cv_b_pw2", "mlp_w1", "mlp_w2")
SHARDED_SMALL = ("ln_gain", "ln_bias", "cv_b_pw1", "cv_w_dw", "cv_b_dw", "cv_ln_g", "cv_ln_b", "cv_b_pw2")
REPLICATED_SMALL = ("s5_lam_re", "s5_lam_im", "s5_log_dt", "s5_b_re", "s5_b_im", "s5_c_re", "s5_c_im", "s5_d", "s5_b_glu")
NATIVE_SMALL = ("s5_lam_re", "s5_lam_im", "s5_b_re", "s5_b_im", "s5_c_re", "s5_c_im")
BIG = ("mlp_w1", "mlp_w2", "s5_w_glu", "cv_w_pw1", "cv_w_pw2")


def _step(a):
    x, c, ctx = a["x"], a["c"], a["ctx"]
    nb, seq, d = x.shape
    lc = ctx.shape[1]
    nl = a["w_ada"].shape[0]
    tr = lc
    tpl = seq // tr
    cfg = {"tr": tr, "tpl": tpl, "nb": nb}
    n_rows = nb * (seq + lc)
    alpha = (2.0 * nl) ** 0.25
    me = 4 * lax.axis_index("x") + 2 * lax.axis_index("y") + lax.axis_index("c")
    n_grp, n_state = a["s5_lam_re"].shape[2:]
    ch = a["s5_b_re"].shape[-1]
    gs = LANES // ch
    ns = d // LANES
    tm = 2 * tr if n_rows % (2 * tr) == 0 else tr
    tm_big = n_rows // 3 if n_rows % (3 * 2 * SUBLANES) == 0 else tm
    tm_mid = n_rows // 6 if n_rows % (6 * 2 * SUBLANES) == 0 else tm
    f_sub1_s5, f_sub1_cv, f_sub2 = _make_sub1_s5(alpha), _make_sub1_cv(alpha), _make_sub2(alpha)

    def layer_weights(i):
        mixer = [("s5_w_glu", i // 2, 1)] if i % 2 == 0 else [("cv_w_pw1", i // 2, 1), ("cv_w_pw2", i // 2, 0)]
        return mixer + [("mlp_w1", i, 1), ("mlp_w2", i, 0)]

    weights, wgrads, received = {}, {}, {}
    small_all, c_all = _exchange("gather_small", [_pack([a[n] for n in SHARDED_SMALL]), c], ["slot", "slot"])
    full = dict(zip(SHARDED_SMALL, _unpack_gathered(small_all, [a[n].shape for n in SHARDED_SMALL])))
    c_all = c_all.reshape(N_DEV * nb, d)
    cond_rows = N_DEV * nb + SUBLANES
    cc = jnp.concatenate([c_all, a["c_ctx"][None], jnp.zeros((SUBLANES - 1, d), F32)], axis=0)

    n_ada = a["w_ada"].shape[2]
    b_loc = lax.dynamic_slice(a["b_ada"], (0, me * n_ada), (nl, n_ada))[:, None, :]
    mod_cols = _ada_fwd(cc, a["w_ada"], b_loc)
    mod_all = _exchange("gather_mod", [mod_cols.reshape(nl * cond_rows, n_ada)], ["slot"])[0].reshape(N_DEV, nl, cond_rows, n_ada)
    mod_mine = jnp.concatenate([lax.dynamic_slice(mod_all, (0, 0, nb * me, 0), (N_DEV, nl, nb, n_ada)),
                                mod_all[:, :, N_DEV * nb:N_DEV * nb + 1]], axis=2)
    mod = jnp.transpose(mod_mine, (1, 2, 0, 3)).reshape(nl, nb + 1, 6, 1, d)

    def seg(i, q):
        return mod[i, :, q]

    zero_seg = jnp.zeros((nb + 1, 1, d), F32)

    def vec(v):
        return v.reshape(1, -1)

    pos = _grid_pos_embed(seq // GRID_W, d)
    def latent_rows(v):
        return jnp.transpose(v.reshape(nb, tpl, tr, d), (1, 0, 2, 3)).reshape(nb * seq, d)

    xc = jnp.concatenate([latent_rows(x), ctx.reshape(nb * lc, d)], axis=0)
    pos_rows = jnp.concatenate([latent_rows(jnp.broadcast_to(pos[None], (nb, seq, d))), jnp.zeros((nb * lc, d), F32)], axis=0)
    x_cur, h_cur = _rowwise("entry", _f_entry, [xc, pos_rows], [seg(0, 0), seg(0, 1)], [], [(d, F32), (d, BF16)], [], [], cfg, (1,))
    saved = []
    for i in range(nl):
        j = i // 2
        sv = {"x": x_cur, "h": h_cur}
        sh1, sc1, g1, sh2, sc2, g2 = (seg(i, q) for q in range(6))
        gain0, bias0, gain1, bias1 = (vec(full["ln_gain"][i, 0]), vec(full["ln_bias"][i, 0]),
                                      vec(full["ln_gain"][i, 1]), vec(full["ln_bias"][i, 1]))
        if i % 2 == 0:
            lam_re, lam_im = a["s5_lam_re"][j], a["s5_lam_im"][j]
            log_dt = a["s5_log_dt"][j][:, :, None]
            b_re_t = jnp.transpose(a["s5_b_re"][j], (0, 3, 1, 2))
            b_im_t = jnp.transpose(a["s5_b_im"][j], (0, 3, 1, 2))
            sv["prep_in"] = (lam_re, lam_im, log_dt, b_re_t, b_im_t)
            ab_re, ab_im, bb_re, bb_im = _s5_prep(f"s5_prep{i}", *sv["prep_in"])
            sv["ab"] = (ab_re, ab_im)
            ys = []
            for dirn in range(2):
                def blocks(t):
                    return jnp.transpose(t, (1, 0, 2)).reshape(ns, gs, ch, n_state)
                bmat = jnp.concatenate([_block_diag(blocks(bb_re[dirn])), _block_diag(blocks(bb_im[dirn]))], axis=2).astype(BF16)
                c_re_t = jnp.transpose(a["s5_c_re"][j, dirn], (0, 2, 1)).reshape(ns, gs, n_state, ch)
                c_im_t = jnp.transpose(a["s5_c_im"][j, dirn], (0, 2, 1)).reshape(ns, gs, n_state, ch)
                cmat = jnp.concatenate([_block_diag(c_re_t), -_block_diag(c_im_t)], axis=1).astype(BF16)
                tab = _s5_tables(ab_re[dirn], ab_im[dirn], dirn == 1, False, ns, tr // SUBLANES)
                group = layer_weights(i + dirn)
                cps = _Copies([a[n][idx].astype(BF16) for n, idx, _ in group], [axis for _, _, axis in group])
                h_states, y_dir, *gathered = _s5_fwd(f"s5_fwd{i}_{dirn}", h_cur, bmat, cmat, tab, dirn, cfg, cps)
                weights.update({(n, idx): w[None] for (n, idx, _), w in zip(group, gathered)})
                sv[f"mats{dirn}"] = (jnp.transpose(bmat, (0, 2, 1)), jnp.transpose(cmat, (0, 2, 1)))
                sv[f"states{dirn}"] = h_states
                ys.append(y_dir)
            sv["y"] = ys
            dsk = vec(a["s5_d"][j])
            z = _rowwise(f"gelu{i}", _f_gelu, [x_cur, ys[0], ys[1]], [sh1, sc1], [dsk], [(d, BF16)], [], [], cfg)[0]
            zz = _mm_nn(f"glu{i}", z, weights["s5_w_glu", j], 0, tm_big, min(2 * d, 1024))[0]
            bglu = vec(a["s5_b_glu"][j])
            x1, h2 = _rowwise(f"sub1_{i}", f_sub1_s5, [x_cur, zz], [g1, sh2, sc2], [bglu, gain0, bias0],
                              [(d, F32), (d, BF16)], [], [], cfg)
            sv.update(z=z, zz=zz)
        else:
            zz = _mm_nn(f"pw1_{i}", h_cur, weights["cv_w_pw1", j], 0, tm_big, min(2 * d, 1024))[0]
            bpw1 = vec(full["cv_b_pw1"][j])
            act = _rowwise(f"cvglu{i}", _f_cvglu, [zz], [], [bpw1], [(d, F32)], [], [], cfg)[0]
            w_dw = full["cv_w_dw"][j]
            cv = _dwconv(f"dwconv{i}", act, w_dw, cfg)
            bdw, lng, lnb = vec(full["cv_b_dw"][j]), vec(full["cv_ln_g"][j]), vec(full["cv_ln_b"][j])
            s_act = _rowwise(f"cvln{i}", _f_cvln, [cv], [], [bdw, lng, lnb], [(d, BF16)], [], [], cfg)[0]
            mm = _mm_nn(f"pw2_{i}", s_act, weights["cv_w_pw2", j], 0, tm_big, d)[0]
            bpw2 = vec(full["cv_b_pw2"][j])
            x1, h2 = _rowwise(f"sub1_{i}", f_sub1_cv, [x_cur, mm], [g1, sh2, sc2], [bpw2, gain0, bias0],
                              [(d, F32), (d, BF16)], [], [], cfg)
            sv.update(zz=zz, act=act, cv=cv, s_act=s_act, mm=mm, w_dw=w_dw)
        dff = weights["mlp_w1", i].shape[2]
        p_act, r_act = _mm_nn(f"mlp1_{i}", h2, weights["mlp_w1", i], 0, tm_big, min(dff, 1024), (BF16, BF16),
                              lambda acc: (jnp.square(jnp.maximum(acc, 0.0)), jnp.maximum(acc, 0.0)))
        m_out = _mm_nn(f"mlp2_{i}", p_act, weights["mlp_w2", i], 0, tm_mid, d)[0]
        shn, scn = (seg(i + 1, 0), seg(i + 1, 1)) if i + 1 < nl else (zero_seg, zero_seg)
        x2, hn = _rowwise(f"sub2_{i}", f_sub2, [x1, m_out], [g2, shn, scn], [gain1, bias1], [(d, F32), (d, BF16)], [], [], cfg,
                          (1,) if (i + 1) % 2 == 0 and i + 1 < nl else ())
        sv.update(x1=x1, h2=h2, p=p_act, r=r_act, m=m_out, shn=shn, scn=scn)
        saved.append(sv)
        x_cur, h_cur = x2, hn

    target = jnp.concatenate([latent_rows(a["loss_target"]), jnp.zeros((nb * lc, d), F32)], axis=0)
    mask = jnp.concatenate([jnp.ones((nb, 1, d), F32), jnp.zeros((1, 1, d), F32)], axis=0)

    def f_loss(xf, tgt, msk):
        err = (xf - tgt) * msk
        part = 0.5 * jnp.sum(jnp.square(err), axis=(0, 1), keepdims=True) / d
        return err / d, jnp.broadcast_to(part, (1, LANES))

    dx_final, loss_part = _rowwise("loss", f_loss, [x_cur, target], [mask], [], [(d, F32)], [], [LANES], cfg)
    loss = lax.psum(loss_part[0, 0], ("x", "y", "c"))

    grads = {n: [None] * a[n].shape[0] for n in WEIGHTS if n not in ("c_ctx", "w_ada", "b_ada")}
    dmod = [[None] * 6 for _ in range(nl)]

    def add_mod(i, q, val):
        dmod[i][q] = val if dmod[i][q] is None else dmod[i][q] + val

    dx_parts, dh_parts = [dx_final], []
    for i in reversed(range(nl)):
        j = i // 2
        sv = saved[i]
        sh1, sc1, g1, sh2, sc2, g2 = (seg(i, q) for q in range(6))
        gain0, bias0, gain1, bias1 = (vec(full["ln_gain"][i, 0]), vec(full["ln_bias"][i, 0]),
                                      vec(full["ln_gain"][i, 1]), vec(full["ln_bias"][i, 1]))
        bwd = _vjp_fn(f_sub2, 2, (len(dx_parts), len(dh_parts)), (0, 1, 2, 3, 4, 5, 6))
        dx1, dm, dg2, dshn, dscn, dgain1, dbias1 = _rowwise(
            f"sub2_bwd{i}", bwd, [sv["x1"], sv["m"]] + dx_parts + dh_parts, [g2, sv["shn"], sv["scn"]], [gain1, bias1],
            [(d, F32), (d, BF16)], [d, d, d], [d, d], cfg)
        add_mod(i, 5, dg2)
        if i + 1 < nl:
            add_mod(i + 1, 0, dshn)
            add_mod(i + 1, 1, dscn)
        da = _mm_nt(f"mlp2_dgrad{i}", dm, weights["mlp_w2", i], 0, tm_big, min(dff, 1024), [sv["r"]], BF16,
                    lambda acc, r: (acc * 2.0 * r,))
        wgrads["mlp_w2", i] = _mm_wgrad_rows(f"mlp2_wgrad{i}", sv["p"], dm, tm_big)
        wgrads["mlp_w1", i] = _mm_wgrad_cols(f"mlp1_wgrad{i}", sv["h2"], da, tm_big)
        dh2 = _mm_nt(f"mlp1_dgrad{i}", da, weights["mlp_w1", i], 0, tm_mid, d)
        if i % 2 == 0:
            bglu = vec(a["s5_b_glu"][j])
            bwd = _vjp_fn(f_sub1_s5, 2, (1, 1), (0, 1, 2, 3, 4, 5, 6, 7))
            dxa, dzz, dg1, dsh2, dsc2, dbglu, dgain0, dbias0 = _rowwise(
                f"sub1_bwd{i}", bwd, [sv["x"], sv["zz"], dx1, dh2], [g1, sh2, sc2], [bglu, gain0, bias0],
                [(d, F32), (2 * d, BF16)], [d, d, d], [2 * d, d, d], cfg)
            grads["s5_b_glu"][j] = dbglu[0]
            wgrads["s5_w_glu", j] = _mm_wgrad_cols(f"glu_wgrad{i}", sv["z"], dzz, tm_big)
            dz = _mm_nt(f"glu_dgrad{i}", dzz, weights["s5_w_glu", j], 0, tm_mid, d)
            dsk = vec(a["s5_d"][j])
            bwd = _vjp_fn(_f_gelu, 3, (1,), (0, 1, 3, 4, 5))
            dxb, dy, dsh1, dsc1, ddsk = _rowwise(f"gelu_bwd{i}", bwd, [sv["x"], sv["y"][0], sv["y"][1], dz], [sh1, sc1], [dsk],
                                                 [(d, F32), (d, BF16)], [d, d], [d], cfg, (1,))
            grads["s5_d"][j] = ddsk[0]
            add_mod(i, 0, dsh1)
            add_mod(i, 1, dsc1)
            ab_re, ab_im = sv["ab"]
            dus, d_ab_re, d_ab_im, d_bb_re, d_bb_im, d_c_re, d_c_im = [], [], [], [], [], [], []
            for dirn in range(2):
                bmat_t, cmat_t = sv[f"mats{dirn}"]
                tab = _s5_tables(ab_re[dirn], ab_im[dirn], dirn == 0, True, ns, tr // SUBLANES)
                group = layer_weights(i + 1 - dirn)
                cps = _Copies([wgrads[n, idx] for n, idx, _ in group], ["scatter"] * len(group))
                du, d_b, d_c, d_a, *parts = _s5_bwd(f"s5_bwd{i}_{dirn}", dy, sv[f"states{dirn}"], sv["h"], cmat_t, bmat_t, tab,
                                                    dirn, cfg, cps)
                received.update({(n, idx): p for (n, idx, _), p in zip(group, parts)})
                dus.append(du)
                sw = d_a.shape[2] // 2
                d_a = jnp.sum(d_a, axis=1)
                d_ab_re.append(d_a[:, :sw].reshape(n_grp, n_state))
                d_ab_im.append(d_a[:, sw:].reshape(n_grp, n_state))

                def unblock_b(t):
                    return jnp.transpose(_diag_blocks(t, gs).reshape(n_grp, ch, n_state), (1, 0, 2))

                def unblock_c(t):
                    return jnp.transpose(_diag_blocks(t, gs).reshape(n_grp, n_state, ch), (0, 2, 1))
                d_bb_re.append(unblock_b(d_b[:, :, :sw]))
                d_bb_im.append(unblock_b(d_b[:, :, sw:]))
                d_c_re.append(unblock_c(d_c[:, :sw]))
                d_c_im.append(-unblock_c(d_c[:, sw:]))
            g_lre, g_lim, g_ldt, g_bre, g_bim = _s5_prep_bwd(
                f"s5_prep_bwd{i}", *sv["prep_in"], (jnp.stack(d_ab_re), jnp.stack(d_ab_im), jnp.stack(d_bb_re), jnp.stack(d_bb_im)))
            grads["s5_lam_re"][j], grads["s5_lam_im"][j], grads["s5_log_dt"][j] = g_lre, g_lim, g_ldt[:, :, 0]
            grads["s5_b_re"][j] = jnp.transpose(g_bre, (0, 2, 3, 1))
            grads["s5_b_im"][j] = jnp.transpose(g_bim, (0, 2, 3, 1))
            grads["s5_c_re"][j], grads["s5_c_im"][j] = jnp.stack(d_c_re), jnp.stack(d_c_im)
            dx_parts, dh_parts = [dxa, dxb], dus
        else:
            bpw2 = vec(full["cv_b_pw2"][j])
            bwd = _vjp_fn(f_sub1_cv, 2, (1, 1), (0, 1, 2, 3, 4, 5, 6, 7))
            dxa, dmm, dg1, dsh2, dsc2, dbpw2, dgain0, dbias0 = _rowwise(
                f"sub1_bwd{i}", bwd, [sv["x"], sv["mm"], dx1, dh2], [g1, sh2, sc2], [bpw2, gain0, bias0],
                [(d, F32), (d, BF16)], [d, d, d], [d, d, d], cfg)
            grads["cv_b_pw2"][j] = dbpw2[0]
            wgrads["cv_w_pw2", j] = _mm_wgrad_rows(f"pw2_wgrad{i}", sv["s_act"], dmm, tm_big)
            ds = _mm_nt(f"pw2_dgrad{i}", dmm, weights["cv_w_pw2", j], 0, tm_big, d)
            bdw, lng, lnb = vec(full["cv_b_dw"][j]), vec(full["cv_ln_g"][j]), vec(full["cv_ln_b"][j])
            bwd = _vjp_fn(_f_cvln, 1, (1,), (0, 1, 2, 3))
            dcv, dbdw, dlng, dlnb = _rowwise(f"cvln_bwd{i}", bwd, [sv["cv"], ds], [], [bdw, lng, lnb], [(d, F32)], [], [d, d, d], cfg)
            grads["cv_b_dw"][j], grads["cv_ln_g"][j], grads["cv_ln_b"][j] = dbdw[0], dlng[0], dlnb[0]
            dact = _dwconv(f"dwconv_bwd{i}", dcv, sv["w_dw"][::-1], cfg)
            grads["cv_w_dw"][j] = _dwconv_wgrad(f"dwconv_wgrad{i}", sv["act"], dcv, sv["w_dw"].shape[0], cfg)
            bpw1 = vec(full["cv_b_pw1"][j])
            bwd = _vjp_fn(_f_cvglu, 1, (1,), (0, 1))
            dzz, dbpw1 = _rowwise(f"cvglu_bwd{i}", bwd, [sv["zz"], dact], [], [bpw1], [(2 * d, BF16)], [], [2 * d], cfg)
            grads["cv_b_pw1"][j] = dbpw1[0]
            wgrads["cv_w_pw1", j] = _mm_wgrad_cols(f"pw1_wgrad{i}", sv["h"], dzz, tm_big)
            dh = _mm_nt(f"pw1_dgrad{i}", dzz, weights["cv_w_pw1", j], 0, tm_mid, d)
            dx_parts, dh_parts = [dxa], [dh]
        grads["ln_gain"][i] = jnp.stack([dgain0[0], dgain1[0]])
        grads["ln_bias"][i] = jnp.stack([dbias0[0], dbias1[0]])
        add_mod(i, 2, dg1)
        add_mod(i, 3, dsh2)
        add_mod(i, 4, dsc2)
    bwd = _vjp_fn(_f_entry, 2, (len(dx_parts), len(dh_parts)), (0, 2, 3))
    dxc, dsh1, dsc1 = _rowwise("entry_bwd", bwd, [xc, pos_rows] + dx_parts + dh_parts, [seg(0, 0), seg(0, 1)], [],
                               [(d, F32)], [d, d], [], cfg)
    add_mod(0, 0, dsh1)
    add_mod(0, 1, dsc1)
    grad_x = jnp.transpose(dxc[:nb * seq].reshape(tpl, nb, tr, d), (1, 0, 2, 3)).reshape(nb, seq, d)

    dmod_loc = jnp.stack([jnp.concatenate([q[:, 0] for q in dmod[i]], axis=1) for i in range(nl)])
    dmod_all = _exchange("gather_dmod", [dmod_loc.reshape(nl * (nb + 1), 6 * d)], ["slot"])[0].reshape(N_DEV, nl, nb + 1, 6 * d)
    mine = lax.dynamic_slice(dmod_all, (0, 0, 0, me * n_ada), (N_DEV, nl, nb + 1, n_ada))
    dmod_rows = jnp.transpose(mine[:, :, :nb], (1, 0, 2, 3)).reshape(nl, N_DEV * nb, n_ada)
    dmod_rows = jnp.concatenate([dmod_rows, jnp.zeros((nl, SUBLANES, n_ada), F32)], axis=1)
    g_w_ada, _, dcond = _ada_bwd(cc, a["w_ada"], dmod_rows, mine[:, :, nb:])
    g_b_ada = _sum_lead("b_ada_sum", jnp.transpose(dmod_all, (0, 2, 1, 3)).reshape(N_DEV * (nb + 1), nl, 6 * d), nl)

    small_names = SHARDED_SMALL + REPLICATED_SMALL
    small_full = [jnp.stack(grads[n]) for n in small_names]
    small_packed = _pack(small_full, N_DEV * SUBLANES)
    small_parts, dcond_all = _exchange("scatter_small_grads", [small_packed.reshape(N_DEV, -1, LANES), dcond[N_DEV * nb:N_DEV * nb + 1]],
                                       ["scatter", "slot"])
    g_c_ctx = _cctx_grad(dcond_all, a["c_ctx"][None])[0]
    small_part = _sum_lead("small_grad_sum", small_parts, _row_tile(small_parts.shape[1], 512))
    small_sum = _exchange("gather_small_sum", [small_part], ["slot"])[0]
    small_g = dict(zip(small_names, _unpack(small_sum, [g.shape for g in small_full])))
    for n in SHARDED_SMALL:
        width = a[n].shape[-1]
        start = (0,) * (small_g[n].ndim - 1) + (me * width,)
        small_g[n] = lax.dynamic_slice(small_g[n], start, a[n].shape)
    small_g["c_ctx"], small_g["b_ada"] = g_c_ctx, g_b_ada

    out = {}

    def update(n, parts):
        shp = a[n].shape
        cols = parts.shape[-1]
        rows = parts.shape[1]
        res = _adamw(f"adamw_{n}", parts, a[n].reshape(rows, cols), a["m_" + n].reshape(rows, cols), a["v_" + n].reshape(rows, cols),
                     _row_tile(rows, max(SUBLANES, 131072 // cols)))
        out[n] = [r.reshape(shp) for r in res]

    for n in BIG:
        rows, cols = a[n].shape[1:]
        bufs = [lax.empty(a[n].shape, F32) for _ in range(4)]
        for idx in range(a[n].shape[0]):
            bufs = _adamw_layer(f"adamw_{n}{idx}", received[n, idx], a[n], a["m_" + n], a["v_" + n], bufs, idx,
                                _row_tile(rows, max(SUBLANES, 131072 // cols)))
        out[n] = bufs
    update("w_ada", g_w_ada.reshape(1, -1, n_ada))
    for n in NATIVE_SMALL:
        out[n] = [small_g[n], *_adamw_native(f"adamw_{n}", small_g[n], a[n], a["m_" + n], a["v_" + n])]
    small_all_names = ("c_ctx", "b_ada") + tuple(n for n in small_names if n not in NATIVE_SMALL)
    packed = [_pack([src[n] for n in small_all_names]) for src in
              (small_g, a, {n: a["m_" + n] for n in small_all_names}, {n: a["v_" + n] for n in small_all_names})]
    res = _adamw("adamw_small", packed[0][None], packed[1], packed[2], packed[3], _row_tile(packed[0].shape[0], 512))
    shapes = [a[n].shape for n in small_all_names]
    for n, vals in zip(small_all_names, zip(*[_unpack(r, shapes) for r in res])):
        out[n] = list(vals)
    return (loss, grad_x, *[out[n][0] for n in WEIGHTS], *[out[n][1] for n in WEIGHTS],
            *[out[n][2] for n in WEIGHTS], *[out[n][3] for n in WEIGHTS])


def kernel(x, c, ctx, c_ctx, w_ada, b_ada, ln_gain, ln_bias, s5_lam_re, s5_lam_im, s5_log_dt, s5_b_re, s5_b_im, s5_c_re, s5_c_im, s5_d, s5_w_glu, s5_b_glu, cv_w_pw1, cv_b_pw1, cv_w_dw, cv_b_dw, cv_ln_g, cv_ln_b, cv_w_pw2, cv_b_pw2, mlp_w1, mlp_w2, loss_target, m_c_ctx, m_w_ada, m_b_ada, m_ln_gain, m_ln_bias, m_s5_lam_re, m_s5_lam_im, m_s5_log_dt, m_s5_b_re, m_s5_b_im, m_s5_c_re, m_s5_c_im, m_s5_d, m_s5_w_glu, m_s5_b_glu, m_cv_w_pw1, m_cv_b_pw1, m_cv_w_dw, m_cv_b_dw, m_cv_ln_g, m_cv_ln_b, m_cv_w_pw2, m_cv_b_pw2, m_mlp_w1, m_mlp_w2, v_c_ctx, v_w_ada, v_b_ada, v_ln_gain, v_ln_bias, v_s5_lam_re, v_s5_lam_im, v_s5_log_dt, v_s5_b_re, v_s5_b_im, v_s5_c_re, v_s5_c_im, v_s5_d, v_s5_w_glu, v_s5_b_glu, v_cv_w_pw1, v_cv_b_pw1, v_cv_w_dw, v_cv_b_dw, v_cv_ln_g, v_cv_ln_b, v_cv_w_pw2, v_cv_b_pw2, v_mlp_w1, v_mlp_w2):
    return _step(dict(locals()))
```

```python
import functools
import math

import jax
import jax.numpy as jnp
from jax import lax
from jax.experimental import pallas as pl
from jax.experimental.pallas import tpu as pltpu

F32 = jnp.float32
BF16 = jnp.bfloat16
N_DEV = 8
LANES = 128
SUBLANES = 8
VMEM_LIMIT = 56 * 1024 * 1024
GRID_W = 64
POS_TEMP = 10000.0
LN_EPS = 1e-5
LAMBDA_RE_MAX = -1e-4
ADAM_LR, ADAM_B1, ADAM_B2, ADAM_EPS, ADAM_WD, ADAM_STEP = 0.001, 0.9, 0.999, 1e-08, 0.01, 10
MESH = pl.DeviceIdType.MESH


def _params(sem):
    return pltpu.CompilerParams(dimension_semantics=sem, vmem_limit_bytes=VMEM_LIMIT)


def _accumulate(ref, val, first):
    @pl.when(first)
    def _():
        ref[...] = val

    @pl.when(jnp.logical_not(first))
    def _():
        ref[...] += val


def _rowwise(name, fn, rows, segs, vecs, row_outs, seg_accs, vec_accs, cfg, interleaved=(), row_tiles=None):
    tr, tpl, nb = cfg["tr"], cfg["tpl"], cfg["nb"]
    n_rows = rows[0].shape[0]
    nt = n_rows // tr
    nr, ns, nv = len(rows), len(segs), len(vecs)
    nro, nsa, nva = len(row_outs), len(seg_accs), len(vec_accs)

    def seg_of(t):
        return jnp.minimum(t // tpl, nb)

    def body(*refs):
        t = pl.program_id(0)
        ins, outs, mix_refs = refs[:nr + ns + nv], refs[nr + ns + nv:nr + ns + nv + nro + nsa + nva], refs[nr + ns + nv + nro + nsa + nva:]
        vals = [r[...] for r in ins[:nr]] + [r[0] for r in ins[nr:nr + ns]] + [r[...] for r in ins[nr + ns:]]
        res = fn(*vals)
        for idx, (o, v) in enumerate(zip(outs[:nro], res[:nro])):
            if idx in interleaved:
                mix_ref = mix_refs[interleaved.index(idx)]
                for p in range(mix_ref.shape[0]):
                    mix_ref[p] = v[:, p * LANES:(p + 1) * LANES].astype(F32)
                o[...] = jnp.concatenate([_interleave_rows(mix_ref.at[p], 1, o.dtype) for p in range(mix_ref.shape[0])], axis=1)
            else:
                o[...] = v.astype(o.dtype)
        first_seg = jnp.logical_or(t == 0, seg_of(t) != seg_of(jnp.maximum(t - 1, 0)))
        for o, v in zip(outs[nro:nro + nsa], res[nro:nro + nsa]):
            _accumulate(o.at[0], v, first_seg)
        for o, v in zip(outs[nro + nsa:], res[nro + nsa:]):
            _accumulate(o, v, t == 0)

    row_tiles = row_tiles or {}
    in_specs = ([pl.BlockSpec((tr, a.shape[1]), lambda t, pick=row_tiles.get(i, lambda t: _phys_tile(t, cfg)): (pick(t), 0))
                 for i, a in enumerate(rows)]
                + [pl.BlockSpec((1, 1, a.shape[2]), lambda t: (seg_of(t), 0, 0)) for a in segs]
                + [pl.BlockSpec((1, a.shape[1]), lambda t: (0, 0)) for a in vecs])
    out_specs = ([pl.BlockSpec((tr, c), lambda t: (_phys_tile(t, cfg), 0)) for c, _ in row_outs]
                 + [pl.BlockSpec((1, 1, c), lambda t: (seg_of(t), 0, 0)) for c in seg_accs]
                 + [pl.BlockSpec((1, c), lambda t: (0, 0)) for c in vec_accs])
    out_shape = ([jax.ShapeDtypeStruct((n_rows, c), dt) for c, dt in row_outs]
                 + [jax.ShapeDtypeStruct((nb + 1, 1, c), F32) for c in seg_accs]
                 + [jax.ShapeDtypeStruct((1, c), F32) for c in vec_accs])
    return pl.pallas_call(body, name=name, grid=(nt,), in_specs=in_specs, out_specs=out_specs, out_shape=out_shape,
                          scratch_shapes=[pltpu.VMEM((row_outs[idx][0] // LANES, tr, LANES), F32) for idx in interleaved],
                          compiler_params=_params(("arbitrary",)))(*rows, *segs, *vecs)


def _vjp_fn(fn, n_row, cot_groups, want):
    n_cot = sum(cot_groups)

    def bwd(*args):
        primals = [a.astype(F32) for a in args[:n_row] + args[n_row + n_cot:]]
        outs, vjp = jax.vjp(fn, *primals)
        cots, pos = [], n_row
        for n, o in zip(cot_groups, outs):
            cot = jnp.zeros_like(o)
            for part in args[pos:pos + n]:
                cot = cot + part.astype(F32)
            cots.append(cot)
            pos += n
        grads = vjp(tuple(cots))
        return tuple(grads[i] for i in want)
    return bwd


def _ln(r, g, b):
    mu = jnp.mean(r, axis=-1, keepdims=True)
    var = jnp.mean(jnp.square(r - mu), axis=-1, keepdims=True)
    return (r - mu) * lax.rsqrt(var + LN_EPS) * g + b


def _glu(zz, bias):
    d = zz.shape[1] // 2
    return (zz[:, :d] + bias[:, :d]) * jax.nn.sigmoid(zz[:, d:] + bias[:, d:])


def _f_entry(xc, pos, sh, sc):
    x0 = xc + pos
    return x0, x0 * (1 + sc) + sh


def _f_gelu(x, y0, y1, sh, sc, dsk):
    u = x * (1 + sc) + sh
    y = dsk * u + y0 + y1
    return (0.5 * y * (1.0 + lax.erf(y * (2.0 ** -0.5))),)


def _make_sub1_s5(alpha):
    def f(x, zz, g1, sh2, sc2, bglu, gain, bias):
        x1 = _ln(alpha * x + g1 * _glu(zz, bglu), gain, bias)
        return x1, x1 * (1 + sc2) + sh2
    return f


def _make_sub1_cv(alpha):
    def f(x, mm, g1, sh2, sc2, bpw2, gain, bias):
        x1 = _ln(alpha * x + g1 * (mm + bpw2), gain, bias)
        return x1, x1 * (1 + sc2) + sh2
    return f


def _make_sub2(alpha):
    def f(x1, m, g2, shn, scn, gain, bias):
        x2 = _ln(alpha * x1 + g2 * m, gain, bias)
        return x2, x2 * (1 + scn) + shn
    return f


def _f_cvglu(zz, bpw1):
    return (_glu(zz, bpw1),)


def _f_cvln(cv, bdw, lng, lnb):
    return (jax.nn.silu(_ln(cv + bdw, lng, lnb)),)


def _matmul(name, a, b, extras, grid, a_spec, b_spec, extra_specs, o_specs, out_shape, dims, red_axis, epi, sem, acc_shape=None):
    n_extra = len(extras)
    n_out = len(out_shape)
    if acc_shape is None:
        acc_shape = tuple(s for s in o_specs[0].block_shape if s is not None)

    def body(*refs):
        a_ref, b_ref = refs[0], refs[1]
        ex = refs[2:2 + n_extra]
        outs = refs[2 + n_extra:2 + n_extra + n_out]
        prod = lax.dot_general(a_ref[...], b_ref[...], dims, preferred_element_type=F32)

        def finish(acc):
            res = epi(acc, *[e[...] for e in ex]) if epi is not None else (acc,)
            for o, v in zip(outs, res):
                o[...] = v.astype(o.dtype)

        if red_axis is None:
            finish(prod)
        else:
            acc_ref = refs[-1]
            k = pl.program_id(red_axis)
            nk = pl.num_programs(red_axis)

            @pl.when(k == 0)
            def _():
                acc_ref[...] = prod

            @pl.when(k > 0)
            def _():
                acc_ref[...] += prod

            @pl.when(k == nk - 1)
            def _():
                finish(acc_ref[...])

    scratch = [] if red_axis is None else [pltpu.VMEM(acc_shape, F32)]
    res = pl.pallas_call(body, name=name, grid=grid, in_specs=[a_spec, b_spec] + list(extra_specs),
                         out_specs=list(o_specs), out_shape=list(out_shape), scratch_shapes=scratch,
                         compiler_params=_params(sem))(a, b, *extras)
    return res


NN = (((1,), (0,)), ((), ()))
NT = (((1,), (1,)), ((), ()))
TN = (((0,), (0,)), ((), ()))


def _mm_nn(name, a, w3, layer, tm, tn, out_dtypes=(F32,), epi=None):
    m, k = a.shape
    n = w3.shape[2]
    return _matmul(name, a, w3, (), (n // tn, m // tm),
                   pl.BlockSpec((tm, k), lambda j, i: (i, 0)), pl.BlockSpec((None, k, tn), lambda j, i: (layer, 0, j)), (),
                   [pl.BlockSpec((tm, tn), lambda j, i: (i, j)) for _ in out_dtypes],
                   [jax.ShapeDtypeStruct((m, n), dt) for dt in out_dtypes], NN, None, epi, ("arbitrary", "arbitrary"))


def _mm_nt(name, dy, w3, layer, tm, tkw, extras=(), out_dtype=F32, epi=None):
    m, n = dy.shape
    kw = w3.shape[1]
    return _matmul(name, dy, w3, tuple(extras), (kw // tkw, m // tm),
                   pl.BlockSpec((tm, n), lambda j, i: (i, 0)), pl.BlockSpec((None, tkw, n), lambda j, i: (layer, j, 0)),
                   [pl.BlockSpec((tm, tkw), lambda j, i: (i, j)) for _ in extras],
                   [pl.BlockSpec((tm, tkw), lambda j, i: (i, j))], [jax.ShapeDtypeStruct((m, kw), out_dtype)], NT, None, epi,
                   ("arbitrary", "arbitrary"))[0]


def _mm_wgrad_cols(name, a, dy, tm):
    m, k = a.shape
    n = dy.shape[1] // N_DEV
    per = max(1, min(N_DEV, 1024 // n))
    return _matmul(name, a, dy, (), (N_DEV // per, m // tm),
                   pl.BlockSpec((tm, k), lambda j, i: (i, 0)), pl.BlockSpec((tm, per * n), lambda j, i: (i, j)), (),
                   [pl.BlockSpec((per, k, n), lambda j, i: (j, 0, 0))], [jax.ShapeDtypeStruct((N_DEV, k, n), BF16)],
                   TN, 1, lambda acc: (jnp.stack([acc[:, q * n:(q + 1) * n] for q in range(per)]),), ("arbitrary", "arbitrary"),
                   (k, per * n))[0]


def _mm_wgrad_rows(name, a, dy, tm):
    m, k = a.shape
    r = k // N_DEV
    n = dy.shape[1]
    rows = r * max(1, min(N_DEV, 1024 // r))
    out = _matmul(name, a, dy, (), (k // rows, m // tm),
                  pl.BlockSpec((tm, rows), lambda j, i: (i, j)), pl.BlockSpec((tm, n), lambda j, i: (i, 0)), (),
                  [pl.BlockSpec((rows, n), lambda j, i: (j, 0))], [jax.ShapeDtypeStruct((k, n), BF16)],
                  TN, 1, None, ("arbitrary", "arbitrary"))[0]
    return out.reshape(N_DEV, r, n)


class _Copies:
    def __init__(self, arrays, kinds):
        self.arrays, self.kinds, self.n = list(arrays), list(kinds), len(arrays)
        any_spec = pl.BlockSpec(memory_space=pl.ANY)
        self.in_specs = [any_spec] * self.n
        self.out_specs = [any_spec] * self.n
        self.out_shape = [jax.ShapeDtypeStruct(self._result(a, kind), a.dtype) for a, kind in zip(arrays, kinds)]
        self.scratch = [pltpu.SemaphoreType.DMA((self.n, N_DEV - 1)), pltpu.SemaphoreType.DMA((self.n, N_DEV - 1)),
                        pltpu.SemaphoreType.DMA((self.n,))] if self.n else []

    @staticmethod
    def _result(a, kind):
        if kind == "slot":
            return (N_DEV,) + a.shape
        if kind == "scatter":
            return a.shape
        return a.shape[:kind] + (N_DEV * a.shape[kind],) + a.shape[kind + 1:]

    def descriptors(self, ins, outs, sems):
        send_sems, recv_sems, local_sems = sems
        x, y, c = lax.axis_index("x"), lax.axis_index("y"), lax.axis_index("c")
        me = 4 * x + 2 * y + c
        first, relay, finish = [], [], []

        def remote(i, k, src, dst, to):
            return pltpu.make_async_remote_copy(src_ref=src, dst_ref=dst, send_sem=send_sems.at[i, k], recv_sem=recv_sems.at[i, k],
                                                device_id=to, device_id_type=MESH)

        for i, kind in enumerate(self.kinds):
            if kind in ("slot", "scatter"):
                scatter = kind == "scatter"
                local = pltpu.make_async_copy(ins[i].at[me] if scatter else ins[i], outs[i].at[me], local_sems.at[i])
                first.append(local)
                finish.append((local, "all"))
                for k in range(1, N_DEV):
                    px = 1 - x if k & 4 else x
                    py = 1 - y if k & 2 else y
                    pc = 1 - c if k & 1 else c
                    cp = remote(i, k - 1, ins[i].at[4 * px + 2 * py + pc] if scatter else ins[i], outs[i].at[me], (px, py, pc))
                    first.append(cp)
                    finish.append((cp, "all"))
                continue
            size = ins[i].shape[kind]

            def block(px, py, pc):
                return outs[i].at[(slice(None),) * kind + (pl.ds(pl.multiple_of((4 * px + 2 * py + pc) * size, size), size),)]

            local = pltpu.make_async_copy(ins[i], block(x, y, c), local_sems.at[i])
            sibling = remote(i, 0, ins[i], block(x, y, c), (x, y, 1 - c))
            first += [local, sibling]
            finish += [(local, "all"), (sibling, "all")]
            for j, (qx, qy) in enumerate([(1 - x, y), (x, 1 - y), (1 - x, 1 - y)]):
                out = remote(i, 1 + j, ins[i], block(x, y, c), (qx, qy, c))
                onward = remote(i, 4 + j, block(qx, qy, c), block(qx, qy, c), (x, y, 1 - c))
                first.append(out)
                relay.append((out, onward))
                finish += [(out, "send"), (onward, "all")]
        return first, relay, finish


def _guarded(when, fn):
    if when is None:
        fn()
    else:
        pl.when(when)(fn)


def _start_all(plan, when=None):
    def run():
        for cp in plan[0]:
            cp.start()
    _guarded(when, run)


def _relay_all(plan, when=None):
    def run():
        for arrived, onward in plan[1]:
            arrived.wait_recv()
            onward.start()
    if plan[1]:
        _guarded(when, run)


def _wait_all(plan, when=None):
    def run():
        for cp, left in plan[2]:
            if left == "send":
                cp.wait_send()
            else:
                cp.wait()
    _guarded(when, run)


def _exchange(name, arrays, kinds):
    cps = _Copies(arrays, kinds)
    n = cps.n

    def body(*refs):
        plan = cps.descriptors(refs[:n], refs[n:2 * n], refs[2 * n:])
        _start_all(plan)
        _relay_all(plan)
        _wait_all(plan)

    return pl.pallas_call(body, name=name, in_specs=cps.in_specs, out_specs=cps.out_specs, out_shape=cps.out_shape,
                          scratch_shapes=cps.scratch)(*arrays)


def _carried(cps, n_in, n_out, n_scratch, refs):
    if cps is None:
        return ([], [], []), refs
    n = cps.n
    ins = refs[n_in:n_in + n]
    outs = refs[n_in + n + n_out:n_in + n + n_out + n]
    sems = refs[n_in + n + n_out + n + n_scratch:]
    own = refs[:n_in] + refs[n_in + n:n_in + n + n_out] + refs[n_in + n + n_out + n:n_in + n + n_out + n + n_scratch]
    return cps.descriptors(ins, outs, sems), own


def _sum_lead(name, parts, tr):
    npart, r, c = parts.shape

    def body(p_ref, o_ref):
        acc = p_ref[0].astype(F32)
        for p in range(1, npart):
            acc = acc + p_ref[p].astype(F32)
        o_ref[...] = acc

    return pl.pallas_call(body, name=name, grid=(r // tr,), in_specs=[pl.BlockSpec((npart, tr, c), lambda i: (0, i, 0))],
                          out_specs=pl.BlockSpec((tr, c), lambda i: (i, 0)), out_shape=jax.ShapeDtypeStruct((r, c), F32),
                          compiler_params=_params(("arbitrary",)))(parts)


def _adamw_math(g, w, m, v):
    m2 = ADAM_B1 * m + (1.0 - ADAM_B1) * g
    v2 = ADAM_B2 * v + (1.0 - ADAM_B2) * jnp.square(g)
    m_hat = m2 / (1.0 - ADAM_B1 ** ADAM_STEP)
    v_hat = v2 / (1.0 - ADAM_B2 ** ADAM_STEP)
    return -ADAM_LR * (m_hat / (jnp.sqrt(v_hat) + ADAM_EPS) + ADAM_WD * w), m2, v2


def _adamw_body(npart):
    def body(p_ref, w_ref, m_ref, v_ref, *rest):
        g_out, d_out, m_out, v_out = rest[-4:]
        g = p_ref[0].astype(F32)
        for p in range(1, npart):
            g = g + p_ref[p].astype(F32)
        g_out[...] = g
        d_out[...], m_out[...], v_out[...] = _adamw_math(g, w_ref[...], m_ref[...], v_ref[...])
    return body


def _adamw_native(name, g, w, m, v):
    rest = w.shape[2:]
    spec = pl.BlockSpec((None, None) + rest, lambda i, j: (i, j) + (0,) * len(rest))

    def body(g_ref, w_ref, m_ref, v_ref, d_out, m_out, v_out):
        d_out[...], m_out[...], v_out[...] = _adamw_math(g_ref[...], w_ref[...], m_ref[...], v_ref[...])

    return pl.pallas_call(body, name=name, grid=w.shape[:2], in_specs=[spec] * 4, out_specs=[spec] * 3,
                          out_shape=[jax.ShapeDtypeStruct(w.shape, F32)] * 3,
                          compiler_params=_params(("arbitrary", "arbitrary")))(g, w, m, v)


def _adamw(name, parts, w, m, v, tr):
    npart, r, c = parts.shape
    row = pl.BlockSpec((tr, c), lambda i: (i, 0))
    return pl.pallas_call(_adamw_body(npart), name=name, grid=(r // tr,),
                          in_specs=[pl.BlockSpec((npart, tr, c), lambda i: (0, i, 0)), row, row, row],
                          out_specs=[row] * 4, out_shape=[jax.ShapeDtypeStruct((r, c), F32)] * 4,
                          compiler_params=_params(("arbitrary",)))(parts, w, m, v)


def _adamw_layer(name, parts, w3, m3, v3, bufs, layer, tr):
    npart, r, c = parts.shape
    lay = pl.BlockSpec((None, tr, c), lambda i: (layer, i, 0))
    hbm = pl.BlockSpec(memory_space=pl.ANY)
    return pl.pallas_call(_adamw_body(npart), name=name, grid=(r // tr,),
                          in_specs=[pl.BlockSpec((npart, tr, c), lambda i: (0, i, 0)), lay, lay, lay] + [hbm] * 4,
                          out_specs=[lay] * 4, out_shape=[jax.ShapeDtypeStruct(w3.shape, F32)] * 4,
                          input_output_aliases={4: 0, 5: 1, 6: 2, 7: 3},
                          compiler_params=_params(("arbitrary",)))(parts, w3, m3, v3, *bufs)


def _row_tile(r, cap):
    if r <= cap:
        return r
    t = cap - cap % SUBLANES
    while r % t:
        t -= SUBLANES
    return t


def _ada_fwd(cc, w_ada, b_loc):
    nl, d, n = w_ada.shape
    rows = cc.shape[0]

    def body(c_ref, w_ref, b_ref, o_ref):
        cond = jax.nn.silu(c_ref[...]).astype(BF16)
        o_ref[...] = jnp.dot(cond, w_ref[...].astype(BF16), preferred_element_type=F32) + b_ref[...]

    return pl.pallas_call(body, name="ada_fwd", grid=(nl,),
                          in_specs=[pl.BlockSpec((rows, d), lambda i: (0, 0)), pl.BlockSpec((None, d, n), lambda i: (i, 0, 0)),
                                    pl.BlockSpec((None, 1, n), lambda i: (i, 0, 0))],
                          out_specs=pl.BlockSpec((None, rows, n), lambda i: (i, 0, 0)),
                          out_shape=jax.ShapeDtypeStruct((nl, rows, n), F32), compiler_params=_params(("arbitrary",)))(cc, w_ada, b_loc)


def _ada_bwd(cc, w_ada, dmod_rows, dmod_ctx):
    nl, d, n = w_ada.shape
    rows = cc.shape[0]
    ctx_row = rows - SUBLANES

    def body(c_ref, w_ref, dr_ref, dc_ref, gw_ref, tot_ref, dcond_ref):
        i = pl.program_id(0)
        total = dc_ref[0]
        for p in range(1, N_DEV):
            total = total + dc_ref[p]
        tot_ref[...] = total
        row_id = lax.broadcasted_iota(jnp.int32, (rows, n), 0)
        dm = jnp.where(row_id == ctx_row, jnp.broadcast_to(total, (rows, n)), dr_ref[...]).astype(BF16)
        cond = jax.nn.silu(c_ref[...]).astype(BF16)
        gw_ref[...] = lax.dot_general(cond, dm, TN, preferred_element_type=F32)
        part = lax.dot_general(dm, w_ref[...].astype(BF16), NT, preferred_element_type=F32)
        _accumulate(dcond_ref, part, i == 0)

    return pl.pallas_call(body, name="ada_bwd", grid=(nl,),
                          in_specs=[pl.BlockSpec((rows, d), lambda i: (0, 0)), pl.BlockSpec((None, d, n), lambda i: (i, 0, 0)),
                                    pl.BlockSpec((None, rows, n), lambda i: (i, 0, 0)),
                                    pl.BlockSpec((N_DEV, None, 1, n), lambda i: (0, i, 0, 0))],
                          out_specs=[pl.BlockSpec((None, d, n), lambda i: (i, 0, 0)), pl.BlockSpec((None, 1, n), lambda i: (i, 0, 0)),
                                     pl.BlockSpec((rows, d), lambda i: (0, 0))],
                          out_shape=[jax.ShapeDtypeStruct((nl, d, n), F32), jax.ShapeDtypeStruct((nl, 1, n), F32),
                                     jax.ShapeDtypeStruct((rows, d), F32)],
                          compiler_params=_params(("arbitrary",)))(cc, w_ada, dmod_rows, dmod_ctx)


def _cctx_grad(parts, c_ctx):
    def body(p_ref, c_ref, o_ref):
        tot = p_ref[0]
        for p in range(1, N_DEV):
            tot = tot + p_ref[p]
        _, vjp = jax.vjp(jax.nn.silu, c_ref[...])
        o_ref[...] = vjp(tot)[0]

    return pl.pallas_call(body, name="cctx_grad", out_shape=jax.ShapeDtypeStruct(c_ctx.shape, F32))(parts, c_ctx)


def _discretise(lam_re, lam_im, log_dt, b_re, b_im):
    lr = jnp.minimum(lam_re, LAMBDA_RE_MAX)
    li = lam_im
    dt = jnp.exp(log_dt)
    mag = jnp.exp(lr * dt)
    ab_re = mag * jnp.cos(li * dt)
    ab_im = mag * jnp.sin(li * dt)
    den = lr * lr + li * li
    nr = ab_re - 1.0
    ni = ab_im
    coef_re = ((nr * lr + ni * li) / den)[:, None]
    coef_im = ((ni * lr - nr * li) / den)[:, None]
    bb_re = coef_re * b_re - coef_im * b_im
    bb_im = coef_re * b_im + coef_im * b_re
    return ab_re, ab_im, bb_re, bb_im


def _s5_prep(name, lam_re, lam_im, log_dt, b_re, b_im):
    def body(a, b, c, d, e, o1, o2, o3, o4):
        res = _discretise(a[...], b[...], c[...], d[...], e[...])
        for o, v in zip((o1, o2, o3, o4), res):
            o[...] = v

    shp = [jax.ShapeDtypeStruct(lam_re.shape, F32)] * 2 + [jax.ShapeDtypeStruct(b_re.shape, F32)] * 2
    return pl.pallas_call(body, name=name, out_shape=shp)(lam_re, lam_im, log_dt, b_re, b_im)


def _s5_prep_bwd(name, lam_re, lam_im, log_dt, b_re, b_im, cots):
    def body(a, b, c, d, e, c1, c2, c3, c4, o1, o2, o3, o4, o5):
        _, vjp = jax.vjp(_discretise, a[...], b[...], c[...], d[...], e[...])
        grads = vjp((c1[...], c2[...], c3[...], c4[...]))
        for o, v in zip((o1, o2, o3, o4, o5), grads):
            o[...] = v

    shp = [jax.ShapeDtypeStruct(a.shape, F32) for a in (lam_re, lam_im, log_dt, b_re, b_im)]
    return pl.pallas_call(body, name=name, out_shape=shp)(lam_re, lam_im, log_dt, b_re, b_im, *cots)


def _interleave_rows(ref, n_seq, dtype):
    n_j = ref.shape[0] // (SUBLANES * n_seq)
    return jnp.concatenate([ref[pl.ds(q * SUBLANES * n_j + j, SUBLANES, stride=n_j), :] for q in range(n_seq) for j in range(n_j)],
                           axis=0).astype(dtype)


def _store_tokens(out_ref, ref, n_seq):
    n_j = ref.shape[0] // (SUBLANES * n_seq)
    for q in range(n_seq):
        for s in range(SUBLANES):
            start = (q * SUBLANES + s) * n_j
            out_ref[start:start + n_j, :] = ref[pl.ds(q * SUBLANES * n_j + s, n_j, stride=SUBLANES), :].astype(out_ref.dtype)


def _expand_powers(t_ref, pow_ref):
    for j in range(pow_ref.shape[0] // SUBLANES):
        row = 5 * SUBLANES + j
        pow_ref[j * SUBLANES:(j + 1) * SUBLANES, :] = jnp.broadcast_to(t_ref[row:row + 1, :], (SUBLANES, pow_ref.shape[1]))


def _scan_tile(h_ref, t_ref, pow_ref, carry_ref, up, n_seq, states_ref=None):
    sw = h_ref.shape[1] // 2
    n_j = h_ref.shape[0] // (SUBLANES * n_seq)
    seqs = range(n_seq)

    def rows(g):
        if isinstance(g, int):
            return pl.ds(g * SUBLANES, SUBLANES)
        return pl.ds(pl.multiple_of(g * SUBLANES, SUBLANES), SUBLANES)

    def at(q, j):
        return rows(q * n_j + j)

    def tab(g):
        return t_ref[rows(g), :sw], t_ref[rows(g), sw:]

    def order(i):
        return n_j - 1 - i if up else i

    def cmul_add(xr, xi, ar, ai, yr, yi):
        return xr + ar * yr - ai * yi, xi + ar * yi + ai * yr

    a_re, a_im = tab(0)

    def local_step(i, xs):
        j = order(i)
        out = []
        for q in seqs:
            xr, xi = cmul_add(h_ref[at(q, j), :sw], h_ref[at(q, j), sw:], a_re, a_im, *xs[q])
            h_ref[at(q, j), :sw] = xr
            h_ref[at(q, j), sw:] = xi
            out.append((xr, xi))
        return tuple(out)

    zero = jnp.zeros((SUBLANES, sw), F32)
    ends = lax.fori_loop(0, n_j, local_step, tuple((zero, zero) for _ in seqs))
    out_row = 0 if up else SUBLANES - 1
    in_row = SUBLANES - 1 if up else 0
    one = SUBLANES - 1 if up else 1
    is_in = lax.broadcasted_iota(jnp.int32, (SUBLANES, sw), 0) == in_row
    carried, enters = [], []
    for q in seqs:
        dr, di = ends[q]
        for level, sh in enumerate((1, 2, 4)):
            amount = SUBLANES - sh if up else sh
            dr, di = cmul_add(dr, di, *tab(1 + level), pltpu.roll(dr, amount, 0), pltpu.roll(di, amount, 0))
        cr, ci = carry_ref[rows(q), :sw], carry_ref[rows(q), sw:]
        dr, di = cmul_add(dr, di, *tab(4), cr, ci)
        carry_ref[rows(q), :sw] = jnp.broadcast_to(dr[out_row:out_row + 1], dr.shape)
        carry_ref[rows(q), sw:] = jnp.broadcast_to(di[out_row:out_row + 1], di.shape)
        carried.append((cr, ci))
        enters.append((jnp.where(is_in, cr, pltpu.roll(dr, one, 0)), jnp.where(is_in, ci, pltpu.roll(di, one, 0))))

    def fix_step(i, state):
        j = order(i)
        nows = []
        for q in seqs:
            xr, xi = cmul_add(h_ref[at(q, j), :sw], h_ref[at(q, j), sw:], pow_ref[rows(j), :sw], pow_ref[rows(j), sw:], *enters[q])
            h_ref[at(q, j), :sw] = xr
            h_ref[at(q, j), sw:] = xi
            nows.append((xr, xi))
        if states_ref is None:
            return state
        befores, (acc_r, acc_i) = state
        for q in seqs:
            lr, li = befores[q]
            hr, hi = states_ref[at(q, j), :sw], states_ref[at(q, j), sw:]
            acc_r, acc_i = acc_r + lr * hr + li * hi, acc_i + li * hr - lr * hi
        return tuple(nows), (acc_r, acc_i)

    if states_ref is None:
        lax.fori_loop(0, n_j, fix_step, 0)
        return None
    lasts, (acc_r, acc_i) = lax.fori_loop(0, n_j, fix_step, (tuple((zero, zero) for _ in seqs), (zero, zero)))
    for q in seqs:
        lr = jnp.where(is_in, carried[q][0], pltpu.roll(lasts[q][0], one, 0))
        li = jnp.where(is_in, carried[q][1], pltpu.roll(lasts[q][1], one, 0))
        hr, hi = states_ref[at(q, order(0)), :sw], states_ref[at(q, order(0)), sw:]
        acc_r, acc_i = acc_r + lr * hr + li * hi, acc_i + li * hr - lr * hi
    return acc_r, acc_i


def _phys_tile(t, cfg):
    tpl, nb = cfg["tpl"], cfg["nb"]
    return jnp.where(t < nb * tpl, (t % tpl) * nb + t // tpl, t)


def _s5_block_index(cfg, dirn, adjoint):
    tpl = cfg["tpl"]

    def idx(k):
        if not adjoint:
            return jnp.where(k == 0, tpl, k - 1 if dirn == 0 else tpl - k)
        return jnp.where(k == tpl, tpl, tpl - 1 - k if dirn == 0 else k)
    return idx


RELAY_AT = 0.75


def _grid_ends(grid):
    step = 0
    for i, n in enumerate(grid):
        step = step * n + pl.program_id(i)
    total = math.prod(grid)
    return step == 0, step == int(RELAY_AT * total), step == total - 1


def _s5_fwd(name, u, bmat, cmat, tab, dirn, cfg, cps=None):
    tr, tpl, nb = cfg["tr"], cfg["tpl"], cfg["nb"]
    n_rows, d = u.shape
    ns, _, sw2 = bmat.shape
    block = _s5_block_index(cfg, dirn, False)
    up = dirn == 1
    grid = (ns, tpl + 1)
    br = nb * tr

    def body(*refs):
        copies, (u_ref, b_ref, c_ref, t_ref, h_ref, y_ref, carry_ref, mix_ref, pow_ref) = _carried(cps, 4, 2, 3, refs)
        first, middle, last = _grid_ends(grid)
        _start_all(copies, first)
        _relay_all(copies, middle)

        @pl.when(pl.program_id(1) == 0)
        def _():
            carry_ref[...] = jnp.zeros_like(carry_ref)
            _expand_powers(t_ref, pow_ref)

        h_ref[...] = jnp.dot(u_ref[...], b_ref[...], preferred_element_type=F32)
        _scan_tile(h_ref, t_ref, pow_ref, carry_ref, up, nb)
        mix_ref[...] = jnp.dot(h_ref[...].astype(BF16), c_ref[...], preferred_element_type=F32)
        _store_tokens(y_ref, mix_ref, nb)
        _wait_all(copies, last)

    extra = cps if cps is not None else _Copies([], [])
    return pl.pallas_call(
        body, name=name, grid=grid,
        in_specs=[pl.BlockSpec((br, LANES), lambda s, k: (block(k), s)),
                  pl.BlockSpec((None, LANES, sw2), lambda s, k: (s, 0, 0)),
                  pl.BlockSpec((None, sw2, LANES), lambda s, k: (s, 0, 0)),
                  pl.BlockSpec((None, tab.shape[1], sw2), lambda s, k: (s, 0, 0))] + extra.in_specs,
        out_specs=[pl.BlockSpec((br, sw2), lambda s, k: (block(k), s)),
                   pl.BlockSpec((br, LANES), lambda s, k: (block(k), s))] + extra.out_specs,
        out_shape=[jax.ShapeDtypeStruct((n_rows, ns * sw2), F32), jax.ShapeDtypeStruct((n_rows, d), F32)] + extra.out_shape,
        scratch_shapes=[pltpu.VMEM((nb * SUBLANES, sw2), F32), pltpu.VMEM((br, LANES), F32), pltpu.VMEM((tr, sw2), F32)] + extra.scratch,
        compiler_params=_params(("arbitrary", "arbitrary")))(u, bmat, cmat, tab, *extra.arrays)


def _s5_bwd(name, dy, h, u, cmat_t, bmat_t, tab, dirn, cfg, cps=None):
    tr, tpl, nb = cfg["tr"], cfg["tpl"], cfg["nb"]
    n_rows, d = u.shape
    ns, _, sw2 = cmat_t.shape
    sw = sw2 // 2
    block = _s5_block_index(cfg, dirn, True)
    up = dirn == 0
    grid = (ns, tpl + 1)
    br = nb * tr

    def body(*refs):
        copies, own = _carried(cps, 6, 4, 4, refs)
        dy_ref, h_ref, u_ref, ct_ref, bt_ref, t_ref, du_ref, db_ref, dc_ref, da_ref, lam_ref, carry_ref, mix_ref, pow_ref = own
        grid_first, grid_middle, grid_last = _grid_ends(grid)
        _start_all(copies, grid_first)
        _relay_all(copies, grid_middle)
        first = pl.program_id(1) == 0

        @pl.when(first)
        def _():
            carry_ref[...] = jnp.zeros_like(carry_ref)
            _expand_powers(t_ref, pow_ref)

        dy = dy_ref[...]
        u_mixed = u_ref[...]
        lam_ref[...] = jnp.dot(dy, ct_ref[...], preferred_element_type=F32)
        acc = _scan_tile(lam_ref, t_ref, pow_ref, carry_ref, up, nb, h_ref)
        lam = lam_ref[...].astype(BF16)
        d_b = lax.dot_general(u_mixed, lam, TN, preferred_element_type=F32)
        d_c = lax.dot_general(h_ref[...].astype(BF16), dy, TN, preferred_element_type=F32)
        mix_ref[...] = jnp.dot(lam, bt_ref[...], preferred_element_type=F32)
        _store_tokens(du_ref, mix_ref, nb)

        @pl.when(first)
        def _():
            db_ref[...] = d_b
            dc_ref[...] = d_c
            da_ref[:, :sw] = acc[0]
            da_ref[:, sw:] = acc[1]

        @pl.when(jnp.logical_not(first))
        def _():
            db_ref[...] += d_b
            dc_ref[...] += d_c
            da_ref[:, :sw] += acc[0]
            da_ref[:, sw:] += acc[1]

        _wait_all(copies, grid_last)

    extra = cps if cps is not None else _Copies([], [])
    return pl.pallas_call(
        body, name=name, grid=grid,
        in_specs=[pl.BlockSpec((br, LANES), lambda s, k: (block(k), s)),
                  pl.BlockSpec((br, sw2), lambda s, k: (block(k), s)),
                  pl.BlockSpec((br, LANES), lambda s, k: (block(k), s)),
                  pl.BlockSpec((None, LANES, sw2), lambda s, k: (s, 0, 0)),
                  pl.BlockSpec((None, sw2, LANES), lambda s, k: (s, 0, 0)),
                  pl.BlockSpec((None, tab.shape[1], sw2), lambda s, k: (s, 0, 0))] + extra.in_specs,
        out_specs=[pl.BlockSpec((br, LANES), lambda s, k: (block(k), s)),
                   pl.BlockSpec((None, LANES, sw2), lambda s, k: (s, 0, 0)),
                   pl.BlockSpec((None, sw2, LANES), lambda s, k: (s, 0, 0)),
                   pl.BlockSpec((None, SUBLANES, sw2), lambda s, k: (s, 0, 0))] + extra.out_specs,
        out_shape=[jax.ShapeDtypeStruct((n_rows, d), F32), jax.ShapeDtypeStruct((ns, LANES, sw2), F32),
                   jax.ShapeDtypeStruct((ns, sw2, LANES), F32), jax.ShapeDtypeStruct((ns, SUBLANES, sw2), F32)] + extra.out_shape,
        scratch_shapes=[pltpu.VMEM((br, sw2), F32), pltpu.VMEM((nb * SUBLANES, sw2), F32), pltpu.VMEM((br, LANES), F32),
                        pltpu.VMEM((tr, sw2), F32)] + extra.scratch,
        compiler_params=_params(("arbitrary", "arbitrary")))(dy, h, u, cmat_t, bmat_t, tab, *extra.arrays)


def _s5_tables(ab_re, ab_im, up, conj, ns, n_j):
    def powers_of(base, count):
        out = [base]
        for _ in range(count - 1):
            q_re, q_im = out[-1]
            out.append((q_re * base[0] - q_im * base[1], q_re * base[1] + q_im * base[0]))
        return out

    def spread(q):
        return jnp.broadcast_to(q[:, None, :], (q.shape[0], SUBLANES, q.shape[1]))

    steps = powers_of((ab_re.reshape(ns, -1), (-ab_im if conj else ab_im).reshape(ns, -1)), n_j)
    jumps = powers_of(steps[-1], SUBLANES)
    rows = jnp.arange(SUBLANES)
    blocks = [tuple(spread(q) for q in steps[0])]
    for sh in (1, 2, 4):
        keep = ((rows <= SUBLANES - 1 - sh) if up else (rows >= sh))[None, :, None]
        blocks.append(tuple(jnp.where(keep, q[:, None, :], 0.0) for q in jumps[sh - 1]))
    dist = range(SUBLANES, 0, -1) if up else range(1, SUBLANES + 1)
    blocks.append(tuple(jnp.stack([jumps[dd - 1][part] for dd in dist], axis=1) for part in (0, 1)))
    ordered = steps[::-1] if up else steps
    blocks.append(tuple(jnp.stack([q[part] for q in ordered], axis=1) for part in (0, 1)))
    return jnp.concatenate([jnp.concatenate([b[0] for b in blocks], axis=1), jnp.concatenate([b[1] for b in blocks], axis=1)], axis=2)


def _block_diag(blocks):
    ns, gs, a, b = blocks.shape
    eye = jnp.eye(gs, dtype=blocks.dtype)
    return (blocks[:, :, :, None, :] * eye[None, :, None, :, None]).reshape(ns, gs * a, gs * b)


def _diag_blocks(mat, gs):
    ns, ra, rb = mat.shape
    a, b = ra // gs, rb // gs
    m5 = mat.reshape(ns, gs, a, gs, b)
    eye = jnp.eye(gs, dtype=mat.dtype)
    return jnp.sum(m5 * eye[None, :, None, :, None], axis=3)


def _conv_flags(t, cfg):
    tpl, nb = cfg["tpl"], cfg["nb"]
    latent = t < nb * tpl
    first = jnp.logical_or(jnp.logical_not(latent), t % tpl == 0)
    last = jnp.logical_or(jnp.logical_not(latent), t % tpl == tpl - 1)
    return first, last


def _fill_ext(ext_ref, prev_ref, cur_ref, next_ref, t, cfg, halo):
    first, last = _conv_flags(t, cfg)
    tr = cur_ref.shape[0]
    for p in range(ext_ref.shape[0]):
        lanes = slice(p * LANES, (p + 1) * LANES)
        ext_ref[p, 0:halo, :] = jnp.where(first, 0.0, prev_ref[:, lanes])
        ext_ref[p, halo:halo + tr, :] = cur_ref[:, lanes]
        ext_ref[p, halo + tr:, :] = jnp.where(last, 0.0, next_ref[:, lanes])


CONV_LANES = 4 * LANES


def _conv_specs(tr, n_rows, halo, cw, cfg):
    per = tr // halo
    n_halo = n_rows // halo
    nb = cfg["nb"]
    return [pl.BlockSpec((halo, cw), lambda c, t: (jnp.maximum((_phys_tile(t, cfg) - nb + 1) * per - 1, 0), c)),
            pl.BlockSpec((tr, cw), lambda c, t: (_phys_tile(t, cfg), c)),
            pl.BlockSpec((halo, cw), lambda c, t: (jnp.minimum((_phys_tile(t, cfg) + nb) * per, n_halo - 1), c))]


def _dwconv(name, a, w, cfg):
    tr = cfg["tr"]
    n_rows, d = a.shape
    kw = w.shape[0]
    half = kw // 2
    halo = 2 * SUBLANES
    cw = min(d, CONV_LANES)

    def body(prev_ref, cur_ref, next_ref, w_ref, o_ref, ext_ref):
        _fill_ext(ext_ref, prev_ref, cur_ref, next_ref, pl.program_id(1), cfg, halo)
        for p in range(cw // LANES):
            lanes = slice(p * LANES, (p + 1) * LANES)
            acc = jnp.zeros((tr, LANES), F32)
            for k in range(kw):
                acc = acc + ext_ref[p, pl.ds(halo - half + k, tr), :] * w_ref[k:k + 1, lanes]
            o_ref[:, lanes] = acc

    return pl.pallas_call(body, name=name, grid=(d // cw, n_rows // tr),
                          in_specs=_conv_specs(tr, n_rows, halo, cw, cfg) + [pl.BlockSpec((kw, cw), lambda c, t: (0, c))],
                          out_specs=pl.BlockSpec((tr, cw), lambda c, t: (_phys_tile(t, cfg), c)),
                          out_shape=jax.ShapeDtypeStruct((n_rows, d), F32),
                          scratch_shapes=[pltpu.VMEM((cw // LANES, tr + 2 * halo, LANES), F32)],
                          compiler_params=_params(("arbitrary", "arbitrary")))(a, a, a, w)


def _dwconv_wgrad(name, a, dout, kw, cfg):
    tr = cfg["tr"]
    n_rows, d = a.shape
    half = kw // 2
    halo = 2 * SUBLANES
    cw = min(d, CONV_LANES)

    def body(prev_ref, cur_ref, next_ref, do_ref, o_ref, ext_ref):
        t = pl.program_id(1)
        _fill_ext(ext_ref, prev_ref, cur_ref, next_ref, t, cfg, halo)
        for p in range(cw // LANES):
            lanes = slice(p * LANES, (p + 1) * LANES)
            dout_t = do_ref[:, lanes]
            rows = [jnp.sum(ext_ref[p, pl.ds(halo - half + k, tr), :] * dout_t, axis=0, keepdims=True) for k in range(kw)]
            _accumulate(o_ref.at[:, lanes], jnp.concatenate(rows, axis=0), t == 0)

    return pl.pallas_call(body, name=name, grid=(d // cw, n_rows // tr),
                          in_specs=_conv_specs(tr, n_rows, halo, cw, cfg) + [pl.BlockSpec((tr, cw), lambda c, t: (_phys_tile(t, cfg), c))],
                          out_specs=pl.BlockSpec((kw, cw), lambda c, t: (0, c)),
                          out_shape=jax.ShapeDtypeStruct((kw, d), F32),
                          scratch_shapes=[pltpu.VMEM((cw // LANES, tr + 2 * halo, LANES), F32)],
                          compiler_params=_params(("arbitrary", "arbitrary")))(a, a, a, dout)


def _sincos_1d(pos, dim):
    quarter = dim // 2
    omega = POS_TEMP ** (-jnp.arange(quarter, dtype=F32) / quarter)
    ang = pos[:, None] * omega[None, :]
    return jnp.concatenate([jnp.sin(ang), jnp.cos(ang)], axis=-1)


def _grid_pos_embed(rows, dim):
    row_idx = jnp.repeat(jnp.arange(rows), GRID_W).astype(F32)
    col_idx = jnp.tile(jnp.arange(GRID_W), rows).astype(F32)
    return jnp.concatenate([_sincos_1d(row_idx, dim // 2), _sincos_1d(col_idx, dim // 2)], axis=-1)


def _pack(arrs, row_multiple=SUBLANES):
    flat = jnp.concatenate([a.reshape(-1).astype(F32) for a in arrs])
    pad = (-flat.shape[0]) % (row_multiple * LANES)
    return jnp.pad(flat, (0, pad)).reshape(-1, LANES)


def _unpack(buf, shapes):
    flat = buf.reshape(-1)
    out, pos = [], 0
    for shp in shapes:
        n = math.prod(shp)
        out.append(flat[pos:pos + n].reshape(shp))
        pos += n
    return out


def _unpack_gathered(buf, shapes):
    flat = buf.reshape(N_DEV, -1)
    out, pos = [], 0
    for shp in shapes:
        n = math.prod(shp)
        part = flat[:, pos:pos + n].reshape((N_DEV,) + tuple(shp))
        out.append(jnp.moveaxis(part, 0, -2).reshape(tuple(shp[:-1]) + (N_DEV * shp[-1],)))
        pos += n
    return out


WEIGHTS = ("c_ctx", "w_ada", "b_ada", "ln_gain", "ln_bias", "s5_lam_re", "s5_lam_im", "s5_log_dt", "s5_b_re", "s5_b_im",
           "s5_c_re", "s5_c_im", "s5_d", "s5_w_glu", "s5_b_glu", "cv_w_pw1", "cv_b_pw1", "cv_w_dw", "cv_b_dw", "cv_ln_g",
           "cv_ln_b", "cv_w_pw2", "cv_b_pw2", "mlp_w1", "mlp_w2")
SHARDED_SMALL = ("ln_gain", "ln_bias", "cv_b_pw1", "cv_w_dw", "cv_b_dw", "cv_ln_g", "cv_ln_b", "cv_b_pw2")
REPLICATED_SMALL = ("s5_lam_re", "s5_lam_im", "s5_log_dt", "s5_b_re", "s5_b_im", "s5_c_re", "s5_c_im", "s5_d", "s5_b_glu")
NATIVE_SMALL = ("s5_lam_re", "s5_lam_im", "s5_b_re", "s5_b_im", "s5_c_re", "s5_c_im")
BIG = ("mlp_w1", "mlp_w2", "s5_w_glu", "cv_w_pw1", "cv_w_pw2")


def _step(a):
    x, c, ctx = a["x"], a["c"], a["ctx"]
    nb, seq, d = x.shape
    lc = ctx.shape[1]
    nl = a["w_ada"].shape[0]
    tr = lc
    tpl = seq // tr
    cfg = {"tr": tr, "tpl": tpl, "nb": nb}
    n_rows = nb * (seq + lc)
    alpha = (2.0 * nl) ** 0.25
    me = 4 * lax.axis_index("x") + 2 * lax.axis_index("y") + lax.axis_index("c")
    n_grp, n_state = a["s5_lam_re"].shape[2:]
    ch = a["s5_b_re"].shape[-1]
    gs = LANES // ch
    ns = d // LANES
    tm = 2 * tr if n_rows % (2 * tr) == 0 else tr
    tm_big = n_rows // 3 if n_rows % (3 * 2 * SUBLANES) == 0 else tm
    tm_mid = n_rows // 6 if n_rows % (6 * 2 * SUBLANES) == 0 else tm
    f_sub1_s5, f_sub1_cv, f_sub2 = _make_sub1_s5(alpha), _make_sub1_cv(alpha), _make_sub2(alpha)

    def layer_weights(i):
        mixer = [("s5_w_glu", i // 2, 1)] if i % 2 == 0 else [("cv_w_pw1", i // 2, 1), ("cv_w_pw2", i // 2, 0)]
        return mixer + [("mlp_w1", i, 1), ("mlp_w2", i, 0)]

    weights, wgrads, received = {}, {}, {}
    small_all, c_all = _exchange("gather_small", [_pack([a[n] for n in SHARDED_SMALL]), c], ["slot", "slot"])
    full = dict(zip(SHARDED_SMALL, _unpack_gathered(small_all, [a[n].shape for n in SHARDED_SMALL])))
    c_all = c_all.reshape(N_DEV * nb, d)
    cond_rows = N_DEV * nb + SUBLANES
    cc = jnp.concatenate([c_all, a["c_ctx"][None], jnp.zeros((SUBLANES - 1, d), F32)], axis=0)

    n_ada = a["w_ada"].shape[2]
    b_loc = lax.dynamic_slice(a["b_ada"], (0, me * n_ada), (nl, n_ada))[:, None, :]
    mod_cols = _ada_fwd(cc, a["w_ada"], b_loc)
    mod_all = _exchange("gather_mod", [mod_cols.reshape(nl * cond_rows, n_ada)], ["slot"])[0].reshape(N_DEV, nl, cond_rows, n_ada)
    mod_mine = jnp.concatenate([lax.dynamic_slice(mod_all, (0, 0, nb * me, 0), (N_DEV, nl, nb, n_ada)),
                                mod_all[:, :, N_DEV * nb:N_DEV * nb + 1]], axis=2)
    mod = jnp.transpose(mod_mine, (1, 2, 0, 3)).reshape(nl, nb + 1, 6, 1, d)

    def seg(i, q):
        return mod[i, :, q]

    zero_seg = jnp.zeros((nb + 1, 1, d), F32)

    def vec(v):
        return v.reshape(1, -1)

    pos = _grid_pos_embed(seq // GRID_W, d)
    def latent_rows(v):
        return jnp.transpose(v.reshape(nb, tpl, tr, d), (1, 0, 2, 3)).reshape(nb * seq, d)

    xc = jnp.concatenate([latent_rows(x), ctx.reshape(nb * lc, d)], axis=0)
    pos_rows = jnp.concatenate([pos, jnp.zeros((tr, d), F32)], axis=0)

    def pos_tile(t):
        return jnp.where(t < nb * tpl, t % tpl, tpl)

    x_cur, h_cur = _rowwise("entry", _f_entry, [xc, pos_rows], [seg(0, 0), seg(0, 1)], [], [(d, F32), (d, BF16)], [], [], cfg, (1,),
                              {1: pos_tile})
    saved = []
    for i in range(nl):
        j = i // 2
        sv = {"x": x_cur, "h": h_cur}
        sh1, sc1, g1, sh2, sc2, g2 = (seg(i, q) for q in range(6))
        gain0, bias0, gain1, bias1 = (vec(full["ln_gain"][i, 0]), vec(full["ln_bias"][i, 0]),
                                      vec(full["ln_gain"][i, 1]), vec(full["ln_bias"][i, 1]))
        if i % 2 == 0:
            lam_re, lam_im = a["s5_lam_re"][j], a["s5_lam_im"][j]
            log_dt = a["s5_log_dt"][j][:, :, None]
            b_re_t = jnp.transpose(a["s5_b_re"][j], (0, 3, 1, 2))
            b_im_t = jnp.transpose(a["s5_b_im"][j], (0, 3, 1, 2))
            sv["prep_in"] = (lam_re, lam_im, log_dt, b_re_t, b_im_t)
            ab_re, ab_im, bb_re, bb_im = _s5_prep(f"s5_prep{i}", *sv["prep_in"])
            sv["ab"] = (ab_re, ab_im)
            ys = []
            for dirn in range(2):
                def blocks(t):
                    return jnp.transpose(t, (1, 0, 2)).reshape(ns, gs, ch, n_state)
                bmat = jnp.concatenate([_block_diag(blocks(bb_re[dirn])), _block_diag(blocks(bb_im[dirn]))], axis=2).astype(BF16)
                c_re_t = jnp.transpose(a["s5_c_re"][j, dirn], (0, 2, 1)).reshape(ns, gs, n_state, ch)
                c_im_t = jnp.transpose(a["s5_c_im"][j, dirn], (0, 2, 1)).reshape(ns, gs, n_state, ch)
                cmat = jnp.concatenate([_block_diag(c_re_t), -_block_diag(c_im_t)], axis=1).astype(BF16)
                tab = _s5_tables(ab_re[dirn], ab_im[dirn], dirn == 1, False, ns, tr // SUBLANES)
                group = layer_weights(i + dirn)
                cps = _Copies([a[n][idx].astype(BF16) for n, idx, _ in group], [axis for _, _, axis in group])
                h_states, y_dir, *gathered = _s5_fwd(f"s5_fwd{i}_{dirn}", h_cur, bmat, cmat, tab, dirn, cfg, cps)
                weights.update({(n, idx): w[None] for (n, idx, _), w in zip(group, gathered)})
                sv[f"mats{dirn}"] = (jnp.transpose(bmat, (0, 2, 1)), jnp.transpose(cmat, (0, 2, 1)))
                sv[f"states{dirn}"] = h_states
                ys.append(y_dir)
            sv["y"] = ys
            dsk = vec(a["s5_d"][j])
            z = _rowwise(f"gelu{i}", _f_gelu, [x_cur, ys[0], ys[1]], [sh1, sc1], [dsk], [(d, BF16)], [], [], cfg)[0]
            zz = _mm_nn(f"glu{i}", z, weights["s5_w_glu", j], 0, tm_big, min(2 * d, 1024))[0]
            bglu = vec(a["s5_b_glu"][j])
            x1, h2 = _rowwise(f"sub1_{i}", f_sub1_s5, [x_cur, zz], [g1, sh2, sc2], [bglu, gain0, bias0],
                              [(d, F32), (d, BF16)], [], [], cfg)
            sv.update(z=z, zz=zz)
        else:
            zz = _mm_nn(f"pw1_{i}", h_cur, weights["cv_w_pw1", j], 0, tm_big, min(2 * d, 1024))[0]
            bpw1 = vec(full["cv_b_pw1"][j])
            act = _rowwise(f"cvglu{i}", _f_cvglu, [zz], [], [bpw1], [(d, F32)], [], [], cfg)[0]
            w_dw = full["cv_w_dw"][j]
            cv = _dwconv(f"dwconv{i}", act, w_dw, cfg)
            bdw, lng, lnb = vec(full["cv_b_dw"][j]), vec(full["cv_ln_g"][j]), vec(full["cv_ln_b"][j])
            s_act = _rowwise(f"cvln{i}", _f_cvln, [cv], [], [bdw, lng, lnb], [(d, BF16)], [], [], cfg)[0]
            mm = _mm_nn(f"pw2_{i}", s_act, weights["cv_w_pw2", j], 0, tm_big, d)[0]
            bpw2 = vec(full["cv_b_pw2"][j])
            x1, h2 = _rowwise(f"sub1_{i}", f_sub1_cv, [x_cur, mm], [g1, sh2, sc2], [bpw2, gain0, bias0],
                              [(d, F32), (d, BF16)], [], [], cfg)
            sv.update(zz=zz, act=act, cv=cv, s_act=s_act, mm=mm, w_dw=w_dw)
        dff = weights["mlp_w1", i].shape[2]
        p_act, r_act = _mm_nn(f"mlp1_{i}", h2, weights["mlp_w1", i], 0, tm_big, min(dff, 1024), (BF16, BF16),
                              lambda acc: (jnp.square(jnp.maximum(acc, 0.0)), jnp.maximum(acc, 0.0)))
        m_out = _mm_nn(f"mlp2_{i}", p_act, weights["mlp_w2", i], 0, tm_mid, d)[0]
        shn, scn = (seg(i + 1, 0), seg(i + 1, 1)) if i + 1 < nl else (zero_seg, zero_seg)
        x2, hn = _rowwise(f"sub2_{i}", f_sub2, [x1, m_out], [g2, shn, scn], [gain1, bias1], [(d, F32), (d, BF16)], [], [], cfg,
                          (1,) if (i + 1) % 2 == 0 and i + 1 < nl else ())
        sv.update(x1=x1, h2=h2, p=p_act, r=r_act, m=m_out, shn=shn, scn=scn)
        saved.append(sv)
        x_cur, h_cur = x2, hn

    target = jnp.concatenate([latent_rows(a["loss_target"]), jnp.zeros((nb * lc, d), F32)], axis=0)
    mask = jnp.concatenate([jnp.ones((nb, 1, d), F32), jnp.zeros((1, 1, d), F32)], axis=0)

    def f_loss(xf, tgt, msk):
        err = (xf - tgt) * msk
        part = 0.5 * jnp.sum(jnp.square(err), axis=(0, 1), keepdims=True) / d
        return err / d, jnp.broadcast_to(part, (1, LANES))

    dx_final, loss_part = _rowwise("loss", f_loss, [x_cur, target], [mask], [], [(d, F32)], [], [LANES], cfg)
    loss = lax.psum(loss_part[0, 0], ("x", "y", "c"))

    grads = {n: [None] * a[n].shape[0] for n in WEIGHTS if n not in ("c_ctx", "w_ada", "b_ada")}
    dmod = [[None] * 6 for _ in range(nl)]

    def add_mod(i, q, val):
        dmod[i][q] = val if dmod[i][q] is None else dmod[i][q] + val

    dx_parts, dh_parts = [dx_final], []
    for i in reversed(range(nl)):
        j = i // 2
        sv = saved[i]
        sh1, sc1, g1, sh2, sc2, g2 = (seg(i, q) for q in range(6))
        gain0, bias0, gain1, bias1 = (vec(full["ln_gain"][i, 0]), vec(full["ln_bias"][i, 0]),
                                      vec(full["ln_gain"][i, 1]), vec(full["ln_bias"][i, 1]))
        bwd = _vjp_fn(f_sub2, 2, (len(dx_parts), len(dh_parts)), (0, 1, 2, 3, 4, 5, 6))
        dx1, dm, dg2, dshn, dscn, dgain1, dbias1 = _rowwise(
            f"sub2_bwd{i}", bwd, [sv["x1"], sv["m"]] + dx_parts + dh_parts, [g2, sv["shn"], sv["scn"]], [gain1, bias1],
            [(d, F32), (d, BF16)], [d, d, d], [d, d], cfg)
        add_mod(i, 5, dg2)
        if i + 1 < nl:
            add_mod(i + 1, 0, dshn)
            add_mod(i + 1, 1, dscn)
        da = _mm_nt(f"mlp2_dgrad{i}", dm, weights["mlp_w2", i], 0, tm_big, min(dff, 1024), [sv["r"]], BF16,
                    lambda acc, r: (acc * 2.0 * r,))
        wgrads["mlp_w2", i] = _mm_wgrad_rows(f"mlp2_wgrad{i}", sv["p"], dm, tm_big)
        wgrads["mlp_w1", i] = _mm_wgrad_cols(f"mlp1_wgrad{i}", sv["h2"], da, tm_big)
        dh2 = _mm_nt(f"mlp1_dgrad{i}", da, weights["mlp_w1", i], 0, tm_mid, d)
        if i % 2 == 0:
            bglu = vec(a["s5_b_glu"][j])
            bwd = _vjp_fn(f_sub1_s5, 2, (1, 1), (0, 1, 2, 3, 4, 5, 6, 7))
            dxa, dzz, dg1, dsh2, dsc2, dbglu, dgain0, dbias0 = _rowwise(
                f"sub1_bwd{i}", bwd, [sv["x"], sv["zz"], dx1, dh2], [g1, sh2, sc2], [bglu, gain0, bias0],
                [(d, F32), (2 * d, BF16)], [d, d, d], [2 * d, d, d], cfg)
            grads["s5_b_glu"][j] = dbglu[0]
            wgrads["s5_w_glu", j] = _mm_wgrad_cols(f"glu_wgrad{i}", sv["z"], dzz, tm_big)
            dz = _mm_nt(f"glu_dgrad{i}", dzz, weights["s5_w_glu", j], 0, tm_mid, d)
            dsk = vec(a["s5_d"][j])
            bwd = _vjp_fn(_f_gelu, 3, (1,), (0, 1, 3, 4, 5))
            dxb, dy, dsh1, dsc1, ddsk = _rowwise(f"gelu_bwd{i}", bwd, [sv["x"], sv["y"][0], sv["y"][1], dz], [sh1, sc1], [dsk],
                                                 [(d, F32), (d, BF16)], [d, d], [d], cfg, (1,))
            grads["s5_d"][j] = ddsk[0]
            add_mod(i, 0, dsh1)
            add_mod(i, 1, dsc1)
            ab_re, ab_im = sv["ab"]
            dus, d_ab_re, d_ab_im, d_bb_re, d_bb_im, d_c_re, d_c_im = [], [], [], [], [], [], []
            for dirn in range(2):
                bmat_t, cmat_t = sv[f"mats{dirn}"]
                tab = _s5_tables(ab_re[dirn], ab_im[dirn], dirn == 0, True, ns, tr // SUBLANES)
                group = layer_weights(i + 1 - dirn)
                cps = _Copies([wgrads[n, idx] for n, idx, _ in group], ["scatter"] * len(group))
                du, d_b, d_c, d_a, *parts = _s5_bwd(f"s5_bwd{i}_{dirn}", dy, sv[f"states{dirn}"], sv["h"], cmat_t, bmat_t, tab,
                                                    dirn, cfg, cps)
                received.update({(n, idx): p for (n, idx, _), p in zip(group, parts)})
                dus.append(du)
                sw = d_a.shape[2] // 2
                d_a = jnp.sum(d_a, axis=1)
                d_ab_re.append(d_a[:, :sw].reshape(n_grp, n_state))
                d_ab_im.append(d_a[:, sw:].reshape(n_grp, n_state))

                def unblock_b(t):
                    return jnp.transpose(_diag_blocks(t, gs).reshape(n_grp, ch, n_state), (1, 0, 2))

                def unblock_c(t):
                    return jnp.transpose(_diag_blocks(t, gs).reshape(n_grp, n_state, ch), (0, 2, 1))
                d_bb_re.append(unblock_b(d_b[:, :, :sw]))
                d_bb_im.append(unblock_b(d_b[:, :, sw:]))
                d_c_re.append(unblock_c(d_c[:, :sw]))
                d_c_im.append(-unblock_c(d_c[:, sw:]))
            g_lre, g_lim, g_ldt, g_bre, g_bim = _s5_prep_bwd(
                f"s5_prep_bwd{i}", *sv["prep_in"], (jnp.stack(d_ab_re), jnp.stack(d_ab_im), jnp.stack(d_bb_re), jnp.stack(d_bb_im)))
            grads["s5_lam_re"][j], grads["s5_lam_im"][j], grads["s5_log_dt"][j] = g_lre, g_lim, g_ldt[:, :, 0]
            grads["s5_b_re"][j] = jnp.transpose(g_bre, (0, 2, 3, 1))
            grads["s5_b_im"][j] = jnp.transpose(g_bim, (0, 2, 3, 1))
            grads["s5_c_re"][j], grads["s5_c_im"][j] = jnp.stack(d_c_re), jnp.stack(d_c_im)
            dx_parts, dh_parts = [dxa, dxb], dus
        else:
            bpw2 = vec(full["cv_b_pw2"][j])
            bwd = _vjp_fn(f_sub1_cv, 2, (1, 1), (0, 1, 2, 3, 4, 5, 6, 7))
            dxa, dmm, dg1, dsh2, dsc2, dbpw2, dgain0, dbias0 = _rowwise(
                f"sub1_bwd{i}", bwd, [sv["x"], sv["mm"], dx1, dh2], [g1, sh2, sc2], [bpw2, gain0, bias0],
                [(d, F32), (d, BF16)], [d, d, d], [d, d, d], cfg)
            grads["cv_b_pw2"][j] = dbpw2[0]
            wgrads["cv_w_pw2", j] = _mm_wgrad_rows(f"pw2_wgrad{i}", sv["s_act"], dmm, tm_big)
            ds = _mm_nt(f"pw2_dgrad{i}", dmm, weights["cv_w_pw2", j], 0, tm_big, d)
            bdw, lng, lnb = vec(full["cv_b_dw"][j]), vec(full["cv_ln_g"][j]), vec(full["cv_ln_b"][j])
            bwd = _vjp_fn(_f_cvln, 1, (1,), (0, 1, 2, 3))
            dcv, dbdw, dlng, dlnb = _rowwise(f"cvln_bwd{i}", bwd, [sv["cv"], ds], [], [bdw, lng, lnb], [(d, F32)], [], [d, d, d], cfg)
            grads["cv_b_dw"][j], grads["cv_ln_g"][j], grads["cv_ln_b"][j] = dbdw[0], dlng[0], dlnb[0]
            dact = _dwconv(f"dwconv_bwd{i}", dcv, sv["w_dw"][::-1], cfg)
            grads["cv_w_dw"][j] = _dwconv_wgrad(f"dwconv_wgrad{i}", sv["act"], dcv, sv["w_dw"].shape[0], cfg)
            bpw1 = vec(full["cv_b_pw1"][j])
            bwd = _vjp_fn(_f_cvglu, 1, (1,), (0, 1))
            dzz, dbpw1 = _rowwise(f"cvglu_bwd{i}", bwd, [sv["zz"], dact], [], [bpw1], [(2 * d, BF16)], [], [2 * d], cfg)
            grads["cv_b_pw1"][j] = dbpw1[0]
            wgrads["cv_w_pw1", j] = _mm_wgrad_cols(f"pw1_wgrad{i}", sv["h"], dzz, tm_big)
            dh = _mm_nt(f"pw1_dgrad{i}", dzz, weights["cv_w_pw1", j], 0, tm_mid, d)
            dx_parts, dh_parts = [dxa], [dh]
        grads["ln_gain"][i] = jnp.stack([dgain0[0], dgain1[0]])
        grads["ln_bias"][i] = jnp.stack([dbias0[0], dbias1[0]])
        add_mod(i, 2, dg1)
        add_mod(i, 3, dsh2)
        add_mod(i, 4, dsc2)
    bwd = _vjp_fn(_f_entry, 2, (len(dx_parts), len(dh_parts)), (0, 2, 3))
    dxc, dsh1, dsc1 = _rowwise("entry_bwd", bwd, [xc, pos_rows] + dx_parts + dh_parts, [seg(0, 0), seg(0, 1)], [],
                               [(d, F32)], [d, d], [], cfg, (), {1: pos_tile})
    add_mod(0, 0, dsh1)
    add_mod(0, 1, dsc1)
    grad_x = jnp.transpose(dxc[:nb * seq].reshape(tpl, nb, tr, d), (1, 0, 2, 3)).reshape(nb, seq, d)

    dmod_loc = jnp.stack([jnp.concatenate([q[:, 0] for q in dmod[i]], axis=1) for i in range(nl)])
    dmod_all = _exchange("gather_dmod", [dmod_loc.reshape(nl * (nb + 1), 6 * d)], ["slot"])[0].reshape(N_DEV, nl, nb + 1, 6 * d)
    mine = lax.dynamic_slice(dmod_all, (0, 0, 0, me * n_ada), (N_DEV, nl, nb + 1, n_ada))
    dmod_rows = jnp.transpose(mine[:, :, :nb], (1, 0, 2, 3)).reshape(nl, N_DEV * nb, n_ada)
    dmod_rows = jnp.concatenate([dmod_rows, jnp.zeros((nl, SUBLANES, n_ada), F32)], axis=1)
    g_w_ada, _, dcond = _ada_bwd(cc, a["w_ada"], dmod_rows, mine[:, :, nb:])
    g_b_ada = _sum_lead("b_ada_sum", jnp.transpose(dmod_all, (0, 2, 1, 3)).reshape(N_DEV * (nb + 1), nl, 6 * d), nl)

    small_names = SHARDED_SMALL + REPLICATED_SMALL
    small_full = [jnp.stack(grads[n]) for n in small_names]
    small_packed = _pack(small_full, N_DEV * SUBLANES)
    small_parts, dcond_all = _exchange("scatter_small_grads", [small_packed.reshape(N_DEV, -1, LANES), dcond[N_DEV * nb:N_DEV * nb + 1]],
                                       ["scatter", "slot"])
    g_c_ctx = _cctx_grad(dcond_all, a["c_ctx"][None])[0]
    small_part = _sum_lead("small_grad_sum", small_parts, _row_tile(small_parts.shape[1], 512))
    small_sum = _exchange("gather_small_sum", [small_part], ["slot"])[0]
    small_g = dict(zip(small_names, _unpack(small_sum, [g.shape for g in small_full])))
    for n in SHARDED_SMALL:
        width = a[n].shape[-1]
        start = (0,) * (small_g[n].ndim - 1) + (me * width,)
        small_g[n] = lax.dynamic_slice(small_g[n], start, a[n].shape)
    small_g["c_ctx"], small_g["b_ada"] = g_c_ctx, g_b_ada

    out = {}

    def update(n, parts):
        shp = a[n].shape
        cols = parts.shape[-1]
        rows = parts.shape[1]
        res = _adamw(f"adamw_{n}", parts, a[n].reshape(rows, cols), a["m_" + n].reshape(rows, cols), a["v_" + n].reshape(rows, cols),
                     _row_tile(rows, max(SUBLANES, 131072 // cols)))
        out[n] = [r.reshape(shp) for r in res]

    for n in BIG:
        rows, cols = a[n].shape[1:]
        bufs = [lax.empty(a[n].shape, F32) for _ in range(4)]
        for idx in range(a[n].shape[0]):
            bufs = _adamw_layer(f"adamw_{n}{idx}", received[n, idx], a[n], a["m_" + n], a["v_" + n], bufs, idx,
                                _row_tile(rows, max(SUBLANES, 131072 // cols)))
        out[n] = bufs
    update("w_ada", g_w_ada.reshape(1, -1, n_ada))
    for n in NATIVE_SMALL:
        out[n] = [small_g[n], *_adamw_native(f"adamw_{n}", small_g[n], a[n], a["m_" + n], a["v_" + n])]
    small_all_names = ("c_ctx", "b_ada") + tuple(n for n in small_names if n not in NATIVE_SMALL)
    packed = [_pack([src[n] for n in small_all_names]) for src in
              (small_g, a, {n: a["m_" + n] for n in small_all_names}, {n: a["v_" + n] for n in small_all_names})]
    res = _adamw("adamw_small", packed[0][None], packed[1], packed[2], packed[3], _row_tile(packed[0].shape[0], 512))
    shapes = [a[n].shape for n in small_all_names]
    for n, vals in zip(small_all_names, zip(*[_unpack(r, shapes) for r in res])):
        out[n] = list(vals)
    return (loss, grad_x, *[out[n][0] for n in WEIGHTS], *[out[n][1] for n in WEIGHTS],
            *[out[n][2] for n in WEIGHTS], *[out[n][3] for n in WEIGHTS])


def kernel(x, c, ctx, c_ctx, w_ada, b_ada, ln_gain, ln_bias, s5_lam_re, s5_lam_im, s5_log_dt, s5_b_re, s5_b_im, s5_c_re, s5_c_im, s5_d, s5_w_glu, s5_b_glu, cv_w_pw1, cv_b_pw1, cv_w_dw, cv_b_dw, cv_ln_g, cv_ln_b, cv_w_pw2, cv_b_pw2, mlp_w1, mlp_w2, loss_target, m_c_ctx, m_w_ada, m_b_ada, m_ln_gain, m_ln_bias, m_s5_lam_re, m_s5_lam_im, m_s5_log_dt, m_s5_b_re, m_s5_b_im, m_s5_c_re, m_s5_c_im, m_s5_d, m_s5_w_glu, m_s5_b_glu, m_cv_w_pw1, m_cv_b_pw1, m_cv_w_dw, m_cv_b_dw, m_cv_ln_g, m_cv_ln_b, m_cv_w_pw2, m_cv_b_pw2, m_mlp_w1, m_mlp_w2, v_c_ctx, v_w_ada, v_b_ada, v_ln_gain, v_ln_bias, v_s5_lam_re, v_s5_lam_im, v_s5_log_dt, v_s5_b_re, v_s5_b_im, v_s5_c_re, v_s5_c_im, v_s5_d, v_s5_w_glu, v_s5_b_glu, v_cv_w_pw1, v_cv_b_pw1, v_cv_w_dw, v_cv_b_dw, v_cv_ln_g, v_cv_ln_b, v_cv_w_pw2, v_cv_b_pw2, v_mlp_w1, v_mlp_w2):
    return _step(dict(locals()))
```

```python
import functools
import math

import jax
import jax.numpy as jnp
from jax import lax
from jax.experimental import pallas as pl
from jax.experimental.pallas import tpu as pltpu

F32 = jnp.float32
BF16 = jnp.bfloat16
N_DEV = 8
LANES = 128
SUBLANES = 8
VMEM_LIMIT = 56 * 1024 * 1024
GRID_W = 64
POS_TEMP = 10000.0
LN_EPS = 1e-5
LAMBDA_RE_MAX = -1e-4
ADAM_LR, ADAM_B1, ADAM_B2, ADAM_EPS, ADAM_WD, ADAM_STEP = 0.001, 0.9, 0.999, 1e-08, 0.01, 10
MESH = pl.DeviceIdType.MESH


def _params(sem):
    return pltpu.CompilerParams(dimension_semantics=sem, vmem_limit_bytes=VMEM_LIMIT)


def _accumulate(ref, val, first):
    @pl.when(first)
    def _():
        ref[...] = val

    @pl.when(jnp.logical_not(first))
    def _():
        ref[...] += val


def _rowwise(name, fn, rows, segs, vecs, row_outs, seg_accs, vec_accs, cfg, interleaved=(), row_tiles=None):
    tr, tpl, nb = cfg["tr"], cfg["tpl"], cfg["nb"]
    n_rows = rows[0].shape[0]
    nt = n_rows // tr
    nr, ns, nv = len(rows), len(segs), len(vecs)
    nro, nsa, nva = len(row_outs), len(seg_accs), len(vec_accs)

    def seg_of(t):
        return jnp.minimum(t // tpl, nb)

    def body(*refs):
        t = pl.program_id(0)
        ins, outs, mix_refs = refs[:nr + ns + nv], refs[nr + ns + nv:nr + ns + nv + nro + nsa + nva], refs[nr + ns + nv + nro + nsa + nva:]
        vals = [r[...] for r in ins[:nr]] + [r[0] for r in ins[nr:nr + ns]] + [r[...] for r in ins[nr + ns:]]
        res = fn(*vals)
        for idx, (o, v) in enumerate(zip(outs[:nro], res[:nro])):
            if idx in interleaved:
                mix_ref = mix_refs[interleaved.index(idx)]
                for p in range(mix_ref.shape[0]):
                    mix_ref[p] = v[:, p * LANES:(p + 1) * LANES].astype(F32)
                o[...] = jnp.concatenate([_interleave_rows(mix_ref.at[p], 1, o.dtype) for p in range(mix_ref.shape[0])], axis=1)
            else:
                o[...] = v.astype(o.dtype)
        first_seg = jnp.logical_or(t == 0, seg_of(t) != seg_of(jnp.maximum(t - 1, 0)))
        for o, v in zip(outs[nro:nro + nsa], res[nro:nro + nsa]):
            _accumulate(o.at[0], v, first_seg)
        for o, v in zip(outs[nro + nsa:], res[nro + nsa:]):
            _accumulate(o, v, t == 0)

    row_tiles = row_tiles or {}
    in_specs = ([pl.BlockSpec((tr, a.shape[1]), lambda t, pick=row_tiles.get(i, lambda t: _phys_tile(t, cfg)): (pick(t), 0))
                 for i, a in enumerate(rows)]
                + [pl.BlockSpec((1, 1, a.shape[2]), lambda t: (seg_of(t), 0, 0)) for a in segs]
                + [pl.BlockSpec((1, a.shape[1]), lambda t: (0, 0)) for a in vecs])
    out_specs = ([pl.BlockSpec((tr, c), lambda t: (_phys_tile(t, cfg), 0)) for c, _ in row_outs]
                 + [pl.BlockSpec((1, 1, c), lambda t: (seg_of(t), 0, 0)) for c in seg_accs]
                 + [pl.BlockSpec((1, c), lambda t: (0, 0)) for c in vec_accs])
    out_shape = ([jax.ShapeDtypeStruct((n_rows, c), dt) for c, dt in row_outs]
                 + [jax.ShapeDtypeStruct((nb + 1, 1, c), F32) for c in seg_accs]
                 + [jax.ShapeDtypeStruct((1, c), F32) for c in vec_accs])
    return pl.pallas_call(body, name=name, grid=(nt,), in_specs=in_specs, out_specs=out_specs, out_shape=out_shape,
                          scratch_shapes=[pltpu.VMEM((row_outs[idx][0] // LANES, tr, LANES), F32) for idx in interleaved],
                          compiler_params=_params(("arbitrary",)))(*rows, *segs, *vecs)


def _vjp_fn(fn, n_row, cot_groups, want):
    n_cot = sum(cot_groups)

    def bwd(*args):
        primals = [a.astype(F32) for a in args[:n_row] + args[n_row + n_cot:]]
        outs, vjp = jax.vjp(fn, *primals)
        cots, pos = [], n_row
        for n, o in zip(cot_groups, outs):
            cot = jnp.zeros_like(o)
            for part in args[pos:pos + n]:
                cot = cot + part.astype(F32)
            cots.append(cot)
            pos += n
        grads = vjp(tuple(cots))
        return tuple(grads[i] for i in want)
    return bwd


def _ln(r, g, b):
    mu = jnp.mean(r, axis=-1, keepdims=True)
    var = jnp.mean(jnp.square(r - mu), axis=-1, keepdims=True)
    return (r - mu) * lax.rsqrt(var + LN_EPS) * g + b


def _glu(zz, bias):
    d = zz.shape[1] // 2
    return (zz[:, :d] + bias[:, :d]) * jax.nn.sigmoid(zz[:, d:] + bias[:, d:])


def _f_entry(xc, pos, sh, sc):
    x0 = xc + pos
    return x0, x0 * (1 + sc) + sh


def _f_gelu(x, y0, y1, sh, sc, dsk):
    u = x * (1 + sc) + sh
    y = dsk * u + y0 + y1
    return (0.5 * y * (1.0 + lax.erf(y * (2.0 ** -0.5))),)


def _make_sub1_s5(alpha):
    def f(x, zz, g1, sh2, sc2, bglu, gain, bias):
        x1 = _ln(alpha * x + g1 * _glu(zz, bglu), gain, bias)
        return x1, x1 * (1 + sc2) + sh2
    return f


def _make_sub1_cv(alpha):
    def f(x, mm, g1, sh2, sc2, bpw2, gain, bias):
        x1 = _ln(alpha * x + g1 * (mm + bpw2), gain, bias)
        return x1, x1 * (1 + sc2) + sh2
    return f


def _make_sub2(alpha):
    def f(x1, m, g2, shn, scn, gain, bias):
        x2 = _ln(alpha * x1 + g2 * m, gain, bias)
        return x2, x2 * (1 + scn) + shn
    return f


def _f_cvglu(zz, bpw1):
    return (_glu(zz, bpw1),)


def _f_cvln(cv, bdw, lng, lnb):
    return (jax.nn.silu(_ln(cv + bdw, lng, lnb)),)


def _matmul(name, a, b, extras, grid, a_spec, b_spec, extra_specs, o_specs, out_shape, dims, red_axis, epi, sem, acc_shape=None):
    n_extra = len(extras)
    n_out = len(out_shape)
    if acc_shape is None:
        acc_shape = tuple(s for s in o_specs[0].block_shape if s is not None)

    def body(*refs):
        a_ref, b_ref = refs[0], refs[1]
        ex = refs[2:2 + n_extra]
        outs = refs[2 + n_extra:2 + n_extra + n_out]
        prod = lax.dot_general(a_ref[...], b_ref[...], dims, preferred_element_type=F32)

        def finish(acc):
            res = epi(acc, *[e[...] for e in ex]) if epi is not None else (acc,)
            for o, v in zip(outs, res):
                o[...] = v.astype(o.dtype)

        if red_axis is None:
            finish(prod)
        else:
            acc_ref = refs[-1]
            k = pl.program_id(red_axis)
            nk = pl.num_programs(red_axis)

            @pl.when(k == 0)
            def _():
                acc_ref[...] = prod

            @pl.when(k > 0)
            def _():
                acc_ref[...] += prod

            @pl.when(k == nk - 1)
            def _():
                finish(acc_ref[...])

    scratch = [] if red_axis is None else [pltpu.VMEM(acc_shape, F32)]
    res = pl.pallas_call(body, name=name, grid=grid, in_specs=[a_spec, b_spec] + list(extra_specs),
                         out_specs=list(o_specs), out_shape=list(out_shape), scratch_shapes=scratch,
                         compiler_params=_params(sem))(a, b, *extras)
    return res


NN = (((1,), (0,)), ((), ()))
NT = (((1,), (1,)), ((), ()))
TN = (((0,), (0,)), ((), ()))


def _mm_nn(name, a, w3, layer, tm, tn, out_dtypes=(F32,), epi=None):
    m, k = a.shape
    n = w3.shape[2]
    return _matmul(name, a, w3, (), (n // tn, m // tm),
                   pl.BlockSpec((tm, k), lambda j, i: (i, 0)), pl.BlockSpec((None, k, tn), lambda j, i: (layer, 0, j)), (),
                   [pl.BlockSpec((tm, tn), lambda j, i: (i, j)) for _ in out_dtypes],
                   [jax.ShapeDtypeStruct((m, n), dt) for dt in out_dtypes], NN, None, epi, ("arbitrary", "arbitrary"))


def _mm_nt(name, dy, w3, layer, tm, tkw, extras=(), out_dtype=F32, epi=None):
    m, n = dy.shape
    kw = w3.shape[1]
    return _matmul(name, dy, w3, tuple(extras), (kw // tkw, m // tm),
                   pl.BlockSpec((tm, n), lambda j, i: (i, 0)), pl.BlockSpec((None, tkw, n), lambda j, i: (layer, j, 0)),
                   [pl.BlockSpec((tm, tkw), lambda j, i: (i, j)) for _ in extras],
                   [pl.BlockSpec((tm, tkw), lambda j, i: (i, j))], [jax.ShapeDtypeStruct((m, kw), out_dtype)], NT, None, epi,
                   ("arbitrary", "arbitrary"))[0]


def _mm_wgrad_cols(name, a, dy, tm):
    m, k = a.shape
    n = dy.shape[1] // N_DEV
    per = max(1, min(N_DEV, 1024 // n))
    return _matmul(name, a, dy, (), (N_DEV // per, m // tm),
                   pl.BlockSpec((tm, k), lambda j, i: (i, 0)), pl.BlockSpec((tm, per * n), lambda j, i: (i, j)), (),
                   [pl.BlockSpec((per, k, n), lambda j, i: (j, 0, 0))], [jax.ShapeDtypeStruct((N_DEV, k, n), BF16)],
                   TN, 1, lambda acc: (jnp.stack([acc[:, q * n:(q + 1) * n] for q in range(per)]),), ("arbitrary", "arbitrary"),
                   (k, per * n))[0]


def _mm_wgrad_rows(name, a, dy, tm):
    m, k = a.shape
    r = k // N_DEV
    n = dy.shape[1]
    rows = r * max(1, min(N_DEV, 1024 // r))
    out = _matmul(name, a, dy, (), (k // rows, m // tm),
                  pl.BlockSpec((tm, rows), lambda j, i: (i, j)), pl.BlockSpec((tm, n), lambda j, i: (i, 0)), (),
                  [pl.BlockSpec((rows, n), lambda j, i: (j, 0))], [jax.ShapeDtypeStruct((k, n), BF16)],
                  TN, 1, None, ("arbitrary", "arbitrary"))[0]
    return out.reshape(N_DEV, r, n)


class _Copies:
    def __init__(self, arrays, kinds):
        self.arrays, self.kinds, self.n = list(arrays), list(kinds), len(arrays)
        any_spec = pl.BlockSpec(memory_space=pl.ANY)
        self.in_specs = [any_spec] * self.n
        self.out_specs = [any_spec] * self.n
        self.out_shape = [jax.ShapeDtypeStruct(self._result(a, kind), a.dtype) for a, kind in zip(arrays, kinds)]
        self.scratch = [pltpu.SemaphoreType.DMA((self.n, N_DEV - 1)), pltpu.SemaphoreType.DMA((self.n, N_DEV - 1)),
                        pltpu.SemaphoreType.DMA((self.n,))] if self.n else []

    @staticmethod
    def _result(a, kind):
        if kind == "slot":
            return (N_DEV,) + a.shape
        if kind == "scatter":
            return a.shape
        return a.shape[:kind] + (N_DEV * a.shape[kind],) + a.shape[kind + 1:]

    def descriptors(self, ins, outs, sems):
        send_sems, recv_sems, local_sems = sems
        x, y, c = lax.axis_index("x"), lax.axis_index("y"), lax.axis_index("c")
        me = 4 * x + 2 * y + c
        first, relay, finish = [], [], []

        def remote(i, k, src, dst, to):
            return pltpu.make_async_remote_copy(src_ref=src, dst_ref=dst, send_sem=send_sems.at[i, k], recv_sem=recv_sems.at[i, k],
                                                device_id=to, device_id_type=MESH)

        for i, kind in enumerate(self.kinds):
            if kind in ("slot", "scatter"):
                scatter = kind == "scatter"
                local = pltpu.make_async_copy(ins[i].at[me] if scatter else ins[i], outs[i].at[me], local_sems.at[i])
                first.append(local)
                finish.append((local, "all"))
                for k in range(1, N_DEV):
                    px = 1 - x if k & 4 else x
                    py = 1 - y if k & 2 else y
                    pc = 1 - c if k & 1 else c
                    cp = remote(i, k - 1, ins[i].at[4 * px + 2 * py + pc] if scatter else ins[i], outs[i].at[me], (px, py, pc))
                    first.append(cp)
                    finish.append((cp, "all"))
                continue
            size = ins[i].shape[kind]

            def block(px, py, pc):
                return outs[i].at[(slice(None),) * kind + (pl.ds(pl.multiple_of((4 * px + 2 * py + pc) * size, size), size),)]

            local = pltpu.make_async_copy(ins[i], block(x, y, c), local_sems.at[i])
            sibling = remote(i, 0, ins[i], block(x, y, c), (x, y, 1 - c))
            first += [local, sibling]
            finish += [(local, "all"), (sibling, "all")]
            for j, (qx, qy) in enumerate([(1 - x, y), (x, 1 - y), (1 - x, 1 - y)]):
                out = remote(i, 1 + j, ins[i], block(x, y, c), (qx, qy, c))
                onward = remote(i, 4 + j, block(qx, qy, c), block(qx, qy, c), (x, y, 1 - c))
                first.append(out)
                relay.append((out, onward))
                finish += [(out, "send"), (onward, "all")]
        return first, relay, finish


def _guarded(when, fn):
    if when is None:
        fn()
    else:
        pl.when(when)(fn)


def _start_all(plan, when=None):
    def run():
        for cp in plan[0]:
            cp.start()
    _guarded(when, run)


def _relay_all(plan, when=None):
    def run():
        for arrived, onward in plan[1]:
            arrived.wait_recv()
            onward.start()
    if plan[1]:
        _guarded(when, run)


def _wait_all(plan, when=None):
    def run():
        for cp, left in plan[2]:
            if left == "send":
                cp.wait_send()
            else:
                cp.wait()
    _guarded(when, run)


def _exchange(name, arrays, kinds):
    cps = _Copies(arrays, kinds)
    n = cps.n

    def body(*refs):
        plan = cps.descriptors(refs[:n], refs[n:2 * n], refs[2 * n:])
        _start_all(plan)
        _relay_all(plan)
        _wait_all(plan)

    return pl.pallas_call(body, name=name, in_specs=cps.in_specs, out_specs=cps.out_specs, out_shape=cps.out_shape,
                          scratch_shapes=cps.scratch)(*arrays)


def _carried(cps, n_in, n_out, n_scratch, refs):
    if cps is None:
        return ([], [], []), refs
    n = cps.n
    ins = refs[n_in:n_in + n]
    outs = refs[n_in + n + n_out:n_in + n + n_out + n]
    sems = refs[n_in + n + n_out + n + n_scratch:]
    own = refs[:n_in] + refs[n_in + n:n_in + n + n_out] + refs[n_in + n + n_out + n:n_in + n + n_out + n + n_scratch]
    return cps.descriptors(ins, outs, sems), own


def _sum_lead(name, parts, tr):
    npart, r, c = parts.shape

    def body(p_ref, o_ref):
        acc = p_ref[0].astype(F32)
        for p in range(1, npart):
            acc = acc + p_ref[p].astype(F32)
        o_ref[...] = acc

    return pl.pallas_call(body, name=name, grid=(r // tr,), in_specs=[pl.BlockSpec((npart, tr, c), lambda i: (0, i, 0))],
                          out_specs=pl.BlockSpec((tr, c), lambda i: (i, 0)), out_shape=jax.ShapeDtypeStruct((r, c), F32),
                          compiler_params=_params(("arbitrary",)))(parts)


def _adamw_math(g, w, m, v):
    m2 = ADAM_B1 * m + (1.0 - ADAM_B1) * g
    v2 = ADAM_B2 * v + (1.0 - ADAM_B2) * jnp.square(g)
    m_hat = m2 / (1.0 - ADAM_B1 ** ADAM_STEP)
    v_hat = v2 / (1.0 - ADAM_B2 ** ADAM_STEP)
    return -ADAM_LR * (m_hat / (jnp.sqrt(v_hat) + ADAM_EPS) + ADAM_WD * w), m2, v2


def _adamw_body(npart):
    def body(p_ref, w_ref, m_ref, v_ref, *rest):
        g_out, d_out, m_out, v_out = rest[-4:]
        g = p_ref[0].astype(F32)
        for p in range(1, npart):
            g = g + p_ref[p].astype(F32)
        g_out[...] = g
        d_out[...], m_out[...], v_out[...] = _adamw_math(g, w_ref[...], m_ref[...], v_ref[...])
    return body


def _adamw_native(name, g, w, m, v):
    rest = w.shape[2:]
    spec = pl.BlockSpec((None, None) + rest, lambda i, j: (i, j) + (0,) * len(rest))

    def body(g_ref, w_ref, m_ref, v_ref, d_out, m_out, v_out):
        d_out[...], m_out[...], v_out[...] = _adamw_math(g_ref[...], w_ref[...], m_ref[...], v_ref[...])

    return pl.pallas_call(body, name=name, grid=w.shape[:2], in_specs=[spec] * 4, out_specs=[spec] * 3,
                          out_shape=[jax.ShapeDtypeStruct(w.shape, F32)] * 3,
                          compiler_params=_params(("arbitrary", "arbitrary")))(g, w, m, v)


def _adamw(name, parts, w, m, v, tr):
    npart, r, c = parts.shape
    row = pl.BlockSpec((tr, c), lambda i: (i, 0))
    return pl.pallas_call(_adamw_body(npart), name=name, grid=(r // tr,),
                          in_specs=[pl.BlockSpec((npart, tr, c), lambda i: (0, i, 0)), row, row, row],
                          out_specs=[row] * 4, out_shape=[jax.ShapeDtypeStruct((r, c), F32)] * 4,
                          compiler_params=_params(("arbitrary",)))(parts, w, m, v)


def _adamw_layer(name, parts, w3, m3, v3, bufs, layer, tr):
    npart, r, c = parts.shape
    lay = pl.BlockSpec((None, tr, c), lambda i: (layer, i, 0))
    hbm = pl.BlockSpec(memory_space=pl.ANY)
    return pl.pallas_call(_adamw_body(npart), name=name, grid=(r // tr,),
                          in_specs=[pl.BlockSpec((npart, tr, c), lambda i: (0, i, 0)), lay, lay, lay] + [hbm] * 4,
                          out_specs=[lay] * 4, out_shape=[jax.ShapeDtypeStruct(w3.shape, F32)] * 4,
                          input_output_aliases={4: 0, 5: 1, 6: 2, 7: 3},
                          compiler_params=_params(("arbitrary",)))(parts, w3, m3, v3, *bufs)


def _row_tile(r, cap):
    if r <= cap:
        return r
    t = cap - cap % SUBLANES
    while r % t:
        t -= SUBLANES
    return t


def _ada_fwd(cc, w_ada, b_loc):
    nl, d, n = w_ada.shape
    rows = cc.shape[0]

    def body(c_ref, w_ref, b_ref, o_ref):
        cond = jax.nn.silu(c_ref[...]).astype(BF16)
        o_ref[...] = jnp.dot(cond, w_ref[...].astype(BF16), preferred_element_type=F32) + b_ref[...]

    return pl.pallas_call(body, name="ada_fwd", grid=(nl,),
                          in_specs=[pl.BlockSpec((rows, d), lambda i: (0, 0)), pl.BlockSpec((None, d, n), lambda i: (i, 0, 0)),
                                    pl.BlockSpec((None, 1, n), lambda i: (i, 0, 0))],
                          out_specs=pl.BlockSpec((None, rows, n), lambda i: (i, 0, 0)),
                          out_shape=jax.ShapeDtypeStruct((nl, rows, n), F32), compiler_params=_params(("arbitrary",)))(cc, w_ada, b_loc)


def _ada_bwd(cc, w_ada, dmod_rows, dmod_ctx):
    nl, d, n = w_ada.shape
    rows = cc.shape[0]
    ctx_row = rows - SUBLANES

    def body(c_ref, w_ref, dr_ref, dc_ref, gw_ref, tot_ref, dcond_ref):
        i = pl.program_id(0)
        total = dc_ref[0]
        for p in range(1, N_DEV):
            total = total + dc_ref[p]
        tot_ref[...] = total
        row_id = lax.broadcasted_iota(jnp.int32, (rows, n), 0)
        dm = jnp.where(row_id == ctx_row, jnp.broadcast_to(total, (rows, n)), dr_ref[...]).astype(BF16)
        cond = jax.nn.silu(c_ref[...]).astype(BF16)
        gw_ref[...] = lax.dot_general(cond, dm, TN, preferred_element_type=F32)
        part = lax.dot_general(dm, w_ref[...].astype(BF16), NT, preferred_element_type=F32)
        _accumulate(dcond_ref, part, i == 0)

    return pl.pallas_call(body, name="ada_bwd", grid=(nl,),
                          in_specs=[pl.BlockSpec((rows, d), lambda i: (0, 0)), pl.BlockSpec((None, d, n), lambda i: (i, 0, 0)),
                                    pl.BlockSpec((None, rows, n), lambda i: (i, 0, 0)),
                                    pl.BlockSpec((N_DEV, None, 1, n), lambda i: (0, i, 0, 0))],
                          out_specs=[pl.BlockSpec((None, d, n), lambda i: (i, 0, 0)), pl.BlockSpec((None, 1, n), lambda i: (i, 0, 0)),
                                     pl.BlockSpec((rows, d), lambda i: (0, 0))],
                          out_shape=[jax.ShapeDtypeStruct((nl, d, n), F32), jax.ShapeDtypeStruct((nl, 1, n), F32),
                                     jax.ShapeDtypeStruct((rows, d), F32)],
                          compiler_params=_params(("arbitrary",)))(cc, w_ada, dmod_rows, dmod_ctx)


def _cctx_grad(parts, c_ctx):
    def body(p_ref, c_ref, o_ref):
        tot = p_ref[0]
        for p in range(1, N_DEV):
            tot = tot + p_ref[p]
        _, vjp = jax.vjp(jax.nn.silu, c_ref[...])
        o_ref[...] = vjp(tot)[0]

    return pl.pallas_call(body, name="cctx_grad", out_shape=jax.ShapeDtypeStruct(c_ctx.shape, F32))(parts, c_ctx)


def _discretise(lam_re, lam_im, log_dt, b_re, b_im):
    lr = jnp.minimum(lam_re, LAMBDA_RE_MAX)
    li = lam_im
    dt = jnp.exp(log_dt)
    mag = jnp.exp(lr * dt)
    ab_re = mag * jnp.cos(li * dt)
    ab_im = mag * jnp.sin(li * dt)
    den = lr * lr + li * li
    nr = ab_re - 1.0
    ni = ab_im
    coef_re = ((nr * lr + ni * li) / den)[:, None]
    coef_im = ((ni * lr - nr * li) / den)[:, None]
    bb_re = coef_re * b_re - coef_im * b_im
    bb_im = coef_re * b_im + coef_im * b_re
    return ab_re, ab_im, bb_re, bb_im


def _s5_prep(name, lam_re, lam_im, log_dt, b_re, b_im):
    def body(a, b, c, d, e, o1, o2, o3, o4):
        res = _discretise(a[...], b[...], c[...], d[...], e[...])
        for o, v in zip((o1, o2, o3, o4), res):
            o[...] = v

    shp = [jax.ShapeDtypeStruct(lam_re.shape, F32)] * 2 + [jax.ShapeDtypeStruct(b_re.shape, F32)] * 2
    return pl.pallas_call(body, name=name, out_shape=shp)(lam_re, lam_im, log_dt, b_re, b_im)


def _s5_prep_bwd(name, lam_re, lam_im, log_dt, b_re, b_im, cots):
    def body(a, b, c, d, e, c1, c2, c3, c4, o1, o2, o3, o4, o5):
        _, vjp = jax.vjp(_discretise, a[...], b[...], c[...], d[...], e[...])
        grads = vjp((c1[...], c2[...], c3[...], c4[...]))
        for o, v in zip((o1, o2, o3, o4, o5), grads):
            o[...] = v

    shp = [jax.ShapeDtypeStruct(a.shape, F32) for a in (lam_re, lam_im, log_dt, b_re, b_im)]
    return pl.pallas_call(body, name=name, out_shape=shp)(lam_re, lam_im, log_dt, b_re, b_im, *cots)


def _interleave_rows(ref, n_seq, dtype):
    n_j = ref.shape[0] // (SUBLANES * n_seq)
    return jnp.concatenate([ref[pl.ds(q * SUBLANES * n_j + j, SUBLANES, stride=n_j), :] for q in range(n_seq) for j in range(n_j)],
                           axis=0).astype(dtype)


def _store_tokens(out_ref, ref, n_seq):
    n_j = ref.shape[0] // (SUBLANES * n_seq)
    for q in range(n_seq):
        for s in range(SUBLANES):
            start = (q * SUBLANES + s) * n_j
            out_ref[start:start + n_j, :] = ref[pl.ds(q * SUBLANES * n_j + s, n_j, stride=SUBLANES), :].astype(out_ref.dtype)


def _expand_powers(t_ref, pow_ref):
    for j in range(pow_ref.shape[0] // SUBLANES):
        row = 5 * SUBLANES + j
        pow_ref[j * SUBLANES:(j + 1) * SUBLANES, :] = jnp.broadcast_to(t_ref[row:row + 1, :], (SUBLANES, pow_ref.shape[1]))


def _scan_tile(h_ref, t_ref, pow_ref, carry_ref, up, n_seq, states_ref=None):
    sw = h_ref.shape[1] // 2
    n_j = h_ref.shape[0] // (SUBLANES * n_seq)
    seqs = range(n_seq)

    def rows(g):
        if isinstance(g, int):
            return pl.ds(g * SUBLANES, SUBLANES)
        return pl.ds(pl.multiple_of(g * SUBLANES, SUBLANES), SUBLANES)

    def at(q, j):
        return rows(q * n_j + j)

    def tab(g):
        return t_ref[rows(g), :sw], t_ref[rows(g), sw:]

    def order(i):
        return n_j - 1 - i if up else i

    def cmul_add(xr, xi, ar, ai, yr, yi):
        return xr + ar * yr - ai * yi, xi + ar * yi + ai * yr

    a_re, a_im = tab(0)

    def local_step(i, xs):
        j = order(i)
        out = []
        for q in seqs:
            xr, xi = cmul_add(h_ref[at(q, j), :sw], h_ref[at(q, j), sw:], a_re, a_im, *xs[q])
            h_ref[at(q, j), :sw] = xr
            h_ref[at(q, j), sw:] = xi
            out.append((xr, xi))
        return tuple(out)

    zero = jnp.zeros((SUBLANES, sw), F32)
    ends = lax.fori_loop(0, n_j, local_step, tuple((zero, zero) for _ in seqs))
    out_row = 0 if up else SUBLANES - 1
    in_row = SUBLANES - 1 if up else 0
    one = SUBLANES - 1 if up else 1
    is_in = lax.broadcasted_iota(jnp.int32, (SUBLANES, sw), 0) == in_row
    carried, enters = [], []
    for q in seqs:
        dr, di = ends[q]
        for level, sh in enumerate((1, 2, 4)):
            amount = SUBLANES - sh if up else sh
            dr, di = cmul_add(dr, di, *tab(1 + level), pltpu.roll(dr, amount, 0), pltpu.roll(di, amount, 0))
        cr, ci = carry_ref[rows(q), :sw], carry_ref[rows(q), sw:]
        dr, di = cmul_add(dr, di, *tab(4), cr, ci)
        carry_ref[rows(q), :sw] = jnp.broadcast_to(dr[out_row:out_row + 1], dr.shape)
        carry_ref[rows(q), sw:] = jnp.broadcast_to(di[out_row:out_row + 1], di.shape)
        carried.append((cr, ci))
        enters.append((jnp.where(is_in, cr, pltpu.roll(dr, one, 0)), jnp.where(is_in, ci, pltpu.roll(di, one, 0))))

    def fix_step(i, state):
        j = order(i)
        nows = []
        for q in seqs:
            xr, xi = cmul_add(h_ref[at(q, j), :sw], h_ref[at(q, j), sw:], pow_ref[rows(j), :sw], pow_ref[rows(j), sw:], *enters[q])
            h_ref[at(q, j), :sw] = xr
            h_ref[at(q, j), sw:] = xi
            nows.append((xr, xi))
        if states_ref is None:
            return state
        befores, (acc_r, acc_i) = state
        for q in seqs:
            lr, li = befores[q]
            hr, hi = states_ref[at(q, j), :sw], states_ref[at(q, j), sw:]
            acc_r, acc_i = acc_r + lr * hr + li * hi, acc_i + li * hr - lr * hi
        return tuple(nows), (acc_r, acc_i)

    if states_ref is None:
        lax.fori_loop(0, n_j, fix_step, 0)
        return None
    lasts, (acc_r, acc_i) = lax.fori_loop(0, n_j, fix_step, (tuple((zero, zero) for _ in seqs), (zero, zero)))
    for q in seqs:
        lr = jnp.where(is_in, carried[q][0], pltpu.roll(lasts[q][0], one, 0))
        li = jnp.where(is_in, carried[q][1], pltpu.roll(lasts[q][1], one, 0))
        hr, hi = states_ref[at(q, order(0)), :sw], states_ref[at(q, order(0)), sw:]
        acc_r, acc_i = acc_r + lr * hr + li * hi, acc_i + li * hr - lr * hi
    return acc_r, acc_i


def _phys_tile(t, cfg):
    tpl, nb = cfg["tpl"], cfg["nb"]
    return jnp.where(t < nb * tpl, (t % tpl) * nb + t // tpl, t)


def _s5_block_index(cfg, dirn, adjoint):
    tpl = cfg["tpl"]

    def idx(k):
        if not adjoint:
            return jnp.where(k == 0, tpl, k - 1 if dirn == 0 else tpl - k)
        return jnp.where(k == tpl, tpl, tpl - 1 - k if dirn == 0 else k)
    return idx


RELAY_AT = 0.75


def _grid_ends(grid):
    step = 0
    for i, n in enumerate(grid):
        step = step * n + pl.program_id(i)
    total = math.prod(grid)
    return step == 0, step == int(RELAY_AT * total), step == total - 1


def _s5_fwd(name, u, bmat, cmat, tab, dirn, cfg, cps=None):
    tr, tpl, nb = cfg["tr"], cfg["tpl"], cfg["nb"]
    n_rows, d = u.shape
    ns, _, sw2 = bmat.shape
    block = _s5_block_index(cfg, dirn, False)
    up = dirn == 1
    grid = (ns, tpl + 1)
    br = nb * tr

    def body(*refs):
        copies, (u_ref, b_ref, c_ref, t_ref, h_ref, y_ref, carry_ref, mix_ref, pow_ref) = _carried(cps, 4, 2, 3, refs)
        first, middle, last = _grid_ends(grid)
        _start_all(copies, first)
        _relay_all(copies, middle)

        @pl.when(pl.program_id(1) == 0)
        def _():
            carry_ref[...] = jnp.zeros_like(carry_ref)
            _expand_powers(t_ref, pow_ref)

        h_ref[...] = jnp.dot(u_ref[...], b_ref[...], preferred_element_type=F32)
        _scan_tile(h_ref, t_ref, pow_ref, carry_ref, up, nb)
        mix_ref[...] = jnp.dot(h_ref[...].astype(BF16), c_ref[...], preferred_element_type=F32)
        _store_tokens(y_ref, mix_ref, nb)
        _wait_all(copies, last)

    extra = cps if cps is not None else _Copies([], [])
    return pl.pallas_call(
        body, name=name, grid=grid,
        in_specs=[pl.BlockSpec((br, LANES), lambda s, k: (block(k), s)),
                  pl.BlockSpec((None, LANES, sw2), lambda s, k: (s, 0, 0)),
                  pl.BlockSpec((None, sw2, LANES), lambda s, k: (s, 0, 0)),
                  pl.BlockSpec((None, tab.shape[1], sw2), lambda s, k: (s, 0, 0))] + extra.in_specs,
        out_specs=[pl.BlockSpec((br, sw2), lambda s, k: (block(k), s)),
                   pl.BlockSpec((br, LANES), lambda s, k: (block(k), s))] + extra.out_specs,
        out_shape=[jax.ShapeDtypeStruct((n_rows, ns * sw2), F32), jax.ShapeDtypeStruct((n_rows, d), F32)] + extra.out_shape,
        scratch_shapes=[pltpu.VMEM((nb * SUBLANES, sw2), F32), pltpu.VMEM((br, LANES), F32), pltpu.VMEM((tr, sw2), F32)] + extra.scratch,
        compiler_params=_params(("arbitrary", "arbitrary")))(u, bmat, cmat, tab, *extra.arrays)


def _s5_bwd(name, dy, h, u, cmat_t, bmat_t, tab, dirn, cfg, cps=None):
    tr, tpl, nb = cfg["tr"], cfg["tpl"], cfg["nb"]
    n_rows, d = u.shape
    ns, _, sw2 = cmat_t.shape
    sw = sw2 // 2
    block = _s5_block_index(cfg, dirn, True)
    up = dirn == 0
    grid = (ns, tpl + 1)
    br = nb * tr

    def body(*refs):
        copies, own = _carried(cps, 6, 4, 4, refs)
        dy_ref, h_ref, u_ref, ct_ref, bt_ref, t_ref, du_ref, db_ref, dc_ref, da_ref, lam_ref, carry_ref, mix_ref, pow_ref = own
        grid_first, grid_middle, grid_last = _grid_ends(grid)
        _start_all(copies, grid_first)
        _relay_all(copies, grid_middle)
        first = pl.program_id(1) == 0

        @pl.when(first)
        def _():
            carry_ref[...] = jnp.zeros_like(carry_ref)
            _expand_powers(t_ref, pow_ref)

        dy = dy_ref[...]
        u_mixed = u_ref[...]
        lam_ref[...] = jnp.dot(dy, ct_ref[...], preferred_element_type=F32)
        acc = _scan_tile(lam_ref, t_ref, pow_ref, carry_ref, up, nb, h_ref)
        lam = lam_ref[...].astype(BF16)
        d_b = lax.dot_general(u_mixed, lam, TN, preferred_element_type=F32)
        d_c = lax.dot_general(dy, h_ref[...].astype(BF16), TN, preferred_element_type=F32)
        mix_ref[...] = jnp.dot(lam, bt_ref[...], preferred_element_type=F32)
        _store_tokens(du_ref, mix_ref, nb)

        @pl.when(first)
        def _():
            db_ref[...] = d_b
            dc_ref[...] = d_c
            da_ref[:, :sw] = acc[0]
            da_ref[:, sw:] = acc[1]

        @pl.when(jnp.logical_not(first))
        def _():
            db_ref[...] += d_b
            dc_ref[...] += d_c
            da_ref[:, :sw] += acc[0]
            da_ref[:, sw:] += acc[1]

        _wait_all(copies, grid_last)

    extra = cps if cps is not None else _Copies([], [])
    return pl.pallas_call(
        body, name=name, grid=grid,
        in_specs=[pl.BlockSpec((br, LANES), lambda s, k: (block(k), s)),
                  pl.BlockSpec((br, sw2), lambda s, k: (block(k), s)),
                  pl.BlockSpec((br, LANES), lambda s, k: (block(k), s)),
                  pl.BlockSpec((None, LANES, sw2), lambda s, k: (s, 0, 0)),
                  pl.BlockSpec((None, sw2, LANES), lambda s, k: (s, 0, 0)),
                  pl.BlockSpec((None, tab.shape[1], sw2), lambda s, k: (s, 0, 0))] + extra.in_specs,
        out_specs=[pl.BlockSpec((br, LANES), lambda s, k: (block(k), s)),
                   pl.BlockSpec((None, LANES, sw2), lambda s, k: (s, 0, 0)),
                   pl.BlockSpec((None, LANES, sw2), lambda s, k: (s, 0, 0)),
                   pl.BlockSpec((None, SUBLANES, sw2), lambda s, k: (s, 0, 0))] + extra.out_specs,
        out_shape=[jax.ShapeDtypeStruct((n_rows, d), F32), jax.ShapeDtypeStruct((ns, LANES, sw2), F32),
                   jax.ShapeDtypeStruct((ns, LANES, sw2), F32), jax.ShapeDtypeStruct((ns, SUBLANES, sw2), F32)] + extra.out_shape,
        scratch_shapes=[pltpu.VMEM((br, sw2), F32), pltpu.VMEM((nb * SUBLANES, sw2), F32), pltpu.VMEM((br, LANES), F32),
                        pltpu.VMEM((tr, sw2), F32)] + extra.scratch,
        compiler_params=_params(("arbitrary", "arbitrary")))(dy, h, u, cmat_t, bmat_t, tab, *extra.arrays)


def _s5_tables(ab_re, ab_im, up, conj, ns, n_j):
    def powers_of(base, count):
        out = [base]
        for _ in range(count - 1):
            q_re, q_im = out[-1]
            out.append((q_re * base[0] - q_im * base[1], q_re * base[1] + q_im * base[0]))
        return out

    def spread(q):
        return jnp.broadcast_to(q[:, None, :], (q.shape[0], SUBLANES, q.shape[1]))

    steps = powers_of((ab_re.reshape(ns, -1), (-ab_im if conj else ab_im).reshape(ns, -1)), n_j)
    jumps = powers_of(steps[-1], SUBLANES)
    rows = jnp.arange(SUBLANES)
    blocks = [tuple(spread(q) for q in steps[0])]
    for sh in (1, 2, 4):
        keep = ((rows <= SUBLANES - 1 - sh) if up else (rows >= sh))[None, :, None]
        blocks.append(tuple(jnp.where(keep, q[:, None, :], 0.0) for q in jumps[sh - 1]))
    dist = range(SUBLANES, 0, -1) if up else range(1, SUBLANES + 1)
    blocks.append(tuple(jnp.stack([jumps[dd - 1][part] for dd in dist], axis=1) for part in (0, 1)))
    ordered = steps[::-1] if up else steps
    blocks.append(tuple(jnp.stack([q[part] for q in ordered], axis=1) for part in (0, 1)))
    return jnp.concatenate([jnp.concatenate([b[0] for b in blocks], axis=1), jnp.concatenate([b[1] for b in blocks], axis=1)], axis=2)


def _block_diag(blocks):
    ns, gs, a, b = blocks.shape
    eye = jnp.eye(gs, dtype=blocks.dtype)
    return (blocks[:, :, :, None, :] * eye[None, :, None, :, None]).reshape(ns, gs * a, gs * b)


def _diag_blocks(mat, gs):
    ns, ra, rb = mat.shape
    a, b = ra // gs, rb // gs
    m5 = mat.reshape(ns, gs, a, gs, b)
    eye = jnp.eye(gs, dtype=mat.dtype)
    return jnp.sum(m5 * eye[None, :, None, :, None], axis=3)


def _conv_flags(t, cfg):
    tpl, nb = cfg["tpl"], cfg["nb"]
    latent = t < nb * tpl
    first = jnp.logical_or(jnp.logical_not(latent), t % tpl == 0)
    last = jnp.logical_or(jnp.logical_not(latent), t % tpl == tpl - 1)
    return first, last


def _fill_ext(ext_ref, prev_ref, cur_ref, next_ref, t, cfg, halo):
    first, last = _conv_flags(t, cfg)
    tr = cur_ref.shape[0]
    for p in range(ext_ref.shape[0]):
        lanes = slice(p * LANES, (p + 1) * LANES)
        ext_ref[p, 0:halo, :] = jnp.where(first, 0.0, prev_ref[:, lanes])
        ext_ref[p, halo:halo + tr, :] = cur_ref[:, lanes]
        ext_ref[p, halo + tr:, :] = jnp.where(last, 0.0, next_ref[:, lanes])


CONV_LANES = 4 * LANES


def _conv_specs(tr, n_rows, halo, cw, cfg):
    per = tr // halo
    n_halo = n_rows // halo
    nb = cfg["nb"]
    return [pl.BlockSpec((halo, cw), lambda c, t: (jnp.maximum((_phys_tile(t, cfg) - nb + 1) * per - 1, 0), c)),
            pl.BlockSpec((tr, cw), lambda c, t: (_phys_tile(t, cfg), c)),
            pl.BlockSpec((halo, cw), lambda c, t: (jnp.minimum((_phys_tile(t, cfg) + nb) * per, n_halo - 1), c))]


def _dwconv(name, a, w, cfg):
    tr = cfg["tr"]
    n_rows, d = a.shape
    kw = w.shape[0]
    half = kw // 2
    halo = 2 * SUBLANES
    cw = min(d, CONV_LANES)

    def body(prev_ref, cur_ref, next_ref, w_ref, o_ref, ext_ref):
        _fill_ext(ext_ref, prev_ref, cur_ref, next_ref, pl.program_id(1), cfg, halo)
        for p in range(cw // LANES):
            lanes = slice(p * LANES, (p + 1) * LANES)
            acc = jnp.zeros((tr, LANES), F32)
            for k in range(kw):
                acc = acc + ext_ref[p, pl.ds(halo - half + k, tr), :] * w_ref[k:k + 1, lanes]
            o_ref[:, lanes] = acc

    return pl.pallas_call(body, name=name, grid=(d // cw, n_rows // tr),
                          in_specs=_conv_specs(tr, n_rows, halo, cw, cfg) + [pl.BlockSpec((kw, cw), lambda c, t: (0, c))],
                          out_specs=pl.BlockSpec((tr, cw), lambda c, t: (_phys_tile(t, cfg), c)),
                          out_shape=jax.ShapeDtypeStruct((n_rows, d), F32),
                          scratch_shapes=[pltpu.VMEM((cw // LANES, tr + 2 * halo, LANES), F32)],
                          compiler_params=_params(("arbitrary", "arbitrary")))(a, a, a, w)


def _dwconv_wgrad(name, a, dout, kw, cfg):
    tr = cfg["tr"]
    n_rows, d = a.shape
    half = kw // 2
    halo = 2 * SUBLANES
    cw = min(d, CONV_LANES)

    def body(prev_ref, cur_ref, next_ref, do_ref, o_ref, ext_ref):
        t = pl.program_id(1)
        _fill_ext(ext_ref, prev_ref, cur_ref, next_ref, t, cfg, halo)
        for p in range(cw // LANES):
            lanes = slice(p * LANES, (p + 1) * LANES)
            dout_t = do_ref[:, lanes]
            rows = [jnp.sum(ext_ref[p, pl.ds(halo - half + k, tr), :] * dout_t, axis=0, keepdims=True) for k in range(kw)]
            _accumulate(o_ref.at[:, lanes], jnp.concatenate(rows, axis=0), t == 0)

    return pl.pallas_call(body, name=name, grid=(d // cw, n_rows // tr),
                          in_specs=_conv_specs(tr, n_rows, halo, cw, cfg) + [pl.BlockSpec((tr, cw), lambda c, t: (_phys_tile(t, cfg), c))],
                          out_specs=pl.BlockSpec((kw, cw), lambda c, t: (0, c)),
                          out_shape=jax.ShapeDtypeStruct((kw, d), F32),
                          scratch_shapes=[pltpu.VMEM((cw // LANES, tr + 2 * halo, LANES), F32)],
                          compiler_params=_params(("arbitrary", "arbitrary")))(a, a, a, dout)


def _sincos_1d(pos, dim):
    quarter = dim // 2
    omega = POS_TEMP ** (-jnp.arange(quarter, dtype=F32) / quarter)
    ang = pos[:, None] * omega[None, :]
    return jnp.concatenate([jnp.sin(ang), jnp.cos(ang)], axis=-1)


def _grid_pos_embed(rows, dim):
    row_idx = jnp.repeat(jnp.arange(rows), GRID_W).astype(F32)
    col_idx = jnp.tile(jnp.arange(GRID_W), rows).astype(F32)
    return jnp.concatenate([_sincos_1d(row_idx, dim // 2), _sincos_1d(col_idx, dim // 2)], axis=-1)


def _pack(arrs, row_multiple=SUBLANES):
    flat = jnp.concatenate([a.reshape(-1).astype(F32) for a in arrs])
    pad = (-flat.shape[0]) % (row_multiple * LANES)
    return jnp.pad(flat, (0, pad)).reshape(-1, LANES)


def _unpack(buf, shapes):
    flat = buf.reshape(-1)
    out, pos = [], 0
    for shp in shapes:
        n = math.prod(shp)
        out.append(flat[pos:pos + n].reshape(shp))
        pos += n
    return out


def _unpack_gathered(buf, shapes):
    flat = buf.reshape(N_DEV, -1)
    out, pos = [], 0
    for shp in shapes:
        n = math.prod(shp)
        part = flat[:, pos:pos + n].reshape((N_DEV,) + tuple(shp))
        out.append(jnp.moveaxis(part, 0, -2).reshape(tuple(shp[:-1]) + (N_DEV * shp[-1],)))
        pos += n
    return out


WEIGHTS = ("c_ctx", "w_ada", "b_ada", "ln_gain", "ln_bias", "s5_lam_re", "s5_lam_im", "s5_log_dt", "s5_b_re", "s5_b_im",
           "s5_c_re", "s5_c_im", "s5_d", "s5_w_glu", "s5_b_glu", "cv_w_pw1", "cv_b_pw1", "cv_w_dw", "cv_b_dw", "cv_ln_g",
           "cv_ln_b", "cv_w_pw2", "cv_b_pw2", "mlp_w1", "mlp_w2")
SHARDED_SMALL = ("ln_gain", "ln_bias", "cv_b_pw1", "cv_w_dw", "cv_b_dw", "cv_ln_g", "cv_ln_b", "cv_b_pw2")
REPLICATED_SMALL = ("s5_lam_re", "s5_lam_im", "s5_log_dt", "s5_b_re", "s5_b_im", "s5_c_re", "s5_c_im", "s5_d", "s5_b_glu")
NATIVE_SMALL = ("s5_lam_re", "s5_lam_im", "s5_b_re", "s5_b_im", "s5_c_re", "s5_c_im")
BIG = ("mlp_w1", "mlp_w2", "s5_w_glu", "cv_w_pw1", "cv_w_pw2")


def _step(a):
    x, c, ctx = a["x"], a["c"], a["ctx"]
    nb, seq, d = x.shape
    lc = ctx.shape[1]
    nl = a["w_ada"].shape[0]
    tr = lc
    tpl = seq // tr
    cfg = {"tr": tr, "tpl": tpl, "nb": nb}
    n_rows = nb * (seq + lc)
    alpha = (2.0 * nl) ** 0.25
    me = 4 * lax.axis_index("x") + 2 * lax.axis_index("y") + lax.axis_index("c")
    n_grp, n_state = a["s5_lam_re"].shape[2:]
    ch = a["s5_b_re"].shape[-1]
    gs = LANES // ch
    ns = d // LANES
    tm = 2 * tr if n_rows % (2 * tr) == 0 else tr
    tm_big = n_rows // 3 if n_rows % (3 * 2 * SUBLANES) == 0 else tm
    tm_mid = n_rows // 6 if n_rows % (6 * 2 * SUBLANES) == 0 else tm
    f_sub1_s5, f_sub1_cv, f_sub2 = _make_sub1_s5(alpha), _make_sub1_cv(alpha), _make_sub2(alpha)

    def layer_weights(i):
        mixer = [("s5_w_glu", i // 2, 1)] if i % 2 == 0 else [("cv_w_pw1", i // 2, 1), ("cv_w_pw2", i // 2, 0)]
        return mixer + [("mlp_w1", i, 1), ("mlp_w2", i, 0)]

    weights, wgrads, received = {}, {}, {}
    small_all, c_all = _exchange("gather_small", [_pack([a[n] for n in SHARDED_SMALL]), c], ["slot", "slot"])
    full = dict(zip(SHARDED_SMALL, _unpack_gathered(small_all, [a[n].shape for n in SHARDED_SMALL])))
    c_all = c_all.reshape(N_DEV * nb, d)
    cond_rows = N_DEV * nb + SUBLANES
    cc = jnp.concatenate([c_all, a["c_ctx"][None], jnp.zeros((SUBLANES - 1, d), F32)], axis=0)

    n_ada = a["w_ada"].shape[2]
    b_loc = lax.dynamic_slice(a["b_ada"], (0, me * n_ada), (nl, n_ada))[:, None, :]
    mod_cols = _ada_fwd(cc, a["w_ada"], b_loc)
    mod_all = _exchange("gather_mod", [mod_cols.reshape(nl * cond_rows, n_ada)], ["slot"])[0].reshape(N_DEV, nl, cond_rows, n_ada)
    mod_mine = jnp.concatenate([lax.dynamic_slice(mod_all, (0, 0, nb * me, 0), (N_DEV, nl, nb, n_ada)),
                                mod_all[:, :, N_DEV * nb:N_DEV * nb + 1]], axis=2)
    mod = jnp.transpose(mod_mine, (1, 2, 0, 3)).reshape(nl, nb + 1, 6, 1, d)

    def seg(i, q):
        return mod[i, :, q]

    zero_seg = jnp.zeros((nb + 1, 1, d), F32)

    def vec(v):
        return v.reshape(1, -1)

    pos = _grid_pos_embed(seq // GRID_W, d)
    def latent_rows(v):
        return jnp.transpose(v.reshape(nb, tpl, tr, d), (1, 0, 2, 3)).reshape(nb * seq, d)

    xc = jnp.concatenate([latent_rows(x), ctx.reshape(nb * lc, d)], axis=0)
    pos_rows = jnp.concatenate([pos, jnp.zeros((tr, d), F32)], axis=0)

    def pos_tile(t):
        return jnp.where(t < nb * tpl, t % tpl, tpl)

    x_cur, h_cur = _rowwise("entry", _f_entry, [xc, pos_rows], [seg(0, 0), seg(0, 1)], [], [(d, F32), (d, BF16)], [], [], cfg, (1,),
                              {1: pos_tile})
    saved = []
    for i in range(nl):
        j = i // 2
        sv = {"x": x_cur, "h": h_cur}
        sh1, sc1, g1, sh2, sc2, g2 = (seg(i, q) for q in range(6))
        gain0, bias0, gain1, bias1 = (vec(full["ln_gain"][i, 0]), vec(full["ln_bias"][i, 0]),
                                      vec(full["ln_gain"][i, 1]), vec(full["ln_bias"][i, 1]))
        if i % 2 == 0:
            lam_re, lam_im = a["s5_lam_re"][j], a["s5_lam_im"][j]
            log_dt = a["s5_log_dt"][j][:, :, None]
            b_re_t = jnp.transpose(a["s5_b_re"][j], (0, 3, 1, 2))
            b_im_t = jnp.transpose(a["s5_b_im"][j], (0, 3, 1, 2))
            sv["prep_in"] = (lam_re, lam_im, log_dt, b_re_t, b_im_t)
            ab_re, ab_im, bb_re, bb_im = _s5_prep(f"s5_prep{i}", *sv["prep_in"])
            sv["ab"] = (ab_re, ab_im)
            ys = []
            for dirn in range(2):
                def blocks(t):
                    return jnp.transpose(t, (1, 0, 2)).reshape(ns, gs, ch, n_state)
                bmat = jnp.concatenate([_block_diag(blocks(bb_re[dirn])), _block_diag(blocks(bb_im[dirn]))], axis=2).astype(BF16)
                c_re_t = jnp.transpose(a["s5_c_re"][j, dirn], (0, 2, 1)).reshape(ns, gs, n_state, ch)
                c_im_t = jnp.transpose(a["s5_c_im"][j, dirn], (0, 2, 1)).reshape(ns, gs, n_state, ch)
                cmat = jnp.concatenate([_block_diag(c_re_t), -_block_diag(c_im_t)], axis=1).astype(BF16)
                tab = _s5_tables(ab_re[dirn], ab_im[dirn], dirn == 1, False, ns, tr // SUBLANES)
                group = layer_weights(i + dirn)
                cps = _Copies([a[n][idx].astype(BF16) for n, idx, _ in group], [axis for _, _, axis in group])
                h_states, y_dir, *gathered = _s5_fwd(f"s5_fwd{i}_{dirn}", h_cur, bmat, cmat, tab, dirn, cfg, cps)
                weights.update({(n, idx): w[None] for (n, idx, _), w in zip(group, gathered)})
                sv[f"mats{dirn}"] = (jnp.transpose(bmat, (0, 2, 1)), jnp.transpose(cmat, (0, 2, 1)))
                sv[f"states{dirn}"] = h_states
                ys.append(y_dir)
            sv["y"] = ys
            dsk = vec(a["s5_d"][j])
            z = _rowwise(f"gelu{i}", _f_gelu, [x_cur, ys[0], ys[1]], [sh1, sc1], [dsk], [(d, BF16)], [], [], cfg)[0]
            zz = _mm_nn(f"glu{i}", z, weights["s5_w_glu", j], 0, tm_big, min(2 * d, 1024))[0]
            bglu = vec(a["s5_b_glu"][j])
            x1, h2 = _rowwise(f"sub1_{i}", f_sub1_s5, [x_cur, zz], [g1, sh2, sc2], [bglu, gain0, bias0],
                              [(d, F32), (d, BF16)], [], [], cfg)
            sv.update(z=z, zz=zz)
        else:
            zz = _mm_nn(f"pw1_{i}", h_cur, weights["cv_w_pw1", j], 0, tm_big, min(2 * d, 1024))[0]
            bpw1 = vec(full["cv_b_pw1"][j])
            act = _rowwise(f"cvglu{i}", _f_cvglu, [zz], [], [bpw1], [(d, F32)], [], [], cfg)[0]
            w_dw = full["cv_w_dw"][j]
            cv = _dwconv(f"dwconv{i}", act, w_dw, cfg)
            bdw, lng, lnb = vec(full["cv_b_dw"][j]), vec(full["cv_ln_g"][j]), vec(full["cv_ln_b"][j])
            s_act = _rowwise(f"cvln{i}", _f_cvln, [cv], [], [bdw, lng, lnb], [(d, BF16)], [], [], cfg)[0]
            mm = _mm_nn(f"pw2_{i}", s_act, weights["cv_w_pw2", j], 0, tm_big, d)[0]
            bpw2 = vec(full["cv_b_pw2"][j])
            x1, h2 = _rowwise(f"sub1_{i}", f_sub1_cv, [x_cur, mm], [g1, sh2, sc2], [bpw2, gain0, bias0],
                              [(d, F32), (d, BF16)], [], [], cfg)
            sv.update(zz=zz, act=act, cv=cv, s_act=s_act, mm=mm, w_dw=w_dw)
        dff = weights["mlp_w1", i].shape[2]
        p_act, r_act = _mm_nn(f"mlp1_{i}", h2, weights["mlp_w1", i], 0, tm_big, min(dff, 1024), (BF16, BF16),
                              lambda acc: (jnp.square(jnp.maximum(acc, 0.0)), jnp.maximum(acc, 0.0)))
        m_out = _mm_nn(f"mlp2_{i}", p_act, weights["mlp_w2", i], 0, tm_mid, d)[0]
        shn, scn = (seg(i + 1, 0), seg(i + 1, 1)) if i + 1 < nl else (zero_seg, zero_seg)
        x2, hn = _rowwise(f"sub2_{i}", f_sub2, [x1, m_out], [g2, shn, scn], [gain1, bias1], [(d, F32), (d, BF16)], [], [], cfg,
                          (1,) if (i + 1) % 2 == 0 and i + 1 < nl else ())
        sv.update(x1=x1, h2=h2, p=p_act, r=r_act, m=m_out, shn=shn, scn=scn)
        saved.append(sv)
        x_cur, h_cur = x2, hn

    target = jnp.concatenate([latent_rows(a["loss_target"]), jnp.zeros((nb * lc, d), F32)], axis=0)
    mask = jnp.concatenate([jnp.ones((nb, 1, d), F32), jnp.zeros((1, 1, d), F32)], axis=0)

    def f_loss(xf, tgt, msk):
        err = (xf - tgt) * msk
        part = 0.5 * jnp.sum(jnp.square(err), axis=(0, 1), keepdims=True) / d
        return err / d, jnp.broadcast_to(part, (1, LANES))

    dx_final, loss_part = _rowwise("loss", f_loss, [x_cur, target], [mask], [], [(d, F32)], [], [LANES], cfg)
    loss = lax.psum(loss_part[0, 0], ("x", "y", "c"))

    grads = {n: [None] * a[n].shape[0] for n in WEIGHTS if n not in ("c_ctx", "w_ada", "b_ada")}
    dmod = [[None] * 6 for _ in range(nl)]

    def add_mod(i, q, val):
        dmod[i][q] = val if dmod[i][q] is None else dmod[i][q] + val

    dx_parts, dh_parts = [dx_final], []
    for i in reversed(range(nl)):
        j = i // 2
        sv = saved[i]
        sh1, sc1, g1, sh2, sc2, g2 = (seg(i, q) for q in range(6))
        gain0, bias0, gain1, bias1 = (vec(full["ln_gain"][i, 0]), vec(full["ln_bias"][i, 0]),
                                      vec(full["ln_gain"][i, 1]), vec(full["ln_bias"][i, 1]))
        bwd = _vjp_fn(f_sub2, 2, (len(dx_parts), len(dh_parts)), (0, 1, 2, 3, 4, 5, 6))
        dx1, dm, dg2, dshn, dscn, dgain1, dbias1 = _rowwise(
            f"sub2_bwd{i}", bwd, [sv["x1"], sv["m"]] + dx_parts + dh_parts, [g2, sv["shn"], sv["scn"]], [gain1, bias1],
            [(d, F32), (d, BF16)], [d, d, d], [d, d], cfg)
        add_mod(i, 5, dg2)
        if i + 1 < nl:
            add_mod(i + 1, 0, dshn)
            add_mod(i + 1, 1, dscn)
        da = _mm_nt(f"mlp2_dgrad{i}", dm, weights["mlp_w2", i], 0, tm_big, min(dff, 1024), [sv["r"]], BF16,
                    lambda acc, r: (acc * 2.0 * r,))
        wgrads["mlp_w2", i] = _mm_wgrad_rows(f"mlp2_wgrad{i}", sv["p"], dm, tm_big)
        wgrads["mlp_w1", i] = _mm_wgrad_cols(f"mlp1_wgrad{i}", sv["h2"], da, tm_big)
        dh2 = _mm_nt(f"mlp1_dgrad{i}", da, weights["mlp_w1", i], 0, tm_mid, d)
        if i % 2 == 0:
            bglu = vec(a["s5_b_glu"][j])
            bwd = _vjp_fn(f_sub1_s5, 2, (1, 1), (0, 1, 2, 3, 4, 5, 6, 7))
            dxa, dzz, dg1, dsh2, dsc2, dbglu, dgain0, dbias0 = _rowwise(
                f"sub1_bwd{i}", bwd, [sv["x"], sv["zz"], dx1, dh2], [g1, sh2, sc2], [bglu, gain0, bias0],
                [(d, F32), (2 * d, BF16)], [d, d, d], [2 * d, d, d], cfg)
            grads["s5_b_glu"][j] = dbglu[0]
            wgrads["s5_w_glu", j] = _mm_wgrad_cols(f"glu_wgrad{i}", sv["z"], dzz, tm_big)
            dz = _mm_nt(f"glu_dgrad{i}", dzz, weights["s5_w_glu", j], 0, tm_mid, d)
            dsk = vec(a["s5_d"][j])
            bwd = _vjp_fn(_f_gelu, 3, (1,), (0, 1, 3, 4, 5))
            dxb, dy, dsh1, dsc1, ddsk = _rowwise(f"gelu_bwd{i}", bwd, [sv["x"], sv["y"][0], sv["y"][1], dz], [sh1, sc1], [dsk],
                                                 [(d, F32), (d, BF16)], [d, d], [d], cfg, (1,))
            grads["s5_d"][j] = ddsk[0]
            add_mod(i, 0, dsh1)
            add_mod(i, 1, dsc1)
            ab_re, ab_im = sv["ab"]
            dus, d_ab_re, d_ab_im, d_bb_re, d_bb_im, d_c_re, d_c_im = [], [], [], [], [], [], []
            for dirn in range(2):
                bmat_t, cmat_t = sv[f"mats{dirn}"]
                tab = _s5_tables(ab_re[dirn], ab_im[dirn], dirn == 0, True, ns, tr // SUBLANES)
                group = layer_weights(i + 1 - dirn)
                cps = _Copies([wgrads[n, idx] for n, idx, _ in group], ["scatter"] * len(group))
                du, d_b, d_c, d_a, *parts = _s5_bwd(f"s5_bwd{i}_{dirn}", dy, sv[f"states{dirn}"], sv["h"], cmat_t, bmat_t, tab,
                                                    dirn, cfg, cps)
                received.update({(n, idx): p for (n, idx, _), p in zip(group, parts)})
                dus.append(du)
                sw = d_a.shape[2] // 2
                d_a = jnp.sum(d_a, axis=1)
                d_ab_re.append(d_a[:, :sw].reshape(n_grp, n_state))
                d_ab_im.append(d_a[:, sw:].reshape(n_grp, n_state))

                def unblock_b(t):
                    return jnp.transpose(_diag_blocks(t, gs).reshape(n_grp, ch, n_state), (1, 0, 2))

                def unblock_c(t):
                    return jnp.transpose(_diag_blocks(t, gs).reshape(n_grp, n_state, ch), (0, 2, 1))
                d_bb_re.append(unblock_b(d_b[:, :, :sw]))
                d_bb_im.append(unblock_b(d_b[:, :, sw:]))
                d_c = jnp.transpose(d_c, (0, 2, 1))
                d_c_re.append(unblock_c(d_c[:, :sw]))
                d_c_im.append(-unblock_c(d_c[:, sw:]))
            g_lre, g_lim, g_ldt, g_bre, g_bim = _s5_prep_bwd(
                f"s5_prep_bwd{i}", *sv["prep_in"], (jnp.stack(d_ab_re), jnp.stack(d_ab_im), jnp.stack(d_bb_re), jnp.stack(d_bb_im)))
            grads["s5_lam_re"][j], grads["s5_lam_im"][j], grads["s5_log_dt"][j] = g_lre, g_lim, g_ldt[:, :, 0]
            grads["s5_b_re"][j] = jnp.transpose(g_bre, (0, 2, 3, 1))
            grads["s5_b_im"][j] = jnp.transpose(g_bim, (0, 2, 3, 1))
            grads["s5_c_re"][j], grads["s5_c_im"][j] = jnp.stack(d_c_re), jnp.stack(d_c_im)
            dx_parts, dh_parts = [dxa, dxb], dus
        else:
            bpw2 = vec(full["cv_b_pw2"][j])
            bwd = _vjp_fn(f_sub1_cv, 2, (1, 1), (0, 1, 2, 3, 4, 5, 6, 7))
            dxa, dmm, dg1, dsh2, dsc2, dbpw2, dgain0, dbias0 = _rowwise(
                f"sub1_bwd{i}", bwd, [sv["x"], sv["mm"], dx1, dh2], [g1, sh2, sc2], [bpw2, gain0, bias0],
                [(d, F32), (d, BF16)], [d, d, d], [d, d, d], cfg)
            grads["cv_b_pw2"][j] = dbpw2[0]
            wgrads["cv_w_pw2", j] = _mm_wgrad_rows(f"pw2_wgrad{i}", sv["s_act"], dmm, tm_big)
            ds = _mm_nt(f"pw2_dgrad{i}", dmm, weights["cv_w_pw2", j], 0, tm_big, d)
            bdw, lng, lnb = vec(full["cv_b_dw"][j]), vec(full["cv_ln_g"][j]), vec(full["cv_ln_b"][j])
            bwd = _vjp_fn(_f_cvln, 1, (1,), (0, 1, 2, 3))
            dcv, dbdw, dlng, dlnb = _rowwise(f"cvln_bwd{i}", bwd, [sv["cv"], ds], [], [bdw, lng, lnb], [(d, F32)], [], [d, d, d], cfg)
            grads["cv_b_dw"][j], grads["cv_ln_g"][j], grads["cv_ln_b"][j] = dbdw[0], dlng[0], dlnb[0]
            dact = _dwconv(f"dwconv_bwd{i}", dcv, sv["w_dw"][::-1], cfg)
            grads["cv_w_dw"][j] = _dwconv_wgrad(f"dwconv_wgrad{i}", sv["act"], dcv, sv["w_dw"].shape[0], cfg)
            bpw1 = vec(full["cv_b_pw1"][j])
            bwd = _vjp_fn(_f_cvglu, 1, (1,), (0, 1))
            dzz, dbpw1 = _rowwise(f"cvglu_bwd{i}", bwd, [sv["zz"], dact], [], [bpw1], [(2 * d, BF16)], [], [2 * d], cfg)
            grads["cv_b_pw1"][j] = dbpw1[0]
            wgrads["cv_w_pw1", j] = _mm_wgrad_cols(f"pw1_wgrad{i}", sv["h"], dzz, tm_big)
            dh = _mm_nt(f"pw1_dgrad{i}", dzz, weights["cv_w_pw1", j], 0, tm_mid, d)
            dx_parts, dh_parts = [dxa], [dh]
        grads["ln_gain"][i] = jnp.stack([dgain0[0], dgain1[0]])
        grads["ln_bias"][i] = jnp.stack([dbias0[0], dbias1[0]])
        add_mod(i, 2, dg1)
        add_mod(i, 3, dsh2)
        add_mod(i, 4, dsc2)
    bwd = _vjp_fn(_f_entry, 2, (len(dx_parts), len(dh_parts)), (0, 2, 3))
    dxc, dsh1, dsc1 = _rowwise("entry_bwd", bwd, [xc, pos_rows] + dx_parts + dh_parts, [seg(0, 0), seg(0, 1)], [],
                               [(d, F32)], [d, d], [], cfg, (), {1: pos_tile})
    add_mod(0, 0, dsh1)
    add_mod(0, 1, dsc1)
    grad_x = jnp.transpose(dxc[:nb * seq].reshape(tpl, nb, tr, d), (1, 0, 2, 3)).reshape(nb, seq, d)

    dmod_loc = jnp.stack([jnp.concatenate([q[:, 0] for q in dmod[i]], axis=1) for i in range(nl)])
    dmod_all = _exchange("gather_dmod", [dmod_loc.reshape(nl * (nb + 1), 6 * d)], ["slot"])[0].reshape(N_DEV, nl, nb + 1, 6 * d)
    mine = lax.dynamic_slice(dmod_all, (0, 0, 0, me * n_ada), (N_DEV, nl, nb + 1, n_ada))
    dmod_rows = jnp.transpose(mine[:, :, :nb], (1, 0, 2, 3)).reshape(nl, N_DEV * nb, n_ada)
    dmod_rows = jnp.concatenate([dmod_rows, jnp.zeros((nl, SUBLANES, n_ada), F32)], axis=1)
    g_w_ada, _, dcond = _ada_bwd(cc, a["w_ada"], dmod_rows, mine[:, :, nb:])
    g_b_ada = _sum_lead("b_ada_sum", jnp.transpose(dmod_all, (0, 2, 1, 3)).reshape(N_DEV * (nb + 1), nl, 6 * d), nl)

    small_names = SHARDED_SMALL + REPLICATED_SMALL
    small_full = [jnp.stack(grads[n]) for n in small_names]
    small_packed = _pack(small_full, N_DEV * SUBLANES)
    small_parts, dcond_all = _exchange("scatter_small_grads", [small_packed.reshape(N_DEV, -1, LANES), dcond[N_DEV * nb:N_DEV * nb + 1]],
                                       ["scatter", "slot"])
    g_c_ctx = _cctx_grad(dcond_all, a["c_ctx"][None])[0]
    small_part = _sum_lead("small_grad_sum", small_parts, _row_tile(small_parts.shape[1], 512))
    small_sum = _exchange("gather_small_sum", [small_part], ["slot"])[0]
    small_g = dict(zip(small_names, _unpack(small_sum, [g.shape for g in small_full])))
    for n in SHARDED_SMALL:
        width = a[n].shape[-1]
        start = (0,) * (small_g[n].ndim - 1) + (me * width,)
        small_g[n] = lax.dynamic_slice(small_g[n], start, a[n].shape)
    small_g["c_ctx"], small_g["b_ada"] = g_c_ctx, g_b_ada

    out = {}

    def update(n, parts):
        shp = a[n].shape
        cols = parts.shape[-1]
        rows = parts.shape[1]
        res = _adamw(f"adamw_{n}", parts, a[n].reshape(rows, cols), a["m_" + n].reshape(rows, cols), a["v_" + n].reshape(rows, cols),
                     _row_tile(rows, max(SUBLANES, 131072 // cols)))
        out[n] = [r.reshape(shp) for r in res]

    for n in BIG:
        rows, cols = a[n].shape[1:]
        bufs = [lax.empty(a[n].shape, F32) for _ in range(4)]
        for idx in range(a[n].shape[0]):
            bufs = _adamw_layer(f"adamw_{n}{idx}", received[n, idx], a[n], a["m_" + n], a["v_" + n], bufs, idx,
                                _row_tile(rows, max(SUBLANES, 131072 // cols)))
        out[n] = bufs
    update("w_ada", g_w_ada.reshape(1, -1, n_ada))
    for n in NATIVE_SMALL:
        out[n] = [small_g[n], *_adamw_native(f"adamw_{n}", small_g[n], a[n], a["m_" + n], a["v_" + n])]
    small_all_names = ("c_ctx", "b_ada") + tuple(n for n in small_names if n not in NATIVE_SMALL)
    packed = [_pack([src[n] for n in small_all_names]) for src in
              (small_g, a, {n: a["m_" + n] for n in small_all_names}, {n: a["v_" + n] for n in small_all_names})]
    res = _adamw("adamw_small", packed[0][None], packed[1], packed[2], packed[3], _row_tile(packed[0].shape[0], 512))
    shapes = [a[n].shape for n in small_all_names]
    for n, vals in zip(small_all_names, zip(*[_unpack(r, shapes) for r in res])):
        out[n] = list(vals)
    return (loss, grad_x, *[out[n][0] for n in WEIGHTS], *[out[n][1] for n in WEIGHTS],
            *[out[n][2] for n in WEIGHTS], *[out[n][3] for n in WEIGHTS])


def kernel(x, c, ctx, c_ctx, w_ada, b_ada, ln_gain, ln_bias, s5_lam_re, s5_lam_im, s5_log_dt, s5_b_re, s5_b_im, s5_c_re, s5_c_im, s5_d, s5_w_glu, s5_b_glu, cv_w_pw1, cv_b_pw1, cv_w_dw, cv_b_dw, cv_ln_g, cv_ln_b, cv_w_pw2, cv_b_pw2, mlp_w1, mlp_w2, loss_target, m_c_ctx, m_w_ada, m_b_ada, m_ln_gain, m_ln_bias, m_s5_lam_re, m_s5_lam_im, m_s5_log_dt, m_s5_b_re, m_s5_b_im, m_s5_c_re, m_s5_c_im, m_s5_d, m_s5_w_glu, m_s5_b_glu, m_cv_w_pw1, m_cv_b_pw1, m_cv_w_dw, m_cv_b_dw, m_cv_ln_g, m_cv_ln_b, m_cv_w_pw2, m_cv_b_pw2, m_mlp_w1, m_mlp_w2, v_c_ctx, v_w_ada, v_b_ada, v_ln_gain, v_ln_bias, v_s5_lam_re, v_s5_lam_im, v_s5_log_dt, v_s5_b_re, v_s5_b_im, v_s5_c_re, v_s5_c_im, v_s5_d, v_s5_w_glu, v_s5_b_glu, v_cv_w_pw1, v_cv_b_pw1, v_cv_w_dw, v_cv_b_dw, v_cv_ln_g, v_cv_ln_b, v_cv_w_pw2, v_cv_b_pw2, v_mlp_w1, v_mlp_w2):
    return _step(dict(locals()))
```
